```python
import math
import jax, jax.numpy as jnp
from jax import lax
import numpy as np

D_MODEL = 2048
BATCH = 4
SEQ = 2048
DEPTH = 1
DEC_BATCH = 32
DEC_SEQ = 8
PAST_LEN = 8192
PAGE_SIZE = 128

HEAD_DIM = 64
RW_WIDTH = D_MODEL // 2
RW_HEADS = RW_WIDTH // HEAD_DIM
AT_WIDTH = D_MODEL - RW_WIDTH
AT_HEADS = AT_WIDTH // HEAD_DIM
MIX_WIDTH = RW_WIDTH + AT_WIDTH
DECAY_LORA = 64
ICLR_LORA = 64
GATE_LORA = 128
GN_EPS = 64e-5
IDX_HEADS = 16
IDX_DIM = 64
TOPK_MAX = 256
Q_BLOCK = 128
N_BUCKETS = 32
MAX_DISTANCE = 128
MEM_TOKENS = 256
X_HEADS = 4
X_HEAD_DIM = 128
X_WIDTH = X_HEADS * X_HEAD_DIM
N_GROUPS = 4
EXPERTS_PER_GROUP = 8
N_EXPERTS = N_GROUPS * EXPERTS_PER_GROUP
TOP_K_EXPERT = 2
D_EXPERT = 512
MOE_BLOCK = 64
NORM_EPS = 1e-6

RW_SPLITS = [RW_WIDTH, 2 * RW_WIDTH, 3 * RW_WIDTH, 3 * RW_WIDTH + DECAY_LORA, 3 * RW_WIDTH + DECAY_LORA + ICLR_LORA]
RW_PROJ = 3 * RW_WIDTH + DECAY_LORA + ICLR_LORA + GATE_LORA
AT_SPLITS = [AT_WIDTH, 2 * AT_WIDTH, 3 * AT_WIDTH, 3 * AT_WIDTH + IDX_HEADS * IDX_DIM, 3 * AT_WIDTH + IDX_HEADS * IDX_DIM + IDX_DIM]
AT_PROJ = 3 * AT_WIDTH + IDX_HEADS * IDX_DIM + IDX_DIM + IDX_HEADS
PROJ_WIDTH = RW_PROJ + AT_PROJ

kernel_name = 'hymba_rwkv7_dsa_hiermoe_step'


def rmsnorm(x, g):
    xf = x.astype(jnp.float32)
    y = xf * lax.rsqrt(jnp.mean(xf * xf, -1, keepdims=True) + NORM_EPS)
    return (y * g.astype(jnp.float32)).astype(x.dtype)


def token_shift(f, prev_row, mu):
    f_prev = jnp.concatenate([prev_row[:, None].astype(f.dtype), f[:, :-1]], axis=1)
    return f + (f_prev - f) * mu


def rwkv_group(fs, wkv0, w0, w2, a0, a2, g2, k_k, k_a, r_k, ln_g, ln_b):
    B, T, _ = fs.shape
    f = fs.astype(jnp.float32)
    r, k, v, wd, ad, gd = jnp.split(f, RW_SPLITS, axis=-1)
    w = -jax.nn.softplus(-(w0 + jnp.tanh(wd) @ w2)) - 0.5
    decay = jnp.exp(-jnp.exp(w))
    a = jax.nn.sigmoid(a0 + ad @ a2)
    g = jax.nn.sigmoid(gd) @ g2
    heads = lambda z: z.reshape(B, T, RW_HEADS, HEAD_DIM)
    kk = heads(k * k_k)
    kk = kk / jnp.maximum(jnp.sqrt(jnp.sum(kk * kk, -1, keepdims=True)), 1e-12)
    k = heads(k * (1.0 + (a - 1.0) * k_a))
    r, v, a, decay = heads(r), heads(v), heads(a), heads(decay)

    def step(S, inp):
        r_t, k_t, v_t, kk_t, a_t, w_t = inp
        sa = jnp.einsum('bhij,bhj->bhi', S, -kk_t)
        S = S * w_t[:, :, None, :] + sa[..., None] * (kk_t * a_t)[:, :, None, :] + v_t[..., None] * k_t[:, :, None, :]
        return S, jnp.einsum('bhij,bhj->bhi', S, r_t)

    seq = tuple(jnp.moveaxis(z, 1, 0) for z in (r, k, v, kk, a, decay))
    S_T, y = lax.scan(step, wkv0.astype(jnp.float32), seq)
    y = jnp.moveaxis(y, 0, 1)
    mu = jnp.mean(y, -1, keepdims=True)
    var = jnp.mean(jnp.square(y - mu), -1, keepdims=True)
    y = ((y - mu) * lax.rsqrt(var + GN_EPS)).reshape(B, T, RW_WIDTH) * ln_g + ln_b
    bonus = jnp.sum(r * k * r_k, -1, keepdims=True) * v
    out = (y + bonus.reshape(B, T, RW_WIDTH)) * g
    return out.astype(fs.dtype), S_T


def split_attn(fat):
    B, T, _ = fat.shape
    q, k, v, iq, ik, iw = jnp.split(fat, AT_SPLITS, axis=-1)
    hd = lambda z: z.reshape(B, T, AT_HEADS, HEAD_DIM)
    return hd(q), hd(k), hd(v), iq.reshape(B, T, IDX_HEADS, IDX_DIM), ik, iw


def t5_bucket(dist):
    exact = N_BUCKETS // 2
    d = jnp.maximum(dist, 0)
    far = exact + (jnp.log(jnp.maximum(d, 1).astype(jnp.float32) / exact) / math.log(MAX_DISTANCE / exact) * (N_BUCKETS - exact)).astype(jnp.int32)
    return jnp.where(d < exact, d, jnp.minimum(far, N_BUCKETS - 1))


def indexer_scores(iq, iw, ik):
    dots = jnp.einsum('bqhd,bsd->bqhs', iq, ik).astype(jnp.float32)
    wts = iw.astype(jnp.float32) * (IDX_HEADS ** -0.5 * IDX_DIM ** -0.5)
    return jnp.einsum('bqhs,bqh->bqs', jax.nn.relu(dots), wts)


def select_keys(scores, q_pos, n_keys, topk):
    key_pos = jnp.arange(n_keys)
    scores = jnp.where(key_pos[None, None, :] <= q_pos[None, :, None], scores, -jnp.inf)
    return lax.top_k(scores, topk)[1]


def sparse_attend(q, k_sel, v_sel, q_pos, sel, rel_bias):
    dist = q_pos[None, :, None] - sel
    valid = dist >= 0
    bias = jnp.moveaxis(rel_bias[t5_bucket(dist)], -1, 2).astype(jnp.float32)
    logits = jnp.einsum('bqhd,bqkhd->bqhk', q, k_sel).astype(jnp.float32) * HEAD_DIM ** -0.5 + bias
    logits = jnp.where(valid[:, :, None, :], logits, -jnp.inf)
    p = jax.nn.softmax(logits, axis=-1)
    return jnp.einsum('bqhk,bqkhd->bqhd', p.astype(v_sel.dtype), v_sel)


def dsa_prompt(q, k, v, iq, ik, iw, rel_bias, topk):
    B, S = q.shape[:2]
    nb = S // Q_BLOCK
    blockify = lambda z: jnp.moveaxis(z.reshape(B, nb, Q_BLOCK, *z.shape[2:]), 1, 0)
    bidx = jnp.arange(B)[:, None, None]

    def one_block(args):
        i, qb, iqb, iwb = args
        q_pos = i * Q_BLOCK + jnp.arange(Q_BLOCK)
        sel = select_keys(indexer_scores(iqb, iwb, ik), q_pos, S, topk)
        return sparse_attend(qb, k[bidx, sel], v[bidx, sel], q_pos, sel, rel_bias)

    out = lax.map(one_block, (jnp.arange(nb), blockify(q), blockify(iq), blockify(iw)))
    return jnp.moveaxis(out, 0, 1).reshape(B, S, AT_WIDTH)


def dsa_sample(l, q, k_new, v_new, iq, ik_new, iw, cache_k, cache_v, cache_idx_k, page_table, rel_bias, topk):
    DB, T = q.shape[:2]
    past = page_table.shape[1] * PAGE_SIZE
    ik_past = cache_idx_k[l, page_table].reshape(DB, past, IDX_DIM)
    ik_all = jnp.concatenate([ik_past, ik_new.astype(ik_past.dtype)], axis=1)
    q_pos = past + jnp.arange(T)
    sel = select_keys(indexer_scores(iq, iw, ik_all), q_pos, past + T, topk)
    b3 = jnp.arange(DB)[:, None, None]
    ps = jnp.minimum(sel, past - 1)
    phys = page_table[b3, ps // PAGE_SIZE]
    off = ps % PAGE_SIZE
    ns = jnp.clip(sel - past, 0, T - 1)
    in_past = (sel < past)[..., None, None]
    k_sel = jnp.where(in_past, cache_k[l, phys, off], k_new[b3, ns].astype(cache_k.dtype))
    v_sel = jnp.where(in_past, cache_v[l, phys, off], v_new[b3, ns].astype(cache_v.dtype))
    return sparse_attend(q, k_sel, v_sel, q_pos, sel, rel_bias).reshape(DB, T, AT_WIDTH)


def memory_kv(mem, g_mem, w_ck, w_cv):
    B, M, _ = mem.shape
    hm = rmsnorm(mem, g_mem)
    return (hm @ w_ck).reshape(B, M, X_HEADS, X_HEAD_DIM), (hm @ w_cv).reshape(B, M, X_HEADS, X_HEAD_DIM)


def cross_attend(h, mk, mv, w_cq, w_co):
    B, T, _ = h.shape
    q = (h @ w_cq).reshape(B, T, X_HEADS, X_HEAD_DIM)
    logits = jnp.einsum('bthd,bmhd->bhtm', q, mk.astype(q.dtype)).astype(jnp.float32) * X_HEAD_DIM ** -0.5
    p = jax.nn.softmax(logits, axis=-1)
    o = jnp.einsum('bhtm,bmhd->bthd', p.astype(q.dtype), mv.astype(q.dtype)).reshape(B, T, X_WIDTH)
    return o @ w_co


def hier_route(h, w_rg, b_rg, w_re, b_re):
    N = h.shape[0]
    g_logits = (h @ w_rg).astype(jnp.float32) + b_rg.astype(jnp.float32)
    grp = jnp.argmax(g_logits, axis=-1)
    p_grp = jnp.take_along_axis(jax.nn.softmax(g_logits, axis=-1), grp[:, None], axis=1)
    e_logits = ((h @ w_re).astype(jnp.float32) + b_re.astype(jnp.float32)).reshape(N, N_GROUPS, EXPERTS_PER_GROUP)
    e_logits = jnp.take_along_axis(e_logits, grp[:, None, None], axis=1)[:, 0]
    vals, loc = lax.top_k(e_logits, TOP_K_EXPERT)
    gates = jax.nn.softmax(vals, axis=-1) * p_grp
    return gates, grp[:, None].astype(jnp.int32) * EXPERTS_PER_GROUP + loc.astype(jnp.int32)


def moe_ffn(h, w_rg, b_rg, w_re, b_re, w1, w3, w2):
    lead = h.shape[:-1]
    hf = h.reshape(-1, D_MODEL)
    N = hf.shape[0]
    gates, eids = hier_route(hf, w_rg, b_rg, w_re, b_re)
    M = N * TOP_K_EXPERT
    flat_e = eids.reshape(-1)
    flat_tok = jnp.repeat(jnp.arange(N, dtype=jnp.int32), TOP_K_EXPERT)
    flat_g = gates.reshape(-1)
    order = jnp.argsort(flat_e)
    se = flat_e[order]
    counts = jnp.bincount(flat_e, length=N_EXPERTS)
    padded = (counts + MOE_BLOCK - 1) // MOE_BLOCK * MOE_BLOCK
    pad_end = jnp.cumsum(padded)
    pad_start = pad_end - padded
    start = jnp.cumsum(counts) - counts
    dest = pad_start[se] + jnp.arange(M) - start[se]
    n_blocks = -(-M // MOE_BLOCK) + N_EXPERTS
    rows = n_blocks * MOE_BLOCK
    buf_tok = jnp.full((rows,), N, jnp.int32).at[dest].set(flat_tok[order])
    buf_g = jnp.zeros((rows,), jnp.float32).at[dest].set(flat_g[order])
    blk_e = jnp.minimum(jnp.searchsorted(pad_end, jnp.arange(n_blocks) * MOE_BLOCK, side='right'), N_EXPERTS - 1)
    h_pad = jnp.concatenate([hf, jnp.zeros((1, D_MODEL), hf.dtype)], axis=0)

    def run_block(args):
        tok, g, e = args
        xb = h_pad[tok]
        u = jax.nn.silu(xb @ w1[e]) * (xb @ w3[e])
        return (u @ w2[e]) * g[:, None].astype(xb.dtype)

    ys = lax.map(run_block, (buf_tok.reshape(n_blocks, MOE_BLOCK), buf_g.reshape(n_blocks, MOE_BLOCK), blk_e))
    out = jnp.zeros((N + 1, D_MODEL), ys.dtype).at[buf_tok].add(ys.reshape(rows, D_MODEL))[:N]
    return out.reshape(*lead, D_MODEL).astype(h.dtype)


def post_mixer(x, rw_out, at_out, mk, mv, w_out, g_cross, w_cq, w_co, g_ffn, w_rg, b_rg, w_re, b_re, w1, w3, w2):
    x = x + jnp.concatenate([rw_out, at_out.astype(rw_out.dtype)], axis=-1) @ w_out
    x = x + cross_attend(rmsnorm(x, g_cross), mk, mv, w_cq, w_co)
    return x + moe_ffn(rmsnorm(x, g_ffn), w_rg, b_rg, w_re, b_re, w1, w3, w2)


def setup_inputs(seed: int = 0) -> dict:
    key = jax.random.key(seed)
    ks = iter(jax.random.split(key, 64))
    f32 = jnp.float32
    nrm = lambda shape, scale=1.0: jax.random.normal(next(ks), shape, f32) * scale
    gain = lambda shape: 1.0 + jax.random.normal(next(ks), shape, f32) * 0.05
    L = DEPTH
    n_pages = PAST_LEN // PAGE_SIZE
    in_use = DEC_BATCH * n_pages
    n_pool = in_use + max(1, in_use // 4)
    inputs = {}
    inputs['x_prompt'] = nrm((BATCH, SEQ, D_MODEL))
    inputs['x_sample'] = nrm((DEC_BATCH, DEC_SEQ, D_MODEL))
    inputs['mem_prompt'] = nrm((BATCH, MEM_TOKENS, D_MODEL))
    inputs['cache_k'] = nrm((L, n_pool, PAGE_SIZE, AT_HEADS, HEAD_DIM))
    inputs['cache_v'] = nrm((L, n_pool, PAGE_SIZE, AT_HEADS, HEAD_DIM))
    inputs['cache_idx_k'] = nrm((L, n_pool, PAGE_SIZE, IDX_DIM))
    inputs['page_table'] = jax.random.permutation(next(ks), n_pool)[:in_use].reshape(DEC_BATCH, n_pages).astype(jnp.int32)
    inputs['state_wkv'] = nrm((L, DEC_BATCH, RW_HEADS, HEAD_DIM, HEAD_DIM))
    inputs['state_shift'] = nrm((L, DEC_BATCH, RW_PROJ))
    inputs['cache_mem_k'] = nrm((L, DEC_BATCH, MEM_TOKENS, X_HEADS, X_HEAD_DIM))
    inputs['cache_mem_v'] = nrm((L, DEC_BATCH, MEM_TOKENS, X_HEADS, X_HEAD_DIM))
    inputs['g_mix'] = gain((L, D_MODEL))
    inputs['w_in'] = nrm((L, D_MODEL, PROJ_WIDTH), D_MODEL ** -0.5)
    inputs['mu_shift'] = jax.random.uniform(next(ks), (L, RW_PROJ), f32, 0.0, 1.0)
    inputs['rw_w0'] = jax.random.uniform(next(ks), (L, RW_WIDTH), f32, -6.0, -1.0)
    inputs['rw_w2'] = nrm((L, DECAY_LORA, RW_WIDTH), 0.1 * DECAY_LORA ** -0.5)
    inputs['rw_a0'] = nrm((L, RW_WIDTH), 0.1)
    inputs['rw_a2'] = nrm((L, ICLR_LORA, RW_WIDTH), 0.5 * ICLR_LORA ** -0.5)
    inputs['rw_g2'] = nrm((L, GATE_LORA, RW_WIDTH), GATE_LORA ** -0.5)
    inputs['rw_kk'] = 0.85 + nrm((L, RW_WIDTH), 0.05)
    inputs['rw_ka'] = gain((L, RW_WIDTH))
    inputs['rw_rk'] = nrm((L, RW_HEADS, HEAD_DIM), 0.1)
    inputs['rw_ln_g'] = gain((L, RW_WIDTH))
    inputs['rw_ln_b'] = nrm((L, RW_WIDTH), 0.02)
    inputs['w_out'] = nrm((L, MIX_WIDTH, D_MODEL), MIX_WIDTH ** -0.5)
    inputs['g_cross'] = gain((L, D_MODEL))
    inputs['g_mem'] = gain((L, D_MODEL))
    inputs['w_cq'] = nrm((L, D_MODEL, X_WIDTH), D_MODEL ** -0.5)
    inputs['w_ck'] = nrm((L, D_MODEL, X_WIDTH), D_MODEL ** -0.5)
    inputs['w_cv'] = nrm((L, D_MODEL, X_WIDTH), D_MODEL ** -0.5)
    inputs['w_co'] = nrm((L, X_WIDTH, D_MODEL), X_WIDTH ** -0.5)
    inputs['g_ffn'] = gain((L, D_MODEL))
    inputs['w_rg'] = nrm((L, D_MODEL, N_GROUPS), D_MODEL ** -0.5)
    inputs['b_rg'] = nrm((L, N_GROUPS), 0.01)
    inputs['w_re'] = nrm((L, D_MODEL, N_EXPERTS), D_MODEL ** -0.5)
    inputs['b_re'] = nrm((L, N_EXPERTS), 0.01)
    inputs['w_e1'] = nrm((L, N_EXPERTS, D_MODEL, D_EXPERT), D_MODEL ** -0.5)
    inputs['w_e3'] = nrm((L, N_EXPERTS, D_MODEL, D_EXPERT), D_MODEL ** -0.5)
    inputs['w_e2'] = nrm((L, N_EXPERTS, D_EXPERT, D_MODEL), D_EXPERT ** -0.5)
    inputs['rel_bias'] = nrm((N_BUCKETS, AT_HEADS), 0.1)
    inputs['g_final'] = gain((D_MODEL,))
    return inputs


def reference(x_prompt, x_sample, mem_prompt, cache_k, cache_v, cache_idx_k, page_table, state_wkv, state_shift,
              cache_mem_k, cache_mem_v, g_mix, w_in, mu_shift, rw_w0, rw_w2, rw_a0, rw_a2, rw_g2, rw_kk, rw_ka,
              rw_rk, rw_ln_g, rw_ln_b, w_out, g_cross, g_mem, w_cq, w_ck, w_cv, w_co, g_ffn, w_rg, b_rg, w_re,
              b_re, w_e1, w_e3, w_e2, rel_bias, g_final):
    B, S, _ = x_prompt.shape
    DB, T, _ = x_sample.shape
    past = page_table.shape[1] * PAGE_SIZE
    topk_p = min(TOPK_MAX, S // 4)
    topk_s = min(TOPK_MAX, (past + T) // 4)
    xp, xs = x_prompt, x_sample
    kp, vp, ikp, wkvp, shp, mkp, mvp = [], [], [], [], [], [], []
    ks_, vs_, iks, wkvs, shs = [], [], [], [], []
    for l in range(DEPTH):
        rw_w = (rw_w0[l], rw_w2[l], rw_a0[l], rw_a2[l], rw_g2[l], rw_kk[l], rw_ka[l], rw_rk[l], rw_ln_g[l], rw_ln_b[l])
        tail_w = (w_out[l], g_cross[l], w_cq[l], w_co[l], g_ffn[l], w_rg[l], b_rg[l], w_re[l], b_re[l], w_e1[l], w_e3[l], w_e2[l])
        fp = rmsnorm(xp, g_mix[l]) @ w_in[l]
        frw, fat = fp[..., :RW_PROJ], fp[..., RW_PROJ:]
        rw_p, wkv_p = rwkv_group(token_shift(frw, jnp.zeros((B, RW_PROJ), frw.dtype), mu_shift[l]),
                                 jnp.zeros((B, RW_HEADS, HEAD_DIM, HEAD_DIM), jnp.float32), *rw_w)
        q, k, v, iq, ik, iw = split_attn(fat)
        at_p = dsa_prompt(q, k, v, iq, ik, iw, rel_bias, topk_p)
        mk, mv = memory_kv(mem_prompt, g_mem[l], w_ck[l], w_cv[l])
        xp = post_mixer(xp, rw_p, at_p, mk, mv, *tail_w)
        kp.append(k); vp.append(v); ikp.append(ik); wkvp.append(wkv_p.astype(state_wkv.dtype))
        shp.append(frw[:, -1]); mkp.append(mk); mvp.append(mv)
        fs = rmsnorm(xs, g_mix[l]) @ w_in[l]
        frw_s, fat_s = fs[..., :RW_PROJ], fs[..., RW_PROJ:]
        rw_s, wkv_s = rwkv_group(token_shift(frw_s, state_shift[l], mu_shift[l]), state_wkv[l], *rw_w)
        q2, k2, v2, iq2, ik2, iw2 = split_attn(fat_s)
        at_s = dsa_sample(l, q2, k2, v2, iq2, ik2, iw2, cache_k, cache_v, cache_idx_k, page_table, rel_bias, topk_s)
        xs = post_mixer(xs, rw_s, at_s, cache_mem_k[l], cache_mem_v[l], *tail_w)
        ks_.append(k2); vs_.append(v2); iks.append(ik2); wkvs.append(wkv_s.astype(state_wkv.dtype)); shs.append(frw_s[:, -1])
    y_prompt = rmsnorm(xp, g_final)
    y_sample = rmsnorm(xs, g_final)
    return (y_prompt, y_sample, jnp.stack(kp), jnp.stack(vp), jnp.stack(ikp), jnp.stack(wkvp), jnp.stack(shp),
            jnp.stack(mkp), jnp.stack(mvp), jnp.stack(ks_), jnp.stack(vs_), jnp.stack(iks), jnp.stack(wkvs), jnp.stack(shs))
```

```python
import functools
import math

import jax
import jax.numpy as jnp
from jax import lax
from jax.experimental import pallas as pl
from jax.experimental.pallas import tpu as pltpu

F32 = jnp.float32
BF16 = jnp.bfloat16
I32 = jnp.int32

LANES = 128
SUBLANES = 8
VMEM_LIMIT_BYTES = 56 * 1024 * 1024

HEAD_DIM = 64
PAIR = 2 * HEAD_DIM
GN_EPS = 64e-5
NORM_EPS = 1e-6
TOPK_MAX = 256
Q_BLOCK = 128
N_BUCKETS = 32
MAX_DISTANCE = 128
PAGE_SIZE = 128
N_GROUPS = 4
EXPERTS_PER_GROUP = 8
INT_MIN = -(2 ** 31)


def _cparams(*sem):
    return pltpu.CompilerParams(dimension_semantics=sem, vmem_limit_bytes=VMEM_LIMIT_BYTES)


def _mm(a, b):
    return jnp.dot(a.astype(BF16), b.astype(BF16), preferred_element_type=F32)


def _mm_nt(a, b):
    return lax.dot_general(a.astype(BF16), b.astype(BF16), (((1,), (1,)), ((), ())),
                           preferred_element_type=F32)


def _split2(x):
    hi = x.astype(BF16)
    lo = (x - hi.astype(F32)).astype(BF16)
    return hi, lo


def _mm3(a, b):
    ah, al = _split2(a)
    bh, bl = _split2(b)
    d = lambda x, y: jnp.dot(x, y, preferred_element_type=F32)
    return d(ah, bh) + (d(ah, bl) + d(al, bh))


def _mm_exact_rhs(a, b_bf16):
    hi = a.astype(BF16)
    r1 = a - hi.astype(F32)
    mid = r1.astype(BF16)
    lo = (r1 - mid.astype(F32)).astype(BF16)
    d = lambda x: jnp.dot(x, b_bf16, preferred_element_type=F32)
    return d(hi) + (d(mid) + d(lo))


def _rmsnorm(x, g):
    ms = jnp.mean(x * x, axis=-1, keepdims=True)
    return x * lax.rsqrt(ms + NORM_EPS) * g


def _norm_matmul_body(x_ref, g_ref, w_ref, o_ref, xn_ref):
    @pl.when(pl.program_id(1) == 0)
    def _():
        xn_ref[...] = _rmsnorm(x_ref[...], g_ref[...]).astype(BF16)

    o_ref[...] = jnp.dot(xn_ref[...], w_ref[...], preferred_element_type=F32)


def norm_matmul(x, g, w, *, tm, tn):
    n, d = x.shape
    m = w.shape[1]
    return pl.pallas_call(
        _norm_matmul_body,
        grid=(n // tm, m // tn),
        in_specs=[pl.BlockSpec((tm, d), lambda i, j: (i, 0)),
                  pl.BlockSpec((1, d), lambda i, j: (0, 0)),
                  pl.BlockSpec((d, tn), lambda i, j: (0, j))],
        out_specs=pl.BlockSpec((tm, tn), lambda i, j: (i, j)),
        out_shape=jax.ShapeDtypeStruct((n, m), F32),
        scratch_shapes=[pltpu.VMEM((tm, d), BF16)],
        compiler_params=_cparams("parallel", "arbitrary"),
        name="norm_matmul",
    )(x, g.reshape(1, d), w)


def _pair_ones():
    r = lax.broadcasted_iota(I32, (PAIR, PAIR), 0) // HEAD_DIM
    c = lax.broadcasted_iota(I32, (PAIR, PAIR), 1) // HEAD_DIM
    return (r == c).astype(BF16)


def _head_sum(x, ones_bd):
    return _mm_exact_rhs(x, ones_bd)


def _rwkv_prep_body(f_ref, prev8_ref, init_ref, mu_ref, w0_ref, w2_ref, a0_ref, a2_ref, g2_ref,
                    kk_ref, ka_ref, rk_ref,
                    r_o, k_o, v_o, kk_o, b_o, ld_o, g_o, bon_o, *, tm, width):
    i = pl.program_id(1)
    f = f_ref[0]
    prev_row = jnp.where(i == 0, init_ref[0], prev8_ref[0, SUBLANES - 1:SUBLANES, :])
    rolled = pltpu.roll(f, shift=1, axis=0)
    row = lax.broadcasted_iota(I32, f.shape, 0)
    f_prev = jnp.where(row == 0, prev_row, rolled)
    fs = f + (f_prev - f) * mu_ref[...]
    w_ = width
    r = fs[:, 0:w_]
    k = fs[:, w_:2 * w_]
    v = fs[:, 2 * w_:3 * w_]
    o = 3 * w_
    n_dec = w2_ref.shape[0]
    n_icl = a2_ref.shape[0]
    wd = fs[:, o:o + n_dec]
    ad = fs[:, o + n_dec:o + n_dec + n_icl]
    gd = fs[:, o + n_dec + n_icl:]
    z = w0_ref[...] + _mm3(jnp.tanh(wd), w2_ref[...])
    nz = -z
    softplus = jnp.maximum(nz, 0.0) + jnp.log(1.0 + jnp.exp(-jnp.abs(nz)))
    w = -softplus - 0.5
    ld = -jnp.exp(w)
    a = 1.0 / (1.0 + jnp.exp(-(a0_ref[...] + _mm3(ad, a2_ref[...]))))
    g = _mm3(1.0 / (1.0 + jnp.exp(-gd)), g2_ref[...])
    kk = k * kk_ref[...]
    k2 = k * (1.0 + (a - 1.0) * ka_ref[...])
    rk = r * k2 * rk_ref[...]
    ones_bd = _pair_ones()
    for p in range(w_ // PAIR):
        sl = slice(p * PAIR, (p + 1) * PAIR)
        kkp = kk[:, sl]
        nrm = jnp.sqrt(_head_sum(kkp * kkp, ones_bd))
        kkp = kkp / jnp.maximum(nrm, 1e-12)
        ap = a[:, sl]
        r_o[0, p] = r[:, sl]
        k_o[0, p] = k2[:, sl]
        v_o[0, p] = v[:, sl]
        kk_o[0, p] = kkp
        b_o[0, p] = kkp * ap
        ld_o[0, p] = ld[:, sl]
        g_o[0, p] = g[:, sl]
        bon_o[0, p] = _head_sum(rk[:, sl], ones_bd) * v[:, sl]


def rwkv_prep(f3, init_prev, mu, w0, w2, a0, a2, g2, k_k, k_a, r_k, *, tm, rw_proj, width):
    b, t, _ = f3.shape
    npair = width // PAIR
    row1 = lambda x: x.reshape(1, -1)
    kern = functools.partial(_rwkv_prep_body, tm=tm, width=width)
    full = lambda a: pl.BlockSpec(a.shape, lambda bi, i: (0,) * a.ndim)
    args = [row1(mu), row1(w0), w2, row1(a0), a2, g2, row1(k_k), row1(k_a), row1(r_k)]
    out_spec = pl.BlockSpec((1, npair, tm, PAIR), lambda bi, i: (bi, 0, i, 0))
    out_shape = jax.ShapeDtypeStruct((b, npair, t, PAIR), F32)
    return pl.pallas_call(
        kern,
        grid=(b, t // tm),
        in_specs=[pl.BlockSpec((1, tm, rw_proj), lambda bi, i: (bi, i, 0)),
                  pl.BlockSpec((1, SUBLANES, rw_proj),
                               lambda bi, i: (bi, jnp.maximum(i * (tm // SUBLANES) - 1, 0), 0)),
                  pl.BlockSpec((1, 1, rw_proj), lambda bi, i: (bi, 0, 0))] + [full(a) for a in args],
        out_specs=[out_spec] * 8,
        out_shape=[out_shape] * 8,
        compiler_params=_cparams("parallel", "parallel"),
        name="rwkv_prep",
    )(f3, f3, init_prev.reshape(b, 1, rw_proj), *args)


def _rwkv_chunk_body(r_ref, k_ref, v_ref, kk_ref, b_ref, ld_ref, g_ref, bon_ref, s0_ref,
                     lng_ref, lnb_ref, o_ref, st_ref, s_ref, *, L, npair):
    c = pl.program_id(1)

    @pl.when(c == 0)
    def _():
        s_ref[...] = s0_ref[0]

    L2 = 2 * L
    row = lax.broadcasted_iota(I32, (L2, L2), 0)
    col = lax.broadcasted_iota(I32, (L2, L2), 1)
    same = (row // L) == (col // L)
    tri_strict = same & (col < row)
    tri_incl = same & (col <= row)
    eye = (row == col).astype(F32)
    tr = lax.broadcasted_iota(I32, (L, L), 0)
    tc = lax.broadcasted_iota(I32, (L, L), 1)
    cum_mat = (tc <= tr).astype(BF16)
    lane = lax.broadcasted_iota(I32, (L, PAIR), 1)
    first = lane < HEAD_DIM
    ones_bd = _pair_ones()
    n_sq = max(int(math.ceil(math.log2(L))) - 1, 0)

    def block_diag(x):
        return jnp.concatenate([jnp.where(first, x, 0.0), jnp.where(first, 0.0, x)], axis=0)

    def pair_step(p, carry):
        ld = ld_ref[0, p]
        cum = _mm_exact_rhs_t(cum_mat, ld)
        dec_prev = jnp.exp(cum - ld)
        dec_inv = jnp.exp(-cum)
        dec = jnp.exp(cum)
        a_t = block_diag(-kk_ref[0, p] * dec_prev)
        b_t = block_diag(b_ref[0, p] * dec_inv)
        k_t = block_diag(k_ref[0, p] * dec_inv)
        r_t = block_diag(r_ref[0, p] * dec)
        v_b = block_diag(v_ref[0, p])
        dec_end = dec[L - 1:L, :]
        a_ab = jnp.where(tri_strict, _mm_nt(a_t, b_t), 0.0)
        a_ak = jnp.where(tri_strict, _mm_nt(a_t, k_t), 0.0)
        a_rb = jnp.where(tri_incl, _mm_nt(r_t, b_t), 0.0)
        a_rk = jnp.where(tri_incl, _mm_nt(r_t, k_t), 0.0)
        x = a_ab
        t_inv = eye + a_ab
        for _ in range(n_sq):
            x = _mm(x, x)
            t_inv = t_inv + _mm(x, t_inv)
        s = s_ref[p]
        u = _mm(t_inv, _mm_nt(a_t, s) + _mm(a_ak, v_b))
        y_b = _mm_nt(r_t, s) + _mm(a_rb, u) + _mm(a_rk, v_b)
        s_ref[p] = (s + _mm(u.T, b_t) + _mm(v_b.T, k_t)) * dec_end
        y = y_b[:L] + y_b[L:]
        mean = _head_sum(y, ones_bd) * (1.0 / HEAD_DIM)
        d = y - mean
        var = _head_sum(d * d, ones_bd) * (1.0 / HEAD_DIM)
        yn = d * lax.rsqrt(var + GN_EPS) * lng_ref[p] + lnb_ref[p]
        o_ref[0, p] = (yn + bon_ref[0, p]) * g_ref[0, p]
        return carry

    lax.fori_loop(0, npair, pair_step, 0)

    @pl.when(c == pl.num_programs(1) - 1)
    def _():
        st_ref[0] = s_ref[...]


def _mm_exact_rhs_t(m_bf16, x):
    hi = x.astype(BF16)
    r1 = x - hi.astype(F32)
    mid = r1.astype(BF16)
    lo = (r1 - mid.astype(F32)).astype(BF16)
    d = lambda y: jnp.dot(m_bf16, y, preferred_element_type=F32)
    return d(hi) + (d(mid) + d(lo))


def rwkv_chunk(feats, s0_bd, ln_g, ln_b, *, L):
    b, npair, t, _ = feats[0].shape
    blk = pl.BlockSpec((1, npair, L, PAIR), lambda bi, c: (bi, 0, c, 0))
    st_spec = pl.BlockSpec((1, npair, PAIR, PAIR), lambda bi, c: (bi, 0, 0, 0))
    par_spec = pl.BlockSpec((npair, 1, PAIR), lambda bi, c: (0, 0, 0))
    kern = functools.partial(_rwkv_chunk_body, L=L, npair=npair)
    return pl.pallas_call(
        kern,
        grid=(b, t // L),
        in_specs=[blk] * 8 + [st_spec, par_spec, par_spec],
        out_specs=[blk, st_spec],
        out_shape=[jax.ShapeDtypeStruct((b, npair, t, PAIR), F32),
                   jax.ShapeDtypeStruct((b, npair, PAIR, PAIR), F32)],
        scratch_shapes=[pltpu.VMEM((npair, PAIR, PAIR), F32)],
        compiler_params=_cparams("parallel", "arbitrary"),
        name="rwkv_chunk",
    )(*feats, s0_bd, ln_g.reshape(npair, 1, PAIR), ln_b.reshape(npair, 1, PAIR))


def _state_to_block_diag(s):
    b, h, n, _ = s.shape
    s = s.reshape(b, h // 2, 2, n, n)
    z = jnp.zeros_like(s[:, :, 0])
    top = jnp.concatenate([s[:, :, 0], z], axis=-1)
    bot = jnp.concatenate([z, s[:, :, 1]], axis=-1)
    return jnp.concatenate([top, bot], axis=-2)


def _state_from_block_diag(s_bd):
    b, p, _, _ = s_bd.shape
    n = HEAD_DIM
    return jnp.stack([s_bd[:, :, :n, :n], s_bd[:, :, n:, n:]], axis=2).reshape(b, 2 * p, n, n)


def _t5_bucket(dist):
    exact = N_BUCKETS // 2
    d = jnp.maximum(dist, 0)
    far = exact + (jnp.log(jnp.maximum(d, 1).astype(F32) / exact) / math.log(MAX_DISTANCE / exact)
                   * (N_BUCKETS - exact)).astype(I32)
    return jnp.where(d < exact, d, jnp.minimum(far, N_BUCKETS - 1))


def _bias_tables_body(rb_ref, o_ref, *, offsets, n_heads):
    qi = lax.broadcasted_iota(I32, (Q_BLOCK, Q_BLOCK), 0)
    sj = lax.broadcasted_iota(I32, (Q_BLOCK, Q_BLOCK), 1)
    for t, off in enumerate(offsets):
        bucket = _t5_bucket(qi - sj + off)
        for h in range(n_heads):
            def body(bk, acc):
                return jnp.where(bucket == bk, rb_ref[bk, h], acc)
            o_ref[t, h] = lax.fori_loop(0, N_BUCKETS, body, jnp.zeros((Q_BLOCK, Q_BLOCK), F32))


def bias_tables(rel_bias, offsets):
    n_heads = rel_bias.shape[1]
    kern = functools.partial(_bias_tables_body, offsets=tuple(offsets), n_heads=n_heads)
    return pl.pallas_call(
        kern,
        in_specs=[pl.BlockSpec(memory_space=pltpu.SMEM)],
        out_specs=pl.BlockSpec(memory_space=pltpu.VMEM),
        out_shape=jax.ShapeDtypeStruct((len(offsets), n_heads, Q_BLOCK, Q_BLOCK), F32),
        name="bias_tables",
    )(rel_bias)


def _sortable_key(scores):
    bits = lax.bitcast_convert_type(scores + 0.0, I32)
    return jnp.where(bits < 0, bits ^ 0x7FFFFFFF, bits)


def _count(mask):
    return jnp.sum(mask.astype(F32), axis=-1, keepdims=True)


def _topk_select(key, topk, n_index_bits):
    rows, n = key.shape
    kf = float(topk)
    t0 = jnp.where(_count(key >= 0) >= kf, 0, INT_MIN).astype(I32)

    def value_bit(i, t):
        cand = t + lax.shift_left(jnp.int32(1), 30 - i)
        return jnp.where(_count(key >= cand) >= kf, cand, t)

    thr = lax.fori_loop(0, 31, value_bit, t0)
    above = key > thr
    ties = key == thr
    need = kf - _count(above)
    idx = lax.broadcasted_iota(I32, (rows, n), 1)

    def index_bit(i, m):
        cand = m + lax.shift_left(jnp.int32(1), n_index_bits - 1 - i)
        return jnp.where(_count(ties & (idx < cand)) <= need, cand, m)

    m = lax.fori_loop(0, n_index_bits, index_bit, jnp.zeros((rows, 1), I32))
    return above | (ties & (idx < m))


def _dsa_prompt_body(far_ref, iq_ref, wt_ref, ikt_ref, q_ref, kt_ref, v_ref, bias_ref, o_ref,
                     *, n_heads, n_idx_heads, seq, topk):
    i = pl.program_id(1)
    qpos = i * Q_BLOCK + lax.broadcasted_iota(I32, (Q_BLOCK, seq), 0)
    kpos = lax.broadcasted_iota(I32, (Q_BLOCK, seq), 1)
    valid = kpos <= qpos
    ikt = ikt_ref[0]

    def idx_head(h, acc):
        dots = jnp.dot(iq_ref[0, h], ikt, preferred_element_type=F32)
        return acc + jnp.maximum(dots, 0.0) * wt_ref[0, h]

    scores = lax.fori_loop(0, n_idx_heads, idx_head, jnp.zeros((Q_BLOCK, seq), F32))
    key = jnp.where(valid, _sortable_key(scores), INT_MIN)
    sel = valid & _topk_select(key, topk, int(math.log2(seq)) + 1)
    kblk = kpos // Q_BLOCK
    on_diag = kblk == i
    on_sub = kblk == i - 1
    reps = seq // Q_BLOCK

    def attn_head(h, carry):
        logits = jnp.dot(q_ref[0, h], kt_ref[0, h], preferred_element_type=F32) * (HEAD_DIM ** -0.5)
        bias = jnp.where(on_diag, jnp.tile(bias_ref[0, h], (1, reps)),
                         jnp.where(on_sub, jnp.tile(bias_ref[1, h], (1, reps)), far_ref[h]))
        logits = jnp.where(sel, logits + bias, -jnp.inf)
        mx = jnp.max(logits, axis=-1, keepdims=True)
        e = jnp.exp(logits - mx)
        p = e / jnp.sum(e, axis=-1, keepdims=True)
        o_ref[0, h] = jnp.dot(p.astype(BF16), v_ref[0, h], preferred_element_type=F32).astype(o_ref.dtype)
        return carry

    lax.fori_loop(0, n_heads, attn_head, 0)


def dsa_prompt(iq, wts, ikt, q, kt, v, bias_tiles, far_bias, *, topk):
    b, h, s, dh = q.shape
    ih = iq.shape[1]
    kern = functools.partial(_dsa_prompt_body, n_heads=h, n_idx_heads=ih, seq=s, topk=topk)
    grid_spec = pltpu.PrefetchScalarGridSpec(
        num_scalar_prefetch=0,
        grid=(b, s // Q_BLOCK),
        in_specs=[pl.BlockSpec(memory_space=pltpu.SMEM),
                  pl.BlockSpec((1, ih, Q_BLOCK, dh), lambda bi, i: (bi, 0, i, 0)),
                  pl.BlockSpec((1, ih, Q_BLOCK, 1), lambda bi, i: (bi, 0, i, 0)),
                  pl.BlockSpec((1, dh, s), lambda bi, i: (bi, 0, 0)),
                  pl.BlockSpec((1, h, Q_BLOCK, dh), lambda bi, i: (bi, 0, i, 0)),
                  pl.BlockSpec((1, h, dh, s), lambda bi, i: (bi, 0, 0, 0)),
                  pl.BlockSpec((1, h, s, dh), lambda bi, i: (bi, 0, 0, 0)),
                  pl.BlockSpec(bias_tiles.shape, lambda bi, i: (0, 0, 0, 0))],
        out_specs=pl.BlockSpec((1, h, Q_BLOCK, dh), lambda bi, i: (bi, 0, i, 0)),
    )
    return pl.pallas_call(
        kern,
        grid_spec=grid_spec,
        out_shape=jax.ShapeDtypeStruct((b, h, s, dh), BF16),
        compiler_params=_cparams("parallel", "arbitrary"),
        name="dsa_prompt",
    )(far_bias, iq, wts, ikt, q, kt, v, bias_tiles)


def _dsa_sample_body(pt_ref, iq_ref, wt_ref, iknew_ref, qbd_ref, knew_ref, vnew_ref, far_ref,
                     blast_ref, bnew_ref, cidx_hbm, ck_hbm, cv_hbm, o_ref,
                     ikbuf, kbuf, vbuf, sem_ik, sem_k, sem_v,
                     *, layer, n_pages, chunk, n_heads, n_idx_heads, t_new, topk):
    b = pl.program_id(0)
    past = n_pages * PAGE_SIZE
    n_chunks = n_pages // chunk
    rows = n_heads * t_new
    ck = chunk * PAGE_SIZE

    def page(p):
        return pt_ref[b * n_pages + p]

    def ik_copy(p):
        return pltpu.make_async_copy(cidx_hbm.at[layer, page(p)], ikbuf.at[p], sem_ik)

    def kv_copies(c, j):
        slot = c % 2
        p = c * chunk + j
        dst = pl.ds(j * PAGE_SIZE, PAGE_SIZE)
        return (pltpu.make_async_copy(ck_hbm.at[layer, page(p)], kbuf.at[slot, dst], sem_k.at[slot]),
                pltpu.make_async_copy(cv_hbm.at[layer, page(p)], vbuf.at[slot, dst], sem_v.at[slot]))

    def start_chunk(c):
        for j in range(chunk):
            kc, vc = kv_copies(c, j)
            kc.start()
            vc.start()

    def wait_chunk(c):
        for j in range(chunk):
            kc, vc = kv_copies(c, j)
            kc.wait()
            vc.wait()

    def ik_start(p, carry):
        ik_copy(p).start()
        return carry

    def ik_wait(p, carry):
        ik_copy(p).wait()
        return carry

    lax.fori_loop(0, n_pages, ik_start, 0)
    start_chunk(0)
    ikbuf[n_pages] = jnp.zeros((PAGE_SIZE, ikbuf.shape[2]), F32)
    ikbuf[n_pages, 0:t_new, :] = iknew_ref[0]
    lax.fori_loop(0, n_pages, ik_wait, 0)

    n_keys = past + PAGE_SIZE
    ik_all = ikbuf[...].reshape(n_keys, ikbuf.shape[2])
    dots = _mm_nt(iq_ref[0], ik_all)
    weighted = jnp.maximum(dots, 0.0) * wt_ref[0]
    scores = jnp.sum(weighted.reshape(n_idx_heads, t_new, n_keys), axis=0)
    qpos = past + lax.broadcasted_iota(I32, (t_new, n_keys), 0)
    kpos = lax.broadcasted_iota(I32, (t_new, n_keys), 1)
    valid = kpos <= qpos
    key = jnp.where(valid, _sortable_key(scores), INT_MIN)
    sel = valid & _topk_select(key, topk, int(math.log2(n_keys)) + 1)
    sel_rows = jnp.tile(sel, (n_heads, 1))

    qbd = qbd_ref[0]
    far = far_ref[...]
    scale = HEAD_DIM ** -0.5
    neg = -1e30

    def update(state, logits, selc, v_bf16):
        m, l, acc = state
        s = jnp.where(selc, logits, -jnp.inf)
        m_new = jnp.maximum(m, jnp.max(s, axis=-1, keepdims=True))
        alpha = jnp.exp(m - m_new)
        p = jnp.exp(s - m_new)
        l = alpha * l + jnp.sum(p, axis=-1, keepdims=True)
        acc = alpha * acc + jnp.dot(p.astype(BF16), v_bf16, preferred_element_type=F32)
        return m_new, l, acc

    state = (jnp.full((rows, 1), neg, F32), jnp.zeros((rows, 1), F32),
             jnp.zeros((rows, qbd.shape[1]), F32))
    for c in range(n_chunks):
        if c + 1 < n_chunks:
            start_chunk(c + 1)
        wait_chunk(c)
        slot = c % 2
        logits = _mm_nt(qbd, kbuf[slot]) * scale
        if c == n_chunks - 1:
            logits = jnp.concatenate([logits[:, :ck - PAGE_SIZE] + far,
                                      logits[:, ck - PAGE_SIZE:] + blast_ref[...]], axis=1)
        else:
            logits = logits + far
        state = update(state, logits, sel_rows[:, c * ck:(c + 1) * ck], vbuf[slot].astype(BF16))
    logits = _mm_nt(qbd, knew_ref[0]) * scale + bnew_ref[...]
    m, l, acc = update(state, logits, sel_rows[:, past:], vnew_ref[0])
    out = acc / l
    rh = lax.broadcasted_iota(I32, out.shape, 0) // t_new
    ch = lax.broadcasted_iota(I32, out.shape, 1) // HEAD_DIM
    out = jnp.where(rh == ch, out, 0.0)
    o_ref[0] = jnp.sum(out.reshape(n_heads, t_new, out.shape[1]), axis=0)


def dsa_sample(page_table, iq_rows, wt_rows, ik_new, q_bd, k_new_pad, v_new_pad, far_rows, bias_last,
               bias_new, cache_idx_k, cache_k, cache_v, *, layer, n_heads, n_idx_heads, t_new, topk, chunk):
    db, n_pages = page_table.shape
    rows = n_heads * t_new
    width = q_bd.shape[2]
    idx_dim = cache_idx_k.shape[-1]
    kern = functools.partial(_dsa_sample_body, layer=layer, n_pages=n_pages, chunk=chunk, n_heads=n_heads,
                             n_idx_heads=n_idx_heads, t_new=t_new, topk=topk)
    per_b = lambda shape: pl.BlockSpec((1,) + shape, lambda bi, pt: (bi,) + (0,) * len(shape))
    const = lambda shape: pl.BlockSpec(shape, lambda bi, pt: (0,) * len(shape))
    any_spec = pl.BlockSpec(memory_space=pl.ANY)
    grid_spec = pltpu.PrefetchScalarGridSpec(
        num_scalar_prefetch=1,
        grid=(db,),
        in_specs=[per_b((n_idx_heads * t_new, idx_dim)), per_b((n_idx_heads * t_new, 1)), per_b((t_new, idx_dim)),
                  per_b((rows, width)), per_b((PAGE_SIZE, width)), per_b((PAGE_SIZE, width)),
                  const((rows, 1)), const((rows, PAGE_SIZE)), const((rows, PAGE_SIZE)),
                  any_spec, any_spec, any_spec],
        out_specs=per_b((t_new, width)),
        scratch_shapes=[pltpu.VMEM((n_pages + 1, PAGE_SIZE, idx_dim), F32),
                        pltpu.VMEM((2, chunk * PAGE_SIZE, width), F32),
                        pltpu.VMEM((2, chunk * PAGE_SIZE, width), F32),
                        pltpu.SemaphoreType.DMA(()),
                        pltpu.SemaphoreType.DMA((2,)),
                        pltpu.SemaphoreType.DMA((2,))],
    )
    return pl.pallas_call(
        kern,
        grid_spec=grid_spec,
        out_shape=jax.ShapeDtypeStruct((db, t_new, width), F32),
        compiler_params=_cparams("arbitrary"),
        name="dsa_sample",
    )(page_table.reshape(-1), iq_rows, wt_rows, ik_new, q_bd, k_new_pad, v_new_pad, far_rows, bias_last,
      bias_new, cache_idx_k, cache_k, cache_v)


def _dsa_sample_inputs(q, k_new, v_new, iq, ik_new, iw, rel_bias, bias_tiles):
    db, t, h, dh = q.shape
    ih = iq.shape[2]
    iq_rows = iq.transpose(0, 2, 1, 3).reshape(db, ih * t, -1).astype(BF16)
    wt_rows = (iw * (ih ** -0.5 * iq.shape[3] ** -0.5)).transpose(0, 2, 1).reshape(db, ih * t, 1)
    eye = jnp.eye(h, dtype=q.dtype)
    q_bd = (q.transpose(0, 2, 1, 3)[:, :, :, None, :] * eye[None, :, None, :, None]).reshape(db, h * t, h * dh)
    pad = lambda x: jnp.pad(x.reshape(db, t, h * dh), ((0, 0), (0, PAGE_SIZE - t), (0, 0))).astype(BF16)
    far_rows = jnp.repeat(rel_bias[N_BUCKETS - 1], t)[:, None]
    bias_new = bias_tiles[0, :, :t, :].reshape(h * t, Q_BLOCK)
    bias_last = bias_tiles[1, :, :t, :].reshape(h * t, Q_BLOCK)
    return (iq_rows, wt_rows, ik_new, q_bd.astype(BF16), pad(k_new), pad(v_new), far_rows, bias_last, bias_new)


def _matmul_residual_body(x_ref, a_ref, w_ref, o_ref):
    o_ref[...] = x_ref[...] + jnp.dot(a_ref[...], w_ref[...], preferred_element_type=F32)


def matmul_residual(x, a, w, *, tm, tn):
    n, d = x.shape
    kd = a.shape[1]
    return pl.pallas_call(
        _matmul_residual_body,
        grid=(n // tm, d // tn),
        in_specs=[pl.BlockSpec((tm, tn), lambda i, j: (i, j)),
                  pl.BlockSpec((tm, kd), lambda i, j: (i, 0)),
                  pl.BlockSpec((kd, tn), lambda i, j: (0, j))],
        out_specs=pl.BlockSpec((tm, tn), lambda i, j: (i, j)),
        out_shape=jax.ShapeDtypeStruct((n, d), F32),
        compiler_params=_cparams("parallel", "parallel"),
        name="matmul_residual",
    )(x, a, w)


def _cross_attn_body(x_ref, g_ref, wq_ref, mk_ref, mv_ref, wo_ref, o_ref, *, groups, t_rows, n_heads, head_dim):
    x = x_ref[...]
    h = _rmsnorm(x, g_ref[...]).astype(BF16)
    q = jnp.dot(h, wq_ref[...], preferred_element_type=F32).astype(BF16)
    scale = head_dim ** -0.5
    outs = []
    for gi in range(groups):
        qg = q[gi * t_rows:(gi + 1) * t_rows]
        heads = []
        for hh in range(n_heads):
            sl = slice(hh * head_dim, (hh + 1) * head_dim)
            logits = _mm_nt(qg[:, sl], mk_ref[gi, :, sl]) * scale
            mx = jnp.max(logits, axis=-1, keepdims=True)
            e = jnp.exp(logits - mx)
            p = e / jnp.sum(e, axis=-1, keepdims=True)
            heads.append(jnp.dot(p.astype(BF16), mv_ref[gi, :, sl], preferred_element_type=F32))
        outs.append(jnp.concatenate(heads, axis=1))
    o = jnp.concatenate(outs, axis=0) if groups > 1 else outs[0]
    o_ref[...] = x + jnp.dot(o.astype(BF16), wo_ref[...], preferred_element_type=F32)


def cross_attn(x, g, wq, mk, mv, wo, *, groups, t_rows, seq_tiles, n_heads):
    n, d = x.shape
    xw = wq.shape[1]
    rows = groups * t_rows
    m = mk.shape[1]
    kern = functools.partial(_cross_attn_body, groups=groups, t_rows=t_rows, n_heads=n_heads,
                             head_dim=xw // n_heads)
    return pl.pallas_call(
        kern,
        grid=(n // rows,),
        in_specs=[pl.BlockSpec((rows, d), lambda i: (i, 0)),
                  pl.BlockSpec((1, d), lambda i: (0, 0)),
                  pl.BlockSpec((d, xw), lambda i: (0, 0)),
                  pl.BlockSpec((groups, m, xw), lambda i: (i // seq_tiles, 0, 0)),
                  pl.BlockSpec((groups, m, xw), lambda i: (i // seq_tiles, 0, 0)),
                  pl.BlockSpec((xw, d), lambda i: (0, 0))],
        out_specs=pl.BlockSpec((rows, d), lambda i: (i, 0)),
        out_shape=jax.ShapeDtypeStruct((n, d), F32),
        compiler_params=_cparams("parallel"),
        name="cross_attn",
    )(x, g.reshape(1, d), wq, mk, mv, wo)


def _router_body(x_ref, g_ref, wr_ref, br_ref, h_ref, r_ref, *, n_groups, per_group):
    h = _rmsnorm(x_ref[...], g_ref[...])
    h_ref[...] = h
    logits = _mm3(h, wr_ref[...]) + br_ref[...]
    lane = lax.broadcasted_iota(I32, logits.shape, 1).astype(F32)
    big = 1e9
    first_lane = lambda hit: jnp.min(jnp.where(hit, lane, big), axis=-1, keepdims=True)
    gl = jnp.where(lane < n_groups, logits, -jnp.inf)
    gmax = jnp.max(gl, axis=-1, keepdims=True)
    grp = first_lane(gl == gmax)
    p_grp = 1.0 / jnp.sum(jnp.exp(gl - gmax), axis=-1, keepdims=True)
    e_id = lane - n_groups
    in_grp = (e_id >= grp * per_group) & (e_id < (grp + 1.0) * per_group)
    el = jnp.where(in_grp, logits, -jnp.inf)
    v1 = jnp.max(el, axis=-1, keepdims=True)
    i1 = first_lane(el == v1) - n_groups
    el2 = jnp.where(e_id == i1, -jnp.inf, el)
    v2 = jnp.max(el2, axis=-1, keepdims=True)
    i2 = first_lane(el2 == v2) - n_groups
    e2 = jnp.exp(v2 - v1)
    g1 = p_grp / (1.0 + e2)
    g2 = p_grp * e2 / (1.0 + e2)
    r_ref[...] = jnp.where(lane == 0, g1, jnp.where(lane == 1, g2, jnp.where(
        lane == 2, i1, jnp.where(lane == 3, i2, 0.0))))


def router(x, g, w_r, b_r, *, tm, n_groups, per_group):
    n, d = x.shape
    kern = functools.partial(_router_body, n_groups=n_groups, per_group=per_group)
    return pl.pallas_call(
        kern,
        grid=(n // tm,),
        in_specs=[pl.BlockSpec((tm, d), lambda i: (i, 0)),
                  pl.BlockSpec((1, d), lambda i: (0, 0)),
                  pl.BlockSpec((d, LANES), lambda i: (0, 0)),
                  pl.BlockSpec((1, LANES), lambda i: (0, 0))],
        out_specs=[pl.BlockSpec((tm, d), lambda i: (i, 0)),
                   pl.BlockSpec((tm, LANES), lambda i: (i, 0))],
        out_shape=[jax.ShapeDtypeStruct((n, d), F32), jax.ShapeDtypeStruct((n, LANES), F32)],
        compiler_params=_cparams("parallel"),
        name="moe_router",
    )(x, g.reshape(1, d), w_r, b_r)


def _slab_rows(buf, base, n_rows, n_slab):
    return jnp.concatenate([buf[pl.ds(base * n_slab + s, n_rows, stride=n_slab), :] for s in range(n_slab)],
                           axis=1)


def _moe_ffn_body(te_ref, tv_ref, tok_ref, h_hbm, w1_ref, w3_ref, w2_ref, o_ref, xbuf, sem, *, tm):
    t = pl.program_id(0)
    n_slab = h_hbm.shape[1]

    def row_copy(r):
        dst = xbuf.at[pl.ds(pl.multiple_of(r * n_slab, n_slab), n_slab)]
        return pltpu.make_async_copy(h_hbm.at[tok_ref[t * tm + r]], dst, sem)

    @pl.when(tv_ref[t] != 0)
    def _():
        def start(r, c):
            row_copy(r).start()
            return c

        def wait(r, c):
            row_copy(r).wait()
            return c

        lax.fori_loop(0, tm, start, 0)
        lax.fori_loop(0, tm, wait, 0)
        x = _slab_rows(xbuf, 0, tm, n_slab).astype(BF16)
        a = jnp.dot(x, w1_ref[0].astype(BF16), preferred_element_type=F32)
        bgate = jnp.dot(x, w3_ref[0].astype(BF16), preferred_element_type=F32)
        u = (a / (1.0 + jnp.exp(-a))) * bgate
        o_ref[...] = jnp.dot(u.astype(BF16), w2_ref[0].astype(BF16), preferred_element_type=F32)

    @pl.when(tv_ref[t] == 0)
    def _():
        o_ref[...] = jnp.zeros(o_ref.shape, F32)


def moe_ffn(tile_expert, tile_valid, row_tok, h, w1, w3, w2, *, tm):
    n_tiles = tile_expert.shape[0]
    d = h.shape[1]
    de = w1.shape[2]
    h = h.reshape(h.shape[0], d // LANES, LANES)
    grid_spec = pltpu.PrefetchScalarGridSpec(
        num_scalar_prefetch=3,
        grid=(n_tiles,),
        in_specs=[pl.BlockSpec(memory_space=pl.ANY),
                  pl.BlockSpec((1, d, de), lambda t, te, tv, tok: (te[t], 0, 0)),
                  pl.BlockSpec((1, d, de), lambda t, te, tv, tok: (te[t], 0, 0)),
                  pl.BlockSpec((1, de, d), lambda t, te, tv, tok: (te[t], 0, 0))],
        out_specs=pl.BlockSpec((tm, d), lambda t, te, tv, tok: (t, 0)),
        scratch_shapes=[pltpu.VMEM((tm * (d // LANES), LANES), F32), pltpu.SemaphoreType.DMA(())],
    )
    return pl.pallas_call(
        functools.partial(_moe_ffn_body, tm=tm),
        grid_spec=grid_spec,
        out_shape=jax.ShapeDtypeStruct((n_tiles * tm, d), F32),
        compiler_params=_cparams("arbitrary"),
        name="moe_ffn",
    )(tile_expert, tile_valid, row_tok, h, w1, w3, w2)


def _moe_dispatch(eids, n_experts, tm):
    n, k = eids.shape
    m = n * k
    flat_e = eids.reshape(-1)
    order = jnp.argsort(flat_e, stable=True)
    se = flat_e[order]
    counts = jnp.bincount(flat_e, length=n_experts)
    padded = (counts + tm - 1) // tm * tm
    pad_end = jnp.cumsum(padded)
    pad_start = pad_end - padded
    start = jnp.cumsum(counts) - counts
    dest = (pad_start[se] + jnp.arange(m) - start[se]).astype(I32)
    n_tiles = -(-m // tm) + n_experts
    row_tok = jnp.zeros((n_tiles * tm,), I32).at[dest].set((order // k).astype(I32))
    pos = jnp.zeros((m,), I32).at[order].set(dest)
    tile_start = jnp.arange(n_tiles) * tm
    tile_valid = (tile_start < pad_end[-1]).astype(I32)
    last = jnp.maximum(pad_end[-1] - 1, 0)
    tile_expert = jnp.minimum(jnp.searchsorted(pad_end, jnp.minimum(tile_start, last), side='right'),
                              n_experts - 1).astype(I32)
    return tile_expert, tile_valid, row_tok, pos


def _combine_body(pos_ref, x_ref, r_ref, g_ref, y_hbm, o_ref, ybuf, sem, *, tm, top_k):
    i = pl.program_id(0)
    n_slab = y_hbm.shape[1]

    def row_copy(j):
        tok = j // top_k
        kk = j % top_k
        dst = ybuf.at[pl.ds(pl.multiple_of((kk * tm + tok) * n_slab, n_slab), n_slab)]
        return pltpu.make_async_copy(y_hbm.at[pos_ref[i * tm * top_k + j]], dst, sem)

    def start(j, c):
        row_copy(j).start()
        return c

    def wait(j, c):
        row_copy(j).wait()
        return c

    lax.fori_loop(0, tm * top_k, start, 0)
    lax.fori_loop(0, tm * top_k, wait, 0)
    route = r_ref[...]
    x = x_ref[...]
    for kk in range(top_k):
        x = x + _slab_rows(ybuf, kk * tm, tm, n_slab) * route[:, kk:kk + 1]
    o_ref[...] = _rmsnorm(x, g_ref[...])


def moe_combine(pos, x, route, g, y, *, tm, top_k):
    n, d = x.shape
    y = y.reshape(y.shape[0], d // LANES, LANES)
    grid_spec = pltpu.PrefetchScalarGridSpec(
        num_scalar_prefetch=1,
        grid=(n // tm,),
        in_specs=[pl.BlockSpec((tm, d), lambda i, p: (i, 0)),
                  pl.BlockSpec((tm, LANES), lambda i, p: (i, 0)),
                  pl.BlockSpec((1, d), lambda i, p: (0, 0)),
                  pl.BlockSpec(memory_space=pl.ANY)],
        out_specs=pl.BlockSpec((tm, d), lambda i, p: (i, 0)),
        scratch_shapes=[pltpu.VMEM((top_k * tm * (d // LANES), LANES), F32), pltpu.SemaphoreType.DMA(())],
    )
    return pl.pallas_call(
        functools.partial(_combine_body, tm=tm, top_k=top_k),
        grid_spec=grid_spec,
        out_shape=jax.ShapeDtypeStruct((n, d), F32),
        compiler_params=_cparams("arbitrary"),
        name="moe_combine",
    )(pos, x, route, g.reshape(1, d), y)


def kernel(x_prompt, x_sample, mem_prompt, cache_k, cache_v, cache_idx_k, page_table, state_wkv, state_shift, cache_mem_k, cache_mem_v, g_mix, w_in, mu_shift, rw_w0, rw_w2, rw_a0, rw_a2, rw_g2, rw_kk, rw_ka, rw_rk, rw_ln_g, rw_ln_b, w_out, g_cross, g_mem, w_cq, w_ck, w_cv, w_co, g_ffn, w_rg, b_rg, w_re, b_re, w_e1, w_e3, w_e2, rel_bias, g_final):
    B, S, D = x_prompt.shape
    DB, T, _ = x_sample.shape
    assert w_in.shape[0] == 1, "single-layer trunk only"
    l = 0
    n_pages = page_table.shape[1]
    past = n_pages * PAGE_SIZE
    topk_p = min(TOPK_MAX, S // 4)
    topk_s = min(TOPK_MAX, (past + T) // 4)
    rw_proj = mu_shift.shape[1]
    width = rw_w0.shape[1]
    at_w = D - width
    n_heads = at_w // HEAD_DIM
    idx_dim = cache_idx_k.shape[-1]
    ih = (w_in.shape[2] - rw_proj - 3 * at_w - idx_dim) // (idx_dim + 1)
    xw = w_cq.shape[2]
    x_heads = cache_mem_k.shape[3]
    n_mem = mem_prompt.shape[1]
    n_experts = w_e1.shape[1]
    top_k = 2
    proj_pad = -(-w_in.shape[2] // 512) * 512
    o = rw_proj

    w_all = jnp.pad(w_in[l], ((0, 0), (0, proj_pad - w_in.shape[2]))).astype(BF16)
    w_out_b = w_out[l].astype(BF16)
    w_cq_b, w_co_b = w_cq[l].astype(BF16), w_co[l].astype(BF16)
    w_ckv = jnp.concatenate([w_ck[l], w_cv[l]], axis=1).astype(BF16)
    n_route = w_rg.shape[2] + w_re.shape[2]
    w_r = jnp.pad(jnp.concatenate([w_rg[l], w_re[l]], axis=1), ((0, 0), (0, LANES - n_route)))
    b_r = jnp.pad(jnp.concatenate([b_rg[l], b_re[l]]), (0, LANES - n_route)).reshape(1, LANES)
    rw_args = (mu_shift[l], rw_w0[l], rw_w2[l], rw_a0[l], rw_a2[l], rw_g2[l], rw_kk[l], rw_ka[l],
               rw_rk[l].reshape(-1))
    tiles = bias_tables(rel_bias, (0, Q_BLOCK))

    def split_attn(f3):
        q = f3[..., o:o + at_w]
        k = f3[..., o + at_w:o + 2 * at_w]
        v = f3[..., o + 2 * at_w:o + 3 * at_w]
        iq = f3[..., o + 3 * at_w:o + 3 * at_w + ih * idx_dim]
        ik = f3[..., o + 3 * at_w + ih * idx_dim:o + 3 * at_w + (ih + 1) * idx_dim]
        iw = f3[..., o + 3 * at_w + (ih + 1) * idx_dim:o + 3 * at_w + (ih + 1) * idx_dim + ih]
        return q, k, v, iq, ik, iw

    def rw_rows(y):
        b_, p_, t_, _ = y.shape
        return y.transpose(0, 2, 1, 3).reshape(b_ * t_, p_ * PAIR).astype(BF16)

    xp = x_prompt.reshape(B * S, D)
    fp = norm_matmul(xp, g_mix[l], w_all, tm=1024, tn=512).reshape(B, S, proj_pad)
    feats_p = rwkv_prep(fp, jnp.zeros((B, rw_proj), F32), *rw_args, tm=256, rw_proj=rw_proj, width=width)
    rw_p, st_p = rwkv_chunk(feats_p, jnp.zeros((B, width // PAIR, PAIR, PAIR), F32), rw_ln_g[l], rw_ln_b[l], L=64)
    q, k_p, v_p, iq, ik_p, iw = split_attn(fp)
    hm = lambda z, h_: z.reshape(B, S, h_, -1).transpose(0, 2, 1, 3).astype(BF16)
    at_p = dsa_prompt(hm(iq, ih), (iw * (ih ** -0.5 * idx_dim ** -0.5)).transpose(0, 2, 1)[..., None],
                      ik_p.transpose(0, 2, 1).astype(BF16), hm(q, n_heads),
                      k_p.reshape(B, S, n_heads, HEAD_DIM).transpose(0, 2, 3, 1).astype(BF16), hm(v_p, n_heads),
                      tiles, rel_bias[N_BUCKETS - 1], topk=topk_p)
    mix_p = jnp.concatenate([rw_rows(rw_p), at_p.transpose(0, 2, 1, 3).reshape(B * S, at_w)], axis=1)
    x1_p = matmul_residual(xp, mix_p, w_out_b, tm=512, tn=512)
    mkv = norm_matmul(mem_prompt.reshape(B * n_mem, D), g_mem[l], w_ckv, tm=256, tn=512)
    mk_p = mkv[:, :xw].reshape(B, n_mem, xw)
    mv_p = mkv[:, xw:].reshape(B, n_mem, xw)
    x2_p = cross_attn(x1_p, g_cross[l], w_cq_b, mk_p.astype(BF16), mv_p.astype(BF16), w_co_b,
                      groups=1, t_rows=512, seq_tiles=S // 512, n_heads=x_heads)

    xs = x_sample.reshape(DB * T, D)
    fs = norm_matmul(xs, g_mix[l], w_all, tm=DB * T, tn=512).reshape(DB, T, proj_pad)
    feats_s = rwkv_prep(fs, state_shift[l], *rw_args, tm=T, rw_proj=rw_proj, width=width)
    rw_s, st_s = rwkv_chunk(feats_s, _state_to_block_diag(state_wkv[l]), rw_ln_g[l], rw_ln_b[l], L=T)
    q2, k_s, v_s, iq2, ik_s, iw2 = split_attn(fs)
    r4 = lambda z, h_: z.reshape(DB, T, h_, -1)
    s_args = _dsa_sample_inputs(r4(q2, n_heads), r4(k_s, n_heads), r4(v_s, n_heads), r4(iq2, ih), ik_s, iw2,
                                rel_bias, tiles)
    n_pool = cache_k.shape[1]
    at_s = dsa_sample(page_table, *s_args, cache_idx_k,
                      cache_k.reshape(-1, n_pool, PAGE_SIZE, at_w), cache_v.reshape(-1, n_pool, PAGE_SIZE, at_w),
                      layer=l, n_heads=n_heads, n_idx_heads=ih, t_new=T, topk=topk_s, chunk=8)
    mix_s = jnp.concatenate([rw_rows(rw_s), at_s.reshape(DB * T, at_w).astype(BF16)], axis=1)
    x1_s = matmul_residual(xs, mix_s, w_out_b, tm=DB * T, tn=512)
    x2_s = cross_attn(x1_s, g_cross[l], w_cq_b, cache_mem_k[l].reshape(DB, n_mem, xw).astype(BF16),
                      cache_mem_v[l].reshape(DB, n_mem, xw).astype(BF16), w_co_b,
                      groups=8, t_rows=T, seq_tiles=1, n_heads=x_heads)

    h_p, route_p = router(x2_p, g_ffn[l], w_r, b_r, tm=512, n_groups=w_rg.shape[2],
                          per_group=w_re.shape[2] // w_rg.shape[2])
    h_s, route_s = router(x2_s, g_ffn[l], w_r, b_r, tm=DB * T, n_groups=w_rg.shape[2],
                          per_group=w_re.shape[2] // w_rg.shape[2])
    h_all = jnp.concatenate([h_p, h_s], axis=0)
    eids = jnp.concatenate([route_p[:, top_k:2 * top_k], route_s[:, top_k:2 * top_k]], axis=0).astype(I32)
    tile_expert, tile_valid, row_tok, pos = _moe_dispatch(eids, n_experts, 256)
    y_rows = moe_ffn(tile_expert, tile_valid, row_tok, h_all, w_e1[l], w_e3[l], w_e2[l], tm=256)
    n_p = B * S
    y_p = moe_combine(pos[:n_p * top_k], x2_p, route_p, g_final, y_rows, tm=256, top_k=top_k)
    y_s = moe_combine(pos[n_p * top_k:], x2_s, route_s, g_final, y_rows, tm=DB * T, top_k=top_k)

    hd = lambda z, b_, t_: z.reshape(1, b_, t_, n_heads, HEAD_DIM)
    return (y_p.reshape(B, S, D), y_s.reshape(DB, T, D),
            hd(k_p, B, S), hd(v_p, B, S), ik_p[None], _state_from_block_diag(st_p)[None], fp[:, -1, :rw_proj][None],
            mk_p.reshape(1, B, n_mem, x_heads, xw // x_heads), mv_p.reshape(1, B, n_mem, x_heads, xw // x_heads),
            hd(k_s, DB, T), hd(v_s, DB, T), ik_s[None], _state_from_block_diag(st_s)[None],
            fs[:, -1, :rw_proj][None])
```

```python
import functools
import math

import jax
import jax.numpy as jnp
from jax import lax
from jax.experimental import pallas as pl
from jax.experimental.pallas import tpu as pltpu

F32 = jnp.float32
BF16 = jnp.bfloat16
I32 = jnp.int32

LANES = 128
SUBLANES = 8
VMEM_LIMIT_BYTES = 56 * 1024 * 1024

HEAD_DIM = 64
PAIR = 2 * HEAD_DIM
GN_EPS = 64e-5
NORM_EPS = 1e-6
TOPK_MAX = 256
Q_BLOCK = 128
N_BUCKETS = 32
MAX_DISTANCE = 128
PAGE_SIZE = 128
N_GROUPS = 4
EXPERTS_PER_GROUP = 8
INT_MIN = -(2 ** 31)


def _cparams(*sem):
    return pltpu.CompilerParams(dimension_semantics=sem, vmem_limit_bytes=VMEM_LIMIT_BYTES)


def _mm(a, b):
    return jnp.dot(a.astype(BF16), b.astype(BF16), preferred_element_type=F32)


def _mm_nt(a, b):
    return lax.dot_general(a.astype(BF16), b.astype(BF16), (((1,), (1,)), ((), ())),
                           preferred_element_type=F32)


def _split2(x):
    hi = x.astype(BF16)
    lo = (x - hi.astype(F32)).astype(BF16)
    return hi, lo


def _mm3(a, b):
    ah, al = _split2(a)
    bh, bl = _split2(b)
    d = lambda x, y: jnp.dot(x, y, preferred_element_type=F32)
    return d(ah, bh) + (d(ah, bl) + d(al, bh))


def _mm_exact_rhs(a, b_bf16):
    hi = a.astype(BF16)
    r1 = a - hi.astype(F32)
    mid = r1.astype(BF16)
    lo = (r1 - mid.astype(F32)).astype(BF16)
    d = lambda x: jnp.dot(x, b_bf16, preferred_element_type=F32)
    return d(hi) + (d(mid) + d(lo))


def _rmsnorm(x, g):
    ms = jnp.mean(x * x, axis=-1, keepdims=True)
    return x * lax.rsqrt(ms + NORM_EPS) * g


def _norm_matmul_body(x_ref, g_ref, w_ref, o_ref, xn_ref):
    @pl.when(pl.program_id(1) == 0)
    def _():
        xn_ref[...] = _rmsnorm(x_ref[...], g_ref[...]).astype(BF16)

    o_ref[...] = jnp.dot(xn_ref[...], w_ref[...], preferred_element_type=F32)


def norm_matmul(x, g, w, *, tm, tn):
    n, d = x.shape
    m = w.shape[1]
    return pl.pallas_call(
        _norm_matmul_body,
        grid=(n // tm, m // tn),
        in_specs=[pl.BlockSpec((tm, d), lambda i, j: (i, 0)),
                  pl.BlockSpec((1, d), lambda i, j: (0, 0)),
                  pl.BlockSpec((d, tn), lambda i, j: (0, j))],
        out_specs=pl.BlockSpec((tm, tn), lambda i, j: (i, j)),
        out_shape=jax.ShapeDtypeStruct((n, m), F32),
        scratch_shapes=[pltpu.VMEM((tm, d), BF16)],
        compiler_params=_cparams("parallel", "arbitrary"),
        name="norm_matmul",
    )(x, g.reshape(1, d), w)


def _pair_ones():
    r = lax.broadcasted_iota(I32, (PAIR, PAIR), 0) // HEAD_DIM
    c = lax.broadcasted_iota(I32, (PAIR, PAIR), 1) // HEAD_DIM
    return (r == c).astype(BF16)


def _head_sum(x, ones_bd):
    return _mm_exact_rhs(x, ones_bd)


def _rwkv_prep_body(f_ref, prev8_ref, init_ref, mu_ref, w0_ref, w2_ref, a0_ref, a2_ref, g2_ref,
                    kk_ref, ka_ref, rk_ref,
                    r_o, k_o, v_o, kk_o, b_o, ld_o, g_o, bon_o, *, tm, width):
    i = pl.program_id(1)
    f = f_ref[0]
    prev_row = jnp.where(i == 0, init_ref[0], prev8_ref[0, SUBLANES - 1:SUBLANES, :])
    rolled = pltpu.roll(f, shift=1, axis=0)
    row = lax.broadcasted_iota(I32, f.shape, 0)
    f_prev = jnp.where(row == 0, prev_row, rolled)
    fs = f + (f_prev - f) * mu_ref[...]
    w_ = width
    r = fs[:, 0:w_]
    k = fs[:, w_:2 * w_]
    v = fs[:, 2 * w_:3 * w_]
    o = 3 * w_
    n_dec = w2_ref.shape[0]
    n_icl = a2_ref.shape[0]
    wd = fs[:, o:o + n_dec]
    ad = fs[:, o + n_dec:o + n_dec + n_icl]
    gd = fs[:, o + n_dec + n_icl:]
    z = w0_ref[...] + _mm3(jnp.tanh(wd), w2_ref[...])
    nz = -z
    softplus = jnp.maximum(nz, 0.0) + jnp.log(1.0 + jnp.exp(-jnp.abs(nz)))
    w = -softplus - 0.5
    ld = -jnp.exp(w)
    a = 1.0 / (1.0 + jnp.exp(-(a0_ref[...] + _mm3(ad, a2_ref[...]))))
    g = _mm3(1.0 / (1.0 + jnp.exp(-gd)), g2_ref[...])
    kk = k * kk_ref[...]
    k2 = k * (1.0 + (a - 1.0) * ka_ref[...])
    rk = r * k2 * rk_ref[...]
    ones_bd = _pair_ones()
    for p in range(w_ // PAIR):
        sl = slice(p * PAIR, (p + 1) * PAIR)
        kkp = kk[:, sl]
        nrm = jnp.sqrt(_head_sum(kkp * kkp, ones_bd))
        kkp = kkp / jnp.maximum(nrm, 1e-12)
        ap = a[:, sl]
        r_o[0, p] = r[:, sl]
        k_o[0, p] = k2[:, sl]
        v_o[0, p] = v[:, sl]
        kk_o[0, p] = kkp
        b_o[0, p] = kkp * ap
        ld_o[0, p] = ld[:, sl]
        g_o[0, p] = g[:, sl]
        bon_o[0, p] = _head_sum(rk[:, sl], ones_bd) * v[:, sl]


def rwkv_prep(f3, init_prev, mu, w0, w2, a0, a2, g2, k_k, k_a, r_k, *, tm, rw_proj, width):
    b, t, _ = f3.shape
    npair = width // PAIR
    row1 = lambda x: x.reshape(1, -1)
    kern = functools.partial(_rwkv_prep_body, tm=tm, width=width)
    full = lambda a: pl.BlockSpec(a.shape, lambda bi, i: (0,) * a.ndim)
    args = [row1(mu), row1(w0), w2, row1(a0), a2, g2, row1(k_k), row1(k_a), row1(r_k)]
    out_spec = pl.BlockSpec((1, npair, tm, PAIR), lambda bi, i: (bi, 0, i, 0))
    out_shape = jax.ShapeDtypeStruct((b, npair, t, PAIR), F32)
    return pl.pallas_call(
        kern,
        grid=(b, t // tm),
        in_specs=[pl.BlockSpec((1, tm, rw_proj), lambda bi, i: (bi, i, 0)),
                  pl.BlockSpec((1, SUBLANES, rw_proj),
                               lambda bi, i: (bi, jnp.maximum(i * (tm // SUBLANES) - 1, 0), 0)),
                  pl.BlockSpec((1, 1, rw_proj), lambda bi, i: (bi, 0, 0))] + [full(a) for a in args],
        out_specs=[out_spec] * 8,
        out_shape=[out_shape] * 8,
        compiler_params=_cparams("parallel", "parallel"),
        name="rwkv_prep",
    )(f3, f3, init_prev.reshape(b, 1, rw_proj), *args)


def _rwkv_chunk_body(r_ref, k_ref, v_ref, kk_ref, b_ref, ld_ref, g_ref, bon_ref, s0_ref,
                     lng_ref, lnb_ref, o_ref, st_ref, s_ref, *, L, npair, group):
    c = pl.program_id(1)

    @pl.when(c == 0)
    def _():
        s_ref[...] = s0_ref[0]

    L2 = 2 * L
    row = lax.broadcasted_iota(I32, (L2, L2), 0)
    col = lax.broadcasted_iota(I32, (L2, L2), 1)
    same = (row // L) == (col // L)
    tri_strict = same & (col < row)
    tri_incl = same & (col <= row)
    eye = (row == col).astype(F32)
    tr = lax.broadcasted_iota(I32, (L, L), 0)
    tc = lax.broadcasted_iota(I32, (L, L), 1)
    cum_mat = (tc <= tr).astype(BF16)
    lane = lax.broadcasted_iota(I32, (L, PAIR), 1)
    first = lane < HEAD_DIM
    ones_bd = _pair_ones()
    n_sq = max(int(math.ceil(math.log2(L))) - 1, 0)

    def block_diag(x):
        return jnp.concatenate([jnp.where(first, x, 0.0), jnp.where(first, 0.0, x)], axis=0)

    def group_step(gi, carry):
        ps = [gi * group + j for j in range(group)]
        each = lambda f, *cols: [f(*args) for args in zip(*cols)]
        ld = [ld_ref[0, p] for p in ps]
        cum = each(lambda x: _mm_exact_rhs_t(cum_mat, x), ld)
        dec = each(jnp.exp, cum)
        dec_inv = each(lambda c_: jnp.exp(-c_), cum)
        a_t = each(lambda p, c_, l_: block_diag(-kk_ref[0, p] * jnp.exp(c_ - l_)), ps, cum, ld)
        b_t = each(lambda p, e: block_diag(b_ref[0, p] * e), ps, dec_inv)
        k_t = each(lambda p, e: block_diag(k_ref[0, p] * e), ps, dec_inv)
        r_t = each(lambda p, e: block_diag(r_ref[0, p] * e), ps, dec)
        v_b = each(lambda p: block_diag(v_ref[0, p]), ps)
        a_ab = each(lambda a, b: jnp.where(tri_strict, _mm_nt(a, b), 0.0), a_t, b_t)
        a_ak = each(lambda a, k: jnp.where(tri_strict, _mm_nt(a, k), 0.0), a_t, k_t)
        a_rb = each(lambda r, b: jnp.where(tri_incl, _mm_nt(r, b), 0.0), r_t, b_t)
        a_rk = each(lambda r, k: jnp.where(tri_incl, _mm_nt(r, k), 0.0), r_t, k_t)
        x = a_ab
        t_inv = each(lambda a: eye + a, a_ab)
        for _ in range(n_sq):
            x = each(lambda x_: _mm(x_, x_), x)
            t_inv = each(lambda t, x_: t + _mm(x_, t), t_inv, x)
        s = [s_ref[p] for p in ps]
        rhs = each(lambda a, s_, ak, v: _mm_nt(a, s_) + _mm(ak, v), a_t, s, a_ak, v_b)
        u = each(_mm, t_inv, rhs)
        y_b = each(lambda r, s_, rb, u_, rk, v: _mm_nt(r, s_) + _mm(rb, u_) + _mm(rk, v),
                   r_t, s, a_rb, u, a_rk, v_b)
        s_new = each(lambda s_, u_, b, v, k, d: (s_ + _mm(u_.T, b) + _mm(v.T, k)) * d[L - 1:L, :],
                     s, u, b_t, v_b, k_t, dec)
        for p, sn in zip(ps, s_new):
            s_ref[p] = sn
        y = each(lambda yb: yb[:L] + yb[L:], y_b)
        mean = each(lambda y_: _head_sum(y_, ones_bd) * (1.0 / HEAD_DIM), y)
        d = each(lambda y_, m: y_ - m, y, mean)
        var = each(lambda d_: _head_sum(d_ * d_, ones_bd) * (1.0 / HEAD_DIM), d)
        for p, d_, v_ in zip(ps, d, var):
            yn = d_ * lax.rsqrt(v_ + GN_EPS) * lng_ref[p] + lnb_ref[p]
            o_ref[0, p] = (yn + bon_ref[0, p]) * g_ref[0, p]
        return carry

    lax.fori_loop(0, npair // group, group_step, 0)

    @pl.when(c == pl.num_programs(1) - 1)
    def _():
        st_ref[0] = s_ref[...]


def _mm_exact_rhs_t(m_bf16, x):
    hi = x.astype(BF16)
    r1 = x - hi.astype(F32)
    mid = r1.astype(BF16)
    lo = (r1 - mid.astype(F32)).astype(BF16)
    d = lambda y: jnp.dot(m_bf16, y, preferred_element_type=F32)
    return d(hi) + (d(mid) + d(lo))


def rwkv_chunk(feats, s0_bd, ln_g, ln_b, *, L, group=8):
    b, npair, t, _ = feats[0].shape
    blk = pl.BlockSpec((1, npair, L, PAIR), lambda bi, c: (bi, 0, c, 0))
    st_spec = pl.BlockSpec((1, npair, PAIR, PAIR), lambda bi, c: (bi, 0, 0, 0))
    par_spec = pl.BlockSpec((npair, 1, PAIR), lambda bi, c: (0, 0, 0))
    kern = functools.partial(_rwkv_chunk_body, L=L, npair=npair, group=group)
    return pl.pallas_call(
        kern,
        grid=(b, t // L),
        in_specs=[blk] * 8 + [st_spec, par_spec, par_spec],
        out_specs=[blk, st_spec],
        out_shape=[jax.ShapeDtypeStruct((b, npair, t, PAIR), F32),
                   jax.ShapeDtypeStruct((b, npair, PAIR, PAIR), F32)],
        scratch_shapes=[pltpu.VMEM((npair, PAIR, PAIR), F32)],
        compiler_params=_cparams("parallel", "arbitrary"),
        name="rwkv_chunk",
    )(*feats, s0_bd, ln_g.reshape(npair, 1, PAIR), ln_b.reshape(npair, 1, PAIR))


def _state_to_block_diag(s):
    b, h, n, _ = s.shape
    s = s.reshape(b, h // 2, 2, n, n)
    z = jnp.zeros_like(s[:, :, 0])
    top = jnp.concatenate([s[:, :, 0], z], axis=-1)
    bot = jnp.concatenate([z, s[:, :, 1]], axis=-1)
    return jnp.concatenate([top, bot], axis=-2)


def _state_from_block_diag(s_bd):
    b, p, _, _ = s_bd.shape
    n = HEAD_DIM
    return jnp.stack([s_bd[:, :, :n, :n], s_bd[:, :, n:, n:]], axis=2).reshape(b, 2 * p, n, n)


def _t5_bucket(dist):
    exact = N_BUCKETS // 2
    d = jnp.maximum(dist, 0)
    far = exact + (jnp.log(jnp.maximum(d, 1).astype(F32) / exact) / math.log(MAX_DISTANCE / exact)
                   * (N_BUCKETS - exact)).astype(I32)
    return jnp.where(d < exact, d, jnp.minimum(far, N_BUCKETS - 1))


def _bias_tables_body(rb_ref, o_ref, *, offsets, n_heads):
    r = lax.broadcasted_iota(I32, (Q_BLOCK, Q_BLOCK), 0)
    c = lax.broadcasted_iota(I32, (Q_BLOCK, Q_BLOCK), 1)
    for t, (off, key_major) in enumerate(offsets):
        bucket = _t5_bucket((c - r if key_major else r - c) + off)
        for h in range(n_heads):
            def body(bk, acc):
                return jnp.where(bucket == bk, rb_ref[bk, h], acc)
            tile = lax.fori_loop(0, N_BUCKETS, body, jnp.zeros((Q_BLOCK, Q_BLOCK), F32))
            o_ref[t, h] = tile - rb_ref[N_BUCKETS - 1, h]


def bias_tables(rel_bias, offsets):
    n_heads = rel_bias.shape[1]
    kern = functools.partial(_bias_tables_body, offsets=tuple(offsets), n_heads=n_heads)
    return pl.pallas_call(
        kern,
        in_specs=[pl.BlockSpec(memory_space=pltpu.SMEM)],
        out_specs=pl.BlockSpec(memory_space=pltpu.VMEM),
        out_shape=jax.ShapeDtypeStruct((len(offsets), n_heads, Q_BLOCK, Q_BLOCK), F32),
        name="bias_tables",
    )(rel_bias)


def _sortable_key(scores):
    bits = lax.bitcast_convert_type(scores + 0.0, I32)
    return jnp.where(bits < 0, bits ^ 0x7FFFFFFF, bits)


def _count(mask):
    return jnp.sum(mask.astype(F32), axis=-1, keepdims=True)


def _topk_select(key, topk, n_index_bits):
    rows, n = key.shape
    kf = float(topk)
    t0 = jnp.where(_count(key >= 0) >= kf, 0, INT_MIN).astype(I32)

    def value_bit(i, t):
        cand = t + lax.shift_left(jnp.int32(1), 30 - i)
        return jnp.where(_count(key >= cand) >= kf, cand, t)

    thr = lax.fori_loop(0, 31, value_bit, t0)
    above = key > thr
    ties = key == thr
    need = kf - _count(above)
    idx = lax.broadcasted_iota(I32, (rows, n), 1)

    def index_bit(i, m):
        cand = m + lax.shift_left(jnp.int32(1), n_index_bits - 1 - i)
        return jnp.where(_count(ties & (idx < cand)) <= need, cand, m)

    m = lax.fori_loop(0, n_index_bits, index_bit, jnp.zeros((rows, 1), I32))
    return above | (ties & (idx < m))


SUM_CHAINS = 4


def _sum_rows(x):
    r = x.shape[0]
    if r % (SUM_CHAINS * SUBLANES) == 0 and r > SUM_CHAINS * SUBLANES:
        x = jnp.sum(x.reshape(SUM_CHAINS, r // SUM_CHAINS, x.shape[1]), axis=1)
    return jnp.sum(x, axis=0, keepdims=True)


def _max_rows(x):
    r = x.shape[0]
    if r % (SUM_CHAINS * SUBLANES) == 0 and r > SUM_CHAINS * SUBLANES:
        x = jnp.max(x.reshape(SUM_CHAINS, r // SUM_CHAINS, x.shape[1]), axis=1)
    return jnp.max(x, axis=0, keepdims=True)


def _topk_select_cols(key, topk, n_index_bits):
    n, cols = key.shape
    kf = float(topk)
    cnt = lambda m: _sum_rows(m.astype(F32))
    t0 = jnp.where(cnt(key >= 0) >= kf, 0, INT_MIN).astype(I32)

    def value_bit(i, t):
        cand = t + lax.shift_left(jnp.int32(1), 30 - i)
        return jnp.where(cnt(key >= cand) >= kf, cand, t)

    thr = lax.fori_loop(0, 31, value_bit, t0)
    above = key > thr
    ties = key == thr
    need = kf - cnt(above)
    idx = lax.broadcasted_iota(I32, (n, cols), 0)

    def index_bit(i, m):
        cand = m + lax.shift_left(jnp.int32(1), n_index_bits - 1 - i)
        return jnp.where(cnt(ties & (idx < cand)) <= need, cand, m)

    m = lax.fori_loop(0, n_index_bits, index_bit, jnp.zeros((1, cols), I32))
    return above | (ties & (idx < m))


def _dsa_prompt_block(nb, ik_ref, iqt_ref, wt_ref, k_ref, qt_ref, vt_ref, bias_ref, o_ref, mask_ref,
                      *, n_heads, n_idx_heads, topk):
    w = nb * Q_BLOCK
    ik = ik_ref[0, :w, :]

    def idx_head(h, acc):
        dots = jnp.dot(ik, iqt_ref[0, h], preferred_element_type=F32)
        return acc + jnp.maximum(dots, 0.0) * wt_ref[0, h]

    scores = lax.fori_loop(0, n_idx_heads, idx_head, jnp.zeros((w, Q_BLOCK), F32), unroll=2)
    kpos = lax.broadcasted_iota(I32, (w, Q_BLOCK), 0)
    qpos = (nb - 1) * Q_BLOCK + lax.broadcasted_iota(I32, (w, Q_BLOCK), 1)
    valid = kpos <= qpos
    if w <= topk:
        sel = valid
    else:
        key = jnp.where(valid, _sortable_key(scores), INT_MIN)
        sel = valid & _topk_select_cols(key, topk, int(math.ceil(math.log2(w))) + 1)
    mask_ref[:w, :] = jnp.where(sel, 0.0, -jnp.inf)

    def attn_head(h, carry):
        logits = jnp.dot(k_ref[0, h, :w, :], qt_ref[0, h], preferred_element_type=F32) + mask_ref[:w, :]
        near = [logits[w - Q_BLOCK:] + bias_ref[0, h]]
        if nb >= 2:
            near = [logits[w - 2 * Q_BLOCK:w - Q_BLOCK] + bias_ref[1, h]] + near
        if nb >= 3:
            near = [logits[:w - 2 * Q_BLOCK]] + near
        logits = jnp.concatenate(near, axis=0) if len(near) > 1 else near[0]
        e = jnp.exp(logits - _max_rows(logits))
        den = _sum_rows(e)
        o = jnp.dot(vt_ref[0, h, :, :w], e.astype(BF16), preferred_element_type=F32)
        o_ref[0, h] = (o / den).astype(o_ref.dtype)
        return carry

    lax.fori_loop(0, n_heads, attn_head, 0, unroll=2)


def _dsa_prompt_body(ik_ref, iqt_ref, wt_ref, k_ref, qt_ref, vt_ref, bias_ref, o_ref, mask_ref,
                     *, n_heads, n_idx_heads, seq, topk):
    i = pl.program_id(1)
    for nb in range(1, seq // Q_BLOCK + 1):
        @pl.when(i == nb - 1)
        def _(nb=nb):
            _dsa_prompt_block(nb, ik_ref, iqt_ref, wt_ref, k_ref, qt_ref, vt_ref, bias_ref, o_ref, mask_ref,
                              n_heads=n_heads, n_idx_heads=n_idx_heads, topk=topk)


def dsa_prompt(ik, iqt, wts, k, qt, vt, bias_tiles, *, topk):
    b, h, dh, s = qt.shape
    ih = iqt.shape[1]
    kern = functools.partial(_dsa_prompt_body, n_heads=h, n_idx_heads=ih, seq=s, topk=topk)
    grid_spec = pltpu.PrefetchScalarGridSpec(
        num_scalar_prefetch=0,
        grid=(b, s // Q_BLOCK),
        in_specs=[pl.BlockSpec((1, s, ik.shape[2]), lambda bi, i: (bi, 0, 0)),
                  pl.BlockSpec((1, ih, iqt.shape[2], Q_BLOCK), lambda bi, i: (bi, 0, 0, i)),
                  pl.BlockSpec((1, ih, 1, Q_BLOCK), lambda bi, i: (bi, 0, 0, i)),
                  pl.BlockSpec((1, h, s, dh), lambda bi, i: (bi, 0, 0, 0)),
                  pl.BlockSpec((1, h, dh, Q_BLOCK), lambda bi, i: (bi, 0, 0, i)),
                  pl.BlockSpec((1, h, dh, s), lambda bi, i: (bi, 0, 0, 0)),
                  pl.BlockSpec(bias_tiles.shape, lambda bi, i: (0, 0, 0, 0))],
        out_specs=pl.BlockSpec((1, h, dh, Q_BLOCK), lambda bi, i: (bi, 0, 0, i)),
        scratch_shapes=[pltpu.VMEM((s, Q_BLOCK), F32)],
    )
    return pl.pallas_call(
        kern,
        grid_spec=grid_spec,
        out_shape=jax.ShapeDtypeStruct((b, h, dh, s), BF16),
        compiler_params=_cparams("parallel", "arbitrary"),
        name="dsa_prompt",
    )(ik, iqt, wts, k, qt, vt, bias_tiles)


def _dsa_sample_body(pt_ref, iq_ref, wt_ref, iknew_ref, qbd_ref, knew_ref, vnew_ref,
                     blast_ref, bnew_ref, cidx_hbm, ck_hbm, cv_hbm, o_ref,
                     ikbuf, kbuf, vbuf, sem_ik, sem_k, sem_v,
                     *, layer, n_pages, chunk, n_heads, n_idx_heads, t_new, topk):
    b = pl.program_id(0)
    past = n_pages * PAGE_SIZE
    n_chunks = n_pages // chunk
    rows = n_heads * t_new
    ck = chunk * PAGE_SIZE

    def page(p):
        return pt_ref[b * n_pages + p]

    def ik_copy(p):
        return pltpu.make_async_copy(cidx_hbm.at[layer, page(p)], ikbuf.at[p], sem_ik)

    def kv_copies(c, j):
        slot = c % 2
        p = c * chunk + j
        return (pltpu.make_async_copy(ck_hbm.at[layer, page(p)], kbuf.at[slot, j], sem_k.at[slot]),
                pltpu.make_async_copy(cv_hbm.at[layer, page(p)], vbuf.at[slot, j], sem_v.at[slot]))

    def start_chunk(c):
        for j in range(chunk):
            kc, vc = kv_copies(c, j)
            kc.start()
            vc.start()

    def wait_chunk(c):
        for j in range(chunk):
            kc, vc = kv_copies(c, j)
            kc.wait()
            vc.wait()

    def ik_start(p, carry):
        ik_copy(p).start()
        return carry

    def ik_wait(p, carry):
        ik_copy(p).wait()
        return carry

    lax.fori_loop(0, n_pages, ik_start, 0)
    start_chunk(0)
    ikbuf[n_pages] = iknew_ref[0]
    lax.fori_loop(0, n_pages, ik_wait, 0)

    n_keys = past + PAGE_SIZE
    ikt_all = jnp.concatenate([ikbuf[p] for p in range(n_pages + 1)], axis=1)
    dots = _mm(iq_ref[0], ikt_all)
    weighted = jnp.maximum(dots, 0.0) * wt_ref[0]
    scores = jnp.sum(weighted.reshape(n_idx_heads, t_new, n_keys), axis=0)
    qpos = past + lax.broadcasted_iota(I32, (t_new, n_keys), 0)
    kpos = lax.broadcasted_iota(I32, (t_new, n_keys), 1)
    valid = kpos <= qpos
    key = jnp.where(valid, _sortable_key(scores), INT_MIN)
    sel = valid & _topk_select(key, topk, int(math.log2(n_keys)) + 1)
    sel_rows = jnp.tile(sel, (n_heads, 1))

    qbd = qbd_ref[0]
    neg = -1e30

    def update(state, logits, selc, vt_bf16):
        m, l, acc = state
        s = jnp.where(selc, logits, -jnp.inf)
        m_new = jnp.maximum(m, jnp.max(s, axis=-1, keepdims=True))
        alpha = jnp.exp(m - m_new)
        p = jnp.exp(s - m_new)
        l = alpha * l + jnp.sum(p, axis=-1, keepdims=True)
        acc = alpha * acc + _mm_nt(p, vt_bf16)
        return m_new, l, acc

    def pages_t(buf, slot):
        return jnp.concatenate([buf[slot, j] for j in range(chunk)], axis=1).astype(BF16)

    state = (jnp.full((rows, 1), neg, F32), jnp.zeros((rows, 1), F32),
             jnp.zeros((rows, qbd.shape[1]), F32))
    for c in range(n_chunks):
        if c + 1 < n_chunks:
            start_chunk(c + 1)
        wait_chunk(c)
        slot = c % 2
        logits = _mm(qbd, pages_t(kbuf, slot))
        if c == n_chunks - 1:
            logits = jnp.concatenate([logits[:, :ck - PAGE_SIZE],
                                      logits[:, ck - PAGE_SIZE:] + blast_ref[...]], axis=1)
        state = update(state, logits, sel_rows[:, c * ck:(c + 1) * ck], pages_t(vbuf, slot))
    logits = _mm(qbd, knew_ref[0]) + bnew_ref[...]
    m, l, acc = update(state, logits, sel_rows[:, past:], vnew_ref[0])
    out = acc / l
    rh = lax.broadcasted_iota(I32, out.shape, 0) // t_new
    ch = lax.broadcasted_iota(I32, out.shape, 1) // HEAD_DIM
    out = jnp.where(rh == ch, out, 0.0)
    o_ref[0] = jnp.sum(out.reshape(n_heads, t_new, out.shape[1]), axis=0)


def dsa_sample(page_table, iq_rows, wt_rows, ik_new_t, q_bd, k_new_t, v_new_t, bias_last,
               bias_new, cache_idx_kt, cache_kt, cache_vt, *, layer, n_heads, n_idx_heads, t_new, topk, chunk):
    db, n_pages = page_table.shape
    rows = n_heads * t_new
    width = q_bd.shape[2]
    idx_dim = cache_idx_kt.shape[2]
    kern = functools.partial(_dsa_sample_body, layer=layer, n_pages=n_pages, chunk=chunk, n_heads=n_heads,
                             n_idx_heads=n_idx_heads, t_new=t_new, topk=topk)
    per_b = lambda shape: pl.BlockSpec((1,) + shape, lambda bi, pt: (bi,) + (0,) * len(shape))
    const = lambda shape: pl.BlockSpec(shape, lambda bi, pt: (0,) * len(shape))
    any_spec = pl.BlockSpec(memory_space=pl.ANY)
    grid_spec = pltpu.PrefetchScalarGridSpec(
        num_scalar_prefetch=1,
        grid=(db,),
        in_specs=[per_b((n_idx_heads * t_new, idx_dim)), per_b((n_idx_heads * t_new, 1)),
                  per_b((idx_dim, PAGE_SIZE)),
                  per_b((rows, width)), per_b((width, PAGE_SIZE)), per_b((width, PAGE_SIZE)),
                  const((rows, PAGE_SIZE)), const((rows, PAGE_SIZE)),
                  any_spec, any_spec, any_spec],
        out_specs=per_b((t_new, width)),
        scratch_shapes=[pltpu.VMEM((n_pages + 1, idx_dim, PAGE_SIZE), F32),
                        pltpu.VMEM((2, chunk, width, PAGE_SIZE), F32),
                        pltpu.VMEM((2, chunk, width, PAGE_SIZE), F32),
                        pltpu.SemaphoreType.DMA(()),
                        pltpu.SemaphoreType.DMA((2,)),
                        pltpu.SemaphoreType.DMA((2,))],
    )
    return pl.pallas_call(
        kern,
        grid_spec=grid_spec,
        out_shape=jax.ShapeDtypeStruct((db, t_new, width), F32),
        compiler_params=_cparams("arbitrary"),
        name="dsa_sample",
    )(page_table.reshape(-1), iq_rows, wt_rows, ik_new_t, q_bd, k_new_t, v_new_t, bias_last,
      bias_new, cache_idx_kt, cache_kt, cache_vt)


def _token_minor_cache(cache):
    l, pool, page = cache.shape[:3]
    nd = cache.ndim
    return cache.transpose((0, 1) + tuple(range(3, nd)) + (2,)).reshape(l, pool, -1, page)


def _dsa_sample_inputs(q, k_new, v_new, iq, ik_new, iw, bias_tiles):
    db, t, h, dh = q.shape
    ih = iq.shape[2]
    iq_rows = iq.transpose(0, 2, 1, 3).reshape(db, ih * t, -1).astype(BF16)
    wt_rows = (iw * (ih ** -0.5 * iq.shape[3] ** -0.5)).transpose(0, 2, 1).reshape(db, ih * t, 1)
    eye = jnp.eye(h, dtype=q.dtype) * dh ** -0.5
    q_bd = (q.transpose(0, 2, 1, 3)[:, :, :, None, :] * eye[None, :, None, :, None]).reshape(db, h * t, h * dh)
    page_t = lambda x: jnp.pad(x.reshape(db, t, -1).transpose(0, 2, 1), ((0, 0), (0, 0), (0, PAGE_SIZE - t)))
    bias_new = bias_tiles[0, :, :t, :].reshape(h * t, Q_BLOCK)
    bias_last = bias_tiles[1, :, :t, :].reshape(h * t, Q_BLOCK)
    return (iq_rows, wt_rows, page_t(ik_new), q_bd.astype(BF16), page_t(k_new).astype(BF16),
            page_t(v_new).astype(BF16), bias_last, bias_new)


def _matmul_residual_body(x_ref, a_ref, w_ref, o_ref):
    o_ref[...] = x_ref[...] + jnp.dot(a_ref[...], w_ref[...], preferred_element_type=F32)


def matmul_residual(x, a, w, *, tm, tn):
    n, d = x.shape
    kd = a.shape[1]
    return pl.pallas_call(
        _matmul_residual_body,
        grid=(n // tm, d // tn),
        in_specs=[pl.BlockSpec((tm, tn), lambda i, j: (i, j)),
                  pl.BlockSpec((tm, kd), lambda i, j: (i, 0)),
                  pl.BlockSpec((kd, tn), lambda i, j: (0, j))],
        out_specs=pl.BlockSpec((tm, tn), lambda i, j: (i, j)),
        out_shape=jax.ShapeDtypeStruct((n, d), F32),
        compiler_params=_cparams("parallel", "parallel"),
        name="matmul_residual",
    )(x, a, w)


def _cross_attn_body(x_ref, g_ref, wq_ref, mk_ref, mv_ref, wo_ref, o_ref, *, groups, t_rows, n_heads, head_dim):
    x = x_ref[...]
    h = _rmsnorm(x, g_ref[...]).astype(BF16)
    q = jnp.dot(h, wq_ref[...], preferred_element_type=F32).astype(BF16)
    scale = head_dim ** -0.5
    outs = []
    for gi in range(groups):
        qg = q[gi * t_rows:(gi + 1) * t_rows]
        heads = []
        for hh in range(n_heads):
            sl = slice(hh * head_dim, (hh + 1) * head_dim)
            logits = _mm_nt(qg[:, sl], mk_ref[gi, :, sl]) * scale
            mx = jnp.max(logits, axis=-1, keepdims=True)
            e = jnp.exp(logits - mx)
            p = e / jnp.sum(e, axis=-1, keepdims=True)
            heads.append(jnp.dot(p.astype(BF16), mv_ref[gi, :, sl], preferred_element_type=F32))
        outs.append(jnp.concatenate(heads, axis=1))
    o = jnp.concatenate(outs, axis=0) if groups > 1 else outs[0]
    o_ref[...] = x + jnp.dot(o.astype(BF16), wo_ref[...], preferred_element_type=F32)


def cross_attn(x, g, wq, mk, mv, wo, *, groups, t_rows, seq_tiles, n_heads):
    n, d = x.shape
    xw = wq.shape[1]
    rows = groups * t_rows
    m = mk.shape[1]
    kern = functools.partial(_cross_attn_body, groups=groups, t_rows=t_rows, n_heads=n_heads,
                             head_dim=xw // n_heads)
    return pl.pallas_call(
        kern,
        grid=(n // rows,),
        in_specs=[pl.BlockSpec((rows, d), lambda i: (i, 0)),
                  pl.BlockSpec((1, d), lambda i: (0, 0)),
                  pl.BlockSpec((d, xw), lambda i: (0, 0)),
                  pl.BlockSpec((groups, m, xw), lambda i: (i // seq_tiles, 0, 0)),
                  pl.BlockSpec((groups, m, xw), lambda i: (i // seq_tiles, 0, 0)),
                  pl.BlockSpec((xw, d), lambda i: (0, 0))],
        out_specs=pl.BlockSpec((rows, d), lambda i: (i, 0)),
        out_shape=jax.ShapeDtypeStruct((n, d), F32),
        compiler_params=_cparams("parallel"),
        name="cross_attn",
    )(x, g.reshape(1, d), wq, mk, mv, wo)


def _router_body(x_ref, g_ref, wr_ref, br_ref, h_ref, r_ref, *, n_groups, per_group):
    h = _rmsnorm(x_ref[...], g_ref[...])
    h_ref[...] = h
    logits = _mm3(h, wr_ref[...]) + br_ref[...]
    lane = lax.broadcasted_iota(I32, logits.shape, 1).astype(F32)
    big = 1e9
    first_lane = lambda hit: jnp.min(jnp.where(hit, lane, big), axis=-1, keepdims=True)
    gl = jnp.where(lane < n_groups, logits, -jnp.inf)
    gmax = jnp.max(gl, axis=-1, keepdims=True)
    grp = first_lane(gl == gmax)
    p_grp = 1.0 / jnp.sum(jnp.exp(gl - gmax), axis=-1, keepdims=True)
    e_id = lane - n_groups
    in_grp = (e_id >= grp * per_group) & (e_id < (grp + 1.0) * per_group)
    el = jnp.where(in_grp, logits, -jnp.inf)
    v1 = jnp.max(el, axis=-1, keepdims=True)
    i1 = first_lane(el == v1) - n_groups
    el2 = jnp.where(e_id == i1, -jnp.inf, el)
    v2 = jnp.max(el2, axis=-1, keepdims=True)
    i2 = first_lane(el2 == v2) - n_groups
    e2 = jnp.exp(v2 - v1)
    g1 = p_grp / (1.0 + e2)
    g2 = p_grp * e2 / (1.0 + e2)
    r_ref[...] = jnp.where(lane == 0, g1, jnp.where(lane == 1, g2, jnp.where(
        lane == 2, i1, jnp.where(lane == 3, i2, 0.0))))


def router(x, g, w_r, b_r, *, tm, n_groups, per_group):
    n, d = x.shape
    kern = functools.partial(_router_body, n_groups=n_groups, per_group=per_group)
    return pl.pallas_call(
        kern,
        grid=(n // tm,),
        in_specs=[pl.BlockSpec((tm, d), lambda i: (i, 0)),
                  pl.BlockSpec((1, d), lambda i: (0, 0)),
                  pl.BlockSpec((d, LANES), lambda i: (0, 0)),
                  pl.BlockSpec((1, LANES), lambda i: (0, 0))],
        out_specs=[pl.BlockSpec((tm, d), lambda i: (i, 0)),
                   pl.BlockSpec((tm, LANES), lambda i: (i, 0))],
        out_shape=[jax.ShapeDtypeStruct((n, d), F32), jax.ShapeDtypeStruct((n, LANES), F32)],
        compiler_params=_cparams("parallel"),
        name="moe_router",
    )(x, g.reshape(1, d), w_r, b_r)


def _slab_rows(buf, base, n_rows, n_slab):
    return jnp.concatenate([buf[pl.ds(base * n_slab + s, n_rows, stride=n_slab), :] for s in range(n_slab)],
                           axis=1)


def _moe_ffn_body(te_ref, tv_ref, tok_ref, h_hbm, w1_ref, w3_ref, w2_ref, o_ref, xbuf, sem, *, tm):
    t = pl.program_id(0)
    n_slab = h_hbm.shape[1]

    def row_copy(r):
        dst = xbuf.at[pl.ds(pl.multiple_of(r * n_slab, n_slab), n_slab)]
        return pltpu.make_async_copy(h_hbm.at[tok_ref[t * tm + r]], dst, sem)

    @pl.when(tv_ref[t] != 0)
    def _():
        def start(r, c):
            row_copy(r).start()
            return c

        def wait(r, c):
            row_copy(r).wait()
            return c

        lax.fori_loop(0, tm, start, 0)
        lax.fori_loop(0, tm, wait, 0)
        x = _slab_rows(xbuf, 0, tm, n_slab).astype(BF16)
        a = jnp.dot(x, w1_ref[0].astype(BF16), preferred_element_type=F32)
        bgate = jnp.dot(x, w3_ref[0].astype(BF16), preferred_element_type=F32)
        u = (a / (1.0 + jnp.exp(-a))) * bgate
        o_ref[...] = jnp.dot(u.astype(BF16), w2_ref[0].astype(BF16), preferred_element_type=F32)

    @pl.when(tv_ref[t] == 0)
    def _():
        o_ref[...] = jnp.zeros(o_ref.shape, F32)


def moe_ffn(tile_expert, tile_valid, row_tok, h, w1, w3, w2, *, tm):
    n_tiles = tile_expert.shape[0]
    d = h.shape[1]
    de = w1.shape[2]
    h = h.reshape(h.shape[0], d // LANES, LANES)
    grid_spec = pltpu.PrefetchScalarGridSpec(
        num_scalar_prefetch=3,
        grid=(n_tiles,),
        in_specs=[pl.BlockSpec(memory_space=pl.ANY),
                  pl.BlockSpec((1, d, de), lambda t, te, tv, tok: (te[t], 0, 0)),
                  pl.BlockSpec((1, d, de), lambda t, te, tv, tok: (te[t], 0, 0)),
                  pl.BlockSpec((1, de, d), lambda t, te, tv, tok: (te[t], 0, 0))],
        out_specs=pl.BlockSpec((tm, d), lambda t, te, tv, tok: (t, 0)),
        scratch_shapes=[pltpu.VMEM((tm * (d // LANES), LANES), F32), pltpu.SemaphoreType.DMA(())],
    )
    return pl.pallas_call(
        functools.partial(_moe_ffn_body, tm=tm),
        grid_spec=grid_spec,
        out_shape=jax.ShapeDtypeStruct((n_tiles * tm, d), F32),
        compiler_params=_cparams("arbitrary"),
        name="moe_ffn",
    )(tile_expert, tile_valid, row_tok, h, w1, w3, w2)


def _moe_dispatch(eids, n_experts, tm):
    n, k = eids.shape
    m = n * k
    flat_e = eids.reshape(-1)
    order = jnp.argsort(flat_e, stable=True)
    se = flat_e[order]
    counts = jnp.bincount(flat_e, length=n_experts)
    padded = (counts + tm - 1) // tm * tm
    pad_end = jnp.cumsum(padded)
    pad_start = pad_end - padded
    start = jnp.cumsum(counts) - counts
    dest = (pad_start[se] + jnp.arange(m) - start[se]).astype(I32)
    n_tiles = -(-m // tm) + n_experts
    row_tok = jnp.zeros((n_tiles * tm,), I32).at[dest].set((order // k).astype(I32))
    pos = jnp.zeros((m,), I32).at[order].set(dest)
    tile_start = jnp.arange(n_tiles) * tm
    tile_valid = (tile_start < pad_end[-1]).astype(I32)
    last = jnp.maximum(pad_end[-1] - 1, 0)
    tile_expert = jnp.minimum(jnp.searchsorted(pad_end, jnp.minimum(tile_start, last), side='right'),
                              n_experts - 1).astype(I32)
    return tile_expert, tile_valid, row_tok, pos


def _combine_body(pos_ref, x_ref, r_ref, g_ref, y_hbm, o_ref, ybuf, sem, *, tm, top_k):
    i = pl.program_id(0)
    n_slab = y_hbm.shape[1]

    def row_copy(j):
        tok = j // top_k
        kk = j % top_k
        dst = ybuf.at[pl.ds(pl.multiple_of((kk * tm + tok) * n_slab, n_slab), n_slab)]
        return pltpu.make_async_copy(y_hbm.at[pos_ref[i * tm * top_k + j]], dst, sem)

    def start(j, c):
        row_copy(j).start()
        return c

    def wait(j, c):
        row_copy(j).wait()
        return c

    lax.fori_loop(0, tm * top_k, start, 0)
    lax.fori_loop(0, tm * top_k, wait, 0)
    route = r_ref[...]
    x = x_ref[...]
    for kk in range(top_k):
        x = x + _slab_rows(ybuf, kk * tm, tm, n_slab) * route[:, kk:kk + 1]
    o_ref[...] = _rmsnorm(x, g_ref[...])


def moe_combine(pos, x, route, g, y, *, tm, top_k):
    n, d = x.shape
    y = y.reshape(y.shape[0], d // LANES, LANES)
    grid_spec = pltpu.PrefetchScalarGridSpec(
        num_scalar_prefetch=1,
        grid=(n // tm,),
        in_specs=[pl.BlockSpec((tm, d), lambda i, p: (i, 0)),
                  pl.BlockSpec((tm, LANES), lambda i, p: (i, 0)),
                  pl.BlockSpec((1, d), lambda i, p: (0, 0)),
                  pl.BlockSpec(memory_space=pl.ANY)],
        out_specs=pl.BlockSpec((tm, d), lambda i, p: (i, 0)),
        scratch_shapes=[pltpu.VMEM((top_k * tm * (d // LANES), LANES), F32), pltpu.SemaphoreType.DMA(())],
    )
    return pl.pallas_call(
        functools.partial(_combine_body, tm=tm, top_k=top_k),
        grid_spec=grid_spec,
        out_shape=jax.ShapeDtypeStruct((n, d), F32),
        compiler_params=_cparams("arbitrary"),
        name="moe_combine",
    )(pos, x, route, g.reshape(1, d), y)


def kernel(x_prompt, x_sample, mem_prompt, cache_k, cache_v, cache_idx_k, page_table, state_wkv, state_shift, cache_mem_k, cache_mem_v, g_mix, w_in, mu_shift, rw_w0, rw_w2, rw_a0, rw_a2, rw_g2, rw_kk, rw_ka, rw_rk, rw_ln_g, rw_ln_b, w_out, g_cross, g_mem, w_cq, w_ck, w_cv, w_co, g_ffn, w_rg, b_rg, w_re, b_re, w_e1, w_e3, w_e2, rel_bias, g_final):
    B, S, D = x_prompt.shape
    DB, T, _ = x_sample.shape
    assert w_in.shape[0] == 1, "single-layer trunk only"
    l = 0
    n_pages = page_table.shape[1]
    past = n_pages * PAGE_SIZE
    topk_p = min(TOPK_MAX, S // 4)
    topk_s = min(TOPK_MAX, (past + T) // 4)
    rw_proj = mu_shift.shape[1]
    width = rw_w0.shape[1]
    at_w = D - width
    n_heads = at_w // HEAD_DIM
    idx_dim = cache_idx_k.shape[-1]
    ih = (w_in.shape[2] - rw_proj - 3 * at_w - idx_dim) // (idx_dim + 1)
    xw = w_cq.shape[2]
    x_heads = cache_mem_k.shape[3]
    n_mem = mem_prompt.shape[1]
    n_experts = w_e1.shape[1]
    top_k = 2
    proj_pad = -(-w_in.shape[2] // 512) * 512
    o = rw_proj

    w_all = jnp.pad(w_in[l], ((0, 0), (0, proj_pad - w_in.shape[2]))).astype(BF16)
    w_out_b = w_out[l].astype(BF16)
    w_cq_b, w_co_b = w_cq[l].astype(BF16), w_co[l].astype(BF16)
    w_ckv = jnp.concatenate([w_ck[l], w_cv[l]], axis=1).astype(BF16)
    n_route = w_rg.shape[2] + w_re.shape[2]
    w_r = jnp.pad(jnp.concatenate([w_rg[l], w_re[l]], axis=1), ((0, 0), (0, LANES - n_route)))
    b_r = jnp.pad(jnp.concatenate([b_rg[l], b_re[l]]), (0, LANES - n_route)).reshape(1, LANES)
    rw_args = (mu_shift[l], rw_w0[l], rw_w2[l], rw_a0[l], rw_a2[l], rw_g2[l], rw_kk[l], rw_ka[l],
               rw_rk[l].reshape(-1))
    tiles = bias_tables(rel_bias, ((0, False), (Q_BLOCK, False), (0, True), (Q_BLOCK, True)))

    def split_attn(f3):
        q = f3[..., o:o + at_w]
        k = f3[..., o + at_w:o + 2 * at_w]
        v = f3[..., o + 2 * at_w:o + 3 * at_w]
        iq = f3[..., o + 3 * at_w:o + 3 * at_w + ih * idx_dim]
        ik = f3[..., o + 3 * at_w + ih * idx_dim:o + 3 * at_w + (ih + 1) * idx_dim]
        iw = f3[..., o + 3 * at_w + (ih + 1) * idx_dim:o + 3 * at_w + (ih + 1) * idx_dim + ih]
        return q, k, v, iq, ik, iw

    def rw_rows(y):
        b_, p_, t_, _ = y.shape
        return y.transpose(0, 2, 1, 3).reshape(b_ * t_, p_ * PAIR).astype(BF16)

    xp = x_prompt.reshape(B * S, D)
    fp = norm_matmul(xp, g_mix[l], w_all, tm=1024, tn=512).reshape(B, S, proj_pad)
    feats_p = rwkv_prep(fp, jnp.zeros((B, rw_proj), F32), *rw_args, tm=256, rw_proj=rw_proj, width=width)
    rw_p, st_p = rwkv_chunk(feats_p, jnp.zeros((B, width // PAIR, PAIR, PAIR), F32), rw_ln_g[l], rw_ln_b[l], L=64)
    q, k_p, v_p, iq, ik_p, iw = split_attn(fp)
    hm = lambda z, h_: z.reshape(B, S, h_, -1).transpose(0, 2, 1, 3).astype(BF16)
    hmt = lambda z, h_: z.reshape(B, S, h_, -1).transpose(0, 2, 3, 1).astype(BF16)
    at_p = dsa_prompt(ik_p.astype(BF16), hmt(iq, ih),
                      (iw * (ih ** -0.5 * idx_dim ** -0.5)).transpose(0, 2, 1)[:, :, None, :],
                      hm(k_p, n_heads), hmt(q * HEAD_DIM ** -0.5, n_heads), hmt(v_p, n_heads),
                      tiles[2:4], topk=topk_p)
    mix_p = jnp.concatenate([rw_rows(rw_p), at_p.transpose(0, 3, 1, 2).reshape(B * S, at_w)], axis=1)
    x1_p = matmul_residual(xp, mix_p, w_out_b, tm=512, tn=512)
    mkv = norm_matmul(mem_prompt.reshape(B * n_mem, D), g_mem[l], w_ckv, tm=256, tn=512)
    mk_p = mkv[:, :xw].reshape(B, n_mem, xw)
    mv_p = mkv[:, xw:].reshape(B, n_mem, xw)
    x2_p = cross_attn(x1_p, g_cross[l], w_cq_b, mk_p.astype(BF16), mv_p.astype(BF16), w_co_b,
                      groups=1, t_rows=512, seq_tiles=S // 512, n_heads=x_heads)

    xs = x_sample.reshape(DB * T, D)
    fs = norm_matmul(xs, g_mix[l], w_all, tm=DB * T, tn=512).reshape(DB, T, proj_pad)
    feats_s = rwkv_prep(fs, state_shift[l], *rw_args, tm=T, rw_proj=rw_proj, width=width)
    rw_s, st_s = rwkv_chunk(feats_s, _state_to_block_diag(state_wkv[l]), rw_ln_g[l], rw_ln_b[l], L=T)
    q2, k_s, v_s, iq2, ik_s, iw2 = split_attn(fs)
    r4 = lambda z, h_: z.reshape(DB, T, h_, -1)
    s_args = _dsa_sample_inputs(r4(q2, n_heads), r4(k_s, n_heads), r4(v_s, n_heads), r4(iq2, ih), ik_s, iw2,
                                tiles[0:2])
    at_s = dsa_sample(page_table, *s_args, _token_minor_cache(cache_idx_k), _token_minor_cache(cache_k),
                      _token_minor_cache(cache_v),
                      layer=l, n_heads=n_heads, n_idx_heads=ih, t_new=T, topk=topk_s, chunk=8)
    mix_s = jnp.concatenate([rw_rows(rw_s), at_s.reshape(DB * T, at_w).astype(BF16)], axis=1)
    x1_s = matmul_residual(xs, mix_s, w_out_b, tm=DB * T, tn=512)
    x2_s = cross_attn(x1_s, g_cross[l], w_cq_b, cache_mem_k[l].reshape(DB, n_mem, xw).astype(BF16),
                      cache_mem_v[l].reshape(DB, n_mem, xw).astype(BF16), w_co_b,
                      groups=8, t_rows=T, seq_tiles=1, n_heads=x_heads)

    h_p, route_p = router(x2_p, g_ffn[l], w_r, b_r, tm=512, n_groups=w_rg.shape[2],
                          per_group=w_re.shape[2] // w_rg.shape[2])
    h_s, route_s = router(x2_s, g_ffn[l], w_r, b_r, tm=DB * T, n_groups=w_rg.shape[2],
                          per_group=w_re.shape[2] // w_rg.shape[2])
    h_all = jnp.concatenate([h_p, h_s], axis=0)
    eids = jnp.concatenate([route_p[:, top_k:2 * top_k], route_s[:, top_k:2 * top_k]], axis=0).astype(I32)
    tile_expert, tile_valid, row_tok, pos = _moe_dispatch(eids, n_experts, 256)
    y_rows = moe_ffn(tile_expert, tile_valid, row_tok, h_all, w_e1[l], w_e3[l], w_e2[l], tm=256)
    n_p = B * S
    y_p = moe_combine(pos[:n_p * top_k], x2_p, route_p, g_final, y_rows, tm=256, top_k=top_k)
    y_s = moe_combine(pos[n_p * top_k:], x2_s, route_s, g_final, y_rows, tm=DB * T, top_k=top_k)

    hd = lambda z, b_, t_: z.reshape(1, b_, t_, n_heads, HEAD_DIM)
    return (y_p.reshape(B, S, D), y_s.reshape(DB, T, D),
            hd(k_p, B, S), hd(v_p, B, S), ik_p[None], _state_from_block_diag(st_p)[None], fp[:, -1, :rw_proj][None],
            mk_p.reshape(1, B, n_mem, x_heads, xw // x_heads), mv_p.reshape(1, B, n_mem, x_heads, xw // x_heads),
            hd(k_s, DB, T), hd(v_s, DB, T), ik_s[None], _state_from_block_diag(st_s)[None],
            fs[:, -1, :rw_proj][None])
```

```python
import functools
import math

import jax
import jax.numpy as jnp
from jax import lax
from jax.experimental import pallas as pl
from jax.experimental.pallas import tpu as pltpu

F32 = jnp.float32
BF16 = jnp.bfloat16
I32 = jnp.int32

LANES = 128
SUBLANES = 8
VMEM_LIMIT_BYTES = 56 * 1024 * 1024

HEAD_DIM = 64
PAIR = 2 * HEAD_DIM
GN_EPS = 64e-5
NORM_EPS = 1e-6
TOPK_MAX = 256
Q_BLOCK = 128
N_BUCKETS = 32
MAX_DISTANCE = 128
PAGE_SIZE = 128
N_GROUPS = 4
EXPERTS_PER_GROUP = 8
INT_MIN = -(2 ** 31)


def _cparams(*sem):
    return pltpu.CompilerParams(dimension_semantics=sem, vmem_limit_bytes=VMEM_LIMIT_BYTES)


def _mm(a, b):
    return jnp.dot(a.astype(BF16), b.astype(BF16), preferred_element_type=F32)


def _mm_nt(a, b):
    return lax.dot_general(a.astype(BF16), b.astype(BF16), (((1,), (1,)), ((), ())),
                           preferred_element_type=F32)


def _split2(x):
    hi = x.astype(BF16)
    lo = (x - hi.astype(F32)).astype(BF16)
    return hi, lo


def _mm3(a, b):
    ah, al = _split2(a)
    bh, bl = _split2(b)
    d = lambda x, y: jnp.dot(x, y, preferred_element_type=F32)
    return d(ah, bh) + (d(ah, bl) + d(al, bh))


def _mm_exact_rhs(a, b_bf16):
    hi = a.astype(BF16)
    r1 = a - hi.astype(F32)
    mid = r1.astype(BF16)
    lo = (r1 - mid.astype(F32)).astype(BF16)
    d = lambda x: jnp.dot(x, b_bf16, preferred_element_type=F32)
    return d(hi) + (d(mid) + d(lo))


def _rmsnorm(x, g):
    ms = jnp.mean(x * x, axis=-1, keepdims=True)
    return x * lax.rsqrt(ms + NORM_EPS) * g


def _norm_matmul_body(x_ref, g_ref, w_ref, o_ref, xn_ref, *, w_is_transposed):
    @pl.when(pl.program_id(1) == 0)
    def _():
        xn_ref[...] = _rmsnorm(x_ref[...], g_ref[...]).astype(BF16)

    mm = _mm_nt if w_is_transposed else _mm
    o_ref[...] = mm(xn_ref[...], w_ref[...])


def norm_matmul(x, g, w, *, tm, tn, w_is_transposed=False):
    n, d = x.shape
    m = w.shape[0] if w_is_transposed else w.shape[1]
    w_spec = (pl.BlockSpec((tn, d), lambda i, j: (j, 0)) if w_is_transposed
              else pl.BlockSpec((d, tn), lambda i, j: (0, j)))
    return pl.pallas_call(
        functools.partial(_norm_matmul_body, w_is_transposed=w_is_transposed),
        grid=(n // tm, m // tn),
        in_specs=[pl.BlockSpec((tm, d), lambda i, j: (i, 0)),
                  pl.BlockSpec((1, d), lambda i, j: (0, 0)),
                  w_spec],
        out_specs=pl.BlockSpec((tm, tn), lambda i, j: (i, j)),
        out_shape=jax.ShapeDtypeStruct((n, m), F32),
        scratch_shapes=[pltpu.VMEM((tm, d), BF16)],
        compiler_params=_cparams("parallel", "arbitrary"),
        name="norm_matmul",
    )(x, g.reshape(1, d), w)


def _pair_ones():
    r = lax.broadcasted_iota(I32, (PAIR, PAIR), 0) // HEAD_DIM
    c = lax.broadcasted_iota(I32, (PAIR, PAIR), 1) // HEAD_DIM
    return (r == c).astype(BF16)


def _head_sum(x, ones_bd):
    return _mm_exact_rhs(x, ones_bd)


def _rwkv_prep_body(f_ref, prev8_ref, init_ref, mu_ref, w0_ref, w2_ref, a0_ref, a2_ref, g2_ref,
                    kk_ref, ka_ref, rk_ref,
                    r_o, k_o, v_o, kk_o, b_o, ld_o, g_o, bon_o, *, tm, width):
    i = pl.program_id(1)
    f = f_ref[0]
    prev_row = jnp.where(i == 0, init_ref[0], prev8_ref[0, SUBLANES - 1:SUBLANES, :])
    rolled = pltpu.roll(f, shift=1, axis=0)
    row = lax.broadcasted_iota(I32, f.shape, 0)
    f_prev = jnp.where(row == 0, prev_row, rolled)
    fs = f + (f_prev - f) * mu_ref[...]
    w_ = width
    r = fs[:, 0:w_]
    k = fs[:, w_:2 * w_]
    v = fs[:, 2 * w_:3 * w_]
    o = 3 * w_
    n_dec = w2_ref.shape[0]
    n_icl = a2_ref.shape[0]
    wd = fs[:, o:o + n_dec]
    ad = fs[:, o + n_dec:o + n_dec + n_icl]
    gd = fs[:, o + n_dec + n_icl:]
    z = w0_ref[...] + _mm3(jnp.tanh(wd), w2_ref[...])
    nz = -z
    softplus = jnp.maximum(nz, 0.0) + jnp.log(1.0 + jnp.exp(-jnp.abs(nz)))
    w = -softplus - 0.5
    ld = -jnp.exp(w)
    a = 1.0 / (1.0 + jnp.exp(-(a0_ref[...] + _mm3(ad, a2_ref[...]))))
    g = _mm3(1.0 / (1.0 + jnp.exp(-gd)), g2_ref[...])
    kk = k * kk_ref[...]
    k2 = k * (1.0 + (a - 1.0) * ka_ref[...])
    rk = r * k2 * rk_ref[...]
    ones_bd = _pair_ones()
    for p in range(w_ // PAIR):
        sl = slice(p * PAIR, (p + 1) * PAIR)
        kkp = kk[:, sl]
        nrm = jnp.sqrt(_head_sum(kkp * kkp, ones_bd))
        kkp = kkp / jnp.maximum(nrm, 1e-12)
        ap = a[:, sl]
        r_o[0, p] = r[:, sl]
        k_o[0, p] = k2[:, sl]
        v_o[0, p] = v[:, sl]
        kk_o[0, p] = kkp
        b_o[0, p] = kkp * ap
        ld_o[0, p] = ld[:, sl]
        g_o[0, p] = g[:, sl]
        bon_o[0, p] = _head_sum(rk[:, sl], ones_bd) * v[:, sl]


def rwkv_prep(f3, init_prev, mu, w0, w2, a0, a2, g2, k_k, k_a, r_k, *, tm, rw_proj, width):
    b, t, _ = f3.shape
    npair = width // PAIR
    row1 = lambda x: x.reshape(1, -1)
    kern = functools.partial(_rwkv_prep_body, tm=tm, width=width)
    full = lambda a: pl.BlockSpec(a.shape, lambda bi, i: (0,) * a.ndim)
    args = [row1(mu), row1(w0), w2, row1(a0), a2, g2, row1(k_k), row1(k_a), row1(r_k)]
    out_spec = pl.BlockSpec((1, npair, tm, PAIR), lambda bi, i: (bi, 0, i, 0))
    out_shape = jax.ShapeDtypeStruct((b, npair, t, PAIR), F32)
    return pl.pallas_call(
        kern,
        grid=(b, t // tm),
        in_specs=[pl.BlockSpec((1, tm, rw_proj), lambda bi, i: (bi, i, 0)),
                  pl.BlockSpec((1, SUBLANES, rw_proj),
                               lambda bi, i: (bi, jnp.maximum(i * (tm // SUBLANES) - 1, 0), 0)),
                  pl.BlockSpec((1, 1, rw_proj), lambda bi, i: (bi, 0, 0))] + [full(a) for a in args],
        out_specs=[out_spec] * 8,
        out_shape=[out_shape] * 8,
        compiler_params=_cparams("parallel", "parallel"),
        name="rwkv_prep",
    )(f3, f3, init_prev.reshape(b, 1, rw_proj), *args)


def _rwkv_chunk_body(r_ref, k_ref, v_ref, kk_ref, b_ref, ld_ref, g_ref, bon_ref, s0_ref,
                     lng_ref, lnb_ref, o_ref, st_ref, s_ref, *, L, npair, group):
    c = pl.program_id(1)

    @pl.when(c == 0)
    def _():
        s_ref[...] = s0_ref[0]

    L2 = 2 * L
    row = lax.broadcasted_iota(I32, (L2, L2), 0)
    col = lax.broadcasted_iota(I32, (L2, L2), 1)
    same = (row // L) == (col // L)
    tri_strict = same & (col < row)
    tri_incl = same & (col <= row)
    eye = (row == col).astype(F32)
    tr = lax.broadcasted_iota(I32, (L, L), 0)
    tc = lax.broadcasted_iota(I32, (L, L), 1)
    cum_mat = (tc <= tr).astype(BF16)
    lane = lax.broadcasted_iota(I32, (L, PAIR), 1)
    first = lane < HEAD_DIM
    ones_bd = _pair_ones()
    n_sq = max(int(math.ceil(math.log2(L))) - 1, 0)

    def block_diag(x):
        return jnp.concatenate([jnp.where(first, x, 0.0), jnp.where(first, 0.0, x)], axis=0)

    def group_step(gi, carry):
        ps = [gi * group + j for j in range(group)]
        each = lambda f, *cols: [f(*args) for args in zip(*cols)]
        ld = [ld_ref[0, p] for p in ps]
        cum = each(lambda x: _mm_exact_rhs_t(cum_mat, x), ld)
        dec = each(jnp.exp, cum)
        dec_inv = each(lambda c_: jnp.exp(-c_), cum)
        a_t = each(lambda p, c_, l_: block_diag(-kk_ref[0, p] * jnp.exp(c_ - l_)), ps, cum, ld)
        b_t = each(lambda p, e: block_diag(b_ref[0, p] * e), ps, dec_inv)
        k_t = each(lambda p, e: block_diag(k_ref[0, p] * e), ps, dec_inv)
        r_t = each(lambda p, e: block_diag(r_ref[0, p] * e), ps, dec)
        v_b = each(lambda p: block_diag(v_ref[0, p]), ps)
        a_ab = each(lambda a, b: jnp.where(tri_strict, _mm_nt(a, b), 0.0), a_t, b_t)
        a_ak = each(lambda a, k: jnp.where(tri_strict, _mm_nt(a, k), 0.0), a_t, k_t)
        a_rb = each(lambda r, b: jnp.where(tri_incl, _mm_nt(r, b), 0.0), r_t, b_t)
        a_rk = each(lambda r, k: jnp.where(tri_incl, _mm_nt(r, k), 0.0), r_t, k_t)
        x = a_ab
        t_inv = each(lambda a: eye + a, a_ab)
        for _ in range(n_sq):
            x = each(lambda x_: _mm(x_, x_), x)
            t_inv = each(lambda t, x_: t + _mm(x_, t), t_inv, x)
        s = [s_ref[p] for p in ps]
        rhs = each(lambda a, s_, ak, v: _mm_nt(a, s_) + _mm(ak, v), a_t, s, a_ak, v_b)
        u = each(_mm, t_inv, rhs)
        y_b = each(lambda r, s_, rb, u_, rk, v: _mm_nt(r, s_) + _mm(rb, u_) + _mm(rk, v),
                   r_t, s, a_rb, u, a_rk, v_b)
        s_new = each(lambda s_, u_, b, v, k, d: (s_ + _mm(u_.T, b) + _mm(v.T, k)) * d[L - 1:L, :],
                     s, u, b_t, v_b, k_t, dec)
        for p, sn in zip(ps, s_new):
            s_ref[p] = sn
        y = each(lambda yb: yb[:L] + yb[L:], y_b)
        mean = each(lambda y_: _head_sum(y_, ones_bd) * (1.0 / HEAD_DIM), y)
        d = each(lambda y_, m: y_ - m, y, mean)
        var = each(lambda d_: _head_sum(d_ * d_, ones_bd) * (1.0 / HEAD_DIM), d)
        for p, d_, v_ in zip(ps, d, var):
            yn = d_ * lax.rsqrt(v_ + GN_EPS) * lng_ref[p] + lnb_ref[p]
            o_ref[0, p] = (yn + bon_ref[0, p]) * g_ref[0, p]
        return carry

    lax.fori_loop(0, npair // group, group_step, 0)

    @pl.when(c == pl.num_programs(1) - 1)
    def _():
        st_ref[0] = s_ref[...]


def _mm_exact_rhs_t(m_bf16, x):
    hi = x.astype(BF16)
    r1 = x - hi.astype(F32)
    mid = r1.astype(BF16)
    lo = (r1 - mid.astype(F32)).astype(BF16)
    d = lambda y: jnp.dot(m_bf16, y, preferred_element_type=F32)
    return d(hi) + (d(mid) + d(lo))


def rwkv_chunk(feats, s0_bd, ln_g, ln_b, *, L, group=8):
    b, npair, t, _ = feats[0].shape
    blk = pl.BlockSpec((1, npair, L, PAIR), lambda bi, c: (bi, 0, c, 0))
    st_spec = pl.BlockSpec((1, npair, PAIR, PAIR), lambda bi, c: (bi, 0, 0, 0))
    par_spec = pl.BlockSpec((npair, 1, PAIR), lambda bi, c: (0, 0, 0))
    kern = functools.partial(_rwkv_chunk_body, L=L, npair=npair, group=group)
    return pl.pallas_call(
        kern,
        grid=(b, t // L),
        in_specs=[blk] * 8 + [st_spec, par_spec, par_spec],
        out_specs=[blk, st_spec],
        out_shape=[jax.ShapeDtypeStruct((b, npair, t, PAIR), F32),
                   jax.ShapeDtypeStruct((b, npair, PAIR, PAIR), F32)],
        scratch_shapes=[pltpu.VMEM((npair, PAIR, PAIR), F32)],
        compiler_params=_cparams("parallel", "arbitrary"),
        name="rwkv_chunk",
    )(*feats, s0_bd, ln_g.reshape(npair, 1, PAIR), ln_b.reshape(npair, 1, PAIR))


def _state_to_block_diag(s):
    b, h, n, _ = s.shape
    s = s.reshape(b, h // 2, 2, n, n)
    z = jnp.zeros_like(s[:, :, 0])
    top = jnp.concatenate([s[:, :, 0], z], axis=-1)
    bot = jnp.concatenate([z, s[:, :, 1]], axis=-1)
    return jnp.concatenate([top, bot], axis=-2)


def _state_from_block_diag(s_bd):
    b, p, _, _ = s_bd.shape
    n = HEAD_DIM
    return jnp.stack([s_bd[:, :, :n, :n], s_bd[:, :, n:, n:]], axis=2).reshape(b, 2 * p, n, n)


def _t5_bucket(dist):
    exact = N_BUCKETS // 2
    d = jnp.maximum(dist, 0)
    far = exact + (jnp.log(jnp.maximum(d, 1).astype(F32) / exact) / math.log(MAX_DISTANCE / exact)
                   * (N_BUCKETS - exact)).astype(I32)
    return jnp.where(d < exact, d, jnp.minimum(far, N_BUCKETS - 1))


def _bias_tables_body(rb_ref, o_ref, *, offsets, n_heads):
    r = lax.broadcasted_iota(I32, (Q_BLOCK, Q_BLOCK), 0)
    c = lax.broadcasted_iota(I32, (Q_BLOCK, Q_BLOCK), 1)
    for t, (off, key_major) in enumerate(offsets):
        bucket = _t5_bucket((c - r if key_major else r - c) + off)
        for h in range(n_heads):
            def body(bk, acc):
                return jnp.where(bucket == bk, rb_ref[bk, h], acc)
            tile = lax.fori_loop(0, N_BUCKETS, body, jnp.zeros((Q_BLOCK, Q_BLOCK), F32))
            o_ref[t, h] = tile - rb_ref[N_BUCKETS - 1, h]


def bias_tables(rel_bias, offsets):
    n_heads = rel_bias.shape[1]
    kern = functools.partial(_bias_tables_body, offsets=tuple(offsets), n_heads=n_heads)
    return pl.pallas_call(
        kern,
        in_specs=[pl.BlockSpec(memory_space=pltpu.SMEM)],
        out_specs=pl.BlockSpec(memory_space=pltpu.VMEM),
        out_shape=jax.ShapeDtypeStruct((len(offsets), n_heads, Q_BLOCK, Q_BLOCK), F32),
        name="bias_tables",
    )(rel_bias)


def _sortable_key(scores):
    bits = lax.bitcast_convert_type(scores + 0.0, I32)
    return jnp.where(bits < 0, bits ^ 0x7FFFFFFF, bits)


def _count(mask):
    return jnp.sum(mask.astype(F32), axis=-1, keepdims=True)


def _topk_select(key, topk, n_index_bits):
    rows, n = key.shape
    kf = float(topk)
    t0 = jnp.where(_count(key >= 0) >= kf, 0, INT_MIN).astype(I32)

    def value_bit(i, t):
        cand = t + lax.shift_left(jnp.int32(1), 30 - i)
        return jnp.where(_count(key >= cand) >= kf, cand, t)

    thr = lax.fori_loop(0, 31, value_bit, t0)
    above = key > thr
    ties = key == thr
    need = kf - _count(above)
    idx = lax.broadcasted_iota(I32, (rows, n), 1)

    def index_bit(i, m):
        cand = m + lax.shift_left(jnp.int32(1), n_index_bits - 1 - i)
        return jnp.where(_count(ties & (idx < cand)) <= need, cand, m)

    m = lax.fori_loop(0, n_index_bits, index_bit, jnp.zeros((rows, 1), I32))
    return above | (ties & (idx < m))


SUM_CHAINS = 4


def _sum_rows(x):
    r = x.shape[0]
    if r % (SUM_CHAINS * SUBLANES) == 0 and r > SUM_CHAINS * SUBLANES:
        x = jnp.sum(x.reshape(SUM_CHAINS, r // SUM_CHAINS, x.shape[1]), axis=1)
    return jnp.sum(x, axis=0, keepdims=True)


def _max_rows(x):
    r = x.shape[0]
    if r % (SUM_CHAINS * SUBLANES) == 0 and r > SUM_CHAINS * SUBLANES:
        x = jnp.max(x.reshape(SUM_CHAINS, r // SUM_CHAINS, x.shape[1]), axis=1)
    return jnp.max(x, axis=0, keepdims=True)


def _topk_select_cols(key, topk, n_index_bits):
    n, cols = key.shape
    kf = float(topk)
    cnt = lambda m: _sum_rows(m.astype(F32))
    t0 = jnp.where(cnt(key >= 0) >= kf, 0, INT_MIN).astype(I32)

    def value_bit(i, t):
        cand = t + lax.shift_left(jnp.int32(1), 30 - i)
        return jnp.where(cnt(key >= cand) >= kf, cand, t)

    thr = lax.fori_loop(0, 31, value_bit, t0)
    above = key > thr
    ties = key == thr
    need = kf - cnt(above)
    idx = lax.broadcasted_iota(I32, (n, cols), 0)

    def index_bit(i, m):
        cand = m + lax.shift_left(jnp.int32(1), n_index_bits - 1 - i)
        return jnp.where(cnt(ties & (idx < cand)) <= need, cand, m)

    m = lax.fori_loop(0, n_index_bits, index_bit, jnp.zeros((1, cols), I32))
    return above | (ties & (idx < m))


def _dsa_prompt_block(nb, ik_ref, iqt_ref, wt_ref, k_ref, qt_ref, vt_ref, bias_ref, o_ref, mask_ref,
                      *, n_heads, n_idx_heads, topk):
    w = nb * Q_BLOCK
    ik = ik_ref[0, :w, :]

    def idx_head(h, acc):
        dots = jnp.dot(ik, iqt_ref[0, h], preferred_element_type=F32)
        return acc + jnp.maximum(dots, 0.0) * wt_ref[0, h]

    scores = lax.fori_loop(0, n_idx_heads, idx_head, jnp.zeros((w, Q_BLOCK), F32), unroll=2)
    kpos = lax.broadcasted_iota(I32, (w, Q_BLOCK), 0)
    qpos = (nb - 1) * Q_BLOCK + lax.broadcasted_iota(I32, (w, Q_BLOCK), 1)
    valid = kpos <= qpos
    if w <= topk:
        sel = valid
    else:
        key = jnp.where(valid, _sortable_key(scores), INT_MIN)
        sel = valid & _topk_select_cols(key, topk, int(math.ceil(math.log2(w))) + 1)
    mask_ref[:w, :] = jnp.where(sel, 0.0, -jnp.inf)

    def attn_head(h, carry):
        logits = jnp.dot(k_ref[0, h, :w, :], qt_ref[0, h], preferred_element_type=F32) + mask_ref[:w, :]
        near = [logits[w - Q_BLOCK:] + bias_ref[0, h]]
        if nb >= 2:
            near = [logits[w - 2 * Q_BLOCK:w - Q_BLOCK] + bias_ref[1, h]] + near
        if nb >= 3:
            near = [logits[:w - 2 * Q_BLOCK]] + near
        logits = jnp.concatenate(near, axis=0) if len(near) > 1 else near[0]
        e = jnp.exp(logits - _max_rows(logits))
        den = _sum_rows(e)
        o = jnp.dot(vt_ref[0, h, :, :w], e.astype(BF16), preferred_element_type=F32)
        o_ref[0, h] = (o / den).astype(o_ref.dtype)
        return carry

    lax.fori_loop(0, n_heads, attn_head, 0, unroll=2)


def _dsa_prompt_body(ik_ref, iqt_ref, wt_ref, k_ref, qt_ref, vt_ref, bias_ref, o_ref, mask_ref,
                     *, n_heads, n_idx_heads, seq, topk):
    i = pl.program_id(1)
    for nb in range(1, seq // Q_BLOCK + 1):
        @pl.when(i == nb - 1)
        def _(nb=nb):
            _dsa_prompt_block(nb, ik_ref, iqt_ref, wt_ref, k_ref, qt_ref, vt_ref, bias_ref, o_ref, mask_ref,
                              n_heads=n_heads, n_idx_heads=n_idx_heads, topk=topk)


def dsa_prompt(ik, iqt, wts, k, qt, vt, bias_tiles, *, topk):
    b, h, dh, s = qt.shape
    ih = iqt.shape[1]
    kern = functools.partial(_dsa_prompt_body, n_heads=h, n_idx_heads=ih, seq=s, topk=topk)
    grid_spec = pltpu.PrefetchScalarGridSpec(
        num_scalar_prefetch=0,
        grid=(b, s // Q_BLOCK),
        in_specs=[pl.BlockSpec((1, s, ik.shape[2]), lambda bi, i: (bi, 0, 0)),
                  pl.BlockSpec((1, ih, iqt.shape[2], Q_BLOCK), lambda bi, i: (bi, 0, 0, i)),
                  pl.BlockSpec((1, ih, 1, Q_BLOCK), lambda bi, i: (bi, 0, 0, i)),
                  pl.BlockSpec((1, h, s, dh), lambda bi, i: (bi, 0, 0, 0)),
                  pl.BlockSpec((1, h, dh, Q_BLOCK), lambda bi, i: (bi, 0, 0, i)),
                  pl.BlockSpec((1, h, dh, s), lambda bi, i: (bi, 0, 0, 0)),
                  pl.BlockSpec(bias_tiles.shape, lambda bi, i: (0, 0, 0, 0))],
        out_specs=pl.BlockSpec((1, h, dh, Q_BLOCK), lambda bi, i: (bi, 0, 0, i)),
        scratch_shapes=[pltpu.VMEM((s, Q_BLOCK), F32)],
    )
    return pl.pallas_call(
        kern,
        grid_spec=grid_spec,
        out_shape=jax.ShapeDtypeStruct((b, h, dh, s), BF16),
        compiler_params=_cparams("parallel", "arbitrary"),
        name="dsa_prompt",
    )(ik, iqt, wts, k, qt, vt, bias_tiles)


def _dsa_sample_body(pt_ref, iq_ref, wt_ref, iknew_ref, qbd_ref, knew_ref, vnew_ref,
                     blast_ref, bnew_ref, cidx_hbm, ck_hbm, cv_hbm, o_ref,
                     ikbuf, kbuf, vbuf, sem_ik, sem_k, sem_v,
                     *, layer, n_pages, chunk, n_heads, n_idx_heads, t_new, topk):
    b = pl.program_id(0)
    past = n_pages * PAGE_SIZE
    n_chunks = n_pages // chunk
    rows = n_heads * t_new
    ck = chunk * PAGE_SIZE

    def page(p):
        return pt_ref[b * n_pages + p]

    def ik_copy(p):
        return pltpu.make_async_copy(cidx_hbm.at[layer, page(p)], ikbuf.at[p], sem_ik)

    def kv_copies(c, j):
        slot = c % 2
        p = c * chunk + j
        return (pltpu.make_async_copy(ck_hbm.at[layer, page(p)], kbuf.at[slot, j], sem_k.at[slot]),
                pltpu.make_async_copy(cv_hbm.at[layer, page(p)], vbuf.at[slot, j], sem_v.at[slot]))

    def start_chunk(c):
        for j in range(chunk):
            kc, vc = kv_copies(c, j)
            kc.start()
            vc.start()

    def wait_chunk(c):
        for j in range(chunk):
            kc, vc = kv_copies(c, j)
            kc.wait()
            vc.wait()

    def ik_start(p, carry):
        ik_copy(p).start()
        return carry

    def ik_wait(p, carry):
        ik_copy(p).wait()
        return carry

    lax.fori_loop(0, n_pages, ik_start, 0)
    start_chunk(0)
    ikbuf[n_pages] = iknew_ref[0]
    lax.fori_loop(0, n_pages, ik_wait, 0)

    n_keys = past + PAGE_SIZE
    ikt_all = jnp.concatenate([ikbuf[p] for p in range(n_pages + 1)], axis=1)
    dots = _mm(iq_ref[0], ikt_all)
    weighted = jnp.maximum(dots, 0.0) * wt_ref[0]
    scores = jnp.sum(weighted.reshape(n_idx_heads, t_new, n_keys), axis=0)
    qpos = past + lax.broadcasted_iota(I32, (t_new, n_keys), 0)
    kpos = lax.broadcasted_iota(I32, (t_new, n_keys), 1)
    valid = kpos <= qpos
    key = jnp.where(valid, _sortable_key(scores), INT_MIN)
    sel = valid & _topk_select(key, topk, int(math.log2(n_keys)) + 1)
    sel_rows = jnp.tile(sel, (n_heads, 1))

    q_rep = jnp.tile(qbd_ref[0], (n_heads, 1))
    row_head = lax.broadcasted_iota(I32, q_rep.shape, 0) // t_new
    col_head = lax.broadcasted_iota(I32, q_rep.shape, 1) // HEAD_DIM
    qbd = jnp.where(row_head == col_head, q_rep, jnp.zeros_like(q_rep))
    neg = -1e30

    def update(state, logits, selc, vt_bf16):
        m, l, acc = state
        s = jnp.where(selc, logits, -jnp.inf)
        m_new = jnp.maximum(m, jnp.max(s, axis=-1, keepdims=True))
        alpha = jnp.exp(m - m_new)
        p = jnp.exp(s - m_new)
        l = alpha * l + jnp.sum(p, axis=-1, keepdims=True)
        acc = alpha * acc + _mm_nt(p, vt_bf16)
        return m_new, l, acc

    def pages_t(buf, slot):
        return jnp.concatenate([buf[slot, j] for j in range(chunk)], axis=1).astype(BF16)

    state = (jnp.full((rows, 1), neg, F32), jnp.zeros((rows, 1), F32),
             jnp.zeros((rows, qbd.shape[1]), F32))
    for c in range(n_chunks):
        if c + 1 < n_chunks:
            start_chunk(c + 1)
        wait_chunk(c)
        slot = c % 2
        logits = _mm(qbd, pages_t(kbuf, slot))
        if c == n_chunks - 1:
            logits = jnp.concatenate([logits[:, :ck - PAGE_SIZE],
                                      logits[:, ck - PAGE_SIZE:] + blast_ref[...]], axis=1)
        state = update(state, logits, sel_rows[:, c * ck:(c + 1) * ck], pages_t(vbuf, slot))
    logits = _mm(qbd, knew_ref[0]) + bnew_ref[...]
    m, l, acc = update(state, logits, sel_rows[:, past:], vnew_ref[0])
    out = jnp.where(row_head == col_head, acc / l, 0.0)
    o_ref[0] = jnp.sum(out.reshape(n_heads, t_new, out.shape[1]), axis=0)


def dsa_sample(page_table, iq_rows, wt_rows, ik_new_t, q_bd, k_new_t, v_new_t, bias_last,
               bias_new, cache_idx_kt, cache_kt, cache_vt, *, layer, n_heads, n_idx_heads, t_new, topk, chunk):
    db, n_pages = page_table.shape
    rows = n_heads * t_new
    width = q_bd.shape[2]
    idx_dim = cache_idx_kt.shape[2]
    kern = functools.partial(_dsa_sample_body, layer=layer, n_pages=n_pages, chunk=chunk, n_heads=n_heads,
                             n_idx_heads=n_idx_heads, t_new=t_new, topk=topk)
    per_b = lambda shape: pl.BlockSpec((1,) + shape, lambda bi, pt: (bi,) + (0,) * len(shape))
    const = lambda shape: pl.BlockSpec(shape, lambda bi, pt: (0,) * len(shape))
    any_spec = pl.BlockSpec(memory_space=pl.ANY)
    grid_spec = pltpu.PrefetchScalarGridSpec(
        num_scalar_prefetch=1,
        grid=(db,),
        in_specs=[per_b((n_idx_heads * t_new, idx_dim)), per_b((n_idx_heads * t_new, 1)),
                  per_b((idx_dim, PAGE_SIZE)),
                  per_b((t_new, width)), per_b((width, PAGE_SIZE)), per_b((width, PAGE_SIZE)),
                  const((rows, PAGE_SIZE)), const((rows, PAGE_SIZE)),
                  any_spec, any_spec, any_spec],
        out_specs=per_b((t_new, width)),
        scratch_shapes=[pltpu.VMEM((n_pages + 1, idx_dim, PAGE_SIZE), F32),
                        pltpu.VMEM((2, chunk, width, PAGE_SIZE), F32),
                        pltpu.VMEM((2, chunk, width, PAGE_SIZE), F32),
                        pltpu.SemaphoreType.DMA(()),
                        pltpu.SemaphoreType.DMA((2,)),
                        pltpu.SemaphoreType.DMA((2,))],
    )
    return pl.pallas_call(
        kern,
        grid_spec=grid_spec,
        out_shape=jax.ShapeDtypeStruct((db, t_new, width), F32),
        compiler_params=_cparams("arbitrary"),
        name="dsa_sample",
    )(page_table.reshape(-1), iq_rows, wt_rows, ik_new_t, q_bd, k_new_t, v_new_t, bias_last,
      bias_new, cache_idx_kt, cache_kt, cache_vt)


def _token_minor_cache(cache):
    l, pool, page = cache.shape[:3]
    nd = cache.ndim
    return cache.transpose((0, 1) + tuple(range(3, nd)) + (2,)).reshape(l, pool, -1, page)


def _dsa_sample_inputs(q, k_new, v_new, iq, ik_new, iw, bias_tiles):
    db, t, h, dh = q.shape
    ih = iq.shape[2]
    iq_rows = iq.transpose(0, 2, 1, 3).reshape(db, ih * t, -1).astype(BF16)
    wt_rows = (iw * (ih ** -0.5 * iq.shape[3] ** -0.5)).transpose(0, 2, 1).reshape(db, ih * t, 1)
    q_bd = (q * dh ** -0.5).reshape(db, t, h * dh)
    page_t = lambda x: jnp.pad(x.reshape(db, t, -1).transpose(0, 2, 1), ((0, 0), (0, 0), (0, PAGE_SIZE - t)))
    bias_new = bias_tiles[0, :, :t, :].reshape(h * t, Q_BLOCK)
    bias_last = bias_tiles[1, :, :t, :].reshape(h * t, Q_BLOCK)
    return (iq_rows, wt_rows, page_t(ik_new), q_bd.astype(BF16), page_t(k_new).astype(BF16),
            page_t(v_new).astype(BF16), bias_last, bias_new)


def _matmul_residual_body(x_ref, a_ref, w_ref, o_ref):
    o_ref[...] = x_ref[...] + jnp.dot(a_ref[...], w_ref[...], preferred_element_type=F32)


def matmul_residual(x, a, w, *, tm, tn):
    n, d = x.shape
    kd = a.shape[1]
    return pl.pallas_call(
        _matmul_residual_body,
        grid=(n // tm, d // tn),
        in_specs=[pl.BlockSpec((tm, tn), lambda i, j: (i, j)),
                  pl.BlockSpec((tm, kd), lambda i, j: (i, 0)),
                  pl.BlockSpec((kd, tn), lambda i, j: (0, j))],
        out_specs=pl.BlockSpec((tm, tn), lambda i, j: (i, j)),
        out_shape=jax.ShapeDtypeStruct((n, d), F32),
        compiler_params=_cparams("parallel", "parallel"),
        name="matmul_residual",
    )(x, a, w)


def _cross_attn_body(x_ref, g_ref, wq_ref, mk_ref, mv_ref, wo_ref, o_ref, *, groups, t_rows, n_heads, head_dim):
    x = x_ref[...]
    h = _rmsnorm(x, g_ref[...]).astype(BF16)
    q = jnp.dot(h, wq_ref[...], preferred_element_type=F32).astype(BF16)
    scale = head_dim ** -0.5
    outs = []
    for gi in range(groups):
        qg = q[gi * t_rows:(gi + 1) * t_rows]
        heads = []
        for hh in range(n_heads):
            sl = slice(hh * head_dim, (hh + 1) * head_dim)
            logits = _mm_nt(qg[:, sl], mk_ref[gi, :, sl]) * scale
            mx = jnp.max(logits, axis=-1, keepdims=True)
            e = jnp.exp(logits - mx)
            p = e / jnp.sum(e, axis=-1, keepdims=True)
            heads.append(jnp.dot(p.astype(BF16), mv_ref[gi, :, sl], preferred_element_type=F32))
        outs.append(jnp.concatenate(heads, axis=1))
    o = jnp.concatenate(outs, axis=0) if groups > 1 else outs[0]
    o_ref[...] = x + jnp.dot(o.astype(BF16), wo_ref[...], preferred_element_type=F32)


def cross_attn(x, g, wq, mk, mv, wo, *, groups, t_rows, seq_tiles, n_heads):
    n, d = x.shape
    xw = wq.shape[1]
    rows = groups * t_rows
    m = mk.shape[1]
    kern = functools.partial(_cross_attn_body, groups=groups, t_rows=t_rows, n_heads=n_heads,
                             head_dim=xw // n_heads)
    return pl.pallas_call(
        kern,
        grid=(n // rows,),
        in_specs=[pl.BlockSpec((rows, d), lambda i: (i, 0)),
                  pl.BlockSpec((1, d), lambda i: (0, 0)),
                  pl.BlockSpec((d, xw), lambda i: (0, 0)),
                  pl.BlockSpec((groups, m, xw), lambda i: (i // seq_tiles, 0, 0)),
                  pl.BlockSpec((groups, m, xw), lambda i: (i // seq_tiles, 0, 0)),
                  pl.BlockSpec((xw, d), lambda i: (0, 0))],
        out_specs=pl.BlockSpec((rows, d), lambda i: (i, 0)),
        out_shape=jax.ShapeDtypeStruct((n, d), F32),
        compiler_params=_cparams("parallel"),
        name="cross_attn",
    )(x, g.reshape(1, d), wq, mk, mv, wo)


def _router_body(x_ref, g_ref, wr_ref, br_ref, h_ref, r_ref, *, n_groups, per_group):
    h = _rmsnorm(x_ref[...], g_ref[...])
    h_ref[...] = h
    logits = _mm3(h, wr_ref[...]) + br_ref[...]
    lane = lax.broadcasted_iota(I32, logits.shape, 1).astype(F32)
    big = 1e9
    first_lane = lambda hit: jnp.min(jnp.where(hit, lane, big), axis=-1, keepdims=True)
    gl = jnp.where(lane < n_groups, logits, -jnp.inf)
    gmax = jnp.max(gl, axis=-1, keepdims=True)
    grp = first_lane(gl == gmax)
    p_grp = 1.0 / jnp.sum(jnp.exp(gl - gmax), axis=-1, keepdims=True)
    e_id = lane - n_groups
    in_grp = (e_id >= grp * per_group) & (e_id < (grp + 1.0) * per_group)
    el = jnp.where(in_grp, logits, -jnp.inf)
    v1 = jnp.max(el, axis=-1, keepdims=True)
    i1 = first_lane(el == v1) - n_groups
    el2 = jnp.where(e_id == i1, -jnp.inf, el)
    v2 = jnp.max(el2, axis=-1, keepdims=True)
    i2 = first_lane(el2 == v2) - n_groups
    e2 = jnp.exp(v2 - v1)
    g1 = p_grp / (1.0 + e2)
    g2 = p_grp * e2 / (1.0 + e2)
    r_ref[...] = jnp.where(lane == 0, g1, jnp.where(lane == 1, g2, jnp.where(
        lane == 2, i1, jnp.where(lane == 3, i2, 0.0))))


def router(x, g, w_r, b_r, *, tm, n_groups, per_group):
    n, d = x.shape
    kern = functools.partial(_router_body, n_groups=n_groups, per_group=per_group)
    return pl.pallas_call(
        kern,
        grid=(n // tm,),
        in_specs=[pl.BlockSpec((tm, d), lambda i: (i, 0)),
                  pl.BlockSpec((1, d), lambda i: (0, 0)),
                  pl.BlockSpec((d, LANES), lambda i: (0, 0)),
                  pl.BlockSpec((1, LANES), lambda i: (0, 0))],
        out_specs=[pl.BlockSpec((tm, d), lambda i: (i, 0)),
                   pl.BlockSpec((tm, LANES), lambda i: (i, 0))],
        out_shape=[jax.ShapeDtypeStruct((n, d), F32), jax.ShapeDtypeStruct((n, LANES), F32)],
        compiler_params=_cparams("parallel"),
        name="moe_router",
    )(x, g.reshape(1, d), w_r, b_r)


def _slab_rows(buf, base, n_rows, n_slab):
    return jnp.concatenate([buf[pl.ds(base * n_slab + s, n_rows, stride=n_slab), :] for s in range(n_slab)],
                           axis=1)


def _moe_ffn_body(te_ref, tv_ref, tok_ref, h_hbm, w1_ref, w3_ref, w2_ref, o_ref, xbuf, sem, *, tm):
    t = pl.program_id(0)
    n_tiles = pl.num_programs(0)
    n_slab = h_hbm.shape[1]
    slot = t % 2

    def row_copy(tile, sl, r):
        dst = xbuf.at[sl, pl.ds(pl.multiple_of(r * n_slab, n_slab), n_slab)]
        return pltpu.make_async_copy(h_hbm.at[tok_ref[tile * tm + r]], dst, sem.at[sl])

    def start_tile(tile, sl):
        def start(r, c):
            row_copy(tile, sl, r).start()
            return c
        lax.fori_loop(0, tm, start, 0)

    @pl.when((t == 0) & (tv_ref[0] != 0))
    def _():
        start_tile(0, 0)

    nxt = jnp.minimum(t + 1, n_tiles - 1)

    @pl.when((t + 1 < n_tiles) & (tv_ref[nxt] != 0))
    def _():
        start_tile(nxt, 1 - slot)

    @pl.when(tv_ref[t] != 0)
    def _():
        def wait(r, c):
            row_copy(t, slot, r).wait()
            return c

        lax.fori_loop(0, tm, wait, 0)
        x = _slab_rows(xbuf.at[slot], 0, tm, n_slab).astype(BF16)
        a = jnp.dot(x, w1_ref[0].astype(BF16), preferred_element_type=F32)
        bgate = jnp.dot(x, w3_ref[0].astype(BF16), preferred_element_type=F32)
        u = (a / (1.0 + jnp.exp(-a))) * bgate
        o_ref[...] = jnp.dot(u.astype(BF16), w2_ref[0].astype(BF16), preferred_element_type=F32)

    @pl.when(tv_ref[t] == 0)
    def _():
        o_ref[...] = jnp.zeros(o_ref.shape, F32)


def moe_ffn(tile_expert, tile_valid, row_tok, h, w1, w3, w2, *, tm):
    n_tiles = tile_expert.shape[0]
    d = h.shape[1]
    de = w1.shape[2]
    h = h.reshape(h.shape[0], d // LANES, LANES)
    grid_spec = pltpu.PrefetchScalarGridSpec(
        num_scalar_prefetch=3,
        grid=(n_tiles,),
        in_specs=[pl.BlockSpec(memory_space=pl.ANY),
                  pl.BlockSpec((1, d, de), lambda t, te, tv, tok: (te[t], 0, 0)),
                  pl.BlockSpec((1, d, de), lambda t, te, tv, tok: (te[t], 0, 0)),
                  pl.BlockSpec((1, de, d), lambda t, te, tv, tok: (te[t], 0, 0))],
        out_specs=pl.BlockSpec((tm, d), lambda t, te, tv, tok: (t, 0)),
        scratch_shapes=[pltpu.VMEM((2, tm * (d // LANES), LANES), F32), pltpu.SemaphoreType.DMA((2,))],
    )
    return pl.pallas_call(
        functools.partial(_moe_ffn_body, tm=tm),
        grid_spec=grid_spec,
        out_shape=jax.ShapeDtypeStruct((n_tiles * tm, d), F32),
        compiler_params=_cparams("arbitrary"),
        name="moe_ffn",
    )(tile_expert, tile_valid, row_tok, h, w1, w3, w2)


def _moe_dispatch(eids, n_experts, tm):
    n, k = eids.shape
    m = n * k
    flat_e = eids.reshape(-1)
    order = jnp.argsort(flat_e, stable=True).astype(I32)
    inv = jnp.argsort(order).astype(I32)
    counts = jnp.sum((flat_e[:, None] == jnp.arange(n_experts)[None, :]).astype(I32), axis=0)
    padded = (counts + tm - 1) // tm * tm
    pad_end = jnp.cumsum(padded)
    pad_start = pad_end - padded
    start = jnp.cumsum(counts) - counts
    pos = (pad_start[flat_e] + inv - start[flat_e]).astype(I32)
    n_tiles = -(-m // tm) + n_experts
    tile_start = jnp.arange(n_tiles) * tm
    tile_valid = (tile_start < pad_end[-1]).astype(I32)
    last = jnp.maximum(pad_end[-1] - 1, 0)
    tile_expert = jnp.minimum(jnp.searchsorted(pad_end, jnp.minimum(tile_start, last), side='right'),
                              n_experts - 1).astype(I32)
    row_e = jnp.repeat(tile_expert, tm)
    rank = jnp.arange(n_tiles * tm) - pad_start[row_e]
    src = jnp.clip(start[row_e] + rank, 0, m - 1)
    row_tok = jnp.where(rank < counts[row_e], order[src] // k, 0).astype(I32)
    return tile_expert, tile_valid, row_tok, pos


def _combine_body(pos_ref, x_ref, r_ref, g_ref, y_hbm, o_ref, ybuf, sem, *, tm, top_k):
    i = pl.program_id(0)
    n_tiles = pl.num_programs(0)
    n_slab = y_hbm.shape[1]
    slot = i % 2

    def row_copy(tile, sl, j):
        tok = j // top_k
        kk = j % top_k
        dst = ybuf.at[sl, pl.ds(pl.multiple_of((kk * tm + tok) * n_slab, n_slab), n_slab)]
        return pltpu.make_async_copy(y_hbm.at[pos_ref[tile * tm * top_k + j]], dst, sem.at[sl])

    def start_tile(tile, sl):
        def start(j, c):
            row_copy(tile, sl, j).start()
            return c
        lax.fori_loop(0, tm * top_k, start, 0)

    @pl.when(i == 0)
    def _():
        start_tile(0, 0)

    @pl.when(i + 1 < n_tiles)
    def _():
        start_tile(i + 1, 1 - slot)

    def wait(j, c):
        row_copy(i, slot, j).wait()
        return c

    lax.fori_loop(0, tm * top_k, wait, 0)
    route = r_ref[...]
    x = x_ref[...]
    for kk in range(top_k):
        x = x + _slab_rows(ybuf.at[slot], kk * tm, tm, n_slab) * route[:, kk:kk + 1]
    o_ref[...] = _rmsnorm(x, g_ref[...])


def moe_combine(pos, x, route, g, y, *, tm, top_k):
    n, d = x.shape
    y = y.reshape(y.shape[0], d // LANES, LANES)
    grid_spec = pltpu.PrefetchScalarGridSpec(
        num_scalar_prefetch=1,
        grid=(n // tm,),
        in_specs=[pl.BlockSpec((tm, d), lambda i, p: (i, 0)),
                  pl.BlockSpec((tm, LANES), lambda i, p: (i, 0)),
                  pl.BlockSpec((1, d), lambda i, p: (0, 0)),
                  pl.BlockSpec(memory_space=pl.ANY)],
        out_specs=pl.BlockSpec((tm, d), lambda i, p: (i, 0)),
        scratch_shapes=[pltpu.VMEM((2, top_k * tm * (d // LANES), LANES), F32), pltpu.SemaphoreType.DMA((2,))],
    )
    return pl.pallas_call(
        functools.partial(_combine_body, tm=tm, top_k=top_k),
        grid_spec=grid_spec,
        out_shape=jax.ShapeDtypeStruct((n, d), F32),
        compiler_params=_cparams("arbitrary"),
        name="moe_combine",
    )(pos, x, route, g.reshape(1, d), y)


def kernel(x_prompt, x_sample, mem_prompt, cache_k, cache_v, cache_idx_k, page_table, state_wkv, state_shift, cache_mem_k, cache_mem_v, g_mix, w_in, mu_shift, rw_w0, rw_w2, rw_a0, rw_a2, rw_g2, rw_kk, rw_ka, rw_rk, rw_ln_g, rw_ln_b, w_out, g_cross, g_mem, w_cq, w_ck, w_cv, w_co, g_ffn, w_rg, b_rg, w_re, b_re, w_e1, w_e3, w_e2, rel_bias, g_final):
    B, S, D = x_prompt.shape
    DB, T, _ = x_sample.shape
    assert w_in.shape[0] == 1, "single-layer trunk only"
    l = 0
    n_pages = page_table.shape[1]
    past = n_pages * PAGE_SIZE
    topk_p = min(TOPK_MAX, S // 4)
    topk_s = min(TOPK_MAX, (past + T) // 4)
    rw_proj = mu_shift.shape[1]
    width = rw_w0.shape[1]
    at_w = D - width
    n_heads = at_w // HEAD_DIM
    idx_dim = cache_idx_k.shape[-1]
    ih = (w_in.shape[2] - rw_proj - 3 * at_w - idx_dim) // (idx_dim + 1)
    xw = w_cq.shape[2]
    x_heads = cache_mem_k.shape[3]
    n_mem = mem_prompt.shape[1]
    n_experts = w_e1.shape[1]
    top_k = 2
    proj_pad = -(-w_in.shape[2] // 512) * 512
    o = rw_proj

    w_all = jnp.pad(w_in[l].T, ((0, proj_pad - w_in.shape[2]), (0, 0))).astype(BF16)
    w_out_b = w_out[l].astype(BF16)
    w_cq_b, w_co_b = w_cq[l].astype(BF16), w_co[l].astype(BF16)
    w_ckv = jnp.concatenate([w_ck[l], w_cv[l]], axis=1).astype(BF16)
    n_route = w_rg.shape[2] + w_re.shape[2]
    w_r = jnp.pad(jnp.concatenate([w_rg[l], w_re[l]], axis=1), ((0, 0), (0, LANES - n_route)))
    b_r = jnp.pad(jnp.concatenate([b_rg[l], b_re[l]]), (0, LANES - n_route)).reshape(1, LANES)
    rw_args = (mu_shift[l], rw_w0[l], rw_w2[l], rw_a0[l], rw_a2[l], rw_g2[l], rw_kk[l], rw_ka[l],
               rw_rk[l].reshape(-1))
    tiles = bias_tables(rel_bias, ((0, False), (Q_BLOCK, False), (0, True), (Q_BLOCK, True)))

    def split_attn(f3):
        q = f3[..., o:o + at_w]
        k = f3[..., o + at_w:o + 2 * at_w]
        v = f3[..., o + 2 * at_w:o + 3 * at_w]
        iq = f3[..., o + 3 * at_w:o + 3 * at_w + ih * idx_dim]
        ik = f3[..., o + 3 * at_w + ih * idx_dim:o + 3 * at_w + (ih + 1) * idx_dim]
        iw = f3[..., o + 3 * at_w + (ih + 1) * idx_dim:o + 3 * at_w + (ih + 1) * idx_dim + ih]
        return q, k, v, iq, ik, iw

    def rw_rows(y):
        b_, p_, t_, _ = y.shape
        return y.transpose(0, 2, 1, 3).reshape(b_ * t_, p_ * PAIR).astype(BF16)

    xp = x_prompt.reshape(B * S, D)
    fp = norm_matmul(xp, g_mix[l], w_all, tm=1024, tn=512, w_is_transposed=True).reshape(B, S, proj_pad)
    feats_p = rwkv_prep(fp, jnp.zeros((B, rw_proj), F32), *rw_args, tm=256, rw_proj=rw_proj, width=width)
    rw_p, st_p = rwkv_chunk(feats_p, jnp.zeros((B, width // PAIR, PAIR, PAIR), F32), rw_ln_g[l], rw_ln_b[l], L=64)
    q, k_p, v_p, iq, ik_p, iw = split_attn(fp)
    hm = lambda z, h_: z.reshape(B, S, h_, -1).transpose(0, 2, 1, 3).astype(BF16)
    hmt = lambda z, h_: z.reshape(B, S, h_, -1).transpose(0, 2, 3, 1).astype(BF16)
    at_p = dsa_prompt(ik_p.astype(BF16), hmt(iq, ih),
                      (iw * (ih ** -0.5 * idx_dim ** -0.5)).transpose(0, 2, 1)[:, :, None, :],
                      hm(k_p, n_heads), hmt(q * HEAD_DIM ** -0.5, n_heads), hmt(v_p, n_heads),
                      tiles[2:4], topk=topk_p)
    mix_p = jnp.concatenate([rw_rows(rw_p), at_p.transpose(0, 3, 1, 2).reshape(B * S, at_w)], axis=1)
    x1_p = matmul_residual(xp, mix_p, w_out_b, tm=512, tn=512)
    mkv = norm_matmul(mem_prompt.reshape(B * n_mem, D), g_mem[l], w_ckv, tm=256, tn=512)
    mk_p = mkv[:, :xw].reshape(B, n_mem, xw)
    mv_p = mkv[:, xw:].reshape(B, n_mem, xw)
    x2_p = cross_attn(x1_p, g_cross[l], w_cq_b, mk_p.astype(BF16), mv_p.astype(BF16), w_co_b,
                      groups=1, t_rows=512, seq_tiles=S // 512, n_heads=x_heads)

    xs = x_sample.reshape(DB * T, D)
    fs = norm_matmul(xs, g_mix[l], w_all, tm=DB * T, tn=512, w_is_transposed=True).reshape(DB, T, proj_pad)
    feats_s = rwkv_prep(fs, state_shift[l], *rw_args, tm=T, rw_proj=rw_proj, width=width)
    rw_s, st_s = rwkv_chunk(feats_s, _state_to_block_diag(state_wkv[l]), rw_ln_g[l], rw_ln_b[l], L=T)
    q2, k_s, v_s, iq2, ik_s, iw2 = split_attn(fs)
    r4 = lambda z, h_: z.reshape(DB, T, h_, -1)
    s_args = _dsa_sample_inputs(r4(q2, n_heads), r4(k_s, n_heads), r4(v_s, n_heads), r4(iq2, ih), ik_s, iw2,
                                tiles[0:2])
    at_s = dsa_sample(page_table, *s_args, _token_minor_cache(cache_idx_k), _token_minor_cache(cache_k),
                      _token_minor_cache(cache_v),
                      layer=l, n_heads=n_heads, n_idx_heads=ih, t_new=T, topk=topk_s, chunk=8)
    mix_s = jnp.concatenate([rw_rows(rw_s), at_s.reshape(DB * T, at_w).astype(BF16)], axis=1)
    x1_s = matmul_residual(xs, mix_s, w_out_b, tm=DB * T, tn=512)
    x2_s = cross_attn(x1_s, g_cross[l], w_cq_b, cache_mem_k[l].reshape(DB, n_mem, xw).astype(BF16),
                      cache_mem_v[l].reshape(DB, n_mem, xw).astype(BF16), w_co_b,
                      groups=8, t_rows=T, seq_tiles=1, n_heads=x_heads)

    h_p, route_p = router(x2_p, g_ffn[l], w_r, b_r, tm=512, n_groups=w_rg.shape[2],
                          per_group=w_re.shape[2] // w_rg.shape[2])
    h_s, route_s = router(x2_s, g_ffn[l], w_r, b_r, tm=DB * T, n_groups=w_rg.shape[2],
                          per_group=w_re.shape[2] // w_rg.shape[2])
    h_all = jnp.concatenate([h_p, h_s], axis=0)
    eids = jnp.concatenate([route_p[:, top_k:2 * top_k], route_s[:, top_k:2 * top_k]], axis=0).astype(I32)
    tile_expert, tile_valid, row_tok, pos = _moe_dispatch(eids, n_experts, 256)
    y_rows = moe_ffn(tile_expert, tile_valid, row_tok, h_all, w_e1[l], w_e3[l], w_e2[l], tm=256)
    n_p = B * S
    y_p = moe_combine(pos[:n_p * top_k], x2_p, route_p, g_final, y_rows, tm=256, top_k=top_k)
    y_s = moe_combine(pos[n_p * top_k:], x2_s, route_s, g_final, y_rows, tm=DB * T, top_k=top_k)

    hd = lambda z, b_, t_: z.reshape(1, b_, t_, n_heads, HEAD_DIM)
    return (y_p.reshape(B, S, D), y_s.reshape(DB, T, D),
            hd(k_p, B, S), hd(v_p, B, S), ik_p[None], _state_from_block_diag(st_p)[None], fp[:, -1, :rw_proj][None],
            mk_p.reshape(1, B, n_mem, x_heads, xw // x_heads), mv_p.reshape(1, B, n_mem, x_heads, xw // x_heads),
            hd(k_s, DB, T), hd(v_s, DB, T), ik_s[None], _state_from_block_diag(st_s)[None],
            fs[:, -1, :rw_proj][None])
```

```python
import functools
import math

import jax
import jax.numpy as jnp
from jax import lax
from jax.experimental import pallas as pl
from jax.experimental.pallas import tpu as pltpu

F32 = jnp.float32
BF16 = jnp.bfloat16
I32 = jnp.int32

LANES = 128
SUBLANES = 8
VMEM_LIMIT_BYTES = 56 * 1024 * 1024

HEAD_DIM = 64
PAIR = 2 * HEAD_DIM
GN_EPS = 64e-5
NORM_EPS = 1e-6
TOPK_MAX = 256
Q_BLOCK = 128
N_BUCKETS = 32
MAX_DISTANCE = 128
PAGE_SIZE = 128
N_GROUPS = 4
EXPERTS_PER_GROUP = 8
INT_MIN = -(2 ** 31)


def _cparams(*sem):
    return pltpu.CompilerParams(dimension_semantics=sem, vmem_limit_bytes=VMEM_LIMIT_BYTES)


def _mm(a, b):
    return jnp.dot(a.astype(BF16), b.astype(BF16), preferred_element_type=F32)


def _mm_nt(a, b):
    return lax.dot_general(a.astype(BF16), b.astype(BF16), (((1,), (1,)), ((), ())),
                           preferred_element_type=F32)


def _split2(x):
    hi = x.astype(BF16)
    lo = (x - hi.astype(F32)).astype(BF16)
    return hi, lo


def _mm3(a, b):
    ah, al = _split2(a)
    bh, bl = _split2(b)
    d = lambda x, y: jnp.dot(x, y, preferred_element_type=F32)
    return d(ah, bh) + (d(ah, bl) + d(al, bh))


def _mm_exact_rhs(a, b_bf16):
    hi = a.astype(BF16)
    r1 = a - hi.astype(F32)
    mid = r1.astype(BF16)
    lo = (r1 - mid.astype(F32)).astype(BF16)
    d = lambda x: jnp.dot(x, b_bf16, preferred_element_type=F32)
    return d(hi) + (d(mid) + d(lo))


def _rmsnorm(x, g):
    ms = jnp.mean(x * x, axis=-1, keepdims=True)
    return x * lax.rsqrt(ms + NORM_EPS) * g


def _norm_matmul_body(x_ref, g_ref, w_ref, o_ref, xn_ref, *, w_is_transposed):
    @pl.when(pl.program_id(1) == 0)
    def _():
        xn_ref[...] = _rmsnorm(x_ref[...], g_ref[...]).astype(BF16)

    mm = _mm_nt if w_is_transposed else _mm
    o_ref[...] = mm(xn_ref[...], w_ref[...])


def norm_matmul(x, g, w, *, tm, tn, w_is_transposed=False):
    n, d = x.shape
    m = w.shape[0] if w_is_transposed else w.shape[1]
    w_spec = (pl.BlockSpec((tn, d), lambda i, j: (j, 0)) if w_is_transposed
              else pl.BlockSpec((d, tn), lambda i, j: (0, j)))
    return pl.pallas_call(
        functools.partial(_norm_matmul_body, w_is_transposed=w_is_transposed),
        grid=(n // tm, m // tn),
        in_specs=[pl.BlockSpec((tm, d), lambda i, j: (i, 0)),
                  pl.BlockSpec((1, d), lambda i, j: (0, 0)),
                  w_spec],
        out_specs=pl.BlockSpec((tm, tn), lambda i, j: (i, j)),
        out_shape=jax.ShapeDtypeStruct((n, m), F32),
        scratch_shapes=[pltpu.VMEM((tm, d), BF16)],
        compiler_params=_cparams("parallel", "arbitrary"),
        name="norm_matmul",
    )(x, g.reshape(1, d), w)


def _pair_ones():
    r = lax.broadcasted_iota(I32, (PAIR, PAIR), 0) // HEAD_DIM
    c = lax.broadcasted_iota(I32, (PAIR, PAIR), 1) // HEAD_DIM
    return (r == c).astype(BF16)


def _head_sum(x, ones_bd):
    return _mm_exact_rhs(x, ones_bd)


def _rwkv_prep_body(f_ref, prev8_ref, init_ref, mu_ref, w0_ref, w2_ref, a0_ref, a2_ref, g2_ref,
                    kk_ref, ka_ref, rk_ref,
                    r_o, k_o, v_o, kk_o, b_o, ld_o, g_o, bon_o, *, tm, width):
    i = pl.program_id(1)
    f = f_ref[0]
    prev_row = jnp.where(i == 0, init_ref[0], prev8_ref[0, SUBLANES - 1:SUBLANES, :])
    rolled = pltpu.roll(f, shift=1, axis=0)
    row = lax.broadcasted_iota(I32, f.shape, 0)
    f_prev = jnp.where(row == 0, prev_row, rolled)
    fs = f + (f_prev - f) * mu_ref[...]
    w_ = width
    r = fs[:, 0:w_]
    k = fs[:, w_:2 * w_]
    v = fs[:, 2 * w_:3 * w_]
    o = 3 * w_
    n_dec = w2_ref.shape[0]
    n_icl = a2_ref.shape[0]
    wd = fs[:, o:o + n_dec]
    ad = fs[:, o + n_dec:o + n_dec + n_icl]
    gd = fs[:, o + n_dec + n_icl:]
    z = w0_ref[...] + _mm3(jnp.tanh(wd), w2_ref[...])
    nz = -z
    softplus = jnp.maximum(nz, 0.0) + jnp.log(1.0 + jnp.exp(-jnp.abs(nz)))
    w = -softplus - 0.5
    ld = -jnp.exp(w)
    a = 1.0 / (1.0 + jnp.exp(-(a0_ref[...] + _mm3(ad, a2_ref[...]))))
    g = _mm3(1.0 / (1.0 + jnp.exp(-gd)), g2_ref[...])
    kk = k * kk_ref[...]
    k2 = k * (1.0 + (a - 1.0) * ka_ref[...])
    rk = r * k2 * rk_ref[...]
    ones_bd = _pair_ones()
    for p in range(w_ // PAIR):
        sl = slice(p * PAIR, (p + 1) * PAIR)
        kkp = kk[:, sl]
        nrm = jnp.sqrt(_head_sum(kkp * kkp, ones_bd))
        kkp = kkp / jnp.maximum(nrm, 1e-12)
        ap = a[:, sl]
        r_o[0, p] = r[:, sl]
        k_o[0, p] = k2[:, sl]
        v_o[0, p] = v[:, sl]
        kk_o[0, p] = kkp
        b_o[0, p] = kkp * ap
        ld_o[0, p] = ld[:, sl]
        g_o[0, p] = g[:, sl]
        bon_o[0, p] = _head_sum(rk[:, sl], ones_bd) * v[:, sl]


def rwkv_prep(f3, init_prev, mu, w0, w2, a0, a2, g2, k_k, k_a, r_k, *, tm, rw_proj, width):
    b, t, _ = f3.shape
    npair = width // PAIR
    row1 = lambda x: x.reshape(1, -1)
    kern = functools.partial(_rwkv_prep_body, tm=tm, width=width)
    full = lambda a: pl.BlockSpec(a.shape, lambda bi, i: (0,) * a.ndim)
    args = [row1(mu), row1(w0), w2, row1(a0), a2, g2, row1(k_k), row1(k_a), row1(r_k)]
    out_spec = pl.BlockSpec((1, npair, tm, PAIR), lambda bi, i: (bi, 0, i, 0))
    out_shape = jax.ShapeDtypeStruct((b, npair, t, PAIR), F32)
    return pl.pallas_call(
        kern,
        grid=(b, t // tm),
        in_specs=[pl.BlockSpec((1, tm, rw_proj), lambda bi, i: (bi, i, 0)),
                  pl.BlockSpec((1, SUBLANES, rw_proj),
                               lambda bi, i: (bi, jnp.maximum(i * (tm // SUBLANES) - 1, 0), 0)),
                  pl.BlockSpec((1, 1, rw_proj), lambda bi, i: (bi, 0, 0))] + [full(a) for a in args],
        out_specs=[out_spec] * 8,
        out_shape=[out_shape] * 8,
        compiler_params=_cparams("parallel", "parallel"),
        name="rwkv_prep",
    )(f3, f3, init_prev.reshape(b, 1, rw_proj), *args)


def _rwkv_chunk_body(r_ref, k_ref, v_ref, kk_ref, b_ref, ld_ref, g_ref, bon_ref, s0_ref,
                     lng_ref, lnb_ref, o_ref, st_ref, s_ref, *, L, npair, group):
    c = pl.program_id(1)

    @pl.when(c == 0)
    def _():
        s_ref[...] = s0_ref[0]

    L2 = 2 * L
    row = lax.broadcasted_iota(I32, (L2, L2), 0)
    col = lax.broadcasted_iota(I32, (L2, L2), 1)
    same = (row // L) == (col // L)
    tri_strict = same & (col < row)
    tri_incl = same & (col <= row)
    eye = (row == col).astype(F32)
    tr = lax.broadcasted_iota(I32, (L, L), 0)
    tc = lax.broadcasted_iota(I32, (L, L), 1)
    cum_mat = (tc <= tr).astype(BF16)
    lane = lax.broadcasted_iota(I32, (L, PAIR), 1)
    first = lane < HEAD_DIM
    ones_bd = _pair_ones()
    n_sq = max(int(math.ceil(math.log2(L))) - 1, 0)

    def block_diag(x):
        return jnp.concatenate([jnp.where(first, x, 0.0), jnp.where(first, 0.0, x)], axis=0)

    def group_step(gi, carry):
        ps = [gi * group + j for j in range(group)]
        each = lambda f, *cols: [f(*args) for args in zip(*cols)]
        ld = [ld_ref[0, p] for p in ps]
        cum = each(lambda x: _mm_exact_rhs_t(cum_mat, x), ld)
        dec = each(jnp.exp, cum)
        dec_inv = each(lambda c_: jnp.exp(-c_), cum)
        a_t = each(lambda p, c_, l_: block_diag(-kk_ref[0, p] * jnp.exp(c_ - l_)), ps, cum, ld)
        b_t = each(lambda p, e: block_diag(b_ref[0, p] * e), ps, dec_inv)
        k_t = each(lambda p, e: block_diag(k_ref[0, p] * e), ps, dec_inv)
        r_t = each(lambda p, e: block_diag(r_ref[0, p] * e), ps, dec)
        v_b = each(lambda p: block_diag(v_ref[0, p]), ps)
        a_ab = each(lambda a, b: jnp.where(tri_strict, _mm_nt(a, b), 0.0), a_t, b_t)
        a_ak = each(lambda a, k: jnp.where(tri_strict, _mm_nt(a, k), 0.0), a_t, k_t)
        a_rb = each(lambda r, b: jnp.where(tri_incl, _mm_nt(r, b), 0.0), r_t, b_t)
        a_rk = each(lambda r, k: jnp.where(tri_incl, _mm_nt(r, k), 0.0), r_t, k_t)
        x = a_ab
        t_inv = each(lambda a: eye + a, a_ab)
        for _ in range(n_sq):
            x = each(lambda x_: _mm(x_, x_), x)
            t_inv = each(lambda t, x_: t + _mm(x_, t), t_inv, x)
        s = [s_ref[p] for p in ps]
        rhs = each(lambda a, s_, ak, v: _mm_nt(a, s_) + _mm(ak, v), a_t, s, a_ak, v_b)
        u = each(_mm, t_inv, rhs)
        y_b = each(lambda r, s_, rb, u_, rk, v: _mm_nt(r, s_) + _mm(rb, u_) + _mm(rk, v),
                   r_t, s, a_rb, u, a_rk, v_b)
        s_new = each(lambda s_, u_, b, v, k, d: (s_ + _mm(u_.T, b) + _mm(v.T, k)) * d[L - 1:L, :],
                     s, u, b_t, v_b, k_t, dec)
        for p, sn in zip(ps, s_new):
            s_ref[p] = sn
        y = each(lambda yb: yb[:L] + yb[L:], y_b)
        mean = each(lambda y_: _head_sum(y_, ones_bd) * (1.0 / HEAD_DIM), y)
        d = each(lambda y_, m: y_ - m, y, mean)
        var = each(lambda d_: _head_sum(d_ * d_, ones_bd) * (1.0 / HEAD_DIM), d)
        for p, d_, v_ in zip(ps, d, var):
            yn = d_ * lax.rsqrt(v_ + GN_EPS) * lng_ref[p] + lnb_ref[p]
            o_ref[0, p] = (yn + bon_ref[0, p]) * g_ref[0, p]
        return carry

    lax.fori_loop(0, npair // group, group_step, 0)

    @pl.when(c == pl.num_programs(1) - 1)
    def _():
        st_ref[0] = s_ref[...]


def _mm_exact_rhs_t(m_bf16, x):
    hi = x.astype(BF16)
    r1 = x - hi.astype(F32)
    mid = r1.astype(BF16)
    lo = (r1 - mid.astype(F32)).astype(BF16)
    d = lambda y: jnp.dot(m_bf16, y, preferred_element_type=F32)
    return d(hi) + (d(mid) + d(lo))


def rwkv_chunk(feats, s0_bd, ln_g, ln_b, *, L, group=8):
    b, npair, t, _ = feats[0].shape
    blk = pl.BlockSpec((1, npair, L, PAIR), lambda bi, c: (bi, 0, c, 0))
    st_spec = pl.BlockSpec((1, npair, PAIR, PAIR), lambda bi, c: (bi, 0, 0, 0))
    par_spec = pl.BlockSpec((npair, 1, PAIR), lambda bi, c: (0, 0, 0))
    kern = functools.partial(_rwkv_chunk_body, L=L, npair=npair, group=group)
    return pl.pallas_call(
        kern,
        grid=(b, t // L),
        in_specs=[blk] * 8 + [st_spec, par_spec, par_spec],
        out_specs=[blk, st_spec],
        out_shape=[jax.ShapeDtypeStruct((b, npair, t, PAIR), F32),
                   jax.ShapeDtypeStruct((b, npair, PAIR, PAIR), F32)],
        scratch_shapes=[pltpu.VMEM((npair, PAIR, PAIR), F32)],
        compiler_params=_cparams("parallel", "arbitrary"),
        name="rwkv_chunk",
    )(*feats, s0_bd, ln_g.reshape(npair, 1, PAIR), ln_b.reshape(npair, 1, PAIR))


def _state_to_block_diag(s):
    b, h, n, _ = s.shape
    s = s.reshape(b, h // 2, 2, n, n)
    z = jnp.zeros_like(s[:, :, 0])
    top = jnp.concatenate([s[:, :, 0], z], axis=-1)
    bot = jnp.concatenate([z, s[:, :, 1]], axis=-1)
    return jnp.concatenate([top, bot], axis=-2)


def _state_from_block_diag(s_bd):
    b, p, _, _ = s_bd.shape
    n = HEAD_DIM
    return jnp.stack([s_bd[:, :, :n, :n], s_bd[:, :, n:, n:]], axis=2).reshape(b, 2 * p, n, n)


def _t5_bucket(dist):
    exact = N_BUCKETS // 2
    d = jnp.maximum(dist, 0)
    far = exact + (jnp.log(jnp.maximum(d, 1).astype(F32) / exact) / math.log(MAX_DISTANCE / exact)
                   * (N_BUCKETS - exact)).astype(I32)
    return jnp.where(d < exact, d, jnp.minimum(far, N_BUCKETS - 1))


def _bias_tables_body(rb_ref, o_ref, *, offsets, n_heads):
    r = lax.broadcasted_iota(I32, (Q_BLOCK, Q_BLOCK), 0)
    c = lax.broadcasted_iota(I32, (Q_BLOCK, Q_BLOCK), 1)
    for t, (off, key_major) in enumerate(offsets):
        bucket = _t5_bucket((c - r if key_major else r - c) + off)
        for h in range(n_heads):
            def body(bk, acc):
                return jnp.where(bucket == bk, rb_ref[bk, h], acc)
            tile = lax.fori_loop(0, N_BUCKETS, body, jnp.zeros((Q_BLOCK, Q_BLOCK), F32))
            o_ref[t, h] = tile - rb_ref[N_BUCKETS - 1, h]


def bias_tables(rel_bias, offsets):
    n_heads = rel_bias.shape[1]
    kern = functools.partial(_bias_tables_body, offsets=tuple(offsets), n_heads=n_heads)
    return pl.pallas_call(
        kern,
        in_specs=[pl.BlockSpec(memory_space=pltpu.SMEM)],
        out_specs=pl.BlockSpec(memory_space=pltpu.VMEM),
        out_shape=jax.ShapeDtypeStruct((len(offsets), n_heads, Q_BLOCK, Q_BLOCK), F32),
        name="bias_tables",
    )(rel_bias)


def _sortable_key(scores):
    bits = lax.bitcast_convert_type(scores + 0.0, I32)
    return jnp.where(bits < 0, bits ^ 0x7FFFFFFF, bits)


def _count(mask):
    return jnp.sum(mask.astype(F32), axis=-1, keepdims=True)


def _topk_select(key, topk, n_index_bits):
    rows, n = key.shape
    kf = float(topk)
    t0 = jnp.where(_count(key >= 0) >= kf, 0, INT_MIN).astype(I32)

    def value_bit(i, t):
        cand = t + lax.shift_left(jnp.int32(1), 30 - i)
        return jnp.where(_count(key >= cand) >= kf, cand, t)

    thr = lax.fori_loop(0, 31, value_bit, t0)
    above = key > thr
    ties = key == thr
    need = kf - _count(above)
    idx = lax.broadcasted_iota(I32, (rows, n), 1)

    def index_bit(i, m):
        cand = m + lax.shift_left(jnp.int32(1), n_index_bits - 1 - i)
        return jnp.where(_count(ties & (idx < cand)) <= need, cand, m)

    m = lax.fori_loop(0, n_index_bits, index_bit, jnp.zeros((rows, 1), I32))
    return above | (ties & (idx < m))


SUM_CHAINS = 4


def _sum_rows(x):
    r = x.shape[0]
    if r % (SUM_CHAINS * SUBLANES) == 0 and r > SUM_CHAINS * SUBLANES:
        x = jnp.sum(x.reshape(SUM_CHAINS, r // SUM_CHAINS, x.shape[1]), axis=1)
    return jnp.sum(x, axis=0, keepdims=True)


def _max_rows(x):
    r = x.shape[0]
    if r % (SUM_CHAINS * SUBLANES) == 0 and r > SUM_CHAINS * SUBLANES:
        x = jnp.max(x.reshape(SUM_CHAINS, r // SUM_CHAINS, x.shape[1]), axis=1)
    return jnp.max(x, axis=0, keepdims=True)


def _topk_select_cols(key, topk, n_index_bits):
    n, cols = key.shape
    kf = float(topk)
    cnt = lambda m: _sum_rows(m.astype(F32))
    t0 = jnp.where(cnt(key >= 0) >= kf, 0, INT_MIN).astype(I32)

    def value_bit(i, t):
        cand = t + lax.shift_left(jnp.int32(1), 30 - i)
        return jnp.where(cnt(key >= cand) >= kf, cand, t)

    thr = lax.fori_loop(0, 31, value_bit, t0)
    above = key > thr
    ties = key == thr
    need = kf - cnt(above)
    idx = lax.broadcasted_iota(I32, (n, cols), 0)

    def index_bit(i, m):
        cand = m + lax.shift_left(jnp.int32(1), n_index_bits - 1 - i)
        return jnp.where(cnt(ties & (idx < cand)) <= need, cand, m)

    m = lax.fori_loop(0, n_index_bits, index_bit, jnp.zeros((1, cols), I32))
    return above | (ties & (idx < m))


def _dsa_prompt_block(nb, ik_ref, iqt_ref, wt_ref, k_ref, qt_ref, vt_ref, bias_ref, o_ref, mask_ref,
                      *, n_heads, n_idx_heads, topk):
    w = nb * Q_BLOCK
    ik = ik_ref[0, :w, :]

    def idx_head(h, acc):
        dots = jnp.dot(ik, iqt_ref[0, h], preferred_element_type=F32)
        return acc + jnp.maximum(dots, 0.0) * wt_ref[0, h]

    scores = lax.fori_loop(0, n_idx_heads, idx_head, jnp.zeros((w, Q_BLOCK), F32), unroll=2)
    kpos = lax.broadcasted_iota(I32, (w, Q_BLOCK), 0)
    qpos = (nb - 1) * Q_BLOCK + lax.broadcasted_iota(I32, (w, Q_BLOCK), 1)
    valid = kpos <= qpos
    if w <= topk:
        sel = valid
    else:
        key = jnp.where(valid, _sortable_key(scores), INT_MIN)
        sel = valid & _topk_select_cols(key, topk, int(math.ceil(math.log2(w))) + 1)
    mask_ref[:w, :] = jnp.where(sel, 0.0, -jnp.inf)

    def attn_head(h, carry):
        logits = jnp.dot(k_ref[0, h, :w, :], qt_ref[0, h], preferred_element_type=F32) + mask_ref[:w, :]
        near = [logits[w - Q_BLOCK:] + bias_ref[0, h]]
        if nb >= 2:
            near = [logits[w - 2 * Q_BLOCK:w - Q_BLOCK] + bias_ref[1, h]] + near
        if nb >= 3:
            near = [logits[:w - 2 * Q_BLOCK]] + near
        logits = jnp.concatenate(near, axis=0) if len(near) > 1 else near[0]
        e = jnp.exp(logits - _max_rows(logits))
        den = _sum_rows(e)
        o = jnp.dot(vt_ref[0, h, :, :w], e.astype(BF16), preferred_element_type=F32)
        o_ref[0, h] = (o / den).astype(o_ref.dtype)
        return carry

    lax.fori_loop(0, n_heads, attn_head, 0, unroll=2)


def _dsa_prompt_body(ik_ref, iqt_ref, wt_ref, k_ref, qt_ref, vt_ref, bias_ref, o_ref, mask_ref,
                     *, n_heads, n_idx_heads, seq, topk):
    i = pl.program_id(1)
    for nb in range(1, seq // Q_BLOCK + 1):
        @pl.when(i == nb - 1)
        def _(nb=nb):
            _dsa_prompt_block(nb, ik_ref, iqt_ref, wt_ref, k_ref, qt_ref, vt_ref, bias_ref, o_ref, mask_ref,
                              n_heads=n_heads, n_idx_heads=n_idx_heads, topk=topk)


def dsa_prompt(ik, iqt, wts, k, qt, vt, bias_tiles, *, topk):
    b, h, dh, s = qt.shape
    ih = iqt.shape[1]
    kern = functools.partial(_dsa_prompt_body, n_heads=h, n_idx_heads=ih, seq=s, topk=topk)
    grid_spec = pltpu.PrefetchScalarGridSpec(
        num_scalar_prefetch=0,
        grid=(b, s // Q_BLOCK),
        in_specs=[pl.BlockSpec((1, s, ik.shape[2]), lambda bi, i: (bi, 0, 0)),
                  pl.BlockSpec((1, ih, iqt.shape[2], Q_BLOCK), lambda bi, i: (bi, 0, 0, i)),
                  pl.BlockSpec((1, ih, 1, Q_BLOCK), lambda bi, i: (bi, 0, 0, i)),
                  pl.BlockSpec((1, h, s, dh), lambda bi, i: (bi, 0, 0, 0)),
                  pl.BlockSpec((1, h, dh, Q_BLOCK), lambda bi, i: (bi, 0, 0, i)),
                  pl.BlockSpec((1, h, dh, s), lambda bi, i: (bi, 0, 0, 0)),
                  pl.BlockSpec(bias_tiles.shape, lambda bi, i: (0, 0, 0, 0))],
        out_specs=pl.BlockSpec((1, h, dh, Q_BLOCK), lambda bi, i: (bi, 0, 0, i)),
        scratch_shapes=[pltpu.VMEM((s, Q_BLOCK), F32)],
    )
    return pl.pallas_call(
        kern,
        grid_spec=grid_spec,
        out_shape=jax.ShapeDtypeStruct((b, h, dh, s), BF16),
        compiler_params=_cparams("parallel", "arbitrary"),
        name="dsa_prompt",
    )(ik, iqt, wts, k, qt, vt, bias_tiles)


def _dsa_select_body(pt_ref, iq_ref, wt_ref, iknew_ref, cidx_hbm, o_ref, ikbuf, sem,
                     *, layer, n_pages, group, n_idx_heads, t_new, topk):
    s = pl.program_id(0)
    past = n_pages * PAGE_SIZE
    n_keys = past + PAGE_SIZE

    def ik_copy(i):
        g = i // n_pages
        p = i % n_pages
        page = pt_ref[(s * group + g) * n_pages + p]
        return pltpu.make_async_copy(cidx_hbm.at[layer, page], ikbuf.at[g, p], sem)

    def ik_start(i, carry):
        ik_copy(i).start()
        return carry

    def ik_wait(i, carry):
        ik_copy(i).wait()
        return carry

    lax.fori_loop(0, group * n_pages, ik_start, 0)
    for g in range(group):
        ikbuf[g, n_pages] = iknew_ref[g]
    lax.fori_loop(0, group * n_pages, ik_wait, 0)

    scores = []
    for g in range(group):
        ikt_all = jnp.concatenate([ikbuf[g, p] for p in range(n_pages + 1)], axis=1)
        dots = _mm(iq_ref[g], ikt_all)
        weighted = jnp.maximum(dots, 0.0) * wt_ref[g]
        scores.append(jnp.sum(weighted.reshape(n_idx_heads, t_new, n_keys), axis=0))
    scores = jnp.concatenate(scores, axis=0)
    shape = (group * t_new, n_keys)
    qpos = past + lax.broadcasted_iota(I32, shape, 0) % t_new
    kpos = lax.broadcasted_iota(I32, shape, 1)
    valid = kpos <= qpos
    key = jnp.where(valid, _sortable_key(scores), INT_MIN)
    sel = valid & _topk_select(key, topk, int(math.log2(n_keys)) + 1)
    o_ref[...] = jnp.where(sel, 0.0, -jnp.inf).reshape(group, t_new, n_keys)


def dsa_sample_select(page_table, iq_rows, wt_rows, ik_new_t, cache_idx_kt, *, layer, n_idx_heads, t_new, topk,
                      group):
    db, n_pages = page_table.shape
    idx_dim = cache_idx_kt.shape[2]
    n_keys = (n_pages + 1) * PAGE_SIZE
    kern = functools.partial(_dsa_select_body, layer=layer, n_pages=n_pages, group=group,
                             n_idx_heads=n_idx_heads, t_new=t_new, topk=topk)
    per_g = lambda shape: pl.BlockSpec((group,) + shape, lambda si, pt: (si,) + (0,) * len(shape))
    grid_spec = pltpu.PrefetchScalarGridSpec(
        num_scalar_prefetch=1,
        grid=(db // group,),
        in_specs=[per_g((n_idx_heads * t_new, idx_dim)), per_g((n_idx_heads * t_new, 1)),
                  per_g((idx_dim, PAGE_SIZE)), pl.BlockSpec(memory_space=pl.ANY)],
        out_specs=per_g((t_new, n_keys)),
        scratch_shapes=[pltpu.VMEM((group, n_pages + 1, idx_dim, PAGE_SIZE), F32), pltpu.SemaphoreType.DMA(())],
    )
    return pl.pallas_call(
        kern,
        grid_spec=grid_spec,
        out_shape=jax.ShapeDtypeStruct((db, t_new, n_keys), F32),
        compiler_params=_cparams("arbitrary"),
        name="dsa_sample_select",
    )(page_table.reshape(-1), iq_rows, wt_rows, ik_new_t, cache_idx_kt)


def _dsa_sample_body(pt_ref, mask_ref, qbd_ref, knew_ref, vnew_ref,
                     blast_ref, bnew_ref, ck_hbm, cv_hbm, o_ref,
                     kbuf, vbuf, sem_k, sem_v,
                     *, layer, n_pages, chunk, n_heads, t_new):
    b = pl.program_id(0)
    n_seq = pl.num_programs(0)
    past = n_pages * PAGE_SIZE
    n_chunks = n_pages // chunk
    rows = n_heads * t_new
    ck = chunk * PAGE_SIZE

    def kv_copies(seq, c, j):
        slot = c % 2
        page = pt_ref[seq * n_pages + c * chunk + j]
        return (pltpu.make_async_copy(ck_hbm.at[layer, page], kbuf.at[slot, j], sem_k.at[slot]),
                pltpu.make_async_copy(cv_hbm.at[layer, page], vbuf.at[slot, j], sem_v.at[slot]))

    def start_chunk(seq, c):
        for j in range(chunk):
            kc, vc = kv_copies(seq, c, j)
            kc.start()
            vc.start()

    def wait_chunk(c):
        for j in range(chunk):
            kc, vc = kv_copies(b, c, j)
            kc.wait()
            vc.wait()

    @pl.when(b == 0)
    def _():
        start_chunk(0, 0)

    sel_rows = jnp.tile(mask_ref[0], (n_heads, 1))

    q_rep = jnp.tile(qbd_ref[0], (n_heads, 1))
    row_head = lax.broadcasted_iota(I32, q_rep.shape, 0) // t_new
    col_head = lax.broadcasted_iota(I32, q_rep.shape, 1) // HEAD_DIM
    qbd = jnp.where(row_head == col_head, q_rep, jnp.zeros_like(q_rep))
    neg = -1e30

    def update(state, logits, maskc, vt_bf16):
        m, l, acc = state
        s = logits + maskc
        m_new = jnp.maximum(m, jnp.max(s, axis=-1, keepdims=True))
        alpha = jnp.exp(m - m_new)
        p = jnp.exp(s - m_new)
        l = alpha * l + jnp.sum(p, axis=-1, keepdims=True)
        acc = alpha * acc + _mm_nt(p, vt_bf16)
        return m_new, l, acc

    def pages_t(buf, slot):
        return jnp.concatenate([buf[slot, j] for j in range(chunk)], axis=1).astype(BF16)

    state = (jnp.full((rows, 1), neg, F32), jnp.zeros((rows, 1), F32),
             jnp.zeros((rows, qbd.shape[1]), F32))
    for c in range(n_chunks):
        if c + 1 < n_chunks:
            start_chunk(b, c + 1)
        else:
            @pl.when(b + 1 < n_seq)
            def _():
                start_chunk(b + 1, 0)
        wait_chunk(c)
        slot = c % 2
        logits = _mm(qbd, pages_t(kbuf, slot))
        if c == n_chunks - 1:
            logits = jnp.concatenate([logits[:, :ck - PAGE_SIZE],
                                      logits[:, ck - PAGE_SIZE:] + blast_ref[...]], axis=1)
        state = update(state, logits, sel_rows[:, c * ck:(c + 1) * ck], pages_t(vbuf, slot))
    logits = _mm(qbd, knew_ref[0]) + bnew_ref[...]
    m, l, acc = update(state, logits, sel_rows[:, past:], vnew_ref[0])
    out = jnp.where(row_head == col_head, acc / l, 0.0)
    o_ref[0] = jnp.sum(out.reshape(n_heads, t_new, out.shape[1]), axis=0)


def dsa_sample(page_table, mask, q_bd, k_new_t, v_new_t, bias_last, bias_new, cache_kt, cache_vt,
               *, layer, n_heads, t_new, chunk):
    db, n_pages = page_table.shape
    assert (n_pages // chunk) % 2 == 0, "chunks alternate between two buffers across sequences"
    rows = n_heads * t_new
    width = q_bd.shape[2]
    kern = functools.partial(_dsa_sample_body, layer=layer, n_pages=n_pages, chunk=chunk, n_heads=n_heads,
                             t_new=t_new)
    per_b = lambda shape: pl.BlockSpec((1,) + shape, lambda bi, pt: (bi,) + (0,) * len(shape))
    const = lambda shape: pl.BlockSpec(shape, lambda bi, pt: (0,) * len(shape))
    any_spec = pl.BlockSpec(memory_space=pl.ANY)
    grid_spec = pltpu.PrefetchScalarGridSpec(
        num_scalar_prefetch=1,
        grid=(db,),
        in_specs=[per_b((t_new, mask.shape[2])),
                  per_b((t_new, width)), per_b((width, PAGE_SIZE)), per_b((width, PAGE_SIZE)),
                  const((rows, PAGE_SIZE)), const((rows, PAGE_SIZE)),
                  any_spec, any_spec],
        out_specs=per_b((t_new, width)),
        scratch_shapes=[pltpu.VMEM((2, chunk, width, PAGE_SIZE), F32),
                        pltpu.VMEM((2, chunk, width, PAGE_SIZE), F32),
                        pltpu.SemaphoreType.DMA((2,)),
                        pltpu.SemaphoreType.DMA((2,))],
    )
    return pl.pallas_call(
        kern,
        grid_spec=grid_spec,
        out_shape=jax.ShapeDtypeStruct((db, t_new, width), F32),
        compiler_params=_cparams("arbitrary"),
        name="dsa_sample",
    )(page_table.reshape(-1), mask, q_bd, k_new_t, v_new_t, bias_last, bias_new, cache_kt, cache_vt)


def _token_minor_cache(cache):
    l, pool, page = cache.shape[:3]
    nd = cache.ndim
    return cache.transpose((0, 1) + tuple(range(3, nd)) + (2,)).reshape(l, pool, -1, page)


def _dsa_sample_inputs(q, k_new, v_new, iq, ik_new, iw, bias_tiles):
    db, t, h, dh = q.shape
    ih = iq.shape[2]
    iq_rows = iq.transpose(0, 2, 1, 3).reshape(db, ih * t, -1).astype(BF16)
    wt_rows = (iw * (ih ** -0.5 * iq.shape[3] ** -0.5)).transpose(0, 2, 1).reshape(db, ih * t, 1)
    q_bd = (q * dh ** -0.5).reshape(db, t, h * dh)
    page_t = lambda x: jnp.pad(x.reshape(db, t, -1).transpose(0, 2, 1), ((0, 0), (0, 0), (0, PAGE_SIZE - t)))
    bias_new = bias_tiles[0, :, :t, :].reshape(h * t, Q_BLOCK)
    bias_last = bias_tiles[1, :, :t, :].reshape(h * t, Q_BLOCK)
    return ((iq_rows, wt_rows, page_t(ik_new)),
            (q_bd.astype(BF16), page_t(k_new).astype(BF16), page_t(v_new).astype(BF16), bias_last, bias_new))


def _matmul_residual_body(x_ref, a_ref, w_ref, o_ref):
    o_ref[...] = x_ref[...] + jnp.dot(a_ref[...], w_ref[...], preferred_element_type=F32)


def matmul_residual(x, a, w, *, tm, tn):
    n, d = x.shape
    kd = a.shape[1]
    return pl.pallas_call(
        _matmul_residual_body,
        grid=(n // tm, d // tn),
        in_specs=[pl.BlockSpec((tm, tn), lambda i, j: (i, j)),
                  pl.BlockSpec((tm, kd), lambda i, j: (i, 0)),
                  pl.BlockSpec((kd, tn), lambda i, j: (0, j))],
        out_specs=pl.BlockSpec((tm, tn), lambda i, j: (i, j)),
        out_shape=jax.ShapeDtypeStruct((n, d), F32),
        compiler_params=_cparams("parallel", "parallel"),
        name="matmul_residual",
    )(x, a, w)


def _cross_attn_body(x_ref, g_ref, wq_ref, mk_ref, mv_ref, wo_ref, o_ref, *, groups, t_rows, n_heads, head_dim):
    x = x_ref[...]
    h = _rmsnorm(x, g_ref[...]).astype(BF16)
    q = jnp.dot(h, wq_ref[...], preferred_element_type=F32).astype(BF16)
    scale = head_dim ** -0.5
    outs = []
    for gi in range(groups):
        qg = q[gi * t_rows:(gi + 1) * t_rows]
        heads = []
        for hh in range(n_heads):
            sl = slice(hh * head_dim, (hh + 1) * head_dim)
            logits = _mm_nt(qg[:, sl], mk_ref[gi, :, sl]) * scale
            mx = jnp.max(logits, axis=-1, keepdims=True)
            e = jnp.exp(logits - mx)
            p = e / jnp.sum(e, axis=-1, keepdims=True)
            heads.append(jnp.dot(p.astype(BF16), mv_ref[gi, :, sl], preferred_element_type=F32))
        outs.append(jnp.concatenate(heads, axis=1))
    o = jnp.concatenate(outs, axis=0) if groups > 1 else outs[0]
    o_ref[...] = x + jnp.dot(o.astype(BF16), wo_ref[...], preferred_element_type=F32)


def cross_attn(x, g, wq, mk, mv, wo, *, groups, t_rows, seq_tiles, n_heads):
    n, d = x.shape
    xw = wq.shape[1]
    rows = groups * t_rows
    m = mk.shape[1]
    kern = functools.partial(_cross_attn_body, groups=groups, t_rows=t_rows, n_heads=n_heads,
                             head_dim=xw // n_heads)
    return pl.pallas_call(
        kern,
        grid=(n // rows,),
        in_specs=[pl.BlockSpec((rows, d), lambda i: (i, 0)),
                  pl.BlockSpec((1, d), lambda i: (0, 0)),
                  pl.BlockSpec((d, xw), lambda i: (0, 0)),
                  pl.BlockSpec((groups, m, xw), lambda i: (i // seq_tiles, 0, 0)),
                  pl.BlockSpec((groups, m, xw), lambda i: (i // seq_tiles, 0, 0)),
                  pl.BlockSpec((xw, d), lambda i: (0, 0))],
        out_specs=pl.BlockSpec((rows, d), lambda i: (i, 0)),
        out_shape=jax.ShapeDtypeStruct((n, d), F32),
        compiler_params=_cparams("parallel"),
        name="cross_attn",
    )(x, g.reshape(1, d), wq, mk, mv, wo)


def _router_body(x_ref, g_ref, wr_ref, br_ref, h_ref, r_ref, *, n_groups, per_group):
    h = _rmsnorm(x_ref[...], g_ref[...])
    h_ref[...] = h
    logits = _mm3(h, wr_ref[...]) + br_ref[...]
    lane = lax.broadcasted_iota(I32, logits.shape, 1).astype(F32)
    big = 1e9
    first_lane = lambda hit: jnp.min(jnp.where(hit, lane, big), axis=-1, keepdims=True)
    gl = jnp.where(lane < n_groups, logits, -jnp.inf)
    gmax = jnp.max(gl, axis=-1, keepdims=True)
    grp = first_lane(gl == gmax)
    p_grp = 1.0 / jnp.sum(jnp.exp(gl - gmax), axis=-1, keepdims=True)
    e_id = lane - n_groups
    in_grp = (e_id >= grp * per_group) & (e_id < (grp + 1.0) * per_group)
    el = jnp.where(in_grp, logits, -jnp.inf)
    v1 = jnp.max(el, axis=-1, keepdims=True)
    i1 = first_lane(el == v1) - n_groups
    el2 = jnp.where(e_id == i1, -jnp.inf, el)
    v2 = jnp.max(el2, axis=-1, keepdims=True)
    i2 = first_lane(el2 == v2) - n_groups
    e2 = jnp.exp(v2 - v1)
    g1 = p_grp / (1.0 + e2)
    g2 = p_grp * e2 / (1.0 + e2)
    r_ref[...] = jnp.where(lane == 0, g1, jnp.where(lane == 1, g2, jnp.where(
        lane == 2, i1, jnp.where(lane == 3, i2, 0.0))))


def router(x, g, w_r, b_r, *, tm, n_groups, per_group):
    n, d = x.shape
    kern = functools.partial(_router_body, n_groups=n_groups, per_group=per_group)
    return pl.pallas_call(
        kern,
        grid=(n // tm,),
        in_specs=[pl.BlockSpec((tm, d), lambda i: (i, 0)),
                  pl.BlockSpec((1, d), lambda i: (0, 0)),
                  pl.BlockSpec((d, LANES), lambda i: (0, 0)),
                  pl.BlockSpec((1, LANES), lambda i: (0, 0))],
        out_specs=[pl.BlockSpec((tm, d), lambda i: (i, 0)),
                   pl.BlockSpec((tm, LANES), lambda i: (i, 0))],
        out_shape=[jax.ShapeDtypeStruct((n, d), F32), jax.ShapeDtypeStruct((n, LANES), F32)],
        compiler_params=_cparams("parallel"),
        name="moe_router",
    )(x, g.reshape(1, d), w_r, b_r)


def _slab_rows(buf, base, n_rows, n_slab):
    return jnp.concatenate([buf[pl.ds(base * n_slab + s, n_rows, stride=n_slab), :] for s in range(n_slab)],
                           axis=1)


def _moe_ffn_body(te_ref, tv_ref, tok_ref, h_hbm, w1_ref, w3_ref, w2_ref, o_ref, xbuf, sem, *, tm):
    t = pl.program_id(0)
    n_tiles = pl.num_programs(0)
    n_slab = h_hbm.shape[1]
    slot = t % 2

    def row_copy(tile, sl, r):
        dst = xbuf.at[sl, pl.ds(pl.multiple_of(r * n_slab, n_slab), n_slab)]
        return pltpu.make_async_copy(h_hbm.at[tok_ref[tile * tm + r]], dst, sem.at[sl])

    def start_tile(tile, sl):
        def start(r, c):
            row_copy(tile, sl, 2 * r).start(priority=0)
            row_copy(tile, sl, 2 * r + 1).start(priority=1)
            return c
        lax.fori_loop(0, tm // 2, start, 0)

    @pl.when((t == 0) & (tv_ref[0] != 0))
    def _():
        start_tile(0, 0)

    nxt = jnp.minimum(t + 1, n_tiles - 1)

    @pl.when((t + 1 < n_tiles) & (tv_ref[nxt] != 0))
    def _():
        start_tile(nxt, 1 - slot)

    @pl.when(tv_ref[t] != 0)
    def _():
        def wait(r, c):
            row_copy(t, slot, r).wait()
            return c

        lax.fori_loop(0, tm, wait, 0)
        x = _slab_rows(xbuf.at[slot], 0, tm, n_slab).astype(BF16)
        a = jnp.dot(x, w1_ref[0].astype(BF16), preferred_element_type=F32)
        bgate = jnp.dot(x, w3_ref[0].astype(BF16), preferred_element_type=F32)
        u = (a / (1.0 + jnp.exp(-a))) * bgate
        o_ref[...] = jnp.dot(u.astype(BF16), w2_ref[0].astype(BF16), preferred_element_type=F32)

    @pl.when(tv_ref[t] == 0)
    def _():
        o_ref[...] = jnp.zeros(o_ref.shape, F32)


def moe_ffn(tile_expert, tile_valid, row_tok, h, w1, w3, w2, *, tm):
    n_tiles = tile_expert.shape[0]
    d = h.shape[1]
    de = w1.shape[2]
    h = h.reshape(h.shape[0], d // LANES, LANES)
    grid_spec = pltpu.PrefetchScalarGridSpec(
        num_scalar_prefetch=3,
        grid=(n_tiles,),
        in_specs=[pl.BlockSpec(memory_space=pl.ANY),
                  pl.BlockSpec((1, d, de), lambda t, te, tv, tok: (te[t], 0, 0)),
                  pl.BlockSpec((1, d, de), lambda t, te, tv, tok: (te[t], 0, 0)),
                  pl.BlockSpec((1, de, d), lambda t, te, tv, tok: (te[t], 0, 0))],
        out_specs=pl.BlockSpec((tm, d), lambda t, te, tv, tok: (t, 0)),
        scratch_shapes=[pltpu.VMEM((2, tm * (d // LANES), LANES), F32), pltpu.SemaphoreType.DMA((2,))],
    )
    return pl.pallas_call(
        functools.partial(_moe_ffn_body, tm=tm),
        grid_spec=grid_spec,
        out_shape=jax.ShapeDtypeStruct((n_tiles * tm, d), F32),
        compiler_params=_cparams("arbitrary"),
        name="moe_ffn",
    )(tile_expert, tile_valid, row_tok, h, w1, w3, w2)


def _moe_dispatch(eids, n_experts, tm):
    n, k = eids.shape
    m = n * k
    flat_e = eids.reshape(-1)
    order = jnp.argsort(flat_e, stable=True).astype(I32)
    inv = jnp.argsort(order).astype(I32)
    onehot = flat_e[:, None] == jnp.arange(n_experts)[None, :]
    counts = jnp.sum(onehot.astype(I32), axis=0)
    padded = (counts + tm - 1) // tm * tm
    pad_end = jnp.cumsum(padded)
    pad_start = pad_end - padded
    start = jnp.cumsum(counts) - counts
    shift = jnp.sum(jnp.where(onehot, (pad_start - start)[None, :], 0), axis=1)
    pos = (inv + shift).astype(I32)
    n_tiles = -(-m // tm) + n_experts
    tile_start = jnp.arange(n_tiles) * tm
    tile_valid = (tile_start < pad_end[-1]).astype(I32)
    last = jnp.maximum(pad_end[-1] - 1, 0)
    tile_expert = jnp.minimum(jnp.searchsorted(pad_end, jnp.minimum(tile_start, last), side='right'),
                              n_experts - 1).astype(I32)
    rank = (tile_start - pad_start[tile_expert])[:, None] + jnp.arange(tm)[None, :]
    src = jnp.clip(start[tile_expert][:, None] + rank, 0, m - 1)
    row_tok = jnp.where(rank < counts[tile_expert][:, None], order[src] // k, 0)
    return tile_expert, tile_valid, row_tok.reshape(-1).astype(I32), pos


def _combine_body(pos_ref, x_ref, r_ref, g_ref, y_hbm, o_ref, ybuf, sem, *, tm, top_k):
    i = pl.program_id(0)
    n_tiles = pl.num_programs(0)
    n_slab = y_hbm.shape[1]
    slot = i % 2

    def row_copy(tile, sl, j):
        tok = j // top_k
        kk = j % top_k
        dst = ybuf.at[sl, pl.ds(pl.multiple_of((kk * tm + tok) * n_slab, n_slab), n_slab)]
        return pltpu.make_async_copy(y_hbm.at[pos_ref[tile * tm * top_k + j]], dst, sem.at[sl])

    def start_tile(tile, sl):
        def start(j, c):
            row_copy(tile, sl, 2 * j).start(priority=0)
            row_copy(tile, sl, 2 * j + 1).start(priority=1)
            return c
        lax.fori_loop(0, tm * top_k // 2, start, 0)

    @pl.when(i == 0)
    def _():
        start_tile(0, 0)

    @pl.when(i + 1 < n_tiles)
    def _():
        start_tile(i + 1, 1 - slot)

    def wait(j, c):
        row_copy(i, slot, j).wait()
        return c

    lax.fori_loop(0, tm * top_k, wait, 0)
    route = r_ref[...]
    x = x_ref[...]
    for kk in range(top_k):
        x = x + _slab_rows(ybuf.at[slot], kk * tm, tm, n_slab) * route[:, kk:kk + 1]
    o_ref[...] = _rmsnorm(x, g_ref[...])


def moe_combine(pos, x, route, g, y, *, tm, top_k):
    n, d = x.shape
    y = y.reshape(y.shape[0], d // LANES, LANES)
    grid_spec = pltpu.PrefetchScalarGridSpec(
        num_scalar_prefetch=1,
        grid=(n // tm,),
        in_specs=[pl.BlockSpec((tm, d), lambda i, p: (i, 0)),
                  pl.BlockSpec((tm, LANES), lambda i, p: (i, 0)),
                  pl.BlockSpec((1, d), lambda i, p: (0, 0)),
                  pl.BlockSpec(memory_space=pl.ANY)],
        out_specs=pl.BlockSpec((tm, d), lambda i, p: (i, 0)),
        scratch_shapes=[pltpu.VMEM((2, top_k * tm * (d // LANES), LANES), F32), pltpu.SemaphoreType.DMA((2,))],
    )
    return pl.pallas_call(
        functools.partial(_combine_body, tm=tm, top_k=top_k),
        grid_spec=grid_spec,
        out_shape=jax.ShapeDtypeStruct((n, d), F32),
        compiler_params=_cparams("arbitrary"),
        name="moe_combine",
    )(pos, x, route, g.reshape(1, d), y)


def kernel(x_prompt, x_sample, mem_prompt, cache_k, cache_v, cache_idx_k, page_table, state_wkv, state_shift, cache_mem_k, cache_mem_v, g_mix, w_in, mu_shift, rw_w0, rw_w2, rw_a0, rw_a2, rw_g2, rw_kk, rw_ka, rw_rk, rw_ln_g, rw_ln_b, w_out, g_cross, g_mem, w_cq, w_ck, w_cv, w_co, g_ffn, w_rg, b_rg, w_re, b_re, w_e1, w_e3, w_e2, rel_bias, g_final):
    B, S, D = x_prompt.shape
    DB, T, _ = x_sample.shape
    assert w_in.shape[0] == 1, "single-layer trunk only"
    l = 0
    n_pages = page_table.shape[1]
    past = n_pages * PAGE_SIZE
    topk_p = min(TOPK_MAX, S // 4)
    topk_s = min(TOPK_MAX, (past + T) // 4)
    rw_proj = mu_shift.shape[1]
    width = rw_w0.shape[1]
    at_w = D - width
    n_heads = at_w // HEAD_DIM
    idx_dim = cache_idx_k.shape[-1]
    ih = (w_in.shape[2] - rw_proj - 3 * at_w - idx_dim) // (idx_dim + 1)
    xw = w_cq.shape[2]
    x_heads = cache_mem_k.shape[3]
    n_mem = mem_prompt.shape[1]
    n_experts = w_e1.shape[1]
    top_k = 2
    proj_pad = -(-w_in.shape[2] // 512) * 512
    o = rw_proj

    w_all = jnp.pad(w_in[l].T, ((0, proj_pad - w_in.shape[2]), (0, 0))).astype(BF16)
    w_out_b = w_out[l].astype(BF16)
    w_cq_b, w_co_b = w_cq[l].astype(BF16), w_co[l].astype(BF16)
    w_ckv = jnp.concatenate([w_ck[l], w_cv[l]], axis=1).astype(BF16)
    n_route = w_rg.shape[2] + w_re.shape[2]
    w_r = jnp.pad(jnp.concatenate([w_rg[l], w_re[l]], axis=1), ((0, 0), (0, LANES - n_route)))
    b_r = jnp.pad(jnp.concatenate([b_rg[l], b_re[l]]), (0, LANES - n_route)).reshape(1, LANES)
    rw_args = (mu_shift[l], rw_w0[l], rw_w2[l], rw_a0[l], rw_a2[l], rw_g2[l], rw_kk[l], rw_ka[l],
               rw_rk[l].reshape(-1))
    tiles = bias_tables(rel_bias, ((0, False), (Q_BLOCK, False), (0, True), (Q_BLOCK, True)))

    def split_attn(f3):
        q = f3[..., o:o + at_w]
        k = f3[..., o + at_w:o + 2 * at_w]
        v = f3[..., o + 2 * at_w:o + 3 * at_w]
        iq = f3[..., o + 3 * at_w:o + 3 * at_w + ih * idx_dim]
        ik = f3[..., o + 3 * at_w + ih * idx_dim:o + 3 * at_w + (ih + 1) * idx_dim]
        iw = f3[..., o + 3 * at_w + (ih + 1) * idx_dim:o + 3 * at_w + (ih + 1) * idx_dim + ih]
        return q, k, v, iq, ik, iw

    def rw_rows(y):
        b_, p_, t_, _ = y.shape
        return y.transpose(0, 2, 1, 3).reshape(b_ * t_, p_ * PAIR).astype(BF16)

    xp = x_prompt.reshape(B * S, D)
    fp = norm_matmul(xp, g_mix[l], w_all, tm=1024, tn=512, w_is_transposed=True).reshape(B, S, proj_pad)
    feats_p = rwkv_prep(fp, jnp.zeros((B, rw_proj), F32), *rw_args, tm=256, rw_proj=rw_proj, width=width)
    rw_p, st_p = rwkv_chunk(feats_p, jnp.zeros((B, width // PAIR, PAIR, PAIR), F32), rw_ln_g[l], rw_ln_b[l], L=64)
    q, k_p, v_p, iq, ik_p, iw = split_attn(fp)
    hm = lambda z, h_: z.reshape(B, S, h_, -1).transpose(0, 2, 1, 3).astype(BF16)
    hmt = lambda z, h_: z.reshape(B, S, h_, -1).transpose(0, 2, 3, 1).astype(BF16)
    at_p = dsa_prompt(ik_p.astype(BF16), hmt(iq, ih),
                      (iw * (ih ** -0.5 * idx_dim ** -0.5)).transpose(0, 2, 1)[:, :, None, :],
                      hm(k_p, n_heads), hmt(q * HEAD_DIM ** -0.5, n_heads), hmt(v_p, n_heads),
                      tiles[2:4], topk=topk_p)
    mix_p = jnp.concatenate([rw_rows(rw_p), at_p.transpose(0, 3, 1, 2).reshape(B * S, at_w)], axis=1)
    x1_p = matmul_residual(xp, mix_p, w_out_b, tm=512, tn=512)
    mkv = norm_matmul(mem_prompt.reshape(B * n_mem, D), g_mem[l], w_ckv, tm=256, tn=512)
    mk_p = mkv[:, :xw].reshape(B, n_mem, xw)
    mv_p = mkv[:, xw:].reshape(B, n_mem, xw)
    x2_p = cross_attn(x1_p, g_cross[l], w_cq_b, mk_p.astype(BF16), mv_p.astype(BF16), w_co_b,
                      groups=1, t_rows=512, seq_tiles=S // 512, n_heads=x_heads)

    xs = x_sample.reshape(DB * T, D)
    fs = norm_matmul(xs, g_mix[l], w_all, tm=DB * T, tn=512, w_is_transposed=True).reshape(DB, T, proj_pad)
    feats_s = rwkv_prep(fs, state_shift[l], *rw_args, tm=T, rw_proj=rw_proj, width=width)
    rw_s, st_s = rwkv_chunk(feats_s, _state_to_block_diag(state_wkv[l]), rw_ln_g[l], rw_ln_b[l], L=T)
    q2, k_s, v_s, iq2, ik_s, iw2 = split_attn(fs)
    r4 = lambda z, h_: z.reshape(DB, T, h_, -1)
    sel_args, att_args = _dsa_sample_inputs(r4(q2, n_heads), r4(k_s, n_heads), r4(v_s, n_heads), r4(iq2, ih),
                                            ik_s, iw2, tiles[0:2])
    mask_s = dsa_sample_select(page_table, *sel_args, _token_minor_cache(cache_idx_k), layer=l,
                               n_idx_heads=ih, t_new=T, topk=topk_s, group=4)
    at_s = dsa_sample(page_table, mask_s, *att_args, _token_minor_cache(cache_k), _token_minor_cache(cache_v),
                      layer=l, n_heads=n_heads, t_new=T, chunk=8)
    mix_s = jnp.concatenate([rw_rows(rw_s), at_s.reshape(DB * T, at_w).astype(BF16)], axis=1)
    x1_s = matmul_residual(xs, mix_s, w_out_b, tm=DB * T, tn=512)
    x2_s = cross_attn(x1_s, g_cross[l], w_cq_b, cache_mem_k[l].reshape(DB, n_mem, xw).astype(BF16),
                      cache_mem_v[l].reshape(DB, n_mem, xw).astype(BF16), w_co_b,
                      groups=8, t_rows=T, seq_tiles=1, n_heads=x_heads)

    h_p, route_p = router(x2_p, g_ffn[l], w_r, b_r, tm=512, n_groups=w_rg.shape[2],
                          per_group=w_re.shape[2] // w_rg.shape[2])
    h_s, route_s = router(x2_s, g_ffn[l], w_r, b_r, tm=DB * T, n_groups=w_rg.shape[2],
                          per_group=w_re.shape[2] // w_rg.shape[2])
    h_all = jnp.concatenate([h_p, h_s], axis=0)
    eids = jnp.concatenate([route_p[:, top_k:2 * top_k], route_s[:, top_k:2 * top_k]], axis=0).astype(I32)
    tile_expert, tile_valid, row_tok, pos = _moe_dispatch(eids, n_experts, 256)
    y_rows = moe_ffn(tile_expert, tile_valid, row_tok, h_all, w_e1[l], w_e3[l], w_e2[l], tm=256)
    n_p = B * S
    y_p = moe_combine(pos[:n_p * top_k], x2_p, route_p, g_final, y_rows, tm=256, top_k=top_k)
    y_s = moe_combine(pos[n_p * top_k:], x2_s, route_s, g_final, y_rows, tm=DB * T, top_k=top_k)

    hd = lambda z, b_, t_: z.reshape(1, b_, t_, n_heads, HEAD_DIM)
    return (y_p.reshape(B, S, D), y_s.reshape(DB, T, D),
            hd(k_p, B, S), hd(v_p, B, S), ik_p[None], _state_from_block_diag(st_p)[None], fp[:, -1, :rw_proj][None],
            mk_p.reshape(1, B, n_mem, x_heads, xw // x_heads), mv_p.reshape(1, B, n_mem, x_heads, xw // x_heads),
            hd(k_s, DB, T), hd(v_s, DB, T), ik_s[None], _state_from_block_diag(st_s)[None],
            fs[:, -1, :rw_proj][None])
```

```python
import functools
import math

import jax
import jax.numpy as jnp
from jax import lax
from jax.experimental import pallas as pl
from jax.experimental.pallas import tpu as pltpu

F32 = jnp.float32
BF16 = jnp.bfloat16
I32 = jnp.int32

LANES = 128
SUBLANES = 8
VMEM_LIMIT_BYTES = 56 * 1024 * 1024

HEAD_DIM = 64
PAIR = 2 * HEAD_DIM
GN_EPS = 64e-5
NORM_EPS = 1e-6
TOPK_MAX = 256
Q_BLOCK = 128
N_BUCKETS = 32
MAX_DISTANCE = 128
PAGE_SIZE = 128
N_GROUPS = 4
EXPERTS_PER_GROUP = 8
INT_MIN = -(2 ** 31)


def _cparams(*sem):
    return pltpu.CompilerParams(dimension_semantics=sem, vmem_limit_bytes=VMEM_LIMIT_BYTES)


def _mm(a, b):
    return jnp.dot(a.astype(BF16), b.astype(BF16), preferred_element_type=F32)


def _mm_nt(a, b):
    return lax.dot_general(a.astype(BF16), b.astype(BF16), (((1,), (1,)), ((), ())),
                           preferred_element_type=F32)


def _split2(x):
    hi = x.astype(BF16)
    lo = (x - hi.astype(F32)).astype(BF16)
    return hi, lo


def _mm3(a, b):
    ah, al = _split2(a)
    bh, bl = _split2(b)
    d = lambda x, y: jnp.dot(x, y, preferred_element_type=F32)
    return d(ah, bh) + (d(ah, bl) + d(al, bh))


def _mm_exact_rhs(a, b_bf16):
    hi = a.astype(BF16)
    r1 = a - hi.astype(F32)
    mid = r1.astype(BF16)
    lo = (r1 - mid.astype(F32)).astype(BF16)
    d = lambda x: jnp.dot(x, b_bf16, preferred_element_type=F32)
    return d(hi) + (d(mid) + d(lo))


def _rmsnorm(x, g):
    ms = jnp.mean(x * x, axis=-1, keepdims=True)
    return x * lax.rsqrt(ms + NORM_EPS) * g


def _norm_matmul_body(x_ref, g_ref, w_ref, o_ref, xn_ref):
    @pl.when(pl.program_id(1) == 0)
    def _():
        xn_ref[...] = _rmsnorm(x_ref[...], g_ref[...]).astype(BF16)

    o_ref[...] = _mm(xn_ref[...], w_ref[...])


def norm_matmul(x, g, w, *, tm, tn):
    n, d = x.shape
    m = w.shape[1]
    return pl.pallas_call(
        _norm_matmul_body,
        grid=(n // tm, m // tn),
        in_specs=[pl.BlockSpec((tm, d), lambda i, j: (i, 0)),
                  pl.BlockSpec((1, d), lambda i, j: (0, 0)),
                  pl.BlockSpec((d, tn), lambda i, j: (0, j))],
        out_specs=pl.BlockSpec((tm, tn), lambda i, j: (i, j)),
        out_shape=jax.ShapeDtypeStruct((n, m), F32),
        scratch_shapes=[pltpu.VMEM((tm, d), BF16)],
        compiler_params=_cparams("parallel", "arbitrary"),
        name="norm_matmul",
    )(x, g.reshape(1, d), w)


def _norm_matmul_split_body(x_ref, g_ref, w_ref, *rest, bounds):
    o_refs, xn_ref = rest[:-1], rest[-1]
    j = pl.program_id(1)

    @pl.when(j == 0)
    def _():
        xn_ref[...] = _rmsnorm(x_ref[...], g_ref[...]).astype(BF16)

    res = _mm_nt(xn_ref[...], w_ref[...])
    for o_ref, (lo, hi) in zip(o_refs, bounds):
        @pl.when((j >= lo) & (j < hi))
        def _(o_ref=o_ref):
            o_ref[...] = res.astype(o_ref.dtype)


def norm_matmul_split(x, g, wt, segments, *, tm, tn):
    n, d = x.shape
    m = wt.shape[0]
    bounds, lo = [], 0
    for nt, _ in segments:
        bounds.append((lo, lo + nt))
        lo += nt
    assert lo * tn == m
    out_specs = [pl.BlockSpec((tm, tn), lambda i, j, lo=lo, hi=hi: (i, jnp.clip(j - lo, 0, hi - lo - 1)))
                 for lo, hi in bounds]
    out_shape = [jax.ShapeDtypeStruct((n, nt * tn), dt) for nt, dt in segments]
    return pl.pallas_call(
        functools.partial(_norm_matmul_split_body, bounds=tuple(bounds)),
        grid=(n // tm, m // tn),
        in_specs=[pl.BlockSpec((tm, d), lambda i, j: (i, 0)),
                  pl.BlockSpec((1, d), lambda i, j: (0, 0)),
                  pl.BlockSpec((tn, d), lambda i, j: (j, 0))],
        out_specs=out_specs,
        out_shape=out_shape,
        scratch_shapes=[pltpu.VMEM((tm, d), BF16)],
        compiler_params=_cparams("arbitrary", "arbitrary"),
        name="norm_matmul_split",
    )(x, g.reshape(1, d), wt)


def _pair_ones():
    r = lax.broadcasted_iota(I32, (PAIR, PAIR), 0) // HEAD_DIM
    c = lax.broadcasted_iota(I32, (PAIR, PAIR), 1) // HEAD_DIM
    return (r == c).astype(BF16)


def _head_sum(x, ones_bd):
    return _mm_exact_rhs(x, ones_bd)


def _rwkv_prep_body(f_ref, prev8_ref, init_ref, mu_ref, ft_ref, tprev8_ref, tinit_ref, tmu_ref,
                    w0_ref, w2_ref, a0_ref, a2_ref, g2_ref, kk_ref, ka_ref, rk_ref,
                    r_o, k_o, v_o, kk_o, b_o, ld_o, g_o, bon_o, *, tm, width):
    i = pl.program_id(1)

    def token_shift(f, p8_ref, i_ref, m_ref):
        nc = m_ref.shape[1]
        prev_row = jnp.where(i == 0, i_ref[0], p8_ref[0, SUBLANES - 1:SUBLANES, :nc])
        rolled = pltpu.roll(f, shift=1, axis=0)
        row = lax.broadcasted_iota(I32, f.shape, 0)
        f_prev = jnp.where(row == 0, prev_row, rolled)
        return f + (f_prev - f) * m_ref[...]

    fs = token_shift(f_ref[0], prev8_ref, init_ref, mu_ref)
    n_dec = w2_ref.shape[0]
    n_icl = a2_ref.shape[0]
    n_lora = tmu_ref.shape[1]
    ts = token_shift(ft_ref[0][:, :n_lora], tprev8_ref, tinit_ref, tmu_ref)
    w_ = width
    r = fs[:, 0:w_]
    k = fs[:, w_:2 * w_]
    v = fs[:, 2 * w_:3 * w_]
    wd = ts[:, 0:n_dec]
    ad = ts[:, n_dec:n_dec + n_icl]
    gd = ts[:, n_dec + n_icl:]
    z = w0_ref[...] + _mm3(jnp.tanh(wd), w2_ref[...])
    nz = -z
    softplus = jnp.maximum(nz, 0.0) + jnp.log(1.0 + jnp.exp(-jnp.abs(nz)))
    w = -softplus - 0.5
    ld = -jnp.exp(w)
    a = 1.0 / (1.0 + jnp.exp(-(a0_ref[...] + _mm3(ad, a2_ref[...]))))
    g = _mm3(1.0 / (1.0 + jnp.exp(-gd)), g2_ref[...])
    kk = k * kk_ref[...]
    k2 = k * (1.0 + (a - 1.0) * ka_ref[...])
    rk = r * k2 * rk_ref[...]
    ones_bd = _pair_ones()
    for p in range(w_ // PAIR):
        sl = slice(p * PAIR, (p + 1) * PAIR)
        kkp = kk[:, sl]
        nrm = jnp.sqrt(_head_sum(kkp * kkp, ones_bd))
        kkp = kkp / jnp.maximum(nrm, 1e-12)
        ap = a[:, sl]
        r_o[0, p] = r[:, sl]
        k_o[0, p] = k2[:, sl]
        v_o[0, p] = v[:, sl]
        kk_o[0, p] = kkp
        b_o[0, p] = kkp * ap
        ld_o[0, p] = ld[:, sl]
        g_o[0, p] = g[:, sl]
        bon_o[0, p] = _head_sum(rk[:, sl], ones_bd) * v[:, sl]


def rwkv_prep(f_main, f_tail, init_prev, mu, w0, w2, a0, a2, g2, k_k, k_a, r_k, *, tm, width):
    b, t, n_main = f_main.shape
    n_tail = f_tail.shape[2]
    n_lora = mu.shape[0] - n_main
    npair = width // PAIR
    row1 = lambda x: x.reshape(1, -1)
    kern = functools.partial(_rwkv_prep_body, tm=tm, width=width)
    full = lambda a: pl.BlockSpec(a.shape, lambda bi, i: (0,) * a.ndim)
    args = [row1(w0), w2, row1(a0), a2, g2, row1(k_k), row1(k_a), row1(r_k)]
    out_spec = pl.BlockSpec((1, npair, tm, PAIR), lambda bi, i: (bi, 0, i, 0))
    out_shape = jax.ShapeDtypeStruct((b, npair, t, PAIR), F32)
    prev8_map = lambda bi, i: (bi, jnp.maximum(i * (tm // SUBLANES) - 1, 0), 0)

    def feature_specs(ncols, n_init):
        return [pl.BlockSpec((1, tm, ncols), lambda bi, i: (bi, i, 0)),
                pl.BlockSpec((1, SUBLANES, ncols), prev8_map),
                pl.BlockSpec((1, 1, n_init), lambda bi, i: (bi, 0, 0)),
                pl.BlockSpec((1, n_init), lambda bi, i: (0, 0))]

    return pl.pallas_call(
        kern,
        grid=(b, t // tm),
        in_specs=feature_specs(n_main, n_main) + feature_specs(n_tail, n_lora) + [full(a) for a in args],
        out_specs=[out_spec] * 8,
        out_shape=[out_shape] * 8,
        compiler_params=_cparams("parallel", "parallel"),
        name="rwkv_prep",
    )(f_main, f_main, init_prev[:, :n_main].reshape(b, 1, n_main), row1(mu[:n_main]),
      f_tail, f_tail, init_prev[:, n_main:].reshape(b, 1, n_lora), row1(mu[n_main:]), *args)


def _rwkv_chunk_body(r_ref, k_ref, v_ref, kk_ref, b_ref, ld_ref, g_ref, bon_ref, s0_ref,
                     lng_ref, lnb_ref, o_ref, st_ref, s_ref, *, L, npair, group):
    c = pl.program_id(1)

    @pl.when(c == 0)
    def _():
        s_ref[...] = s0_ref[0]

    L2 = 2 * L
    row = lax.broadcasted_iota(I32, (L2, L2), 0)
    col = lax.broadcasted_iota(I32, (L2, L2), 1)
    same = (row // L) == (col // L)
    tri_strict = same & (col < row)
    tri_incl = same & (col <= row)
    eye = (row == col).astype(F32)
    tr = lax.broadcasted_iota(I32, (L, L), 0)
    tc = lax.broadcasted_iota(I32, (L, L), 1)
    cum_mat = (tc <= tr).astype(BF16)
    lane = lax.broadcasted_iota(I32, (L, PAIR), 1)
    first = lane < HEAD_DIM
    ones_bd = _pair_ones()
    n_sq = max(int(math.ceil(math.log2(L))) - 1, 0)

    def block_diag(x):
        return jnp.concatenate([jnp.where(first, x, 0.0), jnp.where(first, 0.0, x)], axis=0)

    def group_step(gi, carry):
        ps = [gi * group + j for j in range(group)]
        each = lambda f, *cols: [f(*args) for args in zip(*cols)]
        ld = [ld_ref[0, p] for p in ps]
        cum = each(lambda x: _mm_exact_rhs_t(cum_mat, x), ld)
        dec = each(jnp.exp, cum)
        dec_inv = each(lambda c_: jnp.exp(-c_), cum)
        a_t = each(lambda p, c_, l_: block_diag(-kk_ref[0, p] * jnp.exp(c_ - l_)), ps, cum, ld)
        b_t = each(lambda p, e: block_diag(b_ref[0, p] * e), ps, dec_inv)
        k_t = each(lambda p, e: block_diag(k_ref[0, p] * e), ps, dec_inv)
        r_t = each(lambda p, e: block_diag(r_ref[0, p] * e), ps, dec)
        v_b = each(lambda p: block_diag(v_ref[0, p]), ps)
        a_ab = each(lambda a, b: jnp.where(tri_strict, _mm_nt(a, b), 0.0), a_t, b_t)
        a_ak = each(lambda a, k: jnp.where(tri_strict, _mm_nt(a, k), 0.0), a_t, k_t)
        a_rb = each(lambda r, b: jnp.where(tri_incl, _mm_nt(r, b), 0.0), r_t, b_t)
        a_rk = each(lambda r, k: jnp.where(tri_incl, _mm_nt(r, k), 0.0), r_t, k_t)
        x = a_ab
        t_inv = each(lambda a: eye + a, a_ab)
        for _ in range(n_sq):
            x = each(lambda x_: _mm(x_, x_), x)
            t_inv = each(lambda t, x_: t + _mm(x_, t), t_inv, x)
        s = [s_ref[p] for p in ps]
        rhs = each(lambda a, s_, ak, v: _mm_nt(a, s_) + _mm(ak, v), a_t, s, a_ak, v_b)
        u = each(_mm, t_inv, rhs)
        y_b = each(lambda r, s_, rb, u_, rk, v: _mm_nt(r, s_) + _mm(rb, u_) + _mm(rk, v),
                   r_t, s, a_rb, u, a_rk, v_b)
        s_new = each(lambda s_, u_, b, v, k, d: (s_ + _mm(u_.T, b) + _mm(v.T, k)) * d[L - 1:L, :],
                     s, u, b_t, v_b, k_t, dec)
        for p, sn in zip(ps, s_new):
            s_ref[p] = sn
        y = each(lambda yb: yb[:L] + yb[L:], y_b)
        mean = each(lambda y_: _head_sum(y_, ones_bd) * (1.0 / HEAD_DIM), y)
        d = each(lambda y_, m: y_ - m, y, mean)
        var = each(lambda d_: _head_sum(d_ * d_, ones_bd) * (1.0 / HEAD_DIM), d)
        for p, d_, v_ in zip(ps, d, var):
            yn = d_ * lax.rsqrt(v_ + GN_EPS) * lng_ref[p] + lnb_ref[p]
            o_ref[0, p] = (yn + bon_ref[0, p]) * g_ref[0, p]
        return carry

    lax.fori_loop(0, npair // group, group_step, 0)

    @pl.when(c == pl.num_programs(1) - 1)
    def _():
        st_ref[0] = s_ref[...]


def _mm_exact_rhs_t(m_bf16, x):
    hi = x.astype(BF16)
    r1 = x - hi.astype(F32)
    mid = r1.astype(BF16)
    lo = (r1 - mid.astype(F32)).astype(BF16)
    d = lambda y: jnp.dot(m_bf16, y, preferred_element_type=F32)
    return d(hi) + (d(mid) + d(lo))


def rwkv_chunk(feats, s0_bd, ln_g, ln_b, *, L, group=8):
    b, npair, t, _ = feats[0].shape
    blk = pl.BlockSpec((1, npair, L, PAIR), lambda bi, c: (bi, 0, c, 0))
    st_spec = pl.BlockSpec((1, npair, PAIR, PAIR), lambda bi, c: (bi, 0, 0, 0))
    par_spec = pl.BlockSpec((npair, 1, PAIR), lambda bi, c: (0, 0, 0))
    kern = functools.partial(_rwkv_chunk_body, L=L, npair=npair, group=group)
    return pl.pallas_call(
        kern,
        grid=(b, t // L),
        in_specs=[blk] * 8 + [st_spec, par_spec, par_spec],
        out_specs=[blk, st_spec],
        out_shape=[jax.ShapeDtypeStruct((b, npair, t, PAIR), F32),
                   jax.ShapeDtypeStruct((b, npair, PAIR, PAIR), F32)],
        scratch_shapes=[pltpu.VMEM((npair, PAIR, PAIR), F32)],
        compiler_params=_cparams("parallel", "arbitrary"),
        name="rwkv_chunk",
    )(*feats, s0_bd, ln_g.reshape(npair, 1, PAIR), ln_b.reshape(npair, 1, PAIR))


def _state_to_block_diag(s):
    b, h, n, _ = s.shape
    s = s.reshape(b, h // 2, 2, n, n)
    z = jnp.zeros_like(s[:, :, 0])
    top = jnp.concatenate([s[:, :, 0], z], axis=-1)
    bot = jnp.concatenate([z, s[:, :, 1]], axis=-1)
    return jnp.concatenate([top, bot], axis=-2)


def _state_from_block_diag(s_bd):
    b, p, _, _ = s_bd.shape
    n = HEAD_DIM
    return jnp.stack([s_bd[:, :, :n, :n], s_bd[:, :, n:, n:]], axis=2).reshape(b, 2 * p, n, n)


def _t5_bucket(dist):
    exact = N_BUCKETS // 2
    d = jnp.maximum(dist, 0)
    far = exact + (jnp.log(jnp.maximum(d, 1).astype(F32) / exact) / math.log(MAX_DISTANCE / exact)
                   * (N_BUCKETS - exact)).astype(I32)
    return jnp.where(d < exact, d, jnp.minimum(far, N_BUCKETS - 1))


def _bias_tables_body(rb_ref, o_ref, *, offsets, n_heads):
    r = lax.broadcasted_iota(I32, (Q_BLOCK, Q_BLOCK), 0)
    c = lax.broadcasted_iota(I32, (Q_BLOCK, Q_BLOCK), 1)
    for t, (off, key_major) in enumerate(offsets):
        bucket = _t5_bucket((c - r if key_major else r - c) + off)
        for h in range(n_heads):
            def body(bk, acc):
                return jnp.where(bucket == bk, rb_ref[bk, h], acc)
            tile = lax.fori_loop(0, N_BUCKETS, body, jnp.zeros((Q_BLOCK, Q_BLOCK), F32))
            o_ref[t, h] = tile - rb_ref[N_BUCKETS - 1, h]


def bias_tables(rel_bias, offsets):
    n_heads = rel_bias.shape[1]
    kern = functools.partial(_bias_tables_body, offsets=tuple(offsets), n_heads=n_heads)
    return pl.pallas_call(
        kern,
        in_specs=[pl.BlockSpec(memory_space=pltpu.SMEM)],
        out_specs=pl.BlockSpec(memory_space=pltpu.VMEM),
        out_shape=jax.ShapeDtypeStruct((len(offsets), n_heads, Q_BLOCK, Q_BLOCK), F32),
        name="bias_tables",
    )(rel_bias)


def _sortable_key(scores):
    bits = lax.bitcast_convert_type(scores + 0.0, I32)
    return jnp.where(bits < 0, bits ^ 0x7FFFFFFF, bits)


def _count(mask):
    return jnp.sum(mask.astype(F32), axis=-1, keepdims=True)


def _topk_select(key, topk, n_index_bits):
    rows, n = key.shape
    kf = float(topk)
    t0 = jnp.where(_count(key >= 0) >= kf, 0, INT_MIN).astype(I32)

    def value_bit(i, t):
        cand = t + lax.shift_left(jnp.int32(1), 30 - i)
        return jnp.where(_count(key >= cand) >= kf, cand, t)

    thr = lax.fori_loop(0, 31, value_bit, t0)
    above = key > thr
    ties = key == thr
    need = kf - _count(above)
    idx = lax.broadcasted_iota(I32, (rows, n), 1)

    def lowest_ties():
        def index_bit(i, m):
            cand = m + lax.shift_left(jnp.int32(1), n_index_bits - 1 - i)
            return jnp.where(_count(ties & (idx < cand)) <= need, cand, m)
        return lax.fori_loop(0, n_index_bits, index_bit, jnp.zeros((rows, 1), I32))

    surplus = jnp.max(_count(ties) - need) > 0.0
    m = lax.cond(surplus, lowest_ties, lambda: jnp.full((rows, 1), 2 ** n_index_bits, I32))
    return above | (ties & (idx < m))


SUM_CHAINS = 4
HEAD_GROUP = 4


def _sum_rows(x):
    r = x.shape[0]
    if r % (SUM_CHAINS * SUBLANES) == 0 and r > SUM_CHAINS * SUBLANES:
        x = jnp.sum(x.reshape(SUM_CHAINS, r // SUM_CHAINS, x.shape[1]), axis=1)
    return jnp.sum(x, axis=0, keepdims=True)


def _max_rows(x):
    r = x.shape[0]
    if r % (SUM_CHAINS * SUBLANES) == 0 and r > SUM_CHAINS * SUBLANES:
        x = jnp.max(x.reshape(SUM_CHAINS, r // SUM_CHAINS, x.shape[1]), axis=1)
    return jnp.max(x, axis=0, keepdims=True)


def _topk_select_cols(key, topk, n_index_bits):
    n, cols = key.shape
    kf = float(topk)
    cnt = lambda m: _sum_rows(m.astype(F32))
    t0 = jnp.where(cnt(key >= 0) >= kf, 0, INT_MIN).astype(I32)

    def value_bit(i, t):
        cand = t + lax.shift_left(jnp.int32(1), 30 - i)
        return jnp.where(cnt(key >= cand) >= kf, cand, t)

    thr = lax.fori_loop(0, 31, value_bit, t0)
    above = key > thr
    ties = key == thr
    need = kf - cnt(above)
    idx = lax.broadcasted_iota(I32, (n, cols), 0)

    def lowest_ties():
        def index_bit(i, m):
            cand = m + lax.shift_left(jnp.int32(1), n_index_bits - 1 - i)
            return jnp.where(cnt(ties & (idx < cand)) <= need, cand, m)
        return lax.fori_loop(0, n_index_bits, index_bit, jnp.zeros((1, cols), I32))

    surplus = jnp.max(cnt(ties) - need) > 0.0
    m = lax.cond(surplus, lowest_ties, lambda: jnp.full((1, cols), 2 ** n_index_bits, I32))
    return above | (ties & (idx < m))


def _dsa_prompt_block(nb, ik_ref, iqt_ref, wt_ref, k_ref, qt_ref, vt_ref, bias_ref, o_ref, mask_ref,
                      *, n_heads, n_idx_heads, topk):
    w = nb * Q_BLOCK
    ik = ik_ref[0, :w, :]

    def idx_heads(gi, acc):
        hs = [gi * HEAD_GROUP + j for j in range(HEAD_GROUP)]
        dots = [jnp.dot(ik, iqt_ref[0, h], preferred_element_type=F32) for h in hs]
        terms = [jnp.maximum(d, 0.0) * wt_ref[0, h] for d, h in zip(dots, hs)]
        return acc + ((terms[0] + terms[1]) + (terms[2] + terms[3]))

    scores = lax.fori_loop(0, n_idx_heads // HEAD_GROUP, idx_heads, jnp.zeros((w, Q_BLOCK), F32))
    kpos = lax.broadcasted_iota(I32, (w, Q_BLOCK), 0)
    qpos = (nb - 1) * Q_BLOCK + lax.broadcasted_iota(I32, (w, Q_BLOCK), 1)
    valid = kpos <= qpos
    if w <= topk:
        sel = valid
    else:
        key = jnp.where(valid, _sortable_key(scores), INT_MIN)
        sel = valid & _topk_select_cols(key, topk, int(math.ceil(math.log2(w))) + 1)
    mask_ref[:w, :] = jnp.where(sel, 0.0, -jnp.inf)

    def with_near_bias(logits, h):
        near = [logits[w - Q_BLOCK:] + bias_ref[0, h]]
        if nb >= 2:
            near = [logits[w - 2 * Q_BLOCK:w - Q_BLOCK] + bias_ref[1, h]] + near
        if nb >= 3:
            near = [logits[:w - 2 * Q_BLOCK]] + near
        return jnp.concatenate(near, axis=0) if len(near) > 1 else near[0]

    def attn_heads(gi, carry):
        hs = [gi * HEAD_GROUP + j for j in range(HEAD_GROUP)]
        mask = mask_ref[:w, :]
        logits = [jnp.dot(k_ref[0, h, :w, :], qt_ref[0, h], preferred_element_type=F32) + mask for h in hs]
        logits = [with_near_bias(l_, h) for l_, h in zip(logits, hs)]
        mx = [_max_rows(l_) for l_ in logits]
        e = [jnp.exp(l_ - m_) for l_, m_ in zip(logits, mx)]
        den = [_sum_rows(e_) for e_ in e]
        o = [jnp.dot(vt_ref[0, h, :, :w], e_.astype(BF16), preferred_element_type=F32)
             for h, e_ in zip(hs, e)]
        for h, o_, d_ in zip(hs, o, den):
            o_ref[0, h] = (o_ / d_).astype(o_ref.dtype)
        return carry

    lax.fori_loop(0, n_heads // HEAD_GROUP, attn_heads, 0)


def _dsa_prompt_body(ik_ref, iqt_ref, wt_ref, k_ref, qt_ref, vt_ref, bias_ref, o_ref, mask_ref,
                     *, n_heads, n_idx_heads, seq, topk):
    i = pl.program_id(1)
    for nb in range(1, seq // Q_BLOCK + 1):
        @pl.when(i == nb - 1)
        def _(nb=nb):
            _dsa_prompt_block(nb, ik_ref, iqt_ref, wt_ref, k_ref, qt_ref, vt_ref, bias_ref, o_ref, mask_ref,
                              n_heads=n_heads, n_idx_heads=n_idx_heads, topk=topk)


def dsa_prompt(ik, iqt, wts, k, qt, vt, bias_tiles, *, topk):
    b, h, dh, s = qt.shape
    ih = iqt.shape[1]
    kern = functools.partial(_dsa_prompt_body, n_heads=h, n_idx_heads=ih, seq=s, topk=topk)
    grid_spec = pltpu.PrefetchScalarGridSpec(
        num_scalar_prefetch=0,
        grid=(b, s // Q_BLOCK),
        in_specs=[pl.BlockSpec((1, s, ik.shape[2]), lambda bi, i: (bi, 0, 0)),
                  pl.BlockSpec((1, ih, iqt.shape[2], Q_BLOCK), lambda bi, i: (bi, 0, 0, i)),
                  pl.BlockSpec((1, ih, 1, Q_BLOCK), lambda bi, i: (bi, 0, 0, i)),
                  pl.BlockSpec((1, h, s, dh), lambda bi, i: (bi, 0, 0, 0)),
                  pl.BlockSpec((1, h, dh, Q_BLOCK), lambda bi, i: (bi, 0, 0, i)),
                  pl.BlockSpec((1, h, dh, s), lambda bi, i: (bi, 0, 0, 0)),
                  pl.BlockSpec(bias_tiles.shape, lambda bi, i: (0, 0, 0, 0))],
        out_specs=pl.BlockSpec((1, h, dh, Q_BLOCK), lambda bi, i: (bi, 0, 0, i)),
        scratch_shapes=[pltpu.VMEM((s, Q_BLOCK), F32)],
    )
    return pl.pallas_call(
        kern,
        grid_spec=grid_spec,
        out_shape=jax.ShapeDtypeStruct((b, h, dh, s), BF16),
        compiler_params=_cparams("parallel", "arbitrary"),
        name="dsa_prompt",
    )(ik, iqt, wts, k, qt, vt, bias_tiles)


def _dsa_select_body(pt_ref, iq_ref, wt_ref, iknew_ref, cidx_hbm, o_ref, ikbuf, sem,
                     *, layer, n_pages, group, n_idx_heads, t_new, topk):
    s = pl.program_id(0)
    past = n_pages * PAGE_SIZE
    n_keys = past + PAGE_SIZE

    def ik_copy(i):
        g = i // n_pages
        p = i % n_pages
        page = pt_ref[(s * group + g) * n_pages + p]
        return pltpu.make_async_copy(cidx_hbm.at[layer, page], ikbuf.at[g, p], sem)

    def ik_start(i, carry):
        ik_copy(i).start()
        return carry

    def ik_wait(i, carry):
        ik_copy(i).wait()
        return carry

    lax.fori_loop(0, group * n_pages, ik_start, 0)
    for g in range(group):
        ikbuf[g, n_pages] = iknew_ref[g]
    lax.fori_loop(0, group * n_pages, ik_wait, 0)

    scores = []
    for g in range(group):
        ikt_all = jnp.concatenate([ikbuf[g, p] for p in range(n_pages + 1)], axis=1)
        dots = _mm(iq_ref[g], ikt_all)
        weighted = jnp.maximum(dots, 0.0) * wt_ref[g]
        scores.append(jnp.sum(weighted.reshape(n_idx_heads, t_new, n_keys), axis=0))
    scores = jnp.concatenate(scores, axis=0)
    shape = (group * t_new, n_keys)
    qpos = past + lax.broadcasted_iota(I32, shape, 0) % t_new
    kpos = lax.broadcasted_iota(I32, shape, 1)
    valid = kpos <= qpos
    key = jnp.where(valid, _sortable_key(scores), INT_MIN)
    sel = valid & _topk_select(key, topk, int(math.log2(n_keys)) + 1)
    o_ref[...] = jnp.where(sel, 0.0, -jnp.inf).reshape(group, t_new, n_keys)


def dsa_sample_select(page_table, iq_rows, wt_rows, ik_new_t, cache_idx_kt, *, layer, n_idx_heads, t_new, topk,
                      group):
    db, n_pages = page_table.shape
    idx_dim = cache_idx_kt.shape[2]
    n_keys = (n_pages + 1) * PAGE_SIZE
    kern = functools.partial(_dsa_select_body, layer=layer, n_pages=n_pages, group=group,
                             n_idx_heads=n_idx_heads, t_new=t_new, topk=topk)
    per_g = lambda shape: pl.BlockSpec((group,) + shape, lambda si, pt: (si,) + (0,) * len(shape))
    grid_spec = pltpu.PrefetchScalarGridSpec(
        num_scalar_prefetch=1,
        grid=(db // group,),
        in_specs=[per_g((n_idx_heads * t_new, idx_dim)), per_g((n_idx_heads * t_new, 1)),
                  per_g((idx_dim, PAGE_SIZE)), pl.BlockSpec(memory_space=pl.ANY)],
        out_specs=per_g((t_new, n_keys)),
        scratch_shapes=[pltpu.VMEM((group, n_pages + 1, idx_dim, PAGE_SIZE), F32), pltpu.SemaphoreType.DMA(())],
    )
    return pl.pallas_call(
        kern,
        grid_spec=grid_spec,
        out_shape=jax.ShapeDtypeStruct((db, t_new, n_keys), F32),
        compiler_params=_cparams("arbitrary"),
        name="dsa_sample_select",
    )(page_table.reshape(-1), iq_rows, wt_rows, ik_new_t, cache_idx_kt)


def _dsa_sample_body(pt_ref, mask_ref, qbd_ref, knew_ref, vnew_ref,
                     blast_ref, bnew_ref, ck_hbm, cv_hbm, o_ref,
                     kbuf, vbuf, sem_k, sem_v,
                     *, layer, n_pages, chunk, n_heads, t_new):
    b = pl.program_id(0)
    n_seq = pl.num_programs(0)
    past = n_pages * PAGE_SIZE
    n_chunks = n_pages // chunk
    rows = n_heads * t_new
    ck = chunk * PAGE_SIZE

    def kv_copies(seq, c, j):
        slot = c % 2
        page = pt_ref[seq * n_pages + c * chunk + j]
        return (pltpu.make_async_copy(ck_hbm.at[layer, page], kbuf.at[slot, j], sem_k.at[slot]),
                pltpu.make_async_copy(cv_hbm.at[layer, page], vbuf.at[slot, j], sem_v.at[slot]))

    def start_chunk(seq, c):
        for j in range(chunk):
            kc, vc = kv_copies(seq, c, j)
            kc.start()
            vc.start()

    def wait_chunk(c):
        for j in range(chunk):
            kc, vc = kv_copies(b, c, j)
            kc.wait()
            vc.wait()

    @pl.when(b == 0)
    def _():
        start_chunk(0, 0)

    sel_rows = jnp.tile(mask_ref[0], (n_heads, 1))

    q_rep = jnp.tile(qbd_ref[0], (n_heads, 1))
    row_head = lax.broadcasted_iota(I32, q_rep.shape, 0) // t_new
    col_head = lax.broadcasted_iota(I32, q_rep.shape, 1) // HEAD_DIM
    qbd = jnp.where(row_head == col_head, q_rep, jnp.zeros_like(q_rep))
    neg = -1e30

    def update(state, logits, maskc, vt_bf16):
        m, l, acc = state
        s = logits + maskc
        m_new = jnp.maximum(m, jnp.max(s, axis=-1, keepdims=True))
        alpha = jnp.exp(m - m_new)
        p = jnp.exp(s - m_new)
        l = alpha * l + jnp.sum(p, axis=-1, keepdims=True)
        acc = alpha * acc + _mm_nt(p, vt_bf16)
        return m_new, l, acc

    def pages_t(buf, slot):
        return jnp.concatenate([buf[slot, j] for j in range(chunk)], axis=1).astype(BF16)

    state = (jnp.full((rows, 1), neg, F32), jnp.zeros((rows, 1), F32),
             jnp.zeros((rows, qbd.shape[1]), F32))
    for c in range(n_chunks):
        if c + 1 < n_chunks:
            start_chunk(b, c + 1)
        else:
            @pl.when(b + 1 < n_seq)
            def _():
                start_chunk(b + 1, 0)
        wait_chunk(c)
        slot = c % 2
        logits = _mm(qbd, pages_t(kbuf, slot))
        if c == n_chunks - 1:
            logits = jnp.concatenate([logits[:, :ck - PAGE_SIZE],
                                      logits[:, ck - PAGE_SIZE:] + blast_ref[...]], axis=1)
        state = update(state, logits, sel_rows[:, c * ck:(c + 1) * ck], pages_t(vbuf, slot))
    logits = _mm(qbd, knew_ref[0]) + bnew_ref[...]
    m, l, acc = update(state, logits, sel_rows[:, past:], vnew_ref[0])
    out = jnp.where(row_head == col_head, acc / l, 0.0)
    o_ref[0] = jnp.sum(out.reshape(n_heads, t_new, out.shape[1]), axis=0)


def dsa_sample(page_table, mask, q_bd, k_new_t, v_new_t, bias_last, bias_new, cache_kt, cache_vt,
               *, layer, n_heads, t_new, chunk):
    db, n_pages = page_table.shape
    assert (n_pages // chunk) % 2 == 0, "chunks alternate between two buffers across sequences"
    rows = n_heads * t_new
    width = q_bd.shape[2]
    kern = functools.partial(_dsa_sample_body, layer=layer, n_pages=n_pages, chunk=chunk, n_heads=n_heads,
                             t_new=t_new)
    per_b = lambda shape: pl.BlockSpec((1,) + shape, lambda bi, pt: (bi,) + (0,) * len(shape))
    const = lambda shape: pl.BlockSpec(shape, lambda bi, pt: (0,) * len(shape))
    any_spec = pl.BlockSpec(memory_space=pl.ANY)
    grid_spec = pltpu.PrefetchScalarGridSpec(
        num_scalar_prefetch=1,
        grid=(db,),
        in_specs=[per_b((t_new, mask.shape[2])),
                  per_b((t_new, width)), per_b((width, PAGE_SIZE)), per_b((width, PAGE_SIZE)),
                  const((rows, PAGE_SIZE)), const((rows, PAGE_SIZE)),
                  any_spec, any_spec],
        out_specs=per_b((t_new, width)),
        scratch_shapes=[pltpu.VMEM((2, chunk, width, PAGE_SIZE), F32),
                        pltpu.VMEM((2, chunk, width, PAGE_SIZE), F32),
                        pltpu.SemaphoreType.DMA((2,)),
                        pltpu.SemaphoreType.DMA((2,))],
    )
    return pl.pallas_call(
        kern,
        grid_spec=grid_spec,
        out_shape=jax.ShapeDtypeStruct((db, t_new, width), F32),
        compiler_params=_cparams("arbitrary"),
        name="dsa_sample",
    )(page_table.reshape(-1), mask, q_bd, k_new_t, v_new_t, bias_last, bias_new, cache_kt, cache_vt)


def _token_minor_cache(cache):
    l, pool, page = cache.shape[:3]
    nd = cache.ndim
    return cache.transpose((0, 1) + tuple(range(3, nd)) + (2,)).reshape(l, pool, -1, page)


def _dsa_sample_inputs(q, k_new, v_new, iq, ik_new, iw, bias_tiles):
    db, t, h, dh = q.shape
    ih = iq.shape[2]
    iq_rows = iq.transpose(0, 2, 1, 3).reshape(db, ih * t, -1).astype(BF16)
    wt_rows = (iw * (ih ** -0.5 * iq.shape[3] ** -0.5)).transpose(0, 2, 1).reshape(db, ih * t, 1)
    q_bd = (q * dh ** -0.5).reshape(db, t, h * dh)
    page_t = lambda x: jnp.pad(x.reshape(db, t, -1).transpose(0, 2, 1), ((0, 0), (0, 0), (0, PAGE_SIZE - t)))
    bias_new = bias_tiles[0, :, :t, :].reshape(h * t, Q_BLOCK)
    bias_last = bias_tiles[1, :, :t, :].reshape(h * t, Q_BLOCK)
    return ((iq_rows, wt_rows, page_t(ik_new)),
            (q_bd.astype(BF16), page_t(k_new).astype(BF16), page_t(v_new).astype(BF16), bias_last, bias_new))


def _matmul_residual_body(x_ref, a_ref, w_ref, o_ref):
    o_ref[...] = x_ref[...] + jnp.dot(a_ref[...], w_ref[...], preferred_element_type=F32)


def matmul_residual(x, a, w, *, tm, tn):
    n, d = x.shape
    kd = a.shape[1]
    return pl.pallas_call(
        _matmul_residual_body,
        grid=(n // tm, d // tn),
        in_specs=[pl.BlockSpec((tm, tn), lambda i, j: (i, j)),
                  pl.BlockSpec((tm, kd), lambda i, j: (i, 0)),
                  pl.BlockSpec((kd, tn), lambda i, j: (0, j))],
        out_specs=pl.BlockSpec((tm, tn), lambda i, j: (i, j)),
        out_shape=jax.ShapeDtypeStruct((n, d), F32),
        compiler_params=_cparams("parallel", "parallel"),
        name="matmul_residual",
    )(x, a, w)


def _cross_attn_body(x_ref, g_ref, wq_ref, mk_ref, mv_ref, wo_ref, o_ref, *, groups, t_rows, n_heads, head_dim):
    x = x_ref[...]
    h = _rmsnorm(x, g_ref[...]).astype(BF16)
    q = jnp.dot(h, wq_ref[...], preferred_element_type=F32).astype(BF16)
    scale = head_dim ** -0.5
    outs = []
    for gi in range(groups):
        qg = q[gi * t_rows:(gi + 1) * t_rows]
        heads = []
        for hh in range(n_heads):
            sl = slice(hh * head_dim, (hh + 1) * head_dim)
            logits = _mm_nt(qg[:, sl], mk_ref[gi, :, sl]) * scale
            mx = jnp.max(logits, axis=-1, keepdims=True)
            e = jnp.exp(logits - mx)
            p = e / jnp.sum(e, axis=-1, keepdims=True)
            heads.append(jnp.dot(p.astype(BF16), mv_ref[gi, :, sl], preferred_element_type=F32))
        outs.append(jnp.concatenate(heads, axis=1))
    o = jnp.concatenate(outs, axis=0) if groups > 1 else outs[0]
    o_ref[...] = x + jnp.dot(o.astype(BF16), wo_ref[...], preferred_element_type=F32)


def cross_attn(x, g, wq, mk, mv, wo, *, groups, t_rows, seq_tiles, n_heads):
    n, d = x.shape
    xw = wq.shape[1]
    rows = groups * t_rows
    m = mk.shape[1]
    kern = functools.partial(_cross_attn_body, groups=groups, t_rows=t_rows, n_heads=n_heads,
                             head_dim=xw // n_heads)
    return pl.pallas_call(
        kern,
        grid=(n // rows,),
        in_specs=[pl.BlockSpec((rows, d), lambda i: (i, 0)),
                  pl.BlockSpec((1, d), lambda i: (0, 0)),
                  pl.BlockSpec((d, xw), lambda i: (0, 0)),
                  pl.BlockSpec((groups, m, xw), lambda i: (i // seq_tiles, 0, 0)),
                  pl.BlockSpec((groups, m, xw), lambda i: (i // seq_tiles, 0, 0)),
                  pl.BlockSpec((xw, d), lambda i: (0, 0))],
        out_specs=pl.BlockSpec((rows, d), lambda i: (i, 0)),
        out_shape=jax.ShapeDtypeStruct((n, d), F32),
        compiler_params=_cparams("parallel"),
        name="cross_attn",
    )(x, g.reshape(1, d), wq, mk, mv, wo)


def _router_body(x_ref, g_ref, wr_ref, br_ref, h_ref, r_ref, *, n_groups, per_group):
    h = _rmsnorm(x_ref[...], g_ref[...])
    h_ref[...] = h
    logits = _mm3(h, wr_ref[...]) + br_ref[...]
    lane = lax.broadcasted_iota(I32, logits.shape, 1).astype(F32)
    big = 1e9
    first_lane = lambda hit: jnp.min(jnp.where(hit, lane, big), axis=-1, keepdims=True)
    gl = jnp.where(lane < n_groups, logits, -jnp.inf)
    gmax = jnp.max(gl, axis=-1, keepdims=True)
    grp = first_lane(gl == gmax)
    p_grp = 1.0 / jnp.sum(jnp.exp(gl - gmax), axis=-1, keepdims=True)
    e_id = lane - n_groups
    in_grp = (e_id >= grp * per_group) & (e_id < (grp + 1.0) * per_group)
    el = jnp.where(in_grp, logits, -jnp.inf)
    v1 = jnp.max(el, axis=-1, keepdims=True)
    i1 = first_lane(el == v1) - n_groups
    el2 = jnp.where(e_id == i1, -jnp.inf, el)
    v2 = jnp.max(el2, axis=-1, keepdims=True)
    i2 = first_lane(el2 == v2) - n_groups
    e2 = jnp.exp(v2 - v1)
    g1 = p_grp / (1.0 + e2)
    g2 = p_grp * e2 / (1.0 + e2)
    r_ref[...] = jnp.where(lane == 0, g1, jnp.where(lane == 1, g2, jnp.where(
        lane == 2, i1, jnp.where(lane == 3, i2, 0.0))))


def router(x, g, w_r, b_r, *, tm, n_groups, per_group):
    n, d = x.shape
    kern = functools.partial(_router_body, n_groups=n_groups, per_group=per_group)
    return pl.pallas_call(
        kern,
        grid=(n // tm,),
        in_specs=[pl.BlockSpec((tm, d), lambda i: (i, 0)),
                  pl.BlockSpec((1, d), lambda i: (0, 0)),
                  pl.BlockSpec((d, LANES), lambda i: (0, 0)),
                  pl.BlockSpec((1, LANES), lambda i: (0, 0))],
        out_specs=[pl.BlockSpec((tm, d), lambda i: (i, 0)),
                   pl.BlockSpec((tm, LANES), lambda i: (i, 0))],
        out_shape=[jax.ShapeDtypeStruct((n, d), F32), jax.ShapeDtypeStruct((n, LANES), F32)],
        compiler_params=_cparams("parallel"),
        name="moe_router",
    )(x, g.reshape(1, d), w_r, b_r)


def _slab_rows(buf, base, n_rows, n_slab):
    return jnp.concatenate([buf[pl.ds(base * n_slab + s, n_rows, stride=n_slab), :] for s in range(n_slab)],
                           axis=1)


def _moe_ffn_body(te_ref, tv_ref, tok_ref, h_hbm, w1_ref, w3_ref, w2_ref, o_ref, xbuf, sem, *, tm):
    t = pl.program_id(0)
    n_tiles = pl.num_programs(0)
    n_slab = h_hbm.shape[1]
    slot = t % 2

    def row_copy(tile, sl, r):
        dst = xbuf.at[sl, pl.ds(pl.multiple_of(r * n_slab, n_slab), n_slab)]
        return pltpu.make_async_copy(h_hbm.at[tok_ref[tile * tm + r]], dst, sem.at[sl])

    def start_tile(tile, sl):
        def start(r, c):
            row_copy(tile, sl, 2 * r).start(priority=0)
            row_copy(tile, sl, 2 * r + 1).start(priority=1)
            return c
        lax.fori_loop(0, tm // 2, start, 0)

    @pl.when((t == 0) & (tv_ref[0] != 0))
    def _():
        start_tile(0, 0)

    nxt = jnp.minimum(t + 1, n_tiles - 1)

    @pl.when((t + 1 < n_tiles) & (tv_ref[nxt] != 0))
    def _():
        start_tile(nxt, 1 - slot)

    @pl.when(tv_ref[t] != 0)
    def _():
        def wait(r, c):
            row_copy(t, slot, r).wait()
            return c

        lax.fori_loop(0, tm, wait, 0)
        x = _slab_rows(xbuf.at[slot], 0, tm, n_slab).astype(BF16)
        a = jnp.dot(x, w1_ref[0].astype(BF16), preferred_element_type=F32)
        bgate = jnp.dot(x, w3_ref[0].astype(BF16), preferred_element_type=F32)
        u = (a / (1.0 + jnp.exp(-a))) * bgate
        o_ref[...] = jnp.dot(u.astype(BF16), w2_ref[0].astype(BF16), preferred_element_type=F32)

    @pl.when(tv_ref[t] == 0)
    def _():
        o_ref[...] = jnp.zeros(o_ref.shape, F32)


def moe_ffn(tile_expert, tile_valid, row_tok, h, w1, w3, w2, *, tm):
    n_tiles = tile_expert.shape[0]
    d = h.shape[1]
    de = w1.shape[2]
    h = h.reshape(h.shape[0], d // LANES, LANES)
    grid_spec = pltpu.PrefetchScalarGridSpec(
        num_scalar_prefetch=3,
        grid=(n_tiles,),
        in_specs=[pl.BlockSpec(memory_space=pl.ANY),
                  pl.BlockSpec((1, d, de), lambda t, te, tv, tok: (te[t], 0, 0)),
                  pl.BlockSpec((1, d, de), lambda t, te, tv, tok: (te[t], 0, 0)),
                  pl.BlockSpec((1, de, d), lambda t, te, tv, tok: (te[t], 0, 0))],
        out_specs=pl.BlockSpec((tm, d), lambda t, te, tv, tok: (t, 0)),
        scratch_shapes=[pltpu.VMEM((2, tm * (d // LANES), LANES), F32), pltpu.SemaphoreType.DMA((2,))],
    )
    return pl.pallas_call(
        functools.partial(_moe_ffn_body, tm=tm),
        grid_spec=grid_spec,
        out_shape=jax.ShapeDtypeStruct((n_tiles * tm, d), F32),
        compiler_params=_cparams("arbitrary"),
        name="moe_ffn",
    )(tile_expert, tile_valid, row_tok, h, w1, w3, w2)


def _moe_dispatch(eids, n_experts, tm):
    n, k = eids.shape
    m = n * k
    flat_e = eids.reshape(-1)
    order = jnp.argsort(flat_e, stable=True).astype(I32)
    inv = jnp.argsort(order).astype(I32)
    onehot = flat_e[:, None] == jnp.arange(n_experts)[None, :]
    counts = jnp.sum(onehot.astype(I32), axis=0)
    padded = (counts + tm - 1) // tm * tm
    pad_end = jnp.cumsum(padded)
    pad_start = pad_end - padded
    start = jnp.cumsum(counts) - counts
    shift = jnp.sum(jnp.where(onehot, (pad_start - start)[None, :], 0), axis=1)
    pos = (inv + shift).astype(I32)
    n_tiles = -(-m // tm) + n_experts
    tile_start = jnp.arange(n_tiles) * tm
    tile_valid = (tile_start < pad_end[-1]).astype(I32)
    last = jnp.maximum(pad_end[-1] - 1, 0)
    tile_expert = jnp.minimum(jnp.searchsorted(pad_end, jnp.minimum(tile_start, last), side='right'),
                              n_experts - 1).astype(I32)
    rank = (tile_start - pad_start[tile_expert])[:, None] + jnp.arange(tm)[None, :]
    src = jnp.clip(start[tile_expert][:, None] + rank, 0, m - 1)
    row_tok = jnp.where(rank < counts[tile_expert][:, None], order[src] // k, 0)
    return tile_expert, tile_valid, row_tok.reshape(-1).astype(I32), pos


def _combine_body(pos_ref, x_ref, r_ref, g_ref, y_hbm, o_ref, ybuf, sem, *, tm, top_k):
    i = pl.program_id(0)
    n_tiles = pl.num_programs(0)
    n_slab = y_hbm.shape[1]
    slot = i % 2

    def row_copy(tile, sl, j):
        tok = j // top_k
        kk = j % top_k
        dst = ybuf.at[sl, pl.ds(pl.multiple_of((kk * tm + tok) * n_slab, n_slab), n_slab)]
        return pltpu.make_async_copy(y_hbm.at[pos_ref[tile * tm * top_k + j]], dst, sem.at[sl])

    def start_tile(tile, sl):
        def start(j, c):
            row_copy(tile, sl, 2 * j).start(priority=0)
            row_copy(tile, sl, 2 * j + 1).start(priority=1)
            return c
        lax.fori_loop(0, tm * top_k // 2, start, 0)

    @pl.when(i == 0)
    def _():
        start_tile(0, 0)

    @pl.when(i + 1 < n_tiles)
    def _():
        start_tile(i + 1, 1 - slot)

    def wait(j, c):
        row_copy(i, slot, j).wait()
        return c

    lax.fori_loop(0, tm * top_k, wait, 0)
    route = r_ref[...]
    x = x_ref[...]
    for kk in range(top_k):
        x = x + _slab_rows(ybuf.at[slot], kk * tm, tm, n_slab) * route[:, kk:kk + 1]
    o_ref[...] = _rmsnorm(x, g_ref[...])


def moe_combine(pos, x, route, g, y, *, tm, top_k):
    n, d = x.shape
    y = y.reshape(y.shape[0], d // LANES, LANES)
    grid_spec = pltpu.PrefetchScalarGridSpec(
        num_scalar_prefetch=1,
        grid=(n // tm,),
        in_specs=[pl.BlockSpec((tm, d), lambda i, p: (i, 0)),
                  pl.BlockSpec((tm, LANES), lambda i, p: (i, 0)),
                  pl.BlockSpec((1, d), lambda i, p: (0, 0)),
                  pl.BlockSpec(memory_space=pl.ANY)],
        out_specs=pl.BlockSpec((tm, d), lambda i, p: (i, 0)),
        scratch_shapes=[pltpu.VMEM((2, top_k * tm * (d // LANES), LANES), F32), pltpu.SemaphoreType.DMA((2,))],
    )
    return pl.pallas_call(
        functools.partial(_combine_body, tm=tm, top_k=top_k),
        grid_spec=grid_spec,
        out_shape=jax.ShapeDtypeStruct((n, d), F32),
        compiler_params=_cparams("arbitrary"),
        name="moe_combine",
    )(pos, x, route, g.reshape(1, d), y)


def kernel(x_prompt, x_sample, mem_prompt, cache_k, cache_v, cache_idx_k, page_table, state_wkv, state_shift, cache_mem_k, cache_mem_v, g_mix, w_in, mu_shift, rw_w0, rw_w2, rw_a0, rw_a2, rw_g2, rw_kk, rw_ka, rw_rk, rw_ln_g, rw_ln_b, w_out, g_cross, g_mem, w_cq, w_ck, w_cv, w_co, g_ffn, w_rg, b_rg, w_re, b_re, w_e1, w_e3, w_e2, rel_bias, g_final):
    B, S, D = x_prompt.shape
    DB, T, _ = x_sample.shape
    assert w_in.shape[0] == 1, "single-layer trunk only"
    l = 0
    n_pages = page_table.shape[1]
    past = n_pages * PAGE_SIZE
    topk_p = min(TOPK_MAX, S // 4)
    topk_s = min(TOPK_MAX, (past + T) // 4)
    rw_proj = mu_shift.shape[1]
    width = rw_w0.shape[1]
    at_w = D - width
    n_heads = at_w // HEAD_DIM
    idx_dim = cache_idx_k.shape[-1]
    ih = (w_in.shape[2] - rw_proj - 3 * at_w - idx_dim) // (idx_dim + 1)
    xw = w_cq.shape[2]
    x_heads = cache_mem_k.shape[3]
    n_mem = mem_prompt.shape[1]
    n_experts = w_e1.shape[1]
    top_k = 2
    tn = 512
    n_main = 3 * width
    n_lora = rw_proj - n_main
    n_att = 3 * at_w + ih * idx_dim
    n_tail = n_lora + idx_dim + ih
    assert n_main % tn == 0 and at_w % tn == 0 and (ih * idx_dim) % tn == 0 and n_tail <= tn

    w_t = w_in[l].T
    w_all = jnp.concatenate([w_t[:n_main], w_t[rw_proj:rw_proj + n_att], w_t[n_main:rw_proj],
                             w_t[rw_proj + n_att:], jnp.zeros((tn - n_tail, D), w_t.dtype)], axis=0).astype(BF16)
    segments = [(n_main // tn, F32), (at_w // tn, BF16), (at_w // tn, F32), (at_w // tn, F32),
                (ih * idx_dim // tn, BF16), (1, F32)]
    w_out_b = w_out[l].astype(BF16)
    w_cq_b, w_co_b = w_cq[l].astype(BF16), w_co[l].astype(BF16)
    w_ckv = jnp.concatenate([w_ck[l], w_cv[l]], axis=1).astype(BF16)
    n_route = w_rg.shape[2] + w_re.shape[2]
    w_r = jnp.pad(jnp.concatenate([w_rg[l], w_re[l]], axis=1), ((0, 0), (0, LANES - n_route)))
    b_r = jnp.pad(jnp.concatenate([b_rg[l], b_re[l]]), (0, LANES - n_route)).reshape(1, LANES)
    rw_args = (mu_shift[l], rw_w0[l], rw_w2[l], rw_a0[l], rw_a2[l], rw_g2[l], rw_kk[l], rw_ka[l],
               rw_rk[l].reshape(-1))
    tiles = bias_tables(rel_bias, ((0, False), (Q_BLOCK, False), (0, True), (Q_BLOCK, True)))

    def project(x2d, b_, t_, tm):
        outs = norm_matmul_split(x2d, g_mix[l], w_all, segments, tm=tm, tn=tn)
        f_main, q, k, v, iq, f_tail = [z.reshape(b_, t_, -1) for z in outs]
        ik = f_tail[..., n_lora:n_lora + idx_dim]
        iw = f_tail[..., n_lora + idx_dim:n_tail]
        shift = jnp.concatenate([f_main[:, -1], f_tail[:, -1, :n_lora]], axis=-1)
        return f_main, q, k, v, iq, f_tail, ik, iw, shift

    def rw_rows(y):
        b_, p_, t_, _ = y.shape
        return y.transpose(0, 2, 1, 3).reshape(b_ * t_, p_ * PAIR).astype(BF16)

    xp = x_prompt.reshape(B * S, D)
    fm_p, q, k_p, v_p, iq, ft_p, ik_p, iw, shift_p = project(xp, B, S, 1024)
    feats_p = rwkv_prep(fm_p, ft_p, jnp.zeros((B, rw_proj), F32), *rw_args, tm=256, width=width)
    rw_p, st_p = rwkv_chunk(feats_p, jnp.zeros((B, width // PAIR, PAIR, PAIR), F32), rw_ln_g[l], rw_ln_b[l], L=64)
    hm = lambda z, h_: z.reshape(B, S, h_, -1).transpose(0, 2, 1, 3).astype(BF16)
    hmt = lambda z, h_: z.reshape(B, S, h_, -1).transpose(0, 2, 3, 1).astype(BF16)
    at_p = dsa_prompt(ik_p.astype(BF16), hmt(iq, ih),
                      (iw * (ih ** -0.5 * idx_dim ** -0.5)).transpose(0, 2, 1)[:, :, None, :],
                      hm(k_p, n_heads), hmt(q * HEAD_DIM ** -0.5, n_heads), hmt(v_p, n_heads),
                      tiles[2:4], topk=topk_p)
    mix_p = jnp.concatenate([rw_rows(rw_p), at_p.transpose(0, 3, 1, 2).reshape(B * S, at_w)], axis=1)
    x1_p = matmul_residual(xp, mix_p, w_out_b, tm=512, tn=512)
    mkv = norm_matmul(mem_prompt.reshape(B * n_mem, D), g_mem[l], w_ckv, tm=256, tn=512)
    mk_p = mkv[:, :xw].reshape(B, n_mem, xw)
    mv_p = mkv[:, xw:].reshape(B, n_mem, xw)
    x2_p = cross_attn(x1_p, g_cross[l], w_cq_b, mk_p.astype(BF16), mv_p.astype(BF16), w_co_b,
                      groups=1, t_rows=512, seq_tiles=S // 512, n_heads=x_heads)

    xs = x_sample.reshape(DB * T, D)
    fm_s, q2, k_s, v_s, iq2, ft_s, ik_s, iw2, shift_s = project(xs, DB, T, DB * T)
    feats_s = rwkv_prep(fm_s, ft_s, state_shift[l], *rw_args, tm=T, width=width)
    rw_s, st_s = rwkv_chunk(feats_s, _state_to_block_diag(state_wkv[l]), rw_ln_g[l], rw_ln_b[l], L=T)
    r4 = lambda z, h_: z.reshape(DB, T, h_, -1)
    sel_args, att_args = _dsa_sample_inputs(r4(q2, n_heads), r4(k_s, n_heads), r4(v_s, n_heads), r4(iq2, ih),
                                            ik_s, iw2, tiles[0:2])
    mask_s = dsa_sample_select(page_table, *sel_args, _token_minor_cache(cache_idx_k), layer=l,
                               n_idx_heads=ih, t_new=T, topk=topk_s, group=4)
    at_s = dsa_sample(page_table, mask_s, *att_args, _token_minor_cache(cache_k), _token_minor_cache(cache_v),
                      layer=l, n_heads=n_heads, t_new=T, chunk=8)
    mix_s = jnp.concatenate([rw_rows(rw_s), at_s.reshape(DB * T, at_w).astype(BF16)], axis=1)
    x1_s = matmul_residual(xs, mix_s, w_out_b, tm=DB * T, tn=512)
    x2_s = cross_attn(x1_s, g_cross[l], w_cq_b, cache_mem_k[l].reshape(DB, n_mem, xw).astype(BF16),
                      cache_mem_v[l].reshape(DB, n_mem, xw).astype(BF16), w_co_b,
                      groups=8, t_rows=T, seq_tiles=1, n_heads=x_heads)

    h_p, route_p = router(x2_p, g_ffn[l], w_r, b_r, tm=512, n_groups=w_rg.shape[2],
                          per_group=w_re.shape[2] // w_rg.shape[2])
    h_s, route_s = router(x2_s, g_ffn[l], w_r, b_r, tm=DB * T, n_groups=w_rg.shape[2],
                          per_group=w_re.shape[2] // w_rg.shape[2])
    h_all = jnp.concatenate([h_p, h_s], axis=0)
    eids = jnp.concatenate([route_p[:, top_k:2 * top_k], route_s[:, top_k:2 * top_k]], axis=0).astype(I32)
    tile_expert, tile_valid, row_tok, pos = _moe_dispatch(eids, n_experts, 256)
    y_rows = moe_ffn(tile_expert, tile_valid, row_tok, h_all, w_e1[l], w_e3[l], w_e2[l], tm=256)
    n_p = B * S
    y_p = moe_combine(pos[:n_p * top_k], x2_p, route_p, g_final, y_rows, tm=256, top_k=top_k)
    y_s = moe_combine(pos[n_p * top_k:], x2_s, route_s, g_final, y_rows, tm=DB * T, top_k=top_k)

    hd = lambda z, b_, t_: z.reshape(1, b_, t_, n_heads, HEAD_DIM)
    return (y_p.reshape(B, S, D), y_s.reshape(DB, T, D),
            hd(k_p, B, S), hd(v_p, B, S), ik_p[None], _state_from_block_diag(st_p)[None], shift_p[None],
            mk_p.reshape(1, B, n_mem, x_heads, xw // x_heads), mv_p.reshape(1, B, n_mem, x_heads, xw // x_heads),
            hd(k_s, DB, T), hd(v_s, DB, T), ik_s[None], _state_from_block_diag(st_s)[None], shift_s[None])
```

```python
import functools
import math

import jax
import jax.numpy as jnp
from jax import lax
from jax.experimental import pallas as pl
from jax.experimental.pallas import tpu as pltpu

F32 = jnp.float32
BF16 = jnp.bfloat16
I32 = jnp.int32

LANES = 128
SUBLANES = 8
VMEM_LIMIT_BYTES = 56 * 1024 * 1024

HEAD_DIM = 64
PAIR = 2 * HEAD_DIM
GN_EPS = 64e-5
NORM_EPS = 1e-6
TOPK_MAX = 256
Q_BLOCK = 128
N_BUCKETS = 32
MAX_DISTANCE = 128
PAGE_SIZE = 128
N_GROUPS = 4
EXPERTS_PER_GROUP = 8
INT_MIN = -(2 ** 31)


def _cparams(*sem):
    return pltpu.CompilerParams(dimension_semantics=sem, vmem_limit_bytes=VMEM_LIMIT_BYTES)


def _mm(a, b):
    return jnp.dot(a.astype(BF16), b.astype(BF16), preferred_element_type=F32)


def _mm_nt(a, b):
    return lax.dot_general(a.astype(BF16), b.astype(BF16), (((1,), (1,)), ((), ())),
                           preferred_element_type=F32)


def _split2(x):
    hi = x.astype(BF16)
    lo = (x - hi.astype(F32)).astype(BF16)
    return hi, lo


def _mm3(a, b):
    ah, al = _split2(a)
    bh, bl = _split2(b)
    d = lambda x, y: jnp.dot(x, y, preferred_element_type=F32)
    return d(ah, bh) + (d(ah, bl) + d(al, bh))


def _mm_exact_rhs(a, b_bf16):
    hi = a.astype(BF16)
    r1 = a - hi.astype(F32)
    mid = r1.astype(BF16)
    lo = (r1 - mid.astype(F32)).astype(BF16)
    d = lambda x: jnp.dot(x, b_bf16, preferred_element_type=F32)
    return d(hi) + (d(mid) + d(lo))


def _rmsnorm(x, g):
    ms = jnp.mean(x * x, axis=-1, keepdims=True)
    return x * lax.rsqrt(ms + NORM_EPS) * g


def _norm_matmul_body(x_ref, g_ref, w_ref, o_ref, xn_ref):
    @pl.when(pl.program_id(1) == 0)
    def _():
        xn_ref[...] = _rmsnorm(x_ref[...], g_ref[...]).astype(BF16)

    o_ref[...] = _mm(xn_ref[...], w_ref[...])


def norm_matmul(x, g, w, *, tm, tn):
    n, d = x.shape
    m = w.shape[1]
    return pl.pallas_call(
        _norm_matmul_body,
        grid=(n // tm, m // tn),
        in_specs=[pl.BlockSpec((tm, d), lambda i, j: (i, 0)),
                  pl.BlockSpec((1, d), lambda i, j: (0, 0)),
                  pl.BlockSpec((d, tn), lambda i, j: (0, j))],
        out_specs=pl.BlockSpec((tm, tn), lambda i, j: (i, j)),
        out_shape=jax.ShapeDtypeStruct((n, m), F32),
        scratch_shapes=[pltpu.VMEM((tm, d), BF16)],
        compiler_params=_cparams("parallel", "arbitrary"),
        name="norm_matmul",
    )(x, g.reshape(1, d), w)


def _norm_matmul_split_body(x_ref, g_ref, w_ref, *rest, bounds):
    o_refs, xn_ref = rest[:-1], rest[-1]
    j = pl.program_id(1)

    @pl.when(j == 0)
    def _():
        xn_ref[...] = _rmsnorm(x_ref[...], g_ref[...]).astype(BF16)

    res = _mm_nt(xn_ref[...], w_ref[...])
    for o_ref, (lo, hi) in zip(o_refs, bounds):
        @pl.when((j >= lo) & (j < hi))
        def _(o_ref=o_ref):
            if len(o_ref.shape) == 3:
                for hh in range(o_ref.shape[1]):
                    o_ref[:, hh, :] = res[:, hh * HEAD_DIM:(hh + 1) * HEAD_DIM].astype(o_ref.dtype)
            else:
                o_ref[...] = res.astype(o_ref.dtype)


def norm_matmul_split(x, g, wt, segments, *, tm, tn):
    n, d = x.shape
    m = wt.shape[0]
    bounds, out_specs, out_shape, lo = [], [], [], 0
    for nt, outs in segments:
        hi = lo + nt
        for per_head, dt in outs:
            bounds.append((lo, hi))
            col = lambda j, lo=lo, hi=hi: jnp.clip(j - lo, 0, hi - lo - 1)
            if per_head:
                out_specs.append(pl.BlockSpec((tm, tn // HEAD_DIM, HEAD_DIM), lambda i, j, col=col: (i, col(j), 0)))
                out_shape.append(jax.ShapeDtypeStruct((n, nt * tn // HEAD_DIM, HEAD_DIM), dt))
            else:
                out_specs.append(pl.BlockSpec((tm, tn), lambda i, j, col=col: (i, col(j))))
                out_shape.append(jax.ShapeDtypeStruct((n, nt * tn), dt))
        lo = hi
    assert lo * tn == m
    return pl.pallas_call(
        functools.partial(_norm_matmul_split_body, bounds=tuple(bounds)),
        grid=(n // tm, m // tn),
        in_specs=[pl.BlockSpec((tm, d), lambda i, j: (i, 0)),
                  pl.BlockSpec((1, d), lambda i, j: (0, 0)),
                  pl.BlockSpec((tn, d), lambda i, j: (j, 0))],
        out_specs=out_specs,
        out_shape=out_shape,
        scratch_shapes=[pltpu.VMEM((tm, d), BF16)],
        compiler_params=_cparams("arbitrary", "arbitrary"),
        name="norm_matmul_split",
    )(x, g.reshape(1, d), wt)


def _pair_ones():
    r = lax.broadcasted_iota(I32, (PAIR, PAIR), 0) // HEAD_DIM
    c = lax.broadcasted_iota(I32, (PAIR, PAIR), 1) // HEAD_DIM
    return (r == c).astype(BF16)


def _head_sum(x, ones_bd):
    return _mm_exact_rhs(x, ones_bd)


def _rwkv_prep_body(f_ref, prev8_ref, init_ref, mu_ref, ft_ref, tprev8_ref, tinit_ref, tmu_ref,
                    w0_ref, w2_ref, a0_ref, a2_ref, g2_ref, kk_ref, ka_ref, rk_ref,
                    r_o, k_o, v_o, kk_o, b_o, ld_o, g_o, bon_o, *, tm, width):
    i = pl.program_id(1)

    def token_shift(f, p8_ref, i_ref, m_ref):
        nc = m_ref.shape[1]
        prev_row = jnp.where(i == 0, i_ref[0], p8_ref[0, SUBLANES - 1:SUBLANES, :nc])
        rolled = pltpu.roll(f, shift=1, axis=0)
        row = lax.broadcasted_iota(I32, f.shape, 0)
        f_prev = jnp.where(row == 0, prev_row, rolled)
        return f + (f_prev - f) * m_ref[...]

    fs = token_shift(f_ref[0], prev8_ref, init_ref, mu_ref)
    n_dec = w2_ref.shape[0]
    n_icl = a2_ref.shape[0]
    n_lora = tmu_ref.shape[1]
    ts = token_shift(ft_ref[0][:, :n_lora], tprev8_ref, tinit_ref, tmu_ref)
    w_ = width
    r = fs[:, 0:w_]
    k = fs[:, w_:2 * w_]
    v = fs[:, 2 * w_:3 * w_]
    wd = ts[:, 0:n_dec]
    ad = ts[:, n_dec:n_dec + n_icl]
    gd = ts[:, n_dec + n_icl:]
    z = w0_ref[...] + _mm3(jnp.tanh(wd), w2_ref[...])
    nz = -z
    softplus = jnp.maximum(nz, 0.0) + jnp.log(1.0 + jnp.exp(-jnp.abs(nz)))
    w = -softplus - 0.5
    ld = -jnp.exp(w)
    a = 1.0 / (1.0 + jnp.exp(-(a0_ref[...] + _mm3(ad, a2_ref[...]))))
    g = _mm3(1.0 / (1.0 + jnp.exp(-gd)), g2_ref[...])
    kk = k * kk_ref[...]
    k2 = k * (1.0 + (a - 1.0) * ka_ref[...])
    rk = r * k2 * rk_ref[...]
    ones_bd = _pair_ones()
    for p in range(w_ // PAIR):
        sl = slice(p * PAIR, (p + 1) * PAIR)
        kkp = kk[:, sl]
        nrm = jnp.sqrt(_head_sum(kkp * kkp, ones_bd))
        kkp = kkp / jnp.maximum(nrm, 1e-12)
        ap = a[:, sl]
        r_o[0, p] = r[:, sl]
        k_o[0, p] = k2[:, sl]
        v_o[0, p] = v[:, sl]
        kk_o[0, p] = kkp
        b_o[0, p] = kkp * ap
        ld_o[0, p] = ld[:, sl]
        g_o[0, p] = g[:, sl]
        bon_o[0, p] = _head_sum(rk[:, sl], ones_bd) * v[:, sl]


def rwkv_prep(f_main, f_tail, init_prev, mu, w0, w2, a0, a2, g2, k_k, k_a, r_k, *, tm, width):
    b, t, n_main = f_main.shape
    n_tail = f_tail.shape[2]
    n_lora = mu.shape[0] - n_main
    npair = width // PAIR
    row1 = lambda x: x.reshape(1, -1)
    kern = functools.partial(_rwkv_prep_body, tm=tm, width=width)
    full = lambda a: pl.BlockSpec(a.shape, lambda bi, i: (0,) * a.ndim)
    args = [row1(w0), w2, row1(a0), a2, g2, row1(k_k), row1(k_a), row1(r_k)]
    out_spec = pl.BlockSpec((1, npair, tm, PAIR), lambda bi, i: (bi, 0, i, 0))
    out_shape = jax.ShapeDtypeStruct((b, npair, t, PAIR), F32)
    prev8_map = lambda bi, i: (bi, jnp.maximum(i * (tm // SUBLANES) - 1, 0), 0)

    def feature_specs(ncols, n_init):
        return [pl.BlockSpec((1, tm, ncols), lambda bi, i: (bi, i, 0)),
                pl.BlockSpec((1, SUBLANES, ncols), prev8_map),
                pl.BlockSpec((1, 1, n_init), lambda bi, i: (bi, 0, 0)),
                pl.BlockSpec((1, n_init), lambda bi, i: (0, 0))]

    return pl.pallas_call(
        kern,
        grid=(b, t // tm),
        in_specs=feature_specs(n_main, n_main) + feature_specs(n_tail, n_lora) + [full(a) for a in args],
        out_specs=[out_spec] * 8,
        out_shape=[out_shape] * 8,
        compiler_params=_cparams("parallel", "parallel"),
        name="rwkv_prep",
    )(f_main, f_main, init_prev[:, :n_main].reshape(b, 1, n_main), row1(mu[:n_main]),
      f_tail, f_tail, init_prev[:, n_main:].reshape(b, 1, n_lora), row1(mu[n_main:]), *args)


def _rwkv_chunk_body(r_ref, k_ref, v_ref, kk_ref, b_ref, ld_ref, g_ref, bon_ref, s0_ref,
                     lng_ref, lnb_ref, o_ref, st_ref, s_ref, *, L, npair, group):
    c = pl.program_id(1)

    @pl.when(c == 0)
    def _():
        s_ref[...] = s0_ref[0]

    L2 = 2 * L
    row = lax.broadcasted_iota(I32, (L2, L2), 0)
    col = lax.broadcasted_iota(I32, (L2, L2), 1)
    same = (row // L) == (col // L)
    tri_strict = same & (col < row)
    tri_incl = same & (col <= row)
    eye = (row == col).astype(F32)
    tr = lax.broadcasted_iota(I32, (L, L), 0)
    tc = lax.broadcasted_iota(I32, (L, L), 1)
    cum_mat = (tc <= tr).astype(BF16)
    lane = lax.broadcasted_iota(I32, (L, PAIR), 1)
    first = lane < HEAD_DIM
    ones_bd = _pair_ones()
    n_sq = max(int(math.ceil(math.log2(L))) - 1, 0)

    def block_diag(x):
        return jnp.concatenate([jnp.where(first, x, 0.0), jnp.where(first, 0.0, x)], axis=0)

    def group_step(gi, carry):
        ps = [gi * group + j for j in range(group)]
        each = lambda f, *cols: [f(*args) for args in zip(*cols)]
        ld = [ld_ref[0, p] for p in ps]
        cum = each(lambda x: _mm_exact_rhs_t(cum_mat, x), ld)
        dec = each(jnp.exp, cum)
        dec_inv = each(lambda c_: jnp.exp(-c_), cum)
        a_t = each(lambda p, c_, l_: block_diag(-kk_ref[0, p] * jnp.exp(c_ - l_)), ps, cum, ld)
        b_t = each(lambda p, e: block_diag(b_ref[0, p] * e), ps, dec_inv)
        k_t = each(lambda p, e: block_diag(k_ref[0, p] * e), ps, dec_inv)
        r_t = each(lambda p, e: block_diag(r_ref[0, p] * e), ps, dec)
        v_b = each(lambda p: block_diag(v_ref[0, p]), ps)
        a_ab = each(lambda a, b: jnp.where(tri_strict, _mm_nt(a, b), 0.0), a_t, b_t)
        a_ak = each(lambda a, k: jnp.where(tri_strict, _mm_nt(a, k), 0.0), a_t, k_t)
        a_rb = each(lambda r, b: jnp.where(tri_incl, _mm_nt(r, b), 0.0), r_t, b_t)
        a_rk = each(lambda r, k: jnp.where(tri_incl, _mm_nt(r, k), 0.0), r_t, k_t)
        x = a_ab
        t_inv = each(lambda a: eye + a, a_ab)
        for _ in range(n_sq):
            x = each(lambda x_: _mm(x_, x_), x)
            t_inv = each(lambda t, x_: t + _mm(x_, t), t_inv, x)
        s = [s_ref[p] for p in ps]
        rhs = each(lambda a, s_, ak, v: _mm_nt(a, s_) + _mm(ak, v), a_t, s, a_ak, v_b)
        u = each(_mm, t_inv, rhs)
        y_b = each(lambda r, s_, rb, u_, rk, v: _mm_nt(r, s_) + _mm(rb, u_) + _mm(rk, v),
                   r_t, s, a_rb, u, a_rk, v_b)
        s_new = each(lambda s_, u_, b, v, k, d: (s_ + _mm(u_.T, b) + _mm(v.T, k)) * d[L - 1:L, :],
                     s, u, b_t, v_b, k_t, dec)
        for p, sn in zip(ps, s_new):
            s_ref[p] = sn
        y = each(lambda yb: yb[:L] + yb[L:], y_b)
        mean = each(lambda y_: _head_sum(y_, ones_bd) * (1.0 / HEAD_DIM), y)
        d = each(lambda y_, m: y_ - m, y, mean)
        var = each(lambda d_: _head_sum(d_ * d_, ones_bd) * (1.0 / HEAD_DIM), d)
        for p, d_, v_ in zip(ps, d, var):
            yn = d_ * lax.rsqrt(v_ + GN_EPS) * lng_ref[p] + lnb_ref[p]
            o_ref[0, p] = (yn + bon_ref[0, p]) * g_ref[0, p]
        return carry

    lax.fori_loop(0, npair // group, group_step, 0)

    @pl.when(c == pl.num_programs(1) - 1)
    def _():
        st_ref[0] = s_ref[...]


def _mm_exact_rhs_t(m_bf16, x):
    hi = x.astype(BF16)
    r1 = x - hi.astype(F32)
    mid = r1.astype(BF16)
    lo = (r1 - mid.astype(F32)).astype(BF16)
    d = lambda y: jnp.dot(m_bf16, y, preferred_element_type=F32)
    return d(hi) + (d(mid) + d(lo))


def rwkv_chunk(feats, s0_bd, ln_g, ln_b, *, L, group=8):
    b, npair, t, _ = feats[0].shape
    blk = pl.BlockSpec((1, npair, L, PAIR), lambda bi, c: (bi, 0, c, 0))
    st_spec = pl.BlockSpec((1, npair, PAIR, PAIR), lambda bi, c: (bi, 0, 0, 0))
    par_spec = pl.BlockSpec((npair, 1, PAIR), lambda bi, c: (0, 0, 0))
    kern = functools.partial(_rwkv_chunk_body, L=L, npair=npair, group=group)
    return pl.pallas_call(
        kern,
        grid=(b, t // L),
        in_specs=[blk] * 8 + [st_spec, par_spec, par_spec],
        out_specs=[blk, st_spec],
        out_shape=[jax.ShapeDtypeStruct((b, npair, t, PAIR), F32),
                   jax.ShapeDtypeStruct((b, npair, PAIR, PAIR), F32)],
        scratch_shapes=[pltpu.VMEM((npair, PAIR, PAIR), F32)],
        compiler_params=_cparams("parallel", "arbitrary"),
        name="rwkv_chunk",
    )(*feats, s0_bd, ln_g.reshape(npair, 1, PAIR), ln_b.reshape(npair, 1, PAIR))


def _state_to_block_diag(s):
    b, h, n, _ = s.shape
    s = s.reshape(b, h // 2, 2, n, n)
    z = jnp.zeros_like(s[:, :, 0])
    top = jnp.concatenate([s[:, :, 0], z], axis=-1)
    bot = jnp.concatenate([z, s[:, :, 1]], axis=-1)
    return jnp.concatenate([top, bot], axis=-2)


def _state_from_block_diag(s_bd):
    b, p, _, _ = s_bd.shape
    n = HEAD_DIM
    return jnp.stack([s_bd[:, :, :n, :n], s_bd[:, :, n:, n:]], axis=2).reshape(b, 2 * p, n, n)


def _t5_bucket(dist):
    exact = N_BUCKETS // 2
    d = jnp.maximum(dist, 0)
    far = exact + (jnp.log(jnp.maximum(d, 1).astype(F32) / exact) / math.log(MAX_DISTANCE / exact)
                   * (N_BUCKETS - exact)).astype(I32)
    return jnp.where(d < exact, d, jnp.minimum(far, N_BUCKETS - 1))


def _bias_tables_body(rb_ref, o_ref, *, offsets, n_heads):
    r = lax.broadcasted_iota(I32, (Q_BLOCK, Q_BLOCK), 0)
    c = lax.broadcasted_iota(I32, (Q_BLOCK, Q_BLOCK), 1)
    for t, (off, key_major) in enumerate(offsets):
        bucket = _t5_bucket((c - r if key_major else r - c) + off)
        for h in range(n_heads):
            def body(bk, acc):
                return jnp.where(bucket == bk, rb_ref[bk, h], acc)
            tile = lax.fori_loop(0, N_BUCKETS, body, jnp.zeros((Q_BLOCK, Q_BLOCK), F32))
            o_ref[t, h] = tile - rb_ref[N_BUCKETS - 1, h]


def bias_tables(rel_bias, offsets):
    n_heads = rel_bias.shape[1]
    kern = functools.partial(_bias_tables_body, offsets=tuple(offsets), n_heads=n_heads)
    return pl.pallas_call(
        kern,
        in_specs=[pl.BlockSpec(memory_space=pltpu.SMEM)],
        out_specs=pl.BlockSpec(memory_space=pltpu.VMEM),
        out_shape=jax.ShapeDtypeStruct((len(offsets), n_heads, Q_BLOCK, Q_BLOCK), F32),
        name="bias_tables",
    )(rel_bias)


def _sortable_key(scores):
    bits = lax.bitcast_convert_type(scores + 0.0, I32)
    return jnp.where(bits < 0, bits ^ 0x7FFFFFFF, bits)


def _count(mask):
    return jnp.sum(mask.astype(F32), axis=-1, keepdims=True)


def _topk_select(key, topk, n_index_bits):
    rows, n = key.shape
    kf = float(topk)
    t0 = jnp.where(_count(key >= 0) >= kf, 0, INT_MIN).astype(I32)

    def value_bit(i, t):
        cand = t + lax.shift_left(jnp.int32(1), 30 - i)
        return jnp.where(_count(key >= cand) >= kf, cand, t)

    thr = lax.fori_loop(0, 31, value_bit, t0)
    above = key > thr
    ties = key == thr
    need = kf - _count(above)
    idx = lax.broadcasted_iota(I32, (rows, n), 1)

    def lowest_ties():
        def index_bit(i, m):
            cand = m + lax.shift_left(jnp.int32(1), n_index_bits - 1 - i)
            return jnp.where(_count(ties & (idx < cand)) <= need, cand, m)
        return lax.fori_loop(0, n_index_bits, index_bit, jnp.zeros((rows, 1), I32))

    surplus = jnp.max(_count(ties) - need) > 0.0
    m = lax.cond(surplus, lowest_ties, lambda: jnp.full((rows, 1), 2 ** n_index_bits, I32))
    return above | (ties & (idx < m))


SUM_CHAINS = 4
HEAD_GROUP = 4


def _sum_rows(x):
    r = x.shape[0]
    if r % (SUM_CHAINS * SUBLANES) == 0 and r > SUM_CHAINS * SUBLANES:
        x = jnp.sum(x.reshape(SUM_CHAINS, r // SUM_CHAINS, x.shape[1]), axis=1)
    return jnp.sum(x, axis=0, keepdims=True)


def _max_rows(x):
    r = x.shape[0]
    if r % (SUM_CHAINS * SUBLANES) == 0 and r > SUM_CHAINS * SUBLANES:
        x = jnp.max(x.reshape(SUM_CHAINS, r // SUM_CHAINS, x.shape[1]), axis=1)
    return jnp.max(x, axis=0, keepdims=True)


def _topk_select_cols(key, topk, n_index_bits):
    n, cols = key.shape
    kf = float(topk)
    cnt = lambda m: _sum_rows(m.astype(F32))
    t0 = jnp.where(cnt(key >= 0) >= kf, 0, INT_MIN).astype(I32)

    def value_bit(i, t):
        cand = t + lax.shift_left(jnp.int32(1), 30 - i)
        return jnp.where(cnt(key >= cand) >= kf, cand, t)

    thr = lax.fori_loop(0, 31, value_bit, t0)
    above = key > thr
    ties = key == thr
    need = kf - cnt(above)
    idx = lax.broadcasted_iota(I32, (n, cols), 0)

    def lowest_ties():
        def index_bit(i, m):
            cand = m + lax.shift_left(jnp.int32(1), n_index_bits - 1 - i)
            return jnp.where(cnt(ties & (idx < cand)) <= need, cand, m)
        return lax.fori_loop(0, n_index_bits, index_bit, jnp.zeros((1, cols), I32))

    surplus = jnp.max(cnt(ties) - need) > 0.0
    m = lax.cond(surplus, lowest_ties, lambda: jnp.full((1, cols), 2 ** n_index_bits, I32))
    return above | (ties & (idx < m))


def _dsa_prompt_block(nb, ik_ref, iqt_ref, wt_ref, k_ref, qt_ref, vt_ref, bias_ref, o_ref, mask_ref,
                      *, n_heads, n_idx_heads, topk):
    w = nb * Q_BLOCK
    ik = ik_ref[0, :w, :]
    idx_dim = ik.shape[1]

    def head_rows(ref, h, dh, cols=slice(None)):
        return ref[0, pl.ds(pl.multiple_of(h * dh, dh), dh), cols]

    def idx_heads(gi, acc):
        hs = [gi * HEAD_GROUP + j for j in range(HEAD_GROUP)]
        dots = [jnp.dot(ik, head_rows(iqt_ref, h, idx_dim), preferred_element_type=F32) for h in hs]
        terms = [jnp.maximum(d, 0.0) * wt_ref[0, h] for d, h in zip(dots, hs)]
        return acc + ((terms[0] + terms[1]) + (terms[2] + terms[3]))

    scores = lax.fori_loop(0, n_idx_heads // HEAD_GROUP, idx_heads, jnp.zeros((w, Q_BLOCK), F32))
    kpos = lax.broadcasted_iota(I32, (w, Q_BLOCK), 0)
    qpos = (nb - 1) * Q_BLOCK + lax.broadcasted_iota(I32, (w, Q_BLOCK), 1)
    valid = kpos <= qpos
    if w <= topk:
        sel = valid
    else:
        key = jnp.where(valid, _sortable_key(scores), INT_MIN)
        sel = valid & _topk_select_cols(key, topk, int(math.ceil(math.log2(w))) + 1)
    mask_ref[:w, :] = jnp.where(sel, 0.0, -jnp.inf)

    def with_near_bias(logits, h):
        near = [logits[w - Q_BLOCK:] + bias_ref[0, h]]
        if nb >= 2:
            near = [logits[w - 2 * Q_BLOCK:w - Q_BLOCK] + bias_ref[1, h]] + near
        if nb >= 3:
            near = [logits[:w - 2 * Q_BLOCK]] + near
        return jnp.concatenate(near, axis=0) if len(near) > 1 else near[0]

    pair_row_half = lax.broadcasted_iota(I32, (PAIR, Q_BLOCK), 0) // HEAD_DIM

    def head_logits(gi, j):
        p = gi * (HEAD_GROUP // 2) + j // 2
        qt_pair = head_rows(qt_ref, p, PAIR)
        qt_head = jnp.where(pair_row_half == j % 2, qt_pair, jnp.zeros_like(qt_pair))
        return jnp.dot(k_ref[0, p, :w, :], qt_head, preferred_element_type=F32)

    def attn_heads(gi, carry):
        hs = [gi * HEAD_GROUP + j for j in range(HEAD_GROUP)]
        mask = mask_ref[:w, :]
        logits = [head_logits(gi, j) + mask for j in range(HEAD_GROUP)]
        logits = [with_near_bias(l_, h) for l_, h in zip(logits, hs)]
        mx = [_max_rows(l_) for l_ in logits]
        e = [jnp.exp(l_ - m_) for l_, m_ in zip(logits, mx)]
        den = [_sum_rows(e_) for e_ in e]
        o = [jnp.dot(head_rows(vt_ref, h, HEAD_DIM, slice(0, w)), e_.astype(BF16),
                     preferred_element_type=F32) for h, e_ in zip(hs, e)]
        for h, o_, d_ in zip(hs, o, den):
            o_ref[0, pl.ds(pl.multiple_of(h * HEAD_DIM, HEAD_DIM), HEAD_DIM), :] = (o_ / d_).astype(o_ref.dtype)
        return carry

    lax.fori_loop(0, n_heads // HEAD_GROUP, attn_heads, 0)


def _dsa_prompt_body(ik_ref, iqt_ref, wt_ref, k_ref, qt_ref, vt_ref, bias_ref, o_ref, mask_ref,
                     *, n_heads, n_idx_heads, seq, topk):
    i = pl.program_id(1)
    for nb in range(1, seq // Q_BLOCK + 1):
        @pl.when(i == nb - 1)
        def _(nb=nb):
            _dsa_prompt_block(nb, ik_ref, iqt_ref, wt_ref, k_ref, qt_ref, vt_ref, bias_ref, o_ref, mask_ref,
                              n_heads=n_heads, n_idx_heads=n_idx_heads, topk=topk)


def dsa_prompt(ik, iqt, wts, k_pairs, qt, vt, bias_tiles, *, topk):
    b, width, s = qt.shape
    h = width // HEAD_DIM
    ih = wts.shape[1]
    kern = functools.partial(_dsa_prompt_body, n_heads=h, n_idx_heads=ih, seq=s, topk=topk)
    grid_spec = pltpu.PrefetchScalarGridSpec(
        num_scalar_prefetch=0,
        grid=(b, s // Q_BLOCK),
        in_specs=[pl.BlockSpec((1, s, ik.shape[2]), lambda bi, i: (bi, 0, 0)),
                  pl.BlockSpec((1, iqt.shape[1], Q_BLOCK), lambda bi, i: (bi, 0, i)),
                  pl.BlockSpec((1, ih, 1, Q_BLOCK), lambda bi, i: (bi, 0, 0, i)),
                  pl.BlockSpec((1, h // 2, s, PAIR), lambda bi, i: (bi, 0, 0, 0)),
                  pl.BlockSpec((1, width, Q_BLOCK), lambda bi, i: (bi, 0, i)),
                  pl.BlockSpec((1, width, s), lambda bi, i: (bi, 0, 0)),
                  pl.BlockSpec(bias_tiles.shape, lambda bi, i: (0, 0, 0, 0))],
        out_specs=pl.BlockSpec((1, width, Q_BLOCK), lambda bi, i: (bi, 0, i)),
        scratch_shapes=[pltpu.VMEM((s, Q_BLOCK), F32)],
    )
    return pl.pallas_call(
        kern,
        grid_spec=grid_spec,
        out_shape=jax.ShapeDtypeStruct((b, width, s), BF16),
        compiler_params=_cparams("parallel", "arbitrary"),
        name="dsa_prompt",
    )(ik, iqt, wts, k_pairs, qt, vt, bias_tiles)


def _dsa_select_body(pt_ref, iq_ref, wt_ref, iknew_ref, cidx_hbm, o_ref, ikbuf, sem,
                     *, layer, n_pages, group, n_idx_heads, t_new, topk):
    s = pl.program_id(0)
    past = n_pages * PAGE_SIZE
    n_keys = past + PAGE_SIZE

    def ik_copy(i):
        g = i // n_pages
        p = i % n_pages
        page = pt_ref[(s * group + g) * n_pages + p]
        return pltpu.make_async_copy(cidx_hbm.at[layer, page], ikbuf.at[g, p], sem)

    def ik_start(i, carry):
        ik_copy(i).start()
        return carry

    def ik_wait(i, carry):
        ik_copy(i).wait()
        return carry

    lax.fori_loop(0, group * n_pages, ik_start, 0)
    for g in range(group):
        ikbuf[g, n_pages] = iknew_ref[g]
    lax.fori_loop(0, group * n_pages, ik_wait, 0)

    scores = []
    for g in range(group):
        ikt_all = jnp.concatenate([ikbuf[g, p] for p in range(n_pages + 1)], axis=1)
        dots = _mm(iq_ref[g], ikt_all)
        weighted = jnp.maximum(dots, 0.0) * wt_ref[g]
        scores.append(jnp.sum(weighted.reshape(n_idx_heads, t_new, n_keys), axis=0))
    scores = jnp.concatenate(scores, axis=0)
    shape = (group * t_new, n_keys)
    qpos = past + lax.broadcasted_iota(I32, shape, 0) % t_new
    kpos = lax.broadcasted_iota(I32, shape, 1)
    valid = kpos <= qpos
    key = jnp.where(valid, _sortable_key(scores), INT_MIN)
    sel = valid & _topk_select(key, topk, int(math.log2(n_keys)) + 1)
    o_ref[...] = jnp.where(sel, 0.0, -jnp.inf).reshape(group, t_new, n_keys)


def dsa_sample_select(page_table, iq_rows, wt_rows, ik_new_t, cache_idx_kt, *, layer, n_idx_heads, t_new, topk,
                      group):
    db, n_pages = page_table.shape
    idx_dim = cache_idx_kt.shape[2]
    n_keys = (n_pages + 1) * PAGE_SIZE
    kern = functools.partial(_dsa_select_body, layer=layer, n_pages=n_pages, group=group,
                             n_idx_heads=n_idx_heads, t_new=t_new, topk=topk)
    per_g = lambda shape: pl.BlockSpec((group,) + shape, lambda si, pt: (si,) + (0,) * len(shape))
    grid_spec = pltpu.PrefetchScalarGridSpec(
        num_scalar_prefetch=1,
        grid=(db // group,),
        in_specs=[per_g((n_idx_heads * t_new, idx_dim)), per_g((n_idx_heads * t_new, 1)),
                  per_g((idx_dim, PAGE_SIZE)), pl.BlockSpec(memory_space=pl.ANY)],
        out_specs=per_g((t_new, n_keys)),
        scratch_shapes=[pltpu.VMEM((group, n_pages + 1, idx_dim, PAGE_SIZE), F32), pltpu.SemaphoreType.DMA(())],
    )
    return pl.pallas_call(
        kern,
        grid_spec=grid_spec,
        out_shape=jax.ShapeDtypeStruct((db, t_new, n_keys), F32),
        compiler_params=_cparams("arbitrary"),
        name="dsa_sample_select",
    )(page_table.reshape(-1), iq_rows, wt_rows, ik_new_t, cache_idx_kt)


def _dsa_sample_body(pt_ref, mask_ref, qbd_ref, knew_ref, vnew_ref,
                     blast_ref, bnew_ref, ck_hbm, cv_hbm, o_ref,
                     kbuf, vbuf, sem_k, sem_v,
                     *, layer, n_pages, chunk, n_heads, t_new):
    b = pl.program_id(0)
    n_seq = pl.num_programs(0)
    past = n_pages * PAGE_SIZE
    n_chunks = n_pages // chunk
    rows = n_heads * t_new
    ck = chunk * PAGE_SIZE

    def kv_copies(seq, c, j):
        slot = c % 2
        page = pt_ref[seq * n_pages + c * chunk + j]
        return (pltpu.make_async_copy(ck_hbm.at[layer, page], kbuf.at[slot, j], sem_k.at[slot]),
                pltpu.make_async_copy(cv_hbm.at[layer, page], vbuf.at[slot, j], sem_v.at[slot]))

    def start_chunk(seq, c):
        for j in range(chunk):
            kc, vc = kv_copies(seq, c, j)
            kc.start()
            vc.start()

    def wait_chunk(c):
        for j in range(chunk):
            kc, vc = kv_copies(b, c, j)
            kc.wait()
            vc.wait()

    @pl.when(b == 0)
    def _():
        start_chunk(0, 0)

    sel_rows = jnp.tile(mask_ref[0], (n_heads, 1))

    q_rep = jnp.tile(qbd_ref[0], (n_heads, 1))
    row_head = lax.broadcasted_iota(I32, q_rep.shape, 0) // t_new
    col_head = lax.broadcasted_iota(I32, q_rep.shape, 1) // HEAD_DIM
    qbd = jnp.where(row_head == col_head, q_rep, jnp.zeros_like(q_rep))
    neg = -1e30

    def update(state, logits, maskc, vt_bf16):
        m, l, acc = state
        s = logits + maskc
        m_new = jnp.maximum(m, jnp.max(s, axis=-1, keepdims=True))
        alpha = jnp.exp(m - m_new)
        p = jnp.exp(s - m_new)
        l = alpha * l + jnp.sum(p, axis=-1, keepdims=True)
        acc = alpha * acc + _mm_nt(p, vt_bf16)
        return m_new, l, acc

    def pages_t(buf, slot):
        return jnp.concatenate([buf[slot, j] for j in range(chunk)], axis=1).astype(BF16)

    state = (jnp.full((rows, 1), neg, F32), jnp.zeros((rows, 1), F32),
             jnp.zeros((rows, qbd.shape[1]), F32))
    for c in range(n_chunks):
        if c + 1 < n_chunks:
            start_chunk(b, c + 1)
        else:
            @pl.when(b + 1 < n_seq)
            def _():
                start_chunk(b + 1, 0)
        wait_chunk(c)
        slot = c % 2
        logits = _mm(qbd, pages_t(kbuf, slot))
        if c == n_chunks - 1:
            logits = jnp.concatenate([logits[:, :ck - PAGE_SIZE],
                                      logits[:, ck - PAGE_SIZE:] + blast_ref[...]], axis=1)
        state = update(state, logits, sel_rows[:, c * ck:(c + 1) * ck], pages_t(vbuf, slot))
    logits = _mm(qbd, knew_ref[0]) + bnew_ref[...]
    m, l, acc = update(state, logits, sel_rows[:, past:], vnew_ref[0])
    out = jnp.where(row_head == col_head, acc / l, 0.0)
    o_ref[0] = jnp.sum(out.reshape(n_heads, t_new, out.shape[1]), axis=0)


def dsa_sample(page_table, mask, q_bd, k_new_t, v_new_t, bias_last, bias_new, cache_kt, cache_vt,
               *, layer, n_heads, t_new, chunk):
    db, n_pages = page_table.shape
    assert (n_pages // chunk) % 2 == 0, "chunks alternate between two buffers across sequences"
    rows = n_heads * t_new
    width = q_bd.shape[2]
    kern = functools.partial(_dsa_sample_body, layer=layer, n_pages=n_pages, chunk=chunk, n_heads=n_heads,
                             t_new=t_new)
    per_b = lambda shape: pl.BlockSpec((1,) + shape, lambda bi, pt: (bi,) + (0,) * len(shape))
    const = lambda shape: pl.BlockSpec(shape, lambda bi, pt: (0,) * len(shape))
    any_spec = pl.BlockSpec(memory_space=pl.ANY)
    grid_spec = pltpu.PrefetchScalarGridSpec(
        num_scalar_prefetch=1,
        grid=(db,),
        in_specs=[per_b((t_new, mask.shape[2])),
                  per_b((t_new, width)), per_b((width, PAGE_SIZE)), per_b((width, PAGE_SIZE)),
                  const((rows, PAGE_SIZE)), const((rows, PAGE_SIZE)),
                  any_spec, any_spec],
        out_specs=per_b((t_new, width)),
        scratch_shapes=[pltpu.VMEM((2, chunk, width, PAGE_SIZE), F32),
                        pltpu.VMEM((2, chunk, width, PAGE_SIZE), F32),
                        pltpu.SemaphoreType.DMA((2,)),
                        pltpu.SemaphoreType.DMA((2,))],
    )
    return pl.pallas_call(
        kern,
        grid_spec=grid_spec,
        out_shape=jax.ShapeDtypeStruct((db, t_new, width), F32),
        compiler_params=_cparams("arbitrary"),
        name="dsa_sample",
    )(page_table.reshape(-1), mask, q_bd, k_new_t, v_new_t, bias_last, bias_new, cache_kt, cache_vt)


def _token_minor_cache(cache):
    l, pool, page = cache.shape[:3]
    nd = cache.ndim
    return cache.transpose((0, 1) + tuple(range(3, nd)) + (2,)).reshape(l, pool, -1, page)


def _dsa_sample_inputs(q, k_new, v_new, iq, ik_new, iw, bias_tiles):
    db, t, h, dh = q.shape
    ih = iq.shape[2]
    iq_rows = iq.transpose(0, 2, 1, 3).reshape(db, ih * t, -1).astype(BF16)
    wt_rows = (iw * (ih ** -0.5 * iq.shape[3] ** -0.5)).transpose(0, 2, 1).reshape(db, ih * t, 1)
    q_bd = (q * dh ** -0.5).reshape(db, t, h * dh)
    page_t = lambda x: jnp.pad(x.reshape(db, t, -1).transpose(0, 2, 1), ((0, 0), (0, 0), (0, PAGE_SIZE - t)))
    bias_new = bias_tiles[0, :, :t, :].reshape(h * t, Q_BLOCK)
    bias_last = bias_tiles[1, :, :t, :].reshape(h * t, Q_BLOCK)
    return ((iq_rows, wt_rows, page_t(ik_new)),
            (q_bd.astype(BF16), page_t(k_new).astype(BF16), page_t(v_new).astype(BF16), bias_last, bias_new))


def _matmul_residual_body(x_ref, a_ref, w_ref, o_ref):
    o_ref[...] = x_ref[...] + jnp.dot(a_ref[...], w_ref[...], preferred_element_type=F32)


def matmul_residual(x, a, w, *, tm, tn):
    n, d = x.shape
    kd = a.shape[1]
    return pl.pallas_call(
        _matmul_residual_body,
        grid=(n // tm, d // tn),
        in_specs=[pl.BlockSpec((tm, tn), lambda i, j: (i, j)),
                  pl.BlockSpec((tm, kd), lambda i, j: (i, 0)),
                  pl.BlockSpec((kd, tn), lambda i, j: (0, j))],
        out_specs=pl.BlockSpec((tm, tn), lambda i, j: (i, j)),
        out_shape=jax.ShapeDtypeStruct((n, d), F32),
        compiler_params=_cparams("parallel", "parallel"),
        name="matmul_residual",
    )(x, a, w)


def _cross_attn_body(x_ref, g_ref, wq_ref, mk_ref, mv_ref, wo_ref, o_ref, *, groups, t_rows, n_heads, head_dim):
    x = x_ref[...]
    h = _rmsnorm(x, g_ref[...]).astype(BF16)
    q = jnp.dot(h, wq_ref[...], preferred_element_type=F32).astype(BF16)
    scale = head_dim ** -0.5
    outs = []
    for gi in range(groups):
        qg = q[gi * t_rows:(gi + 1) * t_rows]
        heads = []
        for hh in range(n_heads):
            sl = slice(hh * head_dim, (hh + 1) * head_dim)
            logits = _mm_nt(qg[:, sl], mk_ref[gi, :, sl]) * scale
            mx = jnp.max(logits, axis=-1, keepdims=True)
            e = jnp.exp(logits - mx)
            p = e / jnp.sum(e, axis=-1, keepdims=True)
            heads.append(jnp.dot(p.astype(BF16), mv_ref[gi, :, sl], preferred_element_type=F32))
        outs.append(jnp.concatenate(heads, axis=1))
    o = jnp.concatenate(outs, axis=0) if groups > 1 else outs[0]
    o_ref[...] = x + jnp.dot(o.astype(BF16), wo_ref[...], preferred_element_type=F32)


def cross_attn(x, g, wq, mk, mv, wo, *, groups, t_rows, seq_tiles, n_heads):
    n, d = x.shape
    xw = wq.shape[1]
    rows = groups * t_rows
    m = mk.shape[1]
    kern = functools.partial(_cross_attn_body, groups=groups, t_rows=t_rows, n_heads=n_heads,
                             head_dim=xw // n_heads)
    return pl.pallas_call(
        kern,
        grid=(n // rows,),
        in_specs=[pl.BlockSpec((rows, d), lambda i: (i, 0)),
                  pl.BlockSpec((1, d), lambda i: (0, 0)),
                  pl.BlockSpec((d, xw), lambda i: (0, 0)),
                  pl.BlockSpec((groups, m, xw), lambda i: (i // seq_tiles, 0, 0)),
                  pl.BlockSpec((groups, m, xw), lambda i: (i // seq_tiles, 0, 0)),
                  pl.BlockSpec((xw, d), lambda i: (0, 0))],
        out_specs=pl.BlockSpec((rows, d), lambda i: (i, 0)),
        out_shape=jax.ShapeDtypeStruct((n, d), F32),
        compiler_params=_cparams("parallel"),
        name="cross_attn",
    )(x, g.reshape(1, d), wq, mk, mv, wo)


def _router_body(x_ref, g_ref, wr_ref, br_ref, h_ref, r_ref, *, n_groups, per_group):
    h = _rmsnorm(x_ref[...], g_ref[...])
    h_ref[...] = h
    logits = _mm3(h, wr_ref[...]) + br_ref[...]
    lane = lax.broadcasted_iota(I32, logits.shape, 1).astype(F32)
    big = 1e9
    first_lane = lambda hit: jnp.min(jnp.where(hit, lane, big), axis=-1, keepdims=True)
    gl = jnp.where(lane < n_groups, logits, -jnp.inf)
    gmax = jnp.max(gl, axis=-1, keepdims=True)
    grp = first_lane(gl == gmax)
    p_grp = 1.0 / jnp.sum(jnp.exp(gl - gmax), axis=-1, keepdims=True)
    e_id = lane - n_groups
    in_grp = (e_id >= grp * per_group) & (e_id < (grp + 1.0) * per_group)
    el = jnp.where(in_grp, logits, -jnp.inf)
    v1 = jnp.max(el, axis=-1, keepdims=True)
    i1 = first_lane(el == v1) - n_groups
    el2 = jnp.where(e_id == i1, -jnp.inf, el)
    v2 = jnp.max(el2, axis=-1, keepdims=True)
    i2 = first_lane(el2 == v2) - n_groups
    e2 = jnp.exp(v2 - v1)
    g1 = p_grp / (1.0 + e2)
    g2 = p_grp * e2 / (1.0 + e2)
    r_ref[...] = jnp.where(lane == 0, g1, jnp.where(lane == 1, g2, jnp.where(
        lane == 2, i1, jnp.where(lane == 3, i2, 0.0))))


def router(x, g, w_r, b_r, *, tm, n_groups, per_group):
    n, d = x.shape
    kern = functools.partial(_router_body, n_groups=n_groups, per_group=per_group)
    return pl.pallas_call(
        kern,
        grid=(n // tm,),
        in_specs=[pl.BlockSpec((tm, d), lambda i: (i, 0)),
                  pl.BlockSpec((1, d), lambda i: (0, 0)),
                  pl.BlockSpec((d, LANES), lambda i: (0, 0)),
                  pl.BlockSpec((1, LANES), lambda i: (0, 0))],
        out_specs=[pl.BlockSpec((tm, d), lambda i: (i, 0)),
                   pl.BlockSpec((tm, LANES), lambda i: (i, 0))],
        out_shape=[jax.ShapeDtypeStruct((n, d), F32), jax.ShapeDtypeStruct((n, LANES), F32)],
        compiler_params=_cparams("parallel"),
        name="moe_router",
    )(x, g.reshape(1, d), w_r, b_r)


def _slab_rows(buf, base, n_rows, n_slab):
    return jnp.concatenate([buf[pl.ds(base * n_slab + s, n_rows, stride=n_slab), :] for s in range(n_slab)],
                           axis=1)


def _moe_ffn_body(te_ref, tv_ref, tok_ref, h_hbm, w1_ref, w3_ref, w2_ref, o_ref, xbuf, sem, *, tm):
    t = pl.program_id(0)
    n_tiles = pl.num_programs(0)
    n_slab = h_hbm.shape[1]
    slot = t % 2

    def row_copy(tile, sl, r):
        dst = xbuf.at[sl, pl.ds(pl.multiple_of(r * n_slab, n_slab), n_slab)]
        return pltpu.make_async_copy(h_hbm.at[tok_ref[tile * tm + r]], dst, sem.at[sl])

    def start_tile(tile, sl):
        def start(r, c):
            row_copy(tile, sl, 2 * r).start(priority=0)
            row_copy(tile, sl, 2 * r + 1).start(priority=1)
            return c
        lax.fori_loop(0, tm // 2, start, 0)

    @pl.when((t == 0) & (tv_ref[0] != 0))
    def _():
        start_tile(0, 0)

    nxt = jnp.minimum(t + 1, n_tiles - 1)

    @pl.when((t + 1 < n_tiles) & (tv_ref[nxt] != 0))
    def _():
        start_tile(nxt, 1 - slot)

    @pl.when(tv_ref[t] != 0)
    def _():
        def wait(r, c):
            row_copy(t, slot, r).wait()
            return c

        lax.fori_loop(0, tm, wait, 0)
        x = _slab_rows(xbuf.at[slot], 0, tm, n_slab).astype(BF16)
        a = jnp.dot(x, w1_ref[0].astype(BF16), preferred_element_type=F32)
        bgate = jnp.dot(x, w3_ref[0].astype(BF16), preferred_element_type=F32)
        u = (a / (1.0 + jnp.exp(-a))) * bgate
        o_ref[...] = jnp.dot(u.astype(BF16), w2_ref[0].astype(BF16), preferred_element_type=F32)

    @pl.when(tv_ref[t] == 0)
    def _():
        o_ref[...] = jnp.zeros(o_ref.shape, F32)


def moe_ffn(tile_expert, tile_valid, row_tok, h, w1, w3, w2, *, tm):
    n_tiles = tile_expert.shape[0]
    d = h.shape[1]
    de = w1.shape[2]
    h = h.reshape(h.shape[0], d // LANES, LANES)
    grid_spec = pltpu.PrefetchScalarGridSpec(
        num_scalar_prefetch=3,
        grid=(n_tiles,),
        in_specs=[pl.BlockSpec(memory_space=pl.ANY),
                  pl.BlockSpec((1, d, de), lambda t, te, tv, tok: (te[t], 0, 0)),
                  pl.BlockSpec((1, d, de), lambda t, te, tv, tok: (te[t], 0, 0)),
                  pl.BlockSpec((1, de, d), lambda t, te, tv, tok: (te[t], 0, 0))],
        out_specs=pl.BlockSpec((tm, d), lambda t, te, tv, tok: (t, 0)),
        scratch_shapes=[pltpu.VMEM((2, tm * (d // LANES), LANES), F32), pltpu.SemaphoreType.DMA((2,))],
    )
    return pl.pallas_call(
        functools.partial(_moe_ffn_body, tm=tm),
        grid_spec=grid_spec,
        out_shape=jax.ShapeDtypeStruct((n_tiles * tm, d), F32),
        compiler_params=_cparams("arbitrary"),
        name="moe_ffn",
    )(tile_expert, tile_valid, row_tok, h, w1, w3, w2)


def _moe_dispatch(eids, n_experts, tm):
    n, k = eids.shape
    m = n * k
    flat_e = eids.reshape(-1)
    order = jnp.argsort(flat_e, stable=True).astype(I32)
    inv = jnp.argsort(order).astype(I32)
    onehot = flat_e[:, None] == jnp.arange(n_experts)[None, :]
    counts = jnp.sum(onehot.astype(I32), axis=0)
    padded = (counts + tm - 1) // tm * tm
    pad_end = jnp.cumsum(padded)
    pad_start = pad_end - padded
    start = jnp.cumsum(counts) - counts
    shift = jnp.sum(jnp.where(onehot, (pad_start - start)[None, :], 0), axis=1)
    pos = (inv + shift).astype(I32)
    n_tiles = -(-m // tm) + n_experts
    tile_start = jnp.arange(n_tiles) * tm
    tile_valid = (tile_start < pad_end[-1]).astype(I32)
    last = jnp.maximum(pad_end[-1] - 1, 0)
    tile_expert = jnp.minimum(jnp.searchsorted(pad_end, jnp.minimum(tile_start, last), side='right'),
                              n_experts - 1).astype(I32)
    rank = (tile_start - pad_start[tile_expert])[:, None] + jnp.arange(tm)[None, :]
    src = jnp.clip(start[tile_expert][:, None] + rank, 0, m - 1)
    row_tok = jnp.where(rank < counts[tile_expert][:, None], order[src] // k, 0)
    return tile_expert, tile_valid, row_tok.reshape(-1).astype(I32), pos


def _combine_body(pos_ref, x_ref, r_ref, g_ref, y_hbm, o_ref, ybuf, sem, *, tm, top_k):
    i = pl.program_id(0)
    n_tiles = pl.num_programs(0)
    n_slab = y_hbm.shape[1]
    slot = i % 2

    def row_copy(tile, sl, j):
        tok = j // top_k
        kk = j % top_k
        dst = ybuf.at[sl, pl.ds(pl.multiple_of((kk * tm + tok) * n_slab, n_slab), n_slab)]
        return pltpu.make_async_copy(y_hbm.at[pos_ref[tile * tm * top_k + j]], dst, sem.at[sl])

    def start_tile(tile, sl):
        def start(j, c):
            row_copy(tile, sl, 2 * j).start(priority=0)
            row_copy(tile, sl, 2 * j + 1).start(priority=1)
            return c
        lax.fori_loop(0, tm * top_k // 2, start, 0)

    @pl.when(i == 0)
    def _():
        start_tile(0, 0)

    @pl.when(i + 1 < n_tiles)
    def _():
        start_tile(i + 1, 1 - slot)

    def wait(j, c):
        row_copy(i, slot, j).wait()
        return c

    lax.fori_loop(0, tm * top_k, wait, 0)
    route = r_ref[...]
    x = x_ref[...]
    for kk in range(top_k):
        x = x + _slab_rows(ybuf.at[slot], kk * tm, tm, n_slab) * route[:, kk:kk + 1]
    o_ref[...] = _rmsnorm(x, g_ref[...])


def moe_combine(pos, x, route, g, y, *, tm, top_k):
    n, d = x.shape
    y = y.reshape(y.shape[0], d // LANES, LANES)
    grid_spec = pltpu.PrefetchScalarGridSpec(
        num_scalar_prefetch=1,
        grid=(n // tm,),
        in_specs=[pl.BlockSpec((tm, d), lambda i, p: (i, 0)),
                  pl.BlockSpec((tm, LANES), lambda i, p: (i, 0)),
                  pl.BlockSpec((1, d), lambda i, p: (0, 0)),
                  pl.BlockSpec(memory_space=pl.ANY)],
        out_specs=pl.BlockSpec((tm, d), lambda i, p: (i, 0)),
        scratch_shapes=[pltpu.VMEM((2, top_k * tm * (d // LANES), LANES), F32), pltpu.SemaphoreType.DMA((2,))],
    )
    return pl.pallas_call(
        functools.partial(_combine_body, tm=tm, top_k=top_k),
        grid_spec=grid_spec,
        out_shape=jax.ShapeDtypeStruct((n, d), F32),
        compiler_params=_cparams("arbitrary"),
        name="moe_combine",
    )(pos, x, route, g.reshape(1, d), y)


def kernel(x_prompt, x_sample, mem_prompt, cache_k, cache_v, cache_idx_k, page_table, state_wkv, state_shift, cache_mem_k, cache_mem_v, g_mix, w_in, mu_shift, rw_w0, rw_w2, rw_a0, rw_a2, rw_g2, rw_kk, rw_ka, rw_rk, rw_ln_g, rw_ln_b, w_out, g_cross, g_mem, w_cq, w_ck, w_cv, w_co, g_ffn, w_rg, b_rg, w_re, b_re, w_e1, w_e3, w_e2, rel_bias, g_final):
    B, S, D = x_prompt.shape
    DB, T, _ = x_sample.shape
    assert w_in.shape[0] == 1, "single-layer trunk only"
    l = 0
    n_pages = page_table.shape[1]
    past = n_pages * PAGE_SIZE
    topk_p = min(TOPK_MAX, S // 4)
    topk_s = min(TOPK_MAX, (past + T) // 4)
    rw_proj = mu_shift.shape[1]
    width = rw_w0.shape[1]
    at_w = D - width
    n_heads = at_w // HEAD_DIM
    idx_dim = cache_idx_k.shape[-1]
    ih = (w_in.shape[2] - rw_proj - 3 * at_w - idx_dim) // (idx_dim + 1)
    xw = w_cq.shape[2]
    x_heads = cache_mem_k.shape[3]
    n_mem = mem_prompt.shape[1]
    n_experts = w_e1.shape[1]
    top_k = 2
    tn = 512
    n_main = 3 * width
    n_lora = rw_proj - n_main
    n_att = 3 * at_w + ih * idx_dim
    n_tail = n_lora + idx_dim + ih
    assert n_main % tn == 0 and at_w % tn == 0 and (ih * idx_dim) % tn == 0 and n_tail <= tn

    w_t = w_in[l].T
    w_all = jnp.concatenate([w_t[:n_main], w_t[rw_proj:rw_proj + n_att], w_t[n_main:rw_proj],
                             w_t[rw_proj + n_att:], jnp.zeros((tn - n_tail, D), w_t.dtype)], axis=0).astype(BF16)
    flat, per_head = False, True
    segments = [(n_main // tn, [(flat, F32)]),
                (at_w // tn, [(flat, BF16)]),
                (at_w // tn, [(per_head, F32), (flat, BF16)]),
                (at_w // tn, [(per_head, F32), (flat, BF16)]),
                (ih * idx_dim // tn, [(flat, BF16)]),
                (1, [(flat, F32)])]
    w_out_b = w_out[l].astype(BF16)
    w_cq_b, w_co_b = w_cq[l].astype(BF16), w_co[l].astype(BF16)
    w_ckv = jnp.concatenate([w_ck[l], w_cv[l]], axis=1).astype(BF16)
    n_route = w_rg.shape[2] + w_re.shape[2]
    w_r = jnp.pad(jnp.concatenate([w_rg[l], w_re[l]], axis=1), ((0, 0), (0, LANES - n_route)))
    b_r = jnp.pad(jnp.concatenate([b_rg[l], b_re[l]]), (0, LANES - n_route)).reshape(1, LANES)
    rw_args = (mu_shift[l], rw_w0[l], rw_w2[l], rw_a0[l], rw_a2[l], rw_g2[l], rw_kk[l], rw_ka[l],
               rw_rk[l].reshape(-1))
    tiles = bias_tables(rel_bias, ((0, False), (Q_BLOCK, False), (0, True), (Q_BLOCK, True)))

    def project(x2d, b_, t_, tm):
        f_main, q, k_heads, k, v_heads, v, iq, f_tail = norm_matmul_split(x2d, g_mix[l], w_all, segments,
                                                                          tm=tm, tn=tn)
        r3 = lambda z: z.reshape(b_, t_, -1)
        f_main, f_tail = r3(f_main), r3(f_tail)
        ik = f_tail[..., n_lora:n_lora + idx_dim]
        iw = f_tail[..., n_lora + idx_dim:n_tail]
        shift = jnp.concatenate([f_main[:, -1], f_tail[:, -1, :n_lora]], axis=-1)
        heads5 = lambda z: z.reshape(1, b_, t_, n_heads, HEAD_DIM)
        return f_main, r3(q), heads5(k_heads), r3(k), heads5(v_heads), r3(v), r3(iq), f_tail, ik, iw, shift

    def rw_rows(y):
        b_, p_, t_, _ = y.shape
        return y.transpose(0, 2, 1, 3).reshape(b_ * t_, p_ * PAIR).astype(BF16)

    xp = x_prompt.reshape(B * S, D)
    fm_p, q, k_p, kb_p, v_p, vb_p, iq, ft_p, ik_p, iw, shift_p = project(xp, B, S, 512)
    feats_p = rwkv_prep(fm_p, ft_p, jnp.zeros((B, rw_proj), F32), *rw_args, tm=256, width=width)
    rw_p, st_p = rwkv_chunk(feats_p, jnp.zeros((B, width // PAIR, PAIR, PAIR), F32), rw_ln_g[l], rw_ln_b[l], L=64)
    tr = lambda z: z.transpose(0, 2, 1)
    k_pairs = kb_p.reshape(B, S, n_heads // 2, PAIR).transpose(0, 2, 1, 3)
    at_p = dsa_prompt(ik_p.astype(BF16), tr(iq), tr(iw * (ih ** -0.5 * idx_dim ** -0.5))[:, :, None, :],
                      k_pairs, tr(q * HEAD_DIM ** -0.5), tr(vb_p), tiles[2:4], topk=topk_p)
    mix_p = jnp.concatenate([rw_rows(rw_p), tr(at_p).reshape(B * S, at_w)], axis=1)
    x1_p = matmul_residual(xp, mix_p, w_out_b, tm=512, tn=512)
    mkv = norm_matmul(mem_prompt.reshape(B * n_mem, D), g_mem[l], w_ckv, tm=256, tn=512)
    mk_p = mkv[:, :xw].reshape(B, n_mem, xw)
    mv_p = mkv[:, xw:].reshape(B, n_mem, xw)
    x2_p = cross_attn(x1_p, g_cross[l], w_cq_b, mk_p.astype(BF16), mv_p.astype(BF16), w_co_b,
                      groups=1, t_rows=512, seq_tiles=S // 512, n_heads=x_heads)

    xs = x_sample.reshape(DB * T, D)
    fm_s, q2, k_s, kb_s, v_s, vb_s, iq2, ft_s, ik_s, iw2, shift_s = project(xs, DB, T, DB * T)
    feats_s = rwkv_prep(fm_s, ft_s, state_shift[l], *rw_args, tm=T, width=width)
    rw_s, st_s = rwkv_chunk(feats_s, _state_to_block_diag(state_wkv[l]), rw_ln_g[l], rw_ln_b[l], L=T)
    r4 = lambda z, h_: z.reshape(DB, T, h_, -1)
    sel_args, att_args = _dsa_sample_inputs(r4(q2, n_heads), r4(kb_s, n_heads), r4(vb_s, n_heads), r4(iq2, ih),
                                            ik_s, iw2, tiles[0:2])
    mask_s = dsa_sample_select(page_table, *sel_args, _token_minor_cache(cache_idx_k), layer=l,
                               n_idx_heads=ih, t_new=T, topk=topk_s, group=4)
    at_s = dsa_sample(page_table, mask_s, *att_args, _token_minor_cache(cache_k), _token_minor_cache(cache_v),
                      layer=l, n_heads=n_heads, t_new=T, chunk=8)
    mix_s = jnp.concatenate([rw_rows(rw_s), at_s.reshape(DB * T, at_w).astype(BF16)], axis=1)
    x1_s = matmul_residual(xs, mix_s, w_out_b, tm=DB * T, tn=512)
    x2_s = cross_attn(x1_s, g_cross[l], w_cq_b, cache_mem_k[l].reshape(DB, n_mem, xw).astype(BF16),
                      cache_mem_v[l].reshape(DB, n_mem, xw).astype(BF16), w_co_b,
                      groups=8, t_rows=T, seq_tiles=1, n_heads=x_heads)

    h_p, route_p = router(x2_p, g_ffn[l], w_r, b_r, tm=512, n_groups=w_rg.shape[2],
                          per_group=w_re.shape[2] // w_rg.shape[2])
    h_s, route_s = router(x2_s, g_ffn[l], w_r, b_r, tm=DB * T, n_groups=w_rg.shape[2],
                          per_group=w_re.shape[2] // w_rg.shape[2])
    h_all = jnp.concatenate([h_p, h_s], axis=0)
    eids = jnp.concatenate([route_p[:, top_k:2 * top_k], route_s[:, top_k:2 * top_k]], axis=0).astype(I32)
    tile_expert, tile_valid, row_tok, pos = _moe_dispatch(eids, n_experts, 256)
    y_rows = moe_ffn(tile_expert, tile_valid, row_tok, h_all, w_e1[l], w_e3[l], w_e2[l], tm=256)
    n_p = B * S
    y_p = moe_combine(pos[:n_p * top_k], x2_p, route_p, g_final, y_rows, tm=256, top_k=top_k)
    y_s = moe_combine(pos[n_p * top_k:], x2_s, route_s, g_final, y_rows, tm=DB * T, top_k=top_k)

    return (y_p.reshape(B, S, D), y_s.reshape(DB, T, D),
            k_p, v_p, ik_p[None], _state_from_block_diag(st_p)[None], shift_p[None],
            mk_p.reshape(1, B, n_mem, x_heads, xw // x_heads), mv_p.reshape(1, B, n_mem, x_heads, xw // x_heads),
            k_s, v_s, ik_s[None], _state_from_block_diag(st_s)[None], shift_s[None])
```

```python
import functools
import math

import jax
import jax.numpy as jnp
from jax import lax
from jax.experimental import pallas as pl
from jax.experimental.pallas import tpu as pltpu

F32 = jnp.float32
BF16 = jnp.bfloat16
I32 = jnp.int32

LANES = 128
SUBLANES = 8
VMEM_LIMIT_BYTES = 56 * 1024 * 1024

HEAD_DIM = 64
PAIR = 2 * HEAD_DIM
GN_EPS = 64e-5
NORM_EPS = 1e-6
TOPK_MAX = 256
Q_BLOCK = 128
N_BUCKETS = 32
MAX_DISTANCE = 128
PAGE_SIZE = 128
N_GROUPS = 4
EXPERTS_PER_GROUP = 8
INT_MIN = -(2 ** 31)


def _cparams(*sem):
    return pltpu.CompilerParams(dimension_semantics=sem, vmem_limit_bytes=VMEM_LIMIT_BYTES)


def _mm(a, b):
    return jnp.dot(a.astype(BF16), b.astype(BF16), preferred_element_type=F32)


def _mm_nt(a, b):
    return lax.dot_general(a.astype(BF16), b.astype(BF16), (((1,), (1,)), ((), ())),
                           preferred_element_type=F32)


def _split2(x):
    hi = x.astype(BF16)
    lo = (x - hi.astype(F32)).astype(BF16)
    return hi, lo


def _mm3(a, b):
    ah, al = _split2(a)
    bh, bl = _split2(b)
    d = lambda x, y: jnp.dot(x, y, preferred_element_type=F32)
    return d(ah, bh) + (d(ah, bl) + d(al, bh))


def _mm_exact_rhs(a, b_bf16):
    hi = a.astype(BF16)
    r1 = a - hi.astype(F32)
    mid = r1.astype(BF16)
    lo = (r1 - mid.astype(F32)).astype(BF16)
    d = lambda x: jnp.dot(x, b_bf16, preferred_element_type=F32)
    return d(hi) + (d(mid) + d(lo))


def _rmsnorm(x, g):
    ms = jnp.mean(x * x, axis=-1, keepdims=True)
    return x * lax.rsqrt(ms + NORM_EPS) * g


def _norm_matmul_body(x_ref, g_ref, w_ref, o_ref, xn_ref):
    @pl.when(pl.program_id(1) == 0)
    def _():
        xn_ref[...] = _rmsnorm(x_ref[...], g_ref[...]).astype(BF16)

    o_ref[...] = _mm(xn_ref[...], w_ref[...])


def norm_matmul(x, g, w, *, tm, tn):
    n, d = x.shape
    m = w.shape[1]
    return pl.pallas_call(
        _norm_matmul_body,
        grid=(n // tm, m // tn),
        in_specs=[pl.BlockSpec((tm, d), lambda i, j: (i, 0)),
                  pl.BlockSpec((1, d), lambda i, j: (0, 0)),
                  pl.BlockSpec((d, tn), lambda i, j: (0, j))],
        out_specs=pl.BlockSpec((tm, tn), lambda i, j: (i, j)),
        out_shape=jax.ShapeDtypeStruct((n, m), F32),
        scratch_shapes=[pltpu.VMEM((tm, d), BF16)],
        compiler_params=_cparams("parallel", "arbitrary"),
        name="norm_matmul",
    )(x, g.reshape(1, d), w)


def _norm_matmul_split_body(x_ref, g_ref, w_ref, *rest, bounds):
    o_refs, xn_ref = rest[:-1], rest[-1]
    j = pl.program_id(1)

    @pl.when(j == 0)
    def _():
        xn_ref[...] = _rmsnorm(x_ref[...], g_ref[...]).astype(BF16)

    res = _mm_nt(xn_ref[...], w_ref[...])
    tm, tn = res.shape
    for o_ref, (lo, hi) in zip(o_refs, bounds):
        @pl.when((j >= lo) & (j < hi))
        def _(o_ref=o_ref, lo=lo, hi=hi):
            if o_ref.shape[1] == HEAD_DIM:
                n_heads = (hi - lo) * tn // HEAD_DIM
                per_tile = tn // HEAD_DIM
                for c in range(hi - lo):
                    @pl.when(j == lo + c)
                    def _(c=c):
                        for hh in range(per_tile):
                            o_ref[pl.ds(c * per_tile + hh, tm, stride=n_heads), :] = (
                                res[:, hh * HEAD_DIM:(hh + 1) * HEAD_DIM].astype(o_ref.dtype))
            else:
                o_ref[...] = res.astype(o_ref.dtype)


def norm_matmul_split(x, g, wt, segments, *, tm, tn):
    n, d = x.shape
    m = wt.shape[0]
    bounds, out_specs, out_shape, lo = [], [], [], 0
    for nt, outs in segments:
        hi = lo + nt
        for per_head, dt in outs:
            bounds.append((lo, hi))
            if per_head:
                heads = nt * tn // HEAD_DIM
                out_specs.append(pl.BlockSpec((tm * heads, HEAD_DIM), lambda i, j: (i, 0)))
                out_shape.append(jax.ShapeDtypeStruct((n * heads, HEAD_DIM), dt))
            else:
                out_specs.append(pl.BlockSpec((tm, tn),
                                              lambda i, j, lo=lo, hi=hi: (i, jnp.clip(j - lo, 0, hi - lo - 1))))
                out_shape.append(jax.ShapeDtypeStruct((n, nt * tn), dt))
        lo = hi
    assert lo * tn == m
    return pl.pallas_call(
        functools.partial(_norm_matmul_split_body, bounds=tuple(bounds)),
        grid=(n // tm, m // tn),
        in_specs=[pl.BlockSpec((tm, d), lambda i, j: (i, 0)),
                  pl.BlockSpec((1, d), lambda i, j: (0, 0)),
                  pl.BlockSpec((tn, d), lambda i, j: (j, 0))],
        out_specs=out_specs,
        out_shape=out_shape,
        scratch_shapes=[pltpu.VMEM((tm, d), BF16)],
        compiler_params=_cparams("arbitrary", "arbitrary"),
        name="norm_matmul_split",
    )(x, g.reshape(1, d), wt)


def _pair_ones():
    r = lax.broadcasted_iota(I32, (PAIR, PAIR), 0) // HEAD_DIM
    c = lax.broadcasted_iota(I32, (PAIR, PAIR), 1) // HEAD_DIM
    return (r == c).astype(BF16)


def _head_sum(x, ones_bd):
    return _mm_exact_rhs(x, ones_bd)


def _rwkv_prep_body(f_ref, prev8_ref, init_ref, mu_ref, ft_ref, tprev8_ref, tinit_ref, tmu_ref,
                    w0_ref, w2_ref, a0_ref, a2_ref, g2_ref, kk_ref, ka_ref, rk_ref,
                    r_o, k_o, v_o, kk_o, b_o, ld_o, g_o, bon_o, *, tm, width):
    i = pl.program_id(1)

    def token_shift(f, p8_ref, i_ref, m_ref):
        nc = m_ref.shape[1]
        prev_row = jnp.where(i == 0, i_ref[0], p8_ref[0, SUBLANES - 1:SUBLANES, :nc])
        rolled = pltpu.roll(f, shift=1, axis=0)
        row = lax.broadcasted_iota(I32, f.shape, 0)
        f_prev = jnp.where(row == 0, prev_row, rolled)
        return f + (f_prev - f) * m_ref[...]

    fs = token_shift(f_ref[0], prev8_ref, init_ref, mu_ref)
    n_dec = w2_ref.shape[0]
    n_icl = a2_ref.shape[0]
    n_lora = tmu_ref.shape[1]
    ts = token_shift(ft_ref[0][:, :n_lora], tprev8_ref, tinit_ref, tmu_ref)
    w_ = width
    r = fs[:, 0:w_]
    k = fs[:, w_:2 * w_]
    v = fs[:, 2 * w_:3 * w_]
    wd = ts[:, 0:n_dec]
    ad = ts[:, n_dec:n_dec + n_icl]
    gd = ts[:, n_dec + n_icl:]
    z = w0_ref[...] + _mm3(jnp.tanh(wd), w2_ref[...])
    nz = -z
    softplus = jnp.maximum(nz, 0.0) + jnp.log(1.0 + jnp.exp(-jnp.abs(nz)))
    w = -softplus - 0.5
    ld = -jnp.exp(w)
    a = 1.0 / (1.0 + jnp.exp(-(a0_ref[...] + _mm3(ad, a2_ref[...]))))
    g = _mm3(1.0 / (1.0 + jnp.exp(-gd)), g2_ref[...])
    kk = k * kk_ref[...]
    k2 = k * (1.0 + (a - 1.0) * ka_ref[...])
    rk = r * k2 * rk_ref[...]
    ones_bd = _pair_ones()
    for p in range(w_ // PAIR):
        sl = slice(p * PAIR, (p + 1) * PAIR)
        kkp = kk[:, sl]
        nrm = jnp.sqrt(_head_sum(kkp * kkp, ones_bd))
        kkp = kkp / jnp.maximum(nrm, 1e-12)
        ap = a[:, sl]
        r_o[0, p] = r[:, sl]
        k_o[0, p] = k2[:, sl]
        v_o[0, p] = v[:, sl]
        kk_o[0, p] = kkp
        b_o[0, p] = kkp * ap
        ld_o[0, p] = ld[:, sl]
        g_o[0, p] = g[:, sl]
        bon_o[0, p] = _head_sum(rk[:, sl], ones_bd) * v[:, sl]


def rwkv_prep(f_main, f_tail, init_prev, mu, w0, w2, a0, a2, g2, k_k, k_a, r_k, *, tm, width):
    b, t, n_main = f_main.shape
    n_tail = f_tail.shape[2]
    n_lora = mu.shape[0] - n_main
    npair = width // PAIR
    row1 = lambda x: x.reshape(1, -1)
    kern = functools.partial(_rwkv_prep_body, tm=tm, width=width)
    full = lambda a: pl.BlockSpec(a.shape, lambda bi, i: (0,) * a.ndim)
    args = [row1(w0), w2, row1(a0), a2, g2, row1(k_k), row1(k_a), row1(r_k)]
    out_spec = pl.BlockSpec((1, npair, tm, PAIR), lambda bi, i: (bi, 0, i, 0))
    out_shape = jax.ShapeDtypeStruct((b, npair, t, PAIR), F32)
    prev8_map = lambda bi, i: (bi, jnp.maximum(i * (tm // SUBLANES) - 1, 0), 0)

    def feature_specs(ncols, n_init):
        return [pl.BlockSpec((1, tm, ncols), lambda bi, i: (bi, i, 0)),
                pl.BlockSpec((1, SUBLANES, ncols), prev8_map),
                pl.BlockSpec((1, 1, n_init), lambda bi, i: (bi, 0, 0)),
                pl.BlockSpec((1, n_init), lambda bi, i: (0, 0))]

    return pl.pallas_call(
        kern,
        grid=(b, t // tm),
        in_specs=feature_specs(n_main, n_main) + feature_specs(n_tail, n_lora) + [full(a) for a in args],
        out_specs=[out_spec] * 8,
        out_shape=[out_shape] * 8,
        compiler_params=_cparams("parallel", "parallel"),
        name="rwkv_prep",
    )(f_main, f_main, init_prev[:, :n_main].reshape(b, 1, n_main), row1(mu[:n_main]),
      f_tail, f_tail, init_prev[:, n_main:].reshape(b, 1, n_lora), row1(mu[n_main:]), *args)


def _rwkv_chunk_body(r_ref, k_ref, v_ref, kk_ref, b_ref, ld_ref, g_ref, bon_ref, s0_ref,
                     lng_ref, lnb_ref, o_ref, st_ref, s_ref, *, L, npair, group):
    c = pl.program_id(1)

    @pl.when(c == 0)
    def _():
        s_ref[...] = s0_ref[0]

    L2 = 2 * L
    row = lax.broadcasted_iota(I32, (L2, L2), 0)
    col = lax.broadcasted_iota(I32, (L2, L2), 1)
    same = (row // L) == (col // L)
    tri_strict = same & (col < row)
    tri_incl = same & (col <= row)
    eye = (row == col).astype(F32)
    tr = lax.broadcasted_iota(I32, (L, L), 0)
    tc = lax.broadcasted_iota(I32, (L, L), 1)
    cum_mat = (tc <= tr).astype(BF16)
    lane = lax.broadcasted_iota(I32, (L, PAIR), 1)
    first = lane < HEAD_DIM
    ones_bd = _pair_ones()
    n_sq = max(int(math.ceil(math.log2(L))) - 1, 0)

    def block_diag(x):
        return jnp.concatenate([jnp.where(first, x, 0.0), jnp.where(first, 0.0, x)], axis=0)

    def group_step(gi, carry):
        ps = [gi * group + j for j in range(group)]
        each = lambda f, *cols: [f(*args) for args in zip(*cols)]
        ld = [ld_ref[0, p] for p in ps]
        cum = each(lambda x: _mm_exact_rhs_t(cum_mat, x), ld)
        dec = each(jnp.exp, cum)
        dec_inv = each(lambda c_: jnp.exp(-c_), cum)
        a_t = each(lambda p, c_, l_: block_diag(-kk_ref[0, p] * jnp.exp(c_ - l_)), ps, cum, ld)
        b_t = each(lambda p, e: block_diag(b_ref[0, p] * e), ps, dec_inv)
        k_t = each(lambda p, e: block_diag(k_ref[0, p] * e), ps, dec_inv)
        r_t = each(lambda p, e: block_diag(r_ref[0, p] * e), ps, dec)
        v_b = each(lambda p: block_diag(v_ref[0, p]), ps)
        a_ab = each(lambda a, b: jnp.where(tri_strict, _mm_nt(a, b), 0.0), a_t, b_t)
        a_ak = each(lambda a, k: jnp.where(tri_strict, _mm_nt(a, k), 0.0), a_t, k_t)
        a_rb = each(lambda r, b: jnp.where(tri_incl, _mm_nt(r, b), 0.0), r_t, b_t)
        a_rk = each(lambda r, k: jnp.where(tri_incl, _mm_nt(r, k), 0.0), r_t, k_t)
        x = a_ab
        t_inv = each(lambda a: eye + a, a_ab)
        for _ in range(n_sq):
            x = each(lambda x_: _mm(x_, x_), x)
            t_inv = each(lambda t, x_: t + _mm(x_, t), t_inv, x)
        s = [s_ref[p] for p in ps]
        rhs = each(lambda a, s_, ak, v: _mm_nt(a, s_) + _mm(ak, v), a_t, s, a_ak, v_b)
        u = each(_mm, t_inv, rhs)
        y_b = each(lambda r, s_, rb, u_, rk, v: _mm_nt(r, s_) + _mm(rb, u_) + _mm(rk, v),
                   r_t, s, a_rb, u, a_rk, v_b)
        s_new = each(lambda s_, u_, b, v, k, d: (s_ + _mm(u_.T, b) + _mm(v.T, k)) * d[L - 1:L, :],
                     s, u, b_t, v_b, k_t, dec)
        for p, sn in zip(ps, s_new):
            s_ref[p] = sn
        y = each(lambda yb: yb[:L] + yb[L:], y_b)
        mean = each(lambda y_: _head_sum(y_, ones_bd) * (1.0 / HEAD_DIM), y)
        d = each(lambda y_, m: y_ - m, y, mean)
        var = each(lambda d_: _head_sum(d_ * d_, ones_bd) * (1.0 / HEAD_DIM), d)
        for p, d_, v_ in zip(ps, d, var):
            yn = d_ * lax.rsqrt(v_ + GN_EPS) * lng_ref[p] + lnb_ref[p]
            o_ref[0, p] = (yn + bon_ref[0, p]) * g_ref[0, p]
        return carry

    lax.fori_loop(0, npair // group, group_step, 0)

    @pl.when(c == pl.num_programs(1) - 1)
    def _():
        st_ref[0] = s_ref[...]


def _mm_exact_rhs_t(m_bf16, x):
    hi = x.astype(BF16)
    r1 = x - hi.astype(F32)
    mid = r1.astype(BF16)
    lo = (r1 - mid.astype(F32)).astype(BF16)
    d = lambda y: jnp.dot(m_bf16, y, preferred_element_type=F32)
    return d(hi) + (d(mid) + d(lo))


def rwkv_chunk(feats, s0_bd, ln_g, ln_b, *, L, group=8):
    b, npair, t, _ = feats[0].shape
    blk = pl.BlockSpec((1, npair, L, PAIR), lambda bi, c: (bi, 0, c, 0))
    st_spec = pl.BlockSpec((1, npair, PAIR, PAIR), lambda bi, c: (bi, 0, 0, 0))
    par_spec = pl.BlockSpec((npair, 1, PAIR), lambda bi, c: (0, 0, 0))
    kern = functools.partial(_rwkv_chunk_body, L=L, npair=npair, group=group)
    return pl.pallas_call(
        kern,
        grid=(b, t // L),
        in_specs=[blk] * 8 + [st_spec, par_spec, par_spec],
        out_specs=[blk, st_spec],
        out_shape=[jax.ShapeDtypeStruct((b, npair, t, PAIR), F32),
                   jax.ShapeDtypeStruct((b, npair, PAIR, PAIR), F32)],
        scratch_shapes=[pltpu.VMEM((npair, PAIR, PAIR), F32)],
        compiler_params=_cparams("parallel", "arbitrary"),
        name="rwkv_chunk",
    )(*feats, s0_bd, ln_g.reshape(npair, 1, PAIR), ln_b.reshape(npair, 1, PAIR))


def _state_to_block_diag(s):
    b, h, n, _ = s.shape
    s = s.reshape(b, h // 2, 2, n, n)
    z = jnp.zeros_like(s[:, :, 0])
    top = jnp.concatenate([s[:, :, 0], z], axis=-1)
    bot = jnp.concatenate([z, s[:, :, 1]], axis=-1)
    return jnp.concatenate([top, bot], axis=-2)


def _state_from_block_diag(s_bd):
    b, p, _, _ = s_bd.shape
    n = HEAD_DIM
    return jnp.stack([s_bd[:, :, :n, :n], s_bd[:, :, n:, n:]], axis=2).reshape(b, 2 * p, n, n)


def _t5_bucket(dist):
    exact = N_BUCKETS // 2
    d = jnp.maximum(dist, 0)
    far = exact + (jnp.log(jnp.maximum(d, 1).astype(F32) / exact) / math.log(MAX_DISTANCE / exact)
                   * (N_BUCKETS - exact)).astype(I32)
    return jnp.where(d < exact, d, jnp.minimum(far, N_BUCKETS - 1))


def _bias_tables_body(rb_ref, o_ref, *, offsets, n_heads):
    r = lax.broadcasted_iota(I32, (Q_BLOCK, Q_BLOCK), 0)
    c = lax.broadcasted_iota(I32, (Q_BLOCK, Q_BLOCK), 1)
    for t, (off, key_major) in enumerate(offsets):
        bucket = _t5_bucket((c - r if key_major else r - c) + off)
        for h in range(n_heads):
            def body(bk, acc):
                return jnp.where(bucket == bk, rb_ref[bk, h], acc)
            tile = lax.fori_loop(0, N_BUCKETS, body, jnp.zeros((Q_BLOCK, Q_BLOCK), F32))
            o_ref[t, h] = tile - rb_ref[N_BUCKETS - 1, h]


def bias_tables(rel_bias, offsets):
    n_heads = rel_bias.shape[1]
    kern = functools.partial(_bias_tables_body, offsets=tuple(offsets), n_heads=n_heads)
    return pl.pallas_call(
        kern,
        in_specs=[pl.BlockSpec(memory_space=pltpu.SMEM)],
        out_specs=pl.BlockSpec(memory_space=pltpu.VMEM),
        out_shape=jax.ShapeDtypeStruct((len(offsets), n_heads, Q_BLOCK, Q_BLOCK), F32),
        name="bias_tables",
    )(rel_bias)


def _sortable_key(scores):
    bits = lax.bitcast_convert_type(scores + 0.0, I32)
    return jnp.where(bits < 0, bits ^ 0x7FFFFFFF, bits)


def _count(mask):
    return jnp.sum(mask.astype(F32), axis=-1, keepdims=True)


def _topk_select(key, topk, n_index_bits):
    rows, n = key.shape
    kf = float(topk)
    t0 = jnp.where(_count(key >= 0) >= kf, 0, INT_MIN).astype(I32)

    def value_bit(i, t):
        cand = t + lax.shift_left(jnp.int32(1), 30 - i)
        return jnp.where(_count(key >= cand) >= kf, cand, t)

    thr = lax.fori_loop(0, 31, value_bit, t0)
    above = key > thr
    ties = key == thr
    need = kf - _count(above)
    idx = lax.broadcasted_iota(I32, (rows, n), 1)

    def lowest_ties():
        def index_bit(i, m):
            cand = m + lax.shift_left(jnp.int32(1), n_index_bits - 1 - i)
            return jnp.where(_count(ties & (idx < cand)) <= need, cand, m)
        return lax.fori_loop(0, n_index_bits, index_bit, jnp.zeros((rows, 1), I32))

    surplus = jnp.max(_count(ties) - need) > 0.0
    m = lax.cond(surplus, lowest_ties, lambda: jnp.full((rows, 1), 2 ** n_index_bits, I32))
    return above | (ties & (idx < m))


SUM_CHAINS = 4
HEAD_GROUP = 4


def _sum_rows(x):
    r = x.shape[0]
    if r % (SUM_CHAINS * SUBLANES) == 0 and r > SUM_CHAINS * SUBLANES:
        x = jnp.sum(x.reshape(SUM_CHAINS, r // SUM_CHAINS, x.shape[1]), axis=1)
    return jnp.sum(x, axis=0, keepdims=True)


def _max_rows(x):
    r = x.shape[0]
    if r % (SUM_CHAINS * SUBLANES) == 0 and r > SUM_CHAINS * SUBLANES:
        x = jnp.max(x.reshape(SUM_CHAINS, r // SUM_CHAINS, x.shape[1]), axis=1)
    return jnp.max(x, axis=0, keepdims=True)


def _topk_select_cols(key, topk, n_index_bits):
    n, cols = key.shape
    kf = float(topk)
    cnt = lambda m: _sum_rows(m.astype(F32))
    t0 = jnp.where(cnt(key >= 0) >= kf, 0, INT_MIN).astype(I32)

    def value_bit(i, t):
        cand = t + lax.shift_left(jnp.int32(1), 30 - i)
        return jnp.where(cnt(key >= cand) >= kf, cand, t)

    thr = lax.fori_loop(0, 31, value_bit, t0)
    above = key > thr
    ties = key == thr
    need = kf - cnt(above)
    idx = lax.broadcasted_iota(I32, (n, cols), 0)

    def lowest_ties():
        def index_bit(i, m):
            cand = m + lax.shift_left(jnp.int32(1), n_index_bits - 1 - i)
            return jnp.where(cnt(ties & (idx < cand)) <= need, cand, m)
        return lax.fori_loop(0, n_index_bits, index_bit, jnp.zeros((1, cols), I32))

    surplus = jnp.max(cnt(ties) - need) > 0.0
    m = lax.cond(surplus, lowest_ties, lambda: jnp.full((1, cols), 2 ** n_index_bits, I32))
    return above | (ties & (idx < m))


def _dsa_prompt_block(nb, ik_ref, iqt_ref, wt_ref, k_ref, qt_ref, vt_ref, bias_ref, o_ref, mask_ref,
                      *, n_heads, n_idx_heads, topk):
    w = nb * Q_BLOCK
    ik = ik_ref[0, :w, :]
    idx_dim = ik.shape[1]

    def head_rows(ref, h, dh, cols=slice(None)):
        return ref[0, pl.ds(pl.multiple_of(h * dh, dh), dh), cols]

    def idx_heads(gi, acc):
        hs = [gi * HEAD_GROUP + j for j in range(HEAD_GROUP)]
        dots = [jnp.dot(ik, head_rows(iqt_ref, h, idx_dim), preferred_element_type=F32) for h in hs]
        terms = [jnp.maximum(d, 0.0) * wt_ref[0, h] for d, h in zip(dots, hs)]
        return acc + ((terms[0] + terms[1]) + (terms[2] + terms[3]))

    scores = lax.fori_loop(0, n_idx_heads // HEAD_GROUP, idx_heads, jnp.zeros((w, Q_BLOCK), F32))
    kpos = lax.broadcasted_iota(I32, (w, Q_BLOCK), 0)
    qpos = (nb - 1) * Q_BLOCK + lax.broadcasted_iota(I32, (w, Q_BLOCK), 1)
    valid = kpos <= qpos
    if w <= topk:
        sel = valid
    else:
        key = jnp.where(valid, _sortable_key(scores), INT_MIN)
        sel = valid & _topk_select_cols(key, topk, int(math.ceil(math.log2(w))) + 1)
    mask_ref[:w, :] = jnp.where(sel, 0.0, -jnp.inf)

    def with_near_bias(logits, h):
        near = [logits[w - Q_BLOCK:] + bias_ref[0, h]]
        if nb >= 2:
            near = [logits[w - 2 * Q_BLOCK:w - Q_BLOCK] + bias_ref[1, h]] + near
        if nb >= 3:
            near = [logits[:w - 2 * Q_BLOCK]] + near
        return jnp.concatenate(near, axis=0) if len(near) > 1 else near[0]

    pair_row_half = lax.broadcasted_iota(I32, (PAIR, Q_BLOCK), 0) // HEAD_DIM

    def head_logits(gi, j):
        p = gi * (HEAD_GROUP // 2) + j // 2
        qt_pair = head_rows(qt_ref, p, PAIR)
        qt_head = jnp.where(pair_row_half == j % 2, qt_pair, jnp.zeros_like(qt_pair))
        return jnp.dot(k_ref[0, p, :w, :], qt_head, preferred_element_type=F32)

    def attn_heads(gi, carry):
        hs = [gi * HEAD_GROUP + j for j in range(HEAD_GROUP)]
        mask = mask_ref[:w, :]
        logits = [head_logits(gi, j) + mask for j in range(HEAD_GROUP)]
        logits = [with_near_bias(l_, h) for l_, h in zip(logits, hs)]
        mx = [_max_rows(l_) for l_ in logits]
        e = [jnp.exp(l_ - m_) for l_, m_ in zip(logits, mx)]
        den = [_sum_rows(e_) for e_ in e]
        o = [jnp.dot(head_rows(vt_ref, h, HEAD_DIM, slice(0, w)), e_.astype(BF16),
                     preferred_element_type=F32) for h, e_ in zip(hs, e)]
        for h, o_, d_ in zip(hs, o, den):
            o_ref[0, pl.ds(pl.multiple_of(h * HEAD_DIM, HEAD_DIM), HEAD_DIM), :] = (o_ / d_).astype(o_ref.dtype)
        return carry

    lax.fori_loop(0, n_heads // HEAD_GROUP, attn_heads, 0)


def _dsa_prompt_body(ik_ref, iqt_ref, wt_ref, k_ref, qt_ref, vt_ref, bias_ref, o_ref, mask_ref,
                     *, n_heads, n_idx_heads, seq, topk):
    i = pl.program_id(1)
    for nb in range(1, seq // Q_BLOCK + 1):
        @pl.when(i == nb - 1)
        def _(nb=nb):
            _dsa_prompt_block(nb, ik_ref, iqt_ref, wt_ref, k_ref, qt_ref, vt_ref, bias_ref, o_ref, mask_ref,
                              n_heads=n_heads, n_idx_heads=n_idx_heads, topk=topk)


def dsa_prompt(ik, iqt, wts, k_pairs, qt, vt, bias_tiles, *, topk):
    b, width, s = qt.shape
    h = width // HEAD_DIM
    ih = wts.shape[1]
    kern = functools.partial(_dsa_prompt_body, n_heads=h, n_idx_heads=ih, seq=s, topk=topk)
    grid_spec = pltpu.PrefetchScalarGridSpec(
        num_scalar_prefetch=0,
        grid=(b, s // Q_BLOCK),
        in_specs=[pl.BlockSpec((1, s, ik.shape[2]), lambda bi, i: (bi, 0, 0)),
                  pl.BlockSpec((1, iqt.shape[1], Q_BLOCK), lambda bi, i: (bi, 0, i)),
                  pl.BlockSpec((1, ih, 1, Q_BLOCK), lambda bi, i: (bi, 0, 0, i)),
                  pl.BlockSpec((1, h // 2, s, PAIR), lambda bi, i: (bi, 0, 0, 0)),
                  pl.BlockSpec((1, width, Q_BLOCK), lambda bi, i: (bi, 0, i)),
                  pl.BlockSpec((1, width, s), lambda bi, i: (bi, 0, 0)),
                  pl.BlockSpec(bias_tiles.shape, lambda bi, i: (0, 0, 0, 0))],
        out_specs=pl.BlockSpec((1, width, Q_BLOCK), lambda bi, i: (bi, 0, i)),
        scratch_shapes=[pltpu.VMEM((s, Q_BLOCK), F32)],
    )
    return pl.pallas_call(
        kern,
        grid_spec=grid_spec,
        out_shape=jax.ShapeDtypeStruct((b, width, s), BF16),
        compiler_params=_cparams("parallel", "arbitrary"),
        name="dsa_prompt",
    )(ik, iqt, wts, k_pairs, qt, vt, bias_tiles)


def _dsa_select_body(pt_ref, iq_ref, wt_ref, iknew_ref, cidx_hbm, o_ref, ikbuf, sem,
                     *, layer, n_pages, group, n_idx_heads, t_new, topk):
    s = pl.program_id(0)
    past = n_pages * PAGE_SIZE
    n_keys = past + PAGE_SIZE

    def ik_copy(i):
        g = i // n_pages
        p = i % n_pages
        page = pt_ref[(s * group + g) * n_pages + p]
        return pltpu.make_async_copy(cidx_hbm.at[layer, page], ikbuf.at[g, p], sem)

    def ik_start(i, carry):
        ik_copy(i).start()
        return carry

    def ik_wait(i, carry):
        ik_copy(i).wait()
        return carry

    lax.fori_loop(0, group * n_pages, ik_start, 0)
    for g in range(group):
        ikbuf[g, n_pages] = iknew_ref[g]
    lax.fori_loop(0, group * n_pages, ik_wait, 0)

    scores = []
    for g in range(group):
        ikt_all = jnp.concatenate([ikbuf[g, p] for p in range(n_pages + 1)], axis=1)
        dots = _mm(iq_ref[g], ikt_all)
        weighted = jnp.maximum(dots, 0.0) * wt_ref[g]
        scores.append(jnp.sum(weighted.reshape(n_idx_heads, t_new, n_keys), axis=0))
    scores = jnp.concatenate(scores, axis=0)
    shape = (group * t_new, n_keys)
    qpos = past + lax.broadcasted_iota(I32, shape, 0) % t_new
    kpos = lax.broadcasted_iota(I32, shape, 1)
    valid = kpos <= qpos
    key = jnp.where(valid, _sortable_key(scores), INT_MIN)
    sel = valid & _topk_select(key, topk, int(math.log2(n_keys)) + 1)
    o_ref[...] = jnp.where(sel, 0.0, -jnp.inf).reshape(group, t_new, n_keys)


def dsa_sample_select(page_table, iq_rows, wt_rows, ik_new_t, cache_idx_kt, *, layer, n_idx_heads, t_new, topk,
                      group):
    db, n_pages = page_table.shape
    idx_dim = cache_idx_kt.shape[2]
    n_keys = (n_pages + 1) * PAGE_SIZE
    kern = functools.partial(_dsa_select_body, layer=layer, n_pages=n_pages, group=group,
                             n_idx_heads=n_idx_heads, t_new=t_new, topk=topk)
    per_g = lambda shape: pl.BlockSpec((group,) + shape, lambda si, pt: (si,) + (0,) * len(shape))
    grid_spec = pltpu.PrefetchScalarGridSpec(
        num_scalar_prefetch=1,
        grid=(db // group,),
        in_specs=[per_g((n_idx_heads * t_new, idx_dim)), per_g((n_idx_heads * t_new, 1)),
                  per_g((idx_dim, PAGE_SIZE)), pl.BlockSpec(memory_space=pl.ANY)],
        out_specs=per_g((t_new, n_keys)),
        scratch_shapes=[pltpu.VMEM((group, n_pages + 1, idx_dim, PAGE_SIZE), F32), pltpu.SemaphoreType.DMA(())],
    )
    return pl.pallas_call(
        kern,
        grid_spec=grid_spec,
        out_shape=jax.ShapeDtypeStruct((db, t_new, n_keys), F32),
        compiler_params=_cparams("arbitrary"),
        name="dsa_sample_select",
    )(page_table.reshape(-1), iq_rows, wt_rows, ik_new_t, cache_idx_kt)


def _dsa_sample_body(pt_ref, mask_ref, qbd_ref, knew_ref, vnew_ref,
                     blast_ref, bnew_ref, ck_hbm, cv_hbm, o_ref,
                     kbuf, vbuf, sem_k, sem_v,
                     *, layer, n_pages, chunk, n_heads, t_new):
    b = pl.program_id(0)
    n_seq = pl.num_programs(0)
    past = n_pages * PAGE_SIZE
    n_chunks = n_pages // chunk
    rows = n_heads * t_new
    ck = chunk * PAGE_SIZE

    def kv_copies(seq, c, j):
        slot = c % 2
        page = pt_ref[seq * n_pages + c * chunk + j]
        return (pltpu.make_async_copy(ck_hbm.at[layer, page], kbuf.at[slot, j], sem_k.at[slot]),
                pltpu.make_async_copy(cv_hbm.at[layer, page], vbuf.at[slot, j], sem_v.at[slot]))

    def start_chunk(seq, c):
        for j in range(chunk):
            kc, vc = kv_copies(seq, c, j)
            kc.start()
            vc.start()

    def wait_chunk(c):
        for j in range(chunk):
            kc, vc = kv_copies(b, c, j)
            kc.wait()
            vc.wait()

    @pl.when(b == 0)
    def _():
        start_chunk(0, 0)

    sel_rows = jnp.tile(mask_ref[0], (n_heads, 1))

    q_rep = jnp.tile(qbd_ref[0], (n_heads, 1))
    row_head = lax.broadcasted_iota(I32, q_rep.shape, 0) // t_new
    col_head = lax.broadcasted_iota(I32, q_rep.shape, 1) // HEAD_DIM
    qbd = jnp.where(row_head == col_head, q_rep, jnp.zeros_like(q_rep))
    neg = -1e30

    def update(state, logits, maskc, vt_bf16):
        m, l, acc = state
        s = logits + maskc
        m_new = jnp.maximum(m, jnp.max(s, axis=-1, keepdims=True))
        alpha = jnp.exp(m - m_new)
        p = jnp.exp(s - m_new)
        l = alpha * l + jnp.sum(p, axis=-1, keepdims=True)
        acc = alpha * acc + _mm_nt(p, vt_bf16)
        return m_new, l, acc

    def pages_t(buf, slot):
        return jnp.concatenate([buf[slot, j] for j in range(chunk)], axis=1).astype(BF16)

    state = (jnp.full((rows, 1), neg, F32), jnp.zeros((rows, 1), F32),
             jnp.zeros((rows, qbd.shape[1]), F32))
    for c in range(n_chunks):
        if c + 1 < n_chunks:
            start_chunk(b, c + 1)
        else:
            @pl.when(b + 1 < n_seq)
            def _():
                start_chunk(b + 1, 0)
        wait_chunk(c)
        slot = c % 2
        logits = _mm(qbd, pages_t(kbuf, slot))
        if c == n_chunks - 1:
            logits = jnp.concatenate([logits[:, :ck - PAGE_SIZE],
                                      logits[:, ck - PAGE_SIZE:] + blast_ref[...]], axis=1)
        state = update(state, logits, sel_rows[:, c * ck:(c + 1) * ck], pages_t(vbuf, slot))
    logits = _mm(qbd, knew_ref[0]) + bnew_ref[...]
    m, l, acc = update(state, logits, sel_rows[:, past:], vnew_ref[0])
    out = jnp.where(row_head == col_head, acc / l, 0.0)
    o_ref[0] = jnp.sum(out.reshape(n_heads, t_new, out.shape[1]), axis=0)


def dsa_sample(page_table, mask, q_bd, k_new_t, v_new_t, bias_last, bias_new, cache_kt, cache_vt,
               *, layer, n_heads, t_new, chunk):
    db, n_pages = page_table.shape
    assert (n_pages // chunk) % 2 == 0, "chunks alternate between two buffers across sequences"
    rows = n_heads * t_new
    width = q_bd.shape[2]
    kern = functools.partial(_dsa_sample_body, layer=layer, n_pages=n_pages, chunk=chunk, n_heads=n_heads,
                             t_new=t_new)
    per_b = lambda shape: pl.BlockSpec((1,) + shape, lambda bi, pt: (bi,) + (0,) * len(shape))
    const = lambda shape: pl.BlockSpec(shape, lambda bi, pt: (0,) * len(shape))
    any_spec = pl.BlockSpec(memory_space=pl.ANY)
    grid_spec = pltpu.PrefetchScalarGridSpec(
        num_scalar_prefetch=1,
        grid=(db,),
        in_specs=[per_b((t_new, mask.shape[2])),
                  per_b((t_new, width)), per_b((width, PAGE_SIZE)), per_b((width, PAGE_SIZE)),
                  const((rows, PAGE_SIZE)), const((rows, PAGE_SIZE)),
                  any_spec, any_spec],
        out_specs=per_b((t_new, width)),
        scratch_shapes=[pltpu.VMEM((2, chunk, width, PAGE_SIZE), F32),
                        pltpu.VMEM((2, chunk, width, PAGE_SIZE), F32),
                        pltpu.SemaphoreType.DMA((2,)),
                        pltpu.SemaphoreType.DMA((2,))],
    )
    return pl.pallas_call(
        kern,
        grid_spec=grid_spec,
        out_shape=jax.ShapeDtypeStruct((db, t_new, width), F32),
        compiler_params=_cparams("arbitrary"),
        name="dsa_sample",
    )(page_table.reshape(-1), mask, q_bd, k_new_t, v_new_t, bias_last, bias_new, cache_kt, cache_vt)


def _token_minor_cache(cache):
    l, pool, page = cache.shape[:3]
    nd = cache.ndim
    return cache.transpose((0, 1) + tuple(range(3, nd)) + (2,)).reshape(l, pool, -1, page)


def _dsa_sample_inputs(q, k_new, v_new, iq, ik_new, iw, bias_tiles):
    db, t, h, dh = q.shape
    ih = iq.shape[2]
    iq_rows = iq.transpose(0, 2, 1, 3).reshape(db, ih * t, -1).astype(BF16)
    wt_rows = (iw * (ih ** -0.5 * iq.shape[3] ** -0.5)).transpose(0, 2, 1).reshape(db, ih * t, 1)
    q_bd = (q * dh ** -0.5).reshape(db, t, h * dh)
    page_t = lambda x: jnp.pad(x.reshape(db, t, -1).transpose(0, 2, 1), ((0, 0), (0, 0), (0, PAGE_SIZE - t)))
    bias_new = bias_tiles[0, :, :t, :].reshape(h * t, Q_BLOCK)
    bias_last = bias_tiles[1, :, :t, :].reshape(h * t, Q_BLOCK)
    return ((iq_rows, wt_rows, page_t(ik_new)),
            (q_bd.astype(BF16), page_t(k_new).astype(BF16), page_t(v_new).astype(BF16), bias_last, bias_new))


def _matmul_residual_body(x_ref, a_ref, w_ref, o_ref):
    o_ref[...] = x_ref[...] + jnp.dot(a_ref[...], w_ref[...], preferred_element_type=F32)


def matmul_residual(x, a, w, *, tm, tn):
    n, d = x.shape
    kd = a.shape[1]
    return pl.pallas_call(
        _matmul_residual_body,
        grid=(n // tm, d // tn),
        in_specs=[pl.BlockSpec((tm, tn), lambda i, j: (i, j)),
                  pl.BlockSpec((tm, kd), lambda i, j: (i, 0)),
                  pl.BlockSpec((kd, tn), lambda i, j: (0, j))],
        out_specs=pl.BlockSpec((tm, tn), lambda i, j: (i, j)),
        out_shape=jax.ShapeDtypeStruct((n, d), F32),
        compiler_params=_cparams("parallel", "parallel"),
        name="matmul_residual",
    )(x, a, w)


def _cross_attn_body(x_ref, g_ref, wq_ref, mk_ref, mv_ref, wo_ref, o_ref, *, groups, t_rows, n_heads, head_dim):
    x = x_ref[...]
    h = _rmsnorm(x, g_ref[...]).astype(BF16)
    q = jnp.dot(h, wq_ref[...], preferred_element_type=F32).astype(BF16)
    scale = head_dim ** -0.5
    outs = []
    for gi in range(groups):
        qg = q[gi * t_rows:(gi + 1) * t_rows]
        heads = []
        for hh in range(n_heads):
            sl = slice(hh * head_dim, (hh + 1) * head_dim)
            logits = _mm_nt(qg[:, sl], mk_ref[gi, :, sl]) * scale
            mx = jnp.max(logits, axis=-1, keepdims=True)
            e = jnp.exp(logits - mx)
            p = e / jnp.sum(e, axis=-1, keepdims=True)
            heads.append(jnp.dot(p.astype(BF16), mv_ref[gi, :, sl], preferred_element_type=F32))
        outs.append(jnp.concatenate(heads, axis=1))
    o = jnp.concatenate(outs, axis=0) if groups > 1 else outs[0]
    o_ref[...] = x + jnp.dot(o.astype(BF16), wo_ref[...], preferred_element_type=F32)


def cross_attn(x, g, wq, mk, mv, wo, *, groups, t_rows, seq_tiles, n_heads):
    n, d = x.shape
    xw = wq.shape[1]
    rows = groups * t_rows
    m = mk.shape[1]
    kern = functools.partial(_cross_attn_body, groups=groups, t_rows=t_rows, n_heads=n_heads,
                             head_dim=xw // n_heads)
    return pl.pallas_call(
        kern,
        grid=(n // rows,),
        in_specs=[pl.BlockSpec((rows, d), lambda i: (i, 0)),
                  pl.BlockSpec((1, d), lambda i: (0, 0)),
                  pl.BlockSpec((d, xw), lambda i: (0, 0)),
                  pl.BlockSpec((groups, m, xw), lambda i: (i // seq_tiles, 0, 0)),
                  pl.BlockSpec((groups, m, xw), lambda i: (i // seq_tiles, 0, 0)),
                  pl.BlockSpec((xw, d), lambda i: (0, 0))],
        out_specs=pl.BlockSpec((rows, d), lambda i: (i, 0)),
        out_shape=jax.ShapeDtypeStruct((n, d), F32),
        compiler_params=_cparams("parallel"),
        name="cross_attn",
    )(x, g.reshape(1, d), wq, mk, mv, wo)


def _router_body(x_ref, g_ref, wr_ref, br_ref, h_ref, r_ref, *, n_groups, per_group):
    h = _rmsnorm(x_ref[...], g_ref[...])
    _store_slab_rows(h_ref, h, h.shape[1] // LANES)
    logits = _mm3(h, wr_ref[...]) + br_ref[...]
    lane = lax.broadcasted_iota(I32, logits.shape, 1).astype(F32)
    big = 1e9
    first_lane = lambda hit: jnp.min(jnp.where(hit, lane, big), axis=-1, keepdims=True)
    gl = jnp.where(lane < n_groups, logits, -jnp.inf)
    gmax = jnp.max(gl, axis=-1, keepdims=True)
    grp = first_lane(gl == gmax)
    p_grp = 1.0 / jnp.sum(jnp.exp(gl - gmax), axis=-1, keepdims=True)
    e_id = lane - n_groups
    in_grp = (e_id >= grp * per_group) & (e_id < (grp + 1.0) * per_group)
    el = jnp.where(in_grp, logits, -jnp.inf)
    v1 = jnp.max(el, axis=-1, keepdims=True)
    i1 = first_lane(el == v1) - n_groups
    el2 = jnp.where(e_id == i1, -jnp.inf, el)
    v2 = jnp.max(el2, axis=-1, keepdims=True)
    i2 = first_lane(el2 == v2) - n_groups
    e2 = jnp.exp(v2 - v1)
    g1 = p_grp / (1.0 + e2)
    g2 = p_grp * e2 / (1.0 + e2)
    r_ref[...] = jnp.where(lane == 0, g1, jnp.where(lane == 1, g2, jnp.where(
        lane == 2, i1, jnp.where(lane == 3, i2, 0.0))))


def router(x, g, w_r, b_r, *, tm, n_groups, per_group):
    n, d = x.shape
    kern = functools.partial(_router_body, n_groups=n_groups, per_group=per_group)
    return pl.pallas_call(
        kern,
        grid=(n // tm,),
        in_specs=[pl.BlockSpec((tm, d), lambda i: (i, 0)),
                  pl.BlockSpec((1, d), lambda i: (0, 0)),
                  pl.BlockSpec((d, LANES), lambda i: (0, 0)),
                  pl.BlockSpec((1, LANES), lambda i: (0, 0))],
        out_specs=[pl.BlockSpec((tm * (d // LANES), LANES), lambda i: (i, 0)),
                   pl.BlockSpec((tm, LANES), lambda i: (i, 0))],
        out_shape=[jax.ShapeDtypeStruct((n * (d // LANES), LANES), F32), jax.ShapeDtypeStruct((n, LANES), F32)],
        compiler_params=_cparams("parallel"),
        name="moe_router",
    )(x, g.reshape(1, d), w_r, b_r)


def _slab_rows(buf, offset, n_rows, n_slab, stride):
    return jnp.concatenate([buf[pl.ds(offset + s, n_rows, stride=stride), :] for s in range(n_slab)], axis=1)


def _store_slab_rows(buf, x, n_slab):
    n_rows = x.shape[0]
    for s in range(n_slab):
        buf[pl.ds(s, n_rows, stride=n_slab), :] = x[:, s * LANES:(s + 1) * LANES]


def _moe_ffn_body(te_ref, nr_ref, tok_ref, dst_ref, h_hbm, w1_ref, w3_ref, w2_ref, y_hbm,
                  xbuf, ybuf, gsem, ssem, *, tm):
    t = pl.program_id(0)
    n_tiles = pl.num_programs(0)
    n_slab = h_hbm.shape[1]
    slot = t % 2

    def slab(buf, sl, r):
        return buf.at[sl, pl.ds(pl.multiple_of(r * n_slab, n_slab), n_slab)]

    def gather_copy(tile, sl, r):
        return pltpu.make_async_copy(h_hbm.at[tok_ref[tile * tm + r]], slab(xbuf, sl, r), gsem.at[sl])

    def scatter_copy(tile, sl, r):
        return pltpu.make_async_copy(slab(ybuf, sl, r), y_hbm.at[dst_ref[tile * tm + r]], ssem.at[sl])

    def start_rows(n, copy):
        def pair(i, c):
            copy(2 * i).start(priority=0)
            copy(2 * i + 1).start(priority=1)
            return c
        lax.fori_loop(0, n // 2, pair, 0)

        @pl.when(n % 2 == 1)
        def _():
            copy(n - 1).start(priority=0)

    def wait_rows(n, copy):
        def one(r, c):
            copy(r).wait()
            return c
        lax.fori_loop(0, n, one, 0)

    @pl.when(t == 0)
    def _():
        xbuf[...] = jnp.zeros(xbuf.shape, F32)
        start_rows(nr_ref[0], lambda r: gather_copy(0, 0, r))

    nxt = jnp.minimum(t + 1, n_tiles - 1)

    @pl.when(t + 1 < n_tiles)
    def _():
        start_rows(nr_ref[nxt], lambda r: gather_copy(nxt, 1 - slot, r))

    @pl.when(t >= 2)
    def _():
        prev2 = jnp.maximum(t - 2, 0)
        wait_rows(nr_ref[prev2], lambda r: scatter_copy(prev2, slot, r))

    n = nr_ref[t]

    @pl.when(n > 0)
    def _():
        wait_rows(n, lambda r: gather_copy(t, slot, r))
        x = _slab_rows(xbuf.at[slot], 0, tm, n_slab, n_slab).astype(BF16)
        a = jnp.dot(x, w1_ref[0].astype(BF16), preferred_element_type=F32)
        bgate = jnp.dot(x, w3_ref[0].astype(BF16), preferred_element_type=F32)
        u = (a / (1.0 + jnp.exp(-a))) * bgate
        y = jnp.dot(u.astype(BF16), w2_ref[0].astype(BF16), preferred_element_type=F32)
        _store_slab_rows(ybuf.at[slot], y, n_slab)
        start_rows(n, lambda r: scatter_copy(t, slot, r))

    @pl.when(t == n_tiles - 1)
    def _():
        @pl.when(t >= 1)
        def _():
            prev1 = jnp.maximum(t - 1, 0)
            wait_rows(nr_ref[prev1], lambda r: scatter_copy(prev1, 1 - slot, r))
        wait_rows(n, lambda r: scatter_copy(t, slot, r))


def moe_ffn(tile_expert, tile_rows, row_tok, row_dst, h_slabs, n_out_rows, w1, w3, w2, *, tm):
    n_tiles = tile_expert.shape[0]
    d, de = w1.shape[1], w1.shape[2]
    n_slab = d // LANES
    h3 = h_slabs.reshape(-1, n_slab, LANES)
    w_in_spec = pl.BlockSpec((1, d, de), lambda t, te, nr, tok, dst: (te[t], 0, 0))
    grid_spec = pltpu.PrefetchScalarGridSpec(
        num_scalar_prefetch=4,
        grid=(n_tiles,),
        in_specs=[pl.BlockSpec(memory_space=pl.ANY), w_in_spec, w_in_spec,
                  pl.BlockSpec((1, de, d), lambda t, te, nr, tok, dst: (te[t], 0, 0))],
        out_specs=pl.BlockSpec(memory_space=pl.ANY),
        scratch_shapes=[pltpu.VMEM((2, tm * n_slab, LANES), F32), pltpu.VMEM((2, tm * n_slab, LANES), F32),
                        pltpu.SemaphoreType.DMA((2,)), pltpu.SemaphoreType.DMA((2,))],
    )
    y = pl.pallas_call(
        functools.partial(_moe_ffn_body, tm=tm),
        grid_spec=grid_spec,
        out_shape=jax.ShapeDtypeStruct((n_out_rows, n_slab, LANES), F32),
        compiler_params=_cparams("arbitrary"),
        name="moe_ffn",
    )(tile_expert, tile_rows, row_tok, row_dst, h3, w1, w3, w2)
    return y.reshape(n_out_rows * n_slab, LANES)


def _moe_dispatch(eids, n_experts, tm):
    n, k = eids.shape
    m = n * k
    flat_e = eids.reshape(-1)
    order = jnp.argsort(flat_e, stable=True).astype(I32)
    counts = jnp.sum((flat_e[:, None] == jnp.arange(n_experts)[None, :]).astype(I32), axis=0)
    padded = (counts + tm - 1) // tm * tm
    pad_end = jnp.cumsum(padded)
    pad_start = pad_end - padded
    start = jnp.cumsum(counts) - counts
    n_tiles = -(-m // tm) + n_experts
    tile_start = jnp.arange(n_tiles) * tm
    last = jnp.maximum(pad_end[-1] - 1, 0)
    tile_expert = jnp.minimum(jnp.searchsorted(pad_end, jnp.minimum(tile_start, last), side='right'),
                              n_experts - 1).astype(I32)
    first = tile_start - pad_start[tile_expert]
    tile_rows = jnp.where(tile_start < pad_end[-1], jnp.clip(counts[tile_expert] - first, 0, tm), 0).astype(I32)
    src = jnp.clip(start[tile_expert][:, None] + first[:, None] + jnp.arange(tm)[None, :], 0, m - 1)
    row_dst = order[src].reshape(-1)
    return tile_expert, tile_rows, (row_dst // k).astype(I32), row_dst.astype(I32)


def _combine_body(x_ref, r_ref, g_ref, y_ref, o_ref, *, tm, top_k):
    n_slab = x_ref.shape[1] // LANES
    route = r_ref[...]
    x = x_ref[...]
    for kk in range(top_k):
        x = x + _slab_rows(y_ref, kk * n_slab, tm, n_slab, top_k * n_slab) * route[:, kk:kk + 1]
    o_ref[...] = _rmsnorm(x, g_ref[...])


def moe_combine(x, route, g, y_slabs, *, tm, top_k, tile_offset):
    n, d = x.shape
    rows = tm * top_k * (d // LANES)
    return pl.pallas_call(
        functools.partial(_combine_body, tm=tm, top_k=top_k),
        grid=(n // tm,),
        in_specs=[pl.BlockSpec((tm, d), lambda i: (i, 0)),
                  pl.BlockSpec((tm, LANES), lambda i: (i, 0)),
                  pl.BlockSpec((1, d), lambda i: (0, 0)),
                  pl.BlockSpec((rows, LANES), lambda i: (i + tile_offset, 0))],
        out_specs=pl.BlockSpec((tm, d), lambda i: (i, 0)),
        out_shape=jax.ShapeDtypeStruct((n, d), F32),
        compiler_params=_cparams("parallel"),
        name="moe_combine",
    )(x, route, g.reshape(1, d), y_slabs)


def kernel(x_prompt, x_sample, mem_prompt, cache_k, cache_v, cache_idx_k, page_table, state_wkv, state_shift, cache_mem_k, cache_mem_v, g_mix, w_in, mu_shift, rw_w0, rw_w2, rw_a0, rw_a2, rw_g2, rw_kk, rw_ka, rw_rk, rw_ln_g, rw_ln_b, w_out, g_cross, g_mem, w_cq, w_ck, w_cv, w_co, g_ffn, w_rg, b_rg, w_re, b_re, w_e1, w_e3, w_e2, rel_bias, g_final):
    B, S, D = x_prompt.shape
    DB, T, _ = x_sample.shape
    assert w_in.shape[0] == 1, "single-layer trunk only"
    l = 0
    n_pages = page_table.shape[1]
    past = n_pages * PAGE_SIZE
    topk_p = min(TOPK_MAX, S // 4)
    topk_s = min(TOPK_MAX, (past + T) // 4)
    rw_proj = mu_shift.shape[1]
    width = rw_w0.shape[1]
    at_w = D - width
    n_heads = at_w // HEAD_DIM
    idx_dim = cache_idx_k.shape[-1]
    ih = (w_in.shape[2] - rw_proj - 3 * at_w - idx_dim) // (idx_dim + 1)
    xw = w_cq.shape[2]
    x_heads = cache_mem_k.shape[3]
    n_mem = mem_prompt.shape[1]
    n_experts = w_e1.shape[1]
    top_k = 2
    tn = 512
    n_main = 3 * width
    n_lora = rw_proj - n_main
    n_att = 3 * at_w + ih * idx_dim
    n_tail = n_lora + idx_dim + ih
    assert n_main % tn == 0 and at_w % tn == 0 and (ih * idx_dim) % tn == 0 and n_tail <= tn

    w_t = w_in[l].T
    w_all = jnp.concatenate([w_t[:n_main], w_t[rw_proj:rw_proj + n_att], w_t[n_main:rw_proj],
                             w_t[rw_proj + n_att:], jnp.zeros((tn - n_tail, D), w_t.dtype)], axis=0).astype(BF16)
    flat, per_head = False, True
    segments = [(n_main // tn, [(flat, F32)]),
                (at_w // tn, [(flat, BF16)]),
                (at_w // tn, [(per_head, F32), (flat, BF16)]),
                (at_w // tn, [(per_head, F32), (flat, BF16)]),
                (ih * idx_dim // tn, [(flat, BF16)]),
                (1, [(flat, F32)])]
    w_out_b = w_out[l].astype(BF16)
    w_cq_b, w_co_b = w_cq[l].astype(BF16), w_co[l].astype(BF16)
    w_ckv = jnp.concatenate([w_ck[l], w_cv[l]], axis=1).astype(BF16)
    n_route = w_rg.shape[2] + w_re.shape[2]
    w_r = jnp.pad(jnp.concatenate([w_rg[l], w_re[l]], axis=1), ((0, 0), (0, LANES - n_route)))
    b_r = jnp.pad(jnp.concatenate([b_rg[l], b_re[l]]), (0, LANES - n_route)).reshape(1, LANES)
    rw_args = (mu_shift[l], rw_w0[l], rw_w2[l], rw_a0[l], rw_a2[l], rw_g2[l], rw_kk[l], rw_ka[l],
               rw_rk[l].reshape(-1))
    tiles = bias_tables(rel_bias, ((0, False), (Q_BLOCK, False), (0, True), (Q_BLOCK, True)))

    def project(x2d, b_, t_, tm):
        f_main, q, k_heads, k, v_heads, v, iq, f_tail = norm_matmul_split(x2d, g_mix[l], w_all, segments,
                                                                          tm=tm, tn=tn)
        r3 = lambda z: z.reshape(b_, t_, -1)
        f_main, f_tail = r3(f_main), r3(f_tail)
        ik = f_tail[..., n_lora:n_lora + idx_dim]
        iw = f_tail[..., n_lora + idx_dim:n_tail]
        shift = jnp.concatenate([f_main[:, -1], f_tail[:, -1, :n_lora]], axis=-1)
        heads5 = lambda z: z.reshape(1, b_, t_, n_heads, HEAD_DIM)
        return f_main, r3(q), heads5(k_heads), r3(k), heads5(v_heads), r3(v), r3(iq), f_tail, ik, iw, shift

    def rw_rows(y):
        b_, p_, t_, _ = y.shape
        return y.transpose(0, 2, 1, 3).reshape(b_ * t_, p_ * PAIR).astype(BF16)

    xp = x_prompt.reshape(B * S, D)
    fm_p, q, k_p, kb_p, v_p, vb_p, iq, ft_p, ik_p, iw, shift_p = project(xp, B, S, 512)
    feats_p = rwkv_prep(fm_p, ft_p, jnp.zeros((B, rw_proj), F32), *rw_args, tm=256, width=width)
    rw_p, st_p = rwkv_chunk(feats_p, jnp.zeros((B, width // PAIR, PAIR, PAIR), F32), rw_ln_g[l], rw_ln_b[l], L=64)
    tr = lambda z: z.transpose(0, 2, 1)
    k_pairs = kb_p.reshape(B, S, n_heads // 2, PAIR).transpose(0, 2, 1, 3)
    at_p = dsa_prompt(ik_p.astype(BF16), tr(iq), tr(iw * (ih ** -0.5 * idx_dim ** -0.5))[:, :, None, :],
                      k_pairs, tr(q * HEAD_DIM ** -0.5), tr(vb_p), tiles[2:4], topk=topk_p)
    mix_p = jnp.concatenate([rw_rows(rw_p), tr(at_p).reshape(B * S, at_w)], axis=1)
    x1_p = matmul_residual(xp, mix_p, w_out_b, tm=512, tn=512)
    mkv = norm_matmul(mem_prompt.reshape(B * n_mem, D), g_mem[l], w_ckv, tm=256, tn=512)
    mk_p = mkv[:, :xw].reshape(B, n_mem, xw)
    mv_p = mkv[:, xw:].reshape(B, n_mem, xw)
    x2_p = cross_attn(x1_p, g_cross[l], w_cq_b, mk_p.astype(BF16), mv_p.astype(BF16), w_co_b,
                      groups=1, t_rows=512, seq_tiles=S // 512, n_heads=x_heads)

    xs = x_sample.reshape(DB * T, D)
    fm_s, q2, k_s, kb_s, v_s, vb_s, iq2, ft_s, ik_s, iw2, shift_s = project(xs, DB, T, DB * T)
    feats_s = rwkv_prep(fm_s, ft_s, state_shift[l], *rw_args, tm=T, width=width)
    rw_s, st_s = rwkv_chunk(feats_s, _state_to_block_diag(state_wkv[l]), rw_ln_g[l], rw_ln_b[l], L=T)
    r4 = lambda z, h_: z.reshape(DB, T, h_, -1)
    sel_args, att_args = _dsa_sample_inputs(r4(q2, n_heads), r4(kb_s, n_heads), r4(vb_s, n_heads), r4(iq2, ih),
                                            ik_s, iw2, tiles[0:2])
    mask_s = dsa_sample_select(page_table, *sel_args, _token_minor_cache(cache_idx_k), layer=l,
                               n_idx_heads=ih, t_new=T, topk=topk_s, group=4)
    at_s = dsa_sample(page_table, mask_s, *att_args, _token_minor_cache(cache_k), _token_minor_cache(cache_v),
                      layer=l, n_heads=n_heads, t_new=T, chunk=8)
    mix_s = jnp.concatenate([rw_rows(rw_s), at_s.reshape(DB * T, at_w).astype(BF16)], axis=1)
    x1_s = matmul_residual(xs, mix_s, w_out_b, tm=DB * T, tn=512)
    x2_s = cross_attn(x1_s, g_cross[l], w_cq_b, cache_mem_k[l].reshape(DB, n_mem, xw).astype(BF16),
                      cache_mem_v[l].reshape(DB, n_mem, xw).astype(BF16), w_co_b,
                      groups=8, t_rows=T, seq_tiles=1, n_heads=x_heads)

    h_p, route_p = router(x2_p, g_ffn[l], w_r, b_r, tm=512, n_groups=w_rg.shape[2],
                          per_group=w_re.shape[2] // w_rg.shape[2])
    h_s, route_s = router(x2_s, g_ffn[l], w_r, b_r, tm=DB * T, n_groups=w_rg.shape[2],
                          per_group=w_re.shape[2] // w_rg.shape[2])
    h_all = jnp.concatenate([h_p, h_s], axis=0)
    eids = jnp.concatenate([route_p[:, top_k:2 * top_k], route_s[:, top_k:2 * top_k]], axis=0).astype(I32)
    tm_moe = DB * T
    n_p, n_all = B * S, B * S + DB * T
    assert n_p % tm_moe == 0
    tile_expert, tile_rows, row_tok, row_dst = _moe_dispatch(eids, n_experts, tm_moe)
    y_slabs = moe_ffn(tile_expert, tile_rows, row_tok, row_dst, h_all, n_all * top_k,
                      w_e1[l], w_e3[l], w_e2[l], tm=tm_moe)
    y_p = moe_combine(x2_p, route_p, g_final, y_slabs, tm=tm_moe, top_k=top_k, tile_offset=0)
    y_s = moe_combine(x2_s, route_s, g_final, y_slabs, tm=tm_moe, top_k=top_k, tile_offset=n_p // tm_moe)

    return (y_p.reshape(B, S, D), y_s.reshape(DB, T, D),
            k_p, v_p, ik_p[None], _state_from_block_diag(st_p)[None], shift_p[None],
            mk_p.reshape(1, B, n_mem, x_heads, xw // x_heads), mv_p.reshape(1, B, n_mem, x_heads, xw // x_heads),
            k_s, v_s, ik_s[None], _state_from_block_diag(st_s)[None], shift_s[None])
```

```python
import functools
import math

import jax
import jax.numpy as jnp
from jax import lax
from jax.experimental import pallas as pl
from jax.experimental.pallas import tpu as pltpu

F32 = jnp.float32
BF16 = jnp.bfloat16
I32 = jnp.int32

LANES = 128
SUBLANES = 8
VMEM_LIMIT_BYTES = 56 * 1024 * 1024

HEAD_DIM = 64
PAIR = 2 * HEAD_DIM
GN_EPS = 64e-5
NORM_EPS = 1e-6
TOPK_MAX = 256
Q_BLOCK = 128
N_BUCKETS = 32
MAX_DISTANCE = 128
PAGE_SIZE = 128
N_GROUPS = 4
EXPERTS_PER_GROUP = 8
INT_MIN = -(2 ** 31)


def _cparams(*sem):
    return pltpu.CompilerParams(dimension_semantics=sem, vmem_limit_bytes=VMEM_LIMIT_BYTES)


def _mm(a, b):
    return jnp.dot(a.astype(BF16), b.astype(BF16), preferred_element_type=F32)


def _mm_nt(a, b):
    return lax.dot_general(a.astype(BF16), b.astype(BF16), (((1,), (1,)), ((), ())),
                           preferred_element_type=F32)


def _split2(x):
    hi = x.astype(BF16)
    lo = (x - hi.astype(F32)).astype(BF16)
    return hi, lo


def _mm3(a, b):
    ah, al = _split2(a)
    bh, bl = _split2(b)
    d = lambda x, y: jnp.dot(x, y, preferred_element_type=F32)
    return d(ah, bh) + (d(ah, bl) + d(al, bh))


def _mm_exact_rhs(a, b_bf16):
    hi = a.astype(BF16)
    r1 = a - hi.astype(F32)
    mid = r1.astype(BF16)
    lo = (r1 - mid.astype(F32)).astype(BF16)
    d = lambda x: jnp.dot(x, b_bf16, preferred_element_type=F32)
    return d(hi) + (d(mid) + d(lo))


def _rmsnorm(x, g):
    ms = jnp.mean(x * x, axis=-1, keepdims=True)
    return x * lax.rsqrt(ms + NORM_EPS) * g


def _norm_matmul_body(x_ref, g_ref, w_ref, o_ref, xn_ref):
    @pl.when(pl.program_id(1) == 0)
    def _():
        xn_ref[...] = _rmsnorm(x_ref[...], g_ref[...]).astype(BF16)

    o_ref[...] = _mm(xn_ref[...], w_ref[...])


def norm_matmul(x, g, w, *, tm, tn):
    n, d = x.shape
    m = w.shape[1]
    return pl.pallas_call(
        _norm_matmul_body,
        grid=(n // tm, m // tn),
        in_specs=[pl.BlockSpec((tm, d), lambda i, j: (i, 0)),
                  pl.BlockSpec((1, d), lambda i, j: (0, 0)),
                  pl.BlockSpec((d, tn), lambda i, j: (0, j))],
        out_specs=pl.BlockSpec((tm, tn), lambda i, j: (i, j)),
        out_shape=jax.ShapeDtypeStruct((n, m), F32),
        scratch_shapes=[pltpu.VMEM((tm, d), BF16)],
        compiler_params=_cparams("parallel", "arbitrary"),
        name="norm_matmul",
    )(x, g.reshape(1, d), w)


def _norm_matmul_split_body(x_ref, g_ref, w_ref, *rest, bounds):
    o_refs, xn_ref = rest[:-1], rest[-1]
    j = pl.program_id(1)

    @pl.when(j == 0)
    def _():
        xn_ref[...] = _rmsnorm(x_ref[...], g_ref[...]).astype(BF16)

    res = _mm_nt(xn_ref[...], w_ref[...])
    tm, tn = res.shape
    for o_ref, (lo, hi) in zip(o_refs, bounds):
        @pl.when((j >= lo) & (j < hi))
        def _(o_ref=o_ref, lo=lo, hi=hi):
            if o_ref.shape[1] == HEAD_DIM:
                n_heads = (hi - lo) * tn // HEAD_DIM
                per_tile = tn // HEAD_DIM
                for c in range(hi - lo):
                    @pl.when(j == lo + c)
                    def _(c=c):
                        for hh in range(per_tile):
                            o_ref[pl.ds(c * per_tile + hh, tm, stride=n_heads), :] = (
                                res[:, hh * HEAD_DIM:(hh + 1) * HEAD_DIM].astype(o_ref.dtype))
            else:
                o_ref[...] = res.astype(o_ref.dtype)


def norm_matmul_split(x, g, wt, segments, *, tm, tn):
    n, d = x.shape
    m = wt.shape[0]
    bounds, out_specs, out_shape, lo = [], [], [], 0
    for nt, outs in segments:
        hi = lo + nt
        for per_head, dt in outs:
            bounds.append((lo, hi))
            if per_head:
                heads = nt * tn // HEAD_DIM
                out_specs.append(pl.BlockSpec((tm * heads, HEAD_DIM), lambda i, j: (i, 0)))
                out_shape.append(jax.ShapeDtypeStruct((n * heads, HEAD_DIM), dt))
            else:
                out_specs.append(pl.BlockSpec((tm, tn),
                                              lambda i, j, lo=lo, hi=hi: (i, jnp.clip(j - lo, 0, hi - lo - 1))))
                out_shape.append(jax.ShapeDtypeStruct((n, nt * tn), dt))
        lo = hi
    assert lo * tn == m
    return pl.pallas_call(
        functools.partial(_norm_matmul_split_body, bounds=tuple(bounds)),
        grid=(n // tm, m // tn),
        in_specs=[pl.BlockSpec((tm, d), lambda i, j: (i, 0)),
                  pl.BlockSpec((1, d), lambda i, j: (0, 0)),
                  pl.BlockSpec((tn, d), lambda i, j: (j, 0))],
        out_specs=out_specs,
        out_shape=out_shape,
        scratch_shapes=[pltpu.VMEM((tm, d), BF16)],
        compiler_params=_cparams("arbitrary", "arbitrary"),
        name="norm_matmul_split",
    )(x, g.reshape(1, d), wt)


def _pair_ones():
    r = lax.broadcasted_iota(I32, (PAIR, PAIR), 0) // HEAD_DIM
    c = lax.broadcasted_iota(I32, (PAIR, PAIR), 1) // HEAD_DIM
    return (r == c).astype(BF16)


def _head_sum(x, ones_bd):
    return _mm_exact_rhs(x, ones_bd)


def _rwkv_prep_body(f_ref, prev8_ref, init_ref, mu_ref, ft_ref, tprev8_ref, tinit_ref, tmu_ref,
                    w0_ref, w2_ref, a0_ref, a2_ref, g2_ref, kk_ref, ka_ref, rk_ref,
                    r_o, k_o, v_o, kk_o, b_o, ld_o, g_o, bon_o, *, tm, width):
    i = pl.program_id(1)

    def token_shift(f, p8_ref, i_ref, m_ref):
        nc = m_ref.shape[1]
        prev_row = jnp.where(i == 0, i_ref[0], p8_ref[0, SUBLANES - 1:SUBLANES, :nc])
        rolled = pltpu.roll(f, shift=1, axis=0)
        row = lax.broadcasted_iota(I32, f.shape, 0)
        f_prev = jnp.where(row == 0, prev_row, rolled)
        return f + (f_prev - f) * m_ref[...]

    fs = token_shift(f_ref[0], prev8_ref, init_ref, mu_ref)
    n_dec = w2_ref.shape[0]
    n_icl = a2_ref.shape[0]
    n_lora = tmu_ref.shape[1]
    ts = token_shift(ft_ref[0][:, :n_lora], tprev8_ref, tinit_ref, tmu_ref)
    w_ = width
    r = fs[:, 0:w_]
    k = fs[:, w_:2 * w_]
    v = fs[:, 2 * w_:3 * w_]
    wd = ts[:, 0:n_dec]
    ad = ts[:, n_dec:n_dec + n_icl]
    gd = ts[:, n_dec + n_icl:]
    z = w0_ref[...] + _mm3(jnp.tanh(wd), w2_ref[...])
    nz = -z
    softplus = jnp.maximum(nz, 0.0) + jnp.log(1.0 + jnp.exp(-jnp.abs(nz)))
    w = -softplus - 0.5
    ld = -jnp.exp(w)
    a = 1.0 / (1.0 + jnp.exp(-(a0_ref[...] + _mm3(ad, a2_ref[...]))))
    g = _mm3(1.0 / (1.0 + jnp.exp(-gd)), g2_ref[...])
    kk = k * kk_ref[...]
    k2 = k * (1.0 + (a - 1.0) * ka_ref[...])
    rk = r * k2 * rk_ref[...]
    ones_bd = _pair_ones()
    for p in range(w_ // PAIR):
        sl = slice(p * PAIR, (p + 1) * PAIR)
        kkp = kk[:, sl]
        nrm = jnp.sqrt(_head_sum(kkp * kkp, ones_bd))
        kkp = kkp / jnp.maximum(nrm, 1e-12)
        ap = a[:, sl]
        r_o[0, p] = r[:, sl]
        k_o[0, p] = k2[:, sl]
        v_o[0, p] = v[:, sl]
        kk_o[0, p] = kkp
        b_o[0, p] = kkp * ap
        ld_o[0, p] = ld[:, sl]
        g_o[0, p] = g[:, sl]
        bon_o[0, p] = _head_sum(rk[:, sl], ones_bd) * v[:, sl]


def rwkv_prep(f_main, f_tail, init_prev, mu, w0, w2, a0, a2, g2, k_k, k_a, r_k, *, tm, width):
    b, t, n_main = f_main.shape
    n_tail = f_tail.shape[2]
    n_lora = mu.shape[0] - n_main
    npair = width // PAIR
    row1 = lambda x: x.reshape(1, -1)
    kern = functools.partial(_rwkv_prep_body, tm=tm, width=width)
    full = lambda a: pl.BlockSpec(a.shape, lambda bi, i: (0,) * a.ndim)
    args = [row1(w0), w2, row1(a0), a2, g2, row1(k_k), row1(k_a), row1(r_k)]
    out_spec = pl.BlockSpec((1, npair, tm, PAIR), lambda bi, i: (bi, 0, i, 0))
    out_shape = jax.ShapeDtypeStruct((b, npair, t, PAIR), F32)
    prev8_map = lambda bi, i: (bi, jnp.maximum(i * (tm // SUBLANES) - 1, 0), 0)

    def feature_specs(ncols, n_init):
        return [pl.BlockSpec((1, tm, ncols), lambda bi, i: (bi, i, 0)),
                pl.BlockSpec((1, SUBLANES, ncols), prev8_map),
                pl.BlockSpec((1, 1, n_init), lambda bi, i: (bi, 0, 0)),
                pl.BlockSpec((1, n_init), lambda bi, i: (0, 0))]

    return pl.pallas_call(
        kern,
        grid=(b, t // tm),
        in_specs=feature_specs(n_main, n_main) + feature_specs(n_tail, n_lora) + [full(a) for a in args],
        out_specs=[out_spec] * 8,
        out_shape=[out_shape] * 8,
        compiler_params=_cparams("parallel", "parallel"),
        name="rwkv_prep",
    )(f_main, f_main, init_prev[:, :n_main].reshape(b, 1, n_main), row1(mu[:n_main]),
      f_tail, f_tail, init_prev[:, n_main:].reshape(b, 1, n_lora), row1(mu[n_main:]), *args)


def _rwkv_chunk_body(r_ref, k_ref, v_ref, kk_ref, b_ref, ld_ref, g_ref, bon_ref, s0_ref,
                     lng_ref, lnb_ref, o_ref, st_ref, s_ref, *, L, npair, group):
    c = pl.program_id(1)

    @pl.when(c == 0)
    def _():
        s_ref[...] = s0_ref[0]

    L2 = 2 * L
    row = lax.broadcasted_iota(I32, (L2, L2), 0)
    col = lax.broadcasted_iota(I32, (L2, L2), 1)
    same = (row // L) == (col // L)
    tri_strict = same & (col < row)
    tri_incl = same & (col <= row)
    eye = (row == col).astype(F32)
    tr = lax.broadcasted_iota(I32, (L, L), 0)
    tc = lax.broadcasted_iota(I32, (L, L), 1)
    cum_mat = (tc <= tr).astype(BF16)
    lane = lax.broadcasted_iota(I32, (L, PAIR), 1)
    first = lane < HEAD_DIM
    ones_bd = _pair_ones()
    n_sq = max(int(math.ceil(math.log2(L))) - 1, 0)

    def block_diag(x):
        return jnp.concatenate([jnp.where(first, x, 0.0), jnp.where(first, 0.0, x)], axis=0)

    def group_step(gi, carry):
        ps = [gi * group + j for j in range(group)]
        each = lambda f, *cols: [f(*args) for args in zip(*cols)]
        ld = [ld_ref[0, p] for p in ps]
        cum = each(lambda x: _mm_exact_rhs_t(cum_mat, x), ld)
        dec = each(jnp.exp, cum)
        dec_inv = each(lambda c_: jnp.exp(-c_), cum)
        a_t = each(lambda p, c_, l_: block_diag(-kk_ref[0, p] * jnp.exp(c_ - l_)), ps, cum, ld)
        b_t = each(lambda p, e: block_diag(b_ref[0, p] * e), ps, dec_inv)
        k_t = each(lambda p, e: block_diag(k_ref[0, p] * e), ps, dec_inv)
        r_t = each(lambda p, e: block_diag(r_ref[0, p] * e), ps, dec)
        v_b = each(lambda p: block_diag(v_ref[0, p]), ps)
        a_ab = each(lambda a, b: jnp.where(tri_strict, _mm_nt(a, b), 0.0), a_t, b_t)
        a_ak = each(lambda a, k: jnp.where(tri_strict, _mm_nt(a, k), 0.0), a_t, k_t)
        a_rb = each(lambda r, b: jnp.where(tri_incl, _mm_nt(r, b), 0.0), r_t, b_t)
        a_rk = each(lambda r, k: jnp.where(tri_incl, _mm_nt(r, k), 0.0), r_t, k_t)
        x = a_ab
        t_inv = each(lambda a: eye + a, a_ab)
        for _ in range(n_sq):
            x = each(lambda x_: _mm(x_, x_), x)
            t_inv = each(lambda t, x_: t + _mm(x_, t), t_inv, x)
        s = [s_ref[p] for p in ps]
        rhs = each(lambda a, s_, ak, v: _mm_nt(a, s_) + _mm(ak, v), a_t, s, a_ak, v_b)
        u = each(_mm, t_inv, rhs)
        y_b = each(lambda r, s_, rb, u_, rk, v: _mm_nt(r, s_) + _mm(rb, u_) + _mm(rk, v),
                   r_t, s, a_rb, u, a_rk, v_b)
        s_new = each(lambda s_, u_, b, v, k, d: (s_ + _mm(u_.T, b) + _mm(v.T, k)) * d[L - 1:L, :],
                     s, u, b_t, v_b, k_t, dec)
        for p, sn in zip(ps, s_new):
            s_ref[p] = sn
        y = each(lambda yb: yb[:L] + yb[L:], y_b)
        mean = each(lambda y_: _head_sum(y_, ones_bd) * (1.0 / HEAD_DIM), y)
        d = each(lambda y_, m: y_ - m, y, mean)
        var = each(lambda d_: _head_sum(d_ * d_, ones_bd) * (1.0 / HEAD_DIM), d)
        for p, d_, v_ in zip(ps, d, var):
            yn = d_ * lax.rsqrt(v_ + GN_EPS) * lng_ref[p] + lnb_ref[p]
            o_ref[0, p] = (yn + bon_ref[0, p]) * g_ref[0, p]
        return carry

    lax.fori_loop(0, npair // group, group_step, 0)

    @pl.when(c == pl.num_programs(1) - 1)
    def _():
        st_ref[0] = s_ref[...]


def _mm_exact_rhs_t(m_bf16, x):
    hi = x.astype(BF16)
    r1 = x - hi.astype(F32)
    mid = r1.astype(BF16)
    lo = (r1 - mid.astype(F32)).astype(BF16)
    d = lambda y: jnp.dot(m_bf16, y, preferred_element_type=F32)
    return d(hi) + (d(mid) + d(lo))


def rwkv_chunk(feats, s0_bd, ln_g, ln_b, *, L, group=8):
    b, npair, t, _ = feats[0].shape
    blk = pl.BlockSpec((1, npair, L, PAIR), lambda bi, c: (bi, 0, c, 0))
    st_spec = pl.BlockSpec((1, npair, PAIR, PAIR), lambda bi, c: (bi, 0, 0, 0))
    par_spec = pl.BlockSpec((npair, 1, PAIR), lambda bi, c: (0, 0, 0))
    kern = functools.partial(_rwkv_chunk_body, L=L, npair=npair, group=group)
    return pl.pallas_call(
        kern,
        grid=(b, t // L),
        in_specs=[blk] * 8 + [st_spec, par_spec, par_spec],
        out_specs=[blk, st_spec],
        out_shape=[jax.ShapeDtypeStruct((b, npair, t, PAIR), F32),
                   jax.ShapeDtypeStruct((b, npair, PAIR, PAIR), F32)],
        scratch_shapes=[pltpu.VMEM((npair, PAIR, PAIR), F32)],
        compiler_params=_cparams("parallel", "arbitrary"),
        name="rwkv_chunk",
    )(*feats, s0_bd, ln_g.reshape(npair, 1, PAIR), ln_b.reshape(npair, 1, PAIR))


def _state_to_block_diag(s):
    b, h, n, _ = s.shape
    s = s.reshape(b, h // 2, 2, n, n)
    z = jnp.zeros_like(s[:, :, 0])
    top = jnp.concatenate([s[:, :, 0], z], axis=-1)
    bot = jnp.concatenate([z, s[:, :, 1]], axis=-1)
    return jnp.concatenate([top, bot], axis=-2)


def _state_from_block_diag(s_bd):
    b, p, _, _ = s_bd.shape
    n = HEAD_DIM
    return jnp.stack([s_bd[:, :, :n, :n], s_bd[:, :, n:, n:]], axis=2).reshape(b, 2 * p, n, n)


def _t5_bucket(dist):
    exact = N_BUCKETS // 2
    d = jnp.maximum(dist, 0)
    far = exact + (jnp.log(jnp.maximum(d, 1).astype(F32) / exact) / math.log(MAX_DISTANCE / exact)
                   * (N_BUCKETS - exact)).astype(I32)
    return jnp.where(d < exact, d, jnp.minimum(far, N_BUCKETS - 1))


def _bias_tables_body(rb_ref, o_ref, *, offsets, n_heads):
    r = lax.broadcasted_iota(I32, (Q_BLOCK, Q_BLOCK), 0)
    c = lax.broadcasted_iota(I32, (Q_BLOCK, Q_BLOCK), 1)
    for t, (off, key_major) in enumerate(offsets):
        bucket = _t5_bucket((c - r if key_major else r - c) + off)
        for h in range(n_heads):
            def body(bk, acc):
                return jnp.where(bucket == bk, rb_ref[bk, h], acc)
            tile = lax.fori_loop(0, N_BUCKETS, body, jnp.zeros((Q_BLOCK, Q_BLOCK), F32))
            o_ref[t, h] = tile - rb_ref[N_BUCKETS - 1, h]


def bias_tables(rel_bias, offsets):
    n_heads = rel_bias.shape[1]
    kern = functools.partial(_bias_tables_body, offsets=tuple(offsets), n_heads=n_heads)
    return pl.pallas_call(
        kern,
        in_specs=[pl.BlockSpec(memory_space=pltpu.SMEM)],
        out_specs=pl.BlockSpec(memory_space=pltpu.VMEM),
        out_shape=jax.ShapeDtypeStruct((len(offsets), n_heads, Q_BLOCK, Q_BLOCK), F32),
        name="bias_tables",
    )(rel_bias)


def _sortable_key(scores):
    bits = lax.bitcast_convert_type(scores + 0.0, I32)
    return jnp.where(bits < 0, bits ^ 0x7FFFFFFF, bits)


def _count(mask):
    return jnp.sum(mask.astype(F32), axis=-1, keepdims=True)


def _topk_select(key, topk, n_index_bits):
    rows, n = key.shape
    kf = float(topk)
    t0 = jnp.where(_count(key >= 0) >= kf, 0, INT_MIN).astype(I32)

    def value_bit(i, t):
        cand = t + lax.shift_left(jnp.int32(1), 30 - i)
        return jnp.where(_count(key >= cand) >= kf, cand, t)

    thr = lax.fori_loop(0, 31, value_bit, t0)
    above = key > thr
    ties = key == thr
    need = kf - _count(above)
    idx = lax.broadcasted_iota(I32, (rows, n), 1)

    def lowest_ties():
        def index_bit(i, m):
            cand = m + lax.shift_left(jnp.int32(1), n_index_bits - 1 - i)
            return jnp.where(_count(ties & (idx < cand)) <= need, cand, m)
        return lax.fori_loop(0, n_index_bits, index_bit, jnp.zeros((rows, 1), I32))

    surplus = jnp.max(_count(ties) - need) > 0.0
    m = lax.cond(surplus, lowest_ties, lambda: jnp.full((rows, 1), 2 ** n_index_bits, I32))
    return above | (ties & (idx < m))


SUM_CHAINS = 4
HEAD_GROUP = 8


def _sum_rows(x):
    r = x.shape[0]
    if r % (SUM_CHAINS * SUBLANES) == 0 and r > SUM_CHAINS * SUBLANES:
        x = jnp.sum(x.reshape(SUM_CHAINS, r // SUM_CHAINS, x.shape[1]), axis=1)
    return jnp.sum(x, axis=0, keepdims=True)


def _max_rows(x):
    r = x.shape[0]
    if r % (SUM_CHAINS * SUBLANES) == 0 and r > SUM_CHAINS * SUBLANES:
        x = jnp.max(x.reshape(SUM_CHAINS, r // SUM_CHAINS, x.shape[1]), axis=1)
    return jnp.max(x, axis=0, keepdims=True)


def _topk_select_cols(key, topk, n_index_bits):
    n, cols = key.shape
    kf = float(topk)
    cnt = lambda m: _sum_rows(m.astype(F32))
    t0 = jnp.where(cnt(key >= 0) >= kf, 0, INT_MIN).astype(I32)

    def value_bit(i, t):
        cand = t + lax.shift_left(jnp.int32(1), 30 - i)
        return jnp.where(cnt(key >= cand) >= kf, cand, t)

    thr = lax.fori_loop(0, 31, value_bit, t0)
    above = key > thr
    ties = key == thr
    need = kf - cnt(above)
    idx = lax.broadcasted_iota(I32, (n, cols), 0)

    def lowest_ties():
        def index_bit(i, m):
            cand = m + lax.shift_left(jnp.int32(1), n_index_bits - 1 - i)
            return jnp.where(cnt(ties & (idx < cand)) <= need, cand, m)
        return lax.fori_loop(0, n_index_bits, index_bit, jnp.zeros((1, cols), I32))

    surplus = jnp.max(cnt(ties) - need) > 0.0
    m = lax.cond(surplus, lowest_ties, lambda: jnp.full((1, cols), 2 ** n_index_bits, I32))
    return above | (ties & (idx < m))


def _dsa_prompt_block(nb, ik_ref, iqt_ref, wt_ref, k_ref, qt_ref, vt_ref, bias_ref, o_ref, mask_ref,
                      *, n_heads, n_idx_heads, topk):
    w = nb * Q_BLOCK
    ik = ik_ref[0, :w, :]
    idx_dim = ik.shape[1]

    def head_rows(ref, h, dh, cols=slice(None)):
        return ref[0, pl.ds(pl.multiple_of(h * dh, dh), dh), cols]

    def idx_heads(gi, acc):
        hs = [gi * HEAD_GROUP + j for j in range(HEAD_GROUP)]
        dots = [jnp.dot(ik, head_rows(iqt_ref, h, idx_dim), preferred_element_type=F32) for h in hs]
        terms = [jnp.maximum(d, 0.0) * wt_ref[0, h] for d, h in zip(dots, hs)]
        while len(terms) > 1:
            terms = [a + b for a, b in zip(terms[0::2], terms[1::2])]
        return acc + terms[0]

    scores = lax.fori_loop(0, n_idx_heads // HEAD_GROUP, idx_heads, jnp.zeros((w, Q_BLOCK), F32))
    kpos = lax.broadcasted_iota(I32, (w, Q_BLOCK), 0)
    qpos = (nb - 1) * Q_BLOCK + lax.broadcasted_iota(I32, (w, Q_BLOCK), 1)
    valid = kpos <= qpos
    if w <= topk:
        sel = valid
    else:
        key = jnp.where(valid, _sortable_key(scores), INT_MIN)
        sel = valid & _topk_select_cols(key, topk, int(math.ceil(math.log2(w))) + 1)
    mask_ref[:w, :] = jnp.where(sel, 0.0, -jnp.inf)

    def with_near_bias(logits, h):
        near = [logits[w - Q_BLOCK:] + bias_ref[0, h]]
        if nb >= 2:
            near = [logits[w - 2 * Q_BLOCK:w - Q_BLOCK] + bias_ref[1, h]] + near
        if nb >= 3:
            near = [logits[:w - 2 * Q_BLOCK]] + near
        return jnp.concatenate(near, axis=0) if len(near) > 1 else near[0]

    pair_row_half = lax.broadcasted_iota(I32, (PAIR, Q_BLOCK), 0) // HEAD_DIM

    def head_logits(gi, j):
        p = gi * (HEAD_GROUP // 2) + j // 2
        qt_pair = head_rows(qt_ref, p, PAIR)
        qt_head = jnp.where(pair_row_half == j % 2, qt_pair, jnp.zeros_like(qt_pair))
        return jnp.dot(k_ref[0, p, :w, :], qt_head, preferred_element_type=F32)

    def attn_heads(gi, carry):
        hs = [gi * HEAD_GROUP + j for j in range(HEAD_GROUP)]
        mask = mask_ref[:w, :]
        logits = [head_logits(gi, j) + mask for j in range(HEAD_GROUP)]
        logits = [with_near_bias(l_, h) for l_, h in zip(logits, hs)]
        mx = [_max_rows(l_) for l_ in logits]
        e = [jnp.exp(l_ - m_) for l_, m_ in zip(logits, mx)]
        den = [_sum_rows(e_) for e_ in e]
        o = [jnp.dot(head_rows(vt_ref, h, HEAD_DIM, slice(0, w)), e_.astype(BF16),
                     preferred_element_type=F32) for h, e_ in zip(hs, e)]
        for h, o_, d_ in zip(hs, o, den):
            o_ref[0, pl.ds(pl.multiple_of(h * HEAD_DIM, HEAD_DIM), HEAD_DIM), :] = (o_ / d_).astype(o_ref.dtype)
        return carry

    lax.fori_loop(0, n_heads // HEAD_GROUP, attn_heads, 0)


def _dsa_prompt_body(ik_ref, iqt_ref, wt_ref, k_ref, qt_ref, vt_ref, bias_ref, o_ref, mask_ref,
                     *, n_heads, n_idx_heads, seq, topk):
    i = pl.program_id(1)
    for nb in range(1, seq // Q_BLOCK + 1):
        @pl.when(i == nb - 1)
        def _(nb=nb):
            _dsa_prompt_block(nb, ik_ref, iqt_ref, wt_ref, k_ref, qt_ref, vt_ref, bias_ref, o_ref, mask_ref,
                              n_heads=n_heads, n_idx_heads=n_idx_heads, topk=topk)


def dsa_prompt(ik, iqt, wts, k_pairs, qt, vt, bias_tiles, *, topk):
    b, width, s = qt.shape
    h = width // HEAD_DIM
    ih = wts.shape[1]
    kern = functools.partial(_dsa_prompt_body, n_heads=h, n_idx_heads=ih, seq=s, topk=topk)
    grid_spec = pltpu.PrefetchScalarGridSpec(
        num_scalar_prefetch=0,
        grid=(b, s // Q_BLOCK),
        in_specs=[pl.BlockSpec((1, s, ik.shape[2]), lambda bi, i: (bi, 0, 0)),
                  pl.BlockSpec((1, iqt.shape[1], Q_BLOCK), lambda bi, i: (bi, 0, i)),
                  pl.BlockSpec((1, ih, 1, Q_BLOCK), lambda bi, i: (bi, 0, 0, i)),
                  pl.BlockSpec((1, h // 2, s, PAIR), lambda bi, i: (bi, 0, 0, 0)),
                  pl.BlockSpec((1, width, Q_BLOCK), lambda bi, i: (bi, 0, i)),
                  pl.BlockSpec((1, width, s), lambda bi, i: (bi, 0, 0)),
                  pl.BlockSpec(bias_tiles.shape, lambda bi, i: (0, 0, 0, 0))],
        out_specs=pl.BlockSpec((1, width, Q_BLOCK), lambda bi, i: (bi, 0, i)),
        scratch_shapes=[pltpu.VMEM((s, Q_BLOCK), F32)],
    )
    return pl.pallas_call(
        kern,
        grid_spec=grid_spec,
        out_shape=jax.ShapeDtypeStruct((b, width, s), BF16),
        compiler_params=_cparams("parallel", "arbitrary"),
        name="dsa_prompt",
    )(ik, iqt, wts, k_pairs, qt, vt, bias_tiles)


def _dsa_select_body(pt_ref, iq_ref, wt_ref, iknew_ref, cidx_hbm, o_ref, ikbuf, sem,
                     *, layer, n_pages, group, n_idx_heads, t_new, topk):
    s = pl.program_id(0)
    past = n_pages * PAGE_SIZE
    n_keys = past + PAGE_SIZE

    def ik_copy(i):
        g = i // n_pages
        p = i % n_pages
        page = pt_ref[(s * group + g) * n_pages + p]
        return pltpu.make_async_copy(cidx_hbm.at[layer, page], ikbuf.at[g, p], sem)

    def ik_start(i, carry):
        ik_copy(i).start()
        return carry

    def ik_wait(i, carry):
        ik_copy(i).wait()
        return carry

    lax.fori_loop(0, group * n_pages, ik_start, 0)
    for g in range(group):
        ikbuf[g, n_pages] = iknew_ref[g]
    lax.fori_loop(0, group * n_pages, ik_wait, 0)

    scores = []
    for g in range(group):
        ikt_all = jnp.concatenate([ikbuf[g, p] for p in range(n_pages + 1)], axis=1)
        dots = _mm(iq_ref[g], ikt_all)
        weighted = jnp.maximum(dots, 0.0) * wt_ref[g]
        scores.append(jnp.sum(weighted.reshape(n_idx_heads, t_new, n_keys), axis=0))
    scores = jnp.concatenate(scores, axis=0)
    shape = (group * t_new, n_keys)
    qpos = past + lax.broadcasted_iota(I32, shape, 0) % t_new
    kpos = lax.broadcasted_iota(I32, shape, 1)
    valid = kpos <= qpos
    key = jnp.where(valid, _sortable_key(scores), INT_MIN)
    sel = valid & _topk_select(key, topk, int(math.log2(n_keys)) + 1)
    o_ref[...] = jnp.where(sel, 0.0, -jnp.inf).reshape(group, t_new, n_keys)


def dsa_sample_select(page_table, iq_rows, wt_rows, ik_new_t, cache_idx_kt, *, layer, n_idx_heads, t_new, topk,
                      group):
    db, n_pages = page_table.shape
    idx_dim = cache_idx_kt.shape[2]
    n_keys = (n_pages + 1) * PAGE_SIZE
    kern = functools.partial(_dsa_select_body, layer=layer, n_pages=n_pages, group=group,
                             n_idx_heads=n_idx_heads, t_new=t_new, topk=topk)
    per_g = lambda shape: pl.BlockSpec((group,) + shape, lambda si, pt: (si,) + (0,) * len(shape))
    grid_spec = pltpu.PrefetchScalarGridSpec(
        num_scalar_prefetch=1,
        grid=(db // group,),
        in_specs=[per_g((n_idx_heads * t_new, idx_dim)), per_g((n_idx_heads * t_new, 1)),
                  per_g((idx_dim, PAGE_SIZE)), pl.BlockSpec(memory_space=pl.ANY)],
        out_specs=per_g((t_new, n_keys)),
        scratch_shapes=[pltpu.VMEM((group, n_pages + 1, idx_dim, PAGE_SIZE), F32), pltpu.SemaphoreType.DMA(())],
    )
    return pl.pallas_call(
        kern,
        grid_spec=grid_spec,
        out_shape=jax.ShapeDtypeStruct((db, t_new, n_keys), F32),
        compiler_params=_cparams("arbitrary"),
        name="dsa_sample_select",
    )(page_table.reshape(-1), iq_rows, wt_rows, ik_new_t, cache_idx_kt)


def _dsa_sample_body(pt_ref, mask_ref, qbd_ref, knew_ref, vnew_ref,
                     blast_ref, bnew_ref, ck_hbm, cv_hbm, o_ref,
                     kbuf, vbuf, sem_k, sem_v,
                     *, layer, n_pages, chunk, n_heads, t_new):
    b = pl.program_id(0)
    n_seq = pl.num_programs(0)
    past = n_pages * PAGE_SIZE
    n_chunks = n_pages // chunk
    rows = n_heads * t_new
    ck = chunk * PAGE_SIZE

    def kv_copies(seq, c, j):
        slot = c % 2
        page = pt_ref[seq * n_pages + c * chunk + j]
        return (pltpu.make_async_copy(ck_hbm.at[layer, page], kbuf.at[slot, j], sem_k.at[slot]),
                pltpu.make_async_copy(cv_hbm.at[layer, page], vbuf.at[slot, j], sem_v.at[slot]))

    def start_chunk(seq, c):
        for j in range(chunk):
            kc, vc = kv_copies(seq, c, j)
            kc.start()
            vc.start()

    def wait_chunk(c):
        for j in range(chunk):
            kc, vc = kv_copies(b, c, j)
            kc.wait()
            vc.wait()

    @pl.when(b == 0)
    def _():
        start_chunk(0, 0)

    sel_rows = jnp.tile(mask_ref[0], (n_heads, 1))

    q_rep = jnp.tile(qbd_ref[0], (n_heads, 1))
    row_head = lax.broadcasted_iota(I32, q_rep.shape, 0) // t_new
    col_head = lax.broadcasted_iota(I32, q_rep.shape, 1) // HEAD_DIM
    qbd = jnp.where(row_head == col_head, q_rep, jnp.zeros_like(q_rep))
    neg = -1e30

    def update(state, logits, maskc, vt_bf16):
        m, l, acc = state
        s = logits + maskc
        m_new = jnp.maximum(m, jnp.max(s, axis=-1, keepdims=True))
        alpha = jnp.exp(m - m_new)
        p = jnp.exp(s - m_new)
        l = alpha * l + jnp.sum(p, axis=-1, keepdims=True)
        acc = alpha * acc + _mm_nt(p, vt_bf16)
        return m_new, l, acc

    def pages_t(buf, slot):
        return jnp.concatenate([buf[slot, j] for j in range(chunk)], axis=1).astype(BF16)

    state = (jnp.full((rows, 1), neg, F32), jnp.zeros((rows, 1), F32),
             jnp.zeros((rows, qbd.shape[1]), F32))
    for c in range(n_chunks):
        if c + 1 < n_chunks:
            start_chunk(b, c + 1)
        else:
            @pl.when(b + 1 < n_seq)
            def _():
                start_chunk(b + 1, 0)
        wait_chunk(c)
        slot = c % 2
        logits = _mm(qbd, pages_t(kbuf, slot))
        if c == n_chunks - 1:
            logits = jnp.concatenate([logits[:, :ck - PAGE_SIZE],
                                      logits[:, ck - PAGE_SIZE:] + blast_ref[...]], axis=1)
        state = update(state, logits, sel_rows[:, c * ck:(c + 1) * ck], pages_t(vbuf, slot))
    logits = _mm(qbd, knew_ref[0]) + bnew_ref[...]
    m, l, acc = update(state, logits, sel_rows[:, past:], vnew_ref[0])
    out = jnp.where(row_head == col_head, acc / l, 0.0)
    o_ref[0] = jnp.sum(out.reshape(n_heads, t_new, out.shape[1]), axis=0)


def dsa_sample(page_table, mask, q_bd, k_new_t, v_new_t, bias_last, bias_new, cache_kt, cache_vt,
               *, layer, n_heads, t_new, chunk):
    db, n_pages = page_table.shape
    assert (n_pages // chunk) % 2 == 0, "chunks alternate between two buffers across sequences"
    rows = n_heads * t_new
    width = q_bd.shape[2]
    kern = functools.partial(_dsa_sample_body, layer=layer, n_pages=n_pages, chunk=chunk, n_heads=n_heads,
                             t_new=t_new)
    per_b = lambda shape: pl.BlockSpec((1,) + shape, lambda bi, pt: (bi,) + (0,) * len(shape))
    const = lambda shape: pl.BlockSpec(shape, lambda bi, pt: (0,) * len(shape))
    any_spec = pl.BlockSpec(memory_space=pl.ANY)
    grid_spec = pltpu.PrefetchScalarGridSpec(
        num_scalar_prefetch=1,
        grid=(db,),
        in_specs=[per_b((t_new, mask.shape[2])),
                  per_b((t_new, width)), per_b((width, PAGE_SIZE)), per_b((width, PAGE_SIZE)),
                  const((rows, PAGE_SIZE)), const((rows, PAGE_SIZE)),
                  any_spec, any_spec],
        out_specs=per_b((t_new, width)),
        scratch_shapes=[pltpu.VMEM((2, chunk, width, PAGE_SIZE), F32),
                        pltpu.VMEM((2, chunk, width, PAGE_SIZE), F32),
                        pltpu.SemaphoreType.DMA((2,)),
                        pltpu.SemaphoreType.DMA((2,))],
    )
    return pl.pallas_call(
        kern,
        grid_spec=grid_spec,
        out_shape=jax.ShapeDtypeStruct((db, t_new, width), F32),
        compiler_params=_cparams("arbitrary"),
        name="dsa_sample",
    )(page_table.reshape(-1), mask, q_bd, k_new_t, v_new_t, bias_last, bias_new, cache_kt, cache_vt)


def _token_minor_cache(cache):
    l, pool, page = cache.shape[:3]
    nd = cache.ndim
    return cache.transpose((0, 1) + tuple(range(3, nd)) + (2,)).reshape(l, pool, -1, page)


def _dsa_sample_inputs(q, k_new, v_new, iq, ik_new, iw, bias_tiles):
    db, t, h, dh = q.shape
    ih = iq.shape[2]
    iq_rows = iq.transpose(0, 2, 1, 3).reshape(db, ih * t, -1).astype(BF16)
    wt_rows = (iw * (ih ** -0.5 * iq.shape[3] ** -0.5)).transpose(0, 2, 1).reshape(db, ih * t, 1)
    q_bd = q.reshape(db, t, h * dh)
    page_t = lambda x: jnp.pad(x.reshape(db, t, -1).transpose(0, 2, 1), ((0, 0), (0, 0), (0, PAGE_SIZE - t)))
    bias_new = bias_tiles[0, :, :t, :].reshape(h * t, Q_BLOCK)
    bias_last = bias_tiles[1, :, :t, :].reshape(h * t, Q_BLOCK)
    return ((iq_rows, wt_rows, page_t(ik_new)),
            (q_bd.astype(BF16), page_t(k_new).astype(BF16), page_t(v_new).astype(BF16), bias_last, bias_new))


def _matmul_residual_body(x_ref, a_ref, b_ref, w_ref, o_ref):
    ka = a_ref.shape[1]
    o_ref[...] = x_ref[...] + (_mm(a_ref[...], w_ref[:ka, :]) + _mm(b_ref[...], w_ref[ka:, :]))


def matmul_residual(x, a, b, w, *, tm, tn):
    n, d = x.shape
    ka, kb = a.shape[1], b.shape[1]
    return pl.pallas_call(
        _matmul_residual_body,
        grid=(n // tm, d // tn),
        in_specs=[pl.BlockSpec((tm, tn), lambda i, j: (i, j)),
                  pl.BlockSpec((tm, ka), lambda i, j: (i, 0)),
                  pl.BlockSpec((tm, kb), lambda i, j: (i, 0)),
                  pl.BlockSpec((ka + kb, tn), lambda i, j: (0, j))],
        out_specs=pl.BlockSpec((tm, tn), lambda i, j: (i, j)),
        out_shape=jax.ShapeDtypeStruct((n, d), F32),
        compiler_params=_cparams("parallel", "parallel"),
        name="matmul_residual",
    )(x, a, b, w)


def _cross_attn_body(x_ref, g_ref, wq_ref, mk_ref, mv_ref, wo_ref, o_ref, *, groups, t_rows, n_heads, head_dim):
    x = x_ref[...]
    h = _rmsnorm(x, g_ref[...]).astype(BF16)
    q = jnp.dot(h, wq_ref[...], preferred_element_type=F32).astype(BF16)
    scale = head_dim ** -0.5
    outs = []
    for gi in range(groups):
        qg = q[gi * t_rows:(gi + 1) * t_rows]
        heads = []
        for hh in range(n_heads):
            sl = slice(hh * head_dim, (hh + 1) * head_dim)
            logits = _mm_nt(qg[:, sl], mk_ref[gi, :, sl]) * scale
            mx = jnp.max(logits, axis=-1, keepdims=True)
            e = jnp.exp(logits - mx)
            p = e / jnp.sum(e, axis=-1, keepdims=True)
            heads.append(jnp.dot(p.astype(BF16), mv_ref[gi, :, sl], preferred_element_type=F32))
        outs.append(jnp.concatenate(heads, axis=1))
    o = jnp.concatenate(outs, axis=0) if groups > 1 else outs[0]
    o_ref[...] = x + jnp.dot(o.astype(BF16), wo_ref[...], preferred_element_type=F32)


def cross_attn(x, g, wq, mk, mv, wo, *, groups, t_rows, seq_tiles, n_heads):
    n, d = x.shape
    xw = wq.shape[1]
    rows = groups * t_rows
    m = mk.shape[1]
    kern = functools.partial(_cross_attn_body, groups=groups, t_rows=t_rows, n_heads=n_heads,
                             head_dim=xw // n_heads)
    return pl.pallas_call(
        kern,
        grid=(n // rows,),
        in_specs=[pl.BlockSpec((rows, d), lambda i: (i, 0)),
                  pl.BlockSpec((1, d), lambda i: (0, 0)),
                  pl.BlockSpec((d, xw), lambda i: (0, 0)),
                  pl.BlockSpec((groups, m, xw), lambda i: (i // seq_tiles, 0, 0)),
                  pl.BlockSpec((groups, m, xw), lambda i: (i // seq_tiles, 0, 0)),
                  pl.BlockSpec((xw, d), lambda i: (0, 0))],
        out_specs=pl.BlockSpec((rows, d), lambda i: (i, 0)),
        out_shape=jax.ShapeDtypeStruct((n, d), F32),
        compiler_params=_cparams("parallel"),
        name="cross_attn",
    )(x, g.reshape(1, d), wq, mk, mv, wo)


def _router_body(x_ref, g_ref, wr_ref, br_ref, h_ref, r_ref, *, n_groups, per_group):
    h = _rmsnorm(x_ref[...], g_ref[...])
    _store_slab_rows(h_ref, h, h.shape[1] // LANES)
    logits = _mm3(h, wr_ref[...]) + br_ref[...]
    lane = lax.broadcasted_iota(I32, logits.shape, 1).astype(F32)
    big = 1e9
    first_lane = lambda hit: jnp.min(jnp.where(hit, lane, big), axis=-1, keepdims=True)
    gl = jnp.where(lane < n_groups, logits, -jnp.inf)
    gmax = jnp.max(gl, axis=-1, keepdims=True)
    grp = first_lane(gl == gmax)
    p_grp = 1.0 / jnp.sum(jnp.exp(gl - gmax), axis=-1, keepdims=True)
    e_id = lane - n_groups
    in_grp = (e_id >= grp * per_group) & (e_id < (grp + 1.0) * per_group)
    el = jnp.where(in_grp, logits, -jnp.inf)
    v1 = jnp.max(el, axis=-1, keepdims=True)
    i1 = first_lane(el == v1) - n_groups
    el2 = jnp.where(e_id == i1, -jnp.inf, el)
    v2 = jnp.max(el2, axis=-1, keepdims=True)
    i2 = first_lane(el2 == v2) - n_groups
    e2 = jnp.exp(v2 - v1)
    g1 = p_grp / (1.0 + e2)
    g2 = p_grp * e2 / (1.0 + e2)
    r_ref[...] = jnp.where(lane == 0, g1, jnp.where(lane == 1, g2, jnp.where(
        lane == 2, i1, jnp.where(lane == 3, i2, 0.0))))


def router(x, g, w_r, b_r, *, tm, n_groups, per_group):
    n, d = x.shape
    kern = functools.partial(_router_body, n_groups=n_groups, per_group=per_group)
    return pl.pallas_call(
        kern,
        grid=(n // tm,),
        in_specs=[pl.BlockSpec((tm, d), lambda i: (i, 0)),
                  pl.BlockSpec((1, d), lambda i: (0, 0)),
                  pl.BlockSpec((d, LANES), lambda i: (0, 0)),
                  pl.BlockSpec((1, LANES), lambda i: (0, 0))],
        out_specs=[pl.BlockSpec((tm * (d // LANES), LANES), lambda i: (i, 0)),
                   pl.BlockSpec((tm, LANES), lambda i: (i, 0))],
        out_shape=[jax.ShapeDtypeStruct((n * (d // LANES), LANES), F32), jax.ShapeDtypeStruct((n, LANES), F32)],
        compiler_params=_cparams("parallel"),
        name="moe_router",
    )(x, g.reshape(1, d), w_r, b_r)


def _slab_rows(buf, offset, n_rows, n_slab, stride):
    return jnp.concatenate([buf[pl.ds(offset + s, n_rows, stride=stride), :] for s in range(n_slab)], axis=1)


def _store_slab_rows(buf, x, n_slab):
    n_rows = x.shape[0]
    for s in range(n_slab):
        buf[pl.ds(s, n_rows, stride=n_slab), :] = x[:, s * LANES:(s + 1) * LANES]


def _moe_ffn_body(te_ref, nr_ref, tok_ref, dst_ref, h_hbm, w1_ref, w3_ref, w2_ref, y_hbm,
                  xbuf, ybuf, gsem, ssem, *, tm):
    t = pl.program_id(0)
    n_tiles = pl.num_programs(0)
    n_slab = h_hbm.shape[1]
    slot = t % 2

    def slab(buf, sl, r):
        return buf.at[sl, pl.ds(pl.multiple_of(r * n_slab, n_slab), n_slab)]

    def gather_copy(tile, sl, r):
        return pltpu.make_async_copy(h_hbm.at[tok_ref[tile * tm + r]], slab(xbuf, sl, r), gsem.at[sl])

    def scatter_copy(tile, sl, r):
        return pltpu.make_async_copy(slab(ybuf, sl, r), y_hbm.at[dst_ref[tile * tm + r]], ssem.at[sl])

    def start_rows(n, copy):
        def pair(i, c):
            copy(2 * i).start(priority=0)
            copy(2 * i + 1).start(priority=1)
            return c
        lax.fori_loop(0, n // 2, pair, 0)

        @pl.when(n % 2 == 1)
        def _():
            copy(n - 1).start(priority=0)

    def wait_rows(n, copy):
        def one(r, c):
            copy(r).wait()
            return c
        lax.fori_loop(0, n, one, 0)

    @pl.when(t == 0)
    def _():
        xbuf[...] = jnp.zeros(xbuf.shape, F32)
        start_rows(nr_ref[0], lambda r: gather_copy(0, 0, r))

    nxt = jnp.minimum(t + 1, n_tiles - 1)

    @pl.when(t + 1 < n_tiles)
    def _():
        start_rows(nr_ref[nxt], lambda r: gather_copy(nxt, 1 - slot, r))

    @pl.when(t >= 2)
    def _():
        prev2 = jnp.maximum(t - 2, 0)
        wait_rows(nr_ref[prev2], lambda r: scatter_copy(prev2, slot, r))

    n = nr_ref[t]

    @pl.when(n > 0)
    def _():
        wait_rows(n, lambda r: gather_copy(t, slot, r))
        x = _slab_rows(xbuf.at[slot], 0, tm, n_slab, n_slab).astype(BF16)
        a = jnp.dot(x, w1_ref[0].astype(BF16), preferred_element_type=F32)
        bgate = jnp.dot(x, w3_ref[0].astype(BF16), preferred_element_type=F32)
        u = (a / (1.0 + jnp.exp(-a))) * bgate
        y = jnp.dot(u.astype(BF16), w2_ref[0].astype(BF16), preferred_element_type=F32)
        _store_slab_rows(ybuf.at[slot], y, n_slab)
        start_rows(n, lambda r: scatter_copy(t, slot, r))

    @pl.when(t == n_tiles - 1)
    def _():
        @pl.when(t >= 1)
        def _():
            prev1 = jnp.maximum(t - 1, 0)
            wait_rows(nr_ref[prev1], lambda r: scatter_copy(prev1, 1 - slot, r))
        wait_rows(n, lambda r: scatter_copy(t, slot, r))


def moe_ffn(tile_expert, tile_rows, row_tok, row_dst, h_slabs, n_out_rows, w1, w3, w2, *, tm):
    n_tiles = tile_expert.shape[0]
    d, de = w1.shape[1], w1.shape[2]
    n_slab = d // LANES
    h3 = h_slabs.reshape(-1, n_slab, LANES)
    w_in_spec = pl.BlockSpec((1, d, de), lambda t, te, nr, tok, dst: (te[t], 0, 0))
    grid_spec = pltpu.PrefetchScalarGridSpec(
        num_scalar_prefetch=4,
        grid=(n_tiles,),
        in_specs=[pl.BlockSpec(memory_space=pl.ANY), w_in_spec, w_in_spec,
                  pl.BlockSpec((1, de, d), lambda t, te, nr, tok, dst: (te[t], 0, 0))],
        out_specs=pl.BlockSpec(memory_space=pl.ANY),
        scratch_shapes=[pltpu.VMEM((2, tm * n_slab, LANES), F32), pltpu.VMEM((2, tm * n_slab, LANES), F32),
                        pltpu.SemaphoreType.DMA((2,)), pltpu.SemaphoreType.DMA((2,))],
    )
    y = pl.pallas_call(
        functools.partial(_moe_ffn_body, tm=tm),
        grid_spec=grid_spec,
        out_shape=jax.ShapeDtypeStruct((n_out_rows, n_slab, LANES), F32),
        compiler_params=_cparams("arbitrary"),
        name="moe_ffn",
    )(tile_expert, tile_rows, row_tok, row_dst, h3, w1, w3, w2)
    return y.reshape(n_out_rows * n_slab, LANES)


def _moe_dispatch(eids, n_experts, tm):
    n, k = eids.shape
    m = n * k
    flat_e = eids.reshape(-1)
    order = jnp.argsort(flat_e, stable=True).astype(I32)
    counts = jnp.sum((flat_e[:, None] == jnp.arange(n_experts)[None, :]).astype(I32), axis=0)
    padded = (counts + tm - 1) // tm * tm
    pad_end = jnp.cumsum(padded)
    pad_start = pad_end - padded
    start = jnp.cumsum(counts) - counts
    n_tiles = -(-m // tm) + n_experts
    tile_start = jnp.arange(n_tiles) * tm
    last = jnp.maximum(pad_end[-1] - 1, 0)
    tile_expert = jnp.minimum(jnp.searchsorted(pad_end, jnp.minimum(tile_start, last), side='right'),
                              n_experts - 1).astype(I32)
    first = tile_start - pad_start[tile_expert]
    tile_rows = jnp.where(tile_start < pad_end[-1], jnp.clip(counts[tile_expert] - first, 0, tm), 0).astype(I32)
    src = jnp.clip(start[tile_expert][:, None] + first[:, None] + jnp.arange(tm)[None, :], 0, m - 1)
    row_dst = order[src].reshape(-1)
    return tile_expert, tile_rows, (row_dst // k).astype(I32), row_dst.astype(I32)


def _combine_body(x_ref, r_ref, g_ref, y_ref, o_ref, *, tm, top_k):
    n_slab = x_ref.shape[1] // LANES
    route = r_ref[...]
    x = x_ref[...]
    for kk in range(top_k):
        x = x + _slab_rows(y_ref, kk * n_slab, tm, n_slab, top_k * n_slab) * route[:, kk:kk + 1]
    o_ref[...] = _rmsnorm(x, g_ref[...])


def moe_combine(x, route, g, y_slabs, *, tm, top_k, tile_offset):
    n, d = x.shape
    rows = tm * top_k * (d // LANES)
    return pl.pallas_call(
        functools.partial(_combine_body, tm=tm, top_k=top_k),
        grid=(n // tm,),
        in_specs=[pl.BlockSpec((tm, d), lambda i: (i, 0)),
                  pl.BlockSpec((tm, LANES), lambda i: (i, 0)),
                  pl.BlockSpec((1, d), lambda i: (0, 0)),
                  pl.BlockSpec((rows, LANES), lambda i: (i + tile_offset, 0))],
        out_specs=pl.BlockSpec((tm, d), lambda i: (i, 0)),
        out_shape=jax.ShapeDtypeStruct((n, d), F32),
        compiler_params=_cparams("parallel"),
        name="moe_combine",
    )(x, route, g.reshape(1, d), y_slabs)


def kernel(x_prompt, x_sample, mem_prompt, cache_k, cache_v, cache_idx_k, page_table, state_wkv, state_shift, cache_mem_k, cache_mem_v, g_mix, w_in, mu_shift, rw_w0, rw_w2, rw_a0, rw_a2, rw_g2, rw_kk, rw_ka, rw_rk, rw_ln_g, rw_ln_b, w_out, g_cross, g_mem, w_cq, w_ck, w_cv, w_co, g_ffn, w_rg, b_rg, w_re, b_re, w_e1, w_e3, w_e2, rel_bias, g_final):
    B, S, D = x_prompt.shape
    DB, T, _ = x_sample.shape
    assert w_in.shape[0] == 1, "single-layer trunk only"
    l = 0
    n_pages = page_table.shape[1]
    past = n_pages * PAGE_SIZE
    topk_p = min(TOPK_MAX, S // 4)
    topk_s = min(TOPK_MAX, (past + T) // 4)
    rw_proj = mu_shift.shape[1]
    width = rw_w0.shape[1]
    at_w = D - width
    n_heads = at_w // HEAD_DIM
    idx_dim = cache_idx_k.shape[-1]
    ih = (w_in.shape[2] - rw_proj - 3 * at_w - idx_dim) // (idx_dim + 1)
    xw = w_cq.shape[2]
    x_heads = cache_mem_k.shape[3]
    n_mem = mem_prompt.shape[1]
    n_experts = w_e1.shape[1]
    top_k = 2
    tn = 512
    n_main = 3 * width
    n_lora = rw_proj - n_main
    n_att = 3 * at_w + ih * idx_dim
    n_tail = n_lora + idx_dim + ih
    assert n_main % tn == 0 and at_w % tn == 0 and (ih * idx_dim) % tn == 0 and n_tail <= tn

    w_t = w_in[l].T
    w_all = jnp.concatenate([w_t[:n_main], w_t[rw_proj:rw_proj + at_w] * HEAD_DIM ** -0.5,
                             w_t[rw_proj + at_w:rw_proj + n_att], w_t[n_main:rw_proj],
                             w_t[rw_proj + n_att:], jnp.zeros((tn - n_tail, D), w_t.dtype)], axis=0).astype(BF16)
    flat, per_head = False, True
    segments = [(n_main // tn, [(flat, F32)]),
                (at_w // tn, [(flat, BF16)]),
                (at_w // tn, [(per_head, F32), (flat, BF16)]),
                (at_w // tn, [(per_head, F32), (flat, BF16)]),
                (ih * idx_dim // tn, [(flat, BF16)]),
                (1, [(flat, F32)])]
    w_out_b = w_out[l].astype(BF16)
    w_cq_b, w_co_b = w_cq[l].astype(BF16), w_co[l].astype(BF16)
    w_ckv = jnp.concatenate([w_ck[l], w_cv[l]], axis=1).astype(BF16)
    n_route = w_rg.shape[2] + w_re.shape[2]
    w_r = jnp.pad(jnp.concatenate([w_rg[l], w_re[l]], axis=1), ((0, 0), (0, LANES - n_route)))
    b_r = jnp.pad(jnp.concatenate([b_rg[l], b_re[l]]), (0, LANES - n_route)).reshape(1, LANES)
    rw_args = (mu_shift[l], rw_w0[l], rw_w2[l], rw_a0[l], rw_a2[l], rw_g2[l], rw_kk[l], rw_ka[l],
               rw_rk[l].reshape(-1))
    tiles = bias_tables(rel_bias, ((0, False), (Q_BLOCK, False), (0, True), (Q_BLOCK, True)))

    def project(x2d, b_, t_, tm):
        f_main, q, k_heads, k, v_heads, v, iq, f_tail = norm_matmul_split(x2d, g_mix[l], w_all, segments,
                                                                          tm=tm, tn=tn)
        r3 = lambda z: z.reshape(b_, t_, -1)
        f_main, f_tail = r3(f_main), r3(f_tail)
        ik = f_tail[..., n_lora:n_lora + idx_dim]
        iw = f_tail[..., n_lora + idx_dim:n_tail]
        shift = jnp.concatenate([f_main[:, -1], f_tail[:, -1, :n_lora]], axis=-1)
        heads5 = lambda z: z.reshape(1, b_, t_, n_heads, HEAD_DIM)
        return f_main, r3(q), heads5(k_heads), r3(k), heads5(v_heads), r3(v), r3(iq), f_tail, ik, iw, shift

    def rw_rows(y):
        b_, p_, t_, _ = y.shape
        return y.transpose(0, 2, 1, 3).reshape(b_ * t_, p_ * PAIR).astype(BF16)

    xp = x_prompt.reshape(B * S, D)
    fm_p, q, k_p, kb_p, v_p, vb_p, iq, ft_p, ik_p, iw, shift_p = project(xp, B, S, 512)
    feats_p = rwkv_prep(fm_p, ft_p, jnp.zeros((B, rw_proj), F32), *rw_args, tm=256, width=width)
    rw_p, st_p = rwkv_chunk(feats_p, jnp.zeros((B, width // PAIR, PAIR, PAIR), F32), rw_ln_g[l], rw_ln_b[l], L=64)
    tr = lambda z: z.transpose(0, 2, 1)
    k_pairs = kb_p.reshape(B, S, n_heads // 2, PAIR).transpose(0, 2, 1, 3)
    at_p = dsa_prompt(ik_p.astype(BF16), tr(iq), tr(iw * (ih ** -0.5 * idx_dim ** -0.5))[:, :, None, :],
                      k_pairs, tr(q), tr(vb_p), tiles[2:4], topk=topk_p)
    x1_p = matmul_residual(xp, rw_rows(rw_p), tr(at_p).reshape(B * S, at_w), w_out_b, tm=512, tn=512)
    mkv = norm_matmul(mem_prompt.reshape(B * n_mem, D), g_mem[l], w_ckv, tm=256, tn=512)
    mk_p = mkv[:, :xw].reshape(B, n_mem, xw)
    mv_p = mkv[:, xw:].reshape(B, n_mem, xw)
    x2_p = cross_attn(x1_p, g_cross[l], w_cq_b, mk_p.astype(BF16), mv_p.astype(BF16), w_co_b,
                      groups=1, t_rows=512, seq_tiles=S // 512, n_heads=x_heads)

    xs = x_sample.reshape(DB * T, D)
    fm_s, q2, k_s, kb_s, v_s, vb_s, iq2, ft_s, ik_s, iw2, shift_s = project(xs, DB, T, DB * T)
    feats_s = rwkv_prep(fm_s, ft_s, state_shift[l], *rw_args, tm=T, width=width)
    rw_s, st_s = rwkv_chunk(feats_s, _state_to_block_diag(state_wkv[l]), rw_ln_g[l], rw_ln_b[l], L=T)
    r4 = lambda z, h_: z.reshape(DB, T, h_, -1)
    sel_args, att_args = _dsa_sample_inputs(r4(q2, n_heads), r4(kb_s, n_heads), r4(vb_s, n_heads), r4(iq2, ih),
                                            ik_s, iw2, tiles[0:2])
    mask_s = dsa_sample_select(page_table, *sel_args, _token_minor_cache(cache_idx_k), layer=l,
                               n_idx_heads=ih, t_new=T, topk=topk_s, group=4)
    at_s = dsa_sample(page_table, mask_s, *att_args, _token_minor_cache(cache_k), _token_minor_cache(cache_v),
                      layer=l, n_heads=n_heads, t_new=T, chunk=8)
    x1_s = matmul_residual(xs, rw_rows(rw_s), at_s.reshape(DB * T, at_w), w_out_b, tm=DB * T, tn=512)
    x2_s = cross_attn(x1_s, g_cross[l], w_cq_b, cache_mem_k[l].reshape(DB, n_mem, xw).astype(BF16),
                      cache_mem_v[l].reshape(DB, n_mem, xw).astype(BF16), w_co_b,
                      groups=8, t_rows=T, seq_tiles=1, n_heads=x_heads)

    h_p, route_p = router(x2_p, g_ffn[l], w_r, b_r, tm=512, n_groups=w_rg.shape[2],
                          per_group=w_re.shape[2] // w_rg.shape[2])
    h_s, route_s = router(x2_s, g_ffn[l], w_r, b_r, tm=DB * T, n_groups=w_rg.shape[2],
                          per_group=w_re.shape[2] // w_rg.shape[2])
    h_all = jnp.concatenate([h_p, h_s], axis=0)
    eids = jnp.concatenate([route_p[:, top_k:2 * top_k], route_s[:, top_k:2 * top_k]], axis=0).astype(I32)
    tm_moe = DB * T
    n_p, n_all = B * S, B * S + DB * T
    assert n_p % tm_moe == 0
    tile_expert, tile_rows, row_tok, row_dst = _moe_dispatch(eids, n_experts, tm_moe)
    y_slabs = moe_ffn(tile_expert, tile_rows, row_tok, row_dst, h_all, n_all * top_k,
                      w_e1[l], w_e3[l], w_e2[l], tm=tm_moe)
    y_p = moe_combine(x2_p, route_p, g_final, y_slabs, tm=tm_moe, top_k=top_k, tile_offset=0)
    y_s = moe_combine(x2_s, route_s, g_final, y_slabs, tm=tm_moe, top_k=top_k, tile_offset=n_p // tm_moe)

    return (y_p.reshape(B, S, D), y_s.reshape(DB, T, D),
            k_p, v_p, ik_p[None], _state_from_block_diag(st_p)[None], shift_p[None],
            mk_p.reshape(1, B, n_mem, x_heads, xw // x_heads), mv_p.reshape(1, B, n_mem, x_heads, xw // x_heads),
            k_s, v_s, ik_s[None], _state_from_block_diag(st_s)[None], shift_s[None])
```

```python
import functools
import math

import jax
import jax.numpy as jnp
from jax import lax
from jax.experimental import pallas as pl
from jax.experimental.pallas import tpu as pltpu

F32 = jnp.float32
BF16 = jnp.bfloat16
I32 = jnp.int32

LANES = 128
SUBLANES = 8
VMEM_LIMIT_BYTES = 56 * 1024 * 1024

HEAD_DIM = 64
PAIR = 2 * HEAD_DIM
GN_EPS = 64e-5
NORM_EPS = 1e-6
TOPK_MAX = 256
Q_BLOCK = 128
N_BUCKETS = 32
MAX_DISTANCE = 128
PAGE_SIZE = 128
N_GROUPS = 4
EXPERTS_PER_GROUP = 8
INT_MIN = -(2 ** 31)


def _cparams(*sem):
    return pltpu.CompilerParams(dimension_semantics=sem, vmem_limit_bytes=VMEM_LIMIT_BYTES)


def _mm(a, b):
    return jnp.dot(a.astype(BF16), b.astype(BF16), preferred_element_type=F32)


def _mm_nt(a, b):
    return lax.dot_general(a.astype(BF16), b.astype(BF16), (((1,), (1,)), ((), ())),
                           preferred_element_type=F32)


def _split2(x):
    hi = x.astype(BF16)
    lo = (x - hi.astype(F32)).astype(BF16)
    return hi, lo


def _mm3(a, b):
    ah, al = _split2(a)
    bh, bl = _split2(b)
    d = lambda x, y: jnp.dot(x, y, preferred_element_type=F32)
    return d(ah, bh) + (d(ah, bl) + d(al, bh))


def _mm_exact_rhs(a, b_bf16):
    hi = a.astype(BF16)
    r1 = a - hi.astype(F32)
    mid = r1.astype(BF16)
    lo = (r1 - mid.astype(F32)).astype(BF16)
    d = lambda x: jnp.dot(x, b_bf16, preferred_element_type=F32)
    return d(hi) + (d(mid) + d(lo))


def _rmsnorm(x, g):
    ms = jnp.mean(x * x, axis=-1, keepdims=True)
    return x * lax.rsqrt(ms + NORM_EPS) * g


def _norm_matmul_body(x_ref, g_ref, w_ref, o_ref, xn_ref):
    @pl.when(pl.program_id(1) == 0)
    def _():
        xn_ref[...] = _rmsnorm(x_ref[...], g_ref[...]).astype(BF16)

    o_ref[...] = _mm(xn_ref[...], w_ref[...])


def norm_matmul(x, g, w, *, tm, tn):
    n, d = x.shape
    m = w.shape[1]
    return pl.pallas_call(
        _norm_matmul_body,
        grid=(n // tm, m // tn),
        in_specs=[pl.BlockSpec((tm, d), lambda i, j: (i, 0)),
                  pl.BlockSpec((1, d), lambda i, j: (0, 0)),
                  pl.BlockSpec((d, tn), lambda i, j: (0, j))],
        out_specs=pl.BlockSpec((tm, tn), lambda i, j: (i, j)),
        out_shape=jax.ShapeDtypeStruct((n, m), F32),
        scratch_shapes=[pltpu.VMEM((tm, d), BF16)],
        compiler_params=_cparams("parallel", "arbitrary"),
        name="norm_matmul",
    )(x, g.reshape(1, d), w)


def _norm_matmul_split_body(x_ref, g_ref, w_ref, *rest, bounds):
    o_refs, xn_ref = rest[:-1], rest[-1]
    j = pl.program_id(1)

    @pl.when(j == 0)
    def _():
        xn_ref[...] = _rmsnorm(x_ref[...], g_ref[...]).astype(BF16)

    res = _mm_nt(xn_ref[...], w_ref[...])
    tm, tn = res.shape
    for o_ref, (lo, hi) in zip(o_refs, bounds):
        @pl.when((j >= lo) & (j < hi))
        def _(o_ref=o_ref, lo=lo, hi=hi):
            if o_ref.shape[1] == HEAD_DIM:
                n_heads = (hi - lo) * tn // HEAD_DIM
                per_tile = tn // HEAD_DIM
                for c in range(hi - lo):
                    @pl.when(j == lo + c)
                    def _(c=c):
                        for hh in range(per_tile):
                            o_ref[pl.ds(c * per_tile + hh, tm, stride=n_heads), :] = (
                                res[:, hh * HEAD_DIM:(hh + 1) * HEAD_DIM].astype(o_ref.dtype))
            else:
                o_ref[...] = res.astype(o_ref.dtype)


def norm_matmul_split(x, g, wt, segments, *, tm, tn):
    n, d = x.shape
    m = wt.shape[0]
    bounds, out_specs, out_shape, lo = [], [], [], 0
    for nt, outs in segments:
        hi = lo + nt
        for per_head, dt in outs:
            bounds.append((lo, hi))
            if per_head:
                heads = nt * tn // HEAD_DIM
                out_specs.append(pl.BlockSpec((tm * heads, HEAD_DIM), lambda i, j: (i, 0)))
                out_shape.append(jax.ShapeDtypeStruct((n * heads, HEAD_DIM), dt))
            else:
                out_specs.append(pl.BlockSpec((tm, tn),
                                              lambda i, j, lo=lo, hi=hi: (i, jnp.clip(j - lo, 0, hi - lo - 1))))
                out_shape.append(jax.ShapeDtypeStruct((n, nt * tn), dt))
        lo = hi
    assert lo * tn == m
    return pl.pallas_call(
        functools.partial(_norm_matmul_split_body, bounds=tuple(bounds)),
        grid=(n // tm, m // tn),
        in_specs=[pl.BlockSpec((tm, d), lambda i, j: (i, 0)),
                  pl.BlockSpec((1, d), lambda i, j: (0, 0)),
                  pl.BlockSpec((tn, d), lambda i, j: (j, 0))],
        out_specs=out_specs,
        out_shape=out_shape,
        scratch_shapes=[pltpu.VMEM((tm, d), BF16)],
        compiler_params=_cparams("arbitrary", "arbitrary"),
        name="norm_matmul_split",
    )(x, g.reshape(1, d), wt)


def _pair_ones():
    r = lax.broadcasted_iota(I32, (PAIR, PAIR), 0) // HEAD_DIM
    c = lax.broadcasted_iota(I32, (PAIR, PAIR), 1) // HEAD_DIM
    return (r == c).astype(BF16)


def _head_sum(x, ones_bd):
    return _mm_exact_rhs(x, ones_bd)


def _rwkv_prep_body(f_ref, prev8_ref, init_ref, mu_ref, ft_ref, tprev8_ref, tinit_ref, tmu_ref,
                    w0_ref, w2_ref, a0_ref, a2_ref, g2_ref, kk_ref, ka_ref, rk_ref,
                    r_o, k_o, v_o, kk_o, b_o, ld_o, g_o, bon_o, *, tm, width):
    i = pl.program_id(1)

    def token_shift(f, p8_ref, i_ref, m_ref):
        nc = m_ref.shape[1]
        prev_row = jnp.where(i == 0, i_ref[0], p8_ref[0, SUBLANES - 1:SUBLANES, :nc])
        rolled = pltpu.roll(f, shift=1, axis=0)
        row = lax.broadcasted_iota(I32, f.shape, 0)
        f_prev = jnp.where(row == 0, prev_row, rolled)
        return f + (f_prev - f) * m_ref[...]

    fs = token_shift(f_ref[0], prev8_ref, init_ref, mu_ref)
    n_dec = w2_ref.shape[0]
    n_icl = a2_ref.shape[0]
    n_lora = tmu_ref.shape[1]
    ts = token_shift(ft_ref[0][:, :n_lora], tprev8_ref, tinit_ref, tmu_ref)
    w_ = width
    r = fs[:, 0:w_]
    k = fs[:, w_:2 * w_]
    v = fs[:, 2 * w_:3 * w_]
    wd = ts[:, 0:n_dec]
    ad = ts[:, n_dec:n_dec + n_icl]
    gd = ts[:, n_dec + n_icl:]
    z = w0_ref[...] + _mm3(jnp.tanh(wd), w2_ref[...])
    nz = -z
    softplus = jnp.maximum(nz, 0.0) + jnp.log(1.0 + jnp.exp(-jnp.abs(nz)))
    w = -softplus - 0.5
    ld = -jnp.exp(w)
    a = 1.0 / (1.0 + jnp.exp(-(a0_ref[...] + _mm3(ad, a2_ref[...]))))
    g = _mm3(1.0 / (1.0 + jnp.exp(-gd)), g2_ref[...])
    kk = k * kk_ref[...]
    k2 = k * (1.0 + (a - 1.0) * ka_ref[...])
    rk = r * k2 * rk_ref[...]
    ones_bd = _pair_ones()
    for p in range(w_ // PAIR):
        sl = slice(p * PAIR, (p + 1) * PAIR)
        kkp = kk[:, sl]
        nrm = jnp.sqrt(_head_sum(kkp * kkp, ones_bd))
        kkp = kkp / jnp.maximum(nrm, 1e-12)
        ap = a[:, sl]
        r_o[0, p] = r[:, sl]
        k_o[0, p] = k2[:, sl]
        v_o[0, p] = v[:, sl]
        kk_o[0, p] = kkp
        b_o[0, p] = kkp * ap
        ld_o[0, p] = ld[:, sl]
        g_o[0, p] = g[:, sl]
        bon_o[0, p] = _head_sum(rk[:, sl], ones_bd) * v[:, sl]


def rwkv_prep(f_main, f_tail, init_prev, mu, w0, w2, a0, a2, g2, k_k, k_a, r_k, *, tm, width):
    b, t, n_main = f_main.shape
    n_tail = f_tail.shape[2]
    n_lora = mu.shape[0] - n_main
    npair = width // PAIR
    row1 = lambda x: x.reshape(1, -1)
    kern = functools.partial(_rwkv_prep_body, tm=tm, width=width)
    full = lambda a: pl.BlockSpec(a.shape, lambda bi, i: (0,) * a.ndim)
    args = [row1(w0), w2, row1(a0), a2, g2, row1(k_k), row1(k_a), row1(r_k)]
    out_spec = pl.BlockSpec((1, npair, tm, PAIR), lambda bi, i: (bi, 0, i, 0))
    out_shape = jax.ShapeDtypeStruct((b, npair, t, PAIR), F32)
    prev8_map = lambda bi, i: (bi, jnp.maximum(i * (tm // SUBLANES) - 1, 0), 0)

    def feature_specs(ncols, n_init):
        return [pl.BlockSpec((1, tm, ncols), lambda bi, i: (bi, i, 0)),
                pl.BlockSpec((1, SUBLANES, ncols), prev8_map),
                pl.BlockSpec((1, 1, n_init), lambda bi, i: (bi, 0, 0)),
                pl.BlockSpec((1, n_init), lambda bi, i: (0, 0))]

    return pl.pallas_call(
        kern,
        grid=(b, t // tm),
        in_specs=feature_specs(n_main, n_main) + feature_specs(n_tail, n_lora) + [full(a) for a in args],
        out_specs=[out_spec] * 8,
        out_shape=[out_shape] * 8,
        compiler_params=_cparams("parallel", "parallel"),
        name="rwkv_prep",
    )(f_main, f_main, init_prev[:, :n_main].reshape(b, 1, n_main), row1(mu[:n_main]),
      f_tail, f_tail, init_prev[:, n_main:].reshape(b, 1, n_lora), row1(mu[n_main:]), *args)


def _rwkv_chunk_body(r_ref, k_ref, v_ref, kk_ref, b_ref, ld_ref, g_ref, bon_ref, s0_ref,
                     lng_ref, lnb_ref, o_ref, st_ref, s_ref, *, L, npair, group):
    c = pl.program_id(1)

    @pl.when(c == 0)
    def _():
        s_ref[...] = s0_ref[0]

    L2 = 2 * L
    row = lax.broadcasted_iota(I32, (L2, L2), 0)
    col = lax.broadcasted_iota(I32, (L2, L2), 1)
    same = (row // L) == (col // L)
    tri_strict = same & (col < row)
    tri_incl = same & (col <= row)
    eye = (row == col).astype(F32)
    tr = lax.broadcasted_iota(I32, (L, L), 0)
    tc = lax.broadcasted_iota(I32, (L, L), 1)
    cum_mat = (tc <= tr).astype(BF16)
    lane = lax.broadcasted_iota(I32, (L, PAIR), 1)
    first = lane < HEAD_DIM
    ones_bd = _pair_ones()
    n_sq = max(int(math.ceil(math.log2(L))) - 1, 0)

    def block_diag(x):
        return jnp.concatenate([jnp.where(first, x, 0.0), jnp.where(first, 0.0, x)], axis=0)

    def group_step(gi, carry):
        ps = [gi * group + j for j in range(group)]
        each = lambda f, *cols: [f(*args) for args in zip(*cols)]
        ld = [ld_ref[0, p] for p in ps]
        cum = each(lambda x: _mm_exact_rhs_t(cum_mat, x), ld)
        dec = each(jnp.exp, cum)
        dec_inv = each(lambda c_: jnp.exp(-c_), cum)
        a_t = each(lambda p, c_, l_: block_diag(-kk_ref[0, p] * jnp.exp(c_ - l_)), ps, cum, ld)
        b_t = each(lambda p, e: block_diag(b_ref[0, p] * e), ps, dec_inv)
        k_t = each(lambda p, e: block_diag(k_ref[0, p] * e), ps, dec_inv)
        r_t = each(lambda p, e: block_diag(r_ref[0, p] * e), ps, dec)
        v_b = each(lambda p: block_diag(v_ref[0, p]), ps)
        a_ab = each(lambda a, b: jnp.where(tri_strict, _mm_nt(a, b), 0.0), a_t, b_t)
        a_ak = each(lambda a, k: jnp.where(tri_strict, _mm_nt(a, k), 0.0), a_t, k_t)
        a_rb = each(lambda r, b: jnp.where(tri_incl, _mm_nt(r, b), 0.0), r_t, b_t)
        a_rk = each(lambda r, k: jnp.where(tri_incl, _mm_nt(r, k), 0.0), r_t, k_t)
        x = a_ab
        t_inv = each(lambda a: eye + a, a_ab)
        for _ in range(n_sq):
            x = each(lambda x_: _mm(x_, x_), x)
            t_inv = each(lambda t, x_: t + _mm(x_, t), t_inv, x)
        s = [s_ref[p] for p in ps]
        rhs = each(lambda a, s_, ak, v: _mm_nt(a, s_) + _mm(ak, v), a_t, s, a_ak, v_b)
        u = each(_mm, t_inv, rhs)
        y_b = each(lambda r, s_, rb, u_, rk, v: _mm_nt(r, s_) + _mm(rb, u_) + _mm(rk, v),
                   r_t, s, a_rb, u, a_rk, v_b)
        s_new = each(lambda s_, u_, b, v, k, d: (s_ + _mm(u_.T, b) + _mm(v.T, k)) * d[L - 1:L, :],
                     s, u, b_t, v_b, k_t, dec)
        for p, sn in zip(ps, s_new):
            s_ref[p] = sn
        y = each(lambda yb: yb[:L] + yb[L:], y_b)
        mean = each(lambda y_: _head_sum(y_, ones_bd) * (1.0 / HEAD_DIM), y)
        d = each(lambda y_, m: y_ - m, y, mean)
        var = each(lambda d_: _head_sum(d_ * d_, ones_bd) * (1.0 / HEAD_DIM), d)
        for p, d_, v_ in zip(ps, d, var):
            yn = d_ * lax.rsqrt(v_ + GN_EPS) * lng_ref[p] + lnb_ref[p]
            o_ref[0, p] = (yn + bon_ref[0, p]) * g_ref[0, p]
        return carry

    lax.fori_loop(0, npair // group, group_step, 0)

    @pl.when(c == pl.num_programs(1) - 1)
    def _():
        st_ref[0] = s_ref[...]


def _mm_exact_rhs_t(m_bf16, x):
    hi = x.astype(BF16)
    r1 = x - hi.astype(F32)
    mid = r1.astype(BF16)
    lo = (r1 - mid.astype(F32)).astype(BF16)
    d = lambda y: jnp.dot(m_bf16, y, preferred_element_type=F32)
    return d(hi) + (d(mid) + d(lo))


def rwkv_chunk(feats, s0_bd, ln_g, ln_b, *, L, group=8):
    b, npair, t, _ = feats[0].shape
    blk = pl.BlockSpec((1, npair, L, PAIR), lambda bi, c: (bi, 0, c, 0))
    st_spec = pl.BlockSpec((1, npair, PAIR, PAIR), lambda bi, c: (bi, 0, 0, 0))
    par_spec = pl.BlockSpec((npair, 1, PAIR), lambda bi, c: (0, 0, 0))
    kern = functools.partial(_rwkv_chunk_body, L=L, npair=npair, group=group)
    return pl.pallas_call(
        kern,
        grid=(b, t // L),
        in_specs=[blk] * 8 + [st_spec, par_spec, par_spec],
        out_specs=[blk, st_spec],
        out_shape=[jax.ShapeDtypeStruct((b, npair, t, PAIR), F32),
                   jax.ShapeDtypeStruct((b, npair, PAIR, PAIR), F32)],
        scratch_shapes=[pltpu.VMEM((npair, PAIR, PAIR), F32)],
        compiler_params=_cparams("parallel", "arbitrary"),
        name="rwkv_chunk",
    )(*feats, s0_bd, ln_g.reshape(npair, 1, PAIR), ln_b.reshape(npair, 1, PAIR))


def _state_to_block_diag(s):
    b, h, n, _ = s.shape
    s = s.reshape(b, h // 2, 2, n, n)
    z = jnp.zeros_like(s[:, :, 0])
    top = jnp.concatenate([s[:, :, 0], z], axis=-1)
    bot = jnp.concatenate([z, s[:, :, 1]], axis=-1)
    return jnp.concatenate([top, bot], axis=-2)


def _state_from_block_diag(s_bd):
    b, p, _, _ = s_bd.shape
    n = HEAD_DIM
    return jnp.stack([s_bd[:, :, :n, :n], s_bd[:, :, n:, n:]], axis=2).reshape(b, 2 * p, n, n)


def _t5_bucket(dist):
    exact = N_BUCKETS // 2
    d = jnp.maximum(dist, 0)
    far = exact + (jnp.log(jnp.maximum(d, 1).astype(F32) / exact) / math.log(MAX_DISTANCE / exact)
                   * (N_BUCKETS - exact)).astype(I32)
    return jnp.where(d < exact, d, jnp.minimum(far, N_BUCKETS - 1))


def _bias_tables_body(rb_ref, o_ref, *, offsets, n_heads):
    r = lax.broadcasted_iota(I32, (Q_BLOCK, Q_BLOCK), 0)
    c = lax.broadcasted_iota(I32, (Q_BLOCK, Q_BLOCK), 1)
    for t, (off, key_major) in enumerate(offsets):
        bucket = _t5_bucket((c - r if key_major else r - c) + off)
        for h in range(n_heads):
            def body(bk, acc):
                return jnp.where(bucket == bk, rb_ref[bk, h], acc)
            tile = lax.fori_loop(0, N_BUCKETS, body, jnp.zeros((Q_BLOCK, Q_BLOCK), F32))
            o_ref[t, h] = tile - rb_ref[N_BUCKETS - 1, h]


def bias_tables(rel_bias, offsets):
    n_heads = rel_bias.shape[1]
    kern = functools.partial(_bias_tables_body, offsets=tuple(offsets), n_heads=n_heads)
    return pl.pallas_call(
        kern,
        in_specs=[pl.BlockSpec(memory_space=pltpu.SMEM)],
        out_specs=pl.BlockSpec(memory_space=pltpu.VMEM),
        out_shape=jax.ShapeDtypeStruct((len(offsets), n_heads, Q_BLOCK, Q_BLOCK), F32),
        name="bias_tables",
    )(rel_bias)


def _sortable_key(scores):
    bits = lax.bitcast_convert_type(scores + 0.0, I32)
    return jnp.where(bits < 0, bits ^ 0x7FFFFFFF, bits)


def _count(mask):
    return jnp.sum(mask.astype(F32), axis=-1, keepdims=True)


def _topk_select(key, topk, n_index_bits):
    rows, n = key.shape
    kf = float(topk)
    t0 = jnp.where(_count(key >= 0) >= kf, 0, INT_MIN).astype(I32)

    def value_bit(i, t):
        cand = t + lax.shift_left(jnp.int32(1), 30 - i)
        return jnp.where(_count(key >= cand) >= kf, cand, t)

    thr = lax.fori_loop(0, 31, value_bit, t0)
    above = key > thr
    ties = key == thr
    need = kf - _count(above)
    idx = lax.broadcasted_iota(I32, (rows, n), 1)

    def lowest_ties():
        def index_bit(i, m):
            cand = m + lax.shift_left(jnp.int32(1), n_index_bits - 1 - i)
            return jnp.where(_count(ties & (idx < cand)) <= need, cand, m)
        return lax.fori_loop(0, n_index_bits, index_bit, jnp.zeros((rows, 1), I32))

    surplus = jnp.max(_count(ties) - need) > 0.0
    m = lax.cond(surplus, lowest_ties, lambda: jnp.full((rows, 1), 2 ** n_index_bits, I32))
    return above | (ties & (idx < m))


SUM_CHAINS = 4
HEAD_GROUP = 4


def _sum_rows(x):
    r = x.shape[0]
    if r % (SUM_CHAINS * SUBLANES) == 0 and r > SUM_CHAINS * SUBLANES:
        x = jnp.sum(x.reshape(SUM_CHAINS, r // SUM_CHAINS, x.shape[1]), axis=1)
    return jnp.sum(x, axis=0, keepdims=True)


def _max_rows(x):
    r = x.shape[0]
    if r % (SUM_CHAINS * SUBLANES) == 0 and r > SUM_CHAINS * SUBLANES:
        x = jnp.max(x.reshape(SUM_CHAINS, r // SUM_CHAINS, x.shape[1]), axis=1)
    return jnp.max(x, axis=0, keepdims=True)


def _topk_select_cols(key, topk, n_index_bits):
    n, cols = key.shape
    kf = float(topk)
    cnt = lambda m: _sum_rows(m.astype(F32))
    t0 = jnp.where(cnt(key >= 0) >= kf, 0, INT_MIN).astype(I32)

    def value_bit(i, t):
        cand = t + lax.shift_left(jnp.int32(1), 30 - i)
        return jnp.where(cnt(key >= cand) >= kf, cand, t)

    thr = lax.fori_loop(0, 31, value_bit, t0)
    above = key > thr
    ties = key == thr
    need = kf - cnt(above)
    idx = lax.broadcasted_iota(I32, (n, cols), 0)

    def lowest_ties():
        def index_bit(i, m):
            cand = m + lax.shift_left(jnp.int32(1), n_index_bits - 1 - i)
            return jnp.where(cnt(ties & (idx < cand)) <= need, cand, m)
        return lax.fori_loop(0, n_index_bits, index_bit, jnp.zeros((1, cols), I32))

    surplus = jnp.max(cnt(ties) - need) > 0.0
    m = lax.cond(surplus, lowest_ties, lambda: jnp.full((1, cols), 2 ** n_index_bits, I32))
    return above | (ties & (idx < m))


def _dsa_prompt_block(nb, ik_ref, iqt_ref, wt_ref, k_ref, qt_ref, vt_ref, bias_ref, o_ref, mask_ref,
                      *, n_heads, n_idx_heads, topk):
    w = nb * Q_BLOCK
    ik = ik_ref[0, :w, :]
    idx_dim = ik.shape[1]

    def head_rows(ref, h, dh, cols=slice(None)):
        return ref[0, pl.ds(pl.multiple_of(h * dh, dh), dh), cols]

    def idx_heads(gi, acc):
        hs = [gi * HEAD_GROUP + j for j in range(HEAD_GROUP)]
        dots = [jnp.dot(ik, head_rows(iqt_ref, h, idx_dim), preferred_element_type=F32) for h in hs]
        terms = [jnp.maximum(d, 0.0) * wt_ref[0, h] for d, h in zip(dots, hs)]
        while len(terms) > 1:
            terms = [a + b for a, b in zip(terms[0::2], terms[1::2])]
        return acc + terms[0]

    scores = lax.fori_loop(0, n_idx_heads // HEAD_GROUP, idx_heads, jnp.zeros((w, Q_BLOCK), F32))
    kpos = lax.broadcasted_iota(I32, (w, Q_BLOCK), 0)
    qpos = (nb - 1) * Q_BLOCK + lax.broadcasted_iota(I32, (w, Q_BLOCK), 1)
    valid = kpos <= qpos
    if w <= topk:
        sel = valid
    else:
        key = jnp.where(valid, _sortable_key(scores), INT_MIN)
        sel = valid & _topk_select_cols(key, topk, int(math.ceil(math.log2(w))) + 1)
    mask_ref[:w, :] = jnp.where(sel, 0.0, -jnp.inf)

    def with_near_bias(logits, h):
        near = [logits[w - Q_BLOCK:] + bias_ref[0, h]]
        if nb >= 2:
            near = [logits[w - 2 * Q_BLOCK:w - Q_BLOCK] + bias_ref[1, h]] + near
        if nb >= 3:
            near = [logits[:w - 2 * Q_BLOCK]] + near
        return jnp.concatenate(near, axis=0) if len(near) > 1 else near[0]

    pair_row_half = lax.broadcasted_iota(I32, (PAIR, Q_BLOCK), 0) // HEAD_DIM

    def head_logits(gi, j):
        p = gi * (HEAD_GROUP // 2) + j // 2
        qt_pair = head_rows(qt_ref, p, PAIR)
        qt_head = jnp.where(pair_row_half == j % 2, qt_pair, jnp.zeros_like(qt_pair))
        return jnp.dot(k_ref[0, p, :w, :], qt_head, preferred_element_type=F32)

    def attn_heads(gi, carry):
        hs = [gi * HEAD_GROUP + j for j in range(HEAD_GROUP)]
        mask = mask_ref[:w, :]
        logits = [head_logits(gi, j) + mask for j in range(HEAD_GROUP)]
        logits = [with_near_bias(l_, h) for l_, h in zip(logits, hs)]
        mx = [_max_rows(l_) for l_ in logits]
        e = [jnp.exp(l_ - m_) for l_, m_ in zip(logits, mx)]
        den = [_sum_rows(e_) for e_ in e]
        o = [jnp.dot(head_rows(vt_ref, h, HEAD_DIM, slice(0, w)), e_.astype(BF16),
                     preferred_element_type=F32) for h, e_ in zip(hs, e)]
        for h, o_, d_ in zip(hs, o, den):
            o_ref[0, pl.ds(pl.multiple_of(h * HEAD_DIM, HEAD_DIM), HEAD_DIM), :] = (o_ / d_).astype(o_ref.dtype)
        return carry

    lax.fori_loop(0, n_heads // HEAD_GROUP, attn_heads, 0)


def _dsa_prompt_body(ik_ref, iqt_ref, wt_ref, k_ref, qt_ref, vt_ref, bias_ref, o_ref, mask_ref,
                     *, n_heads, n_idx_heads, seq, topk):
    i = pl.program_id(0)
    for nb in range(1, seq // Q_BLOCK + 1):
        @pl.when(i == nb - 1)
        def _(nb=nb):
            _dsa_prompt_block(nb, ik_ref, iqt_ref, wt_ref, k_ref, qt_ref, vt_ref, bias_ref, o_ref, mask_ref,
                              n_heads=n_heads, n_idx_heads=n_idx_heads, topk=topk)


def dsa_prompt(ik, iqt, wts, k_pairs, qt, vt, bias_tiles, *, topk):
    b, width, s = qt.shape
    h = width // HEAD_DIM
    ih = wts.shape[1]
    kern = functools.partial(_dsa_prompt_body, n_heads=h, n_idx_heads=ih, seq=s, topk=topk)
    grid_spec = pltpu.PrefetchScalarGridSpec(
        num_scalar_prefetch=0,
        grid=(s // Q_BLOCK, b),
        in_specs=[pl.BlockSpec((1, s, ik.shape[2]), lambda i, bi: (bi, 0, 0)),
                  pl.BlockSpec((1, iqt.shape[1], Q_BLOCK), lambda i, bi: (bi, 0, i)),
                  pl.BlockSpec((1, ih, 1, Q_BLOCK), lambda i, bi: (bi, 0, 0, i)),
                  pl.BlockSpec((1, h // 2, s, PAIR), lambda i, bi: (bi, 0, 0, 0)),
                  pl.BlockSpec((1, width, Q_BLOCK), lambda i, bi: (bi, 0, i)),
                  pl.BlockSpec((1, width, s), lambda i, bi: (bi, 0, 0)),
                  pl.BlockSpec(bias_tiles.shape, lambda i, bi: (0, 0, 0, 0))],
        out_specs=pl.BlockSpec((1, width, Q_BLOCK), lambda i, bi: (bi, 0, i)),
        scratch_shapes=[pltpu.VMEM((s, Q_BLOCK), F32)],
    )
    return pl.pallas_call(
        kern,
        grid_spec=grid_spec,
        out_shape=jax.ShapeDtypeStruct((b, width, s), BF16),
        compiler_params=_cparams("arbitrary", "arbitrary"),
        name="dsa_prompt",
    )(ik, iqt, wts, k_pairs, qt, vt, bias_tiles)


def _dsa_select_body(pt_ref, iq_ref, wt_ref, iknew_ref, cidx_hbm, o_ref, ikbuf, sem,
                     *, layer, n_pages, group, n_idx_heads, t_new, topk):
    s = pl.program_id(0)
    past = n_pages * PAGE_SIZE
    n_keys = past + PAGE_SIZE

    def ik_copy(i):
        g = i // n_pages
        p = i % n_pages
        page = pt_ref[(s * group + g) * n_pages + p]
        return pltpu.make_async_copy(cidx_hbm.at[layer, page], ikbuf.at[g, p], sem)

    def ik_start(i, carry):
        ik_copy(i).start()
        return carry

    def ik_wait(i, carry):
        ik_copy(i).wait()
        return carry

    lax.fori_loop(0, group * n_pages, ik_start, 0)
    for g in range(group):
        ikbuf[g, n_pages] = iknew_ref[g]
    lax.fori_loop(0, group * n_pages, ik_wait, 0)

    scores = []
    for g in range(group):
        ikt_all = jnp.concatenate([ikbuf[g, p] for p in range(n_pages + 1)], axis=1)
        dots = _mm(iq_ref[g], ikt_all)
        weighted = jnp.maximum(dots, 0.0) * wt_ref[g]
        scores.append(jnp.sum(weighted.reshape(n_idx_heads, t_new, n_keys), axis=0))
    scores = jnp.concatenate(scores, axis=0)
    shape = (group * t_new, n_keys)
    qpos = past + lax.broadcasted_iota(I32, shape, 0) % t_new
    kpos = lax.broadcasted_iota(I32, shape, 1)
    valid = kpos <= qpos
    key = jnp.where(valid, _sortable_key(scores), INT_MIN)
    sel = valid & _topk_select(key, topk, int(math.log2(n_keys)) + 1)
    o_ref[...] = jnp.where(sel, 0.0, -jnp.inf).reshape(group, t_new, n_keys)


def dsa_sample_select(page_table, iq_rows, wt_rows, ik_new_t, cache_idx_kt, *, layer, n_idx_heads, t_new, topk,
                      group):
    db, n_pages = page_table.shape
    idx_dim = cache_idx_kt.shape[2]
    n_keys = (n_pages + 1) * PAGE_SIZE
    kern = functools.partial(_dsa_select_body, layer=layer, n_pages=n_pages, group=group,
                             n_idx_heads=n_idx_heads, t_new=t_new, topk=topk)
    per_g = lambda shape: pl.BlockSpec((group,) + shape, lambda si, pt: (si,) + (0,) * len(shape))
    grid_spec = pltpu.PrefetchScalarGridSpec(
        num_scalar_prefetch=1,
        grid=(db // group,),
        in_specs=[per_g((n_idx_heads * t_new, idx_dim)), per_g((n_idx_heads * t_new, 1)),
                  per_g((idx_dim, PAGE_SIZE)), pl.BlockSpec(memory_space=pl.ANY)],
        out_specs=per_g((t_new, n_keys)),
        scratch_shapes=[pltpu.VMEM((group, n_pages + 1, idx_dim, PAGE_SIZE), F32), pltpu.SemaphoreType.DMA(())],
    )
    return pl.pallas_call(
        kern,
        grid_spec=grid_spec,
        out_shape=jax.ShapeDtypeStruct((db, t_new, n_keys), F32),
        compiler_params=_cparams("arbitrary"),
        name="dsa_sample_select",
    )(page_table.reshape(-1), iq_rows, wt_rows, ik_new_t, cache_idx_kt)


def _dsa_sample_body(pt_ref, mask_ref, qbd_ref, knew_ref, vnew_ref,
                     blast_ref, bnew_ref, ck_hbm, cv_hbm, o_ref,
                     kbuf, vbuf, sem_k, sem_v,
                     *, layer, n_pages, chunk, n_heads, t_new):
    b = pl.program_id(0)
    n_seq = pl.num_programs(0)
    past = n_pages * PAGE_SIZE
    n_chunks = n_pages // chunk
    rows = n_heads * t_new
    ck = chunk * PAGE_SIZE

    def kv_copies(seq, c, j):
        slot = c % 2
        page = pt_ref[seq * n_pages + c * chunk + j]
        return (pltpu.make_async_copy(ck_hbm.at[layer, page], kbuf.at[slot, j], sem_k.at[slot]),
                pltpu.make_async_copy(cv_hbm.at[layer, page], vbuf.at[slot, j], sem_v.at[slot]))

    def start_chunk(seq, c):
        for j in range(chunk):
            kc, vc = kv_copies(seq, c, j)
            kc.start()
            vc.start()

    def wait_chunk(c):
        for j in range(chunk):
            kc, vc = kv_copies(b, c, j)
            kc.wait()
            vc.wait()

    @pl.when(b == 0)
    def _():
        start_chunk(0, 0)

    sel_rows = jnp.tile(mask_ref[0], (n_heads, 1))

    q_rep = jnp.tile(qbd_ref[0], (n_heads, 1))
    row_head = lax.broadcasted_iota(I32, q_rep.shape, 0) // t_new
    col_head = lax.broadcasted_iota(I32, q_rep.shape, 1) // HEAD_DIM
    qbd = jnp.where(row_head == col_head, q_rep, jnp.zeros_like(q_rep))
    neg = -1e30

    def update(state, logits, maskc, vt_bf16):
        m, l, acc = state
        s = logits + maskc
        m_new = jnp.maximum(m, jnp.max(s, axis=-1, keepdims=True))
        alpha = jnp.exp(m - m_new)
        p = jnp.exp(s - m_new)
        l = alpha * l + jnp.sum(p, axis=-1, keepdims=True)
        acc = alpha * acc + _mm_nt(p, vt_bf16)
        return m_new, l, acc

    def pages_t(buf, slot):
        return jnp.concatenate([buf[slot, j] for j in range(chunk)], axis=1).astype(BF16)

    state = (jnp.full((rows, 1), neg, F32), jnp.zeros((rows, 1), F32),
             jnp.zeros((rows, qbd.shape[1]), F32))
    for c in range(n_chunks):
        if c + 1 < n_chunks:
            start_chunk(b, c + 1)
        else:
            @pl.when(b + 1 < n_seq)
            def _():
                start_chunk(b + 1, 0)
        wait_chunk(c)
        slot = c % 2
        logits = _mm(qbd, pages_t(kbuf, slot))
        if c == n_chunks - 1:
            logits = jnp.concatenate([logits[:, :ck - PAGE_SIZE],
                                      logits[:, ck - PAGE_SIZE:] + blast_ref[...]], axis=1)
        state = update(state, logits, sel_rows[:, c * ck:(c + 1) * ck], pages_t(vbuf, slot))
    logits = _mm(qbd, knew_ref[0]) + bnew_ref[...]
    m, l, acc = update(state, logits, sel_rows[:, past:], vnew_ref[0])
    out = jnp.where(row_head == col_head, acc / l, 0.0)
    o_ref[0] = jnp.sum(out.reshape(n_heads, t_new, out.shape[1]), axis=0)


def dsa_sample(page_table, mask, q_bd, k_new_t, v_new_t, bias_last, bias_new, cache_kt, cache_vt,
               *, layer, n_heads, t_new, chunk):
    db, n_pages = page_table.shape
    assert (n_pages // chunk) % 2 == 0, "chunks alternate between two buffers across sequences"
    rows = n_heads * t_new
    width = q_bd.shape[2]
    kern = functools.partial(_dsa_sample_body, layer=layer, n_pages=n_pages, chunk=chunk, n_heads=n_heads,
                             t_new=t_new)
    per_b = lambda shape: pl.BlockSpec((1,) + shape, lambda bi, pt: (bi,) + (0,) * len(shape))
    const = lambda shape: pl.BlockSpec(shape, lambda bi, pt: (0,) * len(shape))
    any_spec = pl.BlockSpec(memory_space=pl.ANY)
    grid_spec = pltpu.PrefetchScalarGridSpec(
        num_scalar_prefetch=1,
        grid=(db,),
        in_specs=[per_b((t_new, mask.shape[2])),
                  per_b((t_new, width)), per_b((width, PAGE_SIZE)), per_b((width, PAGE_SIZE)),
                  const((rows, PAGE_SIZE)), const((rows, PAGE_SIZE)),
                  any_spec, any_spec],
        out_specs=per_b((t_new, width)),
        scratch_shapes=[pltpu.VMEM((2, chunk, width, PAGE_SIZE), F32),
                        pltpu.VMEM((2, chunk, width, PAGE_SIZE), F32),
                        pltpu.SemaphoreType.DMA((2,)),
                        pltpu.SemaphoreType.DMA((2,))],
    )
    return pl.pallas_call(
        kern,
        grid_spec=grid_spec,
        out_shape=jax.ShapeDtypeStruct((db, t_new, width), F32),
        compiler_params=_cparams("arbitrary"),
        name="dsa_sample",
    )(page_table.reshape(-1), mask, q_bd, k_new_t, v_new_t, bias_last, bias_new, cache_kt, cache_vt)


def _token_minor_cache(cache):
    l, pool, page = cache.shape[:3]
    nd = cache.ndim
    return cache.transpose((0, 1) + tuple(range(3, nd)) + (2,)).reshape(l, pool, -1, page)


def _dsa_sample_inputs(q, k_new, v_new, iq, ik_new, iw, bias_tiles):
    db, t, h, dh = q.shape
    ih = iq.shape[2]
    iq_rows = iq.transpose(0, 2, 1, 3).reshape(db, ih * t, -1).astype(BF16)
    wt_rows = (iw * (ih ** -0.5 * iq.shape[3] ** -0.5)).transpose(0, 2, 1).reshape(db, ih * t, 1)
    q_bd = q.reshape(db, t, h * dh)
    page_t = lambda x: jnp.pad(x.reshape(db, t, -1).transpose(0, 2, 1), ((0, 0), (0, 0), (0, PAGE_SIZE - t)))
    bias_new = bias_tiles[0, :, :t, :].reshape(h * t, Q_BLOCK)
    bias_last = bias_tiles[1, :, :t, :].reshape(h * t, Q_BLOCK)
    return ((iq_rows, wt_rows, page_t(ik_new)),
            (q_bd.astype(BF16), page_t(k_new).astype(BF16), page_t(v_new).astype(BF16), bias_last, bias_new))


def _matmul_residual_body(x_ref, a_ref, b_ref, w_ref, o_ref):
    ka = a_ref.shape[1]
    o_ref[...] = x_ref[...] + (_mm(a_ref[...], w_ref[:ka, :]) + _mm(b_ref[...], w_ref[ka:, :]))


def matmul_residual(x, a, b, w, *, tm, tn):
    n, d = x.shape
    ka, kb = a.shape[1], b.shape[1]
    return pl.pallas_call(
        _matmul_residual_body,
        grid=(n // tm, d // tn),
        in_specs=[pl.BlockSpec((tm, tn), lambda i, j: (i, j)),
                  pl.BlockSpec((tm, ka), lambda i, j: (i, 0)),
                  pl.BlockSpec((tm, kb), lambda i, j: (i, 0)),
                  pl.BlockSpec((ka + kb, tn), lambda i, j: (0, j))],
        out_specs=pl.BlockSpec((tm, tn), lambda i, j: (i, j)),
        out_shape=jax.ShapeDtypeStruct((n, d), F32),
        compiler_params=_cparams("parallel", "parallel"),
        name="matmul_residual",
    )(x, a, b, w)


def _cross_attn_body(x_ref, g_ref, wq_ref, mk_ref, mv_ref, wo_ref, o_ref, *, groups, t_rows, n_heads, head_dim):
    x = x_ref[...]
    h = _rmsnorm(x, g_ref[...]).astype(BF16)
    q = jnp.dot(h, wq_ref[...], preferred_element_type=F32).astype(BF16)
    scale = head_dim ** -0.5
    outs = []
    for gi in range(groups):
        qg = q[gi * t_rows:(gi + 1) * t_rows]
        heads = []
        for hh in range(n_heads):
            sl = slice(hh * head_dim, (hh + 1) * head_dim)
            logits = _mm_nt(qg[:, sl], mk_ref[gi, :, sl]) * scale
            mx = jnp.max(logits, axis=-1, keepdims=True)
            e = jnp.exp(logits - mx)
            p = e / jnp.sum(e, axis=-1, keepdims=True)
            heads.append(jnp.dot(p.astype(BF16), mv_ref[gi, :, sl], preferred_element_type=F32))
        outs.append(jnp.concatenate(heads, axis=1))
    o = jnp.concatenate(outs, axis=0) if groups > 1 else outs[0]
    o_ref[...] = x + jnp.dot(o.astype(BF16), wo_ref[...], preferred_element_type=F32)


def cross_attn(x, g, wq, mk, mv, wo, *, groups, t_rows, seq_tiles, n_heads):
    n, d = x.shape
    xw = wq.shape[1]
    rows = groups * t_rows
    m = mk.shape[1]
    kern = functools.partial(_cross_attn_body, groups=groups, t_rows=t_rows, n_heads=n_heads,
                             head_dim=xw // n_heads)
    return pl.pallas_call(
        kern,
        grid=(n // rows,),
        in_specs=[pl.BlockSpec((rows, d), lambda i: (i, 0)),
                  pl.BlockSpec((1, d), lambda i: (0, 0)),
                  pl.BlockSpec((d, xw), lambda i: (0, 0)),
                  pl.BlockSpec((groups, m, xw), lambda i: (i // seq_tiles, 0, 0)),
                  pl.BlockSpec((groups, m, xw), lambda i: (i // seq_tiles, 0, 0)),
                  pl.BlockSpec((xw, d), lambda i: (0, 0))],
        out_specs=pl.BlockSpec((rows, d), lambda i: (i, 0)),
        out_shape=jax.ShapeDtypeStruct((n, d), F32),
        compiler_params=_cparams("parallel"),
        name="cross_attn",
    )(x, g.reshape(1, d), wq, mk, mv, wo)


def _router_body(x_ref, g_ref, wr_ref, br_ref, h_ref, r_ref, *, n_groups, per_group):
    h = _rmsnorm(x_ref[...], g_ref[...])
    _store_slab_rows(h_ref, h, h.shape[1] // LANES)
    logits = _mm3(h, wr_ref[...]) + br_ref[...]
    lane = lax.broadcasted_iota(I32, logits.shape, 1).astype(F32)
    big = 1e9
    first_lane = lambda hit: jnp.min(jnp.where(hit, lane, big), axis=-1, keepdims=True)
    gl = jnp.where(lane < n_groups, logits, -jnp.inf)
    gmax = jnp.max(gl, axis=-1, keepdims=True)
    grp = first_lane(gl == gmax)
    p_grp = 1.0 / jnp.sum(jnp.exp(gl - gmax), axis=-1, keepdims=True)
    e_id = lane - n_groups
    in_grp = (e_id >= grp * per_group) & (e_id < (grp + 1.0) * per_group)
    el = jnp.where(in_grp, logits, -jnp.inf)
    v1 = jnp.max(el, axis=-1, keepdims=True)
    i1 = first_lane(el == v1) - n_groups
    el2 = jnp.where(e_id == i1, -jnp.inf, el)
    v2 = jnp.max(el2, axis=-1, keepdims=True)
    i2 = first_lane(el2 == v2) - n_groups
    e2 = jnp.exp(v2 - v1)
    g1 = p_grp / (1.0 + e2)
    g2 = p_grp * e2 / (1.0 + e2)
    r_ref[...] = jnp.where(lane == 0, g1, jnp.where(lane == 1, g2, jnp.where(
        lane == 2, i1, jnp.where(lane == 3, i2, 0.0))))


def router(x, g, w_r, b_r, *, tm, n_groups, per_group):
    n, d = x.shape
    kern = functools.partial(_router_body, n_groups=n_groups, per_group=per_group)
    return pl.pallas_call(
        kern,
        grid=(n // tm,),
        in_specs=[pl.BlockSpec((tm, d), lambda i: (i, 0)),
                  pl.BlockSpec((1, d), lambda i: (0, 0)),
                  pl.BlockSpec((d, LANES), lambda i: (0, 0)),
                  pl.BlockSpec((1, LANES), lambda i: (0, 0))],
        out_specs=[pl.BlockSpec((tm * (d // LANES), LANES), lambda i: (i, 0)),
                   pl.BlockSpec((tm, LANES), lambda i: (i, 0))],
        out_shape=[jax.ShapeDtypeStruct((n * (d // LANES), LANES), F32), jax.ShapeDtypeStruct((n, LANES), F32)],
        compiler_params=_cparams("parallel"),
        name="moe_router",
    )(x, g.reshape(1, d), w_r, b_r)


def _slab_rows(buf, offset, n_rows, n_slab, stride):
    return jnp.concatenate([buf[pl.ds(offset + s, n_rows, stride=stride), :] for s in range(n_slab)], axis=1)


def _store_slab_rows(buf, x, n_slab):
    n_rows = x.shape[0]
    for s in range(n_slab):
        buf[pl.ds(s, n_rows, stride=n_slab), :] = x[:, s * LANES:(s + 1) * LANES]


def _moe_ffn_body(te_ref, nr_ref, tok_ref, dst_ref, h_hbm, w1_ref, w3_ref, w2_ref, y_hbm,
                  xbuf, ybuf, gsem, ssem, *, tm):
    t = pl.program_id(0)
    n_tiles = pl.num_programs(0)
    n_slab = h_hbm.shape[1]
    slot = t % 2

    def slab(buf, sl, r):
        return buf.at[sl, pl.ds(pl.multiple_of(r * n_slab, n_slab), n_slab)]

    def gather_copy(tile, sl, r):
        return pltpu.make_async_copy(h_hbm.at[tok_ref[tile * tm + r]], slab(xbuf, sl, r), gsem.at[sl])

    def scatter_copy(tile, sl, r):
        return pltpu.make_async_copy(slab(ybuf, sl, r), y_hbm.at[dst_ref[tile * tm + r]], ssem.at[sl])

    def start_rows(n, copy):
        def pair(i, c):
            copy(2 * i).start(priority=0)
            copy(2 * i + 1).start(priority=1)
            return c
        lax.fori_loop(0, n // 2, pair, 0)

        @pl.when(n % 2 == 1)
        def _():
            copy(n - 1).start(priority=0)

    def wait_rows(n, copy):
        def one(r, c):
            copy(r).wait()
            return c
        lax.fori_loop(0, n, one, 0)

    @pl.when(t == 0)
    def _():
        xbuf[...] = jnp.zeros(xbuf.shape, F32)
        start_rows(nr_ref[0], lambda r: gather_copy(0, 0, r))

    nxt = jnp.minimum(t + 1, n_tiles - 1)

    @pl.when(t + 1 < n_tiles)
    def _():
        start_rows(nr_ref[nxt], lambda r: gather_copy(nxt, 1 - slot, r))

    @pl.when(t >= 2)
    def _():
        prev2 = jnp.maximum(t - 2, 0)
        wait_rows(nr_ref[prev2], lambda r: scatter_copy(prev2, slot, r))

    n = nr_ref[t]

    @pl.when(n > 0)
    def _():
        wait_rows(n, lambda r: gather_copy(t, slot, r))
        x = _slab_rows(xbuf.at[slot], 0, tm, n_slab, n_slab).astype(BF16)
        a = jnp.dot(x, w1_ref[0].astype(BF16), preferred_element_type=F32)
        bgate = jnp.dot(x, w3_ref[0].astype(BF16), preferred_element_type=F32)
        u = (a / (1.0 + jnp.exp(-a))) * bgate
        y = jnp.dot(u.astype(BF16), w2_ref[0].astype(BF16), preferred_element_type=F32)
        _store_slab_rows(ybuf.at[slot], y, n_slab)
        start_rows(n, lambda r: scatter_copy(t, slot, r))

    @pl.when(t == n_tiles - 1)
    def _():
        @pl.when(t >= 1)
        def _():
            prev1 = jnp.maximum(t - 1, 0)
            wait_rows(nr_ref[prev1], lambda r: scatter_copy(prev1, 1 - slot, r))
        wait_rows(n, lambda r: scatter_copy(t, slot, r))


def moe_ffn(tile_expert, tile_rows, row_tok, row_dst, h_slabs, n_out_rows, w1, w3, w2, *, tm):
    n_tiles = tile_expert.shape[0]
    d, de = w1.shape[1], w1.shape[2]
    n_slab = d // LANES
    h3 = h_slabs.reshape(-1, n_slab, LANES)
    w_in_spec = pl.BlockSpec((1, d, de), lambda t, te, nr, tok, dst: (te[t], 0, 0))
    grid_spec = pltpu.PrefetchScalarGridSpec(
        num_scalar_prefetch=4,
        grid=(n_tiles,),
        in_specs=[pl.BlockSpec(memory_space=pl.ANY), w_in_spec, w_in_spec,
                  pl.BlockSpec((1, de, d), lambda t, te, nr, tok, dst: (te[t], 0, 0))],
        out_specs=pl.BlockSpec(memory_space=pl.ANY),
        scratch_shapes=[pltpu.VMEM((2, tm * n_slab, LANES), F32), pltpu.VMEM((2, tm * n_slab, LANES), F32),
                        pltpu.SemaphoreType.DMA((2,)), pltpu.SemaphoreType.DMA((2,))],
    )
    y = pl.pallas_call(
        functools.partial(_moe_ffn_body, tm=tm),
        grid_spec=grid_spec,
        out_shape=jax.ShapeDtypeStruct((n_out_rows, n_slab, LANES), F32),
        compiler_params=_cparams("arbitrary"),
        name="moe_ffn",
    )(tile_expert, tile_rows, row_tok, row_dst, h3, w1, w3, w2)
    return y.reshape(n_out_rows * n_slab, LANES)


def _moe_dispatch(eids, n_experts, tm):
    n, k = eids.shape
    m = n * k
    flat_e = eids.reshape(-1)
    order = jnp.argsort(flat_e, stable=True).astype(I32)
    counts = jnp.sum((flat_e[:, None] == jnp.arange(n_experts)[None, :]).astype(I32), axis=0)
    padded = (counts + tm - 1) // tm * tm
    pad_end = jnp.cumsum(padded)
    pad_start = pad_end - padded
    start = jnp.cumsum(counts) - counts
    n_tiles = -(-m // tm) + n_experts
    tile_start = jnp.arange(n_tiles) * tm
    last = jnp.maximum(pad_end[-1] - 1, 0)
    tile_expert = jnp.minimum(jnp.searchsorted(pad_end, jnp.minimum(tile_start, last), side='right'),
                              n_experts - 1).astype(I32)
    first = tile_start - pad_start[tile_expert]
    tile_rows = jnp.where(tile_start < pad_end[-1], jnp.clip(counts[tile_expert] - first, 0, tm), 0).astype(I32)
    src = jnp.clip(start[tile_expert][:, None] + first[:, None] + jnp.arange(tm)[None, :], 0, m - 1)
    row_dst = order[src].reshape(-1)
    return tile_expert, tile_rows, (row_dst // k).astype(I32), row_dst.astype(I32)


def _combine_body(x_ref, r_ref, g_ref, y_ref, o_ref, *, tm, top_k):
    n_slab = x_ref.shape[1] // LANES
    route = r_ref[...]
    x = x_ref[...]
    for kk in range(top_k):
        x = x + _slab_rows(y_ref, kk * n_slab, tm, n_slab, top_k * n_slab) * route[:, kk:kk + 1]
    o_ref[...] = _rmsnorm(x, g_ref[...])


def moe_combine(x, route, g, y_slabs, *, tm, top_k, tile_offset):
    n, d = x.shape
    rows = tm * top_k * (d // LANES)
    return pl.pallas_call(
        functools.partial(_combine_body, tm=tm, top_k=top_k),
        grid=(n // tm,),
        in_specs=[pl.BlockSpec((tm, d), lambda i: (i, 0)),
                  pl.BlockSpec((tm, LANES), lambda i: (i, 0)),
                  pl.BlockSpec((1, d), lambda i: (0, 0)),
                  pl.BlockSpec((rows, LANES), lambda i: (i + tile_offset, 0))],
        out_specs=pl.BlockSpec((tm, d), lambda i: (i, 0)),
        out_shape=jax.ShapeDtypeStruct((n, d), F32),
        compiler_params=_cparams("parallel"),
        name="moe_combine",
    )(x, route, g.reshape(1, d), y_slabs)


def kernel(x_prompt, x_sample, mem_prompt, cache_k, cache_v, cache_idx_k, page_table, state_wkv, state_shift, cache_mem_k, cache_mem_v, g_mix, w_in, mu_shift, rw_w0, rw_w2, rw_a0, rw_a2, rw_g2, rw_kk, rw_ka, rw_rk, rw_ln_g, rw_ln_b, w_out, g_cross, g_mem, w_cq, w_ck, w_cv, w_co, g_ffn, w_rg, b_rg, w_re, b_re, w_e1, w_e3, w_e2, rel_bias, g_final):
    B, S, D = x_prompt.shape
    DB, T, _ = x_sample.shape
    assert w_in.shape[0] == 1, "single-layer trunk only"
    l = 0
    n_pages = page_table.shape[1]
    past = n_pages * PAGE_SIZE
    topk_p = min(TOPK_MAX, S // 4)
    topk_s = min(TOPK_MAX, (past + T) // 4)
    rw_proj = mu_shift.shape[1]
    width = rw_w0.shape[1]
    at_w = D - width
    n_heads = at_w // HEAD_DIM
    idx_dim = cache_idx_k.shape[-1]
    ih = (w_in.shape[2] - rw_proj - 3 * at_w - idx_dim) // (idx_dim + 1)
    xw = w_cq.shape[2]
    x_heads = cache_mem_k.shape[3]
    n_mem = mem_prompt.shape[1]
    n_experts = w_e1.shape[1]
    top_k = 2
    tn = 512
    n_main = 3 * width
    n_lora = rw_proj - n_main
    n_att = 3 * at_w + ih * idx_dim
    n_tail = n_lora + idx_dim + ih
    assert n_main % tn == 0 and at_w % tn == 0 and (ih * idx_dim) % tn == 0 and n_tail <= tn

    w_t = w_in[l].T
    w_all = jnp.concatenate([w_t[:n_main], w_t[rw_proj:rw_proj + at_w] * HEAD_DIM ** -0.5,
                             w_t[rw_proj + at_w:rw_proj + n_att], w_t[n_main:rw_proj],
                             w_t[rw_proj + n_att:], jnp.zeros((tn - n_tail, D), w_t.dtype)], axis=0).astype(BF16)
    flat, per_head = False, True
    segments = [(n_main // tn, [(flat, F32)]),
                (at_w // tn, [(flat, BF16)]),
                (at_w // tn, [(per_head, F32), (flat, BF16)]),
                (at_w // tn, [(per_head, F32), (flat, BF16)]),
                (ih * idx_dim // tn, [(flat, BF16)]),
                (1, [(flat, F32)])]
    w_out_b = w_out[l].astype(BF16)
    w_cq_b, w_co_b = w_cq[l].astype(BF16), w_co[l].astype(BF16)
    w_ckv = jnp.concatenate([w_ck[l], w_cv[l]], axis=1).astype(BF16)
    n_route = w_rg.shape[2] + w_re.shape[2]
    w_r = jnp.pad(jnp.concatenate([w_rg[l], w_re[l]], axis=1), ((0, 0), (0, LANES - n_route)))
    b_r = jnp.pad(jnp.concatenate([b_rg[l], b_re[l]]), (0, LANES - n_route)).reshape(1, LANES)
    rw_args = (mu_shift[l], rw_w0[l], rw_w2[l], rw_a0[l], rw_a2[l], rw_g2[l], rw_kk[l], rw_ka[l],
               rw_rk[l].reshape(-1))
    tiles = bias_tables(rel_bias, ((0, False), (Q_BLOCK, False), (0, True), (Q_BLOCK, True)))

    def project(x2d, b_, t_, tm):
        f_main, q, k_heads, k, v_heads, v, iq, f_tail = norm_matmul_split(x2d, g_mix[l], w_all, segments,
                                                                          tm=tm, tn=tn)
        r3 = lambda z: z.reshape(b_, t_, -1)
        f_main, f_tail = r3(f_main), r3(f_tail)
        ik = f_tail[..., n_lora:n_lora + idx_dim]
        iw = f_tail[..., n_lora + idx_dim:n_tail]
        shift = jnp.concatenate([f_main[:, -1], f_tail[:, -1, :n_lora]], axis=-1)
        heads5 = lambda z: z.reshape(1, b_, t_, n_heads, HEAD_DIM)
        return f_main, r3(q), heads5(k_heads), r3(k), heads5(v_heads), r3(v), r3(iq), f_tail, ik, iw, shift

    def rw_rows(y):
        b_, p_, t_, _ = y.shape
        return y.transpose(0, 2, 1, 3).reshape(b_ * t_, p_ * PAIR).astype(BF16)

    xp = x_prompt.reshape(B * S, D)
    fm_p, q, k_p, kb_p, v_p, vb_p, iq, ft_p, ik_p, iw, shift_p = project(xp, B, S, 512)
    feats_p = rwkv_prep(fm_p, ft_p, jnp.zeros((B, rw_proj), F32), *rw_args, tm=256, width=width)
    rw_p, st_p = rwkv_chunk(feats_p, jnp.zeros((B, width // PAIR, PAIR, PAIR), F32), rw_ln_g[l], rw_ln_b[l], L=64)
    tr = lambda z: z.transpose(0, 2, 1)
    k_pairs = kb_p.reshape(B, S, n_heads // 2, PAIR).transpose(0, 2, 1, 3)
    at_p = dsa_prompt(ik_p.astype(BF16), tr(iq), tr(iw * (ih ** -0.5 * idx_dim ** -0.5))[:, :, None, :],
                      k_pairs, tr(q), tr(vb_p), tiles[2:4], topk=topk_p)
    x1_p = matmul_residual(xp, rw_rows(rw_p), tr(at_p).reshape(B * S, at_w), w_out_b, tm=512, tn=512)
    mkv = norm_matmul(mem_prompt.reshape(B * n_mem, D), g_mem[l], w_ckv, tm=256, tn=512)
    mk_p = mkv[:, :xw].reshape(B, n_mem, xw)
    mv_p = mkv[:, xw:].reshape(B, n_mem, xw)
    x2_p = cross_attn(x1_p, g_cross[l], w_cq_b, mk_p.astype(BF16), mv_p.astype(BF16), w_co_b,
                      groups=1, t_rows=512, seq_tiles=S // 512, n_heads=x_heads)

    xs = x_sample.reshape(DB * T, D)
    fm_s, q2, k_s, kb_s, v_s, vb_s, iq2, ft_s, ik_s, iw2, shift_s = project(xs, DB, T, DB * T)
    feats_s = rwkv_prep(fm_s, ft_s, state_shift[l], *rw_args, tm=T, width=width)
    rw_s, st_s = rwkv_chunk(feats_s, _state_to_block_diag(state_wkv[l]), rw_ln_g[l], rw_ln_b[l], L=T)
    r4 = lambda z, h_: z.reshape(DB, T, h_, -1)
    sel_args, att_args = _dsa_sample_inputs(r4(q2, n_heads), r4(kb_s, n_heads), r4(vb_s, n_heads), r4(iq2, ih),
                                            ik_s, iw2, tiles[0:2])
    mask_s = dsa_sample_select(page_table, *sel_args, _token_minor_cache(cache_idx_k), layer=l,
                               n_idx_heads=ih, t_new=T, topk=topk_s, group=4)
    at_s = dsa_sample(page_table, mask_s, *att_args, _token_minor_cache(cache_k), _token_minor_cache(cache_v),
                      layer=l, n_heads=n_heads, t_new=T, chunk=8)
    x1_s = matmul_residual(xs, rw_rows(rw_s), at_s.reshape(DB * T, at_w), w_out_b, tm=DB * T, tn=512)
    x2_s = cross_attn(x1_s, g_cross[l], w_cq_b, cache_mem_k[l].reshape(DB, n_mem, xw).astype(BF16),
                      cache_mem_v[l].reshape(DB, n_mem, xw).astype(BF16), w_co_b,
                      groups=8, t_rows=T, seq_tiles=1, n_heads=x_heads)

    h_p, route_p = router(x2_p, g_ffn[l], w_r, b_r, tm=512, n_groups=w_rg.shape[2],
                          per_group=w_re.shape[2] // w_rg.shape[2])
    h_s, route_s = router(x2_s, g_ffn[l], w_r, b_r, tm=DB * T, n_groups=w_rg.shape[2],
                          per_group=w_re.shape[2] // w_rg.shape[2])
    h_all = jnp.concatenate([h_p, h_s], axis=0)
    eids = jnp.concatenate([route_p[:, top_k:2 * top_k], route_s[:, top_k:2 * top_k]], axis=0).astype(I32)
    tm_moe = DB * T
    n_p, n_all = B * S, B * S + DB * T
    assert n_p % tm_moe == 0
    tile_expert, tile_rows, row_tok, row_dst = _moe_dispatch(eids, n_experts, tm_moe)
    y_slabs = moe_ffn(tile_expert, tile_rows, row_tok, row_dst, h_all, n_all * top_k,
                      w_e1[l], w_e3[l], w_e2[l], tm=tm_moe)
    y_p = moe_combine(x2_p, route_p, g_final, y_slabs, tm=tm_moe, top_k=top_k, tile_offset=0)
    y_s = moe_combine(x2_s, route_s, g_final, y_slabs, tm=tm_moe, top_k=top_k, tile_offset=n_p // tm_moe)

    return (y_p.reshape(B, S, D), y_s.reshape(DB, T, D),
            k_p, v_p, ik_p[None], _state_from_block_diag(st_p)[None], shift_p[None],
            mk_p.reshape(1, B, n_mem, x_heads, xw // x_heads), mv_p.reshape(1, B, n_mem, x_heads, xw // x_heads),
            k_s, v_s, ik_s[None], _state_from_block_diag(st_s)[None], shift_s[None])
```

```python
import functools
import math

import jax
import jax.numpy as jnp
from jax import lax
from jax.experimental import pallas as pl
from jax.experimental.pallas import tpu as pltpu

F32 = jnp.float32
BF16 = jnp.bfloat16
I32 = jnp.int32

LANES = 128
SUBLANES = 8
VMEM_LIMIT_BYTES = 56 * 1024 * 1024

HEAD_DIM = 64
PAIR = 2 * HEAD_DIM
GN_EPS = 64e-5
NORM_EPS = 1e-6
TOPK_MAX = 256
Q_BLOCK = 128
N_BUCKETS = 32
MAX_DISTANCE = 128
PAGE_SIZE = 128
N_GROUPS = 4
EXPERTS_PER_GROUP = 8
INT_MIN = -(2 ** 31)


def _cparams(*sem):
    return pltpu.CompilerParams(dimension_semantics=sem, vmem_limit_bytes=VMEM_LIMIT_BYTES)


def _mm(a, b):
    return jnp.dot(a.astype(BF16), b.astype(BF16), preferred_element_type=F32)


def _mm_nt(a, b):
    return lax.dot_general(a.astype(BF16), b.astype(BF16), (((1,), (1,)), ((), ())),
                           preferred_element_type=F32)


def _split2(x):
    hi = x.astype(BF16)
    lo = (x - hi.astype(F32)).astype(BF16)
    return hi, lo


def _mm3(a, b):
    ah, al = _split2(a)
    bh, bl = _split2(b)
    d = lambda x, y: jnp.dot(x, y, preferred_element_type=F32)
    return d(ah, bh) + (d(ah, bl) + d(al, bh))


def _mm_exact_rhs(a, b_bf16):
    hi = a.astype(BF16)
    r1 = a - hi.astype(F32)
    mid = r1.astype(BF16)
    lo = (r1 - mid.astype(F32)).astype(BF16)
    d = lambda x: jnp.dot(x, b_bf16, preferred_element_type=F32)
    return d(hi) + (d(mid) + d(lo))


def _rmsnorm(x, g):
    ms = jnp.mean(x * x, axis=-1, keepdims=True)
    return x * lax.rsqrt(ms + NORM_EPS) * g


def _norm_matmul_body(x_ref, g_ref, w_ref, o_ref, xn_ref):
    @pl.when(pl.program_id(1) == 0)
    def _():
        xn_ref[...] = _rmsnorm(x_ref[...], g_ref[...]).astype(BF16)

    o_ref[...] = _mm(xn_ref[...], w_ref[...])


def norm_matmul(x, g, w, *, tm, tn):
    n, d = x.shape
    m = w.shape[1]
    return pl.pallas_call(
        _norm_matmul_body,
        grid=(n // tm, m // tn),
        in_specs=[pl.BlockSpec((tm, d), lambda i, j: (i, 0)),
                  pl.BlockSpec((1, d), lambda i, j: (0, 0)),
                  pl.BlockSpec((d, tn), lambda i, j: (0, j))],
        out_specs=pl.BlockSpec((tm, tn), lambda i, j: (i, j)),
        out_shape=jax.ShapeDtypeStruct((n, m), F32),
        scratch_shapes=[pltpu.VMEM((tm, d), BF16)],
        compiler_params=_cparams("parallel", "arbitrary"),
        name="norm_matmul",
    )(x, g.reshape(1, d), w)


def _norm_matmul_split_body(x_ref, g_ref, w_ref, *rest, bounds):
    o_refs, xn_ref = rest[:-1], rest[-1]
    j = pl.program_id(1)

    @pl.when(j == 0)
    def _():
        xn_ref[...] = _rmsnorm(x_ref[...], g_ref[...]).astype(BF16)

    res = _mm_nt(xn_ref[...], w_ref[...])
    tm, tn = res.shape
    for o_ref, (lo, hi) in zip(o_refs, bounds):
        @pl.when((j >= lo) & (j < hi))
        def _(o_ref=o_ref, lo=lo, hi=hi):
            if o_ref.shape[1] == HEAD_DIM:
                n_heads = (hi - lo) * tn // HEAD_DIM
                per_tile = tn // HEAD_DIM
                for c in range(hi - lo):
                    @pl.when(j == lo + c)
                    def _(c=c):
                        for hh in range(per_tile):
                            o_ref[pl.ds(c * per_tile + hh, tm, stride=n_heads), :] = (
                                res[:, hh * HEAD_DIM:(hh + 1) * HEAD_DIM].astype(o_ref.dtype))
            else:
                o_ref[...] = res.astype(o_ref.dtype)


def norm_matmul_split(x, g, wt, segments, *, tm, tn):
    n, d = x.shape
    m = wt.shape[0]
    bounds, out_specs, out_shape, lo = [], [], [], 0
    for nt, outs in segments:
        hi = lo + nt
        for per_head, dt in outs:
            bounds.append((lo, hi))
            if per_head:
                heads = nt * tn // HEAD_DIM
                out_specs.append(pl.BlockSpec((tm * heads, HEAD_DIM), lambda i, j: (i, 0),
                                              pipeline_mode=pl.Buffered(1)))
                out_shape.append(jax.ShapeDtypeStruct((n * heads, HEAD_DIM), dt))
            else:
                out_specs.append(pl.BlockSpec((tm, tn),
                                              lambda i, j, lo=lo, hi=hi: (i, jnp.clip(j - lo, 0, hi - lo - 1))))
                out_shape.append(jax.ShapeDtypeStruct((n, nt * tn), dt))
        lo = hi
    assert lo * tn == m
    return pl.pallas_call(
        functools.partial(_norm_matmul_split_body, bounds=tuple(bounds)),
        grid=(n // tm, m // tn),
        in_specs=[pl.BlockSpec((tm, d), lambda i, j: (i, 0), pipeline_mode=pl.Buffered(1)),
                  pl.BlockSpec((1, d), lambda i, j: (0, 0)),
                  pl.BlockSpec((tn, d), lambda i, j: (j, 0))],
        out_specs=out_specs,
        out_shape=out_shape,
        scratch_shapes=[pltpu.VMEM((tm, d), BF16)],
        compiler_params=_cparams("arbitrary", "arbitrary"),
        name="norm_matmul_split",
    )(x, g.reshape(1, d), wt)


def _pair_ones():
    r = lax.broadcasted_iota(I32, (PAIR, PAIR), 0) // HEAD_DIM
    c = lax.broadcasted_iota(I32, (PAIR, PAIR), 1) // HEAD_DIM
    return (r == c).astype(BF16)


def _head_sum(x, ones_bd):
    return _mm_exact_rhs(x, ones_bd)


def _rwkv_prep_body(f_ref, prev8_ref, init_ref, mu_ref, ft_ref, tprev8_ref, tinit_ref, tmu_ref,
                    w0_ref, w2_ref, a0_ref, a2_ref, g2_ref, kk_ref, ka_ref, rk_ref,
                    r_o, k_o, v_o, kk_o, b_o, ld_o, g_o, bon_o, *, tm, width):
    i = pl.program_id(1)

    def token_shift(f, p8_ref, i_ref, m_ref):
        nc = m_ref.shape[1]
        prev_row = jnp.where(i == 0, i_ref[0], p8_ref[0, SUBLANES - 1:SUBLANES, :nc])
        rolled = pltpu.roll(f, shift=1, axis=0)
        row = lax.broadcasted_iota(I32, f.shape, 0)
        f_prev = jnp.where(row == 0, prev_row, rolled)
        return f + (f_prev - f) * m_ref[...]

    fs = token_shift(f_ref[0], prev8_ref, init_ref, mu_ref)
    n_dec = w2_ref.shape[0]
    n_icl = a2_ref.shape[0]
    n_lora = tmu_ref.shape[1]
    ts = token_shift(ft_ref[0][:, :n_lora], tprev8_ref, tinit_ref, tmu_ref)
    w_ = width
    r = fs[:, 0:w_]
    k = fs[:, w_:2 * w_]
    v = fs[:, 2 * w_:3 * w_]
    wd = ts[:, 0:n_dec]
    ad = ts[:, n_dec:n_dec + n_icl]
    gd = ts[:, n_dec + n_icl:]
    z = w0_ref[...] + _mm3(jnp.tanh(wd), w2_ref[...])
    nz = -z
    softplus = jnp.maximum(nz, 0.0) + jnp.log(1.0 + jnp.exp(-jnp.abs(nz)))
    w = -softplus - 0.5
    ld = -jnp.exp(w)
    a = 1.0 / (1.0 + jnp.exp(-(a0_ref[...] + _mm3(ad, a2_ref[...]))))
    g = _mm3(1.0 / (1.0 + jnp.exp(-gd)), g2_ref[...])
    kk = k * kk_ref[...]
    k2 = k * (1.0 + (a - 1.0) * ka_ref[...])
    rk = r * k2 * rk_ref[...]
    ones_bd = _pair_ones()
    for p in range(w_ // PAIR):
        sl = slice(p * PAIR, (p + 1) * PAIR)
        kkp = kk[:, sl]
        nrm = jnp.sqrt(_head_sum(kkp * kkp, ones_bd))
        kkp = kkp / jnp.maximum(nrm, 1e-12)
        ap = a[:, sl]
        r_o[0, p] = r[:, sl]
        k_o[0, p] = k2[:, sl]
        v_o[0, p] = v[:, sl]
        kk_o[0, p] = kkp
        b_o[0, p] = kkp * ap
        ld_o[0, p] = ld[:, sl]
        g_o[0, p] = g[:, sl]
        bon_o[0, p] = _head_sum(rk[:, sl], ones_bd) * v[:, sl]


def rwkv_prep(f_main, f_tail, init_prev, mu, w0, w2, a0, a2, g2, k_k, k_a, r_k, *, tm, width):
    b, t, n_main = f_main.shape
    n_tail = f_tail.shape[2]
    n_lora = mu.shape[0] - n_main
    npair = width // PAIR
    row1 = lambda x: x.reshape(1, -1)
    kern = functools.partial(_rwkv_prep_body, tm=tm, width=width)
    full = lambda a: pl.BlockSpec(a.shape, lambda bi, i: (0,) * a.ndim)
    args = [row1(w0), w2, row1(a0), a2, g2, row1(k_k), row1(k_a), row1(r_k)]
    out_spec = pl.BlockSpec((1, npair, tm, PAIR), lambda bi, i: (bi, 0, i, 0))
    out_shape = jax.ShapeDtypeStruct((b, npair, t, PAIR), F32)
    prev8_map = lambda bi, i: (bi, jnp.maximum(i * (tm // SUBLANES) - 1, 0), 0)

    def feature_specs(ncols, n_init):
        return [pl.BlockSpec((1, tm, ncols), lambda bi, i: (bi, i, 0)),
                pl.BlockSpec((1, SUBLANES, ncols), prev8_map),
                pl.BlockSpec((1, 1, n_init), lambda bi, i: (bi, 0, 0)),
                pl.BlockSpec((1, n_init), lambda bi, i: (0, 0))]

    return pl.pallas_call(
        kern,
        grid=(b, t // tm),
        in_specs=feature_specs(n_main, n_main) + feature_specs(n_tail, n_lora) + [full(a) for a in args],
        out_specs=[out_spec] * 8,
        out_shape=[out_shape] * 8,
        compiler_params=_cparams("parallel", "parallel"),
        name="rwkv_prep",
    )(f_main, f_main, init_prev[:, :n_main].reshape(b, 1, n_main), row1(mu[:n_main]),
      f_tail, f_tail, init_prev[:, n_main:].reshape(b, 1, n_lora), row1(mu[n_main:]), *args)


def _rwkv_chunk_body(r_ref, k_ref, v_ref, kk_ref, b_ref, ld_ref, g_ref, bon_ref, s0_ref,
                     lng_ref, lnb_ref, o_ref, st_ref, s_ref, *, L, npair, group):
    c = pl.program_id(1)

    @pl.when(c == 0)
    def _():
        s_ref[...] = s0_ref[0]

    L2 = 2 * L
    row = lax.broadcasted_iota(I32, (L2, L2), 0)
    col = lax.broadcasted_iota(I32, (L2, L2), 1)
    eye = (row == col).astype(F32)
    row4 = lax.broadcasted_iota(I32, (2 * L2, 2 * L2), 0)
    col4 = lax.broadcasted_iota(I32, (2 * L2, 2 * L2), 1)
    rr, cc = row4 % L2, col4 % L2
    tri_all = ((rr // L) == (cc // L)) & ((cc < rr) | ((row4 >= L2) & (cc == rr)))
    tr = lax.broadcasted_iota(I32, (L, L), 0)
    tc = lax.broadcasted_iota(I32, (L, L), 1)
    cum_mat = (tc <= tr).astype(BF16)
    lane = lax.broadcasted_iota(I32, (L, PAIR), 1)
    first = lane < HEAD_DIM
    ones_bd = _pair_ones()
    n_sq = max(int(math.ceil(math.log2(L))) - 1, 0)

    def block_diag(x):
        return jnp.concatenate([jnp.where(first, x, 0.0), jnp.where(first, 0.0, x)], axis=0)

    def group_step(gi, carry):
        ps = [gi * group + j for j in range(group)]
        each = lambda f, *cols: [f(*args) for args in zip(*cols)]
        ld = [ld_ref[0, p] for p in ps]
        cum = each(lambda x: _mm_exact_rhs_t(cum_mat, x), ld)
        dec = each(jnp.exp, cum)
        dec_inv = each(lambda c_: jnp.exp(-c_), cum)
        a_t = each(lambda p, c_, l_: block_diag(-kk_ref[0, p] * jnp.exp(c_ - l_)), ps, cum, ld)
        b_t = each(lambda p, e: block_diag(b_ref[0, p] * e), ps, dec_inv)
        k_t = each(lambda p, e: block_diag(k_ref[0, p] * e), ps, dec_inv)
        r_t = each(lambda p, e: block_diag(r_ref[0, p] * e), ps, dec)
        v_b = each(lambda p: block_diag(v_ref[0, p]), ps)
        ar = each(lambda a, r: jnp.concatenate([a, r], axis=0), a_t, r_t)
        bk = each(lambda b, k: jnp.concatenate([b, k], axis=0), b_t, k_t)
        cross = each(lambda x_, y_: jnp.where(tri_all, _mm_nt(x_, y_), 0.0), ar, bk)
        a_ab = [c_[:L2, :L2] for c_ in cross]
        a_rb = [c_[L2:, :L2] for c_ in cross]
        akrk = [c_[:, L2:] for c_ in cross]
        t_inv = each(lambda a: eye + a, a_ab)
        if n_sq >= 1:
            x = each(lambda a: _mm(a, a), a_ab)
            for _ in range(n_sq - 1):
                xt = each(lambda x_, t: _mm(x_, jnp.concatenate([x_, t], axis=1)), x, t_inv)
                x = [z[:, :L2] for z in xt]
                t_inv = each(lambda t, z: t + z[:, L2:], t_inv, xt)
            t_inv = each(lambda t, x_: t + _mm(x_, t), t_inv, x)
        s = [s_ref[p] for p in ps]
        ar_s = each(_mm_nt, ar, s)
        akrk_v = each(_mm, akrk, v_b)
        u = each(lambda t, p1, p2: _mm(t, p1[:L2] + p2[:L2]), t_inv, ar_s, akrk_v)
        y_b = each(lambda p1, rb, u_, p2: p1[L2:] + _mm(rb, u_) + p2[L2:], ar_s, a_rb, u, akrk_v)
        s_new = each(lambda s_, u_, b, v, k, d: (s_ + _mm(u_.T, b) + _mm(v.T, k)) * d[L - 1:L, :],
                     s, u, b_t, v_b, k_t, dec)
        for p, sn in zip(ps, s_new):
            s_ref[p] = sn
        y = each(lambda yb: yb[:L] + yb[L:], y_b)
        mean = each(lambda y_: _head_sum(y_, ones_bd) * (1.0 / HEAD_DIM), y)
        d = each(lambda y_, m: y_ - m, y, mean)
        var = each(lambda d_: _head_sum(d_ * d_, ones_bd) * (1.0 / HEAD_DIM), d)
        for p, d_, v_ in zip(ps, d, var):
            yn = d_ * lax.rsqrt(v_ + GN_EPS) * lng_ref[p] + lnb_ref[p]
            o_ref[0, p] = (yn + bon_ref[0, p]) * g_ref[0, p]
        return carry

    lax.fori_loop(0, npair // group, group_step, 0)

    @pl.when(c == pl.num_programs(1) - 1)
    def _():
        st_ref[0] = s_ref[...]


def _mm_exact_rhs_t(m_bf16, x):
    hi = x.astype(BF16)
    r1 = x - hi.astype(F32)
    mid = r1.astype(BF16)
    lo = (r1 - mid.astype(F32)).astype(BF16)
    d = lambda y: jnp.dot(m_bf16, y, preferred_element_type=F32)
    return d(hi) + (d(mid) + d(lo))


def rwkv_chunk(feats, s0_bd, ln_g, ln_b, *, L, group=8):
    b, npair, t, _ = feats[0].shape
    blk = pl.BlockSpec((1, npair, L, PAIR), lambda bi, c: (bi, 0, c, 0))
    st_spec = pl.BlockSpec((1, npair, PAIR, PAIR), lambda bi, c: (bi, 0, 0, 0))
    par_spec = pl.BlockSpec((npair, 1, PAIR), lambda bi, c: (0, 0, 0))
    kern = functools.partial(_rwkv_chunk_body, L=L, npair=npair, group=group)
    return pl.pallas_call(
        kern,
        grid=(b, t // L),
        in_specs=[blk] * 8 + [st_spec, par_spec, par_spec],
        out_specs=[blk, st_spec],
        out_shape=[jax.ShapeDtypeStruct((b, npair, t, PAIR), F32),
                   jax.ShapeDtypeStruct((b, npair, PAIR, PAIR), F32)],
        scratch_shapes=[pltpu.VMEM((npair, PAIR, PAIR), F32)],
        compiler_params=_cparams("parallel", "arbitrary"),
        name="rwkv_chunk",
    )(*feats, s0_bd, ln_g.reshape(npair, 1, PAIR), ln_b.reshape(npair, 1, PAIR))


def _state_to_block_diag(s):
    b, h, n, _ = s.shape
    s = s.reshape(b, h // 2, 2, n, n)
    z = jnp.zeros_like(s[:, :, 0])
    top = jnp.concatenate([s[:, :, 0], z], axis=-1)
    bot = jnp.concatenate([z, s[:, :, 1]], axis=-1)
    return jnp.concatenate([top, bot], axis=-2)


def _state_from_block_diag(s_bd):
    b, p, _, _ = s_bd.shape
    n = HEAD_DIM
    return jnp.stack([s_bd[:, :, :n, :n], s_bd[:, :, n:, n:]], axis=2).reshape(b, 2 * p, n, n)


def _t5_bucket(dist):
    exact = N_BUCKETS // 2
    d = jnp.maximum(dist, 0)
    far = exact + (jnp.log(jnp.maximum(d, 1).astype(F32) / exact) / math.log(MAX_DISTANCE / exact)
                   * (N_BUCKETS - exact)).astype(I32)
    return jnp.where(d < exact, d, jnp.minimum(far, N_BUCKETS - 1))


def _bias_tables_body(rb_ref, o_ref, *, offsets, n_heads):
    r = lax.broadcasted_iota(I32, (Q_BLOCK, Q_BLOCK), 0)
    c = lax.broadcasted_iota(I32, (Q_BLOCK, Q_BLOCK), 1)
    for t, (off, key_major) in enumerate(offsets):
        bucket = _t5_bucket((c - r if key_major else r - c) + off)
        for h in range(n_heads):
            def body(bk, acc):
                return jnp.where(bucket == bk, rb_ref[bk, h], acc)
            tile = lax.fori_loop(0, N_BUCKETS, body, jnp.zeros((Q_BLOCK, Q_BLOCK), F32))
            o_ref[t, h] = tile - rb_ref[N_BUCKETS - 1, h]


def bias_tables(rel_bias, offsets):
    n_heads = rel_bias.shape[1]
    kern = functools.partial(_bias_tables_body, offsets=tuple(offsets), n_heads=n_heads)
    return pl.pallas_call(
        kern,
        in_specs=[pl.BlockSpec(memory_space=pltpu.SMEM)],
        out_specs=pl.BlockSpec(memory_space=pltpu.VMEM),
        out_shape=jax.ShapeDtypeStruct((len(offsets), n_heads, Q_BLOCK, Q_BLOCK), F32),
        name="bias_tables",
    )(rel_bias)


def _sortable_key(scores):
    bits = lax.bitcast_convert_type(scores + 0.0, I32)
    return jnp.where(bits < 0, bits ^ 0x7FFFFFFF, bits)


def _count(mask):
    return jnp.sum(mask.astype(F32), axis=-1, keepdims=True)


def _topk_select(key, topk, n_index_bits):
    rows, n = key.shape
    kf = float(topk)
    t0 = jnp.where(_count(key >= 0) >= kf, 0, INT_MIN).astype(I32)

    def value_bit(i, t):
        cand = t + lax.shift_left(jnp.int32(1), 30 - i)
        return jnp.where(_count(key >= cand) >= kf, cand, t)

    thr = lax.fori_loop(0, 31, value_bit, t0)
    above = key > thr
    ties = key == thr
    need = kf - _count(above)
    idx = lax.broadcasted_iota(I32, (rows, n), 1)

    def lowest_ties():
        def index_bit(i, m):
            cand = m + lax.shift_left(jnp.int32(1), n_index_bits - 1 - i)
            return jnp.where(_count(ties & (idx < cand)) <= need, cand, m)
        return lax.fori_loop(0, n_index_bits, index_bit, jnp.zeros((rows, 1), I32))

    surplus = jnp.max(_count(ties) - need) > 0.0
    m = lax.cond(surplus, lowest_ties, lambda: jnp.full((rows, 1), 2 ** n_index_bits, I32))
    return above | (ties & (idx < m))


SUM_CHAINS = 4
HEAD_GROUP = 4


def _sum_rows(x):
    r = x.shape[0]
    if r % (SUM_CHAINS * SUBLANES) == 0 and r > SUM_CHAINS * SUBLANES:
        x = jnp.sum(x.reshape(SUM_CHAINS, r // SUM_CHAINS, x.shape[1]), axis=1)
    return jnp.sum(x, axis=0, keepdims=True)


def _max_rows(x):
    r = x.shape[0]
    if r % (SUM_CHAINS * SUBLANES) == 0 and r > SUM_CHAINS * SUBLANES:
        x = jnp.max(x.reshape(SUM_CHAINS, r // SUM_CHAINS, x.shape[1]), axis=1)
    return jnp.max(x, axis=0, keepdims=True)


def _topk_select_cols(key, topk, n_index_bits):
    n, cols = key.shape
    kf = float(topk)
    cnt = lambda m: _sum_rows(m.astype(F32))
    t0 = jnp.where(cnt(key >= 0) >= kf, 0, INT_MIN).astype(I32)

    def value_bit(i, t):
        cand = t + lax.shift_left(jnp.int32(1), 30 - i)
        return jnp.where(cnt(key >= cand) >= kf, cand, t)

    thr = lax.fori_loop(0, 31, value_bit, t0)
    above = key > thr
    ties = key == thr
    need = kf - cnt(above)
    idx = lax.broadcasted_iota(I32, (n, cols), 0)

    def lowest_ties():
        def index_bit(i, m):
            cand = m + lax.shift_left(jnp.int32(1), n_index_bits - 1 - i)
            return jnp.where(cnt(ties & (idx < cand)) <= need, cand, m)
        return lax.fori_loop(0, n_index_bits, index_bit, jnp.zeros((1, cols), I32))

    surplus = jnp.max(cnt(ties) - need) > 0.0
    m = lax.cond(surplus, lowest_ties, lambda: jnp.full((1, cols), 2 ** n_index_bits, I32))
    return above | (ties & (idx < m))


def _dsa_prompt_block(nb, ik_ref, iqt_ref, wt_ref, k_ref, qt_ref, vt_ref, bias_ref, o_ref, mask_ref,
                      *, n_heads, n_idx_heads, topk):
    w = nb * Q_BLOCK
    ik = ik_ref[0, :w, :]
    idx_dim = ik.shape[1]

    def head_rows(ref, h, dh, cols=slice(None)):
        return ref[0, pl.ds(pl.multiple_of(h * dh, dh), dh), cols]

    def idx_heads(gi, acc):
        hs = [gi * HEAD_GROUP + j for j in range(HEAD_GROUP)]
        dots = [jnp.dot(ik, head_rows(iqt_ref, h, idx_dim), preferred_element_type=F32) for h in hs]
        terms = [jnp.maximum(d, 0.0) * wt_ref[0, h] for d, h in zip(dots, hs)]
        while len(terms) > 1:
            terms = [a + b for a, b in zip(terms[0::2], terms[1::2])]
        return acc + terms[0]

    scores = lax.fori_loop(0, n_idx_heads // HEAD_GROUP, idx_heads, jnp.zeros((w, Q_BLOCK), F32))
    kpos = lax.broadcasted_iota(I32, (w, Q_BLOCK), 0)
    qpos = (nb - 1) * Q_BLOCK + lax.broadcasted_iota(I32, (w, Q_BLOCK), 1)
    valid = kpos <= qpos
    if w <= topk:
        sel = valid
    else:
        key = jnp.where(valid, _sortable_key(scores), INT_MIN)
        sel = valid & _topk_select_cols(key, topk, int(math.ceil(math.log2(w))) + 1)
    mask_ref[:w, :] = jnp.where(sel, 0.0, -jnp.inf)

    def with_near_bias(logits, h):
        near = [logits[w - Q_BLOCK:] + bias_ref[0, h]]
        if nb >= 2:
            near = [logits[w - 2 * Q_BLOCK:w - Q_BLOCK] + bias_ref[1, h]] + near
        if nb >= 3:
            near = [logits[:w - 2 * Q_BLOCK]] + near
        return jnp.concatenate(near, axis=0) if len(near) > 1 else near[0]

    pair_row_half = lax.broadcasted_iota(I32, (PAIR, Q_BLOCK), 0) // HEAD_DIM

    def head_logits(gi, j):
        p = gi * (HEAD_GROUP // 2) + j // 2
        qt_pair = head_rows(qt_ref, p, PAIR)
        qt_head = jnp.where(pair_row_half == j % 2, qt_pair, jnp.zeros_like(qt_pair))
        return jnp.dot(k_ref[0, p, :w, :], qt_head, preferred_element_type=F32)

    def attn_heads(gi, carry):
        hs = [gi * HEAD_GROUP + j for j in range(HEAD_GROUP)]
        mask = mask_ref[:w, :]
        logits = [head_logits(gi, j) + mask for j in range(HEAD_GROUP)]
        logits = [with_near_bias(l_, h) for l_, h in zip(logits, hs)]
        mx = [_max_rows(l_) for l_ in logits]
        e = [jnp.exp(l_ - m_) for l_, m_ in zip(logits, mx)]
        den = [_sum_rows(e_) for e_ in e]
        o = [jnp.dot(head_rows(vt_ref, h, HEAD_DIM, slice(0, w)), e_.astype(BF16),
                     preferred_element_type=F32) for h, e_ in zip(hs, e)]
        for h, o_, d_ in zip(hs, o, den):
            o_ref[0, pl.ds(pl.multiple_of(h * HEAD_DIM, HEAD_DIM), HEAD_DIM), :] = (o_ / d_).astype(o_ref.dtype)
        return carry

    lax.fori_loop(0, n_heads // HEAD_GROUP, attn_heads, 0)


def _dsa_prompt_body(ik_ref, iqt_ref, wt_ref, k_ref, qt_ref, vt_ref, bias_ref, o_ref, mask_ref,
                     *, n_heads, n_idx_heads, seq, topk):
    i = pl.program_id(0)
    for nb in range(1, seq // Q_BLOCK + 1):
        @pl.when(i == nb - 1)
        def _(nb=nb):
            _dsa_prompt_block(nb, ik_ref, iqt_ref, wt_ref, k_ref, qt_ref, vt_ref, bias_ref, o_ref, mask_ref,
                              n_heads=n_heads, n_idx_heads=n_idx_heads, topk=topk)


def dsa_prompt(ik, iqt, wts, k_pairs, qt, vt, bias_tiles, *, topk):
    b, width, s = qt.shape
    h = width // HEAD_DIM
    ih = wts.shape[1]
    kern = functools.partial(_dsa_prompt_body, n_heads=h, n_idx_heads=ih, seq=s, topk=topk)
    grid_spec = pltpu.PrefetchScalarGridSpec(
        num_scalar_prefetch=0,
        grid=(s // Q_BLOCK, b),
        in_specs=[pl.BlockSpec((1, s, ik.shape[2]), lambda i, bi: (bi, 0, 0)),
                  pl.BlockSpec((1, iqt.shape[1], Q_BLOCK), lambda i, bi: (bi, 0, i)),
                  pl.BlockSpec((1, ih, 1, Q_BLOCK), lambda i, bi: (bi, 0, 0, i)),
                  pl.BlockSpec((1, h // 2, s, PAIR), lambda i, bi: (bi, 0, 0, 0)),
                  pl.BlockSpec((1, width, Q_BLOCK), lambda i, bi: (bi, 0, i)),
                  pl.BlockSpec((1, width, s), lambda i, bi: (bi, 0, 0)),
                  pl.BlockSpec(bias_tiles.shape, lambda i, bi: (0, 0, 0, 0))],
        out_specs=pl.BlockSpec((1, width, Q_BLOCK), lambda i, bi: (bi, 0, i)),
        scratch_shapes=[pltpu.VMEM((s, Q_BLOCK), F32)],
    )
    return pl.pallas_call(
        kern,
        grid_spec=grid_spec,
        out_shape=jax.ShapeDtypeStruct((b, width, s), BF16),
        compiler_params=_cparams("arbitrary", "arbitrary"),
        name="dsa_prompt",
    )(ik, iqt, wts, k_pairs, qt, vt, bias_tiles)


def _dsa_select_body(pt_ref, iq_ref, wt_ref, iknew_ref, cidx_hbm, o_ref, ikbuf, sem,
                     *, layer, n_pages, group, n_idx_heads, t_new, topk):
    s = pl.program_id(0)
    past = n_pages * PAGE_SIZE
    n_keys = past + PAGE_SIZE

    def ik_copy(i):
        g = i // n_pages
        p = i % n_pages
        page = pt_ref[(s * group + g) * n_pages + p]
        return pltpu.make_async_copy(cidx_hbm.at[layer, page], ikbuf.at[g, p], sem)

    def ik_start(i, carry):
        ik_copy(i).start()
        return carry

    def ik_wait(i, carry):
        ik_copy(i).wait()
        return carry

    lax.fori_loop(0, group * n_pages, ik_start, 0)
    for g in range(group):
        ikbuf[g, n_pages] = iknew_ref[g]
    lax.fori_loop(0, group * n_pages, ik_wait, 0)

    scores = []
    for g in range(group):
        ikt_all = jnp.concatenate([ikbuf[g, p] for p in range(n_pages + 1)], axis=1)
        dots = _mm(iq_ref[g], ikt_all)
        weighted = jnp.maximum(dots, 0.0) * wt_ref[g]
        scores.append(jnp.sum(weighted.reshape(n_idx_heads, t_new, n_keys), axis=0))
    scores = jnp.concatenate(scores, axis=0)
    shape = (group * t_new, n_keys)
    qpos = past + lax.broadcasted_iota(I32, shape, 0) % t_new
    kpos = lax.broadcasted_iota(I32, shape, 1)
    valid = kpos <= qpos
    key = jnp.where(valid, _sortable_key(scores), INT_MIN)
    sel = valid & _topk_select(key, topk, int(math.log2(n_keys)) + 1)
    o_ref[...] = jnp.where(sel, 0.0, -jnp.inf).reshape(group, t_new, n_keys)


def dsa_sample_select(page_table, iq_rows, wt_rows, ik_new_t, cache_idx_kt, *, layer, n_idx_heads, t_new, topk,
                      group):
    db, n_pages = page_table.shape
    idx_dim = cache_idx_kt.shape[2]
    n_keys = (n_pages + 1) * PAGE_SIZE
    kern = functools.partial(_dsa_select_body, layer=layer, n_pages=n_pages, group=group,
                             n_idx_heads=n_idx_heads, t_new=t_new, topk=topk)
    per_g = lambda shape: pl.BlockSpec((group,) + shape, lambda si, pt: (si,) + (0,) * len(shape))
    grid_spec = pltpu.PrefetchScalarGridSpec(
        num_scalar_prefetch=1,
        grid=(db // group,),
        in_specs=[per_g((n_idx_heads * t_new, idx_dim)), per_g((n_idx_heads * t_new, 1)),
                  per_g((idx_dim, PAGE_SIZE)), pl.BlockSpec(memory_space=pl.ANY)],
        out_specs=per_g((t_new, n_keys)),
        scratch_shapes=[pltpu.VMEM((group, n_pages + 1, idx_dim, PAGE_SIZE), F32), pltpu.SemaphoreType.DMA(())],
    )
    return pl.pallas_call(
        kern,
        grid_spec=grid_spec,
        out_shape=jax.ShapeDtypeStruct((db, t_new, n_keys), F32),
        compiler_params=_cparams("arbitrary"),
        name="dsa_sample_select",
    )(page_table.reshape(-1), iq_rows, wt_rows, ik_new_t, cache_idx_kt)


def _dsa_sample_body(pt_ref, mask_ref, qbd_ref, knew_ref, vnew_ref,
                     blast_ref, bnew_ref, ck_hbm, cv_hbm, o_ref,
                     kbuf, vbuf, sem_k, sem_v,
                     *, layer, n_pages, chunk, n_heads, t_new):
    b = pl.program_id(0)
    n_seq = pl.num_programs(0)
    past = n_pages * PAGE_SIZE
    n_chunks = n_pages // chunk
    rows = n_heads * t_new
    ck = chunk * PAGE_SIZE

    def kv_copies(seq, c, j):
        slot = c % 2
        page = pt_ref[seq * n_pages + c * chunk + j]
        return (pltpu.make_async_copy(ck_hbm.at[layer, page], kbuf.at[slot, j], sem_k.at[slot]),
                pltpu.make_async_copy(cv_hbm.at[layer, page], vbuf.at[slot, j], sem_v.at[slot]))

    def start_chunk(seq, c):
        for j in range(chunk):
            kc, vc = kv_copies(seq, c, j)
            kc.start()
            vc.start()

    def wait_chunk(c):
        for j in range(chunk):
            kc, vc = kv_copies(b, c, j)
            kc.wait()
            vc.wait()

    @pl.when(b == 0)
    def _():
        start_chunk(0, 0)

    sel_rows = jnp.tile(mask_ref[0], (n_heads, 1))

    q_rep = jnp.tile(qbd_ref[0], (n_heads, 1))
    row_head = lax.broadcasted_iota(I32, q_rep.shape, 0) // t_new
    col_head = lax.broadcasted_iota(I32, q_rep.shape, 1) // HEAD_DIM
    qbd = jnp.where(row_head == col_head, q_rep, jnp.zeros_like(q_rep))
    neg = -1e30

    def update(state, logits, maskc, vt_bf16):
        m, l, acc = state
        s = logits + maskc
        m_new = jnp.maximum(m, jnp.max(s, axis=-1, keepdims=True))
        alpha = jnp.exp(m - m_new)
        p = jnp.exp(s - m_new)
        l = alpha * l + jnp.sum(p, axis=-1, keepdims=True)
        acc = alpha * acc + _mm_nt(p, vt_bf16)
        return m_new, l, acc

    def pages_t(buf, slot):
        return jnp.concatenate([buf[slot, j] for j in range(chunk)], axis=1).astype(BF16)

    state = (jnp.full((rows, 1), neg, F32), jnp.zeros((rows, 1), F32),
             jnp.zeros((rows, qbd.shape[1]), F32))
    for c in range(n_chunks):
        if c + 1 < n_chunks:
            start_chunk(b, c + 1)
        else:
            @pl.when(b + 1 < n_seq)
            def _():
                start_chunk(b + 1, 0)
        wait_chunk(c)
        slot = c % 2
        logits = _mm(qbd, pages_t(kbuf, slot))
        if c == n_chunks - 1:
            logits = jnp.concatenate([logits[:, :ck - PAGE_SIZE],
                                      logits[:, ck - PAGE_SIZE:] + blast_ref[...]], axis=1)
        state = update(state, logits, sel_rows[:, c * ck:(c + 1) * ck], pages_t(vbuf, slot))
    logits = _mm(qbd, knew_ref[0]) + bnew_ref[...]
    m, l, acc = update(state, logits, sel_rows[:, past:], vnew_ref[0])
    out = jnp.where(row_head == col_head, acc / l, 0.0)
    o_ref[0] = jnp.sum(out.reshape(n_heads, t_new, out.shape[1]), axis=0)


def dsa_sample(page_table, mask, q_bd, k_new_t, v_new_t, bias_last, bias_new, cache_kt, cache_vt,
               *, layer, n_heads, t_new, chunk):
    db, n_pages = page_table.shape
    assert (n_pages // chunk) % 2 == 0, "chunks alternate between two buffers across sequences"
    rows = n_heads * t_new
    width = q_bd.shape[2]
    kern = functools.partial(_dsa_sample_body, layer=layer, n_pages=n_pages, chunk=chunk, n_heads=n_heads,
                             t_new=t_new)
    per_b = lambda shape: pl.BlockSpec((1,) + shape, lambda bi, pt: (bi,) + (0,) * len(shape))
    const = lambda shape: pl.BlockSpec(shape, lambda bi, pt: (0,) * len(shape))
    any_spec = pl.BlockSpec(memory_space=pl.ANY)
    grid_spec = pltpu.PrefetchScalarGridSpec(
        num_scalar_prefetch=1,
        grid=(db,),
        in_specs=[per_b((t_new, mask.shape[2])),
                  per_b((t_new, width)), per_b((width, PAGE_SIZE)), per_b((width, PAGE_SIZE)),
                  const((rows, PAGE_SIZE)), const((rows, PAGE_SIZE)),
                  any_spec, any_spec],
        out_specs=per_b((t_new, width)),
        scratch_shapes=[pltpu.VMEM((2, chunk, width, PAGE_SIZE), F32),
                        pltpu.VMEM((2, chunk, width, PAGE_SIZE), F32),
                        pltpu.SemaphoreType.DMA((2,)),
                        pltpu.SemaphoreType.DMA((2,))],
    )
    return pl.pallas_call(
        kern,
        grid_spec=grid_spec,
        out_shape=jax.ShapeDtypeStruct((db, t_new, width), F32),
        compiler_params=_cparams("arbitrary"),
        name="dsa_sample",
    )(page_table.reshape(-1), mask, q_bd, k_new_t, v_new_t, bias_last, bias_new, cache_kt, cache_vt)


def _token_minor_cache(cache):
    l, pool, page = cache.shape[:3]
    nd = cache.ndim
    return cache.transpose((0, 1) + tuple(range(3, nd)) + (2,)).reshape(l, pool, -1, page)


def _dsa_sample_inputs(q, k_new, v_new, iq, ik_new, iw, bias_tiles):
    db, t, h, dh = q.shape
    ih = iq.shape[2]
    iq_rows = iq.transpose(0, 2, 1, 3).reshape(db, ih * t, -1).astype(BF16)
    wt_rows = (iw * (ih ** -0.5 * iq.shape[3] ** -0.5)).transpose(0, 2, 1).reshape(db, ih * t, 1)
    q_bd = q.reshape(db, t, h * dh)
    page_t = lambda x: jnp.pad(x.reshape(db, t, -1).transpose(0, 2, 1), ((0, 0), (0, 0), (0, PAGE_SIZE - t)))
    bias_new = bias_tiles[0, :, :t, :].reshape(h * t, Q_BLOCK)
    bias_last = bias_tiles[1, :, :t, :].reshape(h * t, Q_BLOCK)
    return ((iq_rows, wt_rows, page_t(ik_new)),
            (q_bd.astype(BF16), page_t(k_new).astype(BF16), page_t(v_new).astype(BF16), bias_last, bias_new))


def _matmul_residual_body(x_ref, a_ref, b_ref, w_ref, o_ref):
    ka = a_ref.shape[1]
    o_ref[...] = x_ref[...] + (_mm(a_ref[...], w_ref[:ka, :]) + _mm(b_ref[...], w_ref[ka:, :]))


def matmul_residual(x, a, b, w, *, tm, tn):
    n, d = x.shape
    ka, kb = a.shape[1], b.shape[1]
    return pl.pallas_call(
        _matmul_residual_body,
        grid=(n // tm, d // tn),
        in_specs=[pl.BlockSpec((tm, tn), lambda i, j: (i, j)),
                  pl.BlockSpec((tm, ka), lambda i, j: (i, 0)),
                  pl.BlockSpec((tm, kb), lambda i, j: (i, 0)),
                  pl.BlockSpec((ka + kb, tn), lambda i, j: (0, j))],
        out_specs=pl.BlockSpec((tm, tn), lambda i, j: (i, j)),
        out_shape=jax.ShapeDtypeStruct((n, d), F32),
        compiler_params=_cparams("parallel", "parallel"),
        name="matmul_residual",
    )(x, a, b, w)


def _cross_attn_body(x_ref, g_ref, wq_ref, mk_ref, mv_ref, wo_ref, o_ref, *, groups, t_rows, n_heads, head_dim):
    x = x_ref[...]
    h = _rmsnorm(x, g_ref[...]).astype(BF16)
    q = jnp.dot(h, wq_ref[...], preferred_element_type=F32).astype(BF16)
    scale = head_dim ** -0.5
    outs = []
    for gi in range(groups):
        qg = q[gi * t_rows:(gi + 1) * t_rows]
        heads = []
        for hh in range(n_heads):
            sl = slice(hh * head_dim, (hh + 1) * head_dim)
            logits = _mm_nt(qg[:, sl], mk_ref[gi, :, sl]) * scale
            mx = jnp.max(logits, axis=-1, keepdims=True)
            e = jnp.exp(logits - mx)
            p = e / jnp.sum(e, axis=-1, keepdims=True)
            heads.append(jnp.dot(p.astype(BF16), mv_ref[gi, :, sl], preferred_element_type=F32))
        outs.append(jnp.concatenate(heads, axis=1))
    o = jnp.concatenate(outs, axis=0) if groups > 1 else outs[0]
    o_ref[...] = x + jnp.dot(o.astype(BF16), wo_ref[...], preferred_element_type=F32)


def cross_attn(x, g, wq, mk, mv, wo, *, groups, t_rows, seq_tiles, n_heads):
    n, d = x.shape
    xw = wq.shape[1]
    rows = groups * t_rows
    m = mk.shape[1]
    kern = functools.partial(_cross_attn_body, groups=groups, t_rows=t_rows, n_heads=n_heads,
                             head_dim=xw // n_heads)
    return pl.pallas_call(
        kern,
        grid=(n // rows,),
        in_specs=[pl.BlockSpec((rows, d), lambda i: (i, 0)),
                  pl.BlockSpec((1, d), lambda i: (0, 0)),
                  pl.BlockSpec((d, xw), lambda i: (0, 0)),
                  pl.BlockSpec((groups, m, xw), lambda i: (i // seq_tiles, 0, 0)),
                  pl.BlockSpec((groups, m, xw), lambda i: (i // seq_tiles, 0, 0)),
                  pl.BlockSpec((xw, d), lambda i: (0, 0))],
        out_specs=pl.BlockSpec((rows, d), lambda i: (i, 0)),
        out_shape=jax.ShapeDtypeStruct((n, d), F32),
        compiler_params=_cparams("parallel"),
        name="cross_attn",
    )(x, g.reshape(1, d), wq, mk, mv, wo)


def _router_body(x_ref, g_ref, wr_ref, br_ref, h_ref, r_ref, *, n_groups, per_group):
    h = _rmsnorm(x_ref[...], g_ref[...])
    _store_slab_rows(h_ref, h, h.shape[1] // LANES)
    logits = _mm3(h, wr_ref[...]) + br_ref[...]
    lane = lax.broadcasted_iota(I32, logits.shape, 1).astype(F32)
    big = 1e9
    first_lane = lambda hit: jnp.min(jnp.where(hit, lane, big), axis=-1, keepdims=True)
    gl = jnp.where(lane < n_groups, logits, -jnp.inf)
    gmax = jnp.max(gl, axis=-1, keepdims=True)
    grp = first_lane(gl == gmax)
    p_grp = 1.0 / jnp.sum(jnp.exp(gl - gmax), axis=-1, keepdims=True)
    e_id = lane - n_groups
    in_grp = (e_id >= grp * per_group) & (e_id < (grp + 1.0) * per_group)
    el = jnp.where(in_grp, logits, -jnp.inf)
    v1 = jnp.max(el, axis=-1, keepdims=True)
    i1 = first_lane(el == v1) - n_groups
    el2 = jnp.where(e_id == i1, -jnp.inf, el)
    v2 = jnp.max(el2, axis=-1, keepdims=True)
    i2 = first_lane(el2 == v2) - n_groups
    e2 = jnp.exp(v2 - v1)
    g1 = p_grp / (1.0 + e2)
    g2 = p_grp * e2 / (1.0 + e2)
    r_ref[...] = jnp.where(lane == 0, g1, jnp.where(lane == 1, g2, jnp.where(
        lane == 2, i1, jnp.where(lane == 3, i2, 0.0))))


def router(x, g, w_r, b_r, *, tm, n_groups, per_group):
    n, d = x.shape
    kern = functools.partial(_router_body, n_groups=n_groups, per_group=per_group)
    return pl.pallas_call(
        kern,
        grid=(n // tm,),
        in_specs=[pl.BlockSpec((tm, d), lambda i: (i, 0)),
                  pl.BlockSpec((1, d), lambda i: (0, 0)),
                  pl.BlockSpec((d, LANES), lambda i: (0, 0)),
                  pl.BlockSpec((1, LANES), lambda i: (0, 0))],
        out_specs=[pl.BlockSpec((tm * (d // LANES), LANES), lambda i: (i, 0)),
                   pl.BlockSpec((tm, LANES), lambda i: (i, 0))],
        out_shape=[jax.ShapeDtypeStruct((n * (d // LANES), LANES), F32), jax.ShapeDtypeStruct((n, LANES), F32)],
        compiler_params=_cparams("parallel"),
        name="moe_router",
    )(x, g.reshape(1, d), w_r, b_r)


def _slab_rows(buf, offset, n_rows, n_slab, stride):
    return jnp.concatenate([buf[pl.ds(offset + s, n_rows, stride=stride), :] for s in range(n_slab)], axis=1)


def _store_slab_rows(buf, x, n_slab):
    n_rows = x.shape[0]
    for s in range(n_slab):
        buf[pl.ds(s, n_rows, stride=n_slab), :] = x[:, s * LANES:(s + 1) * LANES]


def _moe_ffn_body(te_ref, nr_ref, tok_ref, dst_ref, h_hbm, w1_ref, w3_ref, w2_ref, y_hbm,
                  xbuf, ybuf, gsem, ssem, *, tm):
    t = pl.program_id(0)
    n_tiles = pl.num_programs(0)
    n_slab = h_hbm.shape[1]
    slot = t % 2

    def slab(buf, sl, r):
        return buf.at[sl, pl.ds(pl.multiple_of(r * n_slab, n_slab), n_slab)]

    def gather_copy(tile, sl, r):
        return pltpu.make_async_copy(h_hbm.at[tok_ref[tile * tm + r]], slab(xbuf, sl, r), gsem.at[sl])

    def scatter_copy(tile, sl, r):
        return pltpu.make_async_copy(slab(ybuf, sl, r), y_hbm.at[dst_ref[tile * tm + r]], ssem.at[sl])

    def start_rows(n, copy):
        def pair(i, c):
            copy(2 * i).start(priority=0)
            copy(2 * i + 1).start(priority=1)
            return c
        lax.fori_loop(0, n // 2, pair, 0)

        @pl.when(n % 2 == 1)
        def _():
            copy(n - 1).start(priority=0)

    def wait_rows(n, copy):
        def one(r, c):
            copy(r).wait()
            return c
        lax.fori_loop(0, n, one, 0)

    @pl.when(t == 0)
    def _():
        xbuf[...] = jnp.zeros(xbuf.shape, F32)
        start_rows(nr_ref[0], lambda r: gather_copy(0, 0, r))

    nxt = jnp.minimum(t + 1, n_tiles - 1)

    @pl.when(t + 1 < n_tiles)
    def _():
        start_rows(nr_ref[nxt], lambda r: gather_copy(nxt, 1 - slot, r))

    @pl.when(t >= 2)
    def _():
        prev2 = jnp.maximum(t - 2, 0)
        wait_rows(nr_ref[prev2], lambda r: scatter_copy(prev2, slot, r))

    n = nr_ref[t]

    @pl.when(n > 0)
    def _():
        wait_rows(n, lambda r: gather_copy(t, slot, r))
        x = _slab_rows(xbuf.at[slot], 0, tm, n_slab, n_slab).astype(BF16)
        a = jnp.dot(x, w1_ref[0].astype(BF16), preferred_element_type=F32)
        bgate = jnp.dot(x, w3_ref[0].astype(BF16), preferred_element_type=F32)
        u = (a / (1.0 + jnp.exp(-a))) * bgate
        y = jnp.dot(u.astype(BF16), w2_ref[0].astype(BF16), preferred_element_type=F32)
        _store_slab_rows(ybuf.at[slot], y, n_slab)
        start_rows(n, lambda r: scatter_copy(t, slot, r))

    @pl.when(t == n_tiles - 1)
    def _():
        @pl.when(t >= 1)
        def _():
            prev1 = jnp.maximum(t - 1, 0)
            wait_rows(nr_ref[prev1], lambda r: scatter_copy(prev1, 1 - slot, r))
        wait_rows(n, lambda r: scatter_copy(t, slot, r))


def moe_ffn(tile_expert, tile_rows, row_tok, row_dst, h_slabs, n_out_rows, w1, w3, w2, *, tm):
    n_tiles = tile_expert.shape[0]
    d, de = w1.shape[1], w1.shape[2]
    n_slab = d // LANES
    h3 = h_slabs.reshape(-1, n_slab, LANES)
    w_in_spec = pl.BlockSpec((1, d, de), lambda t, te, nr, tok, dst: (te[t], 0, 0))
    grid_spec = pltpu.PrefetchScalarGridSpec(
        num_scalar_prefetch=4,
        grid=(n_tiles,),
        in_specs=[pl.BlockSpec(memory_space=pl.ANY), w_in_spec, w_in_spec,
                  pl.BlockSpec((1, de, d), lambda t, te, nr, tok, dst: (te[t], 0, 0))],
        out_specs=pl.BlockSpec(memory_space=pl.ANY),
        scratch_shapes=[pltpu.VMEM((2, tm * n_slab, LANES), F32), pltpu.VMEM((2, tm * n_slab, LANES), F32),
                        pltpu.SemaphoreType.DMA((2,)), pltpu.SemaphoreType.DMA((2,))],
    )
    y = pl.pallas_call(
        functools.partial(_moe_ffn_body, tm=tm),
        grid_spec=grid_spec,
        out_shape=jax.ShapeDtypeStruct((n_out_rows, n_slab, LANES), F32),
        compiler_params=_cparams("arbitrary"),
        name="moe_ffn",
    )(tile_expert, tile_rows, row_tok, row_dst, h3, w1, w3, w2)
    return y.reshape(n_out_rows * n_slab, LANES)


def _moe_dispatch(eids, n_experts, tm):
    n, k = eids.shape
    m = n * k
    flat_e = eids.reshape(-1)
    order = jnp.argsort(flat_e, stable=True).astype(I32)
    counts = jnp.sum((flat_e[:, None] == jnp.arange(n_experts)[None, :]).astype(I32), axis=0)
    padded = (counts + tm - 1) // tm * tm
    pad_end = jnp.cumsum(padded)
    pad_start = pad_end - padded
    start = jnp.cumsum(counts) - counts
    n_tiles = -(-m // tm) + n_experts
    tile_start = jnp.arange(n_tiles) * tm
    last = jnp.maximum(pad_end[-1] - 1, 0)
    tile_expert = jnp.minimum(jnp.searchsorted(pad_end, jnp.minimum(tile_start, last), side='right'),
                              n_experts - 1).astype(I32)
    first = tile_start - pad_start[tile_expert]
    tile_rows = jnp.where(tile_start < pad_end[-1], jnp.clip(counts[tile_expert] - first, 0, tm), 0).astype(I32)
    src = jnp.clip(start[tile_expert][:, None] + first[:, None] + jnp.arange(tm)[None, :], 0, m - 1)
    row_dst = order[src].reshape(-1)
    return tile_expert, tile_rows, (row_dst // k).astype(I32), row_dst.astype(I32)


def _combine_body(x_ref, r_ref, g_ref, y_ref, o_ref, *, tm, top_k):
    n_slab = x_ref.shape[1] // LANES
    route = r_ref[...]
    x = x_ref[...]
    for kk in range(top_k):
        x = x + _slab_rows(y_ref, kk * n_slab, tm, n_slab, top_k * n_slab) * route[:, kk:kk + 1]
    o_ref[...] = _rmsnorm(x, g_ref[...])


def moe_combine(x, route, g, y_slabs, *, tm, top_k, tile_offset):
    n, d = x.shape
    rows = tm * top_k * (d // LANES)
    return pl.pallas_call(
        functools.partial(_combine_body, tm=tm, top_k=top_k),
        grid=(n // tm,),
        in_specs=[pl.BlockSpec((tm, d), lambda i: (i, 0)),
                  pl.BlockSpec((tm, LANES), lambda i: (i, 0)),
                  pl.BlockSpec((1, d), lambda i: (0, 0)),
                  pl.BlockSpec((rows, LANES), lambda i: (i + tile_offset, 0))],
        out_specs=pl.BlockSpec((tm, d), lambda i: (i, 0)),
        out_shape=jax.ShapeDtypeStruct((n, d), F32),
        compiler_params=_cparams("parallel"),
        name="moe_combine",
    )(x, route, g.reshape(1, d), y_slabs)


def kernel(x_prompt, x_sample, mem_prompt, cache_k, cache_v, cache_idx_k, page_table, state_wkv, state_shift, cache_mem_k, cache_mem_v, g_mix, w_in, mu_shift, rw_w0, rw_w2, rw_a0, rw_a2, rw_g2, rw_kk, rw_ka, rw_rk, rw_ln_g, rw_ln_b, w_out, g_cross, g_mem, w_cq, w_ck, w_cv, w_co, g_ffn, w_rg, b_rg, w_re, b_re, w_e1, w_e3, w_e2, rel_bias, g_final):
    B, S, D = x_prompt.shape
    DB, T, _ = x_sample.shape
    assert w_in.shape[0] == 1, "single-layer trunk only"
    l = 0
    n_pages = page_table.shape[1]
    past = n_pages * PAGE_SIZE
    topk_p = min(TOPK_MAX, S // 4)
    topk_s = min(TOPK_MAX, (past + T) // 4)
    rw_proj = mu_shift.shape[1]
    width = rw_w0.shape[1]
    at_w = D - width
    n_heads = at_w // HEAD_DIM
    idx_dim = cache_idx_k.shape[-1]
    ih = (w_in.shape[2] - rw_proj - 3 * at_w - idx_dim) // (idx_dim + 1)
    xw = w_cq.shape[2]
    x_heads = cache_mem_k.shape[3]
    n_mem = mem_prompt.shape[1]
    n_experts = w_e1.shape[1]
    top_k = 2
    tn = 512
    n_main = 3 * width
    n_lora = rw_proj - n_main
    n_att = 3 * at_w + ih * idx_dim
    n_tail = n_lora + idx_dim + ih
    assert n_main % tn == 0 and at_w % tn == 0 and (ih * idx_dim) % tn == 0 and n_tail <= tn

    w_t = w_in[l].T
    w_all = jnp.concatenate([w_t[:n_main], w_t[rw_proj:rw_proj + at_w] * HEAD_DIM ** -0.5,
                             w_t[rw_proj + at_w:rw_proj + n_att], w_t[n_main:rw_proj],
                             w_t[rw_proj + n_att:], jnp.zeros((tn - n_tail, D), w_t.dtype)], axis=0).astype(BF16)
    flat, per_head = False, True
    segments = [(n_main // tn, [(flat, F32)]),
                (at_w // tn, [(flat, BF16)]),
                (at_w // tn, [(per_head, F32), (flat, BF16)]),
                (at_w // tn, [(per_head, F32), (flat, BF16)]),
                (ih * idx_dim // tn, [(flat, BF16)]),
                (1, [(flat, F32)])]
    w_out_b = w_out[l].astype(BF16)
    w_cq_b, w_co_b = w_cq[l].astype(BF16), w_co[l].astype(BF16)
    w_ckv = jnp.concatenate([w_ck[l], w_cv[l]], axis=1).astype(BF16)
    n_route = w_rg.shape[2] + w_re.shape[2]
    w_r = jnp.pad(jnp.concatenate([w_rg[l], w_re[l]], axis=1), ((0, 0), (0, LANES - n_route)))
    b_r = jnp.pad(jnp.concatenate([b_rg[l], b_re[l]]), (0, LANES - n_route)).reshape(1, LANES)
    rw_args = (mu_shift[l], rw_w0[l], rw_w2[l], rw_a0[l], rw_a2[l], rw_g2[l], rw_kk[l], rw_ka[l],
               rw_rk[l].reshape(-1))
    tiles = bias_tables(rel_bias, ((0, False), (Q_BLOCK, False), (0, True), (Q_BLOCK, True)))

    def project(x2d, b_, t_, tm):
        f_main, q, k_heads, k, v_heads, v, iq, f_tail = norm_matmul_split(x2d, g_mix[l], w_all, segments,
                                                                          tm=tm, tn=tn)
        r3 = lambda z: z.reshape(b_, t_, -1)
        f_main, f_tail = r3(f_main), r3(f_tail)
        ik = f_tail[..., n_lora:n_lora + idx_dim]
        iw = f_tail[..., n_lora + idx_dim:n_tail]
        shift = jnp.concatenate([f_main[:, -1], f_tail[:, -1, :n_lora]], axis=-1)
        heads5 = lambda z: z.reshape(1, b_, t_, n_heads, HEAD_DIM)
        return f_main, r3(q), heads5(k_heads), r3(k), heads5(v_heads), r3(v), r3(iq), f_tail, ik, iw, shift

    def rw_rows(y):
        b_, p_, t_, _ = y.shape
        return y.transpose(0, 2, 1, 3).reshape(b_ * t_, p_ * PAIR).astype(BF16)

    xp = x_prompt.reshape(B * S, D)
    fm_p, q, k_p, kb_p, v_p, vb_p, iq, ft_p, ik_p, iw, shift_p = project(xp, B, S, 1024)
    feats_p = rwkv_prep(fm_p, ft_p, jnp.zeros((B, rw_proj), F32), *rw_args, tm=256, width=width)
    rw_p, st_p = rwkv_chunk(feats_p, jnp.zeros((B, width // PAIR, PAIR, PAIR), F32), rw_ln_g[l], rw_ln_b[l], L=64)
    tr = lambda z: z.transpose(0, 2, 1)
    k_pairs = kb_p.reshape(B, S, n_heads // 2, PAIR).transpose(0, 2, 1, 3)
    at_p = dsa_prompt(ik_p.astype(BF16), tr(iq), tr(iw * (ih ** -0.5 * idx_dim ** -0.5))[:, :, None, :],
                      k_pairs, tr(q), tr(vb_p), tiles[2:4], topk=topk_p)
    x1_p = matmul_residual(xp, rw_rows(rw_p), tr(at_p).reshape(B * S, at_w), w_out_b, tm=512, tn=512)
    mkv = norm_matmul(mem_prompt.reshape(B * n_mem, D), g_mem[l], w_ckv, tm=256, tn=512)
    mk_p = mkv[:, :xw].reshape(B, n_mem, xw)
    mv_p = mkv[:, xw:].reshape(B, n_mem, xw)
    x2_p = cross_attn(x1_p, g_cross[l], w_cq_b, mk_p.astype(BF16), mv_p.astype(BF16), w_co_b,
                      groups=1, t_rows=512, seq_tiles=S // 512, n_heads=x_heads)

    xs = x_sample.reshape(DB * T, D)
    fm_s, q2, k_s, kb_s, v_s, vb_s, iq2, ft_s, ik_s, iw2, shift_s = project(xs, DB, T, DB * T)
    feats_s = rwkv_prep(fm_s, ft_s, state_shift[l], *rw_args, tm=T, width=width)
    rw_s, st_s = rwkv_chunk(feats_s, _state_to_block_diag(state_wkv[l]), rw_ln_g[l], rw_ln_b[l], L=T)
    r4 = lambda z, h_: z.reshape(DB, T, h_, -1)
    sel_args, att_args = _dsa_sample_inputs(r4(q2, n_heads), r4(kb_s, n_heads), r4(vb_s, n_heads), r4(iq2, ih),
                                            ik_s, iw2, tiles[0:2])
    mask_s = dsa_sample_select(page_table, *sel_args, _token_minor_cache(cache_idx_k), layer=l,
                               n_idx_heads=ih, t_new=T, topk=topk_s, group=4)
    at_s = dsa_sample(page_table, mask_s, *att_args, _token_minor_cache(cache_k), _token_minor_cache(cache_v),
                      layer=l, n_heads=n_heads, t_new=T, chunk=8)
    x1_s = matmul_residual(xs, rw_rows(rw_s), at_s.reshape(DB * T, at_w), w_out_b, tm=DB * T, tn=512)
    x2_s = cross_attn(x1_s, g_cross[l], w_cq_b, cache_mem_k[l].reshape(DB, n_mem, xw).astype(BF16),
                      cache_mem_v[l].reshape(DB, n_mem, xw).astype(BF16), w_co_b,
                      groups=8, t_rows=T, seq_tiles=1, n_heads=x_heads)

    h_p, route_p = router(x2_p, g_ffn[l], w_r, b_r, tm=512, n_groups=w_rg.shape[2],
                          per_group=w_re.shape[2] // w_rg.shape[2])
    h_s, route_s = router(x2_s, g_ffn[l], w_r, b_r, tm=DB * T, n_groups=w_rg.shape[2],
                          per_group=w_re.shape[2] // w_rg.shape[2])
    h_all = jnp.concatenate([h_p, h_s], axis=0)
    eids = jnp.concatenate([route_p[:, top_k:2 * top_k], route_s[:, top_k:2 * top_k]], axis=0).astype(I32)
    tm_moe = DB * T
    n_p, n_all = B * S, B * S + DB * T
    assert n_p % tm_moe == 0
    tile_expert, tile_rows, row_tok, row_dst = _moe_dispatch(eids, n_experts, tm_moe)
    y_slabs = moe_ffn(tile_expert, tile_rows, row_tok, row_dst, h_all, n_all * top_k,
                      w_e1[l], w_e3[l], w_e2[l], tm=tm_moe)
    y_p = moe_combine(x2_p, route_p, g_final, y_slabs, tm=tm_moe, top_k=top_k, tile_offset=0)
    y_s = moe_combine(x2_s, route_s, g_final, y_slabs, tm=tm_moe, top_k=top_k, tile_offset=n_p // tm_moe)

    return (y_p.reshape(B, S, D), y_s.reshape(DB, T, D),
            k_p, v_p, ik_p[None], _state_from_block_diag(st_p)[None], shift_p[None],
            mk_p.reshape(1, B, n_mem, x_heads, xw // x_heads), mv_p.reshape(1, B, n_mem, x_heads, xw // x_heads),
            k_s, v_s, ik_s[None], _state_from_block_diag(st_s)[None], shift_s[None])
```

```python
import functools
import math

import jax
import jax.numpy as jnp
from jax import lax
from jax.experimental import pallas as pl
from jax.experimental.pallas import tpu as pltpu

F32 = jnp.float32
BF16 = jnp.bfloat16
I32 = jnp.int32

LANES = 128
SUBLANES = 8
VMEM_LIMIT_BYTES = 56 * 1024 * 1024

HEAD_DIM = 64
PAIR = 2 * HEAD_DIM
GN_EPS = 64e-5
NORM_EPS = 1e-6
TOPK_MAX = 256
Q_BLOCK = 128
N_BUCKETS = 32
MAX_DISTANCE = 128
PAGE_SIZE = 128
N_GROUPS = 4
EXPERTS_PER_GROUP = 8
INT_MIN = -(2 ** 31)

PROJ_ROWS = 1024
PROJ_COLS = 512
PREP_ROWS = 256
RWKV_CHUNK = 64
ROW_TILE = 512
MEM_ROWS = 256
SAMPLE_CHUNK_PAGES = 8
SELECT_GROUP = 8
CROSS_GROUP = 8


def _cparams(*sem):
    return pltpu.CompilerParams(dimension_semantics=sem, vmem_limit_bytes=VMEM_LIMIT_BYTES)


def _mm(a, b):
    return jnp.dot(a.astype(BF16), b.astype(BF16), preferred_element_type=F32)


def _mm_nt(a, b):
    return lax.dot_general(a.astype(BF16), b.astype(BF16), (((1,), (1,)), ((), ())),
                           preferred_element_type=F32)


def _split2(x):
    hi = x.astype(BF16)
    lo = (x - hi.astype(F32)).astype(BF16)
    return hi, lo


def _mm3(a, b):
    ah, al = _split2(a)
    bh, bl = _split2(b)
    d = lambda x, y: jnp.dot(x, y, preferred_element_type=F32)
    return d(ah, bh) + (d(ah, bl) + d(al, bh))


def _mm_exact_rhs(a, b_bf16):
    hi = a.astype(BF16)
    r1 = a - hi.astype(F32)
    mid = r1.astype(BF16)
    lo = (r1 - mid.astype(F32)).astype(BF16)
    d = lambda x: jnp.dot(x, b_bf16, preferred_element_type=F32)
    return d(hi) + (d(mid) + d(lo))


def _rmsnorm(x, g):
    ms = jnp.mean(x * x, axis=-1, keepdims=True)
    return x * lax.rsqrt(ms + NORM_EPS) * g


def _norm_matmul_body(x_ref, g_ref, w_ref, o_ref, xn_ref):
    @pl.when(pl.program_id(1) == 0)
    def _():
        xn_ref[...] = _rmsnorm(x_ref[...], g_ref[...]).astype(BF16)

    o_ref[...] = _mm(xn_ref[...], w_ref[...])


def norm_matmul(x, g, w, *, tm, tn):
    n, d = x.shape
    m = w.shape[1]
    return pl.pallas_call(
        _norm_matmul_body,
        grid=(n // tm, m // tn),
        in_specs=[pl.BlockSpec((tm, d), lambda i, j: (i, 0)),
                  pl.BlockSpec((1, d), lambda i, j: (0, 0)),
                  pl.BlockSpec((d, tn), lambda i, j: (0, j))],
        out_specs=pl.BlockSpec((tm, tn), lambda i, j: (i, j)),
        out_shape=jax.ShapeDtypeStruct((n, m), F32),
        scratch_shapes=[pltpu.VMEM((tm, d), BF16)],
        compiler_params=_cparams("parallel", "arbitrary"),
        name="norm_matmul",
    )(x, g.reshape(1, d), w)


def _norm_matmul_split_body(x_ref, g_ref, w_ref, *rest, bounds):
    o_refs, xn_ref = rest[:-1], rest[-1]
    j = pl.program_id(1)

    @pl.when(j == 0)
    def _():
        xn_ref[...] = _rmsnorm(x_ref[...], g_ref[...]).astype(BF16)

    res = _mm_nt(xn_ref[...], w_ref[...])
    tm, tn = res.shape
    for o_ref, (lo, hi) in zip(o_refs, bounds):
        @pl.when((j >= lo) & (j < hi))
        def _(o_ref=o_ref, lo=lo, hi=hi):
            if o_ref.shape[1] == HEAD_DIM:
                n_heads = (hi - lo) * tn // HEAD_DIM
                per_tile = tn // HEAD_DIM
                for c in range(hi - lo):
                    @pl.when(j == lo + c)
                    def _(c=c):
                        for hh in range(per_tile):
                            o_ref[pl.ds(c * per_tile + hh, tm, stride=n_heads), :] = (
                                res[:, hh * HEAD_DIM:(hh + 1) * HEAD_DIM].astype(o_ref.dtype))
            else:
                o_ref[...] = res.astype(o_ref.dtype)


def norm_matmul_split(x, g, wt, segments, *, tm, tn):
    n, d = x.shape
    m = wt.shape[0]
    bounds, out_specs, out_shape, lo = [], [], [], 0
    for nt, outs in segments:
        hi = lo + nt
        for per_head, dt in outs:
            bounds.append((lo, hi))
            if per_head:
                heads = nt * tn // HEAD_DIM
                out_specs.append(pl.BlockSpec((tm * heads, HEAD_DIM), lambda i, j: (i, 0),
                                              pipeline_mode=pl.Buffered(1)))
                out_shape.append(jax.ShapeDtypeStruct((n * heads, HEAD_DIM), dt))
            else:
                out_specs.append(pl.BlockSpec((tm, tn),
                                              lambda i, j, lo=lo, hi=hi: (i, jnp.clip(j - lo, 0, hi - lo - 1))))
                out_shape.append(jax.ShapeDtypeStruct((n, nt * tn), dt))
        lo = hi
    assert lo * tn == m
    return pl.pallas_call(
        functools.partial(_norm_matmul_split_body, bounds=tuple(bounds)),
        grid=(n // tm, m // tn),
        in_specs=[pl.BlockSpec((tm, d), lambda i, j: (i, 0), pipeline_mode=pl.Buffered(1)),
                  pl.BlockSpec((1, d), lambda i, j: (0, 0)),
                  pl.BlockSpec((tn, d), lambda i, j: (j, 0))],
        out_specs=out_specs,
        out_shape=out_shape,
        scratch_shapes=[pltpu.VMEM((tm, d), BF16)],
        compiler_params=_cparams("arbitrary", "arbitrary"),
        name="norm_matmul_split",
    )(x, g.reshape(1, d), wt)


def _pair_ones():
    r = lax.broadcasted_iota(I32, (PAIR, PAIR), 0) // HEAD_DIM
    c = lax.broadcasted_iota(I32, (PAIR, PAIR), 1) // HEAD_DIM
    return (r == c).astype(BF16)


def _head_sum(x, ones_bd):
    return _mm_exact_rhs(x, ones_bd)


def _rwkv_prep_body(f_ref, prev8_ref, init_ref, mu_ref, ft_ref, tprev8_ref, tinit_ref, tmu_ref,
                    w0_ref, w2_ref, a0_ref, a2_ref, g2_ref, kk_ref, ka_ref, rk_ref,
                    r_o, k_o, v_o, kk_o, b_o, ld_o, g_o, bon_o, *, tm, width):
    i = pl.program_id(1)

    def token_shift(f, p8_ref, i_ref, m_ref):
        nc = m_ref.shape[1]
        prev_row = jnp.where(i == 0, i_ref[0], p8_ref[0, SUBLANES - 1:SUBLANES, :nc])
        rolled = pltpu.roll(f, shift=1, axis=0)
        row = lax.broadcasted_iota(I32, f.shape, 0)
        f_prev = jnp.where(row == 0, prev_row, rolled)
        return f + (f_prev - f) * m_ref[...]

    fs = token_shift(f_ref[0], prev8_ref, init_ref, mu_ref)
    n_dec = w2_ref.shape[0]
    n_icl = a2_ref.shape[0]
    n_lora = tmu_ref.shape[1]
    ts = token_shift(ft_ref[0][:, :n_lora], tprev8_ref, tinit_ref, tmu_ref)
    w_ = width
    r = fs[:, 0:w_]
    k = fs[:, w_:2 * w_]
    v = fs[:, 2 * w_:3 * w_]
    wd = ts[:, 0:n_dec]
    ad = ts[:, n_dec:n_dec + n_icl]
    gd = ts[:, n_dec + n_icl:]
    z = w0_ref[...] + _mm3(jnp.tanh(wd), w2_ref[...])
    nz = -z
    softplus = jnp.maximum(nz, 0.0) + jnp.log(1.0 + jnp.exp(-jnp.abs(nz)))
    w = -softplus - 0.5
    ld = -jnp.exp(w)
    a = 1.0 / (1.0 + jnp.exp(-(a0_ref[...] + _mm3(ad, a2_ref[...]))))
    g = _mm3(1.0 / (1.0 + jnp.exp(-gd)), g2_ref[...])
    kk = k * kk_ref[...]
    k2 = k * (1.0 + (a - 1.0) * ka_ref[...])
    rk = r * k2 * rk_ref[...]
    ones_bd = _pair_ones()
    for p in range(w_ // PAIR):
        sl = slice(p * PAIR, (p + 1) * PAIR)
        kkp = kk[:, sl]
        nrm = jnp.sqrt(_head_sum(kkp * kkp, ones_bd))
        kkp = kkp / jnp.maximum(nrm, 1e-12)
        ap = a[:, sl]
        r_o[0, p] = r[:, sl]
        k_o[0, p] = k2[:, sl]
        v_o[0, p] = v[:, sl]
        kk_o[0, p] = kkp
        b_o[0, p] = kkp * ap
        ld_o[0, p] = ld[:, sl]
        g_o[0, p] = g[:, sl]
        bon_o[0, p] = _head_sum(rk[:, sl], ones_bd) * v[:, sl]


def rwkv_prep(f_main, f_tail, init_prev, mu, w0, w2, a0, a2, g2, k_k, k_a, r_k, *, tm, width):
    b, t, n_main = f_main.shape
    n_tail = f_tail.shape[2]
    n_lora = mu.shape[0] - n_main
    npair = width // PAIR
    row1 = lambda x: x.reshape(1, -1)
    kern = functools.partial(_rwkv_prep_body, tm=tm, width=width)
    full = lambda a: pl.BlockSpec(a.shape, lambda bi, i: (0,) * a.ndim)
    args = [row1(w0), w2, row1(a0), a2, g2, row1(k_k), row1(k_a), row1(r_k)]
    out_spec = pl.BlockSpec((1, npair, tm, PAIR), lambda bi, i: (bi, 0, i, 0))
    out_shape = jax.ShapeDtypeStruct((b, npair, t, PAIR), F32)
    prev8_map = lambda bi, i: (bi, jnp.maximum(i * (tm // SUBLANES) - 1, 0), 0)

    def feature_specs(ncols, n_init):
        return [pl.BlockSpec((1, tm, ncols), lambda bi, i: (bi, i, 0)),
                pl.BlockSpec((1, SUBLANES, ncols), prev8_map),
                pl.BlockSpec((1, 1, n_init), lambda bi, i: (bi, 0, 0)),
                pl.BlockSpec((1, n_init), lambda bi, i: (0, 0))]

    return pl.pallas_call(
        kern,
        grid=(b, t // tm),
        in_specs=feature_specs(n_main, n_main) + feature_specs(n_tail, n_lora) + [full(a) for a in args],
        out_specs=[out_spec] * 8,
        out_shape=[out_shape] * 8,
        compiler_params=_cparams("parallel", "parallel"),
        name="rwkv_prep",
    )(f_main, f_main, init_prev[:, :n_main].reshape(b, 1, n_main), row1(mu[:n_main]),
      f_tail, f_tail, init_prev[:, n_main:].reshape(b, 1, n_lora), row1(mu[n_main:]), *args)


def _rwkv_chunk_body(r_ref, k_ref, v_ref, kk_ref, b_ref, ld_ref, g_ref, bon_ref, s0_ref,
                     lng_ref, lnb_ref, o_ref, st_ref, s_ref, *, L, npair, group):
    c = pl.program_id(1)

    @pl.when(c == 0)
    def _():
        s_ref[...] = s0_ref[0]

    L2 = 2 * L
    row = lax.broadcasted_iota(I32, (L2, L2), 0)
    col = lax.broadcasted_iota(I32, (L2, L2), 1)
    eye = (row == col).astype(F32)
    row4 = lax.broadcasted_iota(I32, (2 * L2, 2 * L2), 0)
    col4 = lax.broadcasted_iota(I32, (2 * L2, 2 * L2), 1)
    rr, cc = row4 % L2, col4 % L2
    tri_all = ((rr // L) == (cc // L)) & ((cc < rr) | ((row4 >= L2) & (cc == rr)))
    tr = lax.broadcasted_iota(I32, (L, L), 0)
    tc = lax.broadcasted_iota(I32, (L, L), 1)
    cum_mat = (tc <= tr).astype(BF16)
    lane = lax.broadcasted_iota(I32, (L, PAIR), 1)
    first = lane < HEAD_DIM
    ones_bd = _pair_ones()
    n_sq = max(int(math.ceil(math.log2(L))) - 1, 0)

    def block_diag(x):
        return jnp.concatenate([jnp.where(first, x, 0.0), jnp.where(first, 0.0, x)], axis=0)

    def group_step(gi, carry):
        ps = [gi * group + j for j in range(group)]
        each = lambda f, *cols: [f(*args) for args in zip(*cols)]
        ld = [ld_ref[0, p] for p in ps]
        cum = each(lambda x: _mm_exact_rhs_t(cum_mat, x), ld)
        dec = each(jnp.exp, cum)
        dec_inv = each(lambda c_: jnp.exp(-c_), cum)
        a_t = each(lambda p, c_, l_: block_diag(-kk_ref[0, p] * jnp.exp(c_ - l_)), ps, cum, ld)
        b_t = each(lambda p, e: block_diag(b_ref[0, p] * e), ps, dec_inv)
        k_t = each(lambda p, e: block_diag(k_ref[0, p] * e), ps, dec_inv)
        r_t = each(lambda p, e: block_diag(r_ref[0, p] * e), ps, dec)
        v_b = each(lambda p: block_diag(v_ref[0, p]), ps)
        ar = each(lambda a, r: jnp.concatenate([a, r], axis=0), a_t, r_t)
        bk = each(lambda b, k: jnp.concatenate([b, k], axis=0), b_t, k_t)
        cross = each(lambda x_, y_: jnp.where(tri_all, _mm_nt(x_, y_), 0.0), ar, bk)
        a_ab = [c_[:L2, :L2] for c_ in cross]
        a_rb = [c_[L2:, :L2] for c_ in cross]
        akrk = [c_[:, L2:] for c_ in cross]
        t_inv = each(lambda a: eye + a, a_ab)
        if n_sq >= 1:
            x = each(lambda a: _mm(a, a), a_ab)
            for _ in range(n_sq - 1):
                xt = each(lambda x_, t: _mm(x_, jnp.concatenate([x_, t], axis=1)), x, t_inv)
                x = [z[:, :L2] for z in xt]
                t_inv = each(lambda t, z: t + z[:, L2:], t_inv, xt)
            t_inv = each(lambda t, x_: t + _mm(x_, t), t_inv, x)
        s = [s_ref[p] for p in ps]
        ar_s = each(_mm_nt, ar, s)
        akrk_v = each(_mm, akrk, v_b)
        u = each(lambda t, p1, p2: _mm(t, p1[:L2] + p2[:L2]), t_inv, ar_s, akrk_v)
        y_b = each(lambda p1, rb, u_, p2: p1[L2:] + _mm(rb, u_) + p2[L2:], ar_s, a_rb, u, akrk_v)
        s_new = each(lambda s_, u_, b, v, k, d: (s_ + _mm(u_.T, b) + _mm(v.T, k)) * d[L - 1:L, :],
                     s, u, b_t, v_b, k_t, dec)
        for p, sn in zip(ps, s_new):
            s_ref[p] = sn
        y = each(lambda yb: yb[:L] + yb[L:], y_b)
        mean = each(lambda y_: _head_sum(y_, ones_bd) * (1.0 / HEAD_DIM), y)
        d = each(lambda y_, m: y_ - m, y, mean)
        var = each(lambda d_: _head_sum(d_ * d_, ones_bd) * (1.0 / HEAD_DIM), d)
        for p, d_, v_ in zip(ps, d, var):
            yn = d_ * lax.rsqrt(v_ + GN_EPS) * lng_ref[p] + lnb_ref[p]
            o_ref[0, p] = (yn + bon_ref[0, p]) * g_ref[0, p]
        return carry

    lax.fori_loop(0, npair // group, group_step, 0)

    @pl.when(c == pl.num_programs(1) - 1)
    def _():
        st_ref[0] = s_ref[...]


def _mm_exact_rhs_t(m_bf16, x):
    hi = x.astype(BF16)
    r1 = x - hi.astype(F32)
    mid = r1.astype(BF16)
    lo = (r1 - mid.astype(F32)).astype(BF16)
    d = lambda y: jnp.dot(m_bf16, y, preferred_element_type=F32)
    return d(hi) + (d(mid) + d(lo))


def rwkv_chunk(feats, s0_bd, ln_g, ln_b, *, L, group=8):
    b, npair, t, _ = feats[0].shape
    blk = pl.BlockSpec((1, npair, L, PAIR), lambda bi, c: (bi, 0, c, 0))
    st_spec = pl.BlockSpec((1, npair, PAIR, PAIR), lambda bi, c: (bi, 0, 0, 0))
    par_spec = pl.BlockSpec((npair, 1, PAIR), lambda bi, c: (0, 0, 0))
    kern = functools.partial(_rwkv_chunk_body, L=L, npair=npair, group=group)
    return pl.pallas_call(
        kern,
        grid=(b, t // L),
        in_specs=[blk] * 8 + [st_spec, par_spec, par_spec],
        out_specs=[blk, st_spec],
        out_shape=[jax.ShapeDtypeStruct((b, npair, t, PAIR), F32),
                   jax.ShapeDtypeStruct((b, npair, PAIR, PAIR), F32)],
        scratch_shapes=[pltpu.VMEM((npair, PAIR, PAIR), F32)],
        compiler_params=_cparams("parallel", "arbitrary"),
        name="rwkv_chunk",
    )(*feats, s0_bd, ln_g.reshape(npair, 1, PAIR), ln_b.reshape(npair, 1, PAIR))


def _state_to_block_diag(s):
    b, h, n, _ = s.shape
    s = s.reshape(b, h // 2, 2, n, n)
    z = jnp.zeros_like(s[:, :, 0])
    top = jnp.concatenate([s[:, :, 0], z], axis=-1)
    bot = jnp.concatenate([z, s[:, :, 1]], axis=-1)
    return jnp.concatenate([top, bot], axis=-2)


def _state_from_block_diag(s_bd):
    b, p, _, _ = s_bd.shape
    n = HEAD_DIM
    return jnp.stack([s_bd[:, :, :n, :n], s_bd[:, :, n:, n:]], axis=2).reshape(b, 2 * p, n, n)


def _t5_bucket(dist):
    exact = N_BUCKETS // 2
    d = jnp.maximum(dist, 0)
    far = exact + (jnp.log(jnp.maximum(d, 1).astype(F32) / exact) / math.log(MAX_DISTANCE / exact)
                   * (N_BUCKETS - exact)).astype(I32)
    return jnp.where(d < exact, d, jnp.minimum(far, N_BUCKETS - 1))


def _bias_tables_body(rb_ref, o_ref, *, offsets, n_heads):
    r = lax.broadcasted_iota(I32, (Q_BLOCK, Q_BLOCK), 0)
    c = lax.broadcasted_iota(I32, (Q_BLOCK, Q_BLOCK), 1)
    for t, (off, key_major) in enumerate(offsets):
        bucket = _t5_bucket((c - r if key_major else r - c) + off)
        for h in range(n_heads):
            def body(bk, acc):
                return jnp.where(bucket == bk, rb_ref[bk, h], acc)
            tile = lax.fori_loop(0, N_BUCKETS, body, jnp.zeros((Q_BLOCK, Q_BLOCK), F32))
            o_ref[t, h] = tile - rb_ref[N_BUCKETS - 1, h]


def bias_tables(rel_bias, offsets):
    n_heads = rel_bias.shape[1]
    kern = functools.partial(_bias_tables_body, offsets=tuple(offsets), n_heads=n_heads)
    return pl.pallas_call(
        kern,
        in_specs=[pl.BlockSpec(memory_space=pltpu.SMEM)],
        out_specs=pl.BlockSpec(memory_space=pltpu.VMEM),
        out_shape=jax.ShapeDtypeStruct((len(offsets), n_heads, Q_BLOCK, Q_BLOCK), F32),
        name="bias_tables",
    )(rel_bias)


def _sortable_key(scores):
    bits = lax.bitcast_convert_type(scores + 0.0, I32)
    return jnp.where(bits < 0, bits ^ 0x7FFFFFFF, bits)


def _count(mask):
    return jnp.sum(mask.astype(F32), axis=-1, keepdims=True)


def _topk_select(key, topk, n_index_bits):
    rows, n = key.shape
    kf = float(topk)
    t0 = jnp.where(_count(key >= 0) >= kf, 0, INT_MIN).astype(I32)

    def value_bit(i, t):
        cand = t + lax.shift_left(jnp.int32(1), 30 - i)
        return jnp.where(_count(key >= cand) >= kf, cand, t)

    thr = lax.fori_loop(0, 31, value_bit, t0)
    above = key > thr
    ties = key == thr
    need = kf - _count(above)
    idx = lax.broadcasted_iota(I32, (rows, n), 1)

    def lowest_ties():
        def index_bit(i, m):
            cand = m + lax.shift_left(jnp.int32(1), n_index_bits - 1 - i)
            return jnp.where(_count(ties & (idx < cand)) <= need, cand, m)
        return lax.fori_loop(0, n_index_bits, index_bit, jnp.zeros((rows, 1), I32))

    surplus = jnp.max(_count(ties) - need) > 0.0
    m = lax.cond(surplus, lowest_ties, lambda: jnp.full((rows, 1), 2 ** n_index_bits, I32))
    return above | (ties & (idx < m))


SUM_CHAINS = 4
HEAD_GROUP = 4


def _sum_rows(x):
    r = x.shape[0]
    if r % (SUM_CHAINS * SUBLANES) == 0 and r > SUM_CHAINS * SUBLANES:
        x = jnp.sum(x.reshape(SUM_CHAINS, r // SUM_CHAINS, x.shape[1]), axis=1)
    return jnp.sum(x, axis=0, keepdims=True)


def _max_rows(x):
    r = x.shape[0]
    if r % (SUM_CHAINS * SUBLANES) == 0 and r > SUM_CHAINS * SUBLANES:
        x = jnp.max(x.reshape(SUM_CHAINS, r // SUM_CHAINS, x.shape[1]), axis=1)
    return jnp.max(x, axis=0, keepdims=True)


def _topk_select_cols(key, topk, n_index_bits):
    n, cols = key.shape
    kf = float(topk)
    cnt = lambda m: _sum_rows(m.astype(F32))
    t0 = jnp.where(cnt(key >= 0) >= kf, 0, INT_MIN).astype(I32)

    def value_bit(i, t):
        cand = t + lax.shift_left(jnp.int32(1), 30 - i)
        return jnp.where(cnt(key >= cand) >= kf, cand, t)

    thr = lax.fori_loop(0, 31, value_bit, t0)
    above = key > thr
    ties = key == thr
    need = kf - cnt(above)
    idx = lax.broadcasted_iota(I32, (n, cols), 0)

    def lowest_ties():
        def index_bit(i, m):
            cand = m + lax.shift_left(jnp.int32(1), n_index_bits - 1 - i)
            return jnp.where(cnt(ties & (idx < cand)) <= need, cand, m)
        return lax.fori_loop(0, n_index_bits, index_bit, jnp.zeros((1, cols), I32))

    surplus = jnp.max(cnt(ties) - need) > 0.0
    m = lax.cond(surplus, lowest_ties, lambda: jnp.full((1, cols), 2 ** n_index_bits, I32))
    return above | (ties & (idx < m))


def _dsa_prompt_block(nb, ik_ref, iqt_ref, wt_ref, k_ref, qt_ref, vt_ref, bias_ref, o_ref, mask_ref,
                      *, n_heads, n_idx_heads, topk):
    w = nb * Q_BLOCK
    ik = ik_ref[0, :w, :]
    idx_dim = ik.shape[1]

    def head_rows(ref, h, dh, cols=slice(None)):
        return ref[0, pl.ds(pl.multiple_of(h * dh, dh), dh), cols]

    def idx_heads(gi, acc):
        hs = [gi * HEAD_GROUP + j for j in range(HEAD_GROUP)]
        dots = [jnp.dot(ik, head_rows(iqt_ref, h, idx_dim), preferred_element_type=F32) for h in hs]
        terms = [jnp.maximum(d, 0.0) * wt_ref[0, h] for d, h in zip(dots, hs)]
        while len(terms) > 1:
            terms = [a + b for a, b in zip(terms[0::2], terms[1::2])]
        return acc + terms[0]

    scores = lax.fori_loop(0, n_idx_heads // HEAD_GROUP, idx_heads, jnp.zeros((w, Q_BLOCK), F32))
    kpos = lax.broadcasted_iota(I32, (w, Q_BLOCK), 0)
    qpos = (nb - 1) * Q_BLOCK + lax.broadcasted_iota(I32, (w, Q_BLOCK), 1)
    valid = kpos <= qpos
    if w <= topk:
        sel = valid
    else:
        key = jnp.where(valid, _sortable_key(scores), INT_MIN)
        sel = valid & _topk_select_cols(key, topk, int(math.ceil(math.log2(w))) + 1)
    mask_ref[:w, :] = jnp.where(sel, 0.0, -jnp.inf)

    def with_near_bias(logits, h):
        near = [logits[w - Q_BLOCK:] + bias_ref[0, h]]
        if nb >= 2:
            near = [logits[w - 2 * Q_BLOCK:w - Q_BLOCK] + bias_ref[1, h]] + near
        if nb >= 3:
            near = [logits[:w - 2 * Q_BLOCK]] + near
        return jnp.concatenate(near, axis=0) if len(near) > 1 else near[0]

    pair_row_half = lax.broadcasted_iota(I32, (PAIR, Q_BLOCK), 0) // HEAD_DIM

    def head_logits(gi, j):
        p = gi * (HEAD_GROUP // 2) + j // 2
        qt_pair = head_rows(qt_ref, p, PAIR)
        qt_head = jnp.where(pair_row_half == j % 2, qt_pair, jnp.zeros_like(qt_pair))
        return jnp.dot(k_ref[0, p, :w, :], qt_head, preferred_element_type=F32)

    def attn_heads(gi, carry):
        hs = [gi * HEAD_GROUP + j for j in range(HEAD_GROUP)]
        mask = mask_ref[:w, :]
        logits = [head_logits(gi, j) + mask for j in range(HEAD_GROUP)]
        logits = [with_near_bias(l_, h) for l_, h in zip(logits, hs)]
        mx = [_max_rows(l_) for l_ in logits]
        e = [jnp.exp(l_ - m_) for l_, m_ in zip(logits, mx)]
        den = [_sum_rows(e_) for e_ in e]
        o = [jnp.dot(head_rows(vt_ref, h, HEAD_DIM, slice(0, w)), e_.astype(BF16),
                     preferred_element_type=F32) for h, e_ in zip(hs, e)]
        for h, o_, d_ in zip(hs, o, den):
            o_ref[0, pl.ds(pl.multiple_of(h * HEAD_DIM, HEAD_DIM), HEAD_DIM), :] = (o_ / d_).astype(o_ref.dtype)
        return carry

    lax.fori_loop(0, n_heads // HEAD_GROUP, attn_heads, 0)


def _dsa_prompt_body(ik_ref, iqt_ref, wt_ref, k_ref, qt_ref, vt_ref, bias_ref, o_ref, mask_ref,
                     *, n_heads, n_idx_heads, seq, topk):
    i = pl.program_id(0)
    for nb in range(1, seq // Q_BLOCK + 1):
        @pl.when(i == nb - 1)
        def _(nb=nb):
            _dsa_prompt_block(nb, ik_ref, iqt_ref, wt_ref, k_ref, qt_ref, vt_ref, bias_ref, o_ref, mask_ref,
                              n_heads=n_heads, n_idx_heads=n_idx_heads, topk=topk)


def dsa_prompt(ik, iqt, wts, k_pairs, qt, vt, bias_tiles, *, topk):
    b, width, s = qt.shape
    h = width // HEAD_DIM
    ih = wts.shape[1]
    kern = functools.partial(_dsa_prompt_body, n_heads=h, n_idx_heads=ih, seq=s, topk=topk)
    grid_spec = pltpu.PrefetchScalarGridSpec(
        num_scalar_prefetch=0,
        grid=(s // Q_BLOCK, b),
        in_specs=[pl.BlockSpec((1, s, ik.shape[2]), lambda i, bi: (bi, 0, 0)),
                  pl.BlockSpec((1, iqt.shape[1], Q_BLOCK), lambda i, bi: (bi, 0, i)),
                  pl.BlockSpec((1, ih, 1, Q_BLOCK), lambda i, bi: (bi, 0, 0, i)),
                  pl.BlockSpec((1, h // 2, s, PAIR), lambda i, bi: (bi, 0, 0, 0)),
                  pl.BlockSpec((1, width, Q_BLOCK), lambda i, bi: (bi, 0, i)),
                  pl.BlockSpec((1, width, s), lambda i, bi: (bi, 0, 0)),
                  pl.BlockSpec(bias_tiles.shape, lambda i, bi: (0, 0, 0, 0))],
        out_specs=pl.BlockSpec((1, width, Q_BLOCK), lambda i, bi: (bi, 0, i)),
        scratch_shapes=[pltpu.VMEM((s, Q_BLOCK), F32)],
    )
    return pl.pallas_call(
        kern,
        grid_spec=grid_spec,
        out_shape=jax.ShapeDtypeStruct((b, width, s), BF16),
        compiler_params=_cparams("arbitrary", "arbitrary"),
        name="dsa_prompt",
    )(ik, iqt, wts, k_pairs, qt, vt, bias_tiles)


def _dsa_select_body(pt_ref, iq_ref, wt_ref, iknew_ref, cidx_hbm, o_ref, ikbuf, sem,
                     *, layer, n_pages, group, n_idx_heads, t_new, topk):
    s = pl.program_id(0)
    past = n_pages * PAGE_SIZE
    n_keys = past + PAGE_SIZE

    def ik_copy(i):
        g = i // n_pages
        p = i % n_pages
        page = pt_ref[(s * group + g) * n_pages + p]
        return pltpu.make_async_copy(cidx_hbm.at[layer, page], ikbuf.at[g, p], sem)

    def ik_start(i, carry):
        ik_copy(i).start()
        return carry

    def ik_wait(i, carry):
        ik_copy(i).wait()
        return carry

    lax.fori_loop(0, group * n_pages, ik_start, 0)
    for g in range(group):
        ikbuf[g, n_pages] = iknew_ref[g]
    lax.fori_loop(0, group * n_pages, ik_wait, 0)

    scores = []
    for g in range(group):
        ikt_all = jnp.concatenate([ikbuf[g, p] for p in range(n_pages + 1)], axis=1)
        dots = _mm(iq_ref[g], ikt_all)
        weighted = jnp.maximum(dots, 0.0) * wt_ref[g]
        scores.append(jnp.sum(weighted.reshape(n_idx_heads, t_new, n_keys), axis=0))
    scores = jnp.concatenate(scores, axis=0)
    shape = (group * t_new, n_keys)
    qpos = past + lax.broadcasted_iota(I32, shape, 0) % t_new
    kpos = lax.broadcasted_iota(I32, shape, 1)
    valid = kpos <= qpos
    key = jnp.where(valid, _sortable_key(scores), INT_MIN)
    sel = valid & _topk_select(key, topk, int(math.log2(n_keys)) + 1)
    o_ref[...] = jnp.where(sel, 0.0, -jnp.inf).reshape(group, t_new, n_keys)


def dsa_sample_select(page_table, iq_rows, wt_rows, ik_new_t, cache_idx_kt, *, layer, n_idx_heads, t_new, topk,
                      group):
    db, n_pages = page_table.shape
    idx_dim = cache_idx_kt.shape[2]
    n_keys = (n_pages + 1) * PAGE_SIZE
    kern = functools.partial(_dsa_select_body, layer=layer, n_pages=n_pages, group=group,
                             n_idx_heads=n_idx_heads, t_new=t_new, topk=topk)
    per_g = lambda shape: pl.BlockSpec((group,) + shape, lambda si, pt: (si,) + (0,) * len(shape))
    grid_spec = pltpu.PrefetchScalarGridSpec(
        num_scalar_prefetch=1,
        grid=(db // group,),
        in_specs=[per_g((n_idx_heads * t_new, idx_dim)), per_g((n_idx_heads * t_new, 1)),
                  per_g((idx_dim, PAGE_SIZE)), pl.BlockSpec(memory_space=pl.ANY)],
        out_specs=per_g((t_new, n_keys)),
        scratch_shapes=[pltpu.VMEM((group, n_pages + 1, idx_dim, PAGE_SIZE), F32), pltpu.SemaphoreType.DMA(())],
    )
    return pl.pallas_call(
        kern,
        grid_spec=grid_spec,
        out_shape=jax.ShapeDtypeStruct((db, t_new, n_keys), F32),
        compiler_params=_cparams("arbitrary"),
        name="dsa_sample_select",
    )(page_table.reshape(-1), iq_rows, wt_rows, ik_new_t, cache_idx_kt)


def _dsa_sample_body(pt_ref, mask_ref, qbd_ref, knew_ref, vnew_ref,
                     blast_ref, bnew_ref, ck_hbm, cv_hbm, o_ref,
                     kbuf, vbuf, sem_k, sem_v,
                     *, layer, n_pages, chunk, n_heads, t_new):
    b = pl.program_id(0)
    n_seq = pl.num_programs(0)
    past = n_pages * PAGE_SIZE
    n_chunks = n_pages // chunk
    rows = n_heads * t_new
    ck = chunk * PAGE_SIZE

    def kv_copies(seq, c, j):
        slot = c % 2
        page = pt_ref[seq * n_pages + c * chunk + j]
        return (pltpu.make_async_copy(ck_hbm.at[layer, page], kbuf.at[slot, j], sem_k.at[slot]),
                pltpu.make_async_copy(cv_hbm.at[layer, page], vbuf.at[slot, j], sem_v.at[slot]))

    def start_chunk(seq, c):
        for j in range(chunk):
            kc, vc = kv_copies(seq, c, j)
            kc.start()
            vc.start()

    def wait_chunk(c):
        for j in range(chunk):
            kc, vc = kv_copies(b, c, j)
            kc.wait()
            vc.wait()

    @pl.when(b == 0)
    def _():
        start_chunk(0, 0)

    sel_rows = jnp.tile(mask_ref[0], (n_heads, 1))

    q_rep = jnp.tile(qbd_ref[0], (n_heads, 1))
    row_head = lax.broadcasted_iota(I32, q_rep.shape, 0) // t_new
    col_head = lax.broadcasted_iota(I32, q_rep.shape, 1) // HEAD_DIM
    qbd = jnp.where(row_head == col_head, q_rep, jnp.zeros_like(q_rep))
    neg = -1e30

    def update(state, logits, maskc, vt_bf16):
        m, l, acc = state
        s = logits + maskc
        m_new = jnp.maximum(m, jnp.max(s, axis=-1, keepdims=True))
        alpha = jnp.exp(m - m_new)
        p = jnp.exp(s - m_new)
        l = alpha * l + jnp.sum(p, axis=-1, keepdims=True)
        acc = alpha * acc + _mm_nt(p, vt_bf16)
        return m_new, l, acc

    def pages_t(buf, slot):
        return jnp.concatenate([buf[slot, j] for j in range(chunk)], axis=1).astype(BF16)

    state = (jnp.full((rows, 1), neg, F32), jnp.zeros((rows, 1), F32),
             jnp.zeros((rows, qbd.shape[1]), F32))
    for c in range(n_chunks):
        if c + 1 < n_chunks:
            start_chunk(b, c + 1)
        else:
            @pl.when(b + 1 < n_seq)
            def _():
                start_chunk(b + 1, 0)
        wait_chunk(c)
        slot = c % 2
        logits = _mm(qbd, pages_t(kbuf, slot))
        if c == n_chunks - 1:
            logits = jnp.concatenate([logits[:, :ck - PAGE_SIZE],
                                      logits[:, ck - PAGE_SIZE:] + blast_ref[...]], axis=1)
        state = update(state, logits, sel_rows[:, c * ck:(c + 1) * ck], pages_t(vbuf, slot))
    logits = _mm(qbd, knew_ref[0]) + bnew_ref[...]
    m, l, acc = update(state, logits, sel_rows[:, past:], vnew_ref[0])
    out = jnp.where(row_head == col_head, acc / l, 0.0)
    o_ref[0] = jnp.sum(out.reshape(n_heads, t_new, out.shape[1]), axis=0)


def dsa_sample(page_table, mask, q_bd, k_new_t, v_new_t, bias_last, bias_new, cache_kt, cache_vt,
               *, layer, n_heads, t_new, chunk):
    db, n_pages = page_table.shape
    assert (n_pages // chunk) % 2 == 0, "chunks alternate between two buffers across sequences"
    rows = n_heads * t_new
    width = q_bd.shape[2]
    kern = functools.partial(_dsa_sample_body, layer=layer, n_pages=n_pages, chunk=chunk, n_heads=n_heads,
                             t_new=t_new)
    per_b = lambda shape: pl.BlockSpec((1,) + shape, lambda bi, pt: (bi,) + (0,) * len(shape))
    const = lambda shape: pl.BlockSpec(shape, lambda bi, pt: (0,) * len(shape))
    any_spec = pl.BlockSpec(memory_space=pl.ANY)
    grid_spec = pltpu.PrefetchScalarGridSpec(
        num_scalar_prefetch=1,
        grid=(db,),
        in_specs=[per_b((t_new, mask.shape[2])),
                  per_b((t_new, width)), per_b((width, PAGE_SIZE)), per_b((width, PAGE_SIZE)),
                  const((rows, PAGE_SIZE)), const((rows, PAGE_SIZE)),
                  any_spec, any_spec],
        out_specs=per_b((t_new, width)),
        scratch_shapes=[pltpu.VMEM((2, chunk, width, PAGE_SIZE), F32),
                        pltpu.VMEM((2, chunk, width, PAGE_SIZE), F32),
                        pltpu.SemaphoreType.DMA((2,)),
                        pltpu.SemaphoreType.DMA((2,))],
    )
    return pl.pallas_call(
        kern,
        grid_spec=grid_spec,
        out_shape=jax.ShapeDtypeStruct((db, t_new, width), F32),
        compiler_params=_cparams("arbitrary"),
        name="dsa_sample",
    )(page_table.reshape(-1), mask, q_bd, k_new_t, v_new_t, bias_last, bias_new, cache_kt, cache_vt)


def _token_minor_cache(cache):
    l, pool, page = cache.shape[:3]
    nd = cache.ndim
    return cache.transpose((0, 1) + tuple(range(3, nd)) + (2,)).reshape(l, pool, -1, page)


def _dsa_sample_inputs(q, k_new, v_new, iq, ik_new, iw, bias_tiles):
    db, t, h, dh = q.shape
    ih = iq.shape[2]
    iq_rows = iq.transpose(0, 2, 1, 3).reshape(db, ih * t, -1).astype(BF16)
    wt_rows = (iw * (ih ** -0.5 * iq.shape[3] ** -0.5)).transpose(0, 2, 1).reshape(db, ih * t, 1)
    q_bd = q.reshape(db, t, h * dh)
    page_t = lambda x: jnp.pad(x.reshape(db, t, -1).transpose(0, 2, 1), ((0, 0), (0, 0), (0, PAGE_SIZE - t)))
    bias_new = bias_tiles[0, :, :t, :].reshape(h * t, Q_BLOCK)
    bias_last = bias_tiles[1, :, :t, :].reshape(h * t, Q_BLOCK)
    return ((iq_rows, wt_rows, page_t(ik_new)),
            (q_bd.astype(BF16), page_t(k_new).astype(BF16), page_t(v_new).astype(BF16), bias_last, bias_new))


def _matmul_residual_body(x_ref, a_ref, b_ref, w_ref, o_ref):
    ka = a_ref.shape[1]
    o_ref[...] = x_ref[...] + (_mm(a_ref[...], w_ref[:ka, :]) + _mm(b_ref[...], w_ref[ka:, :]))


def matmul_residual(x, a, b, w, *, tm, tn):
    n, d = x.shape
    ka, kb = a.shape[1], b.shape[1]
    return pl.pallas_call(
        _matmul_residual_body,
        grid=(n // tm, d // tn),
        in_specs=[pl.BlockSpec((tm, tn), lambda i, j: (i, j)),
                  pl.BlockSpec((tm, ka), lambda i, j: (i, 0)),
                  pl.BlockSpec((tm, kb), lambda i, j: (i, 0)),
                  pl.BlockSpec((ka + kb, tn), lambda i, j: (0, j))],
        out_specs=pl.BlockSpec((tm, tn), lambda i, j: (i, j)),
        out_shape=jax.ShapeDtypeStruct((n, d), F32),
        compiler_params=_cparams("parallel", "parallel"),
        name="matmul_residual",
    )(x, a, b, w)


def _cross_attn_body(x_ref, g_ref, wq_ref, mk_ref, mv_ref, wo_ref, o_ref, *, groups, t_rows, n_heads, head_dim):
    x = x_ref[...]
    h = _rmsnorm(x, g_ref[...]).astype(BF16)
    q = jnp.dot(h, wq_ref[...], preferred_element_type=F32).astype(BF16)
    scale = head_dim ** -0.5
    outs = []
    for gi in range(groups):
        qg = q[gi * t_rows:(gi + 1) * t_rows]
        heads = []
        for hh in range(n_heads):
            sl = slice(hh * head_dim, (hh + 1) * head_dim)
            logits = _mm_nt(qg[:, sl], mk_ref[gi, :, sl]) * scale
            mx = jnp.max(logits, axis=-1, keepdims=True)
            e = jnp.exp(logits - mx)
            p = e / jnp.sum(e, axis=-1, keepdims=True)
            heads.append(jnp.dot(p.astype(BF16), mv_ref[gi, :, sl], preferred_element_type=F32))
        outs.append(jnp.concatenate(heads, axis=1))
    o = jnp.concatenate(outs, axis=0) if groups > 1 else outs[0]
    o_ref[...] = x + jnp.dot(o.astype(BF16), wo_ref[...], preferred_element_type=F32)


def cross_attn(x, g, wq, mk, mv, wo, *, groups, t_rows, seq_tiles, n_heads):
    n, d = x.shape
    xw = wq.shape[1]
    rows = groups * t_rows
    m = mk.shape[1]
    kern = functools.partial(_cross_attn_body, groups=groups, t_rows=t_rows, n_heads=n_heads,
                             head_dim=xw // n_heads)
    return pl.pallas_call(
        kern,
        grid=(n // rows,),
        in_specs=[pl.BlockSpec((rows, d), lambda i: (i, 0)),
                  pl.BlockSpec((1, d), lambda i: (0, 0)),
                  pl.BlockSpec((d, xw), lambda i: (0, 0)),
                  pl.BlockSpec((groups, m, xw), lambda i: (i // seq_tiles, 0, 0)),
                  pl.BlockSpec((groups, m, xw), lambda i: (i // seq_tiles, 0, 0)),
                  pl.BlockSpec((xw, d), lambda i: (0, 0))],
        out_specs=pl.BlockSpec((rows, d), lambda i: (i, 0)),
        out_shape=jax.ShapeDtypeStruct((n, d), F32),
        compiler_params=_cparams("parallel"),
        name="cross_attn",
    )(x, g.reshape(1, d), wq, mk, mv, wo)


def _router_body(x_ref, g_ref, wr_ref, br_ref, h_ref, r_ref, *, n_groups, per_group):
    h = _rmsnorm(x_ref[...], g_ref[...])
    _store_slab_rows(h_ref, h, h.shape[1] // LANES)
    logits = _mm3(h, wr_ref[...]) + br_ref[...]
    lane = lax.broadcasted_iota(I32, logits.shape, 1).astype(F32)
    big = 1e9
    first_lane = lambda hit: jnp.min(jnp.where(hit, lane, big), axis=-1, keepdims=True)
    gl = jnp.where(lane < n_groups, logits, -jnp.inf)
    gmax = jnp.max(gl, axis=-1, keepdims=True)
    grp = first_lane(gl == gmax)
    p_grp = 1.0 / jnp.sum(jnp.exp(gl - gmax), axis=-1, keepdims=True)
    e_id = lane - n_groups
    in_grp = (e_id >= grp * per_group) & (e_id < (grp + 1.0) * per_group)
    el = jnp.where(in_grp, logits, -jnp.inf)
    v1 = jnp.max(el, axis=-1, keepdims=True)
    i1 = first_lane(el == v1) - n_groups
    el2 = jnp.where(e_id == i1, -jnp.inf, el)
    v2 = jnp.max(el2, axis=-1, keepdims=True)
    i2 = first_lane(el2 == v2) - n_groups
    e2 = jnp.exp(v2 - v1)
    g1 = p_grp / (1.0 + e2)
    g2 = p_grp * e2 / (1.0 + e2)
    r_ref[...] = jnp.where(lane == 0, g1, jnp.where(lane == 1, g2, jnp.where(
        lane == 2, i1, jnp.where(lane == 3, i2, 0.0))))


def router(x, g, w_r, b_r, *, tm, n_groups, per_group):
    n, d = x.shape
    kern = functools.partial(_router_body, n_groups=n_groups, per_group=per_group)
    return pl.pallas_call(
        kern,
        grid=(n // tm,),
        in_specs=[pl.BlockSpec((tm, d), lambda i: (i, 0)),
                  pl.BlockSpec((1, d), lambda i: (0, 0)),
                  pl.BlockSpec((d, LANES), lambda i: (0, 0)),
                  pl.BlockSpec((1, LANES), lambda i: (0, 0))],
        out_specs=[pl.BlockSpec((tm * (d // LANES), LANES), lambda i: (i, 0)),
                   pl.BlockSpec((tm, LANES), lambda i: (i, 0))],
        out_shape=[jax.ShapeDtypeStruct((n * (d // LANES), LANES), F32), jax.ShapeDtypeStruct((n, LANES), F32)],
        compiler_params=_cparams("parallel"),
        name="moe_router",
    )(x, g.reshape(1, d), w_r, b_r)


def _slab_rows(buf, offset, n_rows, n_slab, stride):
    return jnp.concatenate([buf[pl.ds(offset + s, n_rows, stride=stride), :] for s in range(n_slab)], axis=1)


def _store_slab_rows(buf, x, n_slab):
    n_rows = x.shape[0]
    for s in range(n_slab):
        buf[pl.ds(s, n_rows, stride=n_slab), :] = x[:, s * LANES:(s + 1) * LANES]


def _moe_ffn_body(te_ref, nr_ref, tok_ref, dst_ref, h_hbm, w1_ref, w3_ref, w2_ref, y_hbm,
                  xbuf, ybuf, gsem, ssem, *, tm):
    t = pl.program_id(0)
    n_tiles = pl.num_programs(0)
    n_slab = h_hbm.shape[1]
    slot = t % 2

    def slab(buf, sl, r):
        return buf.at[sl, pl.ds(pl.multiple_of(r * n_slab, n_slab), n_slab)]

    def gather_copy(tile, sl, r):
        return pltpu.make_async_copy(h_hbm.at[tok_ref[tile * tm + r]], slab(xbuf, sl, r), gsem.at[sl])

    def scatter_copy(tile, sl, r):
        return pltpu.make_async_copy(slab(ybuf, sl, r), y_hbm.at[dst_ref[tile * tm + r]], ssem.at[sl])

    def start_rows(n, copy):
        def pair(i, c):
            copy(2 * i).start(priority=0)
            copy(2 * i + 1).start(priority=1)
            return c
        lax.fori_loop(0, n // 2, pair, 0)

        @pl.when(n % 2 == 1)
        def _():
            copy(n - 1).start(priority=0)

    def wait_rows(n, copy):
        def one(r, c):
            copy(r).wait()
            return c
        lax.fori_loop(0, n, one, 0)

    @pl.when(t == 0)
    def _():
        xbuf[...] = jnp.zeros(xbuf.shape, F32)
        start_rows(nr_ref[0], lambda r: gather_copy(0, 0, r))

    nxt = jnp.minimum(t + 1, n_tiles - 1)

    @pl.when(t + 1 < n_tiles)
    def _():
        start_rows(nr_ref[nxt], lambda r: gather_copy(nxt, 1 - slot, r))

    @pl.when(t >= 2)
    def _():
        prev2 = jnp.maximum(t - 2, 0)
        wait_rows(nr_ref[prev2], lambda r: scatter_copy(prev2, slot, r))

    n = nr_ref[t]

    @pl.when(n > 0)
    def _():
        wait_rows(n, lambda r: gather_copy(t, slot, r))
        x = _slab_rows(xbuf.at[slot], 0, tm, n_slab, n_slab).astype(BF16)
        a = jnp.dot(x, w1_ref[0].astype(BF16), preferred_element_type=F32)
        bgate = jnp.dot(x, w3_ref[0].astype(BF16), preferred_element_type=F32)
        u = (a / (1.0 + jnp.exp(-a))) * bgate
        y = jnp.dot(u.astype(BF16), w2_ref[0].astype(BF16), preferred_element_type=F32)
        _store_slab_rows(ybuf.at[slot], y, n_slab)
        start_rows(n, lambda r: scatter_copy(t, slot, r))

    @pl.when(t == n_tiles - 1)
    def _():
        @pl.when(t >= 1)
        def _():
            prev1 = jnp.maximum(t - 1, 0)
            wait_rows(nr_ref[prev1], lambda r: scatter_copy(prev1, 1 - slot, r))
        wait_rows(n, lambda r: scatter_copy(t, slot, r))


def moe_ffn(tile_expert, tile_rows, row_tok, row_dst, h_slabs, n_out_rows, w1, w3, w2, *, tm):
    n_tiles = tile_expert.shape[0]
    d, de = w1.shape[1], w1.shape[2]
    n_slab = d // LANES
    h3 = h_slabs.reshape(-1, n_slab, LANES)
    w_in_spec = pl.BlockSpec((1, d, de), lambda t, te, nr, tok, dst: (te[t], 0, 0))
    grid_spec = pltpu.PrefetchScalarGridSpec(
        num_scalar_prefetch=4,
        grid=(n_tiles,),
        in_specs=[pl.BlockSpec(memory_space=pl.ANY), w_in_spec, w_in_spec,
                  pl.BlockSpec((1, de, d), lambda t, te, nr, tok, dst: (te[t], 0, 0))],
        out_specs=pl.BlockSpec(memory_space=pl.ANY),
        scratch_shapes=[pltpu.VMEM((2, tm * n_slab, LANES), F32), pltpu.VMEM((2, tm * n_slab, LANES), F32),
                        pltpu.SemaphoreType.DMA((2,)), pltpu.SemaphoreType.DMA((2,))],
    )
    y = pl.pallas_call(
        functools.partial(_moe_ffn_body, tm=tm),
        grid_spec=grid_spec,
        out_shape=jax.ShapeDtypeStruct((n_out_rows, n_slab, LANES), F32),
        compiler_params=_cparams("arbitrary"),
        name="moe_ffn",
    )(tile_expert, tile_rows, row_tok, row_dst, h3, w1, w3, w2)
    return y.reshape(n_out_rows * n_slab, LANES)


def _moe_dispatch(eids, n_experts, tm):
    n, k = eids.shape
    m = n * k
    flat_e = eids.reshape(-1)
    order = jnp.argsort(flat_e, stable=True).astype(I32)
    counts = jnp.sum((flat_e[:, None] == jnp.arange(n_experts)[None, :]).astype(I32), axis=0)
    padded = (counts + tm - 1) // tm * tm
    pad_end = jnp.cumsum(padded)
    pad_start = pad_end - padded
    start = jnp.cumsum(counts) - counts
    n_tiles = -(-m // tm) + n_experts
    tile_start = jnp.arange(n_tiles) * tm
    last = jnp.maximum(pad_end[-1] - 1, 0)
    tile_expert = jnp.minimum(jnp.searchsorted(pad_end, jnp.minimum(tile_start, last), side='right'),
                              n_experts - 1).astype(I32)
    first = tile_start - pad_start[tile_expert]
    tile_rows = jnp.where(tile_start < pad_end[-1], jnp.clip(counts[tile_expert] - first, 0, tm), 0).astype(I32)
    src = jnp.clip(start[tile_expert][:, None] + first[:, None] + jnp.arange(tm)[None, :], 0, m - 1)
    row_dst = order[src].reshape(-1)
    return tile_expert, tile_rows, (row_dst // k).astype(I32), row_dst.astype(I32)


def _combine_body(x_ref, r_ref, g_ref, y_ref, o_ref, *, tm, top_k):
    n_slab = x_ref.shape[1] // LANES
    route = r_ref[...]
    x = x_ref[...]
    for kk in range(top_k):
        x = x + _slab_rows(y_ref, kk * n_slab, tm, n_slab, top_k * n_slab) * route[:, kk:kk + 1]
    o_ref[...] = _rmsnorm(x, g_ref[...])


def moe_combine(x, route, g, y_slabs, *, tm, top_k, tile_offset):
    n, d = x.shape
    rows = tm * top_k * (d // LANES)
    return pl.pallas_call(
        functools.partial(_combine_body, tm=tm, top_k=top_k),
        grid=(n // tm,),
        in_specs=[pl.BlockSpec((tm, d), lambda i: (i, 0)),
                  pl.BlockSpec((tm, LANES), lambda i: (i, 0)),
                  pl.BlockSpec((1, d), lambda i: (0, 0)),
                  pl.BlockSpec((rows, LANES), lambda i: (i + tile_offset, 0))],
        out_specs=pl.BlockSpec((tm, d), lambda i: (i, 0)),
        out_shape=jax.ShapeDtypeStruct((n, d), F32),
        compiler_params=_cparams("parallel"),
        name="moe_combine",
    )(x, route, g.reshape(1, d), y_slabs)


def kernel(x_prompt, x_sample, mem_prompt, cache_k, cache_v, cache_idx_k, page_table, state_wkv, state_shift, cache_mem_k, cache_mem_v, g_mix, w_in, mu_shift, rw_w0, rw_w2, rw_a0, rw_a2, rw_g2, rw_kk, rw_ka, rw_rk, rw_ln_g, rw_ln_b, w_out, g_cross, g_mem, w_cq, w_ck, w_cv, w_co, g_ffn, w_rg, b_rg, w_re, b_re, w_e1, w_e3, w_e2, rel_bias, g_final):
    B, S, D = x_prompt.shape
    DB, T, _ = x_sample.shape
    assert w_in.shape[0] == 1, "single-layer trunk only"
    l = 0
    n_pages = page_table.shape[1]
    past = n_pages * PAGE_SIZE
    topk_p = min(TOPK_MAX, S // 4)
    topk_s = min(TOPK_MAX, (past + T) // 4)
    rw_proj = mu_shift.shape[1]
    width = rw_w0.shape[1]
    at_w = D - width
    n_heads = at_w // HEAD_DIM
    idx_dim = cache_idx_k.shape[-1]
    ih = (w_in.shape[2] - rw_proj - 3 * at_w - idx_dim) // (idx_dim + 1)
    xw = w_cq.shape[2]
    x_heads = cache_mem_k.shape[3]
    n_mem = mem_prompt.shape[1]
    n_experts = w_e1.shape[1]
    top_k = 2
    tn = PROJ_COLS
    n_main = 3 * width
    n_lora = rw_proj - n_main
    n_att = 3 * at_w + ih * idx_dim
    n_tail = n_lora + idx_dim + ih
    assert n_main % tn == 0 and at_w % tn == 0 and (ih * idx_dim) % tn == 0 and n_tail <= tn

    w_t = w_in[l].T
    w_all = jnp.concatenate([w_t[:n_main], w_t[rw_proj:rw_proj + at_w] * HEAD_DIM ** -0.5,
                             w_t[rw_proj + at_w:rw_proj + n_att], w_t[n_main:rw_proj],
                             w_t[rw_proj + n_att:], jnp.zeros((tn - n_tail, D), w_t.dtype)], axis=0).astype(BF16)
    flat, per_head = False, True
    segments = [(n_main // tn, [(flat, F32)]),
                (at_w // tn, [(flat, BF16)]),
                (at_w // tn, [(per_head, F32), (flat, BF16)]),
                (at_w // tn, [(per_head, F32), (flat, BF16)]),
                (ih * idx_dim // tn, [(flat, BF16)]),
                (1, [(flat, F32)])]
    w_out_b = w_out[l].astype(BF16)
    w_cq_b, w_co_b = w_cq[l].astype(BF16), w_co[l].astype(BF16)
    w_ckv = jnp.concatenate([w_ck[l], w_cv[l]], axis=1).astype(BF16)
    n_route = w_rg.shape[2] + w_re.shape[2]
    w_r = jnp.pad(jnp.concatenate([w_rg[l], w_re[l]], axis=1), ((0, 0), (0, LANES - n_route)))
    b_r = jnp.pad(jnp.concatenate([b_rg[l], b_re[l]]), (0, LANES - n_route)).reshape(1, LANES)
    rw_args = (mu_shift[l], rw_w0[l], rw_w2[l], rw_a0[l], rw_a2[l], rw_g2[l], rw_kk[l], rw_ka[l],
               rw_rk[l].reshape(-1))
    tiles = bias_tables(rel_bias, ((0, False), (Q_BLOCK, False), (0, True), (Q_BLOCK, True)))

    def project(x2d, b_, t_, tm):
        f_main, q, k_heads, k, v_heads, v, iq, f_tail = norm_matmul_split(x2d, g_mix[l], w_all, segments,
                                                                          tm=tm, tn=tn)
        r3 = lambda z: z.reshape(b_, t_, -1)
        f_main, f_tail = r3(f_main), r3(f_tail)
        ik = f_tail[..., n_lora:n_lora + idx_dim]
        iw = f_tail[..., n_lora + idx_dim:n_tail]
        shift = jnp.concatenate([f_main[:, -1], f_tail[:, -1, :n_lora]], axis=-1)
        heads5 = lambda z: z.reshape(1, b_, t_, n_heads, HEAD_DIM)
        return f_main, r3(q), heads5(k_heads), r3(k), heads5(v_heads), r3(v), r3(iq), f_tail, ik, iw, shift

    def rw_rows(y):
        b_, p_, t_, _ = y.shape
        return y.transpose(0, 2, 1, 3).reshape(b_ * t_, p_ * PAIR).astype(BF16)

    xp = x_prompt.reshape(B * S, D)
    fm_p, q, k_p, kb_p, v_p, vb_p, iq, ft_p, ik_p, iw, shift_p = project(xp, B, S, PROJ_ROWS)
    feats_p = rwkv_prep(fm_p, ft_p, jnp.zeros((B, rw_proj), F32), *rw_args, tm=PREP_ROWS, width=width)
    rw_p, st_p = rwkv_chunk(feats_p, jnp.zeros((B, width // PAIR, PAIR, PAIR), F32), rw_ln_g[l], rw_ln_b[l],
                            L=RWKV_CHUNK)
    tr = lambda z: z.transpose(0, 2, 1)
    k_pairs = kb_p.reshape(B, S, n_heads // 2, PAIR).transpose(0, 2, 1, 3)
    at_p = dsa_prompt(ik_p.astype(BF16), tr(iq), tr(iw * (ih ** -0.5 * idx_dim ** -0.5))[:, :, None, :],
                      k_pairs, tr(q), tr(vb_p), tiles[2:4], topk=topk_p)
    x1_p = matmul_residual(xp, rw_rows(rw_p), tr(at_p).reshape(B * S, at_w), w_out_b, tm=ROW_TILE, tn=D)
    mkv = norm_matmul(mem_prompt.reshape(B * n_mem, D), g_mem[l], w_ckv, tm=MEM_ROWS, tn=xw)
    mk_p = mkv[:, :xw].reshape(B, n_mem, xw)
    mv_p = mkv[:, xw:].reshape(B, n_mem, xw)
    x2_p = cross_attn(x1_p, g_cross[l], w_cq_b, mk_p.astype(BF16), mv_p.astype(BF16), w_co_b,
                      groups=1, t_rows=ROW_TILE, seq_tiles=S // ROW_TILE, n_heads=x_heads)

    xs = x_sample.reshape(DB * T, D)
    fm_s, q2, k_s, kb_s, v_s, vb_s, iq2, ft_s, ik_s, iw2, shift_s = project(xs, DB, T, DB * T)
    feats_s = rwkv_prep(fm_s, ft_s, state_shift[l], *rw_args, tm=T, width=width)
    rw_s, st_s = rwkv_chunk(feats_s, _state_to_block_diag(state_wkv[l]), rw_ln_g[l], rw_ln_b[l], L=T)
    r4 = lambda z, h_: z.reshape(DB, T, h_, -1)
    sel_args, att_args = _dsa_sample_inputs(r4(q2, n_heads), r4(kb_s, n_heads), r4(vb_s, n_heads), r4(iq2, ih),
                                            ik_s, iw2, tiles[0:2])
    mask_s = dsa_sample_select(page_table, *sel_args, _token_minor_cache(cache_idx_k), layer=l,
                               n_idx_heads=ih, t_new=T, topk=topk_s, group=SELECT_GROUP)
    at_s = dsa_sample(page_table, mask_s, *att_args, _token_minor_cache(cache_k), _token_minor_cache(cache_v),
                      layer=l, n_heads=n_heads, t_new=T, chunk=SAMPLE_CHUNK_PAGES)
    x1_s = matmul_residual(xs, rw_rows(rw_s), at_s.reshape(DB * T, at_w), w_out_b, tm=DB * T, tn=D)
    x2_s = cross_attn(x1_s, g_cross[l], w_cq_b, cache_mem_k[l].reshape(DB, n_mem, xw).astype(BF16),
                      cache_mem_v[l].reshape(DB, n_mem, xw).astype(BF16), w_co_b,
                      groups=CROSS_GROUP, t_rows=T, seq_tiles=1, n_heads=x_heads)

    h_p, route_p = router(x2_p, g_ffn[l], w_r, b_r, tm=ROW_TILE, n_groups=w_rg.shape[2],
                          per_group=w_re.shape[2] // w_rg.shape[2])
    h_s, route_s = router(x2_s, g_ffn[l], w_r, b_r, tm=DB * T, n_groups=w_rg.shape[2],
                          per_group=w_re.shape[2] // w_rg.shape[2])
    h_all = jnp.concatenate([h_p, h_s], axis=0)
    eids = jnp.concatenate([route_p[:, top_k:2 * top_k], route_s[:, top_k:2 * top_k]], axis=0).astype(I32)
    tm_moe = DB * T
    n_p, n_all = B * S, B * S + DB * T
    assert n_p % tm_moe == 0
    tile_expert, tile_rows, row_tok, row_dst = _moe_dispatch(eids, n_experts, tm_moe)
    y_slabs = moe_ffn(tile_expert, tile_rows, row_tok, row_dst, h_all, n_all * top_k,
                      w_e1[l], w_e3[l], w_e2[l], tm=tm_moe)
    y_p = moe_combine(x2_p, route_p, g_final, y_slabs, tm=tm_moe, top_k=top_k, tile_offset=0)
    y_s = moe_combine(x2_s, route_s, g_final, y_slabs, tm=tm_moe, top_k=top_k, tile_offset=n_p // tm_moe)

    return (y_p.reshape(B, S, D), y_s.reshape(DB, T, D),
            k_p, v_p, ik_p[None], _state_from_block_diag(st_p)[None], shift_p[None],
            mk_p.reshape(1, B, n_mem, x_heads, xw // x_heads), mv_p.reshape(1, B, n_mem, x_heads, xw // x_heads),
            k_s, v_s, ik_s[None], _state_from_block_diag(st_s)[None], shift_s[None])
```

```python
import functools
import math

import jax
import jax.numpy as jnp
from jax import lax
from jax.experimental import pallas as pl
from jax.experimental.pallas import tpu as pltpu

F32 = jnp.float32
BF16 = jnp.bfloat16
I32 = jnp.int32

LANES = 128
SUBLANES = 8
VMEM_LIMIT_BYTES = 56 * 1024 * 1024

HEAD_DIM = 64
PAIR = 2 * HEAD_DIM
GN_EPS = 64e-5
NORM_EPS = 1e-6
TOPK_MAX = 256
Q_BLOCK = 128
N_BUCKETS = 32
MAX_DISTANCE = 128
PAGE_SIZE = 128
INT_MIN = -(2 ** 31)

PROJ_ROWS = 1024
PROJ_COLS = 512
PREP_ROWS = 512
RWKV_CHUNK = 64
ROW_TILE = 512
MEM_ROWS = 256
SAMPLE_CHUNK_PAGES = 8
SELECT_GROUP = 8
CROSS_GROUP = 8


def _cparams(*sem):
    return pltpu.CompilerParams(dimension_semantics=sem, vmem_limit_bytes=VMEM_LIMIT_BYTES)


def _mm(a, b):
    return jnp.dot(a.astype(BF16), b.astype(BF16), preferred_element_type=F32)


def _mm_nt(a, b):
    return lax.dot_general(a.astype(BF16), b.astype(BF16), (((1,), (1,)), ((), ())),
                           preferred_element_type=F32)


def _split2(x):
    hi = x.astype(BF16)
    lo = (x - hi.astype(F32)).astype(BF16)
    return hi, lo


def _mm3(a, b):
    ah, al = _split2(a)
    bh, bl = _split2(b)
    d = lambda x, y: jnp.dot(x, y, preferred_element_type=F32)
    return d(ah, bh) + (d(ah, bl) + d(al, bh))


def _mm_exact_rhs(a, b_bf16):
    hi = a.astype(BF16)
    r1 = a - hi.astype(F32)
    mid = r1.astype(BF16)
    lo = (r1 - mid.astype(F32)).astype(BF16)
    d = lambda x: jnp.dot(x, b_bf16, preferred_element_type=F32)
    return d(hi) + (d(mid) + d(lo))


def _rmsnorm(x, g):
    ms = jnp.mean(x * x, axis=-1, keepdims=True)
    return x * lax.rsqrt(ms + NORM_EPS) * g


def _norm_matmul_body(x_ref, g_ref, w_ref, o_ref, xn_ref):
    @pl.when(pl.program_id(1) == 0)
    def _():
        xn_ref[...] = _rmsnorm(x_ref[...], g_ref[...]).astype(BF16)

    o_ref[...] = _mm(xn_ref[...], w_ref[...])


def norm_matmul(x, g, w, *, tm, tn):
    n, d = x.shape
    m = w.shape[1]
    return pl.pallas_call(
        _norm_matmul_body,
        grid=(n // tm, m // tn),
        in_specs=[pl.BlockSpec((tm, d), lambda i, j: (i, 0)),
                  pl.BlockSpec((1, d), lambda i, j: (0, 0)),
                  pl.BlockSpec((d, tn), lambda i, j: (0, j))],
        out_specs=pl.BlockSpec((tm, tn), lambda i, j: (i, j)),
        out_shape=jax.ShapeDtypeStruct((n, m), F32),
        scratch_shapes=[pltpu.VMEM((tm, d), BF16)],
        compiler_params=_cparams("parallel", "arbitrary"),
        name="norm_matmul",
    )(x, g.reshape(1, d), w)


def _norm_matmul_split_body(x_ref, g_ref, w_ref, *rest, bounds):
    o_refs, xn_ref = rest[:-1], rest[-1]
    j = pl.program_id(1)

    @pl.when(j == 0)
    def _():
        xn_ref[...] = _rmsnorm(x_ref[...], g_ref[...]).astype(BF16)

    res = _mm_nt(xn_ref[...], w_ref[...])
    tm, tn = res.shape
    for o_ref, (lo, hi) in zip(o_refs, bounds):
        @pl.when((j >= lo) & (j < hi))
        def _(o_ref=o_ref, lo=lo, hi=hi):
            if o_ref.shape[1] == HEAD_DIM:
                n_heads = (hi - lo) * tn // HEAD_DIM
                per_tile = tn // HEAD_DIM
                for c in range(hi - lo):
                    @pl.when(j == lo + c)
                    def _(c=c):
                        for hh in range(per_tile):
                            o_ref[pl.ds(c * per_tile + hh, tm, stride=n_heads), :] = (
                                res[:, hh * HEAD_DIM:(hh + 1) * HEAD_DIM].astype(o_ref.dtype))
            else:
                o_ref[...] = res.astype(o_ref.dtype)


def norm_matmul_split(x, g, wt, segments, *, tm, tn):
    n, d = x.shape
    m = wt.shape[0]
    bounds, out_specs, out_shape, lo = [], [], [], 0
    for nt, outs in segments:
        hi = lo + nt
        for per_head, dt in outs:
            bounds.append((lo, hi))
            if per_head:
                heads = nt * tn // HEAD_DIM
                out_specs.append(pl.BlockSpec((tm * heads, HEAD_DIM), lambda i, j: (i, 0),
                                              pipeline_mode=pl.Buffered(1)))
                out_shape.append(jax.ShapeDtypeStruct((n * heads, HEAD_DIM), dt))
            else:
                out_specs.append(pl.BlockSpec((tm, tn),
                                              lambda i, j, lo=lo, hi=hi: (i, jnp.clip(j - lo, 0, hi - lo - 1))))
                out_shape.append(jax.ShapeDtypeStruct((n, nt * tn), dt))
        lo = hi
    assert lo * tn == m
    return pl.pallas_call(
        functools.partial(_norm_matmul_split_body, bounds=tuple(bounds)),
        grid=(n // tm, m // tn),
        in_specs=[pl.BlockSpec((tm, d), lambda i, j: (i, 0), pipeline_mode=pl.Buffered(1)),
                  pl.BlockSpec((1, d), lambda i, j: (0, 0)),
                  pl.BlockSpec((tn, d), lambda i, j: (j, 0))],
        out_specs=out_specs,
        out_shape=out_shape,
        scratch_shapes=[pltpu.VMEM((tm, d), BF16)],
        compiler_params=_cparams("arbitrary", "arbitrary"),
        name="norm_matmul_split",
    )(x, g.reshape(1, d), wt)


def _pair_ones():
    r = lax.broadcasted_iota(I32, (PAIR, PAIR), 0) // HEAD_DIM
    c = lax.broadcasted_iota(I32, (PAIR, PAIR), 1) // HEAD_DIM
    return (r == c).astype(BF16)


def _head_sum(x, ones_bd):
    return _mm_exact_rhs(x, ones_bd)


def _rwkv_prep_body(f_ref, prev8_ref, init_ref, mu_ref, ft_ref, tprev8_ref, tinit_ref, tmu_ref,
                    w0_ref, w2_ref, a0_ref, a2_ref, g2_ref, kk_ref, ka_ref, rk_ref,
                    r_o, k_o, v_o, kk_o, b_o, ld_o, g_o, bon_o, *, tm, width):
    i = pl.program_id(1)

    def token_shift(f, p8_ref, i_ref, m_ref):
        nc = m_ref.shape[1]
        prev_row = jnp.where(i == 0, i_ref[0], p8_ref[0, SUBLANES - 1:SUBLANES, :nc])
        rolled = pltpu.roll(f, shift=1, axis=0)
        row = lax.broadcasted_iota(I32, f.shape, 0)
        f_prev = jnp.where(row == 0, prev_row, rolled)
        return f + (f_prev - f) * m_ref[...]

    fs = token_shift(f_ref[0], prev8_ref, init_ref, mu_ref)
    n_dec = w2_ref.shape[0]
    n_icl = a2_ref.shape[0]
    n_lora = tmu_ref.shape[1]
    ts = token_shift(ft_ref[0][:, :n_lora], tprev8_ref, tinit_ref, tmu_ref)
    w_ = width
    r = fs[:, 0:w_]
    k = fs[:, w_:2 * w_]
    v = fs[:, 2 * w_:3 * w_]
    wd = ts[:, 0:n_dec]
    ad = ts[:, n_dec:n_dec + n_icl]
    gd = ts[:, n_dec + n_icl:]
    z = w0_ref[...] + _mm3(jnp.tanh(wd), w2_ref[...])
    nz = -z
    softplus = jnp.maximum(nz, 0.0) + jnp.log(1.0 + jnp.exp(-jnp.abs(nz)))
    w = -softplus - 0.5
    ld = -jnp.exp(w)
    a = 1.0 / (1.0 + jnp.exp(-(a0_ref[...] + _mm3(ad, a2_ref[...]))))
    g = _mm3(1.0 / (1.0 + jnp.exp(-gd)), g2_ref[...])
    kk = k * kk_ref[...]
    k2 = k * (1.0 + (a - 1.0) * ka_ref[...])
    rk = r * k2 * rk_ref[...]
    ones_bd = _pair_ones()
    for p in range(w_ // PAIR):
        sl = slice(p * PAIR, (p + 1) * PAIR)
        kkp = kk[:, sl]
        nrm = jnp.sqrt(_head_sum(kkp * kkp, ones_bd))
        kkp = kkp / jnp.maximum(nrm, 1e-12)
        ap = a[:, sl]
        r_o[0, p] = r[:, sl]
        k_o[0, p] = k2[:, sl]
        v_o[0, p] = v[:, sl]
        kk_o[0, p] = kkp
        b_o[0, p] = kkp * ap
        ld_o[0, p] = ld[:, sl]
        g_o[0, p] = g[:, sl]
        bon_o[0, p] = _head_sum(rk[:, sl], ones_bd) * v[:, sl]


def rwkv_prep(f_main, f_tail, init_prev, mu, w0, w2, a0, a2, g2, k_k, k_a, r_k, *, tm, width):
    b, t, n_main = f_main.shape
    n_tail = f_tail.shape[2]
    n_lora = mu.shape[0] - n_main
    npair = width // PAIR
    row1 = lambda x: x.reshape(1, -1)
    kern = functools.partial(_rwkv_prep_body, tm=tm, width=width)
    full = lambda a: pl.BlockSpec(a.shape, lambda bi, i: (0,) * a.ndim)
    args = [row1(w0), w2, row1(a0), a2, g2, row1(k_k), row1(k_a), row1(r_k)]
    out_spec = pl.BlockSpec((1, npair, tm, PAIR), lambda bi, i: (bi, 0, i, 0))
    out_shape = jax.ShapeDtypeStruct((b, npair, t, PAIR), F32)
    prev8_map = lambda bi, i: (bi, jnp.maximum(i * (tm // SUBLANES) - 1, 0), 0)

    def feature_specs(ncols, n_init):
        return [pl.BlockSpec((1, tm, ncols), lambda bi, i: (bi, i, 0)),
                pl.BlockSpec((1, SUBLANES, ncols), prev8_map),
                pl.BlockSpec((1, 1, n_init), lambda bi, i: (bi, 0, 0)),
                pl.BlockSpec((1, n_init), lambda bi, i: (0, 0))]

    return pl.pallas_call(
        kern,
        grid=(b, t // tm),
        in_specs=feature_specs(n_main, n_main) + feature_specs(n_tail, n_lora) + [full(a) for a in args],
        out_specs=[out_spec] * 8,
        out_shape=[out_shape] * 8,
        compiler_params=_cparams("parallel", "parallel"),
        name="rwkv_prep",
    )(f_main, f_main, init_prev[:, :n_main].reshape(b, 1, n_main), row1(mu[:n_main]),
      f_tail, f_tail, init_prev[:, n_main:].reshape(b, 1, n_lora), row1(mu[n_main:]), *args)


def _rwkv_chunk_body(r_ref, k_ref, v_ref, kk_ref, b_ref, ld_ref, g_ref, bon_ref, s0_ref,
                     lng_ref, lnb_ref, o_ref, st_ref, s_ref, *, L, npair, group):
    c = pl.program_id(1)

    @pl.when(c == 0)
    def _():
        s_ref[...] = s0_ref[0]

    L2 = 2 * L
    row = lax.broadcasted_iota(I32, (L2, L2), 0)
    col = lax.broadcasted_iota(I32, (L2, L2), 1)
    eye = (row == col).astype(F32)
    row4 = lax.broadcasted_iota(I32, (2 * L2, 2 * L2), 0)
    col4 = lax.broadcasted_iota(I32, (2 * L2, 2 * L2), 1)
    rr, cc = row4 % L2, col4 % L2
    tri_all = ((rr // L) == (cc // L)) & ((cc < rr) | ((row4 >= L2) & (cc == rr)))
    tr = lax.broadcasted_iota(I32, (L, L), 0)
    tc = lax.broadcasted_iota(I32, (L, L), 1)
    cum_mat = (tc <= tr).astype(BF16)
    lane = lax.broadcasted_iota(I32, (L, PAIR), 1)
    first = lane < HEAD_DIM
    ones_bd = _pair_ones()
    n_sq = max(int(math.ceil(math.log2(L))) - 1, 0)

    def block_diag(x):
        return jnp.concatenate([jnp.where(first, x, 0.0), jnp.where(first, 0.0, x)], axis=0)

    def group_step(gi, carry):
        ps = [gi * group + j for j in range(group)]
        each = lambda f, *cols: [f(*args) for args in zip(*cols)]
        ld = [ld_ref[0, p] for p in ps]
        cum = each(lambda x: _mm_exact_rhs_t(cum_mat, x), ld)
        dec = each(jnp.exp, cum)
        dec_inv = each(lambda c_: jnp.exp(-c_), cum)
        a_t = each(lambda p, c_, l_: block_diag(-kk_ref[0, p] * jnp.exp(c_ - l_)), ps, cum, ld)
        b_t = each(lambda p, e: block_diag(b_ref[0, p] * e), ps, dec_inv)
        k_t = each(lambda p, e: block_diag(k_ref[0, p] * e), ps, dec_inv)
        r_t = each(lambda p, e: block_diag(r_ref[0, p] * e), ps, dec)
        v_b = each(lambda p: block_diag(v_ref[0, p]), ps)
        ar = each(lambda a, r: jnp.concatenate([a, r], axis=0), a_t, r_t)
        bk = each(lambda b, k: jnp.concatenate([b, k], axis=0), b_t, k_t)
        cross = each(lambda x_, y_: jnp.where(tri_all, _mm_nt(x_, y_), 0.0), ar, bk)
        a_ab = [c_[:L2, :L2] for c_ in cross]
        a_rb = [c_[L2:, :L2] for c_ in cross]
        akrk = [c_[:, L2:] for c_ in cross]
        t_inv = each(lambda a: eye + a, a_ab)
        if n_sq >= 1:
            x = each(lambda a: _mm(a, a), a_ab)
            for _ in range(n_sq - 1):
                xt = each(lambda x_, t: _mm(x_, jnp.concatenate([x_, t], axis=1)), x, t_inv)
                x = [z[:, :L2] for z in xt]
                t_inv = each(lambda t, z: t + z[:, L2:], t_inv, xt)
            t_inv = each(lambda t, x_: t + _mm(x_, t), t_inv, x)
        s = [s_ref[p] for p in ps]
        ar_s = each(_mm_nt, ar, s)
        akrk_v = each(_mm, akrk, v_b)
        u = each(lambda t, p1, p2: _mm(t, p1[:L2] + p2[:L2]), t_inv, ar_s, akrk_v)
        y_b = each(lambda p1, rb, u_, p2: p1[L2:] + _mm(rb, u_) + p2[L2:], ar_s, a_rb, u, akrk_v)
        s_new = each(lambda s_, u_, b, v, k, d: (s_ + _mm(u_.T, b) + _mm(v.T, k)) * d[L - 1:L, :],
                     s, u, b_t, v_b, k_t, dec)
        for p, sn in zip(ps, s_new):
            s_ref[p] = sn
        y = each(lambda yb: yb[:L] + yb[L:], y_b)
        mean = each(lambda y_: _head_sum(y_, ones_bd) * (1.0 / HEAD_DIM), y)
        d = each(lambda y_, m: y_ - m, y, mean)
        var = each(lambda d_: _head_sum(d_ * d_, ones_bd) * (1.0 / HEAD_DIM), d)
        for p, d_, v_ in zip(ps, d, var):
            yn = d_ * lax.rsqrt(v_ + GN_EPS) * lng_ref[p] + lnb_ref[p]
            o_ref[0, p] = (yn + bon_ref[0, p]) * g_ref[0, p]
        return carry

    lax.fori_loop(0, npair // group, group_step, 0)

    @pl.when(c == pl.num_programs(1) - 1)
    def _():
        st_ref[0] = s_ref[...]


def _mm_exact_rhs_t(m_bf16, x):
    hi = x.astype(BF16)
    r1 = x - hi.astype(F32)
    mid = r1.astype(BF16)
    lo = (r1 - mid.astype(F32)).astype(BF16)
    d = lambda y: jnp.dot(m_bf16, y, preferred_element_type=F32)
    return d(hi) + (d(mid) + d(lo))


def rwkv_chunk(feats, s0_bd, ln_g, ln_b, *, L, group=8):
    b, npair, t, _ = feats[0].shape
    blk = pl.BlockSpec((1, npair, L, PAIR), lambda bi, c: (bi, 0, c, 0))
    st_spec = pl.BlockSpec((1, npair, PAIR, PAIR), lambda bi, c: (bi, 0, 0, 0))
    par_spec = pl.BlockSpec((npair, 1, PAIR), lambda bi, c: (0, 0, 0))
    kern = functools.partial(_rwkv_chunk_body, L=L, npair=npair, group=group)
    return pl.pallas_call(
        kern,
        grid=(b, t // L),
        in_specs=[blk] * 8 + [st_spec, par_spec, par_spec],
        out_specs=[blk, st_spec],
        out_shape=[jax.ShapeDtypeStruct((b, npair, t, PAIR), F32),
                   jax.ShapeDtypeStruct((b, npair, PAIR, PAIR), F32)],
        scratch_shapes=[pltpu.VMEM((npair, PAIR, PAIR), F32)],
        compiler_params=_cparams("parallel", "arbitrary"),
        name="rwkv_chunk",
    )(*feats, s0_bd, ln_g.reshape(npair, 1, PAIR), ln_b.reshape(npair, 1, PAIR))


def _state_to_block_diag(s):
    b, h, n, _ = s.shape
    s = s.reshape(b, h // 2, 2, n, n)
    z = jnp.zeros_like(s[:, :, 0])
    top = jnp.concatenate([s[:, :, 0], z], axis=-1)
    bot = jnp.concatenate([z, s[:, :, 1]], axis=-1)
    return jnp.concatenate([top, bot], axis=-2)


def _state_from_block_diag(s_bd):
    b, p, _, _ = s_bd.shape
    n = HEAD_DIM
    return jnp.stack([s_bd[:, :, :n, :n], s_bd[:, :, n:, n:]], axis=2).reshape(b, 2 * p, n, n)


def _t5_bucket(dist):
    exact = N_BUCKETS // 2
    d = jnp.maximum(dist, 0)
    far = exact + (jnp.log(jnp.maximum(d, 1).astype(F32) / exact) / math.log(MAX_DISTANCE / exact)
                   * (N_BUCKETS - exact)).astype(I32)
    return jnp.where(d < exact, d, jnp.minimum(far, N_BUCKETS - 1))


def _bias_tables_body(rb_ref, o_ref, *, offsets, n_heads):
    r = lax.broadcasted_iota(I32, (Q_BLOCK, Q_BLOCK), 0)
    c = lax.broadcasted_iota(I32, (Q_BLOCK, Q_BLOCK), 1)
    for t, (off, key_major) in enumerate(offsets):
        bucket = _t5_bucket((c - r if key_major else r - c) + off)
        for h in range(n_heads):
            def body(bk, acc):
                return jnp.where(bucket == bk, rb_ref[bk, h], acc)
            tile = lax.fori_loop(0, N_BUCKETS, body, jnp.zeros((Q_BLOCK, Q_BLOCK), F32))
            o_ref[t, h] = tile - rb_ref[N_BUCKETS - 1, h]


def bias_tables(rel_bias, offsets):
    n_heads = rel_bias.shape[1]
    kern = functools.partial(_bias_tables_body, offsets=tuple(offsets), n_heads=n_heads)
    return pl.pallas_call(
        kern,
        in_specs=[pl.BlockSpec(memory_space=pltpu.SMEM)],
        out_specs=pl.BlockSpec(memory_space=pltpu.VMEM),
        out_shape=jax.ShapeDtypeStruct((len(offsets), n_heads, Q_BLOCK, Q_BLOCK), F32),
        name="bias_tables",
    )(rel_bias)


def _sortable_key(scores):
    bits = lax.bitcast_convert_type(scores + 0.0, I32)
    return jnp.where(bits < 0, bits ^ 0x7FFFFFFF, bits)


def _count(mask):
    return jnp.sum(mask.astype(F32), axis=-1, keepdims=True)


def _topk_select(key, topk, n_index_bits):
    rows, n = key.shape
    kf = float(topk)
    t0 = jnp.where(_count(key >= 0) >= kf, 0, INT_MIN).astype(I32)

    def value_bit(i, t):
        cand = t + lax.shift_left(jnp.int32(1), 30 - i)
        return jnp.where(_count(key >= cand) >= kf, cand, t)

    thr = lax.fori_loop(0, 31, value_bit, t0)
    above = key > thr
    ties = key == thr
    need = kf - _count(above)
    idx = lax.broadcasted_iota(I32, (rows, n), 1)

    def lowest_ties():
        def index_bit(i, m):
            cand = m + lax.shift_left(jnp.int32(1), n_index_bits - 1 - i)
            return jnp.where(_count(ties & (idx < cand)) <= need, cand, m)
        return lax.fori_loop(0, n_index_bits, index_bit, jnp.zeros((rows, 1), I32))

    surplus = jnp.max(_count(ties) - need) > 0.0
    m = lax.cond(surplus, lowest_ties, lambda: jnp.full((rows, 1), 2 ** n_index_bits, I32))
    return above | (ties & (idx < m))


SUM_CHAINS = 4
HEAD_GROUP = 4


def _sum_rows(x):
    r = x.shape[0]
    if r % (SUM_CHAINS * SUBLANES) == 0 and r > SUM_CHAINS * SUBLANES:
        x = jnp.sum(x.reshape(SUM_CHAINS, r // SUM_CHAINS, x.shape[1]), axis=1)
    return jnp.sum(x, axis=0, keepdims=True)


def _max_rows(x):
    r = x.shape[0]
    if r % (SUM_CHAINS * SUBLANES) == 0 and r > SUM_CHAINS * SUBLANES:
        x = jnp.max(x.reshape(SUM_CHAINS, r // SUM_CHAINS, x.shape[1]), axis=1)
    return jnp.max(x, axis=0, keepdims=True)


def _topk_select_cols(key, topk, n_index_bits):
    n, cols = key.shape
    kf = float(topk)
    cnt = lambda m: _sum_rows(m.astype(F32))
    t0 = jnp.where(cnt(key >= 0) >= kf, 0, INT_MIN).astype(I32)

    def value_bit(i, t):
        cand = t + lax.shift_left(jnp.int32(1), 30 - i)
        return jnp.where(cnt(key >= cand) >= kf, cand, t)

    thr = lax.fori_loop(0, 31, value_bit, t0)
    above = key > thr
    ties = key == thr
    need = kf - cnt(above)
    idx = lax.broadcasted_iota(I32, (n, cols), 0)

    def lowest_ties():
        def index_bit(i, m):
            cand = m + lax.shift_left(jnp.int32(1), n_index_bits - 1 - i)
            return jnp.where(cnt(ties & (idx < cand)) <= need, cand, m)
        return lax.fori_loop(0, n_index_bits, index_bit, jnp.zeros((1, cols), I32))

    surplus = jnp.max(cnt(ties) - need) > 0.0
    m = lax.cond(surplus, lowest_ties, lambda: jnp.full((1, cols), 2 ** n_index_bits, I32))
    return above | (ties & (idx < m))


def _dsa_prompt_block(nb, ik_ref, iqt_ref, wt_ref, k_ref, qt_ref, vt_ref, bias_ref, o_ref, mask_ref,
                      *, n_heads, n_idx_heads, topk):
    w = nb * Q_BLOCK
    ik = ik_ref[0, :w, :]
    idx_dim = ik.shape[1]

    def head_rows(ref, h, dh, cols=slice(None)):
        return ref[0, pl.ds(pl.multiple_of(h * dh, dh), dh), cols]

    def idx_heads(gi, acc):
        hs = [gi * HEAD_GROUP + j for j in range(HEAD_GROUP)]
        dots = [jnp.dot(ik, head_rows(iqt_ref, h, idx_dim), preferred_element_type=F32) for h in hs]
        terms = [jnp.maximum(d, 0.0) * wt_ref[0, h] for d, h in zip(dots, hs)]
        while len(terms) > 1:
            terms = [a + b for a, b in zip(terms[0::2], terms[1::2])]
        return acc + terms[0]

    scores = lax.fori_loop(0, n_idx_heads // HEAD_GROUP, idx_heads, jnp.zeros((w, Q_BLOCK), F32))
    kpos = lax.broadcasted_iota(I32, (w, Q_BLOCK), 0)
    qpos = (nb - 1) * Q_BLOCK + lax.broadcasted_iota(I32, (w, Q_BLOCK), 1)
    valid = kpos <= qpos
    if w <= topk:
        sel = valid
    else:
        key = jnp.where(valid, _sortable_key(scores), INT_MIN)
        sel = valid & _topk_select_cols(key, topk, int(math.ceil(math.log2(w))) + 1)
    mask_ref[:w, :] = jnp.where(sel, 0.0, -jnp.inf)

    def with_near_bias(logits, h):
        near = [logits[w - Q_BLOCK:] + bias_ref[0, h]]
        if nb >= 2:
            near = [logits[w - 2 * Q_BLOCK:w - Q_BLOCK] + bias_ref[1, h]] + near
        if nb >= 3:
            near = [logits[:w - 2 * Q_BLOCK]] + near
        return jnp.concatenate(near, axis=0) if len(near) > 1 else near[0]

    pair_row_half = lax.broadcasted_iota(I32, (PAIR, Q_BLOCK), 0) // HEAD_DIM

    def head_logits(gi, j):
        p = gi * (HEAD_GROUP // 2) + j // 2
        qt_pair = head_rows(qt_ref, p, PAIR)
        qt_head = jnp.where(pair_row_half == j % 2, qt_pair, jnp.zeros_like(qt_pair))
        return jnp.dot(k_ref[0, p, :w, :], qt_head, preferred_element_type=F32)

    def attn_heads(gi, carry):
        hs = [gi * HEAD_GROUP + j for j in range(HEAD_GROUP)]
        mask = mask_ref[:w, :]
        logits = [head_logits(gi, j) + mask for j in range(HEAD_GROUP)]
        logits = [with_near_bias(l_, h) for l_, h in zip(logits, hs)]
        mx = [_max_rows(l_) for l_ in logits]
        e = [jnp.exp(l_ - m_) for l_, m_ in zip(logits, mx)]
        den = [_sum_rows(e_) for e_ in e]
        o = [jnp.dot(head_rows(vt_ref, h, HEAD_DIM, slice(0, w)), e_.astype(BF16),
                     preferred_element_type=F32) for h, e_ in zip(hs, e)]
        for h, o_, d_ in zip(hs, o, den):
            o_ref[0, pl.ds(pl.multiple_of(h * HEAD_DIM, HEAD_DIM), HEAD_DIM), :] = (o_ / d_).astype(o_ref.dtype)
        return carry

    lax.fori_loop(0, n_heads // HEAD_GROUP, attn_heads, 0)


def _dsa_prompt_body(ik_ref, iqt_ref, wt_ref, k_ref, qt_ref, vt_ref, bias_ref, o_ref, mask_ref,
                     *, n_heads, n_idx_heads, seq, topk):
    i = pl.program_id(0)
    for nb in range(1, seq // Q_BLOCK + 1):
        @pl.when(i == nb - 1)
        def _(nb=nb):
            _dsa_prompt_block(nb, ik_ref, iqt_ref, wt_ref, k_ref, qt_ref, vt_ref, bias_ref, o_ref, mask_ref,
                              n_heads=n_heads, n_idx_heads=n_idx_heads, topk=topk)


def dsa_prompt(ik, iqt, wts, k_pairs, qt, vt, bias_tiles, *, topk):
    b, width, s = qt.shape
    h = width // HEAD_DIM
    ih = wts.shape[1]
    kern = functools.partial(_dsa_prompt_body, n_heads=h, n_idx_heads=ih, seq=s, topk=topk)
    grid_spec = pltpu.PrefetchScalarGridSpec(
        num_scalar_prefetch=0,
        grid=(s // Q_BLOCK, b),
        in_specs=[pl.BlockSpec((1, s, ik.shape[2]), lambda i, bi: (bi, 0, 0)),
                  pl.BlockSpec((1, iqt.shape[1], Q_BLOCK), lambda i, bi: (bi, 0, i)),
                  pl.BlockSpec((1, ih, 1, Q_BLOCK), lambda i, bi: (bi, 0, 0, i)),
                  pl.BlockSpec((1, h // 2, s, PAIR), lambda i, bi: (bi, 0, 0, 0)),
                  pl.BlockSpec((1, width, Q_BLOCK), lambda i, bi: (bi, 0, i)),
                  pl.BlockSpec((1, width, s), lambda i, bi: (bi, 0, 0)),
                  pl.BlockSpec(bias_tiles.shape, lambda i, bi: (0, 0, 0, 0))],
        out_specs=pl.BlockSpec((1, width, Q_BLOCK), lambda i, bi: (bi, 0, i)),
        scratch_shapes=[pltpu.VMEM((s, Q_BLOCK), F32)],
    )
    return pl.pallas_call(
        kern,
        grid_spec=grid_spec,
        out_shape=jax.ShapeDtypeStruct((b, width, s), BF16),
        compiler_params=_cparams("arbitrary", "arbitrary"),
        name="dsa_prompt",
    )(ik, iqt, wts, k_pairs, qt, vt, bias_tiles)


def _dsa_select_body(pt_ref, iq_ref, wt_ref, iknew_ref, cidx_hbm, o_ref, ikbuf, sem,
                     *, layer, n_pages, group, n_idx_heads, t_new, topk):
    s = pl.program_id(0)
    past = n_pages * PAGE_SIZE
    n_keys = past + PAGE_SIZE

    def ik_copy(i):
        g = i // n_pages
        p = i % n_pages
        page = pt_ref[(s * group + g) * n_pages + p]
        return pltpu.make_async_copy(cidx_hbm.at[layer, page], ikbuf.at[g, p], sem)

    def ik_start(i, carry):
        ik_copy(i).start()
        return carry

    def ik_wait(i, carry):
        ik_copy(i).wait()
        return carry

    lax.fori_loop(0, group * n_pages, ik_start, 0)
    for g in range(group):
        ikbuf[g, n_pages] = iknew_ref[g]
    lax.fori_loop(0, group * n_pages, ik_wait, 0)

    scores = []
    for g in range(group):
        ikt_all = jnp.concatenate([ikbuf[g, p] for p in range(n_pages + 1)], axis=1)
        dots = _mm(iq_ref[g], ikt_all)
        weighted = jnp.maximum(dots, 0.0) * wt_ref[g]
        scores.append(jnp.sum(weighted.reshape(n_idx_heads, t_new, n_keys), axis=0))
    scores = jnp.concatenate(scores, axis=0)
    shape = (group * t_new, n_keys)
    qpos = past + lax.broadcasted_iota(I32, shape, 0) % t_new
    kpos = lax.broadcasted_iota(I32, shape, 1)
    valid = kpos <= qpos
    key = jnp.where(valid, _sortable_key(scores), INT_MIN)
    sel = valid & _topk_select(key, topk, int(math.log2(n_keys)) + 1)
    o_ref[...] = jnp.where(sel, 0.0, -jnp.inf).reshape(group, t_new, n_keys)


def dsa_sample_select(page_table, iq_rows, wt_rows, ik_new_t, cache_idx_kt, *, layer, n_idx_heads, t_new, topk,
                      group):
    db, n_pages = page_table.shape
    idx_dim = cache_idx_kt.shape[2]
    n_keys = (n_pages + 1) * PAGE_SIZE
    kern = functools.partial(_dsa_select_body, layer=layer, n_pages=n_pages, group=group,
                             n_idx_heads=n_idx_heads, t_new=t_new, topk=topk)
    per_g = lambda shape: pl.BlockSpec((group,) + shape, lambda si, pt: (si,) + (0,) * len(shape))
    grid_spec = pltpu.PrefetchScalarGridSpec(
        num_scalar_prefetch=1,
        grid=(db // group,),
        in_specs=[per_g((n_idx_heads * t_new, idx_dim)), per_g((n_idx_heads * t_new, 1)),
                  per_g((idx_dim, PAGE_SIZE)), pl.BlockSpec(memory_space=pl.ANY)],
        out_specs=per_g((t_new, n_keys)),
        scratch_shapes=[pltpu.VMEM((group, n_pages + 1, idx_dim, PAGE_SIZE), F32), pltpu.SemaphoreType.DMA(())],
    )
    return pl.pallas_call(
        kern,
        grid_spec=grid_spec,
        out_shape=jax.ShapeDtypeStruct((db, t_new, n_keys), F32),
        compiler_params=_cparams("arbitrary"),
        name="dsa_sample_select",
    )(page_table.reshape(-1), iq_rows, wt_rows, ik_new_t, cache_idx_kt)


def _dsa_sample_body(pt_ref, mask_ref, qbd_ref, knew_ref, vnew_ref,
                     blast_ref, bnew_ref, ck_hbm, cv_hbm, o_ref,
                     kbuf, vbuf, sem_k, sem_v,
                     *, layer, n_pages, chunk, n_heads, t_new):
    b = pl.program_id(0)
    n_seq = pl.num_programs(0)
    past = n_pages * PAGE_SIZE
    n_chunks = n_pages // chunk
    rows = n_heads * t_new
    ck = chunk * PAGE_SIZE

    def kv_copies(seq, c, j):
        slot = c % 2
        page = pt_ref[seq * n_pages + c * chunk + j]
        return (pltpu.make_async_copy(ck_hbm.at[layer, page], kbuf.at[slot, j], sem_k.at[slot]),
                pltpu.make_async_copy(cv_hbm.at[layer, page], vbuf.at[slot, j], sem_v.at[slot]))

    def start_chunk(seq, c):
        for j in range(chunk):
            kc, vc = kv_copies(seq, c, j)
            kc.start()
            vc.start()

    def wait_chunk(c):
        for j in range(chunk):
            kc, vc = kv_copies(b, c, j)
            kc.wait()
            vc.wait()

    @pl.when(b == 0)
    def _():
        start_chunk(0, 0)

    sel_rows = jnp.tile(mask_ref[0], (n_heads, 1))

    q_rep = jnp.tile(qbd_ref[0], (n_heads, 1))
    row_head = lax.broadcasted_iota(I32, q_rep.shape, 0) // t_new
    col_head = lax.broadcasted_iota(I32, q_rep.shape, 1) // HEAD_DIM
    qbd = jnp.where(row_head == col_head, q_rep, jnp.zeros_like(q_rep))
    neg = -1e30

    def update(state, logits, maskc, vt_bf16):
        m, l, acc = state
        s = logits + maskc
        m_new = jnp.maximum(m, jnp.max(s, axis=-1, keepdims=True))
        alpha = jnp.exp(m - m_new)
        p = jnp.exp(s - m_new)
        l = alpha * l + jnp.sum(p, axis=-1, keepdims=True)
        acc = alpha * acc + _mm_nt(p, vt_bf16)
        return m_new, l, acc

    def pages_t(buf, slot):
        return jnp.concatenate([buf[slot, j] for j in range(chunk)], axis=1).astype(BF16)

    state = (jnp.full((rows, 1), neg, F32), jnp.zeros((rows, 1), F32),
             jnp.zeros((rows, qbd.shape[1]), F32))
    for c in range(n_chunks):
        if c + 1 < n_chunks:
            start_chunk(b, c + 1)
        else:
            @pl.when(b + 1 < n_seq)
            def _():
                start_chunk(b + 1, 0)
        wait_chunk(c)
        slot = c % 2
        logits = _mm(qbd, pages_t(kbuf, slot))
        if c == n_chunks - 1:
            logits = jnp.concatenate([logits[:, :ck - PAGE_SIZE],
                                      logits[:, ck - PAGE_SIZE:] + blast_ref[...]], axis=1)
        state = update(state, logits, sel_rows[:, c * ck:(c + 1) * ck], pages_t(vbuf, slot))
    logits = _mm(qbd, knew_ref[0]) + bnew_ref[...]
    m, l, acc = update(state, logits, sel_rows[:, past:], vnew_ref[0])
    out = jnp.where(row_head == col_head, acc / l, 0.0)
    o_ref[0] = jnp.sum(out.reshape(n_heads, t_new, out.shape[1]), axis=0)


def dsa_sample(page_table, mask, q_bd, k_new_t, v_new_t, bias_last, bias_new, cache_kt, cache_vt,
               *, layer, n_heads, t_new, chunk):
    db, n_pages = page_table.shape
    assert (n_pages // chunk) % 2 == 0, "chunks alternate between two buffers across sequences"
    rows = n_heads * t_new
    width = q_bd.shape[2]
    kern = functools.partial(_dsa_sample_body, layer=layer, n_pages=n_pages, chunk=chunk, n_heads=n_heads,
                             t_new=t_new)
    per_b = lambda shape: pl.BlockSpec((1,) + shape, lambda bi, pt: (bi,) + (0,) * len(shape))
    const = lambda shape: pl.BlockSpec(shape, lambda bi, pt: (0,) * len(shape))
    any_spec = pl.BlockSpec(memory_space=pl.ANY)
    grid_spec = pltpu.PrefetchScalarGridSpec(
        num_scalar_prefetch=1,
        grid=(db,),
        in_specs=[per_b((t_new, mask.shape[2])),
                  per_b((t_new, width)), per_b((width, PAGE_SIZE)), per_b((width, PAGE_SIZE)),
                  const((rows, PAGE_SIZE)), const((rows, PAGE_SIZE)),
                  any_spec, any_spec],
        out_specs=per_b((t_new, width)),
        scratch_shapes=[pltpu.VMEM((2, chunk, width, PAGE_SIZE), F32),
                        pltpu.VMEM((2, chunk, width, PAGE_SIZE), F32),
                        pltpu.SemaphoreType.DMA((2,)),
                        pltpu.SemaphoreType.DMA((2,))],
    )
    return pl.pallas_call(
        kern,
        grid_spec=grid_spec,
        out_shape=jax.ShapeDtypeStruct((db, t_new, width), F32),
        compiler_params=_cparams("arbitrary"),
        name="dsa_sample",
    )(page_table.reshape(-1), mask, q_bd, k_new_t, v_new_t, bias_last, bias_new, cache_kt, cache_vt)


def _token_minor_cache(cache):
    l, pool, page = cache.shape[:3]
    nd = cache.ndim
    return cache.transpose((0, 1) + tuple(range(3, nd)) + (2,)).reshape(l, pool, -1, page)


def _dsa_sample_inputs(q, k_new, v_new, iq, ik_new, iw, bias_tiles):
    db, t, h, dh = q.shape
    ih = iq.shape[2]
    iq_rows = iq.transpose(0, 2, 1, 3).reshape(db, ih * t, -1).astype(BF16)
    wt_rows = (iw * (ih ** -0.5 * iq.shape[3] ** -0.5)).transpose(0, 2, 1).reshape(db, ih * t, 1)
    q_bd = q.reshape(db, t, h * dh)
    page_t = lambda x: jnp.pad(x.reshape(db, t, -1).transpose(0, 2, 1), ((0, 0), (0, 0), (0, PAGE_SIZE - t)))
    bias_new = bias_tiles[0, :, :t, :].reshape(h * t, Q_BLOCK)
    bias_last = bias_tiles[1, :, :t, :].reshape(h * t, Q_BLOCK)
    return ((iq_rows, wt_rows, page_t(ik_new)),
            (q_bd.astype(BF16), page_t(k_new).astype(BF16), page_t(v_new).astype(BF16), bias_last, bias_new))


def _matmul_residual_body(x_ref, a_ref, b_ref, w_ref, o_ref):
    ka = a_ref.shape[1]
    o_ref[...] = x_ref[...] + (_mm(a_ref[...], w_ref[:ka, :]) + _mm(b_ref[...], w_ref[ka:, :]))


def matmul_residual(x, a, b, w, *, tm, tn):
    n, d = x.shape
    ka, kb = a.shape[1], b.shape[1]
    return pl.pallas_call(
        _matmul_residual_body,
        grid=(n // tm, d // tn),
        in_specs=[pl.BlockSpec((tm, tn), lambda i, j: (i, j)),
                  pl.BlockSpec((tm, ka), lambda i, j: (i, 0)),
                  pl.BlockSpec((tm, kb), lambda i, j: (i, 0)),
                  pl.BlockSpec((ka + kb, tn), lambda i, j: (0, j))],
        out_specs=pl.BlockSpec((tm, tn), lambda i, j: (i, j)),
        out_shape=jax.ShapeDtypeStruct((n, d), F32),
        compiler_params=_cparams("parallel", "parallel"),
        name="matmul_residual",
    )(x, a, b, w)


def _cross_attn_body(x_ref, g_ref, wq_ref, mk_ref, mv_ref, wo_ref, o_ref, *, groups, t_rows, n_heads, head_dim):
    x = x_ref[...]
    h = _rmsnorm(x, g_ref[...]).astype(BF16)
    q = jnp.dot(h, wq_ref[...], preferred_element_type=F32).astype(BF16)
    scale = head_dim ** -0.5
    outs = []
    for gi in range(groups):
        qg = q[gi * t_rows:(gi + 1) * t_rows]
        heads = []
        for hh in range(n_heads):
            sl = slice(hh * head_dim, (hh + 1) * head_dim)
            logits = _mm_nt(qg[:, sl], mk_ref[gi, :, sl]) * scale
            mx = jnp.max(logits, axis=-1, keepdims=True)
            e = jnp.exp(logits - mx)
            p = e / jnp.sum(e, axis=-1, keepdims=True)
            heads.append(_mm(p, mv_ref[gi, :, sl]))
        outs.append(jnp.concatenate(heads, axis=1))
    o = jnp.concatenate(outs, axis=0) if groups > 1 else outs[0]
    o_ref[...] = x + jnp.dot(o.astype(BF16), wo_ref[...], preferred_element_type=F32)


def cross_attn(x, g, wq, mk, mv, wo, *, groups, t_rows, seq_tiles, n_heads):
    n, d = x.shape
    xw = wq.shape[1]
    rows = groups * t_rows
    m = mk.shape[1]
    kern = functools.partial(_cross_attn_body, groups=groups, t_rows=t_rows, n_heads=n_heads,
                             head_dim=xw // n_heads)
    return pl.pallas_call(
        kern,
        grid=(n // rows,),
        in_specs=[pl.BlockSpec((rows, d), lambda i: (i, 0)),
                  pl.BlockSpec((1, d), lambda i: (0, 0)),
                  pl.BlockSpec((d, xw), lambda i: (0, 0)),
                  pl.BlockSpec((groups, m, xw), lambda i: (i // seq_tiles, 0, 0)),
                  pl.BlockSpec((groups, m, xw), lambda i: (i // seq_tiles, 0, 0)),
                  pl.BlockSpec((xw, d), lambda i: (0, 0))],
        out_specs=pl.BlockSpec((rows, d), lambda i: (i, 0)),
        out_shape=jax.ShapeDtypeStruct((n, d), F32),
        compiler_params=_cparams("parallel"),
        name="cross_attn",
    )(x, g.reshape(1, d), wq, mk, mv, wo)


def _router_body(x_ref, g_ref, wr_ref, br_ref, h_ref, r_ref, *, n_groups, per_group):
    h = _rmsnorm(x_ref[...], g_ref[...])
    _store_slab_rows(h_ref, h, h.shape[1] // LANES)
    logits = _mm3(h, wr_ref[...]) + br_ref[...]
    lane = lax.broadcasted_iota(I32, logits.shape, 1).astype(F32)
    big = 1e9
    first_lane = lambda hit: jnp.min(jnp.where(hit, lane, big), axis=-1, keepdims=True)
    gl = jnp.where(lane < n_groups, logits, -jnp.inf)
    gmax = jnp.max(gl, axis=-1, keepdims=True)
    grp = first_lane(gl == gmax)
    p_grp = 1.0 / jnp.sum(jnp.exp(gl - gmax), axis=-1, keepdims=True)
    e_id = lane - n_groups
    in_grp = (e_id >= grp * per_group) & (e_id < (grp + 1.0) * per_group)
    el = jnp.where(in_grp, logits, -jnp.inf)
    v1 = jnp.max(el, axis=-1, keepdims=True)
    i1 = first_lane(el == v1) - n_groups
    el2 = jnp.where(e_id == i1, -jnp.inf, el)
    v2 = jnp.max(el2, axis=-1, keepdims=True)
    i2 = first_lane(el2 == v2) - n_groups
    e2 = jnp.exp(v2 - v1)
    g1 = p_grp / (1.0 + e2)
    g2 = p_grp * e2 / (1.0 + e2)
    r_ref[...] = jnp.where(lane == 0, g1, jnp.where(lane == 1, g2, jnp.where(
        lane == 2, i1, jnp.where(lane == 3, i2, 0.0))))


def router(x, g, w_r, b_r, *, tm, n_groups, per_group):
    n, d = x.shape
    kern = functools.partial(_router_body, n_groups=n_groups, per_group=per_group)
    return pl.pallas_call(
        kern,
        grid=(n // tm,),
        in_specs=[pl.BlockSpec((tm, d), lambda i: (i, 0)),
                  pl.BlockSpec((1, d), lambda i: (0, 0)),
                  pl.BlockSpec((d, LANES), lambda i: (0, 0)),
                  pl.BlockSpec((1, LANES), lambda i: (0, 0))],
        out_specs=[pl.BlockSpec((tm * (d // LANES), LANES), lambda i: (i, 0)),
                   pl.BlockSpec((tm, LANES), lambda i: (i, 0))],
        out_shape=[jax.ShapeDtypeStruct((n * (d // LANES), LANES), F32), jax.ShapeDtypeStruct((n, LANES), F32)],
        compiler_params=_cparams("parallel"),
        name="moe_router",
    )(x, g.reshape(1, d), w_r, b_r)


def _slab_rows(buf, offset, n_rows, n_slab, stride):
    return jnp.concatenate([buf[pl.ds(offset + s, n_rows, stride=stride), :] for s in range(n_slab)], axis=1)


def _store_slab_rows(buf, x, n_slab):
    n_rows = x.shape[0]
    for s in range(n_slab):
        buf[pl.ds(s, n_rows, stride=n_slab), :] = x[:, s * LANES:(s + 1) * LANES]


def _moe_ffn_body(te_ref, nr_ref, tok_ref, dst_ref, h_hbm, w1_ref, w3_ref, w2_ref, y_hbm,
                  xbuf, ybuf, gsem, ssem, *, tm):
    t = pl.program_id(0)
    n_tiles = pl.num_programs(0)
    n_slab = h_hbm.shape[1]
    slot = t % 2

    def slab(buf, sl, r):
        return buf.at[sl, pl.ds(pl.multiple_of(r * n_slab, n_slab), n_slab)]

    def gather_copy(tile, sl, r):
        return pltpu.make_async_copy(h_hbm.at[tok_ref[tile * tm + r]], slab(xbuf, sl, r), gsem.at[sl])

    def scatter_copy(tile, sl, r):
        return pltpu.make_async_copy(slab(ybuf, sl, r), y_hbm.at[dst_ref[tile * tm + r]], ssem.at[sl])

    def start_rows(n, copy):
        def pair(i, c):
            copy(2 * i).start(priority=0)
            copy(2 * i + 1).start(priority=1)
            return c
        lax.fori_loop(0, n // 2, pair, 0)

        @pl.when(n % 2 == 1)
        def _():
            copy(n - 1).start(priority=0)

    def wait_rows(n, copy):
        def one(r, c):
            copy(r).wait()
            return c
        lax.fori_loop(0, n, one, 0)

    @pl.when(t == 0)
    def _():
        xbuf[...] = jnp.zeros(xbuf.shape, F32)
        start_rows(nr_ref[0], lambda r: gather_copy(0, 0, r))

    nxt = jnp.minimum(t + 1, n_tiles - 1)

    @pl.when(t + 1 < n_tiles)
    def _():
        start_rows(nr_ref[nxt], lambda r: gather_copy(nxt, 1 - slot, r))

    @pl.when(t >= 2)
    def _():
        prev2 = jnp.maximum(t - 2, 0)
        wait_rows(nr_ref[prev2], lambda r: scatter_copy(prev2, slot, r))

    n = nr_ref[t]

    @pl.when(n > 0)
    def _():
        wait_rows(n, lambda r: gather_copy(t, slot, r))
        x = _slab_rows(xbuf.at[slot], 0, tm, n_slab, n_slab).astype(BF16)
        a = jnp.dot(x, w1_ref[0].astype(BF16), preferred_element_type=F32)
        bgate = jnp.dot(x, w3_ref[0].astype(BF16), preferred_element_type=F32)
        u = (a / (1.0 + jnp.exp(-a))) * bgate
        y = jnp.dot(u.astype(BF16), w2_ref[0].astype(BF16), preferred_element_type=F32)
        _store_slab_rows(ybuf.at[slot], y, n_slab)
        start_rows(n, lambda r: scatter_copy(t, slot, r))

    @pl.when(t == n_tiles - 1)
    def _():
        @pl.when(t >= 1)
        def _():
            prev1 = jnp.maximum(t - 1, 0)
            wait_rows(nr_ref[prev1], lambda r: scatter_copy(prev1, 1 - slot, r))
        wait_rows(n, lambda r: scatter_copy(t, slot, r))


def moe_ffn(tile_expert, tile_rows, row_tok, row_dst, h_slabs, n_out_rows, w1, w3, w2, *, tm):
    n_tiles = tile_expert.shape[0]
    d, de = w1.shape[1], w1.shape[2]
    n_slab = d // LANES
    h3 = h_slabs.reshape(-1, n_slab, LANES)
    w_in_spec = pl.BlockSpec((1, d, de), lambda t, te, nr, tok, dst: (te[t], 0, 0))
    grid_spec = pltpu.PrefetchScalarGridSpec(
        num_scalar_prefetch=4,
        grid=(n_tiles,),
        in_specs=[pl.BlockSpec(memory_space=pl.ANY), w_in_spec, w_in_spec,
                  pl.BlockSpec((1, de, d), lambda t, te, nr, tok, dst: (te[t], 0, 0))],
        out_specs=pl.BlockSpec(memory_space=pl.ANY),
        scratch_shapes=[pltpu.VMEM((2, tm * n_slab, LANES), F32), pltpu.VMEM((2, tm * n_slab, LANES), F32),
                        pltpu.SemaphoreType.DMA((2,)), pltpu.SemaphoreType.DMA((2,))],
    )
    y = pl.pallas_call(
        functools.partial(_moe_ffn_body, tm=tm),
        grid_spec=grid_spec,
        out_shape=jax.ShapeDtypeStruct((n_out_rows, n_slab, LANES), F32),
        compiler_params=_cparams("arbitrary"),
        name="moe_ffn",
    )(tile_expert, tile_rows, row_tok, row_dst, h3, w1, w3, w2)
    return y.reshape(n_out_rows * n_slab, LANES)


def _moe_dispatch(eids, n_experts, tm):
    n, k = eids.shape
    m = n * k
    flat_e = eids.reshape(-1)
    order = jnp.argsort(flat_e, stable=True).astype(I32)
    counts = jnp.sum((flat_e[:, None] == jnp.arange(n_experts)[None, :]).astype(I32), axis=0)
    padded = (counts + tm - 1) // tm * tm
    pad_end = jnp.cumsum(padded)
    pad_start = pad_end - padded
    start = jnp.cumsum(counts) - counts
    n_tiles = -(-m // tm) + n_experts
    tile_start = jnp.arange(n_tiles) * tm
    last = jnp.maximum(pad_end[-1] - 1, 0)
    tile_expert = jnp.minimum(jnp.searchsorted(pad_end, jnp.minimum(tile_start, last), side='right'),
                              n_experts - 1).astype(I32)
    first = tile_start - pad_start[tile_expert]
    tile_rows = jnp.where(tile_start < pad_end[-1], jnp.clip(counts[tile_expert] - first, 0, tm), 0).astype(I32)
    src = jnp.clip(start[tile_expert][:, None] + first[:, None] + jnp.arange(tm)[None, :], 0, m - 1)
    row_dst = order[src].reshape(-1)
    return tile_expert, tile_rows, (row_dst // k).astype(I32), row_dst.astype(I32)


def _combine_body(x_ref, r_ref, g_ref, y_ref, o_ref, *, tm, top_k):
    n_slab = x_ref.shape[1] // LANES
    route = r_ref[...]
    x = x_ref[...]
    for kk in range(top_k):
        x = x + _slab_rows(y_ref, kk * n_slab, tm, n_slab, top_k * n_slab) * route[:, kk:kk + 1]
    o_ref[...] = _rmsnorm(x, g_ref[...])


def moe_combine(x, route, g, y_slabs, *, tm, top_k, tile_offset):
    n, d = x.shape
    rows = tm * top_k * (d // LANES)
    return pl.pallas_call(
        functools.partial(_combine_body, tm=tm, top_k=top_k),
        grid=(n // tm,),
        in_specs=[pl.BlockSpec((tm, d), lambda i: (i, 0)),
                  pl.BlockSpec((tm, LANES), lambda i: (i, 0)),
                  pl.BlockSpec((1, d), lambda i: (0, 0)),
                  pl.BlockSpec((rows, LANES), lambda i: (i + tile_offset, 0))],
        out_specs=pl.BlockSpec((tm, d), lambda i: (i, 0)),
        out_shape=jax.ShapeDtypeStruct((n, d), F32),
        compiler_params=_cparams("parallel"),
        name="moe_combine",
    )(x, route, g.reshape(1, d), y_slabs)


def kernel(x_prompt, x_sample, mem_prompt, cache_k, cache_v, cache_idx_k, page_table, state_wkv, state_shift, cache_mem_k, cache_mem_v, g_mix, w_in, mu_shift, rw_w0, rw_w2, rw_a0, rw_a2, rw_g2, rw_kk, rw_ka, rw_rk, rw_ln_g, rw_ln_b, w_out, g_cross, g_mem, w_cq, w_ck, w_cv, w_co, g_ffn, w_rg, b_rg, w_re, b_re, w_e1, w_e3, w_e2, rel_bias, g_final):
    B, S, D = x_prompt.shape
    DB, T, _ = x_sample.shape
    assert w_in.shape[0] == 1, "single-layer trunk only"
    l = 0
    n_pages = page_table.shape[1]
    past = n_pages * PAGE_SIZE
    topk_p = min(TOPK_MAX, S // 4)
    topk_s = min(TOPK_MAX, (past + T) // 4)
    rw_proj = mu_shift.shape[1]
    width = rw_w0.shape[1]
    at_w = D - width
    n_heads = at_w // HEAD_DIM
    idx_dim = cache_idx_k.shape[-1]
    ih = (w_in.shape[2] - rw_proj - 3 * at_w - idx_dim) // (idx_dim + 1)
    xw = w_cq.shape[2]
    x_heads = cache_mem_k.shape[3]
    n_mem = mem_prompt.shape[1]
    n_experts = w_e1.shape[1]
    top_k = 2
    tn = PROJ_COLS
    n_main = 3 * width
    n_lora = rw_proj - n_main
    n_att = 3 * at_w + ih * idx_dim
    n_tail = n_lora + idx_dim + ih
    assert n_main % tn == 0 and at_w % tn == 0 and (ih * idx_dim) % tn == 0 and n_tail <= tn

    w_t = w_in[l].T
    w_all = jnp.concatenate([w_t[:n_main], w_t[rw_proj:rw_proj + at_w] * HEAD_DIM ** -0.5,
                             w_t[rw_proj + at_w:rw_proj + n_att], w_t[n_main:rw_proj],
                             w_t[rw_proj + n_att:], jnp.zeros((tn - n_tail, D), w_t.dtype)], axis=0).astype(BF16)
    flat, per_head = False, True
    segments = [(n_main // tn, [(flat, F32)]),
                (at_w // tn, [(flat, BF16)]),
                (at_w // tn, [(per_head, F32), (flat, BF16)]),
                (at_w // tn, [(per_head, F32), (flat, BF16)]),
                (ih * idx_dim // tn, [(flat, BF16)]),
                (1, [(flat, F32)])]
    w_out_b = w_out[l].astype(BF16)
    w_cq_b, w_co_b = w_cq[l].astype(BF16), w_co[l].astype(BF16)
    w_ckv = jnp.concatenate([w_ck[l], w_cv[l]], axis=1).astype(BF16)
    n_route = w_rg.shape[2] + w_re.shape[2]
    w_r = jnp.pad(jnp.concatenate([w_rg[l], w_re[l]], axis=1), ((0, 0), (0, LANES - n_route)))
    b_r = jnp.pad(jnp.concatenate([b_rg[l], b_re[l]]), (0, LANES - n_route)).reshape(1, LANES)
    rw_args = (mu_shift[l], rw_w0[l], rw_w2[l], rw_a0[l], rw_a2[l], rw_g2[l], rw_kk[l], rw_ka[l],
               rw_rk[l].reshape(-1))
    tiles = bias_tables(rel_bias, ((0, False), (Q_BLOCK, False), (0, True), (Q_BLOCK, True)))

    def project(x2d, b_, t_, tm):
        f_main, q, k_heads, k, v_heads, v, iq, f_tail = norm_matmul_split(x2d, g_mix[l], w_all, segments,
                                                                          tm=tm, tn=tn)
        r3 = lambda z: z.reshape(b_, t_, -1)
        f_main, f_tail = r3(f_main), r3(f_tail)
        ik = f_tail[..., n_lora:n_lora + idx_dim]
        iw = f_tail[..., n_lora + idx_dim:n_tail]
        shift = jnp.concatenate([f_main[:, -1], f_tail[:, -1, :n_lora]], axis=-1)
        heads5 = lambda z: z.reshape(1, b_, t_, n_heads, HEAD_DIM)
        return f_main, r3(q), heads5(k_heads), r3(k), heads5(v_heads), r3(v), r3(iq), f_tail, ik, iw, shift

    def rw_rows(y):
        b_, p_, t_, _ = y.shape
        return y.transpose(0, 2, 1, 3).reshape(b_ * t_, p_ * PAIR).astype(BF16)

    xp = x_prompt.reshape(B * S, D)
    fm_p, q, k_p, kb_p, v_p, vb_p, iq, ft_p, ik_p, iw, shift_p = project(xp, B, S, PROJ_ROWS)
    feats_p = rwkv_prep(fm_p, ft_p, jnp.zeros((B, rw_proj), F32), *rw_args, tm=PREP_ROWS, width=width)
    rw_p, st_p = rwkv_chunk(feats_p, jnp.zeros((B, width // PAIR, PAIR, PAIR), F32), rw_ln_g[l], rw_ln_b[l],
                            L=RWKV_CHUNK)
    tr = lambda z: z.transpose(0, 2, 1)
    k_pairs = kb_p.reshape(B, S, n_heads // 2, PAIR).transpose(0, 2, 1, 3)
    at_p = dsa_prompt(ik_p.astype(BF16), tr(iq), tr(iw * (ih ** -0.5 * idx_dim ** -0.5))[:, :, None, :],
                      k_pairs, tr(q), tr(vb_p), tiles[2:4], topk=topk_p)
    x1_p = matmul_residual(xp, rw_rows(rw_p), tr(at_p).reshape(B * S, at_w), w_out_b, tm=ROW_TILE, tn=D)
    mkv = norm_matmul(mem_prompt.reshape(B * n_mem, D), g_mem[l], w_ckv, tm=MEM_ROWS, tn=xw)
    mk_p = mkv[:, :xw].reshape(B, n_mem, xw)
    mv_p = mkv[:, xw:].reshape(B, n_mem, xw)
    x2_p = cross_attn(x1_p, g_cross[l], w_cq_b, mk_p, mv_p, w_co_b,
                      groups=1, t_rows=ROW_TILE, seq_tiles=S // ROW_TILE, n_heads=x_heads)

    xs = x_sample.reshape(DB * T, D)
    fm_s, q2, k_s, kb_s, v_s, vb_s, iq2, ft_s, ik_s, iw2, shift_s = project(xs, DB, T, DB * T)
    feats_s = rwkv_prep(fm_s, ft_s, state_shift[l], *rw_args, tm=T, width=width)
    rw_s, st_s = rwkv_chunk(feats_s, _state_to_block_diag(state_wkv[l]), rw_ln_g[l], rw_ln_b[l], L=T)
    r4 = lambda z, h_: z.reshape(DB, T, h_, -1)
    sel_args, att_args = _dsa_sample_inputs(r4(q2, n_heads), r4(kb_s, n_heads), r4(vb_s, n_heads), r4(iq2, ih),
                                            ik_s, iw2, tiles[0:2])
    mask_s = dsa_sample_select(page_table, *sel_args, _token_minor_cache(cache_idx_k), layer=l,
                               n_idx_heads=ih, t_new=T, topk=topk_s, group=SELECT_GROUP)
    at_s = dsa_sample(page_table, mask_s, *att_args, _token_minor_cache(cache_k), _token_minor_cache(cache_v),
                      layer=l, n_heads=n_heads, t_new=T, chunk=SAMPLE_CHUNK_PAGES)
    x1_s = matmul_residual(xs, rw_rows(rw_s), at_s.reshape(DB * T, at_w), w_out_b, tm=DB * T, tn=D)
    x2_s = cross_attn(x1_s, g_cross[l], w_cq_b, cache_mem_k[l].reshape(DB, n_mem, xw),
                      cache_mem_v[l].reshape(DB, n_mem, xw), w_co_b,
                      groups=CROSS_GROUP, t_rows=T, seq_tiles=1, n_heads=x_heads)

    h_p, route_p = router(x2_p, g_ffn[l], w_r, b_r, tm=ROW_TILE, n_groups=w_rg.shape[2],
                          per_group=w_re.shape[2] // w_rg.shape[2])
    h_s, route_s = router(x2_s, g_ffn[l], w_r, b_r, tm=DB * T, n_groups=w_rg.shape[2],
                          per_group=w_re.shape[2] // w_rg.shape[2])
    h_all = jnp.concatenate([h_p, h_s], axis=0)
    eids = jnp.concatenate([route_p[:, top_k:2 * top_k], route_s[:, top_k:2 * top_k]], axis=0).astype(I32)
    tm_moe = DB * T
    n_p, n_all = B * S, B * S + DB * T
    assert n_p % tm_moe == 0
    tile_expert, tile_rows, row_tok, row_dst = _moe_dispatch(eids, n_experts, tm_moe)
    y_slabs = moe_ffn(tile_expert, tile_rows, row_tok, row_dst, h_all, n_all * top_k,
                      w_e1[l], w_e3[l], w_e2[l], tm=tm_moe)
    y_p = moe_combine(x2_p, route_p, g_final, y_slabs, tm=tm_moe, top_k=top_k, tile_offset=0)
    y_s = moe_combine(x2_s, route_s, g_final, y_slabs, tm=tm_moe, top_k=top_k, tile_offset=n_p // tm_moe)

    return (y_p.reshape(B, S, D), y_s.reshape(DB, T, D),
            k_p, v_p, ik_p[None], _state_from_block_diag(st_p)[None], shift_p[None],
            mk_p.reshape(1, B, n_mem, x_heads, xw // x_heads), mv_p.reshape(1, B, n_mem, x_heads, xw // x_heads),
            k_s, v_s, ik_s[None], _state_from_block_diag(st_s)[None], shift_s[None])
```

```python
import functools
import math

import jax
import jax.numpy as jnp
from jax import lax
from jax.experimental import pallas as pl
from jax.experimental.pallas import tpu as pltpu

F32 = jnp.float32
BF16 = jnp.bfloat16
I32 = jnp.int32

LANES = 128
SUBLANES = 8
VMEM_LIMIT_BYTES = 56 * 1024 * 1024

HEAD_DIM = 64
PAIR = 2 * HEAD_DIM
GN_EPS = 64e-5
NORM_EPS = 1e-6
TOPK_MAX = 256
Q_BLOCK = 128
N_BUCKETS = 32
MAX_DISTANCE = 128
PAGE_SIZE = 128
INT_MIN = -(2 ** 31)

PROJ_ROWS = 1024
PROJ_COLS = 512
PREP_ROWS = 512
RWKV_CHUNK = 64
ROW_TILE = 512
MEM_ROWS = 256
SAMPLE_CHUNK_PAGES = 8
SELECT_GROUP = 8
CROSS_GROUP = 8


def _cparams(*sem):
    return pltpu.CompilerParams(dimension_semantics=sem, vmem_limit_bytes=VMEM_LIMIT_BYTES)


def _mm(a, b):
    return jnp.dot(a.astype(BF16), b.astype(BF16), preferred_element_type=F32)


def _mm_nt(a, b):
    return lax.dot_general(a.astype(BF16), b.astype(BF16), (((1,), (1,)), ((), ())),
                           preferred_element_type=F32)


def _split2(x):
    hi = x.astype(BF16)
    lo = (x - hi.astype(F32)).astype(BF16)
    return hi, lo


def _mm3(a, b):
    ah, al = _split2(a)
    bh, bl = _split2(b)
    d = lambda x, y: jnp.dot(x, y, preferred_element_type=F32)
    return d(ah, bh) + (d(ah, bl) + d(al, bh))


def _mm_exact_rhs(a, b_bf16):
    hi = a.astype(BF16)
    r1 = a - hi.astype(F32)
    mid = r1.astype(BF16)
    lo = (r1 - mid.astype(F32)).astype(BF16)
    d = lambda x: jnp.dot(x, b_bf16, preferred_element_type=F32)
    return d(hi) + (d(mid) + d(lo))


def _rmsnorm(x, g):
    ms = jnp.mean(x * x, axis=-1, keepdims=True)
    return x * lax.rsqrt(ms + NORM_EPS) * g


def _norm_matmul_body(x_ref, g_ref, w_ref, o_ref, xn_ref):
    @pl.when(pl.program_id(1) == 0)
    def _():
        xn_ref[...] = _rmsnorm(x_ref[...], g_ref[...]).astype(BF16)

    o_ref[...] = _mm(xn_ref[...], w_ref[...])


def norm_matmul(x, g, w, *, tm, tn):
    n, d = x.shape
    m = w.shape[1]
    return pl.pallas_call(
        _norm_matmul_body,
        grid=(n // tm, m // tn),
        in_specs=[pl.BlockSpec((tm, d), lambda i, j: (i, 0)),
                  pl.BlockSpec((1, d), lambda i, j: (0, 0)),
                  pl.BlockSpec((d, tn), lambda i, j: (0, j))],
        out_specs=pl.BlockSpec((tm, tn), lambda i, j: (i, j)),
        out_shape=jax.ShapeDtypeStruct((n, m), F32),
        scratch_shapes=[pltpu.VMEM((tm, d), BF16)],
        compiler_params=_cparams("parallel", "arbitrary"),
        name="norm_matmul",
    )(x, g.reshape(1, d), w)


def _norm_matmul_split_body(x_ref, g_ref, w_ref, *rest, bounds):
    o_refs, xn_ref = rest[:-1], rest[-1]
    j = pl.program_id(1)

    @pl.when(j == 0)
    def _():
        xn_ref[...] = _rmsnorm(x_ref[...], g_ref[...]).astype(BF16)

    res = _mm_nt(xn_ref[...], w_ref[...])
    tm, tn = res.shape
    for o_ref, (lo, hi) in zip(o_refs, bounds):
        @pl.when((j >= lo) & (j < hi))
        def _(o_ref=o_ref, lo=lo, hi=hi):
            if o_ref.shape[1] == HEAD_DIM:
                n_heads = (hi - lo) * tn // HEAD_DIM
                per_tile = tn // HEAD_DIM
                for c in range(hi - lo):
                    @pl.when(j == lo + c)
                    def _(c=c):
                        for hh in range(per_tile):
                            o_ref[pl.ds(c * per_tile + hh, tm, stride=n_heads), :] = (
                                res[:, hh * HEAD_DIM:(hh + 1) * HEAD_DIM].astype(o_ref.dtype))
            else:
                o_ref[...] = res.astype(o_ref.dtype)


def norm_matmul_split(x, g, wt, segments, *, tm, tn):
    n, d = x.shape
    m = wt.shape[0]
    bounds, out_specs, out_shape, lo = [], [], [], 0
    for nt, outs in segments:
        hi = lo + nt
        for per_head, dt in outs:
            bounds.append((lo, hi))
            if per_head:
                heads = nt * tn // HEAD_DIM
                out_specs.append(pl.BlockSpec((tm * heads, HEAD_DIM), lambda i, j: (i, 0),
                                              pipeline_mode=pl.Buffered(1)))
                out_shape.append(jax.ShapeDtypeStruct((n * heads, HEAD_DIM), dt))
            else:
                out_specs.append(pl.BlockSpec((tm, tn),
                                              lambda i, j, lo=lo, hi=hi: (i, jnp.clip(j - lo, 0, hi - lo - 1))))
                out_shape.append(jax.ShapeDtypeStruct((n, nt * tn), dt))
        lo = hi
    assert lo * tn == m
    return pl.pallas_call(
        functools.partial(_norm_matmul_split_body, bounds=tuple(bounds)),
        grid=(n // tm, m // tn),
        in_specs=[pl.BlockSpec((tm, d), lambda i, j: (i, 0), pipeline_mode=pl.Buffered(1)),
                  pl.BlockSpec((1, d), lambda i, j: (0, 0)),
                  pl.BlockSpec((tn, d), lambda i, j: (j, 0))],
        out_specs=out_specs,
        out_shape=out_shape,
        scratch_shapes=[pltpu.VMEM((tm, d), BF16)],
        compiler_params=_cparams("arbitrary", "arbitrary"),
        name="norm_matmul_split",
    )(x, g.reshape(1, d), wt)


def _pair_ones():
    r = lax.broadcasted_iota(I32, (PAIR, PAIR), 0) // HEAD_DIM
    c = lax.broadcasted_iota(I32, (PAIR, PAIR), 1) // HEAD_DIM
    return (r == c).astype(BF16)


def _head_sum(x, ones_bd):
    return _mm_exact_rhs(x, ones_bd)


def _rwkv_prep_body(f_ref, prev8_ref, init_ref, mu_ref, ft_ref, tprev8_ref, tinit_ref, tmu_ref,
                    w0_ref, w2_ref, a0_ref, a2_ref, g2_ref, kk_ref, ka_ref, rk_ref,
                    r_o, k_o, v_o, kk_o, b_o, ld_o, g_o, bon_o, *, tm, width):
    i = pl.program_id(1)

    def token_shift(f, p8_ref, i_ref, m_ref):
        nc = m_ref.shape[1]
        prev_row = jnp.where(i == 0, i_ref[0], p8_ref[0, SUBLANES - 1:SUBLANES, :nc])
        rolled = pltpu.roll(f, shift=1, axis=0)
        row = lax.broadcasted_iota(I32, f.shape, 0)
        f_prev = jnp.where(row == 0, prev_row, rolled)
        return f + (f_prev - f) * m_ref[...]

    fs = token_shift(f_ref[0], prev8_ref, init_ref, mu_ref)
    n_dec = w2_ref.shape[0]
    n_icl = a2_ref.shape[0]
    n_lora = tmu_ref.shape[1]
    ts = token_shift(ft_ref[0][:, :n_lora], tprev8_ref, tinit_ref, tmu_ref)
    w_ = width
    r = fs[:, 0:w_]
    k = fs[:, w_:2 * w_]
    v = fs[:, 2 * w_:3 * w_]
    wd = ts[:, 0:n_dec]
    ad = ts[:, n_dec:n_dec + n_icl]
    gd = ts[:, n_dec + n_icl:]
    z = w0_ref[...] + _mm3(jnp.tanh(wd), w2_ref[...])
    nz = -z
    softplus = jnp.maximum(nz, 0.0) + jnp.log(1.0 + jnp.exp(-jnp.abs(nz)))
    w = -softplus - 0.5
    ld = -jnp.exp(w)
    a = 1.0 / (1.0 + jnp.exp(-(a0_ref[...] + _mm3(ad, a2_ref[...]))))
    g = _mm3(1.0 / (1.0 + jnp.exp(-gd)), g2_ref[...])
    kk = k * kk_ref[...]
    k2 = k * (1.0 + (a - 1.0) * ka_ref[...])
    rk = r * k2 * rk_ref[...]
    ones_bd = _pair_ones()
    for p in range(w_ // PAIR):
        sl = slice(p * PAIR, (p + 1) * PAIR)
        kkp = kk[:, sl]
        nrm = jnp.sqrt(_head_sum(kkp * kkp, ones_bd))
        kkp = kkp / jnp.maximum(nrm, 1e-12)
        ap = a[:, sl]
        r_o[0, p] = r[:, sl]
        k_o[0, p] = k2[:, sl]
        v_o[0, p] = v[:, sl]
        kk_o[0, p] = kkp
        b_o[0, p] = kkp * ap
        ld_o[0, p] = ld[:, sl]
        g_o[0, p] = g[:, sl]
        bon_o[0, p] = _head_sum(rk[:, sl], ones_bd) * v[:, sl]


def rwkv_prep(f_main, f_tail, init_prev, mu, w0, w2, a0, a2, g2, k_k, k_a, r_k, *, tm, width):
    b, t, n_main = f_main.shape
    n_tail = f_tail.shape[2]
    n_lora = mu.shape[0] - n_main
    npair = width // PAIR
    row1 = lambda x: x.reshape(1, -1)
    kern = functools.partial(_rwkv_prep_body, tm=tm, width=width)
    full = lambda a: pl.BlockSpec(a.shape, lambda bi, i: (0,) * a.ndim)
    args = [row1(w0), w2, row1(a0), a2, g2, row1(k_k), row1(k_a), row1(r_k)]
    out_spec = pl.BlockSpec((1, npair, tm, PAIR), lambda bi, i: (bi, 0, i, 0))
    out_shape = jax.ShapeDtypeStruct((b, npair, t, PAIR), F32)
    prev8_map = lambda bi, i: (bi, jnp.maximum(i * (tm // SUBLANES) - 1, 0), 0)

    def feature_specs(ncols, n_init):
        return [pl.BlockSpec((1, tm, ncols), lambda bi, i: (bi, i, 0)),
                pl.BlockSpec((1, SUBLANES, ncols), prev8_map),
                pl.BlockSpec((1, 1, n_init), lambda bi, i: (bi, 0, 0)),
                pl.BlockSpec((1, n_init), lambda bi, i: (0, 0))]

    return pl.pallas_call(
        kern,
        grid=(b, t // tm),
        in_specs=feature_specs(n_main, n_main) + feature_specs(n_tail, n_lora) + [full(a) for a in args],
        out_specs=[out_spec] * 8,
        out_shape=[out_shape] * 8,
        compiler_params=_cparams("parallel", "parallel"),
        name="rwkv_prep",
    )(f_main, f_main, init_prev[:, :n_main].reshape(b, 1, n_main), row1(mu[:n_main]),
      f_tail, f_tail, init_prev[:, n_main:].reshape(b, 1, n_lora), row1(mu[n_main:]), *args)


def _rwkv_chunk_body(r_ref, k_ref, v_ref, kk_ref, b_ref, ld_ref, g_ref, bon_ref, s0_ref,
                     lng_ref, lnb_ref, o_ref, st_ref, s_ref, *, L, npair, group):
    c = pl.program_id(1)

    @pl.when(c == 0)
    def _():
        s_ref[...] = s0_ref[0]

    L2 = 2 * L
    row = lax.broadcasted_iota(I32, (L2, L2), 0)
    col = lax.broadcasted_iota(I32, (L2, L2), 1)
    eye = (row == col).astype(F32)
    row4 = lax.broadcasted_iota(I32, (2 * L2, 2 * L2), 0)
    col4 = lax.broadcasted_iota(I32, (2 * L2, 2 * L2), 1)
    rr, cc = row4 % L2, col4 % L2
    tri_all = ((rr // L) == (cc // L)) & ((cc < rr) | ((row4 >= L2) & (cc == rr)))
    tr = lax.broadcasted_iota(I32, (L, L), 0)
    tc = lax.broadcasted_iota(I32, (L, L), 1)
    cum_mat = (tc <= tr).astype(BF16)
    lane = lax.broadcasted_iota(I32, (L, PAIR), 1)
    first = lane < HEAD_DIM
    ones_bd = _pair_ones()
    n_sq = max(int(math.ceil(math.log2(L))) - 1, 0)

    def block_diag(x):
        return jnp.concatenate([jnp.where(first, x, 0.0), jnp.where(first, 0.0, x)], axis=0)

    def group_step(gi, carry):
        ps = [gi * group + j for j in range(group)]
        each = lambda f, *cols: [f(*args) for args in zip(*cols)]
        ld = [ld_ref[0, p] for p in ps]
        cum = each(lambda x: _mm_exact_rhs_t(cum_mat, x), ld)
        dec = each(jnp.exp, cum)
        dec_inv = each(lambda c_: jnp.exp(-c_), cum)
        a_t = each(lambda p, c_, l_: block_diag(-kk_ref[0, p] * jnp.exp(c_ - l_)), ps, cum, ld)
        b_t = each(lambda p, e: block_diag(b_ref[0, p] * e), ps, dec_inv)
        k_t = each(lambda p, e: block_diag(k_ref[0, p] * e), ps, dec_inv)
        r_t = each(lambda p, e: block_diag(r_ref[0, p] * e), ps, dec)
        v_b = each(lambda p: block_diag(v_ref[0, p]), ps)
        ar = each(lambda a, r: jnp.concatenate([a, r], axis=0), a_t, r_t)
        bk = each(lambda b, k: jnp.concatenate([b, k], axis=0), b_t, k_t)
        cross = each(lambda x_, y_: jnp.where(tri_all, _mm_nt(x_, y_), 0.0), ar, bk)
        a_ab = [c_[:L2, :L2] for c_ in cross]
        a_rb = [c_[L2:, :L2] for c_ in cross]
        akrk = [c_[:, L2:] for c_ in cross]
        t_inv = each(lambda a: eye + a, a_ab)
        if n_sq >= 1:
            x = each(lambda a: _mm(a, a), a_ab)
            for _ in range(n_sq - 1):
                xt = each(lambda x_, t: _mm(x_, jnp.concatenate([x_, t], axis=1)), x, t_inv)
                x = [z[:, :L2] for z in xt]
                t_inv = each(lambda t, z: t + z[:, L2:], t_inv, xt)
            t_inv = each(lambda t, x_: t + _mm(x_, t), t_inv, x)
        s = [s_ref[p] for p in ps]
        ar_s = each(_mm_nt, ar, s)
        akrk_v = each(_mm, akrk, v_b)
        u = each(lambda t, p1, p2: _mm(t, p1[:L2] + p2[:L2]), t_inv, ar_s, akrk_v)
        y_b = each(lambda p1, rb, u_, p2: p1[L2:] + _mm(rb, u_) + p2[L2:], ar_s, a_rb, u, akrk_v)
        s_new = each(lambda s_, u_, b, v, k, d: (s_ + _mm(u_.T, b) + _mm(v.T, k)) * d[L - 1:L, :],
                     s, u, b_t, v_b, k_t, dec)
        for p, sn in zip(ps, s_new):
            s_ref[p] = sn
        y = each(lambda yb: yb[:L] + yb[L:], y_b)
        mean = each(lambda y_: _head_sum(y_, ones_bd) * (1.0 / HEAD_DIM), y)
        d = each(lambda y_, m: y_ - m, y, mean)
        var = each(lambda d_: _head_sum(d_ * d_, ones_bd) * (1.0 / HEAD_DIM), d)
        for p, d_, v_ in zip(ps, d, var):
            yn = d_ * lax.rsqrt(v_ + GN_EPS) * lng_ref[p] + lnb_ref[p]
            o_ref[0, p] = (yn + bon_ref[0, p]) * g_ref[0, p]
        return carry

    lax.fori_loop(0, npair // group, group_step, 0)

    @pl.when(c == pl.num_programs(1) - 1)
    def _():
        st_ref[0] = s_ref[...]


def _mm_exact_rhs_t(m_bf16, x):
    hi = x.astype(BF16)
    r1 = x - hi.astype(F32)
    mid = r1.astype(BF16)
    lo = (r1 - mid.astype(F32)).astype(BF16)
    d = lambda y: jnp.dot(m_bf16, y, preferred_element_type=F32)
    return d(hi) + (d(mid) + d(lo))


def rwkv_chunk(feats, s0_bd, ln_g, ln_b, *, L, group=8):
    b, npair, t, _ = feats[0].shape
    blk = pl.BlockSpec((1, npair, L, PAIR), lambda bi, c: (bi, 0, c, 0))
    st_spec = pl.BlockSpec((1, npair, PAIR, PAIR), lambda bi, c: (bi, 0, 0, 0))
    par_spec = pl.BlockSpec((npair, 1, PAIR), lambda bi, c: (0, 0, 0))
    kern = functools.partial(_rwkv_chunk_body, L=L, npair=npair, group=group)
    return pl.pallas_call(
        kern,
        grid=(b, t // L),
        in_specs=[blk] * 8 + [st_spec, par_spec, par_spec],
        out_specs=[blk, st_spec],
        out_shape=[jax.ShapeDtypeStruct((b, npair, t, PAIR), F32),
                   jax.ShapeDtypeStruct((b, npair, PAIR, PAIR), F32)],
        scratch_shapes=[pltpu.VMEM((npair, PAIR, PAIR), F32)],
        compiler_params=_cparams("parallel", "arbitrary"),
        name="rwkv_chunk",
    )(*feats, s0_bd, ln_g.reshape(npair, 1, PAIR), ln_b.reshape(npair, 1, PAIR))


def _state_to_block_diag(s):
    b, h, n, _ = s.shape
    s = s.reshape(b, h // 2, 2, n, n)
    z = jnp.zeros_like(s[:, :, 0])
    top = jnp.concatenate([s[:, :, 0], z], axis=-1)
    bot = jnp.concatenate([z, s[:, :, 1]], axis=-1)
    return jnp.concatenate([top, bot], axis=-2)


def _state_from_block_diag(s_bd):
    b, p, _, _ = s_bd.shape
    n = HEAD_DIM
    return jnp.stack([s_bd[:, :, :n, :n], s_bd[:, :, n:, n:]], axis=2).reshape(b, 2 * p, n, n)


def _t5_bucket(dist):
    exact = N_BUCKETS // 2
    d = jnp.maximum(dist, 0)
    far = exact + (jnp.log(jnp.maximum(d, 1).astype(F32) / exact) / math.log(MAX_DISTANCE / exact)
                   * (N_BUCKETS - exact)).astype(I32)
    return jnp.where(d < exact, d, jnp.minimum(far, N_BUCKETS - 1))


def _bias_tables_body(rb_ref, o_ref, *, offsets, n_heads):
    r = lax.broadcasted_iota(I32, (Q_BLOCK, Q_BLOCK), 0)
    c = lax.broadcasted_iota(I32, (Q_BLOCK, Q_BLOCK), 1)
    for t, (off, key_major) in enumerate(offsets):
        bucket = _t5_bucket((c - r if key_major else r - c) + off)
        for h in range(n_heads):
            def body(bk, acc):
                return jnp.where(bucket == bk, rb_ref[bk, h], acc)
            tile = lax.fori_loop(0, N_BUCKETS, body, jnp.zeros((Q_BLOCK, Q_BLOCK), F32))
            o_ref[t, h] = tile - rb_ref[N_BUCKETS - 1, h]


def bias_tables(rel_bias, offsets):
    n_heads = rel_bias.shape[1]
    kern = functools.partial(_bias_tables_body, offsets=tuple(offsets), n_heads=n_heads)
    return pl.pallas_call(
        kern,
        in_specs=[pl.BlockSpec(memory_space=pltpu.SMEM)],
        out_specs=pl.BlockSpec(memory_space=pltpu.VMEM),
        out_shape=jax.ShapeDtypeStruct((len(offsets), n_heads, Q_BLOCK, Q_BLOCK), F32),
        name="bias_tables",
    )(rel_bias)


def _sortable_key(scores):
    bits = lax.bitcast_convert_type(scores + 0.0, I32)
    return jnp.where(bits < 0, bits ^ 0x7FFFFFFF, bits)


def _count(mask):
    return jnp.sum(mask.astype(F32), axis=-1, keepdims=True)


def _topk_select(key, topk, n_index_bits):
    rows, n = key.shape
    kf = float(topk)
    t0 = jnp.where(_count(key >= 0) >= kf, 0, INT_MIN).astype(I32)

    def value_bit(i, t):
        cand = t + lax.shift_left(jnp.int32(1), 30 - i)
        return jnp.where(_count(key >= cand) >= kf, cand, t)

    thr = lax.fori_loop(0, 31, value_bit, t0)
    above = key > thr
    ties = key == thr
    need = kf - _count(above)
    idx = lax.broadcasted_iota(I32, (rows, n), 1)

    def lowest_ties():
        def index_bit(i, m):
            cand = m + lax.shift_left(jnp.int32(1), n_index_bits - 1 - i)
            return jnp.where(_count(ties & (idx < cand)) <= need, cand, m)
        return lax.fori_loop(0, n_index_bits, index_bit, jnp.zeros((rows, 1), I32))

    surplus = jnp.max(_count(ties) - need) > 0.0
    m = lax.cond(surplus, lowest_ties, lambda: jnp.full((rows, 1), 2 ** n_index_bits, I32))
    return above | (ties & (idx < m))


SUM_CHAINS = 4
HEAD_GROUP = 4


def _sum_rows(x):
    r = x.shape[0]
    if r % (SUM_CHAINS * SUBLANES) == 0 and r > SUM_CHAINS * SUBLANES:
        x = jnp.sum(x.reshape(SUM_CHAINS, r // SUM_CHAINS, x.shape[1]), axis=1)
    return jnp.sum(x, axis=0, keepdims=True)


def _max_rows(x):
    r = x.shape[0]
    if r % (SUM_CHAINS * SUBLANES) == 0 and r > SUM_CHAINS * SUBLANES:
        x = jnp.max(x.reshape(SUM_CHAINS, r // SUM_CHAINS, x.shape[1]), axis=1)
    return jnp.max(x, axis=0, keepdims=True)


def _topk_select_cols(key, topk, n_index_bits):
    n, cols = key.shape
    kf = float(topk)
    cnt = lambda m: _sum_rows(m.astype(F32))
    t0 = jnp.where(cnt(key >= 0) >= kf, 0, INT_MIN).astype(I32)

    def value_bit(i, t):
        cand = t + lax.shift_left(jnp.int32(1), 30 - i)
        return jnp.where(cnt(key >= cand) >= kf, cand, t)

    thr = lax.fori_loop(0, 31, value_bit, t0)
    above = key > thr
    ties = key == thr
    need = kf - cnt(above)
    idx = lax.broadcasted_iota(I32, (n, cols), 0)

    def lowest_ties():
        def index_bit(i, m):
            cand = m + lax.shift_left(jnp.int32(1), n_index_bits - 1 - i)
            return jnp.where(cnt(ties & (idx < cand)) <= need, cand, m)
        return lax.fori_loop(0, n_index_bits, index_bit, jnp.zeros((1, cols), I32))

    surplus = jnp.max(cnt(ties) - need) > 0.0
    m = lax.cond(surplus, lowest_ties, lambda: jnp.full((1, cols), 2 ** n_index_bits, I32))
    return above | (ties & (idx < m))


def _dsa_prompt_block(nb, ik_ref, iqt_ref, wt_ref, k_ref, qt_ref, vt_ref, bias_ref, o_ref, mask_ref,
                      *, n_heads, n_idx_heads, topk):
    w = nb * Q_BLOCK
    ik = ik_ref[0, :w, :]
    idx_dim = ik.shape[1]

    def head_rows(ref, h, dh, cols=slice(None)):
        return ref[0, pl.ds(pl.multiple_of(h * dh, dh), dh), cols]

    def idx_heads(gi, acc):
        hs = [gi * HEAD_GROUP + j for j in range(HEAD_GROUP)]
        dots = [jnp.dot(ik, head_rows(iqt_ref, h, idx_dim), preferred_element_type=F32) for h in hs]
        terms = [jnp.maximum(d, 0.0) * wt_ref[0, h] for d, h in zip(dots, hs)]
        while len(terms) > 1:
            terms = [a + b for a, b in zip(terms[0::2], terms[1::2])]
        return acc + terms[0]

    scores = lax.fori_loop(0, n_idx_heads // HEAD_GROUP, idx_heads, jnp.zeros((w, Q_BLOCK), F32))
    kpos = lax.broadcasted_iota(I32, (w, Q_BLOCK), 0)
    qpos = (nb - 1) * Q_BLOCK + lax.broadcasted_iota(I32, (w, Q_BLOCK), 1)
    valid = kpos <= qpos
    if w <= topk:
        sel = valid
    else:
        key = jnp.where(valid, _sortable_key(scores), INT_MIN)
        sel = valid & _topk_select_cols(key, topk, int(math.ceil(math.log2(w))) + 1)
    mask_ref[:w, :] = jnp.where(sel, 0.0, -jnp.inf)

    def with_near_bias(logits, h):
        near = [logits[w - Q_BLOCK:] + bias_ref[0, h]]
        if nb >= 2:
            near = [logits[w - 2 * Q_BLOCK:w - Q_BLOCK] + bias_ref[1, h]] + near
        if nb >= 3:
            near = [logits[:w - 2 * Q_BLOCK]] + near
        return jnp.concatenate(near, axis=0) if len(near) > 1 else near[0]

    pair_row_half = lax.broadcasted_iota(I32, (PAIR, Q_BLOCK), 0) // HEAD_DIM

    def head_logits(gi, j):
        p = gi * (HEAD_GROUP // 2) + j // 2
        qt_pair = head_rows(qt_ref, p, PAIR)
        qt_head = jnp.where(pair_row_half == j % 2, qt_pair, jnp.zeros_like(qt_pair))
        return jnp.dot(k_ref[0, p, :w, :], qt_head, preferred_element_type=F32)

    def attn_heads(gi, carry):
        hs = [gi * HEAD_GROUP + j for j in range(HEAD_GROUP)]
        mask = mask_ref[:w, :]
        logits = [head_logits(gi, j) + mask for j in range(HEAD_GROUP)]
        logits = [with_near_bias(l_, h) for l_, h in zip(logits, hs)]
        mx = [_max_rows(l_) for l_ in logits]
        e = [jnp.exp(l_ - m_) for l_, m_ in zip(logits, mx)]
        den = [_sum_rows(e_) for e_ in e]
        o = [jnp.dot(head_rows(vt_ref, h, HEAD_DIM, slice(0, w)), e_.astype(BF16),
                     preferred_element_type=F32) for h, e_ in zip(hs, e)]
        for h, o_, d_ in zip(hs, o, den):
            o_ref[0, pl.ds(pl.multiple_of(h * HEAD_DIM, HEAD_DIM), HEAD_DIM), :] = (o_ / d_).astype(o_ref.dtype)
        return carry

    lax.fori_loop(0, n_heads // HEAD_GROUP, attn_heads, 0)


def _dsa_prompt_body(ik_ref, iqt_ref, wt_ref, k_ref, qt_ref, vt_ref, bias_ref, o_ref, mask_ref,
                     *, n_heads, n_idx_heads, seq, topk):
    i = pl.program_id(0)
    for nb in range(1, seq // Q_BLOCK + 1):
        @pl.when(i == nb - 1)
        def _(nb=nb):
            _dsa_prompt_block(nb, ik_ref, iqt_ref, wt_ref, k_ref, qt_ref, vt_ref, bias_ref, o_ref, mask_ref,
                              n_heads=n_heads, n_idx_heads=n_idx_heads, topk=topk)


def dsa_prompt(ik, iqt, wts, k_pairs, qt, vt, bias_tiles, *, topk):
    b, width, s = qt.shape
    h = width // HEAD_DIM
    ih = wts.shape[1]
    kern = functools.partial(_dsa_prompt_body, n_heads=h, n_idx_heads=ih, seq=s, topk=topk)
    grid_spec = pltpu.PrefetchScalarGridSpec(
        num_scalar_prefetch=0,
        grid=(s // Q_BLOCK, b),
        in_specs=[pl.BlockSpec((1, s, ik.shape[2]), lambda i, bi: (bi, 0, 0)),
                  pl.BlockSpec((1, iqt.shape[1], Q_BLOCK), lambda i, bi: (bi, 0, i)),
                  pl.BlockSpec((1, ih, 1, Q_BLOCK), lambda i, bi: (bi, 0, 0, i)),
                  pl.BlockSpec((1, h // 2, s, PAIR), lambda i, bi: (bi, 0, 0, 0)),
                  pl.BlockSpec((1, width, Q_BLOCK), lambda i, bi: (bi, 0, i)),
                  pl.BlockSpec((1, width, s), lambda i, bi: (bi, 0, 0)),
                  pl.BlockSpec(bias_tiles.shape, lambda i, bi: (0, 0, 0, 0))],
        out_specs=pl.BlockSpec((1, width, Q_BLOCK), lambda i, bi: (bi, 0, i)),
        scratch_shapes=[pltpu.VMEM((s, Q_BLOCK), F32)],
    )
    return pl.pallas_call(
        kern,
        grid_spec=grid_spec,
        out_shape=jax.ShapeDtypeStruct((b, width, s), BF16),
        compiler_params=_cparams("arbitrary", "arbitrary"),
        name="dsa_prompt",
    )(ik, iqt, wts, k_pairs, qt, vt, bias_tiles)


def _dsa_select_body(pt_ref, iq_ref, wt_ref, iknew_ref, cidx_hbm, o_ref, ikbuf, sem,
                     *, layer, n_pages, group, n_idx_heads, t_new, topk):
    s = pl.program_id(0)
    past = n_pages * PAGE_SIZE
    n_keys = past + PAGE_SIZE

    def ik_copy(i):
        g = i // n_pages
        p = i % n_pages
        page = pt_ref[(s * group + g) * n_pages + p]
        return pltpu.make_async_copy(cidx_hbm.at[layer, page], ikbuf.at[g, p], sem)

    def ik_start(i, carry):
        ik_copy(i).start()
        return carry

    def ik_wait(i, carry):
        ik_copy(i).wait()
        return carry

    lax.fori_loop(0, group * n_pages, ik_start, 0)
    for g in range(group):
        ikbuf[g, n_pages] = iknew_ref[g]
    lax.fori_loop(0, group * n_pages, ik_wait, 0)

    scores = []
    for g in range(group):
        ikt_all = jnp.concatenate([ikbuf[g, p] for p in range(n_pages + 1)], axis=1)
        dots = _mm(iq_ref[g], ikt_all)
        weighted = jnp.maximum(dots, 0.0) * wt_ref[g]
        scores.append(jnp.sum(weighted.reshape(n_idx_heads, t_new, n_keys), axis=0))
    scores = jnp.concatenate(scores, axis=0)
    shape = (group * t_new, n_keys)
    qpos = past + lax.broadcasted_iota(I32, shape, 0) % t_new
    kpos = lax.broadcasted_iota(I32, shape, 1)
    valid = kpos <= qpos
    key = jnp.where(valid, _sortable_key(scores), INT_MIN)
    sel = valid & _topk_select(key, topk, int(math.log2(n_keys)) + 1)
    o_ref[...] = jnp.where(sel, 0.0, -jnp.inf).reshape(group, t_new, n_keys)


def dsa_sample_select(page_table, iq_rows, wt_rows, ik_new_t, cache_idx_kt, *, layer, n_idx_heads, t_new, topk,
                      group):
    db, n_pages = page_table.shape
    idx_dim = cache_idx_kt.shape[2]
    n_keys = (n_pages + 1) * PAGE_SIZE
    kern = functools.partial(_dsa_select_body, layer=layer, n_pages=n_pages, group=group,
                             n_idx_heads=n_idx_heads, t_new=t_new, topk=topk)
    per_g = lambda shape: pl.BlockSpec((group,) + shape, lambda si, pt: (si,) + (0,) * len(shape))
    grid_spec = pltpu.PrefetchScalarGridSpec(
        num_scalar_prefetch=1,
        grid=(db // group,),
        in_specs=[per_g((n_idx_heads * t_new, idx_dim)), per_g((n_idx_heads * t_new, 1)),
                  per_g((idx_dim, PAGE_SIZE)), pl.BlockSpec(memory_space=pl.ANY)],
        out_specs=per_g((t_new, n_keys)),
        scratch_shapes=[pltpu.VMEM((group, n_pages + 1, idx_dim, PAGE_SIZE), F32), pltpu.SemaphoreType.DMA(())],
    )
    return pl.pallas_call(
        kern,
        grid_spec=grid_spec,
        out_shape=jax.ShapeDtypeStruct((db, t_new, n_keys), F32),
        compiler_params=_cparams("arbitrary"),
        name="dsa_sample_select",
    )(page_table.reshape(-1), iq_rows, wt_rows, ik_new_t, cache_idx_kt)


def _dsa_sample_body(pt_ref, mask_ref, qbd_ref, knew_ref, vnew_ref,
                     blast_ref, bnew_ref, ck_hbm, cv_hbm, o_ref,
                     kbuf, vbuf, sem_k, sem_v,
                     *, layer, n_pages, chunk, n_heads, t_new):
    b = pl.program_id(0)
    n_seq = pl.num_programs(0)
    past = n_pages * PAGE_SIZE
    n_chunks = n_pages // chunk
    rows = n_heads * t_new
    ck = chunk * PAGE_SIZE

    def kv_copies(seq, c, j):
        slot = c % 2
        page = pt_ref[seq * n_pages + c * chunk + j]
        return (pltpu.make_async_copy(ck_hbm.at[layer, page], kbuf.at[slot, j], sem_k.at[slot]),
                pltpu.make_async_copy(cv_hbm.at[layer, page], vbuf.at[slot, j], sem_v.at[slot]))

    def start_chunk(seq, c):
        for j in range(chunk):
            kc, vc = kv_copies(seq, c, j)
            kc.start()
            vc.start()

    def wait_chunk(c):
        for j in range(chunk):
            kc, vc = kv_copies(b, c, j)
            kc.wait()
            vc.wait()

    @pl.when(b == 0)
    def _():
        start_chunk(0, 0)

    sel_rows = jnp.tile(mask_ref[0], (n_heads, 1))

    q_rep = jnp.tile(qbd_ref[0], (n_heads, 1))
    row_head = lax.broadcasted_iota(I32, q_rep.shape, 0) // t_new
    col_head = lax.broadcasted_iota(I32, q_rep.shape, 1) // HEAD_DIM
    qbd = jnp.where(row_head == col_head, q_rep, jnp.zeros_like(q_rep))
    neg = -1e30

    def update(state, logits, maskc, vt_bf16):
        m, l, acc = state
        s = logits + maskc
        m_new = jnp.maximum(m, jnp.max(s, axis=-1, keepdims=True))
        alpha = jnp.exp(m - m_new)
        p = jnp.exp(s - m_new)
        l = alpha * l + jnp.sum(p, axis=-1, keepdims=True)
        acc = alpha * acc + _mm_nt(p, vt_bf16)
        return m_new, l, acc

    def pages_t(buf, slot):
        return jnp.concatenate([buf[slot, j] for j in range(chunk)], axis=1).astype(BF16)

    state = (jnp.full((rows, 1), neg, F32), jnp.zeros((rows, 1), F32),
             jnp.zeros((rows, qbd.shape[1]), F32))
    for c in range(n_chunks):
        if c + 1 < n_chunks:
            start_chunk(b, c + 1)
        else:
            @pl.when(b + 1 < n_seq)
            def _():
                start_chunk(b + 1, 0)
        wait_chunk(c)
        slot = c % 2
        logits = _mm(qbd, pages_t(kbuf, slot))
        if c == n_chunks - 1:
            logits = jnp.concatenate([logits[:, :ck - PAGE_SIZE],
                                      logits[:, ck - PAGE_SIZE:] + blast_ref[...]], axis=1)
        state = update(state, logits, sel_rows[:, c * ck:(c + 1) * ck], pages_t(vbuf, slot))
    logits = _mm(qbd, knew_ref[0]) + bnew_ref[...]
    m, l, acc = update(state, logits, sel_rows[:, past:], vnew_ref[0])
    out = jnp.where(row_head == col_head, acc / l, 0.0)
    o_ref[0] = jnp.sum(out.reshape(n_heads, t_new, out.shape[1]), axis=0)


def dsa_sample(page_table, mask, q_bd, k_new_t, v_new_t, bias_last, bias_new, cache_kt, cache_vt,
               *, layer, n_heads, t_new, chunk):
    db, n_pages = page_table.shape
    assert (n_pages // chunk) % 2 == 0, "chunks alternate between two buffers across sequences"
    rows = n_heads * t_new
    width = q_bd.shape[2]
    kern = functools.partial(_dsa_sample_body, layer=layer, n_pages=n_pages, chunk=chunk, n_heads=n_heads,
                             t_new=t_new)
    per_b = lambda shape: pl.BlockSpec((1,) + shape, lambda bi, pt: (bi,) + (0,) * len(shape))
    const = lambda shape: pl.BlockSpec(shape, lambda bi, pt: (0,) * len(shape))
    any_spec = pl.BlockSpec(memory_space=pl.ANY)
    grid_spec = pltpu.PrefetchScalarGridSpec(
        num_scalar_prefetch=1,
        grid=(db,),
        in_specs=[per_b((t_new, mask.shape[2])),
                  per_b((t_new, width)), per_b((width, PAGE_SIZE)), per_b((width, PAGE_SIZE)),
                  const((rows, PAGE_SIZE)), const((rows, PAGE_SIZE)),
                  any_spec, any_spec],
        out_specs=per_b((t_new, width)),
        scratch_shapes=[pltpu.VMEM((2, chunk, width, PAGE_SIZE), F32),
                        pltpu.VMEM((2, chunk, width, PAGE_SIZE), F32),
                        pltpu.SemaphoreType.DMA((2,)),
                        pltpu.SemaphoreType.DMA((2,))],
    )
    return pl.pallas_call(
        kern,
        grid_spec=grid_spec,
        out_shape=jax.ShapeDtypeStruct((db, t_new, width), F32),
        compiler_params=_cparams("arbitrary"),
        name="dsa_sample",
    )(page_table.reshape(-1), mask, q_bd, k_new_t, v_new_t, bias_last, bias_new, cache_kt, cache_vt)


def _token_minor_cache(cache):
    l, pool, page = cache.shape[:3]
    nd = cache.ndim
    return cache.transpose((0, 1) + tuple(range(3, nd)) + (2,)).reshape(l, pool, -1, page)


def _dsa_sample_inputs(q, k_new, v_new, iq, ik_new, iw, bias_tiles):
    db, t, h, dh = q.shape
    ih = iq.shape[2]
    iq_rows = iq.transpose(0, 2, 1, 3).reshape(db, ih * t, -1).astype(BF16)
    wt_rows = (iw * (ih ** -0.5 * iq.shape[3] ** -0.5)).transpose(0, 2, 1).reshape(db, ih * t, 1)
    q_bd = q.reshape(db, t, h * dh)
    page_t = lambda x: jnp.pad(x.reshape(db, t, -1).transpose(0, 2, 1), ((0, 0), (0, 0), (0, PAGE_SIZE - t)))
    bias_new = bias_tiles[0, :, :t, :].reshape(h * t, Q_BLOCK)
    bias_last = bias_tiles[1, :, :t, :].reshape(h * t, Q_BLOCK)
    return ((iq_rows, wt_rows, page_t(ik_new)),
            (q_bd.astype(BF16), page_t(k_new).astype(BF16), page_t(v_new).astype(BF16), bias_last, bias_new))


def _matmul_residual_body(x_ref, a_ref, b_ref, w_ref, o_ref):
    ka = a_ref.shape[1]
    o_ref[...] = x_ref[...] + (_mm(a_ref[...], w_ref[:ka, :]) + _mm(b_ref[...], w_ref[ka:, :]))


def matmul_residual(x, a, b, w, *, tm, tn):
    n, d = x.shape
    ka, kb = a.shape[1], b.shape[1]
    return pl.pallas_call(
        _matmul_residual_body,
        grid=(n // tm, d // tn),
        in_specs=[pl.BlockSpec((tm, tn), lambda i, j: (i, j)),
                  pl.BlockSpec((tm, ka), lambda i, j: (i, 0)),
                  pl.BlockSpec((tm, kb), lambda i, j: (i, 0)),
                  pl.BlockSpec((ka + kb, tn), lambda i, j: (0, j))],
        out_specs=pl.BlockSpec((tm, tn), lambda i, j: (i, j)),
        out_shape=jax.ShapeDtypeStruct((n, d), F32),
        compiler_params=_cparams("parallel", "parallel"),
        name="matmul_residual",
    )(x, a, b, w)


def _cross_attn_body(x_ref, g_ref, wq_ref, mk_ref, mv_ref, wo_ref, o_ref, *, groups, t_rows, n_heads, head_dim):
    x = x_ref[...]
    h = _rmsnorm(x, g_ref[...]).astype(BF16)
    q = jnp.dot(h, wq_ref[...], preferred_element_type=F32).astype(BF16)
    scale = head_dim ** -0.5
    outs = []
    for gi in range(groups):
        qg = q[gi * t_rows:(gi + 1) * t_rows]
        heads = []
        for hh in range(n_heads):
            sl = slice(hh * head_dim, (hh + 1) * head_dim)
            logits = _mm_nt(qg[:, sl], mk_ref[gi, :, sl]) * scale
            mx = jnp.max(logits, axis=-1, keepdims=True)
            e = jnp.exp(logits - mx)
            p = e / jnp.sum(e, axis=-1, keepdims=True)
            heads.append(_mm(p, mv_ref[gi, :, sl]))
        outs.append(jnp.concatenate(heads, axis=1))
    o = jnp.concatenate(outs, axis=0) if groups > 1 else outs[0]
    o_ref[...] = x + jnp.dot(o.astype(BF16), wo_ref[...], preferred_element_type=F32)


def cross_attn(x, g, wq, mk, mv, wo, *, groups, t_rows, seq_tiles, n_heads):
    n, d = x.shape
    xw = wq.shape[1]
    rows = groups * t_rows
    m = mk.shape[1]
    kern = functools.partial(_cross_attn_body, groups=groups, t_rows=t_rows, n_heads=n_heads,
                             head_dim=xw // n_heads)
    return pl.pallas_call(
        kern,
        grid=(n // rows,),
        in_specs=[pl.BlockSpec((rows, d), lambda i: (i, 0)),
                  pl.BlockSpec((1, d), lambda i: (0, 0)),
                  pl.BlockSpec((d, xw), lambda i: (0, 0)),
                  pl.BlockSpec((groups, m, xw), lambda i: (i // seq_tiles, 0, 0)),
                  pl.BlockSpec((groups, m, xw), lambda i: (i // seq_tiles, 0, 0)),
                  pl.BlockSpec((xw, d), lambda i: (0, 0))],
        out_specs=pl.BlockSpec((rows, d), lambda i: (i, 0)),
        out_shape=jax.ShapeDtypeStruct((n, d), F32),
        compiler_params=_cparams("parallel"),
        name="cross_attn",
    )(x, g.reshape(1, d), wq, mk, mv, wo)


def _router_body(x_ref, g_ref, wr_ref, br_ref, h_ref, r_ref, *, n_groups, per_group):
    h = _rmsnorm(x_ref[...], g_ref[...])
    _store_slab_rows(h_ref, h, h.shape[1] // LANES)
    logits = _mm3(h, wr_ref[...]) + br_ref[...]
    lane = lax.broadcasted_iota(I32, logits.shape, 1).astype(F32)
    big = 1e9
    first_lane = lambda hit: jnp.min(jnp.where(hit, lane, big), axis=-1, keepdims=True)
    gl = jnp.where(lane < n_groups, logits, -jnp.inf)
    gmax = jnp.max(gl, axis=-1, keepdims=True)
    grp = first_lane(gl == gmax)
    p_grp = 1.0 / jnp.sum(jnp.exp(gl - gmax), axis=-1, keepdims=True)
    e_id = lane - n_groups
    in_grp = (e_id >= grp * per_group) & (e_id < (grp + 1.0) * per_group)
    el = jnp.where(in_grp, logits, -jnp.inf)
    v1 = jnp.max(el, axis=-1, keepdims=True)
    i1 = first_lane(el == v1) - n_groups
    el2 = jnp.where(e_id == i1, -jnp.inf, el)
    v2 = jnp.max(el2, axis=-1, keepdims=True)
    i2 = first_lane(el2 == v2) - n_groups
    e2 = jnp.exp(v2 - v1)
    g1 = p_grp / (1.0 + e2)
    g2 = p_grp * e2 / (1.0 + e2)
    r_ref[...] = jnp.where(lane == 0, g1, jnp.where(lane == 1, g2, jnp.where(
        lane == 2, i1, jnp.where(lane == 3, i2, 0.0))))


def router(x, g, w_r, b_r, *, tm, n_groups, per_group):
    n, d = x.shape
    kern = functools.partial(_router_body, n_groups=n_groups, per_group=per_group)
    return pl.pallas_call(
        kern,
        grid=(n // tm,),
        in_specs=[pl.BlockSpec((tm, d), lambda i: (i, 0)),
                  pl.BlockSpec((1, d), lambda i: (0, 0)),
                  pl.BlockSpec((d, LANES), lambda i: (0, 0)),
                  pl.BlockSpec((1, LANES), lambda i: (0, 0))],
        out_specs=[pl.BlockSpec((tm * (d // LANES), LANES), lambda i: (i, 0)),
                   pl.BlockSpec((tm, LANES), lambda i: (i, 0))],
        out_shape=[jax.ShapeDtypeStruct((n * (d // LANES), LANES), F32), jax.ShapeDtypeStruct((n, LANES), F32)],
        compiler_params=_cparams("parallel"),
        name="moe_router",
    )(x, g.reshape(1, d), w_r, b_r)


def _slab_rows(buf, offset, n_rows, n_slab, stride):
    return jnp.concatenate([buf[pl.ds(offset + s, n_rows, stride=stride), :] for s in range(n_slab)], axis=1)


def _store_slab_rows(buf, x, n_slab):
    n_rows = x.shape[0]
    for s in range(n_slab):
        buf[pl.ds(s, n_rows, stride=n_slab), :] = x[:, s * LANES:(s + 1) * LANES]


def _moe_ffn_body(te_ref, nr_ref, tok_ref, dst_ref, h_hbm, w1_ref, w3_ref, w2_ref, y_hbm,
                  xbuf, ybuf, gsem, ssem, *, tm):
    t = pl.program_id(0)
    n_tiles = pl.num_programs(0)
    n_slab = h_hbm.shape[1]
    slot = t % 2

    def slab(buf, sl, r):
        return buf.at[sl, pl.ds(pl.multiple_of(r * n_slab, n_slab), n_slab)]

    def gather_copy(tile, sl, r):
        return pltpu.make_async_copy(h_hbm.at[tok_ref[tile * tm + r]], slab(xbuf, sl, r), gsem.at[sl])

    def scatter_copy(tile, sl, r):
        return pltpu.make_async_copy(slab(ybuf, sl, r), y_hbm.at[dst_ref[tile * tm + r]], ssem.at[sl])

    def start_rows(n, copy):
        def pair(i, c):
            copy(2 * i).start(priority=1)
            copy(2 * i + 1).start(priority=1)
            return c
        lax.fori_loop(0, n // 2, pair, 0)

        @pl.when(n % 2 == 1)
        def _():
            copy(n - 1).start(priority=1)

    def wait_rows(n, copy):
        def one(r, c):
            copy(r).wait()
            return c
        lax.fori_loop(0, n, one, 0)

    @pl.when(t == 0)
    def _():
        xbuf[...] = jnp.zeros(xbuf.shape, F32)
        start_rows(nr_ref[0], lambda r: gather_copy(0, 0, r))

    nxt = jnp.minimum(t + 1, n_tiles - 1)

    @pl.when(t + 1 < n_tiles)
    def _():
        start_rows(nr_ref[nxt], lambda r: gather_copy(nxt, 1 - slot, r))

    @pl.when(t >= 2)
    def _():
        prev2 = jnp.maximum(t - 2, 0)
        wait_rows(nr_ref[prev2], lambda r: scatter_copy(prev2, slot, r))

    n = nr_ref[t]

    @pl.when(n > 0)
    def _():
        wait_rows(n, lambda r: gather_copy(t, slot, r))
        x = _slab_rows(xbuf.at[slot], 0, tm, n_slab, n_slab).astype(BF16)
        a = jnp.dot(x, w1_ref[0].astype(BF16), preferred_element_type=F32)
        bgate = jnp.dot(x, w3_ref[0].astype(BF16), preferred_element_type=F32)
        u = (a / (1.0 + jnp.exp(-a))) * bgate
        y = jnp.dot(u.astype(BF16), w2_ref[0].astype(BF16), preferred_element_type=F32)
        _store_slab_rows(ybuf.at[slot], y, n_slab)
        start_rows(n, lambda r: scatter_copy(t, slot, r))

    @pl.when(t == n_tiles - 1)
    def _():
        @pl.when(t >= 1)
        def _():
            prev1 = jnp.maximum(t - 1, 0)
            wait_rows(nr_ref[prev1], lambda r: scatter_copy(prev1, 1 - slot, r))
        wait_rows(n, lambda r: scatter_copy(t, slot, r))


def moe_ffn(tile_expert, tile_rows, row_tok, row_dst, h_slabs, n_out_rows, w1, w3, w2, *, tm):
    n_tiles = tile_expert.shape[0]
    d, de = w1.shape[1], w1.shape[2]
    n_slab = d // LANES
    h3 = h_slabs.reshape(-1, n_slab, LANES)
    w_in_spec = pl.BlockSpec((1, d, de), lambda t, te, nr, tok, dst: (te[t], 0, 0))
    grid_spec = pltpu.PrefetchScalarGridSpec(
        num_scalar_prefetch=4,
        grid=(n_tiles,),
        in_specs=[pl.BlockSpec(memory_space=pl.ANY), w_in_spec, w_in_spec,
                  pl.BlockSpec((1, de, d), lambda t, te, nr, tok, dst: (te[t], 0, 0))],
        out_specs=pl.BlockSpec(memory_space=pl.ANY),
        scratch_shapes=[pltpu.VMEM((2, tm * n_slab, LANES), F32), pltpu.VMEM((2, tm * n_slab, LANES), F32),
                        pltpu.SemaphoreType.DMA((2,)), pltpu.SemaphoreType.DMA((2,))],
    )
    y = pl.pallas_call(
        functools.partial(_moe_ffn_body, tm=tm),
        grid_spec=grid_spec,
        out_shape=jax.ShapeDtypeStruct((n_out_rows, n_slab, LANES), F32),
        compiler_params=_cparams("arbitrary"),
        name="moe_ffn",
    )(tile_expert, tile_rows, row_tok, row_dst, h3, w1, w3, w2)
    return y.reshape(n_out_rows * n_slab, LANES)


def _moe_dispatch(eids, n_experts, tm):
    n, k = eids.shape
    m = n * k
    flat_e = eids.reshape(-1)
    order = jnp.argsort(flat_e, stable=True).astype(I32)
    counts = jnp.sum((flat_e[:, None] == jnp.arange(n_experts)[None, :]).astype(I32), axis=0)
    padded = (counts + tm - 1) // tm * tm
    pad_end = jnp.cumsum(padded)
    pad_start = pad_end - padded
    start = jnp.cumsum(counts) - counts
    n_tiles = -(-m // tm) + n_experts
    tile_start = jnp.arange(n_tiles) * tm
    last = jnp.maximum(pad_end[-1] - 1, 0)
    tile_expert = jnp.minimum(jnp.searchsorted(pad_end, jnp.minimum(tile_start, last), side='right'),
                              n_experts - 1).astype(I32)
    first = tile_start - pad_start[tile_expert]
    tile_rows = jnp.where(tile_start < pad_end[-1], jnp.clip(counts[tile_expert] - first, 0, tm), 0).astype(I32)
    src = jnp.clip(start[tile_expert][:, None] + first[:, None] + jnp.arange(tm)[None, :], 0, m - 1)
    row_dst = order[src].reshape(-1)
    return tile_expert, tile_rows, (row_dst // k).astype(I32), row_dst.astype(I32)


def _combine_body(x_ref, r_ref, g_ref, y_ref, o_ref, *, tm, top_k):
    n_slab = x_ref.shape[1] // LANES
    route = r_ref[...]
    x = x_ref[...]
    for kk in range(top_k):
        x = x + _slab_rows(y_ref, kk * n_slab, tm, n_slab, top_k * n_slab) * route[:, kk:kk + 1]
    o_ref[...] = _rmsnorm(x, g_ref[...])


def moe_combine(x, route, g, y_slabs, *, tm, top_k, tile_offset):
    n, d = x.shape
    rows = tm * top_k * (d // LANES)
    return pl.pallas_call(
        functools.partial(_combine_body, tm=tm, top_k=top_k),
        grid=(n // tm,),
        in_specs=[pl.BlockSpec((tm, d), lambda i: (i, 0)),
                  pl.BlockSpec((tm, LANES), lambda i: (i, 0)),
                  pl.BlockSpec((1, d), lambda i: (0, 0)),
                  pl.BlockSpec((rows, LANES), lambda i: (i + tile_offset, 0))],
        out_specs=pl.BlockSpec((tm, d), lambda i: (i, 0)),
        out_shape=jax.ShapeDtypeStruct((n, d), F32),
        compiler_params=_cparams("parallel"),
        name="moe_combine",
    )(x, route, g.reshape(1, d), y_slabs)


def kernel(x_prompt, x_sample, mem_prompt, cache_k, cache_v, cache_idx_k, page_table, state_wkv, state_shift, cache_mem_k, cache_mem_v, g_mix, w_in, mu_shift, rw_w0, rw_w2, rw_a0, rw_a2, rw_g2, rw_kk, rw_ka, rw_rk, rw_ln_g, rw_ln_b, w_out, g_cross, g_mem, w_cq, w_ck, w_cv, w_co, g_ffn, w_rg, b_rg, w_re, b_re, w_e1, w_e3, w_e2, rel_bias, g_final):
    B, S, D = x_prompt.shape
    DB, T, _ = x_sample.shape
    assert w_in.shape[0] == 1, "single-layer trunk only"
    l = 0
    n_pages = page_table.shape[1]
    past = n_pages * PAGE_SIZE
    topk_p = min(TOPK_MAX, S // 4)
    topk_s = min(TOPK_MAX, (past + T) // 4)
    rw_proj = mu_shift.shape[1]
    width = rw_w0.shape[1]
    at_w = D - width
    n_heads = at_w // HEAD_DIM
    idx_dim = cache_idx_k.shape[-1]
    ih = (w_in.shape[2] - rw_proj - 3 * at_w - idx_dim) // (idx_dim + 1)
    xw = w_cq.shape[2]
    x_heads = cache_mem_k.shape[3]
    n_mem = mem_prompt.shape[1]
    n_experts = w_e1.shape[1]
    top_k = 2
    tn = PROJ_COLS
    n_main = 3 * width
    n_lora = rw_proj - n_main
    n_att = 3 * at_w + ih * idx_dim
    n_tail = n_lora + idx_dim + ih
    assert n_main % tn == 0 and at_w % tn == 0 and (ih * idx_dim) % tn == 0 and n_tail <= tn

    w_t = w_in[l].T
    w_all = jnp.concatenate([w_t[:n_main], w_t[rw_proj:rw_proj + at_w] * HEAD_DIM ** -0.5,
                             w_t[rw_proj + at_w:rw_proj + n_att], w_t[n_main:rw_proj],
                             w_t[rw_proj + n_att:], jnp.zeros((tn - n_tail, D), w_t.dtype)], axis=0).astype(BF16)
    flat, per_head = False, True
    segments = [(n_main // tn, [(flat, F32)]),
                (at_w // tn, [(flat, BF16)]),
                (at_w // tn, [(per_head, F32), (flat, BF16)]),
                (at_w // tn, [(per_head, F32), (flat, BF16)]),
                (ih * idx_dim // tn, [(flat, BF16)]),
                (1, [(flat, F32)])]
    w_out_b = w_out[l].astype(BF16)
    w_cq_b, w_co_b = w_cq[l].astype(BF16), w_co[l].astype(BF16)
    w_ckv = jnp.concatenate([w_ck[l], w_cv[l]], axis=1).astype(BF16)
    n_route = w_rg.shape[2] + w_re.shape[2]
    w_r = jnp.pad(jnp.concatenate([w_rg[l], w_re[l]], axis=1), ((0, 0), (0, LANES - n_route)))
    b_r = jnp.pad(jnp.concatenate([b_rg[l], b_re[l]]), (0, LANES - n_route)).reshape(1, LANES)
    rw_args = (mu_shift[l], rw_w0[l], rw_w2[l], rw_a0[l], rw_a2[l], rw_g2[l], rw_kk[l], rw_ka[l],
               rw_rk[l].reshape(-1))
    tiles = bias_tables(rel_bias, ((0, False), (Q_BLOCK, False), (0, True), (Q_BLOCK, True)))

    def project(x2d, b_, t_, tm):
        f_main, q, k_heads, k, v_heads, v, iq, f_tail = norm_matmul_split(x2d, g_mix[l], w_all, segments,
                                                                          tm=tm, tn=tn)
        r3 = lambda z: z.reshape(b_, t_, -1)
        f_main, f_tail = r3(f_main), r3(f_tail)
        ik = f_tail[..., n_lora:n_lora + idx_dim]
        iw = f_tail[..., n_lora + idx_dim:n_tail]
        shift = jnp.concatenate([f_main[:, -1], f_tail[:, -1, :n_lora]], axis=-1)
        heads5 = lambda z: z.reshape(1, b_, t_, n_heads, HEAD_DIM)
        return f_main, r3(q), heads5(k_heads), r3(k), heads5(v_heads), r3(v), r3(iq), f_tail, ik, iw, shift

    def rw_rows(y):
        b_, p_, t_, _ = y.shape
        return y.transpose(0, 2, 1, 3).reshape(b_ * t_, p_ * PAIR).astype(BF16)

    xp = x_prompt.reshape(B * S, D)
    fm_p, q, k_p, kb_p, v_p, vb_p, iq, ft_p, ik_p, iw, shift_p = project(xp, B, S, PROJ_ROWS)
    feats_p = rwkv_prep(fm_p, ft_p, jnp.zeros((B, rw_proj), F32), *rw_args, tm=PREP_ROWS, width=width)
    rw_p, st_p = rwkv_chunk(feats_p, jnp.zeros((B, width // PAIR, PAIR, PAIR), F32), rw_ln_g[l], rw_ln_b[l],
                            L=RWKV_CHUNK)
    tr = lambda z: z.transpose(0, 2, 1)
    k_pairs = kb_p.reshape(B, S, n_heads // 2, PAIR).transpose(0, 2, 1, 3)
    at_p = dsa_prompt(ik_p.astype(BF16), tr(iq), tr(iw * (ih ** -0.5 * idx_dim ** -0.5))[:, :, None, :],
                      k_pairs, tr(q), tr(vb_p), tiles[2:4], topk=topk_p)
    x1_p = matmul_residual(xp, rw_rows(rw_p), tr(at_p).reshape(B * S, at_w), w_out_b, tm=ROW_TILE, tn=D)
    mkv = norm_matmul(mem_prompt.reshape(B * n_mem, D), g_mem[l], w_ckv, tm=MEM_ROWS, tn=xw)
    mk_p = mkv[:, :xw].reshape(B, n_mem, xw)
    mv_p = mkv[:, xw:].reshape(B, n_mem, xw)
    x2_p = cross_attn(x1_p, g_cross[l], w_cq_b, mk_p, mv_p, w_co_b,
                      groups=1, t_rows=ROW_TILE, seq_tiles=S // ROW_TILE, n_heads=x_heads)

    xs = x_sample.reshape(DB * T, D)
    fm_s, q2, k_s, kb_s, v_s, vb_s, iq2, ft_s, ik_s, iw2, shift_s = project(xs, DB, T, DB * T)
    feats_s = rwkv_prep(fm_s, ft_s, state_shift[l], *rw_args, tm=T, width=width)
    rw_s, st_s = rwkv_chunk(feats_s, _state_to_block_diag(state_wkv[l]), rw_ln_g[l], rw_ln_b[l], L=T)
    r4 = lambda z, h_: z.reshape(DB, T, h_, -1)
    sel_args, att_args = _dsa_sample_inputs(r4(q2, n_heads), r4(kb_s, n_heads), r4(vb_s, n_heads), r4(iq2, ih),
                                            ik_s, iw2, tiles[0:2])
    mask_s = dsa_sample_select(page_table, *sel_args, _token_minor_cache(cache_idx_k), layer=l,
                               n_idx_heads=ih, t_new=T, topk=topk_s, group=SELECT_GROUP)
    at_s = dsa_sample(page_table, mask_s, *att_args, _token_minor_cache(cache_k), _token_minor_cache(cache_v),
                      layer=l, n_heads=n_heads, t_new=T, chunk=SAMPLE_CHUNK_PAGES)
    x1_s = matmul_residual(xs, rw_rows(rw_s), at_s.reshape(DB * T, at_w), w_out_b, tm=DB * T, tn=D)
    x2_s = cross_attn(x1_s, g_cross[l], w_cq_b, cache_mem_k[l].reshape(DB, n_mem, xw),
                      cache_mem_v[l].reshape(DB, n_mem, xw), w_co_b,
                      groups=CROSS_GROUP, t_rows=T, seq_tiles=1, n_heads=x_heads)

    h_p, route_p = router(x2_p, g_ffn[l], w_r, b_r, tm=ROW_TILE, n_groups=w_rg.shape[2],
                          per_group=w_re.shape[2] // w_rg.shape[2])
    h_s, route_s = router(x2_s, g_ffn[l], w_r, b_r, tm=DB * T, n_groups=w_rg.shape[2],
                          per_group=w_re.shape[2] // w_rg.shape[2])
    h_all = jnp.concatenate([h_p, h_s], axis=0)
    eids = jnp.concatenate([route_p[:, top_k:2 * top_k], route_s[:, top_k:2 * top_k]], axis=0).astype(I32)
    tm_moe = DB * T
    n_p, n_all = B * S, B * S + DB * T
    assert n_p % tm_moe == 0
    tile_expert, tile_rows, row_tok, row_dst = _moe_dispatch(eids, n_experts, tm_moe)
    y_slabs = moe_ffn(tile_expert, tile_rows, row_tok, row_dst, h_all, n_all * top_k,
                      w_e1[l], w_e3[l], w_e2[l], tm=tm_moe)
    y_p = moe_combine(x2_p, route_p, g_final, y_slabs, tm=tm_moe, top_k=top_k, tile_offset=0)
    y_s = moe_combine(x2_s, route_s, g_final, y_slabs, tm=tm_moe, top_k=top_k, tile_offset=n_p // tm_moe)

    return (y_p.reshape(B, S, D), y_s.reshape(DB, T, D),
            k_p, v_p, ik_p[None], _state_from_block_diag(st_p)[None], shift_p[None],
            mk_p.reshape(1, B, n_mem, x_heads, xw // x_heads), mv_p.reshape(1, B, n_mem, x_heads, xw // x_heads),
            k_s, v_s, ik_s[None], _state_from_block_diag(st_s)[None], shift_s[None])
```

```python
import functools
import math

import jax
import jax.numpy as jnp
from jax import lax
from jax.experimental import pallas as pl
from jax.experimental.pallas import tpu as pltpu

F32 = jnp.float32
BF16 = jnp.bfloat16
I32 = jnp.int32

LANES = 128
SUBLANES = 8
VMEM_LIMIT_BYTES = 56 * 1024 * 1024

HEAD_DIM = 64
PAIR = 2 * HEAD_DIM
GN_EPS = 64e-5
NORM_EPS = 1e-6
TOPK_MAX = 256
Q_BLOCK = 128
N_BUCKETS = 32
MAX_DISTANCE = 128
PAGE_SIZE = 128
INT_MIN = -(2 ** 31)

PROJ_ROWS = 1024
PROJ_COLS = 512
PREP_ROWS = 512
RWKV_CHUNK = 64
ROW_TILE = 512
MEM_ROWS = 256
SAMPLE_CHUNK_PAGES = 16
SELECT_GROUP = 8
CROSS_GROUP = 8


def _cparams(*sem):
    return pltpu.CompilerParams(dimension_semantics=sem, vmem_limit_bytes=VMEM_LIMIT_BYTES)


def _mm(a, b):
    return jnp.dot(a.astype(BF16), b.astype(BF16), preferred_element_type=F32)


def _mm_nt(a, b):
    return lax.dot_general(a.astype(BF16), b.astype(BF16), (((1,), (1,)), ((), ())),
                           preferred_element_type=F32)


def _split2(x):
    hi = x.astype(BF16)
    lo = (x - hi.astype(F32)).astype(BF16)
    return hi, lo


def _mm3(a, b):
    ah, al = _split2(a)
    bh, bl = _split2(b)
    d = lambda x, y: jnp.dot(x, y, preferred_element_type=F32)
    return d(ah, bh) + (d(ah, bl) + d(al, bh))


def _mm_exact_rhs(a, b_bf16):
    hi = a.astype(BF16)
    r1 = a - hi.astype(F32)
    mid = r1.astype(BF16)
    lo = (r1 - mid.astype(F32)).astype(BF16)
    d = lambda x: jnp.dot(x, b_bf16, preferred_element_type=F32)
    return d(hi) + (d(mid) + d(lo))


def _rmsnorm(x, g):
    ms = jnp.mean(x * x, axis=-1, keepdims=True)
    return x * lax.rsqrt(ms + NORM_EPS) * g


def _norm_matmul_body(x_ref, g_ref, w_ref, o_ref, xn_ref):
    @pl.when(pl.program_id(1) == 0)
    def _():
        xn_ref[...] = _rmsnorm(x_ref[...], g_ref[...]).astype(BF16)

    o_ref[...] = _mm(xn_ref[...], w_ref[...])


def norm_matmul(x, g, w, *, tm, tn):
    n, d = x.shape
    m = w.shape[1]
    return pl.pallas_call(
        _norm_matmul_body,
        grid=(n // tm, m // tn),
        in_specs=[pl.BlockSpec((tm, d), lambda i, j: (i, 0)),
                  pl.BlockSpec((1, d), lambda i, j: (0, 0)),
                  pl.BlockSpec((d, tn), lambda i, j: (0, j))],
        out_specs=pl.BlockSpec((tm, tn), lambda i, j: (i, j)),
        out_shape=jax.ShapeDtypeStruct((n, m), F32),
        scratch_shapes=[pltpu.VMEM((tm, d), BF16)],
        compiler_params=_cparams("parallel", "arbitrary"),
        name="norm_matmul",
    )(x, g.reshape(1, d), w)


def _norm_matmul_split_body(x_ref, g_ref, w_ref, *rest, bounds):
    o_refs, xn_ref = rest[:-1], rest[-1]
    j = pl.program_id(1)

    @pl.when(j == 0)
    def _():
        xn_ref[...] = _rmsnorm(x_ref[...], g_ref[...]).astype(BF16)

    res = _mm_nt(xn_ref[...], w_ref[...])
    tm, tn = res.shape
    for o_ref, (lo, hi) in zip(o_refs, bounds):
        @pl.when((j >= lo) & (j < hi))
        def _(o_ref=o_ref, lo=lo, hi=hi):
            if o_ref.shape[1] == HEAD_DIM:
                n_heads = (hi - lo) * tn // HEAD_DIM
                per_tile = tn // HEAD_DIM
                for c in range(hi - lo):
                    @pl.when(j == lo + c)
                    def _(c=c):
                        for hh in range(per_tile):
                            o_ref[pl.ds(c * per_tile + hh, tm, stride=n_heads), :] = (
                                res[:, hh * HEAD_DIM:(hh + 1) * HEAD_DIM].astype(o_ref.dtype))
            else:
                o_ref[...] = res.astype(o_ref.dtype)


def norm_matmul_split(x, g, wt, segments, *, tm, tn):
    n, d = x.shape
    m = wt.shape[0]
    bounds, out_specs, out_shape, lo = [], [], [], 0
    for nt, outs in segments:
        hi = lo + nt
        for per_head, dt in outs:
            bounds.append((lo, hi))
            if per_head:
                heads = nt * tn // HEAD_DIM
                out_specs.append(pl.BlockSpec((tm * heads, HEAD_DIM), lambda i, j: (i, 0),
                                              pipeline_mode=pl.Buffered(1)))
                out_shape.append(jax.ShapeDtypeStruct((n * heads, HEAD_DIM), dt))
            else:
                out_specs.append(pl.BlockSpec((tm, tn),
                                              lambda i, j, lo=lo, hi=hi: (i, jnp.clip(j - lo, 0, hi - lo - 1))))
                out_shape.append(jax.ShapeDtypeStruct((n, nt * tn), dt))
        lo = hi
    assert lo * tn == m
    return pl.pallas_call(
        functools.partial(_norm_matmul_split_body, bounds=tuple(bounds)),
        grid=(n // tm, m // tn),
        in_specs=[pl.BlockSpec((tm, d), lambda i, j: (i, 0), pipeline_mode=pl.Buffered(1)),
                  pl.BlockSpec((1, d), lambda i, j: (0, 0)),
                  pl.BlockSpec((tn, d), lambda i, j: (j, 0))],
        out_specs=out_specs,
        out_shape=out_shape,
        scratch_shapes=[pltpu.VMEM((tm, d), BF16)],
        compiler_params=_cparams("arbitrary", "arbitrary"),
        name="norm_matmul_split",
    )(x, g.reshape(1, d), wt)


def _pair_ones():
    r = lax.broadcasted_iota(I32, (PAIR, PAIR), 0) // HEAD_DIM
    c = lax.broadcasted_iota(I32, (PAIR, PAIR), 1) // HEAD_DIM
    return (r == c).astype(BF16)


def _head_sum(x, ones_bd):
    return _mm_exact_rhs(x, ones_bd)


def _rwkv_prep_body(f_ref, prev8_ref, init_ref, mu_ref, ft_ref, tprev8_ref, tinit_ref, tmu_ref,
                    w0_ref, w2_ref, a0_ref, a2_ref, g2_ref, kk_ref, ka_ref, rk_ref,
                    r_o, k_o, v_o, kk_o, b_o, ld_o, g_o, bon_o, *, tm, width):
    i = pl.program_id(1)

    def token_shift(f, p8_ref, i_ref, m_ref):
        nc = m_ref.shape[1]
        prev_row = jnp.where(i == 0, i_ref[0], p8_ref[0, SUBLANES - 1:SUBLANES, :nc])
        rolled = pltpu.roll(f, shift=1, axis=0)
        row = lax.broadcasted_iota(I32, f.shape, 0)
        f_prev = jnp.where(row == 0, prev_row, rolled)
        return f + (f_prev - f) * m_ref[...]

    fs = token_shift(f_ref[0], prev8_ref, init_ref, mu_ref)
    n_dec = w2_ref.shape[0]
    n_icl = a2_ref.shape[0]
    n_lora = tmu_ref.shape[1]
    ts = token_shift(ft_ref[0][:, :n_lora], tprev8_ref, tinit_ref, tmu_ref)
    w_ = width
    r = fs[:, 0:w_]
    k = fs[:, w_:2 * w_]
    v = fs[:, 2 * w_:3 * w_]
    wd = ts[:, 0:n_dec]
    ad = ts[:, n_dec:n_dec + n_icl]
    gd = ts[:, n_dec + n_icl:]
    z = w0_ref[...] + _mm3(jnp.tanh(wd), w2_ref[...])
    nz = -z
    softplus = jnp.maximum(nz, 0.0) + jnp.log(1.0 + jnp.exp(-jnp.abs(nz)))
    w = -softplus - 0.5
    ld = -jnp.exp(w)
    a = 1.0 / (1.0 + jnp.exp(-(a0_ref[...] + _mm3(ad, a2_ref[...]))))
    g = _mm3(1.0 / (1.0 + jnp.exp(-gd)), g2_ref[...])
    kk = k * kk_ref[...]
    k2 = k * (1.0 + (a - 1.0) * ka_ref[...])
    rk = r * k2 * rk_ref[...]
    ones_bd = _pair_ones()
    for p in range(w_ // PAIR):
        sl = slice(p * PAIR, (p + 1) * PAIR)
        kkp = kk[:, sl]
        nrm = jnp.sqrt(_head_sum(kkp * kkp, ones_bd))
        kkp = kkp / jnp.maximum(nrm, 1e-12)
        ap = a[:, sl]
        r_o[0, p] = r[:, sl]
        k_o[0, p] = k2[:, sl]
        v_o[0, p] = v[:, sl]
        kk_o[0, p] = kkp
        b_o[0, p] = kkp * ap
        ld_o[0, p] = ld[:, sl]
        g_o[0, p] = g[:, sl]
        bon_o[0, p] = _head_sum(rk[:, sl], ones_bd) * v[:, sl]


def rwkv_prep(f_main, f_tail, init_prev, mu, w0, w2, a0, a2, g2, k_k, k_a, r_k, *, tm, width):
    b, t, n_main = f_main.shape
    n_tail = f_tail.shape[2]
    n_lora = mu.shape[0] - n_main
    npair = width // PAIR
    row1 = lambda x: x.reshape(1, -1)
    kern = functools.partial(_rwkv_prep_body, tm=tm, width=width)
    full = lambda a: pl.BlockSpec(a.shape, lambda bi, i: (0,) * a.ndim)
    args = [row1(w0), w2, row1(a0), a2, g2, row1(k_k), row1(k_a), row1(r_k)]
    out_spec = pl.BlockSpec((1, npair, tm, PAIR), lambda bi, i: (bi, 0, i, 0))
    out_shape = jax.ShapeDtypeStruct((b, npair, t, PAIR), F32)
    prev8_map = lambda bi, i: (bi, jnp.maximum(i * (tm // SUBLANES) - 1, 0), 0)

    def feature_specs(ncols, n_init):
        return [pl.BlockSpec((1, tm, ncols), lambda bi, i: (bi, i, 0)),
                pl.BlockSpec((1, SUBLANES, ncols), prev8_map),
                pl.BlockSpec((1, 1, n_init), lambda bi, i: (bi, 0, 0)),
                pl.BlockSpec((1, n_init), lambda bi, i: (0, 0))]

    return pl.pallas_call(
        kern,
        grid=(b, t // tm),
        in_specs=feature_specs(n_main, n_main) + feature_specs(n_tail, n_lora) + [full(a) for a in args],
        out_specs=[out_spec] * 8,
        out_shape=[out_shape] * 8,
        compiler_params=_cparams("parallel", "parallel"),
        name="rwkv_prep",
    )(f_main, f_main, init_prev[:, :n_main].reshape(b, 1, n_main), row1(mu[:n_main]),
      f_tail, f_tail, init_prev[:, n_main:].reshape(b, 1, n_lora), row1(mu[n_main:]), *args)


def _rwkv_chunk_body(r_ref, k_ref, v_ref, kk_ref, b_ref, ld_ref, g_ref, bon_ref, s0_ref,
                     lng_ref, lnb_ref, o_ref, st_ref, s_ref, *, L, npair, group):
    c = pl.program_id(1)

    @pl.when(c == 0)
    def _():
        s_ref[...] = s0_ref[0]

    L2 = 2 * L
    row = lax.broadcasted_iota(I32, (L2, L2), 0)
    col = lax.broadcasted_iota(I32, (L2, L2), 1)
    eye = (row == col).astype(F32)
    row4 = lax.broadcasted_iota(I32, (2 * L2, 2 * L2), 0)
    col4 = lax.broadcasted_iota(I32, (2 * L2, 2 * L2), 1)
    rr, cc = row4 % L2, col4 % L2
    tri_all = ((rr // L) == (cc // L)) & ((cc < rr) | ((row4 >= L2) & (cc == rr)))
    tr = lax.broadcasted_iota(I32, (L, L), 0)
    tc = lax.broadcasted_iota(I32, (L, L), 1)
    cum_mat = (tc <= tr).astype(BF16)
    lane = lax.broadcasted_iota(I32, (L, PAIR), 1)
    first = lane < HEAD_DIM
    ones_bd = _pair_ones()
    n_sq = max(int(math.ceil(math.log2(L))) - 1, 0)

    def block_diag(x):
        return jnp.concatenate([jnp.where(first, x, 0.0), jnp.where(first, 0.0, x)], axis=0)

    def group_step(gi, carry):
        ps = [gi * group + j for j in range(group)]
        each = lambda f, *cols: [f(*args) for args in zip(*cols)]
        ld = [ld_ref[0, p] for p in ps]
        cum = each(lambda x: _mm_exact_rhs_t(cum_mat, x), ld)
        dec = each(jnp.exp, cum)
        dec_inv = each(lambda c_: jnp.exp(-c_), cum)
        a_t = each(lambda p, c_, l_: block_diag(-kk_ref[0, p] * jnp.exp(c_ - l_)), ps, cum, ld)
        b_t = each(lambda p, e: block_diag(b_ref[0, p] * e), ps, dec_inv)
        k_t = each(lambda p, e: block_diag(k_ref[0, p] * e), ps, dec_inv)
        r_t = each(lambda p, e: block_diag(r_ref[0, p] * e), ps, dec)
        v_b = each(lambda p: block_diag(v_ref[0, p]), ps)
        ar = each(lambda a, r: jnp.concatenate([a, r], axis=0), a_t, r_t)
        bk = each(lambda b, k: jnp.concatenate([b, k], axis=0), b_t, k_t)
        cross = each(lambda x_, y_: jnp.where(tri_all, _mm_nt(x_, y_), 0.0), ar, bk)
        a_ab = [c_[:L2, :L2] for c_ in cross]
        a_rb = [c_[L2:, :L2] for c_ in cross]
        akrk = [c_[:, L2:] for c_ in cross]
        t_inv = each(lambda a: eye + a, a_ab)
        if n_sq >= 1:
            x = each(lambda a: _mm(a, a), a_ab)
            for _ in range(n_sq - 1):
                xt = each(lambda x_, t: _mm(x_, jnp.concatenate([x_, t], axis=1)), x, t_inv)
                x = [z[:, :L2] for z in xt]
                t_inv = each(lambda t, z: t + z[:, L2:], t_inv, xt)
            t_inv = each(lambda t, x_: t + _mm(x_, t), t_inv, x)
        s = [s_ref[p] for p in ps]
        ar_s = each(_mm_nt, ar, s)
        akrk_v = each(_mm, akrk, v_b)
        u = each(lambda t, p1, p2: _mm(t, p1[:L2] + p2[:L2]), t_inv, ar_s, akrk_v)
        y_b = each(lambda p1, rb, u_, p2: p1[L2:] + _mm(rb, u_) + p2[L2:], ar_s, a_rb, u, akrk_v)
        s_new = each(lambda s_, u_, b, v, k, d: (s_ + _mm(u_.T, b) + _mm(v.T, k)) * d[L - 1:L, :],
                     s, u, b_t, v_b, k_t, dec)
        for p, sn in zip(ps, s_new):
            s_ref[p] = sn
        y = each(lambda yb: yb[:L] + yb[L:], y_b)
        mean = each(lambda y_: _head_sum(y_, ones_bd) * (1.0 / HEAD_DIM), y)
        d = each(lambda y_, m: y_ - m, y, mean)
        var = each(lambda d_: _head_sum(d_ * d_, ones_bd) * (1.0 / HEAD_DIM), d)
        for p, d_, v_ in zip(ps, d, var):
            yn = d_ * lax.rsqrt(v_ + GN_EPS) * lng_ref[p] + lnb_ref[p]
            o_ref[0, p] = (yn + bon_ref[0, p]) * g_ref[0, p]
        return carry

    lax.fori_loop(0, npair // group, group_step, 0)

    @pl.when(c == pl.num_programs(1) - 1)
    def _():
        st_ref[0] = s_ref[...]


def _mm_exact_rhs_t(m_bf16, x):
    hi = x.astype(BF16)
    r1 = x - hi.astype(F32)
    mid = r1.astype(BF16)
    lo = (r1 - mid.astype(F32)).astype(BF16)
    d = lambda y: jnp.dot(m_bf16, y, preferred_element_type=F32)
    return d(hi) + (d(mid) + d(lo))


def rwkv_chunk(feats, s0_bd, ln_g, ln_b, *, L, group=8):
    b, npair, t, _ = feats[0].shape
    blk = pl.BlockSpec((1, npair, L, PAIR), lambda bi, c: (bi, 0, c, 0))
    st_spec = pl.BlockSpec((1, npair, PAIR, PAIR), lambda bi, c: (bi, 0, 0, 0))
    par_spec = pl.BlockSpec((npair, 1, PAIR), lambda bi, c: (0, 0, 0))
    kern = functools.partial(_rwkv_chunk_body, L=L, npair=npair, group=group)
    return pl.pallas_call(
        kern,
        grid=(b, t // L),
        in_specs=[blk] * 8 + [st_spec, par_spec, par_spec],
        out_specs=[blk, st_spec],
        out_shape=[jax.ShapeDtypeStruct((b, npair, t, PAIR), F32),
                   jax.ShapeDtypeStruct((b, npair, PAIR, PAIR), F32)],
        scratch_shapes=[pltpu.VMEM((npair, PAIR, PAIR), F32)],
        compiler_params=_cparams("parallel", "arbitrary"),
        name="rwkv_chunk",
    )(*feats, s0_bd, ln_g.reshape(npair, 1, PAIR), ln_b.reshape(npair, 1, PAIR))


def _state_to_block_diag(s):
    b, h, n, _ = s.shape
    s = s.reshape(b, h // 2, 2, n, n)
    z = jnp.zeros_like(s[:, :, 0])
    top = jnp.concatenate([s[:, :, 0], z], axis=-1)
    bot = jnp.concatenate([z, s[:, :, 1]], axis=-1)
    return jnp.concatenate([top, bot], axis=-2)


def _state_from_block_diag(s_bd):
    b, p, _, _ = s_bd.shape
    n = HEAD_DIM
    return jnp.stack([s_bd[:, :, :n, :n], s_bd[:, :, n:, n:]], axis=2).reshape(b, 2 * p, n, n)


def _t5_bucket(dist):
    exact = N_BUCKETS // 2
    d = jnp.maximum(dist, 0)
    far = exact + (jnp.log(jnp.maximum(d, 1).astype(F32) / exact) / math.log(MAX_DISTANCE / exact)
                   * (N_BUCKETS - exact)).astype(I32)
    return jnp.where(d < exact, d, jnp.minimum(far, N_BUCKETS - 1))


def _bias_tables_body(rb_ref, o_ref, *, offsets, n_heads):
    r = lax.broadcasted_iota(I32, (Q_BLOCK, Q_BLOCK), 0)
    c = lax.broadcasted_iota(I32, (Q_BLOCK, Q_BLOCK), 1)
    for t, (off, key_major) in enumerate(offsets):
        bucket = _t5_bucket((c - r if key_major else r - c) + off)
        for h in range(n_heads):
            def body(bk, acc):
                return jnp.where(bucket == bk, rb_ref[bk, h], acc)
            tile = lax.fori_loop(0, N_BUCKETS, body, jnp.zeros((Q_BLOCK, Q_BLOCK), F32))
            o_ref[t, h] = tile - rb_ref[N_BUCKETS - 1, h]


def bias_tables(rel_bias, offsets):
    n_heads = rel_bias.shape[1]
    kern = functools.partial(_bias_tables_body, offsets=tuple(offsets), n_heads=n_heads)
    return pl.pallas_call(
        kern,
        in_specs=[pl.BlockSpec(memory_space=pltpu.SMEM)],
        out_specs=pl.BlockSpec(memory_space=pltpu.VMEM),
        out_shape=jax.ShapeDtypeStruct((len(offsets), n_heads, Q_BLOCK, Q_BLOCK), F32),
        name="bias_tables",
    )(rel_bias)


def _sortable_key(scores):
    bits = lax.bitcast_convert_type(scores + 0.0, I32)
    return jnp.where(bits < 0, bits ^ 0x7FFFFFFF, bits)


def _count(mask):
    return jnp.sum(mask.astype(F32), axis=-1, keepdims=True)


def _topk_select(key, topk, n_index_bits):
    rows, n = key.shape
    kf = float(topk)
    t0 = jnp.where(_count(key >= 0) >= kf, 0, INT_MIN).astype(I32)

    def value_bit(i, t):
        cand = t + lax.shift_left(jnp.int32(1), 30 - i)
        return jnp.where(_count(key >= cand) >= kf, cand, t)

    thr = lax.fori_loop(0, 31, value_bit, t0)
    above = key > thr
    ties = key == thr
    need = kf - _count(above)
    idx = lax.broadcasted_iota(I32, (rows, n), 1)

    def lowest_ties():
        def index_bit(i, m):
            cand = m + lax.shift_left(jnp.int32(1), n_index_bits - 1 - i)
            return jnp.where(_count(ties & (idx < cand)) <= need, cand, m)
        return lax.fori_loop(0, n_index_bits, index_bit, jnp.zeros((rows, 1), I32))

    surplus = jnp.max(_count(ties) - need) > 0.0
    m = lax.cond(surplus, lowest_ties, lambda: jnp.full((rows, 1), 2 ** n_index_bits, I32))
    return above | (ties & (idx < m))


SUM_CHAINS = 4
HEAD_GROUP = 4


def _sum_rows(x):
    r = x.shape[0]
    if r % (SUM_CHAINS * SUBLANES) == 0 and r > SUM_CHAINS * SUBLANES:
        x = jnp.sum(x.reshape(SUM_CHAINS, r // SUM_CHAINS, x.shape[1]), axis=1)
    return jnp.sum(x, axis=0, keepdims=True)


def _max_rows(x):
    r = x.shape[0]
    if r % (SUM_CHAINS * SUBLANES) == 0 and r > SUM_CHAINS * SUBLANES:
        x = jnp.max(x.reshape(SUM_CHAINS, r // SUM_CHAINS, x.shape[1]), axis=1)
    return jnp.max(x, axis=0, keepdims=True)


def _topk_select_cols(key, topk, n_index_bits):
    n, cols = key.shape
    kf = float(topk)
    cnt = lambda m: _sum_rows(m.astype(F32))
    t0 = jnp.where(cnt(key >= 0) >= kf, 0, INT_MIN).astype(I32)

    def value_bit(i, t):
        cand = t + lax.shift_left(jnp.int32(1), 30 - i)
        return jnp.where(cnt(key >= cand) >= kf, cand, t)

    thr = lax.fori_loop(0, 31, value_bit, t0)
    above = key > thr
    ties = key == thr
    need = kf - cnt(above)
    idx = lax.broadcasted_iota(I32, (n, cols), 0)

    def lowest_ties():
        def index_bit(i, m):
            cand = m + lax.shift_left(jnp.int32(1), n_index_bits - 1 - i)
            return jnp.where(cnt(ties & (idx < cand)) <= need, cand, m)
        return lax.fori_loop(0, n_index_bits, index_bit, jnp.zeros((1, cols), I32))

    surplus = jnp.max(cnt(ties) - need) > 0.0
    m = lax.cond(surplus, lowest_ties, lambda: jnp.full((1, cols), 2 ** n_index_bits, I32))
    return above | (ties & (idx < m))


def _dsa_prompt_block(nb, ik_ref, iqt_ref, wt_ref, k_ref, qt_ref, vt_ref, bias_ref, o_ref, mask_ref,
                      *, n_heads, n_idx_heads, topk):
    w = nb * Q_BLOCK
    ik = ik_ref[0, :w, :]
    idx_dim = ik.shape[1]

    def head_rows(ref, h, dh, cols=slice(None)):
        return ref[0, pl.ds(pl.multiple_of(h * dh, dh), dh), cols]

    def idx_heads(gi, acc):
        hs = [gi * HEAD_GROUP + j for j in range(HEAD_GROUP)]
        dots = [jnp.dot(ik, head_rows(iqt_ref, h, idx_dim), preferred_element_type=F32) for h in hs]
        terms = [jnp.maximum(d, 0.0) * wt_ref[0, h] for d, h in zip(dots, hs)]
        while len(terms) > 1:
            terms = [a + b for a, b in zip(terms[0::2], terms[1::2])]
        return acc + terms[0]

    scores = lax.fori_loop(0, n_idx_heads // HEAD_GROUP, idx_heads, jnp.zeros((w, Q_BLOCK), F32))
    kpos = lax.broadcasted_iota(I32, (w, Q_BLOCK), 0)
    qpos = (nb - 1) * Q_BLOCK + lax.broadcasted_iota(I32, (w, Q_BLOCK), 1)
    valid = kpos <= qpos
    if w <= topk:
        sel = valid
    else:
        key = jnp.where(valid, _sortable_key(scores), INT_MIN)
        sel = valid & _topk_select_cols(key, topk, int(math.ceil(math.log2(w))) + 1)
    mask_ref[:w, :] = jnp.where(sel, 0.0, -jnp.inf)

    def with_near_bias(logits, h):
        near = [logits[w - Q_BLOCK:] + bias_ref[0, h]]
        if nb >= 2:
            near = [logits[w - 2 * Q_BLOCK:w - Q_BLOCK] + bias_ref[1, h]] + near
        if nb >= 3:
            near = [logits[:w - 2 * Q_BLOCK]] + near
        return jnp.concatenate(near, axis=0) if len(near) > 1 else near[0]

    pair_row_half = lax.broadcasted_iota(I32, (PAIR, Q_BLOCK), 0) // HEAD_DIM

    def head_logits(gi, j):
        p = gi * (HEAD_GROUP // 2) + j // 2
        qt_pair = head_rows(qt_ref, p, PAIR)
        qt_head = jnp.where(pair_row_half == j % 2, qt_pair, jnp.zeros_like(qt_pair))
        return jnp.dot(k_ref[0, p, :w, :], qt_head, preferred_element_type=F32)

    def attn_heads(gi, carry):
        hs = [gi * HEAD_GROUP + j for j in range(HEAD_GROUP)]
        mask = mask_ref[:w, :]
        logits = [head_logits(gi, j) + mask for j in range(HEAD_GROUP)]
        logits = [with_near_bias(l_, h) for l_, h in zip(logits, hs)]
        mx = [_max_rows(l_) for l_ in logits]
        e = [jnp.exp(l_ - m_) for l_, m_ in zip(logits, mx)]
        den = [_sum_rows(e_) for e_ in e]
        o = [jnp.dot(head_rows(vt_ref, h, HEAD_DIM, slice(0, w)), e_.astype(BF16),
                     preferred_element_type=F32) for h, e_ in zip(hs, e)]
        for h, o_, d_ in zip(hs, o, den):
            o_ref[0, pl.ds(pl.multiple_of(h * HEAD_DIM, HEAD_DIM), HEAD_DIM), :] = (o_ / d_).astype(o_ref.dtype)
        return carry

    lax.fori_loop(0, n_heads // HEAD_GROUP, attn_heads, 0)


def _dsa_prompt_body(ik_ref, iqt_ref, wt_ref, k_ref, qt_ref, vt_ref, bias_ref, o_ref, mask_ref,
                     *, n_heads, n_idx_heads, seq, topk):
    i = pl.program_id(0)
    for nb in range(1, seq // Q_BLOCK + 1):
        @pl.when(i == nb - 1)
        def _(nb=nb):
            _dsa_prompt_block(nb, ik_ref, iqt_ref, wt_ref, k_ref, qt_ref, vt_ref, bias_ref, o_ref, mask_ref,
                              n_heads=n_heads, n_idx_heads=n_idx_heads, topk=topk)


def dsa_prompt(ik, iqt, wts, k_pairs, qt, vt, bias_tiles, *, topk):
    b, width, s = qt.shape
    h = width // HEAD_DIM
    ih = wts.shape[1]
    kern = functools.partial(_dsa_prompt_body, n_heads=h, n_idx_heads=ih, seq=s, topk=topk)
    grid_spec = pltpu.PrefetchScalarGridSpec(
        num_scalar_prefetch=0,
        grid=(s // Q_BLOCK, b),
        in_specs=[pl.BlockSpec((1, s, ik.shape[2]), lambda i, bi: (bi, 0, 0)),
                  pl.BlockSpec((1, iqt.shape[1], Q_BLOCK), lambda i, bi: (bi, 0, i)),
                  pl.BlockSpec((1, ih, 1, Q_BLOCK), lambda i, bi: (bi, 0, 0, i)),
                  pl.BlockSpec((1, h // 2, s, PAIR), lambda i, bi: (bi, 0, 0, 0)),
                  pl.BlockSpec((1, width, Q_BLOCK), lambda i, bi: (bi, 0, i)),
                  pl.BlockSpec((1, width, s), lambda i, bi: (bi, 0, 0)),
                  pl.BlockSpec(bias_tiles.shape, lambda i, bi: (0, 0, 0, 0))],
        out_specs=pl.BlockSpec((1, width, Q_BLOCK), lambda i, bi: (bi, 0, i)),
        scratch_shapes=[pltpu.VMEM((s, Q_BLOCK), F32)],
    )
    return pl.pallas_call(
        kern,
        grid_spec=grid_spec,
        out_shape=jax.ShapeDtypeStruct((b, width, s), BF16),
        compiler_params=_cparams("arbitrary", "arbitrary"),
        name="dsa_prompt",
    )(ik, iqt, wts, k_pairs, qt, vt, bias_tiles)


def _dsa_select_body(pt_ref, iq_ref, wt_ref, iknew_ref, cidx_hbm, o_ref, ikbuf, sem,
                     *, layer, n_pages, group, n_idx_heads, t_new, topk):
    s = pl.program_id(0)
    past = n_pages * PAGE_SIZE
    n_keys = past + PAGE_SIZE

    def ik_copy(i):
        g = i // n_pages
        p = i % n_pages
        page = pt_ref[(s * group + g) * n_pages + p]
        return pltpu.make_async_copy(cidx_hbm.at[layer, page], ikbuf.at[g, p], sem)

    def ik_start(i, carry):
        ik_copy(i).start()
        return carry

    def ik_wait(i, carry):
        ik_copy(i).wait()
        return carry

    lax.fori_loop(0, group * n_pages, ik_start, 0)
    for g in range(group):
        ikbuf[g, n_pages] = iknew_ref[g]
    lax.fori_loop(0, group * n_pages, ik_wait, 0)

    scores = []
    for g in range(group):
        ikt_all = jnp.concatenate([ikbuf[g, p] for p in range(n_pages + 1)], axis=1)
        dots = _mm(iq_ref[g], ikt_all)
        weighted = jnp.maximum(dots, 0.0) * wt_ref[g]
        scores.append(jnp.sum(weighted.reshape(n_idx_heads, t_new, n_keys), axis=0))
    scores = jnp.concatenate(scores, axis=0)
    shape = (group * t_new, n_keys)
    qpos = past + lax.broadcasted_iota(I32, shape, 0) % t_new
    kpos = lax.broadcasted_iota(I32, shape, 1)
    valid = kpos <= qpos
    key = jnp.where(valid, _sortable_key(scores), INT_MIN)
    sel = valid & _topk_select(key, topk, int(math.log2(n_keys)) + 1)
    o_ref[...] = jnp.where(sel, 0.0, -jnp.inf).reshape(group, t_new, n_keys)


def dsa_sample_select(page_table, iq_rows, wt_rows, ik_new_t, cache_idx_kt, *, layer, n_idx_heads, t_new, topk,
                      group):
    db, n_pages = page_table.shape
    idx_dim = cache_idx_kt.shape[2]
    n_keys = (n_pages + 1) * PAGE_SIZE
    kern = functools.partial(_dsa_select_body, layer=layer, n_pages=n_pages, group=group,
                             n_idx_heads=n_idx_heads, t_new=t_new, topk=topk)
    per_g = lambda shape: pl.BlockSpec((group,) + shape, lambda si, pt: (si,) + (0,) * len(shape))
    grid_spec = pltpu.PrefetchScalarGridSpec(
        num_scalar_prefetch=1,
        grid=(db // group,),
        in_specs=[per_g((n_idx_heads * t_new, idx_dim)), per_g((n_idx_heads * t_new, 1)),
                  per_g((idx_dim, PAGE_SIZE)), pl.BlockSpec(memory_space=pl.ANY)],
        out_specs=per_g((t_new, n_keys)),
        scratch_shapes=[pltpu.VMEM((group, n_pages + 1, idx_dim, PAGE_SIZE), F32), pltpu.SemaphoreType.DMA(())],
    )
    return pl.pallas_call(
        kern,
        grid_spec=grid_spec,
        out_shape=jax.ShapeDtypeStruct((db, t_new, n_keys), F32),
        compiler_params=_cparams("arbitrary"),
        name="dsa_sample_select",
    )(page_table.reshape(-1), iq_rows, wt_rows, ik_new_t, cache_idx_kt)


def _dsa_sample_body(pt_ref, mask_ref, qbd_ref, knew_ref, vnew_ref,
                     blast_ref, bnew_ref, ck_hbm, cv_hbm, o_ref,
                     kbuf, vbuf, sem_k, sem_v,
                     *, layer, n_pages, chunk, n_heads, t_new):
    b = pl.program_id(0)
    n_seq = pl.num_programs(0)
    past = n_pages * PAGE_SIZE
    n_chunks = n_pages // chunk
    rows = n_heads * t_new
    ck = chunk * PAGE_SIZE

    def kv_copies(seq, c, j):
        slot = c % 2
        page = pt_ref[seq * n_pages + c * chunk + j]
        return (pltpu.make_async_copy(ck_hbm.at[layer, page], kbuf.at[slot, j], sem_k.at[slot]),
                pltpu.make_async_copy(cv_hbm.at[layer, page], vbuf.at[slot, j], sem_v.at[slot]))

    def start_chunk(seq, c):
        for j in range(chunk):
            kc, vc = kv_copies(seq, c, j)
            kc.start()
            vc.start()

    def wait_chunk(c):
        for j in range(chunk):
            kc, vc = kv_copies(b, c, j)
            kc.wait()
            vc.wait()

    @pl.when(b == 0)
    def _():
        start_chunk(0, 0)

    sel_rows = jnp.tile(mask_ref[0], (n_heads, 1))

    q_rep = jnp.tile(qbd_ref[0], (n_heads, 1))
    row_head = lax.broadcasted_iota(I32, q_rep.shape, 0) // t_new
    col_head = lax.broadcasted_iota(I32, q_rep.shape, 1) // HEAD_DIM
    qbd = jnp.where(row_head == col_head, q_rep, jnp.zeros_like(q_rep))
    neg = -1e30

    def update(state, logits, maskc, vt_bf16):
        m, l, acc = state
        s = logits + maskc
        m_new = jnp.maximum(m, jnp.max(s, axis=-1, keepdims=True))
        alpha = jnp.exp(m - m_new)
        p = jnp.exp(s - m_new)
        l = alpha * l + jnp.sum(p, axis=-1, keepdims=True)
        acc = alpha * acc + _mm_nt(p, vt_bf16)
        return m_new, l, acc

    def pages_t(buf, slot):
        return jnp.concatenate([buf[slot, j] for j in range(chunk)], axis=1).astype(BF16)

    state = (jnp.full((rows, 1), neg, F32), jnp.zeros((rows, 1), F32),
             jnp.zeros((rows, qbd.shape[1]), F32))
    for c in range(n_chunks):
        if c + 1 < n_chunks:
            start_chunk(b, c + 1)
        else:
            @pl.when(b + 1 < n_seq)
            def _():
                start_chunk(b + 1, 0)
        wait_chunk(c)
        slot = c % 2
        logits = _mm(qbd, pages_t(kbuf, slot))
        if c == n_chunks - 1:
            logits = jnp.concatenate([logits[:, :ck - PAGE_SIZE],
                                      logits[:, ck - PAGE_SIZE:] + blast_ref[...]], axis=1)
        state = update(state, logits, sel_rows[:, c * ck:(c + 1) * ck], pages_t(vbuf, slot))
    logits = _mm(qbd, knew_ref[0]) + bnew_ref[...]
    m, l, acc = update(state, logits, sel_rows[:, past:], vnew_ref[0])
    out = jnp.where(row_head == col_head, acc / l, 0.0)
    o_ref[0] = jnp.sum(out.reshape(n_heads, t_new, out.shape[1]), axis=0)


def dsa_sample(page_table, mask, q_bd, k_new_t, v_new_t, bias_last, bias_new, cache_kt, cache_vt,
               *, layer, n_heads, t_new, chunk):
    db, n_pages = page_table.shape
    assert (n_pages // chunk) % 2 == 0, "chunks alternate between two buffers across sequences"
    rows = n_heads * t_new
    width = q_bd.shape[2]
    kern = functools.partial(_dsa_sample_body, layer=layer, n_pages=n_pages, chunk=chunk, n_heads=n_heads,
                             t_new=t_new)
    per_b = lambda shape: pl.BlockSpec((1,) + shape, lambda bi, pt: (bi,) + (0,) * len(shape))
    const = lambda shape: pl.BlockSpec(shape, lambda bi, pt: (0,) * len(shape))
    any_spec = pl.BlockSpec(memory_space=pl.ANY)
    grid_spec = pltpu.PrefetchScalarGridSpec(
        num_scalar_prefetch=1,
        grid=(db,),
        in_specs=[per_b((t_new, mask.shape[2])),
                  per_b((t_new, width)), per_b((width, PAGE_SIZE)), per_b((width, PAGE_SIZE)),
                  const((rows, PAGE_SIZE)), const((rows, PAGE_SIZE)),
                  any_spec, any_spec],
        out_specs=per_b((t_new, width)),
        scratch_shapes=[pltpu.VMEM((2, chunk, width, PAGE_SIZE), F32),
                        pltpu.VMEM((2, chunk, width, PAGE_SIZE), F32),
                        pltpu.SemaphoreType.DMA((2,)),
                        pltpu.SemaphoreType.DMA((2,))],
    )
    return pl.pallas_call(
        kern,
        grid_spec=grid_spec,
        out_shape=jax.ShapeDtypeStruct((db, t_new, width), F32),
        compiler_params=_cparams("arbitrary"),
        name="dsa_sample",
    )(page_table.reshape(-1), mask, q_bd, k_new_t, v_new_t, bias_last, bias_new, cache_kt, cache_vt)


def _token_minor_cache(cache):
    l, pool, page = cache.shape[:3]
    nd = cache.ndim
    return cache.transpose((0, 1) + tuple(range(3, nd)) + (2,)).reshape(l, pool, -1, page)


def _dsa_sample_inputs(q, k_new, v_new, iq, ik_new, iw, bias_tiles):
    db, t, h, dh = q.shape
    ih = iq.shape[2]
    iq_rows = iq.transpose(0, 2, 1, 3).reshape(db, ih * t, -1).astype(BF16)
    wt_rows = (iw * (ih ** -0.5 * iq.shape[3] ** -0.5)).transpose(0, 2, 1).reshape(db, ih * t, 1)
    q_bd = q.reshape(db, t, h * dh)
    page_t = lambda x: jnp.pad(x.reshape(db, t, -1).transpose(0, 2, 1), ((0, 0), (0, 0), (0, PAGE_SIZE - t)))
    bias_new = bias_tiles[0, :, :t, :].reshape(h * t, Q_BLOCK)
    bias_last = bias_tiles[1, :, :t, :].reshape(h * t, Q_BLOCK)
    return ((iq_rows, wt_rows, page_t(ik_new)),
            (q_bd.astype(BF16), page_t(k_new).astype(BF16), page_t(v_new).astype(BF16), bias_last, bias_new))


def _matmul_residual_body(x_ref, a_ref, b_ref, w_ref, o_ref):
    ka = a_ref.shape[1]
    o_ref[...] = x_ref[...] + (_mm(a_ref[...], w_ref[:ka, :]) + _mm(b_ref[...], w_ref[ka:, :]))


def matmul_residual(x, a, b, w, *, tm, tn):
    n, d = x.shape
    ka, kb = a.shape[1], b.shape[1]
    return pl.pallas_call(
        _matmul_residual_body,
        grid=(n // tm, d // tn),
        in_specs=[pl.BlockSpec((tm, tn), lambda i, j: (i, j)),
                  pl.BlockSpec((tm, ka), lambda i, j: (i, 0)),
                  pl.BlockSpec((tm, kb), lambda i, j: (i, 0)),
                  pl.BlockSpec((ka + kb, tn), lambda i, j: (0, j))],
        out_specs=pl.BlockSpec((tm, tn), lambda i, j: (i, j)),
        out_shape=jax.ShapeDtypeStruct((n, d), F32),
        compiler_params=_cparams("parallel", "parallel"),
        name="matmul_residual",
    )(x, a, b, w)


def _cross_attn_body(x_ref, g_ref, wq_ref, mk_ref, mv_ref, wo_ref, o_ref, *, groups, t_rows, n_heads, head_dim):
    x = x_ref[...]
    h = _rmsnorm(x, g_ref[...]).astype(BF16)
    q = jnp.dot(h, wq_ref[...], preferred_element_type=F32).astype(BF16)
    scale = head_dim ** -0.5
    outs = []
    for gi in range(groups):
        qg = q[gi * t_rows:(gi + 1) * t_rows]
        heads = []
        for hh in range(n_heads):
            sl = slice(hh * head_dim, (hh + 1) * head_dim)
            logits = _mm_nt(qg[:, sl], mk_ref[gi, :, sl]) * scale
            mx = jnp.max(logits, axis=-1, keepdims=True)
            e = jnp.exp(logits - mx)
            p = e / jnp.sum(e, axis=-1, keepdims=True)
            heads.append(_mm(p, mv_ref[gi, :, sl]))
        outs.append(jnp.concatenate(heads, axis=1))
    o = jnp.concatenate(outs, axis=0) if groups > 1 else outs[0]
    o_ref[...] = x + jnp.dot(o.astype(BF16), wo_ref[...], preferred_element_type=F32)


def cross_attn(x, g, wq, mk, mv, wo, *, groups, t_rows, seq_tiles, n_heads):
    n, d = x.shape
    xw = wq.shape[1]
    rows = groups * t_rows
    m = mk.shape[1]
    kern = functools.partial(_cross_attn_body, groups=groups, t_rows=t_rows, n_heads=n_heads,
                             head_dim=xw // n_heads)
    return pl.pallas_call(
        kern,
        grid=(n // rows,),
        in_specs=[pl.BlockSpec((rows, d), lambda i: (i, 0)),
                  pl.BlockSpec((1, d), lambda i: (0, 0)),
                  pl.BlockSpec((d, xw), lambda i: (0, 0)),
                  pl.BlockSpec((groups, m, xw), lambda i: (i // seq_tiles, 0, 0)),
                  pl.BlockSpec((groups, m, xw), lambda i: (i // seq_tiles, 0, 0)),
                  pl.BlockSpec((xw, d), lambda i: (0, 0))],
        out_specs=pl.BlockSpec((rows, d), lambda i: (i, 0)),
        out_shape=jax.ShapeDtypeStruct((n, d), F32),
        compiler_params=_cparams("parallel"),
        name="cross_attn",
    )(x, g.reshape(1, d), wq, mk, mv, wo)


def _router_body(x_ref, g_ref, wr_ref, br_ref, h_ref, r_ref, *, n_groups, per_group):
    h = _rmsnorm(x_ref[...], g_ref[...])
    _store_slab_rows(h_ref, h, h.shape[1] // LANES)
    logits = _mm3(h, wr_ref[...]) + br_ref[...]
    lane = lax.broadcasted_iota(I32, logits.shape, 1).astype(F32)
    big = 1e9
    first_lane = lambda hit: jnp.min(jnp.where(hit, lane, big), axis=-1, keepdims=True)
    gl = jnp.where(lane < n_groups, logits, -jnp.inf)
    gmax = jnp.max(gl, axis=-1, keepdims=True)
    grp = first_lane(gl == gmax)
    p_grp = 1.0 / jnp.sum(jnp.exp(gl - gmax), axis=-1, keepdims=True)
    e_id = lane - n_groups
    in_grp = (e_id >= grp * per_group) & (e_id < (grp + 1.0) * per_group)
    el = jnp.where(in_grp, logits, -jnp.inf)
    v1 = jnp.max(el, axis=-1, keepdims=True)
    i1 = first_lane(el == v1) - n_groups
    el2 = jnp.where(e_id == i1, -jnp.inf, el)
    v2 = jnp.max(el2, axis=-1, keepdims=True)
    i2 = first_lane(el2 == v2) - n_groups
    e2 = jnp.exp(v2 - v1)
    g1 = p_grp / (1.0 + e2)
    g2 = p_grp * e2 / (1.0 + e2)
    r_ref[...] = jnp.where(lane == 0, g1, jnp.where(lane == 1, g2, jnp.where(
        lane == 2, i1, jnp.where(lane == 3, i2, 0.0))))


def router(x, g, w_r, b_r, *, tm, n_groups, per_group):
    n, d = x.shape
    kern = functools.partial(_router_body, n_groups=n_groups, per_group=per_group)
    return pl.pallas_call(
        kern,
        grid=(n // tm,),
        in_specs=[pl.BlockSpec((tm, d), lambda i: (i, 0)),
                  pl.BlockSpec((1, d), lambda i: (0, 0)),
                  pl.BlockSpec((d, LANES), lambda i: (0, 0)),
                  pl.BlockSpec((1, LANES), lambda i: (0, 0))],
        out_specs=[pl.BlockSpec((tm * (d // LANES), LANES), lambda i: (i, 0)),
                   pl.BlockSpec((tm, LANES), lambda i: (i, 0))],
        out_shape=[jax.ShapeDtypeStruct((n * (d // LANES), LANES), F32), jax.ShapeDtypeStruct((n, LANES), F32)],
        compiler_params=_cparams("parallel"),
        name="moe_router",
    )(x, g.reshape(1, d), w_r, b_r)


def _slab_rows(buf, offset, n_rows, n_slab, stride):
    return jnp.concatenate([buf[pl.ds(offset + s, n_rows, stride=stride), :] for s in range(n_slab)], axis=1)


def _store_slab_rows(buf, x, n_slab):
    n_rows = x.shape[0]
    for s in range(n_slab):
        buf[pl.ds(s, n_rows, stride=n_slab), :] = x[:, s * LANES:(s + 1) * LANES]


def _moe_ffn_body(te_ref, nr_ref, tok_ref, dst_ref, h_hbm, w1_ref, w3_ref, w2_ref, y_hbm,
                  xbuf, ybuf, gsem, ssem, *, tm):
    t = pl.program_id(0)
    n_tiles = pl.num_programs(0)
    n_slab = h_hbm.shape[1]
    slot = t % 2

    def slab(buf, sl, r):
        return buf.at[sl, pl.ds(pl.multiple_of(r * n_slab, n_slab), n_slab)]

    def gather_copy(tile, sl, r):
        return pltpu.make_async_copy(h_hbm.at[tok_ref[tile * tm + r]], slab(xbuf, sl, r), gsem.at[sl])

    def scatter_copy(tile, sl, r):
        return pltpu.make_async_copy(slab(ybuf, sl, r), y_hbm.at[dst_ref[tile * tm + r]], ssem.at[sl])

    def start_rows(n, copy):
        def pair(i, c):
            copy(2 * i).start(priority=0)
            copy(2 * i + 1).start(priority=1)
            return c
        lax.fori_loop(0, n // 2, pair, 0)

        @pl.when(n % 2 == 1)
        def _():
            copy(n - 1).start(priority=0)

    def wait_rows(n, copy):
        def one(r, c):
            copy(r).wait()
            return c
        lax.fori_loop(0, n, one, 0)

    @pl.when(t == 0)
    def _():
        xbuf[...] = jnp.zeros(xbuf.shape, F32)
        start_rows(nr_ref[0], lambda r: gather_copy(0, 0, r))

    nxt = jnp.minimum(t + 1, n_tiles - 1)

    @pl.when(t + 1 < n_tiles)
    def _():
        start_rows(nr_ref[nxt], lambda r: gather_copy(nxt, 1 - slot, r))

    @pl.when(t >= 2)
    def _():
        prev2 = jnp.maximum(t - 2, 0)
        wait_rows(nr_ref[prev2], lambda r: scatter_copy(prev2, slot, r))

    n = nr_ref[t]

    @pl.when(n > 0)
    def _():
        wait_rows(n, lambda r: gather_copy(t, slot, r))
        x = _slab_rows(xbuf.at[slot], 0, tm, n_slab, n_slab).astype(BF16)
        a = jnp.dot(x, w1_ref[0].astype(BF16), preferred_element_type=F32)
        bgate = jnp.dot(x, w3_ref[0].astype(BF16), preferred_element_type=F32)
        u = (a / (1.0 + jnp.exp(-a))) * bgate
        y = jnp.dot(u.astype(BF16), w2_ref[0].astype(BF16), preferred_element_type=F32)
        _store_slab_rows(ybuf.at[slot], y, n_slab)
        start_rows(n, lambda r: scatter_copy(t, slot, r))

    @pl.when(t == n_tiles - 1)
    def _():
        @pl.when(t >= 1)
        def _():
            prev1 = jnp.maximum(t - 1, 0)
            wait_rows(nr_ref[prev1], lambda r: scatter_copy(prev1, 1 - slot, r))
        wait_rows(n, lambda r: scatter_copy(t, slot, r))


def moe_ffn(tile_expert, tile_rows, row_tok, row_dst, h_slabs, n_out_rows, w1, w3, w2, *, tm):
    n_tiles = tile_expert.shape[0]
    d, de = w1.shape[1], w1.shape[2]
    n_slab = d // LANES
    h3 = h_slabs.reshape(-1, n_slab, LANES)
    w_in_spec = pl.BlockSpec((1, d, de), lambda t, te, nr, tok, dst: (te[t], 0, 0))
    grid_spec = pltpu.PrefetchScalarGridSpec(
        num_scalar_prefetch=4,
        grid=(n_tiles,),
        in_specs=[pl.BlockSpec(memory_space=pl.ANY), w_in_spec, w_in_spec,
                  pl.BlockSpec((1, de, d), lambda t, te, nr, tok, dst: (te[t], 0, 0))],
        out_specs=pl.BlockSpec(memory_space=pl.ANY),
        scratch_shapes=[pltpu.VMEM((2, tm * n_slab, LANES), F32), pltpu.VMEM((2, tm * n_slab, LANES), F32),
                        pltpu.SemaphoreType.DMA((2,)), pltpu.SemaphoreType.DMA((2,))],
    )
    y = pl.pallas_call(
        functools.partial(_moe_ffn_body, tm=tm),
        grid_spec=grid_spec,
        out_shape=jax.ShapeDtypeStruct((n_out_rows, n_slab, LANES), F32),
        compiler_params=_cparams("arbitrary"),
        name="moe_ffn",
    )(tile_expert, tile_rows, row_tok, row_dst, h3, w1, w3, w2)
    return y.reshape(n_out_rows * n_slab, LANES)


def _moe_dispatch(eids, n_experts, tm):
    n, k = eids.shape
    m = n * k
    flat_e = eids.reshape(-1)
    order = jnp.argsort(flat_e, stable=True).astype(I32)
    counts = jnp.sum((flat_e[:, None] == jnp.arange(n_experts)[None, :]).astype(I32), axis=0)
    padded = (counts + tm - 1) // tm * tm
    pad_end = jnp.cumsum(padded)
    pad_start = pad_end - padded
    start = jnp.cumsum(counts) - counts
    n_tiles = -(-m // tm) + n_experts
    tile_start = jnp.arange(n_tiles) * tm
    last = jnp.maximum(pad_end[-1] - 1, 0)
    tile_expert = jnp.minimum(jnp.searchsorted(pad_end, jnp.minimum(tile_start, last), side='right'),
                              n_experts - 1).astype(I32)
    first = tile_start - pad_start[tile_expert]
    tile_rows = jnp.where(tile_start < pad_end[-1], jnp.clip(counts[tile_expert] - first, 0, tm), 0).astype(I32)
    src = jnp.clip(start[tile_expert][:, None] + first[:, None] + jnp.arange(tm)[None, :], 0, m - 1)
    row_dst = order[src].reshape(-1)
    return tile_expert, tile_rows, (row_dst // k).astype(I32), row_dst.astype(I32)


def _combine_body(x_ref, r_ref, g_ref, y_ref, o_ref, *, tm, top_k):
    n_slab = x_ref.shape[1] // LANES
    route = r_ref[...]
    x = x_ref[...]
    for kk in range(top_k):
        x = x + _slab_rows(y_ref, kk * n_slab, tm, n_slab, top_k * n_slab) * route[:, kk:kk + 1]
    o_ref[...] = _rmsnorm(x, g_ref[...])


def moe_combine(x, route, g, y_slabs, *, tm, top_k, tile_offset):
    n, d = x.shape
    rows = tm * top_k * (d // LANES)
    return pl.pallas_call(
        functools.partial(_combine_body, tm=tm, top_k=top_k),
        grid=(n // tm,),
        in_specs=[pl.BlockSpec((tm, d), lambda i: (i, 0)),
                  pl.BlockSpec((tm, LANES), lambda i: (i, 0)),
                  pl.BlockSpec((1, d), lambda i: (0, 0)),
                  pl.BlockSpec((rows, LANES), lambda i: (i + tile_offset, 0))],
        out_specs=pl.BlockSpec((tm, d), lambda i: (i, 0)),
        out_shape=jax.ShapeDtypeStruct((n, d), F32),
        compiler_params=_cparams("parallel"),
        name="moe_combine",
    )(x, route, g.reshape(1, d), y_slabs)


def kernel(x_prompt, x_sample, mem_prompt, cache_k, cache_v, cache_idx_k, page_table, state_wkv, state_shift, cache_mem_k, cache_mem_v, g_mix, w_in, mu_shift, rw_w0, rw_w2, rw_a0, rw_a2, rw_g2, rw_kk, rw_ka, rw_rk, rw_ln_g, rw_ln_b, w_out, g_cross, g_mem, w_cq, w_ck, w_cv, w_co, g_ffn, w_rg, b_rg, w_re, b_re, w_e1, w_e3, w_e2, rel_bias, g_final):
    B, S, D = x_prompt.shape
    DB, T, _ = x_sample.shape
    assert w_in.shape[0] == 1, "single-layer trunk only"
    l = 0
    n_pages = page_table.shape[1]
    past = n_pages * PAGE_SIZE
    topk_p = min(TOPK_MAX, S // 4)
    topk_s = min(TOPK_MAX, (past + T) // 4)
    rw_proj = mu_shift.shape[1]
    width = rw_w0.shape[1]
    at_w = D - width
    n_heads = at_w // HEAD_DIM
    idx_dim = cache_idx_k.shape[-1]
    ih = (w_in.shape[2] - rw_proj - 3 * at_w - idx_dim) // (idx_dim + 1)
    xw = w_cq.shape[2]
    x_heads = cache_mem_k.shape[3]
    n_mem = mem_prompt.shape[1]
    n_experts = w_e1.shape[1]
    top_k = 2
    tn = PROJ_COLS
    n_main = 3 * width
    n_lora = rw_proj - n_main
    n_att = 3 * at_w + ih * idx_dim
    n_tail = n_lora + idx_dim + ih
    assert n_main % tn == 0 and at_w % tn == 0 and (ih * idx_dim) % tn == 0 and n_tail <= tn

    w_t = w_in[l].T
    w_all = jnp.concatenate([w_t[:n_main], w_t[rw_proj:rw_proj + at_w] * HEAD_DIM ** -0.5,
                             w_t[rw_proj + at_w:rw_proj + n_att], w_t[n_main:rw_proj],
                             w_t[rw_proj + n_att:], jnp.zeros((tn - n_tail, D), w_t.dtype)], axis=0).astype(BF16)
    flat, per_head = False, True
    segments = [(n_main // tn, [(flat, F32)]),
                (at_w // tn, [(flat, BF16)]),
                (at_w // tn, [(per_head, F32), (flat, BF16)]),
                (at_w // tn, [(per_head, F32), (flat, BF16)]),
                (ih * idx_dim // tn, [(flat, BF16)]),
                (1, [(flat, F32)])]
    w_out_b = w_out[l].astype(BF16)
    w_cq_b, w_co_b = w_cq[l].astype(BF16), w_co[l].astype(BF16)
    w_ckv = jnp.concatenate([w_ck[l], w_cv[l]], axis=1).astype(BF16)
    n_route = w_rg.shape[2] + w_re.shape[2]
    w_r = jnp.pad(jnp.concatenate([w_rg[l], w_re[l]], axis=1), ((0, 0), (0, LANES - n_route)))
    b_r = jnp.pad(jnp.concatenate([b_rg[l], b_re[l]]), (0, LANES - n_route)).reshape(1, LANES)
    rw_args = (mu_shift[l], rw_w0[l], rw_w2[l], rw_a0[l], rw_a2[l], rw_g2[l], rw_kk[l], rw_ka[l],
               rw_rk[l].reshape(-1))
    tiles = bias_tables(rel_bias, ((0, False), (Q_BLOCK, False), (0, True), (Q_BLOCK, True)))

    def project(x2d, b_, t_, tm):
        f_main, q, k_heads, k, v_heads, v, iq, f_tail = norm_matmul_split(x2d, g_mix[l], w_all, segments,
                                                                          tm=tm, tn=tn)
        r3 = lambda z: z.reshape(b_, t_, -1)
        f_main, f_tail = r3(f_main), r3(f_tail)
        ik = f_tail[..., n_lora:n_lora + idx_dim]
        iw = f_tail[..., n_lora + idx_dim:n_tail]
        shift = jnp.concatenate([f_main[:, -1], f_tail[:, -1, :n_lora]], axis=-1)
        heads5 = lambda z: z.reshape(1, b_, t_, n_heads, HEAD_DIM)
        return f_main, r3(q), heads5(k_heads), r3(k), heads5(v_heads), r3(v), r3(iq), f_tail, ik, iw, shift

    def rw_rows(y):
        b_, p_, t_, _ = y.shape
        return y.transpose(0, 2, 1, 3).reshape(b_ * t_, p_ * PAIR).astype(BF16)

    xp = x_prompt.reshape(B * S, D)
    fm_p, q, k_p, kb_p, v_p, vb_p, iq, ft_p, ik_p, iw, shift_p = project(xp, B, S, PROJ_ROWS)
    feats_p = rwkv_prep(fm_p, ft_p, jnp.zeros((B, rw_proj), F32), *rw_args, tm=PREP_ROWS, width=width)
    rw_p, st_p = rwkv_chunk(feats_p, jnp.zeros((B, width // PAIR, PAIR, PAIR), F32), rw_ln_g[l], rw_ln_b[l],
                            L=RWKV_CHUNK)
    tr = lambda z: z.transpose(0, 2, 1)
    k_pairs = kb_p.reshape(B, S, n_heads // 2, PAIR).transpose(0, 2, 1, 3)
    at_p = dsa_prompt(ik_p.astype(BF16), tr(iq), tr(iw * (ih ** -0.5 * idx_dim ** -0.5))[:, :, None, :],
                      k_pairs, tr(q), tr(vb_p), tiles[2:4], topk=topk_p)
    x1_p = matmul_residual(xp, rw_rows(rw_p), tr(at_p).reshape(B * S, at_w), w_out_b, tm=ROW_TILE, tn=D)
    mkv = norm_matmul(mem_prompt.reshape(B * n_mem, D), g_mem[l], w_ckv, tm=MEM_ROWS, tn=xw)
    mk_p = mkv[:, :xw].reshape(B, n_mem, xw)
    mv_p = mkv[:, xw:].reshape(B, n_mem, xw)
    x2_p = cross_attn(x1_p, g_cross[l], w_cq_b, mk_p, mv_p, w_co_b,
                      groups=1, t_rows=ROW_TILE, seq_tiles=S // ROW_TILE, n_heads=x_heads)

    xs = x_sample.reshape(DB * T, D)
    fm_s, q2, k_s, kb_s, v_s, vb_s, iq2, ft_s, ik_s, iw2, shift_s = project(xs, DB, T, DB * T)
    feats_s = rwkv_prep(fm_s, ft_s, state_shift[l], *rw_args, tm=T, width=width)
    rw_s, st_s = rwkv_chunk(feats_s, _state_to_block_diag(state_wkv[l]), rw_ln_g[l], rw_ln_b[l], L=T)
    r4 = lambda z, h_: z.reshape(DB, T, h_, -1)
    sel_args, att_args = _dsa_sample_inputs(r4(q2, n_heads), r4(kb_s, n_heads), r4(vb_s, n_heads), r4(iq2, ih),
                                            ik_s, iw2, tiles[0:2])
    mask_s = dsa_sample_select(page_table, *sel_args, _token_minor_cache(cache_idx_k), layer=l,
                               n_idx_heads=ih, t_new=T, topk=topk_s, group=SELECT_GROUP)
    at_s = dsa_sample(page_table, mask_s, *att_args, _token_minor_cache(cache_k), _token_minor_cache(cache_v),
                      layer=l, n_heads=n_heads, t_new=T, chunk=SAMPLE_CHUNK_PAGES)
    x1_s = matmul_residual(xs, rw_rows(rw_s), at_s.reshape(DB * T, at_w), w_out_b, tm=DB * T, tn=D)
    x2_s = cross_attn(x1_s, g_cross[l], w_cq_b, cache_mem_k[l].reshape(DB, n_mem, xw),
                      cache_mem_v[l].reshape(DB, n_mem, xw), w_co_b,
                      groups=CROSS_GROUP, t_rows=T, seq_tiles=1, n_heads=x_heads)

    h_p, route_p = router(x2_p, g_ffn[l], w_r, b_r, tm=ROW_TILE, n_groups=w_rg.shape[2],
                          per_group=w_re.shape[2] // w_rg.shape[2])
    h_s, route_s = router(x2_s, g_ffn[l], w_r, b_r, tm=DB * T, n_groups=w_rg.shape[2],
                          per_group=w_re.shape[2] // w_rg.shape[2])
    h_all = jnp.concatenate([h_p, h_s], axis=0)
    eids = jnp.concatenate([route_p[:, top_k:2 * top_k], route_s[:, top_k:2 * top_k]], axis=0).astype(I32)
    tm_moe = DB * T
    n_p, n_all = B * S, B * S + DB * T
    assert n_p % tm_moe == 0
    tile_expert, tile_rows, row_tok, row_dst = _moe_dispatch(eids, n_experts, tm_moe)
    y_slabs = moe_ffn(tile_expert, tile_rows, row_tok, row_dst, h_all, n_all * top_k,
                      w_e1[l], w_e3[l], w_e2[l], tm=tm_moe)
    y_p = moe_combine(x2_p, route_p, g_final, y_slabs, tm=tm_moe, top_k=top_k, tile_offset=0)
    y_s = moe_combine(x2_s, route_s, g_final, y_slabs, tm=tm_moe, top_k=top_k, tile_offset=n_p // tm_moe)

    return (y_p.reshape(B, S, D), y_s.reshape(DB, T, D),
            k_p, v_p, ik_p[None], _state_from_block_diag(st_p)[None], shift_p[None],
            mk_p.reshape(1, B, n_mem, x_heads, xw // x_heads), mv_p.reshape(1, B, n_mem, x_heads, xw // x_heads),
            k_s, v_s, ik_s[None], _state_from_block_diag(st_s)[None], shift_s[None])
```

```python
import functools
import math

import jax
import jax.numpy as jnp
from jax import lax
from jax.experimental import pallas as pl
from jax.experimental.pallas import tpu as pltpu

F32 = jnp.float32
BF16 = jnp.bfloat16
I32 = jnp.int32

LANES = 128
SUBLANES = 8
VMEM_LIMIT_BYTES = 56 * 1024 * 1024

HEAD_DIM = 64
PAIR = 2 * HEAD_DIM
GN_EPS = 64e-5
NORM_EPS = 1e-6
TOPK_MAX = 256
Q_BLOCK = 128
N_BUCKETS = 32
MAX_DISTANCE = 128
PAGE_SIZE = 128
INT_MIN = -(2 ** 31)

PROJ_ROWS = 1024
PROJ_COLS = 512
PREP_ROWS = 512
RWKV_CHUNK = 64
ROW_TILE = 512
MEM_ROWS = 256
SAMPLE_CHUNK_PAGES = 4
SELECT_GROUP = 8
CROSS_GROUP = 8


def _cparams(*sem):
    return pltpu.CompilerParams(dimension_semantics=sem, vmem_limit_bytes=VMEM_LIMIT_BYTES)


def _mm(a, b):
    return jnp.dot(a.astype(BF16), b.astype(BF16), preferred_element_type=F32)


def _mm_nt(a, b):
    return lax.dot_general(a.astype(BF16), b.astype(BF16), (((1,), (1,)), ((), ())),
                           preferred_element_type=F32)


def _split2(x):
    hi = x.astype(BF16)
    lo = (x - hi.astype(F32)).astype(BF16)
    return hi, lo


def _mm3(a, b):
    ah, al = _split2(a)
    bh, bl = _split2(b)
    d = lambda x, y: jnp.dot(x, y, preferred_element_type=F32)
    return d(ah, bh) + (d(ah, bl) + d(al, bh))


def _mm_exact_rhs(a, b_bf16):
    hi = a.astype(BF16)
    r1 = a - hi.astype(F32)
    mid = r1.astype(BF16)
    lo = (r1 - mid.astype(F32)).astype(BF16)
    d = lambda x: jnp.dot(x, b_bf16, preferred_element_type=F32)
    return d(hi) + (d(mid) + d(lo))


def _rmsnorm(x, g):
    ms = jnp.mean(x * x, axis=-1, keepdims=True)
    return x * lax.rsqrt(ms + NORM_EPS) * g


def _norm_matmul_body(x_ref, g_ref, w_ref, o_ref, xn_ref):
    @pl.when(pl.program_id(1) == 0)
    def _():
        xn_ref[...] = _rmsnorm(x_ref[...], g_ref[...]).astype(BF16)

    o_ref[...] = _mm(xn_ref[...], w_ref[...])


def norm_matmul(x, g, w, *, tm, tn):
    n, d = x.shape
    m = w.shape[1]
    return pl.pallas_call(
        _norm_matmul_body,
        grid=(n // tm, m // tn),
        in_specs=[pl.BlockSpec((tm, d), lambda i, j: (i, 0)),
                  pl.BlockSpec((1, d), lambda i, j: (0, 0)),
                  pl.BlockSpec((d, tn), lambda i, j: (0, j))],
        out_specs=pl.BlockSpec((tm, tn), lambda i, j: (i, j)),
        out_shape=jax.ShapeDtypeStruct((n, m), F32),
        scratch_shapes=[pltpu.VMEM((tm, d), BF16)],
        compiler_params=_cparams("parallel", "arbitrary"),
        name="norm_matmul",
    )(x, g.reshape(1, d), w)


def _norm_matmul_split_body(x_ref, g_ref, w_ref, *rest, bounds):
    o_refs, xn_ref = rest[:-1], rest[-1]
    j = pl.program_id(1)

    @pl.when(j == 0)
    def _():
        xn_ref[...] = _rmsnorm(x_ref[...], g_ref[...]).astype(BF16)

    res = _mm_nt(xn_ref[...], w_ref[...])
    tm, tn = res.shape
    for o_ref, (lo, hi) in zip(o_refs, bounds):
        @pl.when((j >= lo) & (j < hi))
        def _(o_ref=o_ref, lo=lo, hi=hi):
            if o_ref.shape[1] == HEAD_DIM:
                n_heads = (hi - lo) * tn // HEAD_DIM
                per_tile = tn // HEAD_DIM
                for c in range(hi - lo):
                    @pl.when(j == lo + c)
                    def _(c=c):
                        for hh in range(per_tile):
                            o_ref[pl.ds(c * per_tile + hh, tm, stride=n_heads), :] = (
                                res[:, hh * HEAD_DIM:(hh + 1) * HEAD_DIM].astype(o_ref.dtype))
            else:
                o_ref[...] = res.astype(o_ref.dtype)


def norm_matmul_split(x, g, wt, segments, *, tm, tn):
    n, d = x.shape
    m = wt.shape[0]
    bounds, out_specs, out_shape, lo = [], [], [], 0
    for nt, outs in segments:
        hi = lo + nt
        for per_head, dt in outs:
            bounds.append((lo, hi))
            if per_head:
                heads = nt * tn // HEAD_DIM
                out_specs.append(pl.BlockSpec((tm * heads, HEAD_DIM), lambda i, j: (i, 0),
                                              pipeline_mode=pl.Buffered(1)))
                out_shape.append(jax.ShapeDtypeStruct((n * heads, HEAD_DIM), dt))
            else:
                out_specs.append(pl.BlockSpec((tm, tn),
                                              lambda i, j, lo=lo, hi=hi: (i, jnp.clip(j - lo, 0, hi - lo - 1))))
                out_shape.append(jax.ShapeDtypeStruct((n, nt * tn), dt))
        lo = hi
    assert lo * tn == m
    return pl.pallas_call(
        functools.partial(_norm_matmul_split_body, bounds=tuple(bounds)),
        grid=(n // tm, m // tn),
        in_specs=[pl.BlockSpec((tm, d), lambda i, j: (i, 0), pipeline_mode=pl.Buffered(1)),
                  pl.BlockSpec((1, d), lambda i, j: (0, 0)),
                  pl.BlockSpec((tn, d), lambda i, j: (j, 0))],
        out_specs=out_specs,
        out_shape=out_shape,
        scratch_shapes=[pltpu.VMEM((tm, d), BF16)],
        compiler_params=_cparams("arbitrary", "arbitrary"),
        name="norm_matmul_split",
    )(x, g.reshape(1, d), wt)


def _pair_ones():
    r = lax.broadcasted_iota(I32, (PAIR, PAIR), 0) // HEAD_DIM
    c = lax.broadcasted_iota(I32, (PAIR, PAIR), 1) // HEAD_DIM
    return (r == c).astype(BF16)


def _head_sum(x, ones_bd):
    return _mm_exact_rhs(x, ones_bd)


def _rwkv_prep_body(f_ref, prev8_ref, init_ref, mu_ref, ft_ref, tprev8_ref, tinit_ref, tmu_ref,
                    w0_ref, w2_ref, a0_ref, a2_ref, g2_ref, kk_ref, ka_ref, rk_ref,
                    r_o, k_o, v_o, kk_o, b_o, ld_o, g_o, bon_o, *, tm, width):
    i = pl.program_id(1)

    def token_shift(f, p8_ref, i_ref, m_ref):
        nc = m_ref.shape[1]
        prev_row = jnp.where(i == 0, i_ref[0], p8_ref[0, SUBLANES - 1:SUBLANES, :nc])
        rolled = pltpu.roll(f, shift=1, axis=0)
        row = lax.broadcasted_iota(I32, f.shape, 0)
        f_prev = jnp.where(row == 0, prev_row, rolled)
        return f + (f_prev - f) * m_ref[...]

    fs = token_shift(f_ref[0], prev8_ref, init_ref, mu_ref)
    n_dec = w2_ref.shape[0]
    n_icl = a2_ref.shape[0]
    n_lora = tmu_ref.shape[1]
    ts = token_shift(ft_ref[0][:, :n_lora], tprev8_ref, tinit_ref, tmu_ref)
    w_ = width
    r = fs[:, 0:w_]
    k = fs[:, w_:2 * w_]
    v = fs[:, 2 * w_:3 * w_]
    wd = ts[:, 0:n_dec]
    ad = ts[:, n_dec:n_dec + n_icl]
    gd = ts[:, n_dec + n_icl:]
    z = w0_ref[...] + _mm3(jnp.tanh(wd), w2_ref[...])
    nz = -z
    softplus = jnp.maximum(nz, 0.0) + jnp.log(1.0 + jnp.exp(-jnp.abs(nz)))
    w = -softplus - 0.5
    ld = -jnp.exp(w)
    a = 1.0 / (1.0 + jnp.exp(-(a0_ref[...] + _mm3(ad, a2_ref[...]))))
    g = _mm3(1.0 / (1.0 + jnp.exp(-gd)), g2_ref[...])
    kk = k * kk_ref[...]
    k2 = k * (1.0 + (a - 1.0) * ka_ref[...])
    rk = r * k2 * rk_ref[...]
    ones_bd = _pair_ones()
    for p in range(w_ // PAIR):
        sl = slice(p * PAIR, (p + 1) * PAIR)
        kkp = kk[:, sl]
        nrm = jnp.sqrt(_head_sum(kkp * kkp, ones_bd))
        kkp = kkp / jnp.maximum(nrm, 1e-12)
        ap = a[:, sl]
        r_o[0, p] = r[:, sl]
        k_o[0, p] = k2[:, sl]
        v_o[0, p] = v[:, sl]
        kk_o[0, p] = kkp
        b_o[0, p] = kkp * ap
        ld_o[0, p] = ld[:, sl]
        g_o[0, p] = g[:, sl]
        bon_o[0, p] = _head_sum(rk[:, sl], ones_bd) * v[:, sl]


def rwkv_prep(f_main, f_tail, init_prev, mu, w0, w2, a0, a2, g2, k_k, k_a, r_k, *, tm, width):
    b, t, n_main = f_main.shape
    n_tail = f_tail.shape[2]
    n_lora = mu.shape[0] - n_main
    npair = width // PAIR
    row1 = lambda x: x.reshape(1, -1)
    kern = functools.partial(_rwkv_prep_body, tm=tm, width=width)
    full = lambda a: pl.BlockSpec(a.shape, lambda bi, i: (0,) * a.ndim)
    args = [row1(w0), w2, row1(a0), a2, g2, row1(k_k), row1(k_a), row1(r_k)]
    out_spec = pl.BlockSpec((1, npair, tm, PAIR), lambda bi, i: (bi, 0, i, 0))
    out_shape = jax.ShapeDtypeStruct((b, npair, t, PAIR), F32)
    prev8_map = lambda bi, i: (bi, jnp.maximum(i * (tm // SUBLANES) - 1, 0), 0)

    def feature_specs(ncols, n_init):
        return [pl.BlockSpec((1, tm, ncols), lambda bi, i: (bi, i, 0)),
                pl.BlockSpec((1, SUBLANES, ncols), prev8_map),
                pl.BlockSpec((1, 1, n_init), lambda bi, i: (bi, 0, 0)),
                pl.BlockSpec((1, n_init), lambda bi, i: (0, 0))]

    return pl.pallas_call(
        kern,
        grid=(b, t // tm),
        in_specs=feature_specs(n_main, n_main) + feature_specs(n_tail, n_lora) + [full(a) for a in args],
        out_specs=[out_spec] * 8,
        out_shape=[out_shape] * 8,
        compiler_params=_cparams("parallel", "parallel"),
        name="rwkv_prep",
    )(f_main, f_main, init_prev[:, :n_main].reshape(b, 1, n_main), row1(mu[:n_main]),
      f_tail, f_tail, init_prev[:, n_main:].reshape(b, 1, n_lora), row1(mu[n_main:]), *args)


def _rwkv_chunk_body(r_ref, k_ref, v_ref, kk_ref, b_ref, ld_ref, g_ref, bon_ref, s0_ref,
                     lng_ref, lnb_ref, o_ref, st_ref, s_ref, *, L, npair, group):
    c = pl.program_id(1)

    @pl.when(c == 0)
    def _():
        s_ref[...] = s0_ref[0]

    L2 = 2 * L
    row = lax.broadcasted_iota(I32, (L2, L2), 0)
    col = lax.broadcasted_iota(I32, (L2, L2), 1)
    eye = (row == col).astype(F32)
    row4 = lax.broadcasted_iota(I32, (2 * L2, 2 * L2), 0)
    col4 = lax.broadcasted_iota(I32, (2 * L2, 2 * L2), 1)
    rr, cc = row4 % L2, col4 % L2
    tri_all = ((rr // L) == (cc // L)) & ((cc < rr) | ((row4 >= L2) & (cc == rr)))
    tr = lax.broadcasted_iota(I32, (L, L), 0)
    tc = lax.broadcasted_iota(I32, (L, L), 1)
    cum_mat = (tc <= tr).astype(BF16)
    lane = lax.broadcasted_iota(I32, (L, PAIR), 1)
    first = lane < HEAD_DIM
    ones_bd = _pair_ones()
    n_sq = max(int(math.ceil(math.log2(L))) - 1, 0)

    def block_diag(x):
        return jnp.concatenate([jnp.where(first, x, 0.0), jnp.where(first, 0.0, x)], axis=0)

    def group_step(gi, carry):
        ps = [gi * group + j for j in range(group)]
        each = lambda f, *cols: [f(*args) for args in zip(*cols)]
        ld = [ld_ref[0, p] for p in ps]
        cum = each(lambda x: _mm_exact_rhs_t(cum_mat, x), ld)
        dec = each(jnp.exp, cum)
        dec_inv = each(lambda c_: jnp.exp(-c_), cum)
        a_t = each(lambda p, c_, l_: block_diag(-kk_ref[0, p] * jnp.exp(c_ - l_)), ps, cum, ld)
        b_t = each(lambda p, e: block_diag(b_ref[0, p] * e), ps, dec_inv)
        k_t = each(lambda p, e: block_diag(k_ref[0, p] * e), ps, dec_inv)
        r_t = each(lambda p, e: block_diag(r_ref[0, p] * e), ps, dec)
        v_b = each(lambda p: block_diag(v_ref[0, p]), ps)
        ar = each(lambda a, r: jnp.concatenate([a, r], axis=0), a_t, r_t)
        bk = each(lambda b, k: jnp.concatenate([b, k], axis=0), b_t, k_t)
        cross = each(lambda x_, y_: jnp.where(tri_all, _mm_nt(x_, y_), 0.0), ar, bk)
        a_ab = [c_[:L2, :L2] for c_ in cross]
        a_rb = [c_[L2:, :L2] for c_ in cross]
        akrk = [c_[:, L2:] for c_ in cross]
        t_inv = each(lambda a: eye + a, a_ab)
        if n_sq >= 1:
            x = each(lambda a: _mm(a, a), a_ab)
            for _ in range(n_sq - 1):
                xt = each(lambda x_, t: _mm(x_, jnp.concatenate([x_, t], axis=1)), x, t_inv)
                x = [z[:, :L2] for z in xt]
                t_inv = each(lambda t, z: t + z[:, L2:], t_inv, xt)
            t_inv = each(lambda t, x_: t + _mm(x_, t), t_inv, x)
        s = [s_ref[p] for p in ps]
        ar_s = each(_mm_nt, ar, s)
        akrk_v = each(_mm, akrk, v_b)
        u = each(lambda t, p1, p2: _mm(t, p1[:L2] + p2[:L2]), t_inv, ar_s, akrk_v)
        y_b = each(lambda p1, rb, u_, p2: p1[L2:] + _mm(rb, u_) + p2[L2:], ar_s, a_rb, u, akrk_v)
        s_new = each(lambda s_, u_, b, v, k, d: (s_ + _mm(u_.T, b) + _mm(v.T, k)) * d[L - 1:L, :],
                     s, u, b_t, v_b, k_t, dec)
        for p, sn in zip(ps, s_new):
            s_ref[p] = sn
        y = each(lambda yb: yb[:L] + yb[L:], y_b)
        mean = each(lambda y_: _head_sum(y_, ones_bd) * (1.0 / HEAD_DIM), y)
        d = each(lambda y_, m: y_ - m, y, mean)
        var = each(lambda d_: _head_sum(d_ * d_, ones_bd) * (1.0 / HEAD_DIM), d)
        for p, d_, v_ in zip(ps, d, var):
            yn = d_ * lax.rsqrt(v_ + GN_EPS) * lng_ref[p] + lnb_ref[p]
            o_ref[0, p] = (yn + bon_ref[0, p]) * g_ref[0, p]
        return carry

    lax.fori_loop(0, npair // group, group_step, 0)

    @pl.when(c == pl.num_programs(1) - 1)
    def _():
        st_ref[0] = s_ref[...]


def _mm_exact_rhs_t(m_bf16, x):
    hi = x.astype(BF16)
    r1 = x - hi.astype(F32)
    mid = r1.astype(BF16)
    lo = (r1 - mid.astype(F32)).astype(BF16)
    d = lambda y: jnp.dot(m_bf16, y, preferred_element_type=F32)
    return d(hi) + (d(mid) + d(lo))


def rwkv_chunk(feats, s0_bd, ln_g, ln_b, *, L, group=8):
    b, npair, t, _ = feats[0].shape
    blk = pl.BlockSpec((1, npair, L, PAIR), lambda bi, c: (bi, 0, c, 0))
    st_spec = pl.BlockSpec((1, npair, PAIR, PAIR), lambda bi, c: (bi, 0, 0, 0))
    par_spec = pl.BlockSpec((npair, 1, PAIR), lambda bi, c: (0, 0, 0))
    kern = functools.partial(_rwkv_chunk_body, L=L, npair=npair, group=group)
    return pl.pallas_call(
        kern,
        grid=(b, t // L),
        in_specs=[blk] * 8 + [st_spec, par_spec, par_spec],
        out_specs=[blk, st_spec],
        out_shape=[jax.ShapeDtypeStruct((b, npair, t, PAIR), F32),
                   jax.ShapeDtypeStruct((b, npair, PAIR, PAIR), F32)],
        scratch_shapes=[pltpu.VMEM((npair, PAIR, PAIR), F32)],
        compiler_params=_cparams("parallel", "arbitrary"),
        name="rwkv_chunk",
    )(*feats, s0_bd, ln_g.reshape(npair, 1, PAIR), ln_b.reshape(npair, 1, PAIR))


def _state_to_block_diag(s):
    b, h, n, _ = s.shape
    s = s.reshape(b, h // 2, 2, n, n)
    z = jnp.zeros_like(s[:, :, 0])
    top = jnp.concatenate([s[:, :, 0], z], axis=-1)
    bot = jnp.concatenate([z, s[:, :, 1]], axis=-1)
    return jnp.concatenate([top, bot], axis=-2)


def _state_from_block_diag(s_bd):
    b, p, _, _ = s_bd.shape
    n = HEAD_DIM
    return jnp.stack([s_bd[:, :, :n, :n], s_bd[:, :, n:, n:]], axis=2).reshape(b, 2 * p, n, n)


def _t5_bucket(dist):
    exact = N_BUCKETS // 2
    d = jnp.maximum(dist, 0)
    far = exact + (jnp.log(jnp.maximum(d, 1).astype(F32) / exact) / math.log(MAX_DISTANCE / exact)
                   * (N_BUCKETS - exact)).astype(I32)
    return jnp.where(d < exact, d, jnp.minimum(far, N_BUCKETS - 1))


def _bias_tables_body(rb_ref, o_ref, *, offsets, n_heads):
    r = lax.broadcasted_iota(I32, (Q_BLOCK, Q_BLOCK), 0)
    c = lax.broadcasted_iota(I32, (Q_BLOCK, Q_BLOCK), 1)
    for t, (off, key_major) in enumerate(offsets):
        bucket = _t5_bucket((c - r if key_major else r - c) + off)
        for h in range(n_heads):
            def body(bk, acc):
                return jnp.where(bucket == bk, rb_ref[bk, h], acc)
            tile = lax.fori_loop(0, N_BUCKETS, body, jnp.zeros((Q_BLOCK, Q_BLOCK), F32))
            o_ref[t, h] = tile - rb_ref[N_BUCKETS - 1, h]


def bias_tables(rel_bias, offsets):
    n_heads = rel_bias.shape[1]
    kern = functools.partial(_bias_tables_body, offsets=tuple(offsets), n_heads=n_heads)
    return pl.pallas_call(
        kern,
        in_specs=[pl.BlockSpec(memory_space=pltpu.SMEM)],
        out_specs=pl.BlockSpec(memory_space=pltpu.VMEM),
        out_shape=jax.ShapeDtypeStruct((len(offsets), n_heads, Q_BLOCK, Q_BLOCK), F32),
        name="bias_tables",
    )(rel_bias)


def _sortable_key(scores):
    bits = lax.bitcast_convert_type(scores + 0.0, I32)
    return jnp.where(bits < 0, bits ^ 0x7FFFFFFF, bits)


def _count(mask):
    return jnp.sum(mask.astype(F32), axis=-1, keepdims=True)


def _topk_select(key, topk, n_index_bits):
    rows, n = key.shape
    kf = float(topk)
    t0 = jnp.where(_count(key >= 0) >= kf, 0, INT_MIN).astype(I32)

    def value_bit(i, t):
        cand = t + lax.shift_left(jnp.int32(1), 30 - i)
        return jnp.where(_count(key >= cand) >= kf, cand, t)

    thr = lax.fori_loop(0, 31, value_bit, t0)
    above = key > thr
    ties = key == thr
    need = kf - _count(above)
    idx = lax.broadcasted_iota(I32, (rows, n), 1)

    def lowest_ties():
        def index_bit(i, m):
            cand = m + lax.shift_left(jnp.int32(1), n_index_bits - 1 - i)
            return jnp.where(_count(ties & (idx < cand)) <= need, cand, m)
        return lax.fori_loop(0, n_index_bits, index_bit, jnp.zeros((rows, 1), I32))

    surplus = jnp.max(_count(ties) - need) > 0.0
    m = lax.cond(surplus, lowest_ties, lambda: jnp.full((rows, 1), 2 ** n_index_bits, I32))
    return above | (ties & (idx < m))


SUM_CHAINS = 4
HEAD_GROUP = 4


def _sum_rows(x):
    r = x.shape[0]
    if r % (SUM_CHAINS * SUBLANES) == 0 and r > SUM_CHAINS * SUBLANES:
        x = jnp.sum(x.reshape(SUM_CHAINS, r // SUM_CHAINS, x.shape[1]), axis=1)
    return jnp.sum(x, axis=0, keepdims=True)


def _max_rows(x):
    r = x.shape[0]
    if r % (SUM_CHAINS * SUBLANES) == 0 and r > SUM_CHAINS * SUBLANES:
        x = jnp.max(x.reshape(SUM_CHAINS, r // SUM_CHAINS, x.shape[1]), axis=1)
    return jnp.max(x, axis=0, keepdims=True)


def _topk_select_cols(key, topk, n_index_bits):
    n, cols = key.shape
    kf = float(topk)
    cnt = lambda m: _sum_rows(m.astype(F32))
    t0 = jnp.where(cnt(key >= 0) >= kf, 0, INT_MIN).astype(I32)

    def value_bit(i, t):
        cand = t + lax.shift_left(jnp.int32(1), 30 - i)
        return jnp.where(cnt(key >= cand) >= kf, cand, t)

    thr = lax.fori_loop(0, 31, value_bit, t0)
    above = key > thr
    ties = key == thr
    need = kf - cnt(above)
    idx = lax.broadcasted_iota(I32, (n, cols), 0)

    def lowest_ties():
        def index_bit(i, m):
            cand = m + lax.shift_left(jnp.int32(1), n_index_bits - 1 - i)
            return jnp.where(cnt(ties & (idx < cand)) <= need, cand, m)
        return lax.fori_loop(0, n_index_bits, index_bit, jnp.zeros((1, cols), I32))

    surplus = jnp.max(cnt(ties) - need) > 0.0
    m = lax.cond(surplus, lowest_ties, lambda: jnp.full((1, cols), 2 ** n_index_bits, I32))
    return above | (ties & (idx < m))


def _dsa_prompt_block(nb, ik_ref, iqt_ref, wt_ref, k_ref, qt_ref, vt_ref, bias_ref, o_ref, mask_ref,
                      *, n_heads, n_idx_heads, topk):
    w = nb * Q_BLOCK
    ik = ik_ref[0, :w, :]
    idx_dim = ik.shape[1]

    def head_rows(ref, h, dh, cols=slice(None)):
        return ref[0, pl.ds(pl.multiple_of(h * dh, dh), dh), cols]

    def idx_heads(gi, acc):
        hs = [gi * HEAD_GROUP + j for j in range(HEAD_GROUP)]
        dots = [jnp.dot(ik, head_rows(iqt_ref, h, idx_dim), preferred_element_type=F32) for h in hs]
        terms = [jnp.maximum(d, 0.0) * wt_ref[0, h] for d, h in zip(dots, hs)]
        while len(terms) > 1:
            terms = [a + b for a, b in zip(terms[0::2], terms[1::2])]
        return acc + terms[0]

    scores = lax.fori_loop(0, n_idx_heads // HEAD_GROUP, idx_heads, jnp.zeros((w, Q_BLOCK), F32))
    kpos = lax.broadcasted_iota(I32, (w, Q_BLOCK), 0)
    qpos = (nb - 1) * Q_BLOCK + lax.broadcasted_iota(I32, (w, Q_BLOCK), 1)
    valid = kpos <= qpos
    if w <= topk:
        sel = valid
    else:
        key = jnp.where(valid, _sortable_key(scores), INT_MIN)
        sel = valid & _topk_select_cols(key, topk, int(math.ceil(math.log2(w))) + 1)
    mask_ref[:w, :] = jnp.where(sel, 0.0, -jnp.inf)

    def with_near_bias(logits, h):
        near = [logits[w - Q_BLOCK:] + bias_ref[0, h]]
        if nb >= 2:
            near = [logits[w - 2 * Q_BLOCK:w - Q_BLOCK] + bias_ref[1, h]] + near
        if nb >= 3:
            near = [logits[:w - 2 * Q_BLOCK]] + near
        return jnp.concatenate(near, axis=0) if len(near) > 1 else near[0]

    pair_row_half = lax.broadcasted_iota(I32, (PAIR, Q_BLOCK), 0) // HEAD_DIM

    def head_logits(gi, j):
        p = gi * (HEAD_GROUP // 2) + j // 2
        qt_pair = head_rows(qt_ref, p, PAIR)
        qt_head = jnp.where(pair_row_half == j % 2, qt_pair, jnp.zeros_like(qt_pair))
        return jnp.dot(k_ref[0, p, :w, :], qt_head, preferred_element_type=F32)

    def attn_heads(gi, carry):
        hs = [gi * HEAD_GROUP + j for j in range(HEAD_GROUP)]
        mask = mask_ref[:w, :]
        logits = [head_logits(gi, j) + mask for j in range(HEAD_GROUP)]
        logits = [with_near_bias(l_, h) for l_, h in zip(logits, hs)]
        mx = [_max_rows(l_) for l_ in logits]
        e = [jnp.exp(l_ - m_) for l_, m_ in zip(logits, mx)]
        den = [_sum_rows(e_) for e_ in e]
        o = [jnp.dot(head_rows(vt_ref, h, HEAD_DIM, slice(0, w)), e_.astype(BF16),
                     preferred_element_type=F32) for h, e_ in zip(hs, e)]
        for h, o_, d_ in zip(hs, o, den):
            o_ref[0, pl.ds(pl.multiple_of(h * HEAD_DIM, HEAD_DIM), HEAD_DIM), :] = (o_ / d_).astype(o_ref.dtype)
        return carry

    lax.fori_loop(0, n_heads // HEAD_GROUP, attn_heads, 0)


def _dsa_prompt_body(ik_ref, iqt_ref, wt_ref, k_ref, qt_ref, vt_ref, bias_ref, o_ref, mask_ref,
                     *, n_heads, n_idx_heads, seq, topk):
    i = pl.program_id(0)
    for nb in range(1, seq // Q_BLOCK + 1):
        @pl.when(i == nb - 1)
        def _(nb=nb):
            _dsa_prompt_block(nb, ik_ref, iqt_ref, wt_ref, k_ref, qt_ref, vt_ref, bias_ref, o_ref, mask_ref,
                              n_heads=n_heads, n_idx_heads=n_idx_heads, topk=topk)


def dsa_prompt(ik, iqt, wts, k_pairs, qt, vt, bias_tiles, *, topk):
    b, width, s = qt.shape
    h = width // HEAD_DIM
    ih = wts.shape[1]
    kern = functools.partial(_dsa_prompt_body, n_heads=h, n_idx_heads=ih, seq=s, topk=topk)
    grid_spec = pltpu.PrefetchScalarGridSpec(
        num_scalar_prefetch=0,
        grid=(s // Q_BLOCK, b),
        in_specs=[pl.BlockSpec((1, s, ik.shape[2]), lambda i, bi: (bi, 0, 0)),
                  pl.BlockSpec((1, iqt.shape[1], Q_BLOCK), lambda i, bi: (bi, 0, i)),
                  pl.BlockSpec((1, ih, 1, Q_BLOCK), lambda i, bi: (bi, 0, 0, i)),
                  pl.BlockSpec((1, h // 2, s, PAIR), lambda i, bi: (bi, 0, 0, 0)),
                  pl.BlockSpec((1, width, Q_BLOCK), lambda i, bi: (bi, 0, i)),
                  pl.BlockSpec((1, width, s), lambda i, bi: (bi, 0, 0)),
                  pl.BlockSpec(bias_tiles.shape, lambda i, bi: (0, 0, 0, 0))],
        out_specs=pl.BlockSpec((1, width, Q_BLOCK), lambda i, bi: (bi, 0, i)),
        scratch_shapes=[pltpu.VMEM((s, Q_BLOCK), F32)],
    )
    return pl.pallas_call(
        kern,
        grid_spec=grid_spec,
        out_shape=jax.ShapeDtypeStruct((b, width, s), BF16),
        compiler_params=_cparams("arbitrary", "arbitrary"),
        name="dsa_prompt",
    )(ik, iqt, wts, k_pairs, qt, vt, bias_tiles)


def _dsa_select_body(pt_ref, iq_ref, wt_ref, iknew_ref, cidx_hbm, o_ref, ikbuf, sem,
                     *, layer, n_pages, group, n_idx_heads, t_new, topk):
    s = pl.program_id(0)
    past = n_pages * PAGE_SIZE
    n_keys = past + PAGE_SIZE

    def ik_copy(i):
        g = i // n_pages
        p = i % n_pages
        page = pt_ref[(s * group + g) * n_pages + p]
        return pltpu.make_async_copy(cidx_hbm.at[layer, page], ikbuf.at[g, p], sem)

    def ik_start(i, carry):
        ik_copy(i).start()
        return carry

    def ik_wait(i, carry):
        ik_copy(i).wait()
        return carry

    lax.fori_loop(0, group * n_pages, ik_start, 0)
    for g in range(group):
        ikbuf[g, n_pages] = iknew_ref[g]
    lax.fori_loop(0, group * n_pages, ik_wait, 0)

    scores = []
    for g in range(group):
        ikt_all = jnp.concatenate([ikbuf[g, p] for p in range(n_pages + 1)], axis=1)
        dots = _mm(iq_ref[g], ikt_all)
        weighted = jnp.maximum(dots, 0.0) * wt_ref[g]
        scores.append(jnp.sum(weighted.reshape(n_idx_heads, t_new, n_keys), axis=0))
    scores = jnp.concatenate(scores, axis=0)
    shape = (group * t_new, n_keys)
    qpos = past + lax.broadcasted_iota(I32, shape, 0) % t_new
    kpos = lax.broadcasted_iota(I32, shape, 1)
    valid = kpos <= qpos
    key = jnp.where(valid, _sortable_key(scores), INT_MIN)
    sel = valid & _topk_select(key, topk, int(math.log2(n_keys)) + 1)
    o_ref[...] = jnp.where(sel, 0.0, -jnp.inf).reshape(group, t_new, n_keys)


def dsa_sample_select(page_table, iq_rows, wt_rows, ik_new_t, cache_idx_kt, *, layer, n_idx_heads, t_new, topk,
                      group):
    db, n_pages = page_table.shape
    idx_dim = cache_idx_kt.shape[2]
    n_keys = (n_pages + 1) * PAGE_SIZE
    kern = functools.partial(_dsa_select_body, layer=layer, n_pages=n_pages, group=group,
                             n_idx_heads=n_idx_heads, t_new=t_new, topk=topk)
    per_g = lambda shape: pl.BlockSpec((group,) + shape, lambda si, pt: (si,) + (0,) * len(shape))
    grid_spec = pltpu.PrefetchScalarGridSpec(
        num_scalar_prefetch=1,
        grid=(db // group,),
        in_specs=[per_g((n_idx_heads * t_new, idx_dim)), per_g((n_idx_heads * t_new, 1)),
                  per_g((idx_dim, PAGE_SIZE)), pl.BlockSpec(memory_space=pl.ANY)],
        out_specs=per_g((t_new, n_keys)),
        scratch_shapes=[pltpu.VMEM((group, n_pages + 1, idx_dim, PAGE_SIZE), F32), pltpu.SemaphoreType.DMA(())],
    )
    return pl.pallas_call(
        kern,
        grid_spec=grid_spec,
        out_shape=jax.ShapeDtypeStruct((db, t_new, n_keys), F32),
        compiler_params=_cparams("arbitrary"),
        name="dsa_sample_select",
    )(page_table.reshape(-1), iq_rows, wt_rows, ik_new_t, cache_idx_kt)


def _dsa_sample_body(pt_ref, mask_ref, qbd_ref, knew_ref, vnew_ref,
                     blast_ref, bnew_ref, ck_hbm, cv_hbm, o_ref,
                     kbuf, vbuf, sem_k, sem_v,
                     *, layer, n_pages, chunk, n_heads, t_new):
    b = pl.program_id(0)
    n_seq = pl.num_programs(0)
    past = n_pages * PAGE_SIZE
    n_chunks = n_pages // chunk
    rows = n_heads * t_new
    ck = chunk * PAGE_SIZE

    def kv_copies(seq, c, j):
        slot = c % 2
        page = pt_ref[seq * n_pages + c * chunk + j]
        return (pltpu.make_async_copy(ck_hbm.at[layer, page], kbuf.at[slot, j], sem_k.at[slot]),
                pltpu.make_async_copy(cv_hbm.at[layer, page], vbuf.at[slot, j], sem_v.at[slot]))

    def start_chunk(seq, c):
        for j in range(chunk):
            kc, vc = kv_copies(seq, c, j)
            kc.start()
            vc.start()

    def wait_chunk(c):
        for j in range(chunk):
            kc, vc = kv_copies(b, c, j)
            kc.wait()
            vc.wait()

    @pl.when(b == 0)
    def _():
        start_chunk(0, 0)

    sel_rows = jnp.tile(mask_ref[0], (n_heads, 1))

    q_rep = jnp.tile(qbd_ref[0], (n_heads, 1))
    row_head = lax.broadcasted_iota(I32, q_rep.shape, 0) // t_new
    col_head = lax.broadcasted_iota(I32, q_rep.shape, 1) // HEAD_DIM
    qbd = jnp.where(row_head == col_head, q_rep, jnp.zeros_like(q_rep))
    neg = -1e30

    def update(state, logits, maskc, vt_bf16):
        m, l, acc = state
        s = logits + maskc
        m_new = jnp.maximum(m, jnp.max(s, axis=-1, keepdims=True))
        alpha = jnp.exp(m - m_new)
        p = jnp.exp(s - m_new)
        l = alpha * l + jnp.sum(p, axis=-1, keepdims=True)
        acc = alpha * acc + _mm_nt(p, vt_bf16)
        return m_new, l, acc

    def pages_t(buf, slot):
        return jnp.concatenate([buf[slot, j] for j in range(chunk)], axis=1).astype(BF16)

    state = (jnp.full((rows, 1), neg, F32), jnp.zeros((rows, 1), F32),
             jnp.zeros((rows, qbd.shape[1]), F32))
    for c in range(n_chunks):
        if c + 1 < n_chunks:
            start_chunk(b, c + 1)
        else:
            @pl.when(b + 1 < n_seq)
            def _():
                start_chunk(b + 1, 0)
        wait_chunk(c)
        slot = c % 2
        logits = _mm(qbd, pages_t(kbuf, slot))
        if c == n_chunks - 1:
            logits = jnp.concatenate([logits[:, :ck - PAGE_SIZE],
                                      logits[:, ck - PAGE_SIZE:] + blast_ref[...]], axis=1)
        state = update(state, logits, sel_rows[:, c * ck:(c + 1) * ck], pages_t(vbuf, slot))
    logits = _mm(qbd, knew_ref[0]) + bnew_ref[...]
    m, l, acc = update(state, logits, sel_rows[:, past:], vnew_ref[0])
    out = jnp.where(row_head == col_head, acc / l, 0.0)
    o_ref[0] = jnp.sum(out.reshape(n_heads, t_new, out.shape[1]), axis=0)


def dsa_sample(page_table, mask, q_bd, k_new_t, v_new_t, bias_last, bias_new, cache_kt, cache_vt,
               *, layer, n_heads, t_new, chunk):
    db, n_pages = page_table.shape
    assert (n_pages // chunk) % 2 == 0, "chunks alternate between two buffers across sequences"
    rows = n_heads * t_new
    width = q_bd.shape[2]
    kern = functools.partial(_dsa_sample_body, layer=layer, n_pages=n_pages, chunk=chunk, n_heads=n_heads,
                             t_new=t_new)
    per_b = lambda shape: pl.BlockSpec((1,) + shape, lambda bi, pt: (bi,) + (0,) * len(shape))
    const = lambda shape: pl.BlockSpec(shape, lambda bi, pt: (0,) * len(shape))
    any_spec = pl.BlockSpec(memory_space=pl.ANY)
    grid_spec = pltpu.PrefetchScalarGridSpec(
        num_scalar_prefetch=1,
        grid=(db,),
        in_specs=[per_b((t_new, mask.shape[2])),
                  per_b((t_new, width)), per_b((width, PAGE_SIZE)), per_b((width, PAGE_SIZE)),
                  const((rows, PAGE_SIZE)), const((rows, PAGE_SIZE)),
                  any_spec, any_spec],
        out_specs=per_b((t_new, width)),
        scratch_shapes=[pltpu.VMEM((2, chunk, width, PAGE_SIZE), F32),
                        pltpu.VMEM((2, chunk, width, PAGE_SIZE), F32),
                        pltpu.SemaphoreType.DMA((2,)),
                        pltpu.SemaphoreType.DMA((2,))],
    )
    return pl.pallas_call(
        kern,
        grid_spec=grid_spec,
        out_shape=jax.ShapeDtypeStruct((db, t_new, width), F32),
        compiler_params=_cparams("arbitrary"),
        name="dsa_sample",
    )(page_table.reshape(-1), mask, q_bd, k_new_t, v_new_t, bias_last, bias_new, cache_kt, cache_vt)


def _token_minor_cache(cache):
    l, pool, page = cache.shape[:3]
    nd = cache.ndim
    return cache.transpose((0, 1) + tuple(range(3, nd)) + (2,)).reshape(l, pool, -1, page)


def _dsa_sample_inputs(q, k_new, v_new, iq, ik_new, iw, bias_tiles):
    db, t, h, dh = q.shape
    ih = iq.shape[2]
    iq_rows = iq.transpose(0, 2, 1, 3).reshape(db, ih * t, -1).astype(BF16)
    wt_rows = (iw * (ih ** -0.5 * iq.shape[3] ** -0.5)).transpose(0, 2, 1).reshape(db, ih * t, 1)
    q_bd = q.reshape(db, t, h * dh)
    page_t = lambda x: jnp.pad(x.reshape(db, t, -1).transpose(0, 2, 1), ((0, 0), (0, 0), (0, PAGE_SIZE - t)))
    bias_new = bias_tiles[0, :, :t, :].reshape(h * t, Q_BLOCK)
    bias_last = bias_tiles[1, :, :t, :].reshape(h * t, Q_BLOCK)
    return ((iq_rows, wt_rows, page_t(ik_new)),
            (q_bd.astype(BF16), page_t(k_new).astype(BF16), page_t(v_new).astype(BF16), bias_last, bias_new))


def _matmul_residual_body(x_ref, a_ref, b_ref, w_ref, o_ref):
    ka = a_ref.shape[1]
    o_ref[...] = x_ref[...] + (_mm(a_ref[...], w_ref[:ka, :]) + _mm(b_ref[...], w_ref[ka:, :]))


def matmul_residual(x, a, b, w, *, tm, tn):
    n, d = x.shape
    ka, kb = a.shape[1], b.shape[1]
    return pl.pallas_call(
        _matmul_residual_body,
        grid=(n // tm, d // tn),
        in_specs=[pl.BlockSpec((tm, tn), lambda i, j: (i, j)),
                  pl.BlockSpec((tm, ka), lambda i, j: (i, 0)),
                  pl.BlockSpec((tm, kb), lambda i, j: (i, 0)),
                  pl.BlockSpec((ka + kb, tn), lambda i, j: (0, j))],
        out_specs=pl.BlockSpec((tm, tn), lambda i, j: (i, j)),
        out_shape=jax.ShapeDtypeStruct((n, d), F32),
        compiler_params=_cparams("parallel", "parallel"),
        name="matmul_residual",
    )(x, a, b, w)


def _cross_attn_body(x_ref, g_ref, wq_ref, mk_ref, mv_ref, wo_ref, o_ref, *, groups, t_rows, n_heads, head_dim):
    x = x_ref[...]
    h = _rmsnorm(x, g_ref[...]).astype(BF16)
    q = jnp.dot(h, wq_ref[...], preferred_element_type=F32).astype(BF16)
    scale = head_dim ** -0.5
    outs = []
    for gi in range(groups):
        qg = q[gi * t_rows:(gi + 1) * t_rows]
        heads = []
        for hh in range(n_heads):
            sl = slice(hh * head_dim, (hh + 1) * head_dim)
            logits = _mm_nt(qg[:, sl], mk_ref[gi, :, sl]) * scale
            mx = jnp.max(logits, axis=-1, keepdims=True)
            e = jnp.exp(logits - mx)
            p = e / jnp.sum(e, axis=-1, keepdims=True)
            heads.append(_mm(p, mv_ref[gi, :, sl]))
        outs.append(jnp.concatenate(heads, axis=1))
    o = jnp.concatenate(outs, axis=0) if groups > 1 else outs[0]
    o_ref[...] = x + jnp.dot(o.astype(BF16), wo_ref[...], preferred_element_type=F32)


def cross_attn(x, g, wq, mk, mv, wo, *, groups, t_rows, seq_tiles, n_heads):
    n, d = x.shape
    xw = wq.shape[1]
    rows = groups * t_rows
    m = mk.shape[1]
    kern = functools.partial(_cross_attn_body, groups=groups, t_rows=t_rows, n_heads=n_heads,
                             head_dim=xw // n_heads)
    return pl.pallas_call(
        kern,
        grid=(n // rows,),
        in_specs=[pl.BlockSpec((rows, d), lambda i: (i, 0)),
                  pl.BlockSpec((1, d), lambda i: (0, 0)),
                  pl.BlockSpec((d, xw), lambda i: (0, 0)),
                  pl.BlockSpec((groups, m, xw), lambda i: (i // seq_tiles, 0, 0)),
                  pl.BlockSpec((groups, m, xw), lambda i: (i // seq_tiles, 0, 0)),
                  pl.BlockSpec((xw, d), lambda i: (0, 0))],
        out_specs=pl.BlockSpec((rows, d), lambda i: (i, 0)),
        out_shape=jax.ShapeDtypeStruct((n, d), F32),
        compiler_params=_cparams("parallel"),
        name="cross_attn",
    )(x, g.reshape(1, d), wq, mk, mv, wo)


def _router_body(x_ref, g_ref, wr_ref, br_ref, h_ref, r_ref, *, n_groups, per_group):
    h = _rmsnorm(x_ref[...], g_ref[...])
    _store_slab_rows(h_ref, h, h.shape[1] // LANES)
    logits = _mm3(h, wr_ref[...]) + br_ref[...]
    lane = lax.broadcasted_iota(I32, logits.shape, 1).astype(F32)
    big = 1e9
    first_lane = lambda hit: jnp.min(jnp.where(hit, lane, big), axis=-1, keepdims=True)
    gl = jnp.where(lane < n_groups, logits, -jnp.inf)
    gmax = jnp.max(gl, axis=-1, keepdims=True)
    grp = first_lane(gl == gmax)
    p_grp = 1.0 / jnp.sum(jnp.exp(gl - gmax), axis=-1, keepdims=True)
    e_id = lane - n_groups
    in_grp = (e_id >= grp * per_group) & (e_id < (grp + 1.0) * per_group)
    el = jnp.where(in_grp, logits, -jnp.inf)
    v1 = jnp.max(el, axis=-1, keepdims=True)
    i1 = first_lane(el == v1) - n_groups
    el2 = jnp.where(e_id == i1, -jnp.inf, el)
    v2 = jnp.max(el2, axis=-1, keepdims=True)
    i2 = first_lane(el2 == v2) - n_groups
    e2 = jnp.exp(v2 - v1)
    g1 = p_grp / (1.0 + e2)
    g2 = p_grp * e2 / (1.0 + e2)
    r_ref[...] = jnp.where(lane == 0, g1, jnp.where(lane == 1, g2, jnp.where(
        lane == 2, i1, jnp.where(lane == 3, i2, 0.0))))


def router(x, g, w_r, b_r, *, tm, n_groups, per_group):
    n, d = x.shape
    kern = functools.partial(_router_body, n_groups=n_groups, per_group=per_group)
    return pl.pallas_call(
        kern,
        grid=(n // tm,),
        in_specs=[pl.BlockSpec((tm, d), lambda i: (i, 0)),
                  pl.BlockSpec((1, d), lambda i: (0, 0)),
                  pl.BlockSpec((d, LANES), lambda i: (0, 0)),
                  pl.BlockSpec((1, LANES), lambda i: (0, 0))],
        out_specs=[pl.BlockSpec((tm * (d // LANES), LANES), lambda i: (i, 0)),
                   pl.BlockSpec((tm, LANES), lambda i: (i, 0))],
        out_shape=[jax.ShapeDtypeStruct((n * (d // LANES), LANES), F32), jax.ShapeDtypeStruct((n, LANES), F32)],
        compiler_params=_cparams("parallel"),
        name="moe_router",
    )(x, g.reshape(1, d), w_r, b_r)


def _slab_rows(buf, offset, n_rows, n_slab, stride):
    return jnp.concatenate([buf[pl.ds(offset + s, n_rows, stride=stride), :] for s in range(n_slab)], axis=1)


def _store_slab_rows(buf, x, n_slab):
    n_rows = x.shape[0]
    for s in range(n_slab):
        buf[pl.ds(s, n_rows, stride=n_slab), :] = x[:, s * LANES:(s + 1) * LANES]


def _moe_ffn_body(te_ref, nr_ref, tok_ref, dst_ref, h_hbm, w1_ref, w3_ref, w2_ref, y_hbm,
                  xbuf, ybuf, gsem, ssem, *, tm):
    t = pl.program_id(0)
    n_tiles = pl.num_programs(0)
    n_slab = h_hbm.shape[1]
    slot = t % 2

    def slab(buf, sl, r):
        return buf.at[sl, pl.ds(pl.multiple_of(r * n_slab, n_slab), n_slab)]

    def gather_copy(tile, sl, r):
        return pltpu.make_async_copy(h_hbm.at[tok_ref[tile * tm + r]], slab(xbuf, sl, r), gsem.at[sl])

    def scatter_copy(tile, sl, r):
        return pltpu.make_async_copy(slab(ybuf, sl, r), y_hbm.at[dst_ref[tile * tm + r]], ssem.at[sl])

    def start_rows(n, copy):
        def pair(i, c):
            copy(2 * i).start(priority=0)
            copy(2 * i + 1).start(priority=1)
            return c
        lax.fori_loop(0, n // 2, pair, 0)

        @pl.when(n % 2 == 1)
        def _():
            copy(n - 1).start(priority=0)

    def wait_rows(n, copy):
        def one(r, c):
            copy(r).wait()
            return c
        lax.fori_loop(0, n, one, 0)

    @pl.when(t == 0)
    def _():
        xbuf[...] = jnp.zeros(xbuf.shape, F32)
        start_rows(nr_ref[0], lambda r: gather_copy(0, 0, r))

    nxt = jnp.minimum(t + 1, n_tiles - 1)

    @pl.when(t + 1 < n_tiles)
    def _():
        start_rows(nr_ref[nxt], lambda r: gather_copy(nxt, 1 - slot, r))

    @pl.when(t >= 2)
    def _():
        prev2 = jnp.maximum(t - 2, 0)
        wait_rows(nr_ref[prev2], lambda r: scatter_copy(prev2, slot, r))

    n = nr_ref[t]

    @pl.when(n > 0)
    def _():
        wait_rows(n, lambda r: gather_copy(t, slot, r))
        x = _slab_rows(xbuf.at[slot], 0, tm, n_slab, n_slab).astype(BF16)
        a = jnp.dot(x, w1_ref[0].astype(BF16), preferred_element_type=F32)
        bgate = jnp.dot(x, w3_ref[0].astype(BF16), preferred_element_type=F32)
        u = (a / (1.0 + jnp.exp(-a))) * bgate
        y = jnp.dot(u.astype(BF16), w2_ref[0].astype(BF16), preferred_element_type=F32)
        _store_slab_rows(ybuf.at[slot], y, n_slab)
        start_rows(n, lambda r: scatter_copy(t, slot, r))

    @pl.when(t == n_tiles - 1)
    def _():
        @pl.when(t >= 1)
        def _():
            prev1 = jnp.maximum(t - 1, 0)
            wait_rows(nr_ref[prev1], lambda r: scatter_copy(prev1, 1 - slot, r))
        wait_rows(n, lambda r: scatter_copy(t, slot, r))


def moe_ffn(tile_expert, tile_rows, row_tok, row_dst, h_slabs, n_out_rows, w1, w3, w2, *, tm):
    n_tiles = tile_expert.shape[0]
    d, de = w1.shape[1], w1.shape[2]
    n_slab = d // LANES
    h3 = h_slabs.reshape(-1, n_slab, LANES)
    w_in_spec = pl.BlockSpec((1, d, de), lambda t, te, nr, tok, dst: (te[t], 0, 0))
    grid_spec = pltpu.PrefetchScalarGridSpec(
        num_scalar_prefetch=4,
        grid=(n_tiles,),
        in_specs=[pl.BlockSpec(memory_space=pl.ANY), w_in_spec, w_in_spec,
                  pl.BlockSpec((1, de, d), lambda t, te, nr, tok, dst: (te[t], 0, 0))],
        out_specs=pl.BlockSpec(memory_space=pl.ANY),
        scratch_shapes=[pltpu.VMEM((2, tm * n_slab, LANES), F32), pltpu.VMEM((2, tm * n_slab, LANES), F32),
                        pltpu.SemaphoreType.DMA((2,)), pltpu.SemaphoreType.DMA((2,))],
    )
    y = pl.pallas_call(
        functools.partial(_moe_ffn_body, tm=tm),
        grid_spec=grid_spec,
        out_shape=jax.ShapeDtypeStruct((n_out_rows, n_slab, LANES), F32),
        compiler_params=_cparams("arbitrary"),
        name="moe_ffn",
    )(tile_expert, tile_rows, row_tok, row_dst, h3, w1, w3, w2)
    return y.reshape(n_out_rows * n_slab, LANES)


def _moe_dispatch(eids, n_experts, tm):
    n, k = eids.shape
    m = n * k
    flat_e = eids.reshape(-1)
    order = jnp.argsort(flat_e, stable=True).astype(I32)
    counts = jnp.sum((flat_e[:, None] == jnp.arange(n_experts)[None, :]).astype(I32), axis=0)
    padded = (counts + tm - 1) // tm * tm
    pad_end = jnp.cumsum(padded)
    pad_start = pad_end - padded
    start = jnp.cumsum(counts) - counts
    n_tiles = -(-m // tm) + n_experts
    tile_start = jnp.arange(n_tiles) * tm
    last = jnp.maximum(pad_end[-1] - 1, 0)
    tile_expert = jnp.minimum(jnp.searchsorted(pad_end, jnp.minimum(tile_start, last), side='right'),
                              n_experts - 1).astype(I32)
    first = tile_start - pad_start[tile_expert]
    tile_rows = jnp.where(tile_start < pad_end[-1], jnp.clip(counts[tile_expert] - first, 0, tm), 0).astype(I32)
    src = jnp.clip(start[tile_expert][:, None] + first[:, None] + jnp.arange(tm)[None, :], 0, m - 1)
    row_dst = order[src].reshape(-1)
    return tile_expert, tile_rows, (row_dst // k).astype(I32), row_dst.astype(I32)


def _combine_body(x_ref, r_ref, g_ref, y_ref, o_ref, *, tm, top_k):
    n_slab = x_ref.shape[1] // LANES
    route = r_ref[...]
    x = x_ref[...]
    for kk in range(top_k):
        x = x + _slab_rows(y_ref, kk * n_slab, tm, n_slab, top_k * n_slab) * route[:, kk:kk + 1]
    o_ref[...] = _rmsnorm(x, g_ref[...])


def moe_combine(x, route, g, y_slabs, *, tm, top_k, tile_offset):
    n, d = x.shape
    rows = tm * top_k * (d // LANES)
    return pl.pallas_call(
        functools.partial(_combine_body, tm=tm, top_k=top_k),
        grid=(n // tm,),
        in_specs=[pl.BlockSpec((tm, d), lambda i: (i, 0)),
                  pl.BlockSpec((tm, LANES), lambda i: (i, 0)),
                  pl.BlockSpec((1, d), lambda i: (0, 0)),
                  pl.BlockSpec((rows, LANES), lambda i: (i + tile_offset, 0))],
        out_specs=pl.BlockSpec((tm, d), lambda i: (i, 0)),
        out_shape=jax.ShapeDtypeStruct((n, d), F32),
        compiler_params=_cparams("parallel"),
        name="moe_combine",
    )(x, route, g.reshape(1, d), y_slabs)


def kernel(x_prompt, x_sample, mem_prompt, cache_k, cache_v, cache_idx_k, page_table, state_wkv, state_shift, cache_mem_k, cache_mem_v, g_mix, w_in, mu_shift, rw_w0, rw_w2, rw_a0, rw_a2, rw_g2, rw_kk, rw_ka, rw_rk, rw_ln_g, rw_ln_b, w_out, g_cross, g_mem, w_cq, w_ck, w_cv, w_co, g_ffn, w_rg, b_rg, w_re, b_re, w_e1, w_e3, w_e2, rel_bias, g_final):
    B, S, D = x_prompt.shape
    DB, T, _ = x_sample.shape
    assert w_in.shape[0] == 1, "single-layer trunk only"
    l = 0
    n_pages = page_table.shape[1]
    past = n_pages * PAGE_SIZE
    topk_p = min(TOPK_MAX, S // 4)
    topk_s = min(TOPK_MAX, (past + T) // 4)
    rw_proj = mu_shift.shape[1]
    width = rw_w0.shape[1]
    at_w = D - width
    n_heads = at_w // HEAD_DIM
    idx_dim = cache_idx_k.shape[-1]
    ih = (w_in.shape[2] - rw_proj - 3 * at_w - idx_dim) // (idx_dim + 1)
    xw = w_cq.shape[2]
    x_heads = cache_mem_k.shape[3]
    n_mem = mem_prompt.shape[1]
    n_experts = w_e1.shape[1]
    top_k = 2
    tn = PROJ_COLS
    n_main = 3 * width
    n_lora = rw_proj - n_main
    n_att = 3 * at_w + ih * idx_dim
    n_tail = n_lora + idx_dim + ih
    assert n_main % tn == 0 and at_w % tn == 0 and (ih * idx_dim) % tn == 0 and n_tail <= tn

    w_t = w_in[l].T
    w_all = jnp.concatenate([w_t[:n_main], w_t[rw_proj:rw_proj + at_w] * HEAD_DIM ** -0.5,
                             w_t[rw_proj + at_w:rw_proj + n_att], w_t[n_main:rw_proj],
                             w_t[rw_proj + n_att:], jnp.zeros((tn - n_tail, D), w_t.dtype)], axis=0).astype(BF16)
    flat, per_head = False, True
    segments = [(n_main // tn, [(flat, F32)]),
                (at_w // tn, [(flat, BF16)]),
                (at_w // tn, [(per_head, F32), (flat, BF16)]),
                (at_w // tn, [(per_head, F32), (flat, BF16)]),
                (ih * idx_dim // tn, [(flat, BF16)]),
                (1, [(flat, F32)])]
    w_out_b = w_out[l].astype(BF16)
    w_cq_b, w_co_b = w_cq[l].astype(BF16), w_co[l].astype(BF16)
    w_ckv = jnp.concatenate([w_ck[l], w_cv[l]], axis=1).astype(BF16)
    n_route = w_rg.shape[2] + w_re.shape[2]
    w_r = jnp.pad(jnp.concatenate([w_rg[l], w_re[l]], axis=1), ((0, 0), (0, LANES - n_route)))
    b_r = jnp.pad(jnp.concatenate([b_rg[l], b_re[l]]), (0, LANES - n_route)).reshape(1, LANES)
    rw_args = (mu_shift[l], rw_w0[l], rw_w2[l], rw_a0[l], rw_a2[l], rw_g2[l], rw_kk[l], rw_ka[l],
               rw_rk[l].reshape(-1))
    tiles = bias_tables(rel_bias, ((0, False), (Q_BLOCK, False), (0, True), (Q_BLOCK, True)))

    def project(x2d, b_, t_, tm):
        f_main, q, k_heads, k, v_heads, v, iq, f_tail = norm_matmul_split(x2d, g_mix[l], w_all, segments,
                                                                          tm=tm, tn=tn)
        r3 = lambda z: z.reshape(b_, t_, -1)
        f_main, f_tail = r3(f_main), r3(f_tail)
        ik = f_tail[..., n_lora:n_lora + idx_dim]
        iw = f_tail[..., n_lora + idx_dim:n_tail]
        shift = jnp.concatenate([f_main[:, -1], f_tail[:, -1, :n_lora]], axis=-1)
        heads5 = lambda z: z.reshape(1, b_, t_, n_heads, HEAD_DIM)
        return f_main, r3(q), heads5(k_heads), r3(k), heads5(v_heads), r3(v), r3(iq), f_tail, ik, iw, shift

    def rw_rows(y):
        b_, p_, t_, _ = y.shape
        return y.transpose(0, 2, 1, 3).reshape(b_ * t_, p_ * PAIR).astype(BF16)

    xp = x_prompt.reshape(B * S, D)
    fm_p, q, k_p, kb_p, v_p, vb_p, iq, ft_p, ik_p, iw, shift_p = project(xp, B, S, PROJ_ROWS)
    feats_p = rwkv_prep(fm_p, ft_p, jnp.zeros((B, rw_proj), F32), *rw_args, tm=PREP_ROWS, width=width)
    rw_p, st_p = rwkv_chunk(feats_p, jnp.zeros((B, width // PAIR, PAIR, PAIR), F32), rw_ln_g[l], rw_ln_b[l],
                            L=RWKV_CHUNK)
    tr = lambda z: z.transpose(0, 2, 1)
    k_pairs = kb_p.reshape(B, S, n_heads // 2, PAIR).transpose(0, 2, 1, 3)
    at_p = dsa_prompt(ik_p.astype(BF16), tr(iq), tr(iw * (ih ** -0.5 * idx_dim ** -0.5))[:, :, None, :],
                      k_pairs, tr(q), tr(vb_p), tiles[2:4], topk=topk_p)
    x1_p = matmul_residual(xp, rw_rows(rw_p), tr(at_p).reshape(B * S, at_w), w_out_b, tm=ROW_TILE, tn=D)
    mkv = norm_matmul(mem_prompt.reshape(B * n_mem, D), g_mem[l], w_ckv, tm=MEM_ROWS, tn=xw)
    mk_p = mkv[:, :xw].reshape(B, n_mem, xw)
    mv_p = mkv[:, xw:].reshape(B, n_mem, xw)
    x2_p = cross_attn(x1_p, g_cross[l], w_cq_b, mk_p, mv_p, w_co_b,
                      groups=1, t_rows=ROW_TILE, seq_tiles=S // ROW_TILE, n_heads=x_heads)

    xs = x_sample.reshape(DB * T, D)
    fm_s, q2, k_s, kb_s, v_s, vb_s, iq2, ft_s, ik_s, iw2, shift_s = project(xs, DB, T, DB * T)
    feats_s = rwkv_prep(fm_s, ft_s, state_shift[l], *rw_args, tm=T, width=width)
    rw_s, st_s = rwkv_chunk(feats_s, _state_to_block_diag(state_wkv[l]), rw_ln_g[l], rw_ln_b[l], L=T)
    r4 = lambda z, h_: z.reshape(DB, T, h_, -1)
    sel_args, att_args = _dsa_sample_inputs(r4(q2, n_heads), r4(kb_s, n_heads), r4(vb_s, n_heads), r4(iq2, ih),
                                            ik_s, iw2, tiles[0:2])
    mask_s = dsa_sample_select(page_table, *sel_args, _token_minor_cache(cache_idx_k), layer=l,
                               n_idx_heads=ih, t_new=T, topk=topk_s, group=SELECT_GROUP)
    at_s = dsa_sample(page_table, mask_s, *att_args, _token_minor_cache(cache_k), _token_minor_cache(cache_v),
                      layer=l, n_heads=n_heads, t_new=T, chunk=SAMPLE_CHUNK_PAGES)
    x1_s = matmul_residual(xs, rw_rows(rw_s), at_s.reshape(DB * T, at_w), w_out_b, tm=DB * T, tn=D)
    x2_s = cross_attn(x1_s, g_cross[l], w_cq_b, cache_mem_k[l].reshape(DB, n_mem, xw),
                      cache_mem_v[l].reshape(DB, n_mem, xw), w_co_b,
                      groups=CROSS_GROUP, t_rows=T, seq_tiles=1, n_heads=x_heads)

    h_p, route_p = router(x2_p, g_ffn[l], w_r, b_r, tm=ROW_TILE, n_groups=w_rg.shape[2],
                          per_group=w_re.shape[2] // w_rg.shape[2])
    h_s, route_s = router(x2_s, g_ffn[l], w_r, b_r, tm=DB * T, n_groups=w_rg.shape[2],
                          per_group=w_re.shape[2] // w_rg.shape[2])
    h_all = jnp.concatenate([h_p, h_s], axis=0)
    eids = jnp.concatenate([route_p[:, top_k:2 * top_k], route_s[:, top_k:2 * top_k]], axis=0).astype(I32)
    tm_moe = DB * T
    n_p, n_all = B * S, B * S + DB * T
    assert n_p % tm_moe == 0
    tile_expert, tile_rows, row_tok, row_dst = _moe_dispatch(eids, n_experts, tm_moe)
    y_slabs = moe_ffn(tile_expert, tile_rows, row_tok, row_dst, h_all, n_all * top_k,
                      w_e1[l], w_e3[l], w_e2[l], tm=tm_moe)
    y_p = moe_combine(x2_p, route_p, g_final, y_slabs, tm=tm_moe, top_k=top_k, tile_offset=0)
    y_s = moe_combine(x2_s, route_s, g_final, y_slabs, tm=tm_moe, top_k=top_k, tile_offset=n_p // tm_moe)

    return (y_p.reshape(B, S, D), y_s.reshape(DB, T, D),
            k_p, v_p, ik_p[None], _state_from_block_diag(st_p)[None], shift_p[None],
            mk_p.reshape(1, B, n_mem, x_heads, xw // x_heads), mv_p.reshape(1, B, n_mem, x_heads, xw // x_heads),
            k_s, v_s, ik_s[None], _state_from_block_diag(st_s)[None], shift_s[None])
```
